```python
import math
import jax, jax.numpy as jnp
from jax import lax
import numpy as np

D_MODEL = 1024
BATCH = 16
SEQ = 2048
DEPTH = 2

SSD_WIDTH = D_MODEL
SSD_HEAD_DIM = 64
SSD_HEADS = SSD_WIDTH // SSD_HEAD_DIM
SSD_GROUPS = 2
SSD_STATE = 128
SSD_CONV = 4
SSD_CHUNK = 128
SSD_CONV_DIM = SSD_WIDTH + 2 * SSD_GROUPS * SSD_STATE
ATTN_HEAD_DIM = 64
ATTN_WIDTH = D_MODEL // 2
ATTN_Q_HEADS = ATTN_WIDTH // ATTN_HEAD_DIM
ATTN_KV_HEADS = 2
WINDOW = 128
CONF_WIDTH = D_MODEL // 2
CONF_KERNEL = 31
MIX_WIDTH = SSD_WIDTH + ATTN_WIDTH + CONF_WIDTH
IN_SPLITS = (MIX_WIDTH, SSD_CONV_DIM, SSD_HEADS, ATTN_Q_HEADS * ATTN_HEAD_DIM,
             ATTN_KV_HEADS * ATTN_HEAD_DIM, ATTN_KV_HEADS * ATTN_HEAD_DIM, 2 * CONF_WIDTH)
D_IN_PROJ = sum(IN_SPLITS)
EPS = 1e-5

kernel_name = "hybrid_ssd_swa_conformer_parallel_heads"


def _split(a, sizes):
    idx = np.cumsum(sizes)[:-1].tolist()
    return jnp.split(a, idx, axis=-1)


def rmsnorm(x, w):
    xf = x.astype(jnp.float32)
    y = xf * lax.rsqrt(jnp.mean(xf * xf, axis=-1, keepdims=True) + EPS)
    return (y * w.astype(jnp.float32)).astype(x.dtype)


def gated_group_rmsnorm(y, z, w, groups):
    g = (y * jax.nn.silu(z)).astype(jnp.float32)
    shp = g.shape
    g = g.reshape(shp[:-1] + (groups, shp[-1] // groups))
    g = g * lax.rsqrt(jnp.mean(g * g, axis=-1, keepdims=True) + EPS)
    return (g.reshape(shp) * w.astype(jnp.float32)).astype(y.dtype)


def layernorm(x, w, b):
    xf = x.astype(jnp.float32)
    mu = jnp.mean(xf, axis=-1, keepdims=True)
    xc = xf - mu
    y = xc * lax.rsqrt(jnp.mean(xc * xc, axis=-1, keepdims=True) + EPS)
    return (y * w.astype(jnp.float32) + b.astype(jnp.float32)).astype(x.dtype)


def causal_depthwise_conv(x, w, b):
    K, C = w.shape
    y = lax.conv_general_dilated(x, w[:, None, :].astype(x.dtype), window_strides=(1,),
                                 padding=[(K - 1, 0)], dimension_numbers=('NWC', 'WIO', 'NWC'),
                                 feature_group_count=C)
    return y + b.astype(x.dtype)


def ssd_chunked(x, dt, A, B, C):
    b, l, h, p = x.shape
    g, n = B.shape[2], B.shape[3]
    r = h // g
    nc = l // SSD_CHUNK
    Q = SSD_CHUNK
    x = x.reshape(b, nc, Q, g, r, p)
    dt = dt.reshape(b, nc, Q, g, r)
    B = B.reshape(b, nc, Q, g, n)
    C = C.reshape(b, nc, Q, g, n)
    a = dt * A.reshape(g, r)
    a_cs = jnp.cumsum(a, axis=2)
    xdt = x * dt[..., None]
    seg = a_cs[:, :, :, None] - a_cs[:, :, None, :]
    causal = jnp.tril(jnp.ones((Q, Q), dtype=bool))[None, None, :, :, None, None]
    L = jnp.exp(jnp.where(causal, seg, -jnp.inf))
    cb = jnp.einsum('bcign,bcjgn->bcijg', C, B)
    M = cb[..., None] * L
    y_diag = jnp.einsum('bcijgr,bcjgrp->bcigrp', M, xdt)
    decay_to_end = jnp.exp(a_cs[:, :, -1:] - a_cs)
    states = jnp.einsum('bcjgn,bcjgrp->bcgrpn', B, xdt * decay_to_end[..., None])
    chunk_decay = jnp.exp(a_cs[:, :, -1])

    def step(S, inp):
        st, dec = inp
        return dec[..., None, None] * S + st, S

    S0 = jnp.zeros((b, g, r, p, n), jnp.float32)
    _, prev = lax.scan(step, S0, (jnp.moveaxis(states, 1, 0), jnp.moveaxis(chunk_decay, 1, 0)))
    prev = jnp.moveaxis(prev, 0, 1)
    y_off = jnp.einsum('bcign,bcgrpn->bcigrp', C, prev) * jnp.exp(a_cs)[..., None]
    return (y_diag + y_off).reshape(b, l, h, p)


def swa_gqa_sinks(q, k, v, sinks):
    b, l, g, r, d = q.shape
    W = WINDOW
    nb = l // W
    qb = q.reshape(b, nb, W, g, r, d)
    kb = k.reshape(b, nb, W, g, d)
    vb = v.reshape(b, nb, W, g, d)
    pad = ((0, 0), (1, 0), (0, 0), (0, 0), (0, 0))
    kk = jnp.concatenate([jnp.pad(kb, pad)[:, :-1], kb], axis=2)
    vv = jnp.concatenate([jnp.pad(vb, pad)[:, :-1], vb], axis=2)
    s = jnp.einsum('bnqgrd,bnkgd->bngrqk', qb, kk,
                   preferred_element_type=jnp.float32) * (d ** -0.5)
    qi = jnp.arange(W)[:, None]
    kj = jnp.arange(2 * W)[None, :] - W
    rel = qi - kj
    band = (rel >= 0) & (rel < W)
    mask = band[None] & ((jnp.arange(nb)[:, None, None] > 0) | (kj[None] >= 0))
    s = jnp.where(mask[None, :, None, None], s, -jnp.inf)
    sk = sinks.astype(jnp.float32).reshape(g, r)[None, None, :, :, None, None]
    lse = jnp.logaddexp(jax.nn.logsumexp(s, axis=-1, keepdims=True), sk)
    pr = jnp.exp(s - lse)
    o = jnp.einsum('bngrqk,bnkgd->bnqgrd', pr.astype(v.dtype), vv)
    return o.reshape(b, l, g * r * d)


def hybrid_mixer(h, w_in, conv_w, conv_b, dt_bias, a_log, d_skip, ssd_norm_w, sinks,
                 dw_w, dw_b, ln_w, ln_b, w_out):
    b, l, _ = h.shape
    proj = h @ w_in
    z, xbc, dt, q, k, v, conf = _split(proj, IN_SPLITS)
    z_ssd, z_attn, z_conf = _split(z, (SSD_WIDTH, ATTN_WIDTH, CONF_WIDTH))
    xbc = jax.nn.silu(causal_depthwise_conv(xbc, conv_w, conv_b))
    xs, Bs, Cs = _split(xbc, (SSD_WIDTH, SSD_GROUPS * SSD_STATE, SSD_GROUPS * SSD_STATE))
    xs = xs.reshape(b, l, SSD_HEADS, SSD_HEAD_DIM).astype(jnp.float32)
    Bs = Bs.reshape(b, l, SSD_GROUPS, SSD_STATE).astype(jnp.float32)
    Cs = Cs.reshape(b, l, SSD_GROUPS, SSD_STATE).astype(jnp.float32)
    dtp = jax.nn.softplus(dt.astype(jnp.float32) + dt_bias.astype(jnp.float32))
    A = -jnp.exp(a_log.astype(jnp.float32))
    y = ssd_chunked(xs, dtp, A, Bs, Cs) + d_skip.astype(jnp.float32)[:, None] * xs
    y_ssd = gated_group_rmsnorm(y.reshape(b, l, SSD_WIDTH).astype(h.dtype), z_ssd,
                                ssd_norm_w, SSD_GROUPS)
    rep = ATTN_Q_HEADS // ATTN_KV_HEADS
    qh = q.reshape(b, l, ATTN_KV_HEADS, rep, ATTN_HEAD_DIM)
    kh = k.reshape(b, l, ATTN_KV_HEADS, ATTN_HEAD_DIM)
    vh = v.reshape(b, l, ATTN_KV_HEADS, ATTN_HEAD_DIM)
    y_attn = swa_gqa_sinks(qh, kh, vh, sinks) * jax.nn.silu(z_attn)
    ca, cg = _split(conf, (CONF_WIDTH, CONF_WIDTH))
    c = ca * jax.nn.sigmoid(cg)
    c = causal_depthwise_conv(c, dw_w, dw_b)
    c = jax.nn.silu(layernorm(c, ln_w, ln_b))
    y_conf = c * jax.nn.silu(z_conf)
    return jnp.concatenate([y_ssd, y_attn, y_conf], axis=-1) @ w_out


def _fwd_setup_inputs(seed: int = 0) -> dict:
    key = jax.random.key(seed)
    ks = jax.random.split(key, 20)
    f32 = jnp.float32
    nrm = lambda k, s, sc: jax.random.normal(k, s, f32) * sc
    x = jax.random.normal(ks[0], (BATCH, SEQ, D_MODEL), f32)
    norm_w = 1.0 + nrm(ks[1], (DEPTH, D_MODEL), 0.02)
    w_in = nrm(ks[2], (DEPTH, D_MODEL, D_IN_PROJ), D_MODEL ** -0.5)
    ssd_conv_w = nrm(ks[3], (DEPTH, SSD_CONV, SSD_CONV_DIM), SSD_CONV ** -0.5)
    ssd_conv_b = nrm(ks[4], (DEPTH, SSD_CONV_DIM), 0.02)
    dt0 = jnp.exp(jax.random.uniform(ks[5], (DEPTH, SSD_HEADS), f32,
                                     math.log(1e-3), math.log(1e-1)))
    ssd_dt_bias = dt0 + jnp.log(-jnp.expm1(-dt0))
    ssd_a_log = jnp.log(jax.random.uniform(ks[6], (DEPTH, SSD_HEADS), f32, 1.0, 16.0))
    ssd_d = 1.0 + nrm(ks[7], (DEPTH, SSD_HEADS), 0.1)
    ssd_norm_w = 1.0 + nrm(ks[8], (DEPTH, SSD_WIDTH), 0.02)
    attn_sinks = nrm(ks[9], (DEPTH, ATTN_Q_HEADS), 1.0)
    conf_dw_w = nrm(ks[10], (DEPTH, CONF_KERNEL, CONF_WIDTH), CONF_KERNEL ** -0.5)
    conf_dw_b = nrm(ks[11], (DEPTH, CONF_WIDTH), 0.02)
    conf_ln_w = 1.0 + nrm(ks[12], (DEPTH, CONF_WIDTH), 0.02)
    conf_ln_b = nrm(ks[13], (DEPTH, CONF_WIDTH), 0.02)
    w_out = nrm(ks[14], (DEPTH, MIX_WIDTH, D_MODEL), MIX_WIDTH ** -0.5)
    final_norm_w = 1.0 + nrm(ks[15], (D_MODEL,), 0.02)
    return {"x": x, "norm_w": norm_w, "w_in": w_in, "ssd_conv_w": ssd_conv_w,
            "ssd_conv_b": ssd_conv_b, "ssd_dt_bias": ssd_dt_bias, "ssd_a_log": ssd_a_log,
            "ssd_d": ssd_d, "ssd_norm_w": ssd_norm_w, "attn_sinks": attn_sinks,
            "conf_dw_w": conf_dw_w, "conf_dw_b": conf_dw_b, "conf_ln_w": conf_ln_w,
            "conf_ln_b": conf_ln_b, "w_out": w_out, "final_norm_w": final_norm_w}


def _fwd_reference(x, norm_w, w_in, ssd_conv_w, ssd_conv_b, ssd_dt_bias, ssd_a_log, ssd_d,
              ssd_norm_w, attn_sinks, conf_dw_w, conf_dw_b, conf_ln_w, conf_ln_b, w_out,
              final_norm_w):
    for i in range(DEPTH):
        h = rmsnorm(x, norm_w[i])
        x = x + hybrid_mixer(h, w_in[i], ssd_conv_w[i], ssd_conv_b[i], ssd_dt_bias[i],
                             ssd_a_log[i], ssd_d[i], ssd_norm_w[i], attn_sinks[i],
                             conf_dw_w[i], conf_dw_b[i], conf_ln_w[i], conf_ln_b[i], w_out[i])
    return rmsnorm(x, final_norm_w)


import jax as _jax
import jax.numpy as _jnp

TWIN_FORMAT = 'train_step'
FWD_PARAMS = ['x', 'norm_w', 'w_in', 'ssd_conv_w', 'ssd_conv_b', 'ssd_dt_bias', 'ssd_a_log', 'ssd_d', 'ssd_norm_w', 'attn_sinks', 'conf_dw_w', 'conf_dw_b', 'conf_ln_w', 'conf_ln_b', 'w_out', 'final_norm_w']
TWIN_WEIGHTS = ['norm_w', 'w_in', 'ssd_conv_w', 'ssd_conv_b', 'ssd_dt_bias', 'ssd_a_log', 'ssd_d', 'ssd_norm_w', 'attn_sinks', 'conf_dw_w', 'conf_dw_b', 'conf_ln_w', 'conf_ln_b', 'w_out', 'final_norm_w']
TWIN_DIFF_INPUT = 'x'
TWIN_INPUTS = ['x', 'norm_w', 'w_in', 'ssd_conv_w', 'ssd_conv_b', 'ssd_dt_bias', 'ssd_a_log', 'ssd_d', 'ssd_norm_w', 'attn_sinks', 'conf_dw_w', 'conf_dw_b', 'conf_ln_w', 'conf_ln_b', 'w_out', 'final_norm_w', 'loss_target', 'm_norm_w', 'm_w_in', 'm_ssd_conv_w', 'm_ssd_conv_b', 'm_ssd_dt_bias', 'm_ssd_a_log', 'm_ssd_d', 'm_ssd_norm_w', 'm_attn_sinks', 'm_conf_dw_w', 'm_conf_dw_b', 'm_conf_ln_w', 'm_conf_ln_b', 'm_w_out', 'm_final_norm_w', 'v_norm_w', 'v_w_in', 'v_ssd_conv_w', 'v_ssd_conv_b', 'v_ssd_dt_bias', 'v_ssd_a_log', 'v_ssd_d', 'v_ssd_norm_w', 'v_attn_sinks', 'v_conf_dw_w', 'v_conf_dw_b', 'v_conf_ln_w', 'v_conf_ln_b', 'v_w_out', 'v_final_norm_w']
TWIN_OUTPUTS = ['loss', 'grad_x', 'grad_norm_w', 'grad_w_in', 'grad_ssd_conv_w', 'grad_ssd_conv_b', 'grad_ssd_dt_bias', 'grad_ssd_a_log', 'grad_ssd_d', 'grad_ssd_norm_w', 'grad_attn_sinks', 'grad_conf_dw_w', 'grad_conf_dw_b', 'grad_conf_ln_w', 'grad_conf_ln_b', 'grad_w_out', 'grad_final_norm_w', 'delta_norm_w', 'delta_w_in', 'delta_ssd_conv_w', 'delta_ssd_conv_b', 'delta_ssd_dt_bias', 'delta_ssd_a_log', 'delta_ssd_d', 'delta_ssd_norm_w', 'delta_attn_sinks', 'delta_conf_dw_w', 'delta_conf_dw_b', 'delta_conf_ln_w', 'delta_conf_ln_b', 'delta_w_out', 'delta_final_norm_w', 'new_m_norm_w', 'new_m_w_in', 'new_m_ssd_conv_w', 'new_m_ssd_conv_b', 'new_m_ssd_dt_bias', 'new_m_ssd_a_log', 'new_m_ssd_d', 'new_m_ssd_norm_w', 'new_m_attn_sinks', 'new_m_conf_dw_w', 'new_m_conf_dw_b', 'new_m_conf_ln_w', 'new_m_conf_ln_b', 'new_m_w_out', 'new_m_final_norm_w', 'new_v_norm_w', 'new_v_w_in', 'new_v_ssd_conv_w', 'new_v_ssd_conv_b', 'new_v_ssd_dt_bias', 'new_v_ssd_a_log', 'new_v_ssd_d', 'new_v_ssd_norm_w', 'new_v_attn_sinks', 'new_v_conf_dw_w', 'new_v_conf_dw_b', 'new_v_conf_ln_w', 'new_v_conf_ln_b', 'new_v_w_out', 'new_v_final_norm_w']
TWIN_LEAF_KINDS = {'loss': 'loss', 'grad_x': 'grad_x', 'grad_norm_w': 'grad_w', 'grad_w_in': 'grad_w', 'grad_ssd_conv_w': 'grad_w', 'grad_ssd_conv_b': 'grad_w', 'grad_ssd_dt_bias': 'grad_w', 'grad_ssd_a_log': 'grad_w', 'grad_ssd_d': 'grad_w', 'grad_ssd_norm_w': 'grad_w', 'grad_attn_sinks': 'grad_w', 'grad_conf_dw_w': 'grad_w', 'grad_conf_dw_b': 'grad_w', 'grad_conf_ln_w': 'grad_w', 'grad_conf_ln_b': 'grad_w', 'grad_w_out': 'grad_w', 'grad_final_norm_w': 'grad_w', 'delta_norm_w': 'delta_w', 'delta_w_in': 'delta_w', 'delta_ssd_conv_w': 'delta_w', 'delta_ssd_conv_b': 'delta_w', 'delta_ssd_dt_bias': 'delta_w', 'delta_ssd_a_log': 'delta_w', 'delta_ssd_d': 'delta_w', 'delta_ssd_norm_w': 'delta_w', 'delta_attn_sinks': 'delta_w', 'delta_conf_dw_w': 'delta_w', 'delta_conf_dw_b': 'delta_w', 'delta_conf_ln_w': 'delta_w', 'delta_conf_ln_b': 'delta_w', 'delta_w_out': 'delta_w', 'delta_final_norm_w': 'delta_w', 'new_m_norm_w': 'new_m', 'new_m_w_in': 'new_m', 'new_m_ssd_conv_w': 'new_m', 'new_m_ssd_conv_b': 'new_m', 'new_m_ssd_dt_bias': 'new_m', 'new_m_ssd_a_log': 'new_m', 'new_m_ssd_d': 'new_m', 'new_m_ssd_norm_w': 'new_m', 'new_m_attn_sinks': 'new_m', 'new_m_conf_dw_w': 'new_m', 'new_m_conf_dw_b': 'new_m', 'new_m_conf_ln_w': 'new_m', 'new_m_conf_ln_b': 'new_m', 'new_m_w_out': 'new_m', 'new_m_final_norm_w': 'new_m', 'new_v_norm_w': 'new_v', 'new_v_w_in': 'new_v', 'new_v_ssd_conv_w': 'new_v', 'new_v_ssd_conv_b': 'new_v', 'new_v_ssd_dt_bias': 'new_v', 'new_v_ssd_a_log': 'new_v', 'new_v_ssd_d': 'new_v', 'new_v_ssd_norm_w': 'new_v', 'new_v_attn_sinks': 'new_v', 'new_v_conf_dw_w': 'new_v', 'new_v_conf_dw_b': 'new_v', 'new_v_conf_ln_w': 'new_v', 'new_v_conf_ln_b': 'new_v', 'new_v_w_out': 'new_v', 'new_v_final_norm_w': 'new_v'}


def _forward(args):
    return _fwd_reference(*[args[k] for k in FWD_PARAMS])


def _output_shape():
    out = _jax.eval_shape(lambda: _forward(_fwd_setup_inputs(0)))
    return out.shape, out.dtype

N_MICROBATCH = 1
ADAM_LR = 0.001
ADAM_B1 = 0.9
ADAM_B2 = 0.999
ADAM_EPS = 1e-08
ADAM_WD = 0.01
ADAM_STEP = 10
PER_EXAMPLE_BATCH_AXIS = {'x': 0, 'loss_target': 0}
SHARED_INPUTS = []
_WEIGHT_DTYPES = {'norm_w': _jnp.float32, 'w_in': _jnp.float32, 'ssd_conv_w': _jnp.float32, 'ssd_conv_b': _jnp.float32, 'ssd_dt_bias': _jnp.float32, 'ssd_a_log': _jnp.float32, 'ssd_d': _jnp.float32, 'ssd_norm_w': _jnp.float32, 'attn_sinks': _jnp.float32, 'conf_dw_w': _jnp.float32, 'conf_dw_b': _jnp.float32, 'conf_ln_w': _jnp.float32, 'conf_ln_b': _jnp.float32, 'w_out': _jnp.float32, 'final_norm_w': _jnp.float32}
MOMENT_SCALE = {'norm_w': 1.646917e-01, 'w_in': 7.039067e-02, 'ssd_conv_w': 9.145070e-02, 'ssd_conv_b': 1.231953e-01, 'ssd_dt_bias': 2.207949e-01, 'ssd_a_log': 2.519799e-01, 'ssd_d': 6.250658e-01, 'ssd_norm_w': 1.030664e-01, 'attn_sinks': 1.066887e-02, 'conf_dw_w': 3.792795e-02, 'conf_dw_b': 8.050204e-02, 'conf_ln_w': 4.430599e-02, 'conf_ln_b': 3.734898e-02, 'w_out': 1.062584e-01, 'final_norm_w': 3.197652e+01}


def _to_microbatches(a, axis):
    t = _jnp.moveaxis(a, axis, 0)
    t = t.reshape((N_MICROBATCH, t.shape[0] // N_MICROBATCH) + t.shape[1:])
    return _jnp.moveaxis(t, 1, axis + 1)


def setup_inputs(seed: int = 0) -> dict:
    inp = _fwd_setup_inputs(seed)
    key = _jax.random.fold_in(_jax.random.key(seed), 7919)
    shape, _ = _output_shape()
    out = dict(inp)
    out["loss_target"] = _jax.random.normal(_jax.random.fold_in(key, 0), shape, _jnp.float32)
    for i, name in enumerate(TWIN_WEIGHTS):
        w = inp[name].astype(_jnp.float32)
        if MOMENT_SCALE is None:
            s = _jnp.sqrt(_jnp.mean(_jnp.square(w)) + 1e-30)
        else:
            s = MOMENT_SCALE[name]
        km, kv = _jax.random.split(_jax.random.fold_in(key, i + 1))
        out[name] = w
        out["m_" + name] = s * _jax.random.normal(km, w.shape, _jnp.float32)
        out["v_" + name] = (s * s) * _jax.random.uniform(kv, w.shape, _jnp.float32, 0.5, 1.5)
    if N_MICROBATCH > 1:
        for name, axis in PER_EXAMPLE_BATCH_AXIS.items():
            out[name] = _to_microbatches(out[name], axis)
    return {'x': out['x'], 'norm_w': out['norm_w'], 'w_in': out['w_in'], 'ssd_conv_w': out['ssd_conv_w'], 'ssd_conv_b': out['ssd_conv_b'], 'ssd_dt_bias': out['ssd_dt_bias'], 'ssd_a_log': out['ssd_a_log'], 'ssd_d': out['ssd_d'], 'ssd_norm_w': out['ssd_norm_w'], 'attn_sinks': out['attn_sinks'], 'conf_dw_w': out['conf_dw_w'], 'conf_dw_b': out['conf_dw_b'], 'conf_ln_w': out['conf_ln_w'], 'conf_ln_b': out['conf_ln_b'], 'w_out': out['w_out'], 'final_norm_w': out['final_norm_w'], 'loss_target': out['loss_target'], 'm_norm_w': out['m_norm_w'], 'm_w_in': out['m_w_in'], 'm_ssd_conv_w': out['m_ssd_conv_w'], 'm_ssd_conv_b': out['m_ssd_conv_b'], 'm_ssd_dt_bias': out['m_ssd_dt_bias'], 'm_ssd_a_log': out['m_ssd_a_log'], 'm_ssd_d': out['m_ssd_d'], 'm_ssd_norm_w': out['m_ssd_norm_w'], 'm_attn_sinks': out['m_attn_sinks'], 'm_conf_dw_w': out['m_conf_dw_w'], 'm_conf_dw_b': out['m_conf_dw_b'], 'm_conf_ln_w': out['m_conf_ln_w'], 'm_conf_ln_b': out['m_conf_ln_b'], 'm_w_out': out['m_w_out'], 'm_final_norm_w': out['m_final_norm_w'], 'v_norm_w': out['v_norm_w'], 'v_w_in': out['v_w_in'], 'v_ssd_conv_w': out['v_ssd_conv_w'], 'v_ssd_conv_b': out['v_ssd_conv_b'], 'v_ssd_dt_bias': out['v_ssd_dt_bias'], 'v_ssd_a_log': out['v_ssd_a_log'], 'v_ssd_d': out['v_ssd_d'], 'v_ssd_norm_w': out['v_ssd_norm_w'], 'v_attn_sinks': out['v_attn_sinks'], 'v_conf_dw_w': out['v_conf_dw_w'], 'v_conf_dw_b': out['v_conf_dw_b'], 'v_conf_ln_w': out['v_conf_ln_w'], 'v_conf_ln_b': out['v_conf_ln_b'], 'v_w_out': out['v_w_out'], 'v_final_norm_w': out['v_final_norm_w']}


def _loss(weights, diff, rest, loss_target):
    with _jax.named_scope("forward"):
        args = {**rest, TWIN_DIFF_INPUT: diff, **{k: w.astype(_WEIGHT_DTYPES[k]) for k, w in weights.items()}}
        y = _forward(args)
    with _jax.named_scope("loss_head"):
        err = _jnp.square(y.astype(_jnp.float32) - loss_target)
        return 0.5 * _jnp.sum(_jnp.mean(err, axis=-1)) if err.ndim else 0.5 * err


def _adamw(w, g, m, v):
    m = ADAM_B1 * m + (1.0 - ADAM_B1) * g
    v = ADAM_B2 * v + (1.0 - ADAM_B2) * _jnp.square(g)
    m_hat = m / (1.0 - ADAM_B1 ** ADAM_STEP)
    v_hat = v / (1.0 - ADAM_B2 ** ADAM_STEP)
    delta = -ADAM_LR * (m_hat / (_jnp.sqrt(v_hat) + ADAM_EPS) + ADAM_WD * w)
    return delta, m, v


def reference(x, norm_w, w_in, ssd_conv_w, ssd_conv_b, ssd_dt_bias, ssd_a_log, ssd_d, ssd_norm_w, attn_sinks, conf_dw_w, conf_dw_b, conf_ln_w, conf_ln_b, w_out, final_norm_w, loss_target, m_norm_w, m_w_in, m_ssd_conv_w, m_ssd_conv_b, m_ssd_dt_bias, m_ssd_a_log, m_ssd_d, m_ssd_norm_w, m_attn_sinks, m_conf_dw_w, m_conf_dw_b, m_conf_ln_w, m_conf_ln_b, m_w_out, m_final_norm_w, v_norm_w, v_w_in, v_ssd_conv_w, v_ssd_conv_b, v_ssd_dt_bias, v_ssd_a_log, v_ssd_d, v_ssd_norm_w, v_attn_sinks, v_conf_dw_w, v_conf_dw_b, v_conf_ln_w, v_conf_ln_b, v_w_out, v_final_norm_w):
    given = dict(x=x, norm_w=norm_w, w_in=w_in, ssd_conv_w=ssd_conv_w, ssd_conv_b=ssd_conv_b, ssd_dt_bias=ssd_dt_bias, ssd_a_log=ssd_a_log, ssd_d=ssd_d, ssd_norm_w=ssd_norm_w, attn_sinks=attn_sinks, conf_dw_w=conf_dw_w, conf_dw_b=conf_dw_b, conf_ln_w=conf_ln_w, conf_ln_b=conf_ln_b, w_out=w_out, final_norm_w=final_norm_w, loss_target=loss_target, m_norm_w=m_norm_w, m_w_in=m_w_in, m_ssd_conv_w=m_ssd_conv_w, m_ssd_conv_b=m_ssd_conv_b, m_ssd_dt_bias=m_ssd_dt_bias, m_ssd_a_log=m_ssd_a_log, m_ssd_d=m_ssd_d, m_ssd_norm_w=m_ssd_norm_w, m_attn_sinks=m_attn_sinks, m_conf_dw_w=m_conf_dw_w, m_conf_dw_b=m_conf_dw_b, m_conf_ln_w=m_conf_ln_w, m_conf_ln_b=m_conf_ln_b, m_w_out=m_w_out, m_final_norm_w=m_final_norm_w, v_norm_w=v_norm_w, v_w_in=v_w_in, v_ssd_conv_w=v_ssd_conv_w, v_ssd_conv_b=v_ssd_conv_b, v_ssd_dt_bias=v_ssd_dt_bias, v_ssd_a_log=v_ssd_a_log, v_ssd_d=v_ssd_d, v_ssd_norm_w=v_ssd_norm_w, v_attn_sinks=v_attn_sinks, v_conf_dw_w=v_conf_dw_w, v_conf_dw_b=v_conf_dw_b, v_conf_ln_w=v_conf_ln_w, v_conf_ln_b=v_conf_ln_b, v_w_out=v_w_out, v_final_norm_w=v_final_norm_w)
    weights = {n: given[n] for n in TWIN_WEIGHTS}
    shared = {n: given[n] for n in SHARED_INPUTS}
    per_example = {n: given[n] for n in ['x']}
    grad_fn = _jax.value_and_grad(_loss, argnums=(0, 1))

    def one_microbatch(ex, loss_target):
        ex = dict(ex)
        diff = ex.pop(TWIN_DIFF_INPUT)
        return grad_fn(weights, diff, {**shared, **ex}, loss_target)

    if N_MICROBATCH == 1:
        loss, (grad_w, grad_x) = one_microbatch(per_example, given["loss_target"])
    else:
        def body(carry, xs):
            loss_sum, grad_sum = carry
            l_k, (gw_k, gx_k) = one_microbatch(xs[0], xs[1])
            with _jax.named_scope("update"):
                return (loss_sum + l_k, _jax.tree.map(_jnp.add, grad_sum, gw_k)), gx_k

        init = (_jnp.zeros((), _jnp.float32), _jax.tree.map(_jnp.zeros_like, weights))
        (loss, grad_w), grad_x = _jax.lax.scan(body, init, (per_example, given["loss_target"]))
    with _jax.named_scope("update"):
        delta_w, new_m, new_v = {}, {}, {}
        for n in TWIN_WEIGHTS:
            delta_w[n], new_m[n], new_v[n] = _adamw(weights[n], grad_w[n], given["m_" + n], given["v_" + n])
    return (loss, grad_x, *[grad_w[n] for n in TWIN_WEIGHTS], *[delta_w[n] for n in TWIN_WEIGHTS],
            *[new_m[n] for n in TWIN_WEIGHTS], *[new_v[n] for n in TWIN_WEIGHTS])
```

```python
import functools

import jax
import jax.numpy as jnp
from jax import lax
from jax.experimental import pallas as pl
from jax.experimental.pallas import tpu as pltpu

F32 = jnp.float32
BF16 = jnp.bfloat16
MESH_AXES = ("x", "y", "c")
N_DEV = 8
EPS = 1e-5

D_MODEL = 1024
CHUNK = 128
SSD_HEADS = 16
SSD_HEAD_DIM = 64
SSD_STATE = 128
ATTN_HEADS = 8
ATTN_HEAD_DIM = 64
CONF_KERNEL = 31
MIX_WIDTH = 2048
D_IN_PROJ = 5392
C_Z = 0
C_CA = 2048
C_XBC = 3072
C_Q = 4608
C_K = 5120
C_V = 5248
C_DT = 5376
PROJ_W = 5632
N_COL_TILES = 4
COL_TILE = PROJ_W // N_COL_TILES
XBC_HALO = 8
CONF_HALO = 32
VMEM_LIMIT = 56 * 1024 * 1024

ADAM_LR = 0.001
ADAM_B1 = 0.9
ADAM_B2 = 0.999
ADAM_EPS = 1e-08
ADAM_WD = 0.01
ADAM_STEP = 10


def _silu(v):
    return v * jax.nn.sigmoid(v)


def _softplus(v):
    return jnp.maximum(v, 0.0) + jnp.log1p(jnp.exp(-jnp.abs(v)))


def _rmsnorm(v, w):
    return v * lax.rsqrt(jnp.mean(v * v, axis=-1, keepdims=True) + EPS) * w


def _dot(a, b):
    return jnp.dot(a.astype(BF16), b.astype(BF16), preferred_element_type=F32)


def _dot_nt(a, b):
    return lax.dot_general(a.astype(BF16), b.astype(BF16), (((1,), (1,)), ((), ())), preferred_element_type=F32)


def _dot_tn(a, b):
    return lax.dot_general(a.astype(BF16), b.astype(BF16), (((0,), (0,)), ((), ())), preferred_element_type=F32)


def _mixer_chunk(cur, p_xbc, p_kv, p_cc, s_in, cw, cb, dtb, alog, dsk, nw, snk, dww, dwb, lnw, lnb, kvmask):
    qn = cur.shape[0]
    z_ssd = cur[:, 0:1024]
    z_attn = cur[:, 1024:1536]
    z_conf = cur[:, 1536:2048]
    ca = cur[:, C_CA:C_CA + 512]
    cg = cur[:, C_CA + 512:C_CA + 1024]
    xbc = cur[:, C_XBC:C_XBC + 1536]
    q = cur[:, C_Q:C_Q + 512]
    k = cur[:, C_K:C_K + 128]
    v = cur[:, C_V:C_V + 128]
    dtr = cur[:, C_DT:C_DT + 128]

    ext = jnp.concatenate([p_xbc, xbc], axis=0)
    acc = cb
    for t in range(4):
        off = XBC_HALO - 3 + t
        acc = acc + cw[t:t + 1, :] * ext[off:off + qn, :]
    xa = _silu(acc)
    xs = xa[:, 0:1024]
    dt = _softplus(dtr + dtb)
    a = dt * (-jnp.exp(alog))
    rows = lax.broadcasted_iota(jnp.int32, (qn, qn), 0)
    cols = lax.broadcasted_iota(jnp.int32, (qn, qn), 1)
    causal = rows >= cols
    a_cs = jnp.dot(causal.astype(F32), a, precision=lax.Precision.HIGHEST, preferred_element_type=F32)
    a_cs_t = a_cs.T
    a_last = a_cs[qn - 1:qn, :]
    heads_per_group = SSD_HEADS // 2
    ys = []
    s_out = []
    for g in range(2):
        bg = xa[:, 1024 + g * 128:1024 + (g + 1) * 128]
        cgm = xa[:, 1280 + g * 128:1280 + (g + 1) * 128]
        cbm = _dot_nt(cgm, bg)
        for h in range(g * heads_per_group, (g + 1) * heads_per_group):
            col = a_cs[:, h:h + 1]
            seg = col - a_cs_t[h:h + 1, :]
            lmat = jnp.exp(jnp.where(causal, seg, -jnp.inf))
            xh = xs[:, h * 64:(h + 1) * 64]
            xdt = xh * dt[:, h:h + 1]
            y_diag = _dot(cbm * lmat, xdt)
            sprev = s_in[h * 64:(h + 1) * 64, :]
            y_off = _dot_nt(cgm, sprev) * jnp.exp(col)
            last = a_last[:, h:h + 1]
            st = _dot_tn(xdt * jnp.exp(last - col), bg)
            s_out.append(jnp.exp(last) * sprev + st)
            ys.append(y_diag + y_off + dsk[:, h:h + 1] * xh)
    gated = jnp.concatenate(ys, axis=1) * _silu(z_ssd)
    halves = []
    for g in range(2):
        gg = gated[:, g * 512:(g + 1) * 512]
        halves.append(gg * lax.rsqrt(jnp.mean(gg * gg, axis=-1, keepdims=True) + EPS))
    y_ssd = jnp.concatenate(halves, axis=1) * nw

    kk = jnp.concatenate([p_kv[:, 0:128], k], axis=0)
    vv = jnp.concatenate([p_kv[:, 128:256], v], axis=0)
    outs = []
    for hq in range(ATTN_HEADS):
        g = hq // (ATTN_HEADS // 2)
        qh = q[:, hq * 64:(hq + 1) * 64]
        s = _dot_nt(qh, kk[:, g * 64:(g + 1) * 64]) * (ATTN_HEAD_DIM ** -0.5)
        s = jnp.where(kvmask, s, -jnp.inf)
        sk = snk[:, hq:hq + 1]
        m = lax.stop_gradient(jnp.maximum(jnp.max(s, axis=-1, keepdims=True), sk))
        e = jnp.exp(s - m)
        den = jnp.sum(e, axis=-1, keepdims=True) + jnp.exp(sk - m)
        outs.append(_dot(e / den, vv[:, g * 64:(g + 1) * 64]))
    y_attn = jnp.concatenate(outs, axis=1) * _silu(z_attn)

    c0 = ca * jax.nn.sigmoid(cg)
    pc0 = p_cc[:, 0:512] * jax.nn.sigmoid(p_cc[:, 512:1024])
    ext2 = jnp.concatenate([pc0, c0], axis=0)
    acc2 = dwb
    for t in range(CONF_KERNEL):
        off = CONF_HALO - (CONF_KERNEL - 1) + t
        acc2 = acc2 + dww[t:t + 1, :] * ext2[off:off + qn, :]
    mu = jnp.mean(acc2, axis=-1, keepdims=True)
    xc = acc2 - mu
    yln = xc * lax.rsqrt(jnp.mean(xc * xc, axis=-1, keepdims=True) + EPS) * lnw + lnb
    y_conf = _silu(yln) * _silu(z_conf)

    return jnp.concatenate([y_ssd, y_attn, y_conf], axis=1), jnp.concatenate(s_out, axis=0)


def _kv_mask(qn, not_first):
    ii = lax.broadcasted_iota(jnp.int32, (qn, 2 * qn), 0)
    jj = lax.broadcasted_iota(jnp.int32, (qn, 2 * qn), 1)
    d = jj - ii
    return (d >= 1) & (d <= qn) & (not_first | (jj >= qn))


def _my_place():
    return lax.axis_index("x"), lax.axis_index("y"), lax.axis_index("c")


def _all_gather(arrs, name):
    n = len(arrs)

    def body(*refs):
        ins, outs = refs[:n], refs[n:2 * n]
        send_sems, recv_sems, local_sems = refs[2 * n:]
        x, y, c = _my_place()
        me, sibling = (x, y, c), (x, y, 1 - c)
        chips = [(1 - x, y), (x, 1 - y), (1 - x, 1 - y)]

        def slot(a, p):
            return outs[a].at[4 * p[0] + 2 * p[1] + p[2]]

        def copy(a, kk, block, to, src=None):
            return pltpu.make_async_remote_copy(
                src_ref=slot(a, block) if src is None else src, dst_ref=slot(a, block),
                send_sem=send_sems.at[a, kk], recv_sem=recv_sems.at[a, kk],
                device_id=to, device_id_type=pl.DeviceIdType.MESH)

        mine = [pltpu.make_async_copy(ins[a], slot(a, me), local_sems.at[a]) for a in range(n)]
        for cp in mine:
            cp.start()
        first = []
        for a in range(n):
            first.append(copy(a, 0, me, sibling, src=ins[a]))
            first += [copy(a, 1 + j, me, (*chip, c), src=ins[a]) for j, chip in enumerate(chips)]
        for cp in first:
            cp.start()
        passed = []
        for j, chip in enumerate(chips):
            for a in range(n):
                copy(a, 1 + j, (*chip, c), me).wait_recv()
                fwd = copy(a, 4 + j, (*chip, c), sibling)
                fwd.start()
                passed.append(fwd)
        for a in range(n):
            copy(a, 0, sibling, me).wait_recv()
            for j, chip in enumerate(chips):
                copy(a, 4 + j, (*chip, 1 - c), me).wait_recv()
        for cp in first + passed:
            cp.wait_send()
        for cp in mine:
            cp.wait()

    any_spec = pl.BlockSpec(memory_space=pl.ANY)
    return pl.pallas_call(
        body, name=name,
        out_shape=[jax.ShapeDtypeStruct((N_DEV,) + a.shape, a.dtype) for a in arrs],
        in_specs=[any_spec] * n, out_specs=[any_spec] * n,
        scratch_shapes=[pltpu.SemaphoreType.DMA((n, 7)), pltpu.SemaphoreType.DMA((n, 7)),
                        pltpu.SemaphoreType.DMA((n,))],
    )(*arrs)


def _exchange_grads(big, small, name):
    n = len(big)
    ns = small.shape[1]

    def body(*refs):
        ins, small_ref = refs[:n], refs[n]
        outs, sum_ref = refs[n + 1:2 * n + 1], refs[2 * n + 1]
        small_all, send_sems, recv_sems, local_sems = refs[2 * n + 2:]
        x, y, c = _my_place()
        me_idx = 4 * x + 2 * y + c
        local = [pltpu.make_async_copy(ins[a].at[me_idx], outs[a].at[me_idx], local_sems.at[a]) for a in range(n)]
        for cp in local:
            cp.start()
        small_all[me_idx] = small_ref[...]
        copies = []
        for rel in range(1, N_DEV):
            px = 1 - x if rel & 4 else x
            py = 1 - y if rel & 2 else y
            pc = 1 - c if rel & 1 else c
            peer_idx = 4 * px + 2 * py + pc
            for a in range(n):
                copies.append(pltpu.make_async_remote_copy(
                    src_ref=ins[a].at[peer_idx], dst_ref=outs[a].at[me_idx],
                    send_sem=send_sems.at[a, rel - 1], recv_sem=recv_sems.at[a, rel - 1],
                    device_id=(px, py, pc), device_id_type=pl.DeviceIdType.MESH))
            copies.append(pltpu.make_async_remote_copy(
                src_ref=small_ref, dst_ref=small_all.at[me_idx],
                send_sem=send_sems.at[n, rel - 1], recv_sem=recv_sems.at[n, rel - 1],
                device_id=(px, py, pc), device_id_type=pl.DeviceIdType.MESH))
        for cp in copies:
            cp.start()
        for cp in copies:
            cp.wait()
        for cp in local:
            cp.wait()
        total = small_all[0]
        for i in range(1, N_DEV):
            total = total + small_all[i]
        sum_ref[...] = total

    any_spec = pl.BlockSpec(memory_space=pl.ANY)
    vmem_spec = pl.BlockSpec(memory_space=pltpu.VMEM)
    res = pl.pallas_call(
        body, name=name,
        out_shape=[jax.ShapeDtypeStruct(a.shape, a.dtype) for a in big] + [jax.ShapeDtypeStruct((1, ns), F32)],
        in_specs=[any_spec] * n + [vmem_spec], out_specs=[any_spec] * n + [vmem_spec],
        scratch_shapes=[pltpu.VMEM((N_DEV, 1, ns), F32),
                        pltpu.SemaphoreType.DMA((n + 1, 7)), pltpu.SemaphoreType.DMA((n + 1, 7)),
                        pltpu.SemaphoreType.DMA((n,))],
    )(*big, small)
    return res[:n], res[n]


def _full(shape):
    return pl.BlockSpec(shape, lambda *_: (0,) * len(shape))


def _inproj_fwd(x, nw, w, name):
    t = x.shape[0]
    tm = 256

    def body(x_ref, nw_ref, w_ref, proj_ref, h_ref):
        h = _rmsnorm(x_ref[...], nw_ref[...]).astype(BF16)
        h_ref[...] = h
        for j in range(N_COL_TILES):
            sl = slice(j * COL_TILE, (j + 1) * COL_TILE)
            proj_ref[:, sl] = jnp.dot(h, w_ref[:, sl], preferred_element_type=F32)

    return pl.pallas_call(
        body, name=name, grid=(t // tm,),
        out_shape=[jax.ShapeDtypeStruct((t, PROJ_W), F32), jax.ShapeDtypeStruct((t, D_MODEL), BF16)],
        in_specs=[pl.BlockSpec((tm, D_MODEL), lambda i: (i, 0)), _full((1, D_MODEL)), _full((D_MODEL, PROJ_W))],
        out_specs=[pl.BlockSpec((tm, PROJ_W), lambda i: (i, 0)), pl.BlockSpec((tm, D_MODEL), lambda i: (i, 0))],
        compiler_params=pltpu.CompilerParams(dimension_semantics=("arbitrary",), vmem_limit_bytes=VMEM_LIMIT),
    )(x, nw, w)


def _param_specs():
    return [_full((4, 1536)), _full((1, 1536)), _full((1, 128)), _full((1, 128)), _full((1, 128)),
            _full((1, 1024)), _full((1, 128)), _full((CONF_KERNEL, 512)), _full((1, 512)), _full((1, 512)),
            _full((1, 512))]


def _halo_specs(nc, chunk_of):
    def prev_chunk(b, j):
        return jnp.maximum(b * nc + chunk_of(j) - 1, 0)

    per_xbc = CHUNK // XBC_HALO
    per_cc = CHUNK // CONF_HALO
    return [
        pl.BlockSpec((XBC_HALO, 1536), lambda b, j: (prev_chunk(b, j) * per_xbc + per_xbc - 1, C_XBC // 1536)),
        pl.BlockSpec((CHUNK, 256), lambda b, j: (prev_chunk(b, j), C_K // 256)),
        pl.BlockSpec((CONF_HALO, 1024), lambda b, j: (prev_chunk(b, j) * per_cc + per_cc - 1, C_CA // 1024)),
    ]


def _mixer_fwd(x, proj, w_out, params, nb, name):
    t = x.shape[0]
    nc = t // nb // CHUNK

    def body(x_ref, cur_ref, pxbc_ref, pkv_ref, pcc_ref, wo_ref, *rest):
        prm = [r[...] for r in rest[:11]]
        xn_ref, sall_ref, s_scr = rest[11:]
        c = pl.program_id(1)
        not_first = c > 0
        nf = not_first.astype(F32)

        @pl.when(c == 0)
        def _():
            s_scr[...] = jnp.zeros_like(s_scr)

        s_in = s_scr[...]
        sall_ref[0] = s_in
        y_cat, s_out = _mixer_chunk(cur_ref[...], pxbc_ref[...] * nf, pkv_ref[...] * nf, pcc_ref[...] * nf, s_in,
                                    *prm, _kv_mask(CHUNK, not_first))
        s_scr[...] = s_out
        xn_ref[...] = x_ref[...] + _dot(y_cat, wo_ref[...])

    row = lambda b, j: (b * nc + j, 0)
    return pl.pallas_call(
        body, name=name, grid=(nb, nc),
        out_shape=[jax.ShapeDtypeStruct((t, D_MODEL), F32),
                   jax.ShapeDtypeStruct((nb * nc, SSD_HEADS * SSD_HEAD_DIM, SSD_STATE), F32)],
        in_specs=[pl.BlockSpec((CHUNK, D_MODEL), row), pl.BlockSpec((CHUNK, PROJ_W), row)]
                 + _halo_specs(nc, lambda j: j) + [_full((MIX_WIDTH, D_MODEL))] + _param_specs(),
        out_specs=[pl.BlockSpec((CHUNK, D_MODEL), row),
                   pl.BlockSpec((1, SSD_HEADS * SSD_HEAD_DIM, SSD_STATE), lambda b, j: (b * nc + j, 0, 0))],
        scratch_shapes=[pltpu.VMEM((SSD_HEADS * SSD_HEAD_DIM, SSD_STATE), F32)],
        compiler_params=pltpu.CompilerParams(dimension_semantics=("arbitrary", "arbitrary"),
                                             vmem_limit_bytes=VMEM_LIMIT),
    )(x, proj, proj, proj, proj, w_out, *params)


def _mixer_bwd(dxn, proj, s_all, w_out, params, nb, name):
    t = dxn.shape[0]
    nc = t // nb // CHUNK
    n_prm = 11

    def body(dxn_ref, cur_ref, pxbc_ref, pkv_ref, pcc_ref, s_ref, wo_ref, *rest):
        prm = [r[...] for r in rest[:n_prm]]
        dproj_ref, gwo_ref = rest[n_prm:n_prm + 2]
        gprm = rest[n_prm + 2:2 * n_prm + 2]
        ds_scr, pend_xbc, pend_kv, pend_cc = rest[2 * n_prm + 2:]
        b, j = pl.program_id(0), pl.program_id(1)
        c = nc - 1 - j
        not_first = c > 0
        nf = not_first.astype(F32)

        @pl.when((b == 0) & (j == 0))
        def _():
            gwo_ref[...] = jnp.zeros_like(gwo_ref)
            for r in gprm:
                r[...] = jnp.zeros_like(r)

        @pl.when(j == 0)
        def _():
            ds_scr[...] = jnp.zeros_like(ds_scr)
            pend_xbc[...] = jnp.zeros_like(pend_xbc)
            pend_kv[...] = jnp.zeros_like(pend_kv)
            pend_cc[...] = jnp.zeros_like(pend_cc)

        fn = functools.partial(_mixer_chunk, kvmask=_kv_mask(CHUNK, not_first))
        (y_cat, _), vjp = jax.vjp(fn, cur_ref[...], pxbc_ref[...] * nf, pkv_ref[...] * nf, pcc_ref[...] * nf,
                                  s_ref[0], *prm)
        dxn_v = dxn_ref[...]
        gwo_ref[...] += _dot_tn(y_cat, dxn_v)
        grads = vjp((_dot_nt(dxn_v, wo_ref[...]), ds_scr[...]))
        dproj_ref[...] = grads[0]
        dproj_ref[CHUNK - XBC_HALO:CHUNK, C_XBC:C_XBC + 1536] += pend_xbc[...]
        dproj_ref[:, C_K:C_K + 256] += pend_kv[...]
        dproj_ref[CHUNK - CONF_HALO:CHUNK, C_CA:C_CA + 1024] += pend_cc[...]
        pend_xbc[...] = grads[1]
        pend_kv[...] = grads[2]
        pend_cc[...] = grads[3]
        ds_scr[...] = grads[4]
        for r, g in zip(gprm, grads[5:]):
            r[...] += g

    row = lambda b, j: (b * nc + nc - 1 - j, 0)
    prm_shapes = [(4, 1536), (1, 1536), (1, 128), (1, 128), (1, 128), (1, 1024), (1, 128), (CONF_KERNEL, 512),
                  (1, 512), (1, 512), (1, 512)]
    return pl.pallas_call(
        body, name=name, grid=(nb, nc),
        out_shape=[jax.ShapeDtypeStruct((t, PROJ_W), F32), jax.ShapeDtypeStruct((MIX_WIDTH, D_MODEL), F32)]
                  + [jax.ShapeDtypeStruct(s, F32) for s in prm_shapes],
        in_specs=[pl.BlockSpec((CHUNK, D_MODEL), row), pl.BlockSpec((CHUNK, PROJ_W), row)]
                 + _halo_specs(nc, lambda j: nc - 1 - j)
                 + [pl.BlockSpec((1, SSD_HEADS * SSD_HEAD_DIM, SSD_STATE), lambda b, j: (b * nc + nc - 1 - j, 0, 0)),
                    _full((MIX_WIDTH, D_MODEL))] + _param_specs(),
        out_specs=[pl.BlockSpec((CHUNK, PROJ_W), row), _full((MIX_WIDTH, D_MODEL))] + [_full(s) for s in prm_shapes],
        scratch_shapes=[pltpu.VMEM((SSD_HEADS * SSD_HEAD_DIM, SSD_STATE), F32), pltpu.VMEM((XBC_HALO, 1536), F32),
                        pltpu.VMEM((CHUNK, 256), F32), pltpu.VMEM((CONF_HALO, 1024), F32)],
        compiler_params=pltpu.CompilerParams(dimension_semantics=("arbitrary", "arbitrary"),
                                             vmem_limit_bytes=VMEM_LIMIT),
    )(dxn, proj, proj, proj, proj, s_all, w_out, *params)


def _inproj_bwd_x(dproj, w, x, nw, dxn, name):
    t = x.shape[0]
    tm = 256

    def body(dp_ref, w_ref, x_ref, nw_ref, dxn_ref, dx_ref, gnw_ref):
        @pl.when(pl.program_id(0) == 0)
        def _():
            gnw_ref[...] = jnp.zeros_like(gnw_ref)

        dh = jnp.zeros((tm, D_MODEL), F32)
        for j in range(N_COL_TILES):
            sl = slice(j * COL_TILE, (j + 1) * COL_TILE)
            dh = dh + _dot_nt(dp_ref[:, sl], w_ref[:, sl])
        _, vjp = jax.vjp(_rmsnorm, x_ref[...], nw_ref[...])
        dx, dnw = vjp(dh)
        dx_ref[...] = dxn_ref[...] + dx
        gnw_ref[...] += dnw

    tok = lambda i: (i, 0)
    return pl.pallas_call(
        body, name=name, grid=(t // tm,),
        out_shape=[jax.ShapeDtypeStruct((t, D_MODEL), F32), jax.ShapeDtypeStruct((1, D_MODEL), F32)],
        in_specs=[pl.BlockSpec((tm, PROJ_W), tok), _full((D_MODEL, PROJ_W)), pl.BlockSpec((tm, D_MODEL), tok),
                  _full((1, D_MODEL)), pl.BlockSpec((tm, D_MODEL), tok)],
        out_specs=[pl.BlockSpec((tm, D_MODEL), tok), _full((1, D_MODEL))],
        compiler_params=pltpu.CompilerParams(dimension_semantics=("arbitrary",), vmem_limit_bytes=VMEM_LIMIT),
    )(dproj, w, x, nw, dxn)


def _inproj_bwd_w(h, dproj, name):
    t = h.shape[0]
    tk = 512

    def body(h_ref, dp_ref, gw_ref):
        @pl.when(pl.program_id(1) == 0)
        def _():
            gw_ref[...] = jnp.zeros_like(gw_ref)

        gw_ref[...] += _dot_tn(h_ref[...], dp_ref[...])

    return pl.pallas_call(
        body, name=name, grid=(N_COL_TILES, t // tk),
        out_shape=jax.ShapeDtypeStruct((D_MODEL, PROJ_W), F32),
        in_specs=[pl.BlockSpec((tk, D_MODEL), lambda n, k: (k, 0)), pl.BlockSpec((tk, COL_TILE), lambda n, k: (k, n))],
        out_specs=pl.BlockSpec((D_MODEL, COL_TILE), lambda n, k: (0, n)),
        compiler_params=pltpu.CompilerParams(dimension_semantics=("arbitrary", "arbitrary"),
                                             vmem_limit_bytes=VMEM_LIMIT),
    )(h, dproj)


def _loss_head(x, fnw, target, name):
    t = x.shape[0]
    tm = 512

    def body(x_ref, w_ref, t_ref, dx_ref, loss_ref, gw_ref):
        @pl.when(pl.program_id(0) == 0)
        def _():
            loss_ref[...] = jnp.zeros_like(loss_ref)
            gw_ref[...] = jnp.zeros_like(gw_ref)

        y, vjp = jax.vjp(_rmsnorm, x_ref[...], w_ref[...])
        err = y - t_ref[...]
        loss_ref[...] += 0.5 * jnp.sum(jnp.mean(err * err, axis=-1, keepdims=True), axis=0, keepdims=True)
        dx, dw = vjp(err * (1.0 / D_MODEL))
        dx_ref[...] = dx
        gw_ref[...] += dw

    tok = lambda i: (i, 0)
    return pl.pallas_call(
        body, name=name, grid=(t // tm,),
        out_shape=[jax.ShapeDtypeStruct((t, D_MODEL), F32), jax.ShapeDtypeStruct((1, 1), F32),
                   jax.ShapeDtypeStruct((1, D_MODEL), F32)],
        in_specs=[pl.BlockSpec((tm, D_MODEL), tok), _full((1, D_MODEL)), pl.BlockSpec((tm, D_MODEL), tok)],
        out_specs=[pl.BlockSpec((tm, D_MODEL), tok), _full((1, 1)), _full((1, D_MODEL))],
        compiler_params=pltpu.CompilerParams(dimension_semantics=("arbitrary",)),
    )(x, fnw, target)


def _adamw(w, g, m, v):
    m = ADAM_B1 * m + (1.0 - ADAM_B1) * g
    v = ADAM_B2 * v + (1.0 - ADAM_B2) * jnp.square(g)
    m_hat = m / (1.0 - ADAM_B1 ** ADAM_STEP)
    v_hat = v / (1.0 - ADAM_B2 ** ADAM_STEP)
    delta = -ADAM_LR * (m_hat / (jnp.sqrt(v_hat) + ADAM_EPS) + ADAM_WD * w)
    return delta, m, v


def _reduce_adamw(parts, w, m, v, tr, name):
    p, r, cdim = parts.shape

    def body(p_ref, w_ref, m_ref, v_ref, g_ref, d_ref, nm_ref, nv_ref):
        g = p_ref[0]
        for i in range(1, p):
            g = g + p_ref[i]
        g_ref[...] = g
        d_ref[...], nm_ref[...], nv_ref[...] = _adamw(w_ref[...], g, m_ref[...], v_ref[...])

    blk = pl.BlockSpec((tr, cdim), lambda i: (i, 0))
    return pl.pallas_call(
        body, name=name, grid=(r // tr,),
        out_shape=[jax.ShapeDtypeStruct((r, cdim), F32)] * 4,
        in_specs=[pl.BlockSpec((p, tr, cdim), lambda i: (0, i, 0)), blk, blk, blk],
        out_specs=[blk] * 4,
        compiler_params=pltpu.CompilerParams(dimension_semantics=("arbitrary",), vmem_limit_bytes=VMEM_LIMIT),
    )(parts, w, m, v)


def _pad_lanes(v, width=128):
    return jnp.pad(v.reshape(1, -1), ((0, 0), (0, width - v.shape[-1])))


def _repack_w_in(g):
    w = jnp.transpose(g, (1, 0, 2)).reshape(D_MODEL, D_IN_PROJ)
    return jnp.concatenate([w[:, 0:2048], w[:, 4368:5392], w[:, 2048:3584], w[:, 3600:4368], w[:, 3584:3600],
                            jnp.zeros((D_MODEL, PROJ_W - D_IN_PROJ), w.dtype)], axis=1)


def _unpack_gw_in(g):
    w = jnp.concatenate([g[:, 0:2048], g[:, C_XBC:C_XBC + 1536], g[:, C_DT:C_DT + 16], g[:, C_Q:C_Q + 768],
                         g[:, C_CA:C_CA + 1024]], axis=1)
    return jnp.transpose(w.reshape(D_MODEL, N_DEV, D_IN_PROJ // N_DEV), (1, 0, 2))


SMALL_FIELDS = (("norm_w", 1024), ("conv_b", 1536), ("dt_bias", 128), ("a_log", 128), ("d_skip", 128),
                ("ssd_norm_w", 1024), ("sinks", 128), ("dw_b", 512), ("ln_w", 512), ("ln_b", 512))


def kernel(x, norm_w, w_in, ssd_conv_w, ssd_conv_b, ssd_dt_bias, ssd_a_log, ssd_d, ssd_norm_w, attn_sinks, conf_dw_w, conf_dw_b, conf_ln_w, conf_ln_b, w_out, final_norm_w, loss_target, m_norm_w, m_w_in, m_ssd_conv_w, m_ssd_conv_b, m_ssd_dt_bias, m_ssd_a_log, m_ssd_d, m_ssd_norm_w, m_attn_sinks, m_conf_dw_w, m_conf_dw_b, m_conf_ln_w, m_conf_ln_b, m_w_out, m_final_norm_w, v_norm_w, v_w_in, v_ssd_conv_w, v_ssd_conv_b, v_ssd_dt_bias, v_ssd_a_log, v_ssd_d, v_ssd_norm_w, v_attn_sinks, v_conf_dw_w, v_conf_dw_b, v_conf_ln_w, v_conf_ln_b, v_w_out, v_final_norm_w):
    nb, seq, _ = x.shape
    depth = norm_w.shape[0]
    t = nb * seq
    me_idx = 4 * lax.axis_index("x") + 2 * lax.axis_index("y") + lax.axis_index("c")

    g_win, g_wout, g_cw, g_dw = _all_gather(
        [w_in.astype(BF16), w_out.astype(BF16), ssd_conv_w, conf_dw_w], "gather_weights")
    w_in_full = [_repack_w_in(g_win[:, l]) for l in range(depth)]
    w_out_full = [g_wout[:, l].reshape(MIX_WIDTH, D_MODEL) for l in range(depth)]
    conv_w_full = [jnp.transpose(g_cw[:, l], (1, 0, 2)).reshape(4, 1536) for l in range(depth)]
    dw_w_full = [jnp.transpose(g_dw[:, l], (1, 0, 2)).reshape(CONF_KERNEL, 512) for l in range(depth)]

    def layer_params(l):
        return [conv_w_full[l], ssd_conv_b[l].reshape(1, -1), _pad_lanes(ssd_dt_bias[l]), _pad_lanes(ssd_a_log[l]),
                _pad_lanes(ssd_d[l]), ssd_norm_w[l].reshape(1, -1), _pad_lanes(attn_sinks[l]), dw_w_full[l],
                conf_dw_b[l].reshape(1, -1), conf_ln_w[l].reshape(1, -1), conf_ln_b[l].reshape(1, -1)]

    xs = [x.reshape(t, D_MODEL)]
    saved = []
    for l in range(depth):
        proj, h = _inproj_fwd(xs[l], norm_w[l].reshape(1, -1), w_in_full[l], f"inproj_fwd_{l}")
        x_next, s_all = _mixer_fwd(xs[l], proj, w_out_full[l], layer_params(l), nb, f"mixer_fwd_{l}")
        saved.append((proj, h, s_all))
        xs.append(x_next)
    dx, loss_part, g_fnw = _loss_head(xs[depth], final_norm_w.reshape(1, -1), loss_target.reshape(t, D_MODEL),
                                      "loss_head")

    gw_in, gw_out, small_rows = [None] * depth, [None] * depth, [None] * depth
    for l in reversed(range(depth)):
        proj, h, s_all = saved[l]
        res = _mixer_bwd(dx, proj, s_all, w_out_full[l], layer_params(l), nb, f"mixer_bwd_{l}")
        dproj, gw_out[l] = res[0], res[1]
        g_cw_l, g_cb, g_dtb, g_alog, g_dsk, g_nw, g_snk, g_dww, g_dwb, g_lnw, g_lnb = res[2:]
        gw_in[l] = _inproj_bwd_w(h, dproj, f"inproj_bwd_w_{l}")
        dx, g_norm = _inproj_bwd_x(dproj, w_in_full[l], xs[l], norm_w[l].reshape(1, -1), dx, f"inproj_bwd_x_{l}")
        small_rows[l] = [g_norm, g_cb, g_dtb, g_alog, g_dsk, g_nw, g_snk, g_dwb, g_lnw, g_lnb,
                         g_cw_l.reshape(1, -1), g_dww.reshape(1, -1)]
    grad_x = dx.reshape(nb, seq, D_MODEL)

    small = jnp.concatenate([piece for l in range(depth) for piece in small_rows[l]] + [g_fnw], axis=1)
    gin_parts = jnp.stack([_unpack_gw_in(gw_in[l]) for l in range(depth)], axis=1)
    gout_parts = jnp.stack([gw_out[l].reshape(N_DEV, MIX_WIDTH // N_DEV, D_MODEL) for l in range(depth)], axis=1)
    (rin, rout), ssum = _exchange_grads([gin_parts, gout_parts], small, "exchange_grads")

    cols_in = D_IN_PROJ // N_DEV
    rows_out = MIX_WIDTH // N_DEV
    flat_in = lambda a: a.reshape(depth * D_MODEL, cols_in)
    flat_out = lambda a: a.reshape(depth * rows_out, D_MODEL)
    big_in = _reduce_adamw(rin.reshape(N_DEV, depth * D_MODEL, cols_in), flat_in(w_in), flat_in(m_w_in),
                           flat_in(v_w_in), 256, "adamw_w_in")
    big_out = _reduce_adamw(rout.reshape(N_DEV, depth * rows_out, D_MODEL), flat_out(w_out), flat_out(m_w_out),
                            flat_out(v_w_out), 256, "adamw_w_out")
    g_w_in, d_w_in, nm_w_in, nv_w_in = [a.reshape(w_in.shape) for a in big_in]
    g_w_out, d_w_out, nm_w_out, nv_w_out = [a.reshape(w_out.shape) for a in big_out]

    per_layer = sum(n for _, n in SMALL_FIELDS) + 4 * 1536 + CONF_KERNEL * 512
    g_small = {}
    for l in range(depth):
        off = l * per_layer
        for fname, n in SMALL_FIELDS:
            g_small[(fname, l)] = ssum[:, off:off + n]
            off += n
        g_small[("conv_w", l)] = lax.dynamic_slice(ssum[:, off:off + 4 * 1536].reshape(4, 1536),
                                                   (0, me_idx * 192), (4, 192)).reshape(1, -1)
        off += 4 * 1536
        g_small[("dw_w", l)] = lax.dynamic_slice(ssum[:, off:off + CONF_KERNEL * 512].reshape(CONF_KERNEL, 512),
                                                 (0, me_idx * 64), (CONF_KERNEL, 64)).reshape(1, -1)
    g_small["final"] = ssum[:, depth * per_layer:depth * per_layer + 1024]

    def small_triplet(w, m, v, width=None):
        if width is None:
            return [a.reshape(1, -1) for a in (w, m, v)]
        return [_pad_lanes(a, width) for a in (w, m, v)]

    order = []
    given = {"norm_w": (norm_w, m_norm_w, v_norm_w), "conv_b": (ssd_conv_b, m_ssd_conv_b, v_ssd_conv_b),
             "dt_bias": (ssd_dt_bias, m_ssd_dt_bias, v_ssd_dt_bias), "a_log": (ssd_a_log, m_ssd_a_log, v_ssd_a_log),
             "d_skip": (ssd_d, m_ssd_d, v_ssd_d), "ssd_norm_w": (ssd_norm_w, m_ssd_norm_w, v_ssd_norm_w),
             "sinks": (attn_sinks, m_attn_sinks, v_attn_sinks), "dw_b": (conf_dw_b, m_conf_dw_b, v_conf_dw_b),
             "ln_w": (conf_ln_w, m_conf_ln_w, v_conf_ln_w), "ln_b": (conf_ln_b, m_conf_ln_b, v_conf_ln_b),
             "conv_w": (ssd_conv_w, m_ssd_conv_w, v_ssd_conv_w), "dw_w": (conf_dw_w, m_conf_dw_w, v_conf_dw_w)}
    for l in range(depth):
        for fname, n in SMALL_FIELDS:
            w, m, v = given[fname]
            order.append(((fname, l), w.shape[1], small_triplet(w[l], m[l], v[l], n)))
        for fname in ("conv_w", "dw_w"):
            w, m, v = given[fname]
            order.append(((fname, l), w[l].size, small_triplet(w[l], m[l], v[l])))
    order.append(("final", 1024, small_triplet(final_norm_w, m_final_norm_w, v_final_norm_w)))
    g_row = jnp.concatenate([g_small[key] for key, _, _ in order], axis=1)
    pad_to = -g_row.shape[1] % 1024
    rows = [jnp.pad(jnp.concatenate([trip[i] for _, _, trip in order], axis=1), ((0, 0), (0, pad_to)))
            for i in range(3)]
    g_row = jnp.pad(g_row, ((0, 0), (0, pad_to)))
    n_small = g_row.shape[1]
    as_tiles = lambda a: a.reshape(n_small // 1024, 1024)
    sm = _reduce_adamw(as_tiles(g_row)[None], as_tiles(rows[0]), as_tiles(rows[1]), as_tiles(rows[2]),
                       n_small // 1024, "adamw_small")
    sm = [a.reshape(1, n_small) for a in sm]
    pieces = {}
    off = 0
    for key, n_true, trip in order:
        width = trip[0].shape[1]
        pieces[key] = [a[0, off:off + n_true] for a in sm]
        off += width

    def stacked(fname, like, i):
        return jnp.stack([pieces[(fname, l)][i].reshape(like.shape[1:]) for l in range(depth)], axis=0)

    names = [("norm_w", norm_w), ("conv_w", ssd_conv_w), ("conv_b", ssd_conv_b), ("dt_bias", ssd_dt_bias),
             ("a_log", ssd_a_log), ("d_skip", ssd_d), ("ssd_norm_w", ssd_norm_w), ("sinks", attn_sinks),
             ("dw_w", conf_dw_w), ("dw_b", conf_dw_b), ("ln_w", conf_ln_w), ("ln_b", conf_ln_b)]

    def outputs(i, big_in_i, big_out_i):
        vals = {fname: stacked(fname, like, i) for fname, like in names}
        return [vals["norm_w"], big_in_i, vals["conv_w"], vals["conv_b"], vals["dt_bias"], vals["a_log"],
                vals["d_skip"], vals["ssd_norm_w"], vals["sinks"], vals["dw_w"], vals["dw_b"], vals["ln_w"],
                vals["ln_b"], big_out_i, pieces["final"][i]]

    loss = lax.psum(loss_part[0, 0], MESH_AXES)
    return (loss, grad_x, *outputs(0, g_w_in, g_w_out), *outputs(1, d_w_in, d_w_out),
            *outputs(2, nm_w_in, nm_w_out), *outputs(3, nv_w_in, nv_w_out))
```

```python
import functools

import jax
import jax.numpy as jnp
from jax import lax
from jax.experimental import pallas as pl
from jax.experimental.pallas import tpu as pltpu

F32 = jnp.float32
BF16 = jnp.bfloat16
MESH_AXES = ("x", "y", "c")
N_DEV = 8
EPS = 1e-5

D_MODEL = 1024
CHUNK = 128
SSD_HEADS = 16
SSD_HEAD_DIM = 64
SSD_STATE = 128
ATTN_HEADS = 8
ATTN_HEAD_DIM = 64
CONF_KERNEL = 31
MIX_WIDTH = 2048
D_IN_PROJ = 5392
C_Z = 0
C_CA = 2048
C_XBC = 3072
C_Q = 4608
C_K = 5120
C_V = 5248
C_DT = 5376
PROJ_W = 5632
N_COL_TILES = 4
COL_TILE = PROJ_W // N_COL_TILES
XBC_HALO = 8
CONF_HALO = 32
VMEM_LIMIT = 56 * 1024 * 1024

ADAM_LR = 0.001
ADAM_B1 = 0.9
ADAM_B2 = 0.999
ADAM_EPS = 1e-08
ADAM_WD = 0.01
ADAM_STEP = 10


def _silu(v):
    return v * jax.nn.sigmoid(v)


def _softplus(v):
    return jnp.maximum(v, 0.0) + jnp.log1p(jnp.exp(-jnp.abs(v)))


def _rmsnorm(v, w):
    return v * lax.rsqrt(jnp.mean(v * v, axis=-1, keepdims=True) + EPS) * w


def _dot(a, b):
    return jnp.dot(a.astype(BF16), b.astype(BF16), preferred_element_type=F32)


def _dot_nt(a, b):
    return lax.dot_general(a.astype(BF16), b.astype(BF16), (((1,), (1,)), ((), ())), preferred_element_type=F32)


def _dot_tn(a, b):
    return lax.dot_general(a.astype(BF16), b.astype(BF16), (((0,), (0,)), ((), ())), preferred_element_type=F32)


def _taps(ext, w, offs, out_len):
    shifted = {}
    acc = None
    n_rows = ext.shape[0]
    for t, off in enumerate(offs):
        r = off % 8
        if r not in shifted:
            assert max(o for o in offs if o % 8 == r) - r + out_len <= n_rows - r
            shifted[r] = ext if r == 0 else pltpu.roll(ext, n_rows - r, axis=0)
        term = w[t:t + 1, :] * shifted[r][off - r:off - r + out_len, :]
        acc = term if acc is None else acc + term
    return acc, shifted


@functools.partial(jax.custom_vjp, nondiff_argnums=(3,))
def _dwconv(ext, w, b, halo):
    kk = w.shape[0]
    return b + _taps(ext, w, [halo - (kk - 1) + t for t in range(kk)], ext.shape[0] - halo)[0]


def _dwconv_fwd(ext, w, b, halo):
    return _dwconv(ext, w, b, halo), (ext, w)


def _dwconv_bwd(halo, res, g):
    ext, w = res
    kk = w.shape[0]
    n_out = ext.shape[0] - halo
    offs = [halo - (kk - 1) + t for t in range(kk)]
    _, shifted = _taps(ext, w, offs, n_out)
    dw = jnp.concatenate(
        [jnp.sum(g * shifted[off % 8][off - off % 8:off - off % 8 + n_out, :], axis=0, keepdims=True) for off in offs],
        axis=0)
    zeros = jnp.zeros((halo, g.shape[1]), g.dtype)
    gp = jnp.concatenate([zeros, g, zeros], axis=0)
    dext = _taps(gp, w, [halo - off for off in offs], ext.shape[0])[0]
    return dext, dw, jnp.sum(g, axis=0, keepdims=True)


_dwconv.defvjp(_dwconv_fwd, _dwconv_bwd)


def _ssd_part(z_ssd, xbc, dtr, p_xbc, s_in, cw, cb, dtb, alog, dsk, nw):
    qn = xbc.shape[0]
    nh = SSD_HEADS
    per_group = nh // 2
    xa = _silu(_dwconv(jnp.concatenate([p_xbc, xbc], axis=0), cw, cb, XBC_HALO))
    xs = xa[:, 0:1024]
    dt = _softplus(dtr + dtb)
    a = dt * (-jnp.exp(alog))
    rows = lax.broadcasted_iota(jnp.int32, (qn, qn), 0)
    cols = lax.broadcasted_iota(jnp.int32, (qn, qn), 1)
    causal = rows >= cols
    a_cs = jnp.dot(causal.astype(F32), a, precision=lax.Precision.HIGHEST, preferred_element_type=F32)
    a_cs_t = a_cs.T
    bgs = [xa[:, 1024 + g * 128:1024 + (g + 1) * 128] for g in range(2)]
    cgs = [xa[:, 1280 + g * 128:1280 + (g + 1) * 128] for g in range(2)]
    cbms = [_dot_nt(cgs[g], bgs[g]) for g in range(2)]
    colb = [jnp.broadcast_to(a_cs[:, h:h + 1], (qn, qn)) for h in range(nh)]
    lastb = [jnp.broadcast_to(colb[h][qn - 1:qn, :], (qn, qn)) for h in range(nh)]
    lmats = [jnp.exp(jnp.where(causal, colb[h] - a_cs_t[h:h + 1, :], -jnp.inf)) for h in range(nh)]
    xhs = [xs[:, h * 64:(h + 1) * 64] for h in range(nh)]
    xdts = [xhs[h] * jnp.broadcast_to(dt[:, h:h + 1], (qn, 64)) for h in range(nh)]
    y_diag = [_dot(cbms[h // per_group] * lmats[h], xdts[h]) for h in range(nh)]
    ecol = [jnp.exp(colb[h][:, 0:64]) for h in range(nh)]
    dec = [jnp.exp(lastb[h][:, 0:64] - colb[h][:, 0:64]) for h in range(nh)]
    y_off, st = [], []
    for g in range(2):
        hs = range(g * per_group, (g + 1) * per_group)
        y_off.append(_dot_nt(cgs[g], s_in[g * 512:(g + 1) * 512, :]) * jnp.concatenate([ecol[h] for h in hs], axis=1))
        st.append(_dot_tn(jnp.concatenate([xdts[h] * dec[h] for h in hs], axis=1), bgs[g]))
    e_last = jnp.exp(jnp.broadcast_to(a_cs_t[:, qn - 1:qn], (qn, SSD_STATE)))
    scale = jnp.concatenate([jnp.broadcast_to(e_last[h:h + 1, :], (64, SSD_STATE)) for h in range(nh)], axis=0)
    s_out = scale * s_in + jnp.concatenate(st, axis=0)
    d_wide = jnp.concatenate([jnp.broadcast_to(dsk[:, h:h + 1], (1, 64)) for h in range(nh)], axis=1)
    y = jnp.concatenate(y_diag, axis=1) + jnp.concatenate(y_off, axis=1) + d_wide * xs
    gated = y * _silu(z_ssd)
    halves = []
    for g in range(2):
        gg = gated[:, g * 512:(g + 1) * 512]
        halves.append(gg * lax.rsqrt(jnp.mean(gg * gg, axis=-1, keepdims=True) + EPS))
    return jnp.concatenate(halves, axis=1) * nw, s_out


def _attn_part(z_attn, q, kv, p_kv, snk, kvmask):
    qn = q.shape[0]
    per_group = ATTN_HEADS // 2
    kk = jnp.concatenate([p_kv[:, 0:128], kv[:, 0:128]], axis=0)
    vv = jnp.concatenate([p_kv[:, 128:256], kv[:, 128:256]], axis=0)
    groups = range(2)
    heads = [range(g * per_group, (g + 1) * per_group) for g in groups]
    qs = [jnp.concatenate([q[:, h * 64:(h + 1) * 64] for h in heads[g]], axis=0) for g in groups]
    sk = [jnp.concatenate([jnp.broadcast_to(snk[:, h:h + 1], (qn, 1)) for h in heads[g]], axis=0) for g in groups]
    s = [jnp.where(kvmask, _dot_nt(qs[g], kk[:, g * 64:(g + 1) * 64]) * (ATTN_HEAD_DIM ** -0.5), -jnp.inf)
         for g in groups]
    m = [lax.stop_gradient(jnp.maximum(jnp.max(s[g], axis=-1, keepdims=True), sk[g])) for g in groups]
    e = [jnp.exp(s[g] - m[g]) for g in groups]
    den = [jnp.sum(e[g], axis=-1, keepdims=True) + jnp.exp(sk[g] - m[g]) for g in groups]
    o = [_dot(e[g] / den[g], vv[:, g * 64:(g + 1) * 64]) for g in groups]
    outs = [o[g][i * qn:(i + 1) * qn, :] for g in groups for i in range(per_group)]
    return jnp.concatenate(outs, axis=1) * _silu(z_attn)


def _conf_part(z_conf, cacg, p_cc, dww, dwb, lnw, lnb):
    c0 = cacg[:, 0:512] * jax.nn.sigmoid(cacg[:, 512:1024])
    pc0 = p_cc[:, 0:512] * jax.nn.sigmoid(p_cc[:, 512:1024])
    acc = _dwconv(jnp.concatenate([pc0, c0], axis=0), dww, dwb, CONF_HALO)
    xc = acc - jnp.mean(acc, axis=-1, keepdims=True)
    yln = xc * lax.rsqrt(jnp.mean(xc * xc, axis=-1, keepdims=True) + EPS) * lnw + lnb
    return _silu(yln) * _silu(z_conf)


def _kv_mask(qn, not_first, reps):
    ii = lax.broadcasted_iota(jnp.int32, (reps * qn, 2 * qn), 0) & (qn - 1)
    jj = lax.broadcasted_iota(jnp.int32, (reps * qn, 2 * qn), 1)
    d = jj - ii
    return (d >= 1) & (d <= qn) & (not_first | (jj >= qn))


def _my_place():
    return lax.axis_index("x"), lax.axis_index("y"), lax.axis_index("c")


def _all_gather(arrs, name):
    n = len(arrs)

    def body(*refs):
        ins, outs = refs[:n], refs[n:2 * n]
        send_sems, recv_sems, local_sems = refs[2 * n:]
        x, y, c = _my_place()
        me, sibling = (x, y, c), (x, y, 1 - c)
        chips = [(1 - x, y), (x, 1 - y), (1 - x, 1 - y)]

        def slot(a, p):
            return outs[a].at[4 * p[0] + 2 * p[1] + p[2]]

        def copy(a, kk, block, to, src=None):
            return pltpu.make_async_remote_copy(
                src_ref=slot(a, block) if src is None else src, dst_ref=slot(a, block),
                send_sem=send_sems.at[a, kk], recv_sem=recv_sems.at[a, kk],
                device_id=to, device_id_type=pl.DeviceIdType.MESH)

        mine = [pltpu.make_async_copy(ins[a], slot(a, me), local_sems.at[a]) for a in range(n)]
        for cp in mine:
            cp.start()
        first = []
        for a in range(n):
            first.append(copy(a, 0, me, sibling, src=ins[a]))
            first += [copy(a, 1 + j, me, (*chip, c), src=ins[a]) for j, chip in enumerate(chips)]
        for cp in first:
            cp.start()
        passed = []
        for j, chip in enumerate(chips):
            for a in range(n):
                copy(a, 1 + j, (*chip, c), me).wait_recv()
                fwd = copy(a, 4 + j, (*chip, c), sibling)
                fwd.start()
                passed.append(fwd)
        for a in range(n):
            copy(a, 0, sibling, me).wait_recv()
            for j, chip in enumerate(chips):
                copy(a, 4 + j, (*chip, 1 - c), me).wait_recv()
        for cp in first + passed:
            cp.wait_send()
        for cp in mine:
            cp.wait()

    any_spec = pl.BlockSpec(memory_space=pl.ANY)
    return pl.pallas_call(
        body, name=name,
        out_shape=[jax.ShapeDtypeStruct((N_DEV,) + a.shape, a.dtype) for a in arrs],
        in_specs=[any_spec] * n, out_specs=[any_spec] * n,
        scratch_shapes=[pltpu.SemaphoreType.DMA((n, 7)), pltpu.SemaphoreType.DMA((n, 7)),
                        pltpu.SemaphoreType.DMA((n,))],
    )(*arrs)


def _direct_copies(scatter, ins, outs, send_sems, recv_sems, local_sems):
    x, y, c = _my_place()
    me_idx = 4 * x + 2 * y + c
    n = len(ins)
    local = [pltpu.make_async_copy(ins[a].at[me_idx] if scatter else ins[a], outs[a].at[me_idx], local_sems.at[a])
             for a in range(n)]
    remote = []
    for rel in range(1, N_DEV):
        px = 1 - x if rel & 4 else x
        py = 1 - y if rel & 2 else y
        pc = 1 - c if rel & 1 else c
        peer_idx = 4 * px + 2 * py + pc
        for a in range(n):
            remote.append(pltpu.make_async_remote_copy(
                src_ref=ins[a].at[peer_idx] if scatter else ins[a], dst_ref=outs[a].at[me_idx],
                send_sem=send_sems.at[a, rel - 1], recv_sem=recv_sems.at[a, rel - 1],
                device_id=(px, py, pc), device_id_type=pl.DeviceIdType.MESH))
    return local + remote


def _side_scratch(n):
    return [pltpu.SemaphoreType.DMA((n, 7)), pltpu.SemaphoreType.DMA((n, 7)), pltpu.SemaphoreType.DMA((n,))]


def _side_out_shapes(scatter, arrs):
    return [jax.ShapeDtypeStruct(a.shape if scatter else (N_DEV,) + a.shape, a.dtype) for a in arrs]


def _exchange_grads(big, small, name):
    n = len(big)
    ns = small.shape[1]

    def body(*refs):
        ins, small_ref = refs[:n], refs[n]
        outs, sum_ref = refs[n + 1:2 * n + 1], refs[2 * n + 1]
        small_all, send_sems, recv_sems, local_sems, small_send, small_recv, small_local = refs[2 * n + 2:]
        copies = _direct_copies(True, ins, outs, send_sems, recv_sems, local_sems)
        copies += _direct_copies(False, [small_ref], [small_all], small_send, small_recv, small_local)
        for cp in copies:
            cp.start()
        for cp in copies:
            cp.wait()
        total = small_all[0]
        for i in range(1, N_DEV):
            total = total + small_all[i]
        sum_ref[...] = total

    any_spec = pl.BlockSpec(memory_space=pl.ANY)
    vmem_spec = pl.BlockSpec(memory_space=pltpu.VMEM)
    res = pl.pallas_call(
        body, name=name,
        out_shape=_side_out_shapes(True, big) + [jax.ShapeDtypeStruct((1, ns), F32)],
        in_specs=[any_spec] * n + [vmem_spec], out_specs=[any_spec] * n + [vmem_spec],
        scratch_shapes=[pltpu.VMEM((N_DEV, 1, ns), F32)] + _side_scratch(n) + _side_scratch(1),
    )(*big, small)
    return res[:n], res[n]


def _full(shape):
    return pl.BlockSpec(shape, lambda *_: (0,) * len(shape))


def _inproj_fwd(x, nw, w, name):
    t = x.shape[0]
    tm = 256

    def body(x_ref, nw_ref, w_ref, proj_ref, h_ref):
        h = _rmsnorm(x_ref[...], nw_ref[...]).astype(BF16)
        h_ref[...] = h
        for j in range(N_COL_TILES):
            sl = slice(j * COL_TILE, (j + 1) * COL_TILE)
            proj_ref[:, sl] = jnp.dot(h, w_ref[:, sl], preferred_element_type=F32)

    return pl.pallas_call(
        body, name=name, grid=(t // tm,),
        out_shape=[jax.ShapeDtypeStruct((t, PROJ_W), F32), jax.ShapeDtypeStruct((t, D_MODEL), BF16)],
        in_specs=[pl.BlockSpec((tm, D_MODEL), lambda i: (i, 0)), _full((1, D_MODEL)), _full((D_MODEL, PROJ_W))],
        out_specs=[pl.BlockSpec((tm, PROJ_W), lambda i: (i, 0)), pl.BlockSpec((tm, D_MODEL), lambda i: (i, 0))],
        compiler_params=pltpu.CompilerParams(dimension_semantics=("arbitrary",), vmem_limit_bytes=VMEM_LIMIT),
    )(x, nw, w)


def _param_specs():
    return [_full((4, 1536)), _full((1, 1536)), _full((1, 128)), _full((1, 128)), _full((1, 128)),
            _full((1, 1024)), _full((1, 128)), _full((CONF_KERNEL, 512)), _full((1, 512)), _full((1, 512)),
            _full((1, 512))]


def _halo_specs(nc, chunk_of):
    def prev_chunk(b, j):
        return jnp.maximum(b * nc + chunk_of(j) - 1, 0)

    per_xbc = CHUNK // XBC_HALO
    per_cc = CHUNK // CONF_HALO
    return [
        pl.BlockSpec((XBC_HALO, 1536), lambda b, j: (prev_chunk(b, j) * per_xbc + per_xbc - 1, C_XBC // 1536)),
        pl.BlockSpec((CHUNK, 256), lambda b, j: (prev_chunk(b, j), C_K // 256)),
        pl.BlockSpec((CONF_HALO, 1024), lambda b, j: (prev_chunk(b, j) * per_cc + per_cc - 1, C_CA // 1024)),
    ]


def _side_job(side, ins, outs, sems, first, last):
    if side is None:
        return lambda: None
    scatter = side[0]

    @pl.when(first)
    def _():
        for cp in _direct_copies(scatter, ins, outs, *sems):
            cp.start()

    def finish():
        @pl.when(last)
        def _():
            for cp in _direct_copies(scatter, ins, outs, *sems):
                cp.wait()

    return finish


def _mixer_fwd(x, proj, w_out, params, nb, name, side=None):
    t = x.shape[0]
    nc = t // nb // CHUNK
    n_side = 0 if side is None else len(side[1])

    def body(x_ref, cur_ref, pxbc_ref, pkv_ref, pcc_ref, wo_ref, *rest):
        prm = [r[...] for r in rest[:11]]
        side_in = rest[11:11 + n_side]
        xn_ref, sall_ref = rest[11 + n_side:13 + n_side]
        side_out = rest[13 + n_side:13 + 2 * n_side]
        s_scr = rest[13 + 2 * n_side]
        c = pl.program_id(1)
        b = pl.program_id(0)
        not_first = c > 0
        nf = not_first.astype(F32)
        finish_side = _side_job(side, side_in, side_out, rest[14 + 2 * n_side:], (b == 0) & (c == 0),
                                (b == nb - 1) & (c == nc - 1))

        @pl.when(c == 0)
        def _():
            s_scr[...] = jnp.zeros_like(s_scr)

        cw, cb, dtb, alog, dsk, nw, snk, dww, dwb, lnw, lnb = prm
        s_in = s_scr[...]
        sall_ref[0] = s_in
        y_ssd, s_out = _ssd_part(cur_ref[:, 0:1024], cur_ref[:, C_XBC:C_XBC + 1536], cur_ref[:, C_DT:C_DT + 128],
                                 pxbc_ref[...] * nf, s_in, cw, cb, dtb, alog, dsk, nw)
        s_scr[...] = s_out
        y_attn = _attn_part(cur_ref[:, 1024:1536], cur_ref[:, C_Q:C_Q + 512], cur_ref[:, C_K:C_K + 256],
                            pkv_ref[...] * nf, snk, _kv_mask(CHUNK, not_first, ATTN_HEADS // 2))
        y_conf = _conf_part(cur_ref[:, 1536:2048], cur_ref[:, C_CA:C_CA + 1024], pcc_ref[...] * nf,
                            dww, dwb, lnw, lnb)
        xn_ref[...] = (x_ref[...] + _dot(y_ssd, wo_ref[0:1024, :]) + _dot(y_attn, wo_ref[1024:1536, :])
                       + _dot(y_conf, wo_ref[1536:2048, :]))
        finish_side()

    row = lambda b, j: (b * nc + j, 0)
    any_spec = pl.BlockSpec(memory_space=pl.ANY)
    side_arrs = [] if side is None else list(side[1])
    res = pl.pallas_call(
        body, name=name, grid=(nb, nc),
        out_shape=[jax.ShapeDtypeStruct((t, D_MODEL), F32),
                   jax.ShapeDtypeStruct((nb * nc, SSD_HEADS * SSD_HEAD_DIM, SSD_STATE), F32)]
                  + (_side_out_shapes(side[0], side_arrs) if side else []),
        in_specs=[pl.BlockSpec((CHUNK, D_MODEL), row), pl.BlockSpec((CHUNK, PROJ_W), row)]
                 + _halo_specs(nc, lambda j: j) + [_full((MIX_WIDTH, D_MODEL))] + _param_specs()
                 + [any_spec] * n_side,
        out_specs=[pl.BlockSpec((CHUNK, D_MODEL), row),
                   pl.BlockSpec((1, SSD_HEADS * SSD_HEAD_DIM, SSD_STATE), lambda b, j: (b * nc + j, 0, 0))]
                  + [any_spec] * n_side,
        scratch_shapes=[pltpu.VMEM((SSD_HEADS * SSD_HEAD_DIM, SSD_STATE), F32)]
                       + (_side_scratch(n_side) if side else []),
        compiler_params=pltpu.CompilerParams(dimension_semantics=("arbitrary", "arbitrary"),
                                             vmem_limit_bytes=VMEM_LIMIT),
    )(x, proj, proj, proj, proj, w_out, *params, *side_arrs)
    return res[0], res[1], res[2:]


def _mixer_bwd(dxn, proj, s_all, w_out, params, nb, name, side=None):
    t = dxn.shape[0]
    nc = t // nb // CHUNK
    n_prm = 11
    n_side = 0 if side is None else len(side[1])

    def body(dxn_ref, cur_ref, pxbc_ref, pkv_ref, pcc_ref, s_ref, wo_ref, *rest):
        prm = [r[...] for r in rest[:n_prm]]
        side_in = rest[n_prm:n_prm + n_side]
        rest = rest[n_prm + n_side:]
        dproj_ref, gwo_ref = rest[0:2]
        gprm = rest[2:n_prm + 2]
        side_out = rest[n_prm + 2:n_prm + 2 + n_side]
        ds_scr, pend_xbc, pend_kv, pend_cc = rest[n_prm + 2 + n_side:n_prm + 6 + n_side]
        b, j = pl.program_id(0), pl.program_id(1)
        c = nc - 1 - j
        not_first = c > 0
        nf = not_first.astype(F32)
        finish_side = _side_job(side, side_in, side_out, rest[n_prm + 6 + n_side:], (b == 0) & (j == 0),
                                (b == nb - 1) & (j == nc - 1))

        @pl.when((b == 0) & (j == 0))
        def _():
            gwo_ref[...] = jnp.zeros_like(gwo_ref)
            for r in gprm:
                r[...] = jnp.zeros_like(r)

        @pl.when(j == 0)
        def _():
            ds_scr[...] = jnp.zeros_like(ds_scr)
            pend_xbc[...] = jnp.zeros_like(pend_xbc)
            pend_kv[...] = jnp.zeros_like(pend_kv)
            pend_cc[...] = jnp.zeros_like(pend_cc)

        cw, cb, dtb, alog, dsk, nw, snk, dww, dwb, lnw, lnb = prm
        g_cw, g_cb, g_dtb, g_alog, g_dsk, g_nw, g_snk, g_dww, g_dwb, g_lnw, g_lnb = gprm
        dxn_v = dxn_ref[...]

        y, vjp = jax.vjp(_conf_part, cur_ref[:, 1536:2048], cur_ref[:, C_CA:C_CA + 1024], pcc_ref[...] * nf,
                         dww, dwb, lnw, lnb)
        gwo_ref[1536:2048, :] += _dot_tn(y, dxn_v)
        dz, dcacg, dpcc, d_dww, d_dwb, d_lnw, d_lnb = vjp(_dot_nt(dxn_v, wo_ref[1536:2048, :]))
        dproj_ref[:, 1536:2048] = dz
        dproj_ref[:, C_CA:C_CA + 1024] = dcacg
        dproj_ref[CHUNK - CONF_HALO:CHUNK, C_CA:C_CA + 1024] += pend_cc[...]
        pend_cc[...] = dpcc
        for r, g in ((g_dww, d_dww), (g_dwb, d_dwb), (g_lnw, d_lnw), (g_lnb, d_lnb)):
            r[...] += g

        attn = functools.partial(_attn_part, kvmask=_kv_mask(CHUNK, not_first, ATTN_HEADS // 2))
        y, vjp = jax.vjp(attn, cur_ref[:, 1024:1536], cur_ref[:, C_Q:C_Q + 512], cur_ref[:, C_K:C_K + 256],
                         pkv_ref[...] * nf, snk)
        gwo_ref[1024:1536, :] += _dot_tn(y, dxn_v)
        dz, dq, dkv, dpkv, d_snk = vjp(_dot_nt(dxn_v, wo_ref[1024:1536, :]))
        dproj_ref[:, 1024:1536] = dz
        dproj_ref[:, C_Q:C_Q + 512] = dq
        dproj_ref[:, C_K:C_K + 256] = dkv + pend_kv[...]
        pend_kv[...] = dpkv
        g_snk[...] += d_snk

        (y, _), vjp = jax.vjp(_ssd_part, cur_ref[:, 0:1024], cur_ref[:, C_XBC:C_XBC + 1536],
                              cur_ref[:, C_DT:C_DT + 128], pxbc_ref[...] * nf, s_ref[0], cw, cb, dtb, alog, dsk, nw)
        gwo_ref[0:1024, :] += _dot_tn(y, dxn_v)
        dz, dxbc, ddtr, dpxbc, ds_in, d_cw, d_cb, d_dtb, d_alog, d_dsk, d_nw = vjp(
            (_dot_nt(dxn_v, wo_ref[0:1024, :]), ds_scr[...]))
        dproj_ref[:, 0:1024] = dz
        dproj_ref[:, C_XBC:C_XBC + 1536] = dxbc
        dproj_ref[CHUNK - XBC_HALO:CHUNK, C_XBC:C_XBC + 1536] += pend_xbc[...]
        dproj_ref[:, C_DT:C_DT + 128] = ddtr
        dproj_ref[:, C_DT + 128:PROJ_W] = jnp.zeros((CHUNK, PROJ_W - C_DT - 128), F32)
        pend_xbc[...] = dpxbc
        ds_scr[...] = ds_in
        for r, g in ((g_cw, d_cw), (g_cb, d_cb), (g_dtb, d_dtb), (g_alog, d_alog), (g_dsk, d_dsk), (g_nw, d_nw)):
            r[...] += g
        finish_side()

    row = lambda b, j: (b * nc + nc - 1 - j, 0)
    prm_shapes = [(4, 1536), (1, 1536), (1, 128), (1, 128), (1, 128), (1, 1024), (1, 128), (CONF_KERNEL, 512),
                  (1, 512), (1, 512), (1, 512)]
    any_spec = pl.BlockSpec(memory_space=pl.ANY)
    side_arrs = [] if side is None else list(side[1])
    res = pl.pallas_call(
        body, name=name, grid=(nb, nc),
        out_shape=[jax.ShapeDtypeStruct((t, PROJ_W), F32), jax.ShapeDtypeStruct((MIX_WIDTH, D_MODEL), F32)]
                  + [jax.ShapeDtypeStruct(s, F32) for s in prm_shapes]
                  + (_side_out_shapes(side[0], side_arrs) if side else []),
        in_specs=[pl.BlockSpec((CHUNK, D_MODEL), row), pl.BlockSpec((CHUNK, PROJ_W), row)]
                 + _halo_specs(nc, lambda j: nc - 1 - j)
                 + [pl.BlockSpec((1, SSD_HEADS * SSD_HEAD_DIM, SSD_STATE), lambda b, j: (b * nc + nc - 1 - j, 0, 0)),
                    _full((MIX_WIDTH, D_MODEL))] + _param_specs() + [any_spec] * n_side,
        out_specs=[pl.BlockSpec((CHUNK, PROJ_W), row), _full((MIX_WIDTH, D_MODEL))] + [_full(s) for s in prm_shapes]
                  + [any_spec] * n_side,
        scratch_shapes=[pltpu.VMEM((SSD_HEADS * SSD_HEAD_DIM, SSD_STATE), F32), pltpu.VMEM((XBC_HALO, 1536), F32),
                        pltpu.VMEM((CHUNK, 256), F32), pltpu.VMEM((CONF_HALO, 1024), F32)]
                       + (_side_scratch(n_side) if side else []),
        compiler_params=pltpu.CompilerParams(dimension_semantics=("arbitrary", "arbitrary"),
                                             vmem_limit_bytes=VMEM_LIMIT),
    )(dxn, proj, proj, proj, proj, s_all, w_out, *params, *side_arrs)
    n_main = 2 + n_prm
    return res[:n_main], res[n_main:]


def _inproj_bwd_x(dproj, w, x, nw, dxn, name):
    t = x.shape[0]
    tm = 256

    def body(dp_ref, w_ref, x_ref, nw_ref, dxn_ref, dx_ref, gnw_ref):
        @pl.when(pl.program_id(0) == 0)
        def _():
            gnw_ref[...] = jnp.zeros_like(gnw_ref)

        dh = jnp.zeros((tm, D_MODEL), F32)
        for j in range(N_COL_TILES):
            sl = slice(j * COL_TILE, (j + 1) * COL_TILE)
            dh = dh + _dot_nt(dp_ref[:, sl], w_ref[:, sl])
        _, vjp = jax.vjp(_rmsnorm, x_ref[...], nw_ref[...])
        dx, dnw = vjp(dh)
        dx_ref[...] = dxn_ref[...] + dx
        gnw_ref[...] += dnw

    tok = lambda i: (i, 0)
    return pl.pallas_call(
        body, name=name, grid=(t // tm,),
        out_shape=[jax.ShapeDtypeStruct((t, D_MODEL), F32), jax.ShapeDtypeStruct((1, D_MODEL), F32)],
        in_specs=[pl.BlockSpec((tm, PROJ_W), tok), _full((D_MODEL, PROJ_W)), pl.BlockSpec((tm, D_MODEL), tok),
                  _full((1, D_MODEL)), pl.BlockSpec((tm, D_MODEL), tok)],
        out_specs=[pl.BlockSpec((tm, D_MODEL), tok), _full((1, D_MODEL))],
        compiler_params=pltpu.CompilerParams(dimension_semantics=("arbitrary",), vmem_limit_bytes=VMEM_LIMIT),
    )(dproj, w, x, nw, dxn)


def _inproj_bwd_w(h, dproj, name):
    t = h.shape[0]
    tk = 512

    def body(h_ref, dp_ref, gw_ref):
        @pl.when(pl.program_id(1) == 0)
        def _():
            gw_ref[...] = jnp.zeros_like(gw_ref)

        gw_ref[...] += _dot_tn(h_ref[...], dp_ref[...])

    return pl.pallas_call(
        body, name=name, grid=(N_COL_TILES, t // tk),
        out_shape=jax.ShapeDtypeStruct((D_MODEL, PROJ_W), F32),
        in_specs=[pl.BlockSpec((tk, D_MODEL), lambda n, k: (k, 0)), pl.BlockSpec((tk, COL_TILE), lambda n, k: (k, n))],
        out_specs=pl.BlockSpec((D_MODEL, COL_TILE), lambda n, k: (0, n)),
        compiler_params=pltpu.CompilerParams(dimension_semantics=("arbitrary", "arbitrary"),
                                             vmem_limit_bytes=VMEM_LIMIT),
    )(h, dproj)


def _loss_head(x, fnw, target, name):
    t = x.shape[0]
    tm = 512

    def body(x_ref, w_ref, t_ref, dx_ref, loss_ref, gw_ref):
        @pl.when(pl.program_id(0) == 0)
        def _():
            loss_ref[...] = jnp.zeros_like(loss_ref)
            gw_ref[...] = jnp.zeros_like(gw_ref)

        y, vjp = jax.vjp(_rmsnorm, x_ref[...], w_ref[...])
        err = y - t_ref[...]
        loss_ref[...] += 0.5 * jnp.sum(jnp.mean(err * err, axis=-1, keepdims=True), axis=0, keepdims=True)
        dx, dw = vjp(err * (1.0 / D_MODEL))
        dx_ref[...] = dx
        gw_ref[...] += dw

    tok = lambda i: (i, 0)
    return pl.pallas_call(
        body, name=name, grid=(t // tm,),
        out_shape=[jax.ShapeDtypeStruct((t, D_MODEL), F32), jax.ShapeDtypeStruct((1, 1), F32),
                   jax.ShapeDtypeStruct((1, D_MODEL), F32)],
        in_specs=[pl.BlockSpec((tm, D_MODEL), tok), _full((1, D_MODEL)), pl.BlockSpec((tm, D_MODEL), tok)],
        out_specs=[pl.BlockSpec((tm, D_MODEL), tok), _full((1, 1)), _full((1, D_MODEL))],
        compiler_params=pltpu.CompilerParams(dimension_semantics=("arbitrary",)),
    )(x, fnw, target)


def _adamw(w, g, m, v):
    m = ADAM_B1 * m + (1.0 - ADAM_B1) * g
    v = ADAM_B2 * v + (1.0 - ADAM_B2) * jnp.square(g)
    m_hat = m / (1.0 - ADAM_B1 ** ADAM_STEP)
    v_hat = v / (1.0 - ADAM_B2 ** ADAM_STEP)
    delta = -ADAM_LR * (m_hat / (jnp.sqrt(v_hat) + ADAM_EPS) + ADAM_WD * w)
    return delta, m, v


def _reduce_adamw(parts, w, m, v, layer, prev, tr, name):
    p, r, cdim = parts.shape

    def body(p_ref, w_ref, m_ref, v_ref, *rest):
        g_ref, d_ref, nm_ref, nv_ref = rest[-4:]
        g = p_ref[0].astype(F32)
        for i in range(1, p):
            g = g + p_ref[i].astype(F32)
        g_ref[0] = g
        d_ref[0], nm_ref[0], nv_ref[0] = _adamw(w_ref[0], g, m_ref[0], v_ref[0])

    blk = pl.BlockSpec((1, tr, cdim), lambda i: (layer, i, 0))
    n_prev = 0 if prev is None else 4
    any_spec = pl.BlockSpec(memory_space=pl.ANY)
    return pl.pallas_call(
        body, name=name, grid=(r // tr,),
        out_shape=[jax.ShapeDtypeStruct(w.shape, F32)] * 4,
        in_specs=[pl.BlockSpec((p, tr, cdim), lambda i: (0, i, 0)), blk, blk, blk] + [any_spec] * n_prev,
        out_specs=[blk] * 4,
        input_output_aliases={4 + i: i for i in range(n_prev)},
        compiler_params=pltpu.CompilerParams(dimension_semantics=("arbitrary",), vmem_limit_bytes=VMEM_LIMIT),
    )(parts, w, m, v, *(prev or []))


def _pad_lanes(v, width=128):
    return jnp.pad(v.reshape(1, -1), ((0, 0), (0, width - v.shape[-1])))


def _repack_w_in(g):
    w = jnp.transpose(g, (1, 0, 2)).reshape(D_MODEL, D_IN_PROJ)
    return jnp.concatenate([w[:, 0:2048], w[:, 4368:5392], w[:, 2048:3584], w[:, 3600:4368], w[:, 3584:3600],
                            jnp.zeros((D_MODEL, PROJ_W - D_IN_PROJ), w.dtype)], axis=1)


def _unpack_gw_in(g):
    w = jnp.concatenate([g[:, 0:2048], g[:, C_XBC:C_XBC + 1536], g[:, C_DT:C_DT + 16], g[:, C_Q:C_Q + 768],
                         g[:, C_CA:C_CA + 1024]], axis=1)
    return jnp.transpose(w.reshape(D_MODEL, N_DEV, D_IN_PROJ // N_DEV), (1, 0, 2))


SMALL_FIELDS = (("norm_w", 1024), ("conv_b", 1536), ("dt_bias", 128), ("a_log", 128), ("d_skip", 128),
                ("ssd_norm_w", 1024), ("sinks", 128), ("dw_b", 512), ("ln_w", 512), ("ln_b", 512))


def kernel(x, norm_w, w_in, ssd_conv_w, ssd_conv_b, ssd_dt_bias, ssd_a_log, ssd_d, ssd_norm_w, attn_sinks, conf_dw_w, conf_dw_b, conf_ln_w, conf_ln_b, w_out, final_norm_w, loss_target, m_norm_w, m_w_in, m_ssd_conv_w, m_ssd_conv_b, m_ssd_dt_bias, m_ssd_a_log, m_ssd_d, m_ssd_norm_w, m_attn_sinks, m_conf_dw_w, m_conf_dw_b, m_conf_ln_w, m_conf_ln_b, m_w_out, m_final_norm_w, v_norm_w, v_w_in, v_ssd_conv_w, v_ssd_conv_b, v_ssd_dt_bias, v_ssd_a_log, v_ssd_d, v_ssd_norm_w, v_attn_sinks, v_conf_dw_w, v_conf_dw_b, v_conf_ln_w, v_conf_ln_b, v_w_out, v_final_norm_w):
    nb, seq, _ = x.shape
    depth = norm_w.shape[0]
    t = nb * seq
    me_idx = 4 * lax.axis_index("x") + 2 * lax.axis_index("y") + lax.axis_index("c")

    w_in_bf, w_out_bf = w_in.astype(BF16), w_out.astype(BF16)
    g_win0, g_wout0, g_cw, g_dw = _all_gather([w_in_bf[0], w_out_bf[0], ssd_conv_w, conf_dw_w], "gather_weights")
    w_in_full = [_repack_w_in(g_win0)]
    w_out_full = [g_wout0.reshape(MIX_WIDTH, D_MODEL)]
    conv_w_full = [jnp.transpose(g_cw[:, l], (1, 0, 2)).reshape(4, 1536) for l in range(depth)]
    dw_w_full = [jnp.transpose(g_dw[:, l], (1, 0, 2)).reshape(CONF_KERNEL, 512) for l in range(depth)]

    def layer_params(l):
        return [conv_w_full[l], ssd_conv_b[l].reshape(1, -1), _pad_lanes(ssd_dt_bias[l]), _pad_lanes(ssd_a_log[l]),
                _pad_lanes(ssd_d[l]), ssd_norm_w[l].reshape(1, -1), _pad_lanes(attn_sinks[l]), dw_w_full[l],
                conf_dw_b[l].reshape(1, -1), conf_ln_w[l].reshape(1, -1), conf_ln_b[l].reshape(1, -1)]

    xs = [x.reshape(t, D_MODEL)]
    saved = []
    for l in range(depth):
        proj, h = _inproj_fwd(xs[l], norm_w[l].reshape(1, -1), w_in_full[l], f"inproj_fwd_{l}")
        side = (False, [w_in_bf[l + 1], w_out_bf[l + 1]]) if l + 1 < depth else None
        x_next, s_all, gathered = _mixer_fwd(xs[l], proj, w_out_full[l], layer_params(l), nb, f"mixer_fwd_{l}", side)
        if side:
            w_in_full.append(_repack_w_in(gathered[0]))
            w_out_full.append(gathered[1].reshape(MIX_WIDTH, D_MODEL))
        saved.append((proj, h, s_all))
        xs.append(x_next)
    dx, loss_part, g_fnw = _loss_head(xs[depth], final_norm_w.reshape(1, -1), loss_target.reshape(t, D_MODEL),
                                      "loss_head")

    cols_in = D_IN_PROJ // N_DEV
    rows_out = MIX_WIDTH // N_DEV
    small_rows = [None] * depth
    received = [None] * depth
    outgoing = None
    for l in reversed(range(depth)):
        proj, h, s_all = saved[l]
        res, arrived = _mixer_bwd(dx, proj, s_all, w_out_full[l], layer_params(l), nb, f"mixer_bwd_{l}",
                                  (True, outgoing) if outgoing else None)
        if outgoing:
            received[l + 1] = arrived
        dproj, gw_out = res[0], res[1]
        g_cw_l, g_cb, g_dtb, g_alog, g_dsk, g_nw, g_snk, g_dww, g_dwb, g_lnw, g_lnb = res[2:]
        gw_in = _inproj_bwd_w(h, dproj, f"inproj_bwd_w_{l}")
        dx, g_norm = _inproj_bwd_x(dproj, w_in_full[l], xs[l], norm_w[l].reshape(1, -1), dx, f"inproj_bwd_x_{l}")
        small_rows[l] = [g_norm, g_cb, g_dtb, g_alog, g_dsk, g_nw, g_snk, g_dwb, g_lnw, g_lnb,
                         g_cw_l.reshape(1, -1), g_dww.reshape(1, -1)]
        outgoing = [_unpack_gw_in(gw_in).astype(BF16), gw_out.reshape(N_DEV, rows_out, D_MODEL).astype(BF16)]
    grad_x = dx.reshape(nb, seq, D_MODEL)

    small = jnp.concatenate([piece for l in range(depth) for piece in small_rows[l]] + [g_fnw], axis=1)
    received[0], ssum = _exchange_grads(outgoing, small, "exchange_grads")

    big_in, big_out = None, None
    for l in reversed(range(depth)):
        big_in = _reduce_adamw(received[l][0], w_in, m_w_in, v_w_in, l, big_in, 256, f"adamw_w_in_{l}")
        big_out = _reduce_adamw(received[l][1], w_out, m_w_out, v_w_out, l, big_out, 256, f"adamw_w_out_{l}")
    g_w_in, d_w_in, nm_w_in, nv_w_in = big_in
    g_w_out, d_w_out, nm_w_out, nv_w_out = big_out

    per_layer = sum(n for _, n in SMALL_FIELDS) + 4 * 1536 + CONF_KERNEL * 512
    g_small = {}
    for l in range(depth):
        off = l * per_layer
        for fname, n in SMALL_FIELDS:
            g_small[(fname, l)] = ssum[:, off:off + n]
            off += n
        g_small[("conv_w", l)] = lax.dynamic_slice(ssum[:, off:off + 4 * 1536].reshape(4, 1536),
                                                   (0, me_idx * 192), (4, 192)).reshape(1, -1)
        off += 4 * 1536
        g_small[("dw_w", l)] = lax.dynamic_slice(ssum[:, off:off + CONF_KERNEL * 512].reshape(CONF_KERNEL, 512),
                                                 (0, me_idx * 64), (CONF_KERNEL, 64)).reshape(1, -1)
    g_small["final"] = ssum[:, depth * per_layer:depth * per_layer + 1024]

    def small_triplet(w, m, v, width=None):
        if width is None:
            return [a.reshape(1, -1) for a in (w, m, v)]
        return [_pad_lanes(a, width) for a in (w, m, v)]

    order = []
    given = {"norm_w": (norm_w, m_norm_w, v_norm_w), "conv_b": (ssd_conv_b, m_ssd_conv_b, v_ssd_conv_b),
             "dt_bias": (ssd_dt_bias, m_ssd_dt_bias, v_ssd_dt_bias), "a_log": (ssd_a_log, m_ssd_a_log, v_ssd_a_log),
             "d_skip": (ssd_d, m_ssd_d, v_ssd_d), "ssd_norm_w": (ssd_norm_w, m_ssd_norm_w, v_ssd_norm_w),
             "sinks": (attn_sinks, m_attn_sinks, v_attn_sinks), "dw_b": (conf_dw_b, m_conf_dw_b, v_conf_dw_b),
             "ln_w": (conf_ln_w, m_conf_ln_w, v_conf_ln_w), "ln_b": (conf_ln_b, m_conf_ln_b, v_conf_ln_b),
             "conv_w": (ssd_conv_w, m_ssd_conv_w, v_ssd_conv_w), "dw_w": (conf_dw_w, m_conf_dw_w, v_conf_dw_w)}
    for l in range(depth):
        for fname, n in SMALL_FIELDS:
            w, m, v = given[fname]
            order.append(((fname, l), w.shape[1], small_triplet(w[l], m[l], v[l], n)))
        for fname in ("conv_w", "dw_w"):
            w, m, v = given[fname]
            order.append(((fname, l), w[l].size, small_triplet(w[l], m[l], v[l])))
    order.append(("final", 1024, small_triplet(final_norm_w, m_final_norm_w, v_final_norm_w)))
    g_row = jnp.concatenate([g_small[key] for key, _, _ in order], axis=1)
    pad_to = -g_row.shape[1] % 1024
    rows = [jnp.pad(jnp.concatenate([trip[i] for _, _, trip in order], axis=1), ((0, 0), (0, pad_to)))
            for i in range(3)]
    g_row = jnp.pad(g_row, ((0, 0), (0, pad_to)))
    n_small = g_row.shape[1]
    as_tiles = lambda a: a.reshape(n_small // 1024, 1024)
    sm = _reduce_adamw(as_tiles(g_row)[None], as_tiles(rows[0])[None], as_tiles(rows[1])[None],
                       as_tiles(rows[2])[None], 0, None, n_small // 1024, "adamw_small")
    sm = [a.reshape(1, n_small) for a in sm]
    pieces = {}
    off = 0
    for key, n_true, trip in order:
        width = trip[0].shape[1]
        pieces[key] = [a[0, off:off + n_true] for a in sm]
        off += width

    def stacked(fname, like, i):
        return jnp.stack([pieces[(fname, l)][i].reshape(like.shape[1:]) for l in range(depth)], axis=0)

    names = [("norm_w", norm_w), ("conv_w", ssd_conv_w), ("conv_b", ssd_conv_b), ("dt_bias", ssd_dt_bias),
             ("a_log", ssd_a_log), ("d_skip", ssd_d), ("ssd_norm_w", ssd_norm_w), ("sinks", attn_sinks),
             ("dw_w", conf_dw_w), ("dw_b", conf_dw_b), ("ln_w", conf_ln_w), ("ln_b", conf_ln_b)]

    def outputs(i, big_in_i, big_out_i):
        vals = {fname: stacked(fname, like, i) for fname, like in names}
        return [vals["norm_w"], big_in_i, vals["conv_w"], vals["conv_b"], vals["dt_bias"], vals["a_log"],
                vals["d_skip"], vals["ssd_norm_w"], vals["sinks"], vals["dw_w"], vals["dw_b"], vals["ln_w"],
                vals["ln_b"], big_out_i, pieces["final"][i]]

    loss = lax.psum(loss_part[0, 0], MESH_AXES)
    return (loss, grad_x, *outputs(0, g_w_in, g_w_out), *outputs(1, d_w_in, d_w_out),
            *outputs(2, nm_w_in, nm_w_out), *outputs(3, nv_w_in, nv_w_out))
```

```python
import functools

import jax
import jax.numpy as jnp
from jax import lax
from jax.experimental import pallas as pl
from jax.experimental.pallas import tpu as pltpu

F32 = jnp.float32
BF16 = jnp.bfloat16
MESH_AXES = ("x", "y", "c")
N_DEV = 8
EPS = 1e-5

D_MODEL = 1024
CHUNK = 128
SSD_HEADS = 16
SSD_HEAD_DIM = 64
SSD_STATE = 128
ATTN_HEADS = 8
ATTN_HEAD_DIM = 64
CONF_KERNEL = 31
MIX_WIDTH = 2048
D_IN_PROJ = 5392
C_Z = 0
C_CA = 2048
C_XBC = 3072
C_Q = 4608
C_K = 5120
C_V = 5248
C_DT = 5376
PROJ_W = 5632
N_COL_TILES = 4
COL_TILE = PROJ_W // N_COL_TILES
XBC_HALO = 8
CONF_HALO = 32
VMEM_LIMIT = 56 * 1024 * 1024

ADAM_LR = 0.001
ADAM_B1 = 0.9
ADAM_B2 = 0.999
ADAM_EPS = 1e-08
ADAM_WD = 0.01
ADAM_STEP = 10


def _silu(v):
    return v * jax.nn.sigmoid(v)


def _softplus(v):
    return jnp.maximum(v, 0.0) + jnp.log1p(jnp.exp(-jnp.abs(v)))


def _rmsnorm(v, w):
    return v * lax.rsqrt(jnp.mean(v * v, axis=-1, keepdims=True) + EPS) * w


def _dot(a, b):
    return jnp.dot(a.astype(BF16), b.astype(BF16), preferred_element_type=F32)


def _dot_nt(a, b):
    return lax.dot_general(a.astype(BF16), b.astype(BF16), (((1,), (1,)), ((), ())), preferred_element_type=F32)


def _dot_tn(a, b):
    return lax.dot_general(a.astype(BF16), b.astype(BF16), (((0,), (0,)), ((), ())), preferred_element_type=F32)


def _taps(ext, w, offs, out_len):
    shifted = {}
    acc = None
    n_rows = ext.shape[0]
    for t, off in enumerate(offs):
        r = off % 8
        if r not in shifted:
            assert max(o for o in offs if o % 8 == r) - r + out_len <= n_rows - r
            shifted[r] = ext if r == 0 else pltpu.roll(ext, n_rows - r, axis=0)
        term = w[t:t + 1, :] * shifted[r][off - r:off - r + out_len, :]
        acc = term if acc is None else acc + term
    return acc, shifted


@functools.partial(jax.custom_vjp, nondiff_argnums=(3,))
def _dwconv(ext, w, b, halo):
    kk = w.shape[0]
    return b + _taps(ext, w, [halo - (kk - 1) + t for t in range(kk)], ext.shape[0] - halo)[0]


def _dwconv_fwd(ext, w, b, halo):
    return _dwconv(ext, w, b, halo), (ext, w)


def _dwconv_bwd(halo, res, g):
    ext, w = res
    kk = w.shape[0]
    n_out = ext.shape[0] - halo
    offs = [halo - (kk - 1) + t for t in range(kk)]
    _, shifted = _taps(ext, w, offs, n_out)
    dw = jnp.concatenate(
        [jnp.sum(g * shifted[off % 8][off - off % 8:off - off % 8 + n_out, :], axis=0, keepdims=True) for off in offs],
        axis=0)
    zeros = jnp.zeros((halo, g.shape[1]), g.dtype)
    gp = jnp.concatenate([zeros, g, zeros], axis=0)
    dext = _taps(gp, w, [halo - off for off in offs], ext.shape[0])[0]
    return dext, dw, jnp.sum(g, axis=0, keepdims=True)


_dwconv.defvjp(_dwconv_fwd, _dwconv_bwd)


def _ssd_part(z_ssd, xbc, dtr, p_xbc, s_in, cw, cb, dtb, alog, dsk, nw):
    qn = xbc.shape[0]
    nh = SSD_HEADS
    per_group = nh // 2
    xa = _silu(_dwconv(jnp.concatenate([p_xbc, xbc], axis=0), cw, cb, XBC_HALO))
    xs = xa[:, 0:1024]
    dt = _softplus(dtr + dtb)
    a = dt * (-jnp.exp(alog))
    rows = lax.broadcasted_iota(jnp.int32, (qn, qn), 0)
    cols = lax.broadcasted_iota(jnp.int32, (qn, qn), 1)
    causal = rows >= cols
    a_cs = jnp.dot(causal.astype(F32), a, precision=lax.Precision.HIGHEST, preferred_element_type=F32)
    a_cs_t = a_cs.T
    bgs = [xa[:, 1024 + g * 128:1024 + (g + 1) * 128] for g in range(2)]
    cgs = [xa[:, 1280 + g * 128:1280 + (g + 1) * 128] for g in range(2)]
    cbms = [_dot_nt(cgs[g], bgs[g]) for g in range(2)]
    colb = [jnp.broadcast_to(a_cs[:, h:h + 1], (qn, qn)) for h in range(nh)]
    lastb = [jnp.broadcast_to(colb[h][qn - 1:qn, :], (qn, qn)) for h in range(nh)]
    lmats = [jnp.exp(jnp.where(causal, colb[h] - a_cs_t[h:h + 1, :], -jnp.inf)) for h in range(nh)]
    xhs = [xs[:, h * 64:(h + 1) * 64] for h in range(nh)]
    xdts = [xhs[h] * jnp.broadcast_to(dt[:, h:h + 1], (qn, 64)) for h in range(nh)]
    y_diag = [_dot(cbms[h // per_group] * lmats[h], xdts[h]) for h in range(nh)]
    ecol = [jnp.exp(colb[h][:, 0:64]) for h in range(nh)]
    dec = [jnp.exp(lastb[h][:, 0:64] - colb[h][:, 0:64]) for h in range(nh)]
    y_off, st = [], []
    for g in range(2):
        hs = range(g * per_group, (g + 1) * per_group)
        y_off.append(_dot_nt(cgs[g], s_in[g * 512:(g + 1) * 512, :]) * jnp.concatenate([ecol[h] for h in hs], axis=1))
        st.append(_dot_tn(jnp.concatenate([xdts[h] * dec[h] for h in hs], axis=1), bgs[g]))
    e_last = jnp.exp(jnp.broadcast_to(a_cs_t[:, qn - 1:qn], (qn, SSD_STATE)))
    scale = jnp.concatenate([jnp.broadcast_to(e_last[h:h + 1, :], (64, SSD_STATE)) for h in range(nh)], axis=0)
    s_out = scale * s_in + jnp.concatenate(st, axis=0)
    d_wide = jnp.concatenate([jnp.broadcast_to(dsk[:, h:h + 1], (1, 64)) for h in range(nh)], axis=1)
    y = jnp.concatenate(y_diag, axis=1) + jnp.concatenate(y_off, axis=1) + d_wide * xs
    gated = y * _silu(z_ssd)
    halves = []
    for g in range(2):
        gg = gated[:, g * 512:(g + 1) * 512]
        halves.append(gg * lax.rsqrt(jnp.mean(gg * gg, axis=-1, keepdims=True) + EPS))
    return jnp.concatenate(halves, axis=1) * nw, s_out


def _attn_part(z_attn, q, kv, p_kv, snk, kvmask):
    qn = q.shape[0]
    per_group = ATTN_HEADS // 2
    kk = jnp.concatenate([p_kv[:, 0:128], kv[:, 0:128]], axis=0)
    vv = jnp.concatenate([p_kv[:, 128:256], kv[:, 128:256]], axis=0)
    groups = range(2)
    heads = [range(g * per_group, (g + 1) * per_group) for g in groups]
    qs = [jnp.concatenate([q[:, h * 64:(h + 1) * 64] for h in heads[g]], axis=0) for g in groups]
    sk = [jnp.concatenate([jnp.broadcast_to(snk[:, h:h + 1], (qn, 1)) for h in heads[g]], axis=0) for g in groups]
    s = [jnp.where(kvmask, _dot_nt(qs[g], kk[:, g * 64:(g + 1) * 64]) * (ATTN_HEAD_DIM ** -0.5), -jnp.inf)
         for g in groups]
    m = [lax.stop_gradient(jnp.maximum(jnp.max(s[g], axis=-1, keepdims=True), sk[g])) for g in groups]
    e = [jnp.exp(s[g] - m[g]) for g in groups]
    den = [jnp.sum(e[g], axis=-1, keepdims=True) + jnp.exp(sk[g] - m[g]) for g in groups]
    o = [_dot(e[g] / den[g], vv[:, g * 64:(g + 1) * 64]) for g in groups]
    outs = [o[g][i * qn:(i + 1) * qn, :] for g in groups for i in range(per_group)]
    return jnp.concatenate(outs, axis=1) * _silu(z_attn)


def _conf_part(z_conf, cacg, p_cc, dww, dwb, lnw, lnb):
    c0 = cacg[:, 0:512] * jax.nn.sigmoid(cacg[:, 512:1024])
    pc0 = p_cc[:, 0:512] * jax.nn.sigmoid(p_cc[:, 512:1024])
    acc = _dwconv(jnp.concatenate([pc0, c0], axis=0), dww, dwb, CONF_HALO)
    xc = acc - jnp.mean(acc, axis=-1, keepdims=True)
    yln = xc * lax.rsqrt(jnp.mean(xc * xc, axis=-1, keepdims=True) + EPS) * lnw + lnb
    return _silu(yln) * _silu(z_conf)


def _kv_mask(qn, not_first, reps):
    ii = lax.broadcasted_iota(jnp.int32, (reps * qn, 2 * qn), 0) & (qn - 1)
    jj = lax.broadcasted_iota(jnp.int32, (reps * qn, 2 * qn), 1)
    d = jj - ii
    return (d >= 1) & (d <= qn) & (not_first | (jj >= qn))


def _my_place():
    return lax.axis_index("x"), lax.axis_index("y"), lax.axis_index("c")


def _all_gather(arrs, name):
    n = len(arrs)

    def body(*refs):
        ins, outs = refs[:n], refs[n:2 * n]
        send_sems, recv_sems, local_sems = refs[2 * n:]
        x, y, c = _my_place()
        me, sibling = (x, y, c), (x, y, 1 - c)
        chips = [(1 - x, y), (x, 1 - y), (1 - x, 1 - y)]

        def slot(a, p):
            return outs[a].at[4 * p[0] + 2 * p[1] + p[2]]

        def copy(a, kk, block, to, src=None):
            return pltpu.make_async_remote_copy(
                src_ref=slot(a, block) if src is None else src, dst_ref=slot(a, block),
                send_sem=send_sems.at[a, kk], recv_sem=recv_sems.at[a, kk],
                device_id=to, device_id_type=pl.DeviceIdType.MESH)

        mine = [pltpu.make_async_copy(ins[a], slot(a, me), local_sems.at[a]) for a in range(n)]
        for cp in mine:
            cp.start()
        first = []
        for a in range(n):
            first.append(copy(a, 0, me, sibling, src=ins[a]))
            first += [copy(a, 1 + j, me, (*chip, c), src=ins[a]) for j, chip in enumerate(chips)]
        for cp in first:
            cp.start()
        passed = []
        for j, chip in enumerate(chips):
            for a in range(n):
                copy(a, 1 + j, (*chip, c), me).wait_recv()
                fwd = copy(a, 4 + j, (*chip, c), sibling)
                fwd.start()
                passed.append(fwd)
        for a in range(n):
            copy(a, 0, sibling, me).wait_recv()
            for j, chip in enumerate(chips):
                copy(a, 4 + j, (*chip, 1 - c), me).wait_recv()
        for cp in first + passed:
            cp.wait_send()
        for cp in mine:
            cp.wait()

    any_spec = pl.BlockSpec(memory_space=pl.ANY)
    return pl.pallas_call(
        body, name=name,
        out_shape=[jax.ShapeDtypeStruct((N_DEV,) + a.shape, a.dtype) for a in arrs],
        in_specs=[any_spec] * n, out_specs=[any_spec] * n,
        scratch_shapes=[pltpu.SemaphoreType.DMA((n, 7)), pltpu.SemaphoreType.DMA((n, 7)),
                        pltpu.SemaphoreType.DMA((n,))],
    )(*arrs)


def _direct_copies(scatter, ins, outs, send_sems, recv_sems, local_sems):
    x, y, c = _my_place()
    me_idx = 4 * x + 2 * y + c
    n = len(ins)
    local = [pltpu.make_async_copy(ins[a].at[me_idx] if scatter else ins[a], outs[a].at[me_idx], local_sems.at[a])
             for a in range(n)]
    remote = []
    for rel in range(1, N_DEV):
        px = 1 - x if rel & 4 else x
        py = 1 - y if rel & 2 else y
        pc = 1 - c if rel & 1 else c
        peer_idx = 4 * px + 2 * py + pc
        for a in range(n):
            remote.append(pltpu.make_async_remote_copy(
                src_ref=ins[a].at[peer_idx] if scatter else ins[a], dst_ref=outs[a].at[me_idx],
                send_sem=send_sems.at[a, rel - 1], recv_sem=recv_sems.at[a, rel - 1],
                device_id=(px, py, pc), device_id_type=pl.DeviceIdType.MESH))
    return local + remote


def _side_scratch(n):
    return [pltpu.SemaphoreType.DMA((n, 7)), pltpu.SemaphoreType.DMA((n, 7)), pltpu.SemaphoreType.DMA((n,))]


def _side_out_shapes(scatter, arrs):
    return [jax.ShapeDtypeStruct(a.shape if scatter else (N_DEV,) + a.shape, a.dtype) for a in arrs]


def _pallas_with_side(body, side, first, last, n_in, n_out, *, in_specs, out_specs, out_shape, scratch_shapes=(),
                      args, **kwargs):
    side_arrs = [] if side is None else list(side[1])
    ns = len(side_arrs)

    def wrapped(*refs):
        own_in, side_in = refs[:n_in], refs[n_in:n_in + ns]
        o = n_in + ns
        own_out, side_out = refs[o:o + n_out], refs[o + n_out:o + n_out + ns]
        scratch = refs[o + n_out + ns:]
        own_scratch, sems = (scratch[:-3], scratch[-3:]) if ns else (scratch, ())
        if ns:
            @pl.when(first())
            def _():
                for cp in _direct_copies(side[0], side_in, side_out, *sems):
                    cp.start()

        body(*own_in, *own_out, *own_scratch)
        if ns:
            @pl.when(last())
            def _():
                for cp in _direct_copies(side[0], side_in, side_out, *sems):
                    cp.wait()

    any_spec = pl.BlockSpec(memory_space=pl.ANY)
    res = pl.pallas_call(
        wrapped,
        in_specs=list(in_specs) + [any_spec] * ns,
        out_specs=list(out_specs) + [any_spec] * ns,
        out_shape=list(out_shape) + (_side_out_shapes(side[0], side_arrs) if ns else []),
        scratch_shapes=list(scratch_shapes) + (_side_scratch(ns) if ns else []),
        **kwargs,
    )(*args, *side_arrs)
    return res[:n_out], res[n_out:]


def _exchange_small(small, name):
    ns = small.shape[1]

    def body(small_ref, sum_ref, small_all, send_sems, recv_sems, local_sems):
        copies = _direct_copies(False, [small_ref], [small_all], send_sems, recv_sems, local_sems)
        for cp in copies:
            cp.start()
        for cp in copies:
            cp.wait()
        total = small_all[0]
        for i in range(1, N_DEV):
            total = total + small_all[i]
        sum_ref[...] = total

    vmem_spec = pl.BlockSpec(memory_space=pltpu.VMEM)
    return pl.pallas_call(
        body, name=name,
        out_shape=jax.ShapeDtypeStruct((1, ns), F32),
        in_specs=[vmem_spec], out_specs=vmem_spec,
        scratch_shapes=[pltpu.VMEM((N_DEV, 1, ns), F32)] + _side_scratch(1),
    )(small)


def _full(shape):
    return pl.BlockSpec(shape, lambda *_: (0,) * len(shape))


def _inproj_fwd(x, nw, w, name):
    t = x.shape[0]
    tm = 256

    def body(x_ref, nw_ref, w_ref, proj_ref, h_ref):
        h = _rmsnorm(x_ref[...], nw_ref[...]).astype(BF16)
        h_ref[...] = h
        for j in range(N_COL_TILES):
            sl = slice(j * COL_TILE, (j + 1) * COL_TILE)
            proj_ref[:, sl] = jnp.dot(h, w_ref[:, sl], preferred_element_type=F32)

    return pl.pallas_call(
        body, name=name, grid=(t // tm,),
        out_shape=[jax.ShapeDtypeStruct((t, PROJ_W), F32), jax.ShapeDtypeStruct((t, D_MODEL), BF16)],
        in_specs=[pl.BlockSpec((tm, D_MODEL), lambda i: (i, 0)), _full((1, D_MODEL)), _full((D_MODEL, PROJ_W))],
        out_specs=[pl.BlockSpec((tm, PROJ_W), lambda i: (i, 0)), pl.BlockSpec((tm, D_MODEL), lambda i: (i, 0))],
        compiler_params=pltpu.CompilerParams(dimension_semantics=("arbitrary",), vmem_limit_bytes=VMEM_LIMIT),
    )(x, nw, w)


def _param_specs():
    return [_full((4, 1536)), _full((1, 1536)), _full((1, 128)), _full((1, 128)), _full((1, 128)),
            _full((1, 1024)), _full((1, 128)), _full((CONF_KERNEL, 512)), _full((1, 512)), _full((1, 512)),
            _full((1, 512))]


def _halo_specs(nc, chunk_of):
    def prev_chunk(b, j):
        return jnp.maximum(b * nc + chunk_of(j) - 1, 0)

    per_xbc = CHUNK // XBC_HALO
    per_cc = CHUNK // CONF_HALO
    return [
        pl.BlockSpec((XBC_HALO, 1536), lambda b, j: (prev_chunk(b, j) * per_xbc + per_xbc - 1, C_XBC // 1536)),
        pl.BlockSpec((CHUNK, 256), lambda b, j: (prev_chunk(b, j), C_K // 256)),
        pl.BlockSpec((CONF_HALO, 1024), lambda b, j: (prev_chunk(b, j) * per_cc + per_cc - 1, C_CA // 1024)),
    ]


def _grid_ends(grid):
    first = lambda: functools.reduce(lambda p, q: p & q, [pl.program_id(i) == 0 for i in range(len(grid))])
    last = lambda: functools.reduce(lambda p, q: p & q, [pl.program_id(i) == n - 1 for i, n in enumerate(grid)])
    return first, last


def _mixer_fwd(x, proj, w_out, params, nb, name, side=None):
    t = x.shape[0]
    nc = t // nb // CHUNK

    def body(x_ref, cur_ref, pxbc_ref, pkv_ref, pcc_ref, wo_ref, *rest):
        prm = [r[...] for r in rest[:11]]
        xn_ref, sall_ref, s_scr = rest[11:]
        c = pl.program_id(1)
        not_first = c > 0
        nf = not_first.astype(F32)

        @pl.when(c == 0)
        def _():
            s_scr[...] = jnp.zeros_like(s_scr)

        cw, cb, dtb, alog, dsk, nw, snk, dww, dwb, lnw, lnb = prm
        s_in = s_scr[...]
        sall_ref[0] = s_in
        y_ssd, s_out = _ssd_part(cur_ref[:, 0:1024], cur_ref[:, C_XBC:C_XBC + 1536], cur_ref[:, C_DT:C_DT + 128],
                                 pxbc_ref[...] * nf, s_in, cw, cb, dtb, alog, dsk, nw)
        s_scr[...] = s_out
        y_attn = _attn_part(cur_ref[:, 1024:1536], cur_ref[:, C_Q:C_Q + 512], cur_ref[:, C_K:C_K + 256],
                            pkv_ref[...] * nf, snk, _kv_mask(CHUNK, not_first, ATTN_HEADS // 2))
        y_conf = _conf_part(cur_ref[:, 1536:2048], cur_ref[:, C_CA:C_CA + 1024], pcc_ref[...] * nf,
                            dww, dwb, lnw, lnb)
        xn_ref[...] = (x_ref[...] + _dot(y_ssd, wo_ref[0:1024, :]) + _dot(y_attn, wo_ref[1024:1536, :])
                       + _dot(y_conf, wo_ref[1536:2048, :]))

    row = lambda b, j: (b * nc + j, 0)
    grid = (nb, nc)
    return _pallas_with_side(
        body, side, *_grid_ends(grid), 17, 2, name=name, grid=grid,
        out_shape=[jax.ShapeDtypeStruct((t, D_MODEL), F32),
                   jax.ShapeDtypeStruct((nb * nc, SSD_HEADS * SSD_HEAD_DIM, SSD_STATE), F32)],
        in_specs=[pl.BlockSpec((CHUNK, D_MODEL), row), pl.BlockSpec((CHUNK, PROJ_W), row)]
                 + _halo_specs(nc, lambda j: j) + [_full((MIX_WIDTH, D_MODEL))] + _param_specs(),
        out_specs=[pl.BlockSpec((CHUNK, D_MODEL), row),
                   pl.BlockSpec((1, SSD_HEADS * SSD_HEAD_DIM, SSD_STATE), lambda b, j: (b * nc + j, 0, 0))],
        scratch_shapes=[pltpu.VMEM((SSD_HEADS * SSD_HEAD_DIM, SSD_STATE), F32)],
        compiler_params=pltpu.CompilerParams(dimension_semantics=("arbitrary", "arbitrary"),
                                             vmem_limit_bytes=VMEM_LIMIT),
        args=(x, proj, proj, proj, proj, w_out, *params))


def _mixer_bwd(dxn, proj, s_all, w_out, params, nb, name, side=None):
    t = dxn.shape[0]
    nc = t // nb // CHUNK
    n_prm = 11

    def body(dxn_ref, cur_ref, pxbc_ref, pkv_ref, pcc_ref, s_ref, wo_ref, *rest):
        prm = [r[...] for r in rest[:n_prm]]
        dproj_ref, gwo_ref = rest[n_prm:n_prm + 2]
        gprm = rest[n_prm + 2:2 * n_prm + 2]
        ds_scr, pend_xbc, pend_kv, pend_cc = rest[2 * n_prm + 2:]
        b, j = pl.program_id(0), pl.program_id(1)
        c = nc - 1 - j
        not_first = c > 0
        nf = not_first.astype(F32)

        @pl.when((b == 0) & (j == 0))
        def _():
            gwo_ref[...] = jnp.zeros_like(gwo_ref)
            for r in gprm:
                r[...] = jnp.zeros_like(r)

        @pl.when(j == 0)
        def _():
            ds_scr[...] = jnp.zeros_like(ds_scr)
            pend_xbc[...] = jnp.zeros_like(pend_xbc)
            pend_kv[...] = jnp.zeros_like(pend_kv)
            pend_cc[...] = jnp.zeros_like(pend_cc)

        cw, cb, dtb, alog, dsk, nw, snk, dww, dwb, lnw, lnb = prm
        g_cw, g_cb, g_dtb, g_alog, g_dsk, g_nw, g_snk, g_dww, g_dwb, g_lnw, g_lnb = gprm
        dxn_v = dxn_ref[...]

        y, vjp = jax.vjp(_conf_part, cur_ref[:, 1536:2048], cur_ref[:, C_CA:C_CA + 1024], pcc_ref[...] * nf,
                         dww, dwb, lnw, lnb)
        gwo_ref[1536:2048, :] += _dot_tn(y, dxn_v)
        dz, dcacg, dpcc, d_dww, d_dwb, d_lnw, d_lnb = vjp(_dot_nt(dxn_v, wo_ref[1536:2048, :]))
        dproj_ref[:, 1536:2048] = dz
        dproj_ref[:, C_CA:C_CA + 1024] = dcacg
        dproj_ref[CHUNK - CONF_HALO:CHUNK, C_CA:C_CA + 1024] += pend_cc[...]
        pend_cc[...] = dpcc
        for r, g in ((g_dww, d_dww), (g_dwb, d_dwb), (g_lnw, d_lnw), (g_lnb, d_lnb)):
            r[...] += g

        attn = functools.partial(_attn_part, kvmask=_kv_mask(CHUNK, not_first, ATTN_HEADS // 2))
        y, vjp = jax.vjp(attn, cur_ref[:, 1024:1536], cur_ref[:, C_Q:C_Q + 512], cur_ref[:, C_K:C_K + 256],
                         pkv_ref[...] * nf, snk)
        gwo_ref[1024:1536, :] += _dot_tn(y, dxn_v)
        dz, dq, dkv, dpkv, d_snk = vjp(_dot_nt(dxn_v, wo_ref[1024:1536, :]))
        dproj_ref[:, 1024:1536] = dz
        dproj_ref[:, C_Q:C_Q + 512] = dq
        dproj_ref[:, C_K:C_K + 256] = dkv + pend_kv[...]
        pend_kv[...] = dpkv
        g_snk[...] += d_snk

        (y, _), vjp = jax.vjp(_ssd_part, cur_ref[:, 0:1024], cur_ref[:, C_XBC:C_XBC + 1536],
                              cur_ref[:, C_DT:C_DT + 128], pxbc_ref[...] * nf, s_ref[0], cw, cb, dtb, alog, dsk, nw)
        gwo_ref[0:1024, :] += _dot_tn(y, dxn_v)
        dz, dxbc, ddtr, dpxbc, ds_in, d_cw, d_cb, d_dtb, d_alog, d_dsk, d_nw = vjp(
            (_dot_nt(dxn_v, wo_ref[0:1024, :]), ds_scr[...]))
        dproj_ref[:, 0:1024] = dz
        dproj_ref[:, C_XBC:C_XBC + 1536] = dxbc
        dproj_ref[CHUNK - XBC_HALO:CHUNK, C_XBC:C_XBC + 1536] += pend_xbc[...]
        dproj_ref[:, C_DT:C_DT + 128] = ddtr
        dproj_ref[:, C_DT + 128:PROJ_W] = jnp.zeros((CHUNK, PROJ_W - C_DT - 128), F32)
        pend_xbc[...] = dpxbc
        ds_scr[...] = ds_in
        for r, g in ((g_cw, d_cw), (g_cb, d_cb), (g_dtb, d_dtb), (g_alog, d_alog), (g_dsk, d_dsk), (g_nw, d_nw)):
            r[...] += g

    row = lambda b, j: (b * nc + nc - 1 - j, 0)
    prm_shapes = [(4, 1536), (1, 1536), (1, 128), (1, 128), (1, 128), (1, 1024), (1, 128), (CONF_KERNEL, 512),
                  (1, 512), (1, 512), (1, 512)]
    grid = (nb, nc)
    return _pallas_with_side(
        body, side, *_grid_ends(grid), 7 + n_prm, 2 + n_prm, name=name, grid=grid,
        out_shape=[jax.ShapeDtypeStruct((t, PROJ_W), F32), jax.ShapeDtypeStruct((MIX_WIDTH, D_MODEL), F32)]
                  + [jax.ShapeDtypeStruct(s, F32) for s in prm_shapes],
        in_specs=[pl.BlockSpec((CHUNK, D_MODEL), row), pl.BlockSpec((CHUNK, PROJ_W), row)]
                 + _halo_specs(nc, lambda j: nc - 1 - j)
                 + [pl.BlockSpec((1, SSD_HEADS * SSD_HEAD_DIM, SSD_STATE), lambda b, j: (b * nc + nc - 1 - j, 0, 0)),
                    _full((MIX_WIDTH, D_MODEL))] + _param_specs(),
        out_specs=[pl.BlockSpec((CHUNK, PROJ_W), row), _full((MIX_WIDTH, D_MODEL))] + [_full(s) for s in prm_shapes],
        scratch_shapes=[pltpu.VMEM((SSD_HEADS * SSD_HEAD_DIM, SSD_STATE), F32), pltpu.VMEM((XBC_HALO, 1536), F32),
                        pltpu.VMEM((CHUNK, 256), F32), pltpu.VMEM((CONF_HALO, 1024), F32)],
        compiler_params=pltpu.CompilerParams(dimension_semantics=("arbitrary", "arbitrary"),
                                             vmem_limit_bytes=VMEM_LIMIT),
        args=(dxn, proj, proj, proj, proj, s_all, w_out, *params))


def _inproj_bwd_x(dproj, w, x, nw, dxn, name, side=None):
    t = x.shape[0]
    tm = 256

    def body(dp_ref, w_ref, x_ref, nw_ref, dxn_ref, dx_ref, gnw_ref):
        @pl.when(pl.program_id(0) == 0)
        def _():
            gnw_ref[...] = jnp.zeros_like(gnw_ref)

        dh = jnp.zeros((tm, D_MODEL), F32)
        for j in range(N_COL_TILES):
            sl = slice(j * COL_TILE, (j + 1) * COL_TILE)
            dh = dh + _dot_nt(dp_ref[:, sl], w_ref[:, sl])
        _, vjp = jax.vjp(_rmsnorm, x_ref[...], nw_ref[...])
        dx, dnw = vjp(dh)
        dx_ref[...] = dxn_ref[...] + dx
        gnw_ref[...] += dnw

    tok = lambda i: (i, 0)
    grid = (t // tm,)
    return _pallas_with_side(
        body, side, *_grid_ends(grid), 5, 2, name=name, grid=grid,
        out_shape=[jax.ShapeDtypeStruct((t, D_MODEL), F32), jax.ShapeDtypeStruct((1, D_MODEL), F32)],
        in_specs=[pl.BlockSpec((tm, PROJ_W), tok), _full((D_MODEL, PROJ_W)), pl.BlockSpec((tm, D_MODEL), tok),
                  _full((1, D_MODEL)), pl.BlockSpec((tm, D_MODEL), tok)],
        out_specs=[pl.BlockSpec((tm, D_MODEL), tok), _full((1, D_MODEL))],
        compiler_params=pltpu.CompilerParams(dimension_semantics=("arbitrary",), vmem_limit_bytes=VMEM_LIMIT),
        args=(dproj, w, x, nw, dxn))


def _inproj_bwd_w(h, dproj, name, side=None):
    t = h.shape[0]
    tk = 512

    def body(h_ref, dp_ref, gw_ref):
        @pl.when(pl.program_id(1) == 0)
        def _():
            gw_ref[...] = jnp.zeros_like(gw_ref)

        gw_ref[...] += _dot_tn(h_ref[...], dp_ref[...])

    grid = (N_COL_TILES, t // tk)
    return _pallas_with_side(
        body, side, *_grid_ends(grid), 2, 1, name=name, grid=grid,
        out_shape=[jax.ShapeDtypeStruct((D_MODEL, PROJ_W), F32)],
        in_specs=[pl.BlockSpec((tk, D_MODEL), lambda n, k: (k, 0)), pl.BlockSpec((tk, COL_TILE), lambda n, k: (k, n))],
        out_specs=[pl.BlockSpec((D_MODEL, COL_TILE), lambda n, k: (0, n))],
        compiler_params=pltpu.CompilerParams(dimension_semantics=("arbitrary", "arbitrary"),
                                             vmem_limit_bytes=VMEM_LIMIT),
        args=(h, dproj))


def _repack_runs():
    pieces = ((0, 2048, C_Z), (2048, 3584, C_XBC), (3584, 3600, C_DT), (3600, 4368, C_Q), (4368, D_IN_PROJ, C_CA))
    per = D_IN_PROJ // N_DEV
    runs = []
    for j in range(N_DEV):
        lo, hi = per * j, per * (j + 1)
        for a, b, dst in pieces:
            s, e = max(lo, a), min(hi, b)
            if s < e:
                runs.append((j, s - lo, e - lo, dst + s - a))
    return runs


def _repack_w_in(g, name):
    tr = 256

    def body(g_ref, o_ref):
        for j, a, b, dst in _repack_runs():
            o_ref[:, dst:dst + b - a] = g_ref[j, :, a:b]
        o_ref[:, C_DT + 16:PROJ_W] = jnp.zeros((tr, PROJ_W - C_DT - 16), g.dtype)

    return pl.pallas_call(
        body, name=name, grid=(D_MODEL // tr,),
        out_shape=jax.ShapeDtypeStruct((D_MODEL, PROJ_W), g.dtype),
        in_specs=[pl.BlockSpec((N_DEV, tr, D_IN_PROJ // N_DEV), lambda i: (0, i, 0))],
        out_specs=pl.BlockSpec((tr, PROJ_W), lambda i: (i, 0)),
        compiler_params=pltpu.CompilerParams(dimension_semantics=("arbitrary",)),
    )(g)


def _unpack_gw_in(g, name):
    tr = 256

    def body(g_ref, o_ref):
        for j, a, b, dst in _repack_runs():
            o_ref[j, :, a:b] = g_ref[:, dst:dst + b - a].astype(BF16)

    return pl.pallas_call(
        body, name=name, grid=(D_MODEL // tr,),
        out_shape=jax.ShapeDtypeStruct((N_DEV, D_MODEL, D_IN_PROJ // N_DEV), BF16),
        in_specs=[pl.BlockSpec((tr, PROJ_W), lambda i: (i, 0))],
        out_specs=pl.BlockSpec((N_DEV, tr, D_IN_PROJ // N_DEV), lambda i: (0, i, 0)),
        compiler_params=pltpu.CompilerParams(dimension_semantics=("arbitrary",)),
    )(g)


def _loss_head(x, fnw, target, name):
    t = x.shape[0]
    tm = 512

    def body(x_ref, w_ref, t_ref, dx_ref, loss_ref, gw_ref):
        @pl.when(pl.program_id(0) == 0)
        def _():
            loss_ref[...] = jnp.zeros_like(loss_ref)
            gw_ref[...] = jnp.zeros_like(gw_ref)

        y, vjp = jax.vjp(_rmsnorm, x_ref[...], w_ref[...])
        err = y - t_ref[...]
        loss_ref[...] += 0.5 * jnp.sum(jnp.mean(err * err, axis=-1, keepdims=True), axis=0, keepdims=True)
        dx, dw = vjp(err * (1.0 / D_MODEL))
        dx_ref[...] = dx
        gw_ref[...] += dw

    tok = lambda i: (i, 0)
    return pl.pallas_call(
        body, name=name, grid=(t // tm,),
        out_shape=[jax.ShapeDtypeStruct((t, D_MODEL), F32), jax.ShapeDtypeStruct((1, 1), F32),
                   jax.ShapeDtypeStruct((1, D_MODEL), F32)],
        in_specs=[pl.BlockSpec((tm, D_MODEL), tok), _full((1, D_MODEL)), pl.BlockSpec((tm, D_MODEL), tok)],
        out_specs=[pl.BlockSpec((tm, D_MODEL), tok), _full((1, 1)), _full((1, D_MODEL))],
        compiler_params=pltpu.CompilerParams(dimension_semantics=("arbitrary",)),
    )(x, fnw, target)


def _adamw(w, g, m, v):
    m = ADAM_B1 * m + (1.0 - ADAM_B1) * g
    v = ADAM_B2 * v + (1.0 - ADAM_B2) * jnp.square(g)
    m_hat = m / (1.0 - ADAM_B1 ** ADAM_STEP)
    v_hat = v / (1.0 - ADAM_B2 ** ADAM_STEP)
    delta = -ADAM_LR * (m_hat / (jnp.sqrt(v_hat) + ADAM_EPS) + ADAM_WD * w)
    return delta, m, v


def _reduce_adamw(parts, w, m, v, layer, prev, tr, name):
    p, r, cdim = parts.shape

    def body(p_ref, w_ref, m_ref, v_ref, *rest):
        g_ref, d_ref, nm_ref, nv_ref = rest[-4:]
        g = p_ref[0].astype(F32)
        for i in range(1, p):
            g = g + p_ref[i].astype(F32)
        g_ref[0] = g
        d_ref[0], nm_ref[0], nv_ref[0] = _adamw(w_ref[0], g, m_ref[0], v_ref[0])

    blk = pl.BlockSpec((1, tr, cdim), lambda i: (layer, i, 0))
    n_prev = 0 if prev is None else 4
    any_spec = pl.BlockSpec(memory_space=pl.ANY)
    return pl.pallas_call(
        body, name=name, grid=(r // tr,),
        out_shape=[jax.ShapeDtypeStruct(w.shape, F32)] * 4,
        in_specs=[pl.BlockSpec((p, tr, cdim), lambda i: (0, i, 0)), blk, blk, blk] + [any_spec] * n_prev,
        out_specs=[blk] * 4,
        input_output_aliases={4 + i: i for i in range(n_prev)},
        compiler_params=pltpu.CompilerParams(dimension_semantics=("arbitrary",), vmem_limit_bytes=VMEM_LIMIT),
    )(parts, w, m, v, *(prev or []))


def _pad_lanes(v, width=128):
    return jnp.pad(v.reshape(1, -1), ((0, 0), (0, width - v.shape[-1])))


SMALL_FIELDS = (("norm_w", 1024), ("conv_b", 1536), ("dt_bias", 128), ("a_log", 128), ("d_skip", 128),
                ("ssd_norm_w", 1024), ("sinks", 128), ("dw_b", 512), ("ln_w", 512), ("ln_b", 512))


def kernel(x, norm_w, w_in, ssd_conv_w, ssd_conv_b, ssd_dt_bias, ssd_a_log, ssd_d, ssd_norm_w, attn_sinks, conf_dw_w, conf_dw_b, conf_ln_w, conf_ln_b, w_out, final_norm_w, loss_target, m_norm_w, m_w_in, m_ssd_conv_w, m_ssd_conv_b, m_ssd_dt_bias, m_ssd_a_log, m_ssd_d, m_ssd_norm_w, m_attn_sinks, m_conf_dw_w, m_conf_dw_b, m_conf_ln_w, m_conf_ln_b, m_w_out, m_final_norm_w, v_norm_w, v_w_in, v_ssd_conv_w, v_ssd_conv_b, v_ssd_dt_bias, v_ssd_a_log, v_ssd_d, v_ssd_norm_w, v_attn_sinks, v_conf_dw_w, v_conf_dw_b, v_conf_ln_w, v_conf_ln_b, v_w_out, v_final_norm_w):
    nb, seq, _ = x.shape
    depth = norm_w.shape[0]
    t = nb * seq
    me_idx = 4 * lax.axis_index("x") + 2 * lax.axis_index("y") + lax.axis_index("c")

    w_in_bf, w_out_bf = w_in.astype(BF16), w_out.astype(BF16)
    g_win0, g_wout0, g_cw, g_dw = _all_gather([w_in_bf[0], w_out_bf[0], ssd_conv_w, conf_dw_w], "gather_weights")
    w_in_full = [_repack_w_in(g_win0, "repack_w_in_0")]
    w_out_full = [g_wout0.reshape(MIX_WIDTH, D_MODEL)]
    conv_w_full = [jnp.transpose(g_cw[:, l], (1, 0, 2)).reshape(4, 1536) for l in range(depth)]
    dw_w_full = [jnp.transpose(g_dw[:, l], (1, 0, 2)).reshape(CONF_KERNEL, 512) for l in range(depth)]

    def layer_params(l):
        return [conv_w_full[l], ssd_conv_b[l].reshape(1, -1), _pad_lanes(ssd_dt_bias[l]), _pad_lanes(ssd_a_log[l]),
                _pad_lanes(ssd_d[l]), ssd_norm_w[l].reshape(1, -1), _pad_lanes(attn_sinks[l]), dw_w_full[l],
                conf_dw_b[l].reshape(1, -1), conf_ln_w[l].reshape(1, -1), conf_ln_b[l].reshape(1, -1)]

    xs = [x.reshape(t, D_MODEL)]
    saved = []
    for l in range(depth):
        proj, h = _inproj_fwd(xs[l], norm_w[l].reshape(1, -1), w_in_full[l], f"inproj_fwd_{l}")
        side = (False, [w_in_bf[l + 1], w_out_bf[l + 1]]) if l + 1 < depth else None
        (x_next, s_all), gathered = _mixer_fwd(xs[l], proj, w_out_full[l], layer_params(l), nb, f"mixer_fwd_{l}",
                                               side)
        if side:
            w_in_full.append(_repack_w_in(gathered[0], f"repack_w_in_{l + 1}"))
            w_out_full.append(gathered[1].reshape(MIX_WIDTH, D_MODEL))
        saved.append((proj, h, s_all))
        xs.append(x_next)
    dx, loss_part, g_fnw = _loss_head(xs[depth], final_norm_w.reshape(1, -1), loss_target.reshape(t, D_MODEL),
                                      "loss_head")

    cols_in = D_IN_PROJ // N_DEV
    rows_out = MIX_WIDTH // N_DEV
    small_rows = [None] * depth
    received = [None] * depth
    outgoing = None
    for l in reversed(range(depth)):
        proj, h, s_all = saved[l]
        res, arrived = _mixer_bwd(dx, proj, s_all, w_out_full[l], layer_params(l), nb, f"mixer_bwd_{l}",
                                  (True, outgoing) if outgoing else None)
        if outgoing:
            received[l + 1] = arrived
        dproj, gw_out = res[0], res[1]
        g_cw_l, g_cb, g_dtb, g_alog, g_dsk, g_nw, g_snk, g_dww, g_dwb, g_lnw, g_lnb = res[2:]
        gw_out_parts = gw_out.reshape(N_DEV, rows_out, D_MODEL).astype(BF16)
        nw_l = norm_w[l].reshape(1, -1)
        if l > 0:
            (gw_in,), _ = _inproj_bwd_w(h, dproj, f"inproj_bwd_w_{l}")
            (dx, g_norm), _ = _inproj_bwd_x(dproj, w_in_full[l], xs[l], nw_l, dx, f"inproj_bwd_x_{l}")
            outgoing = [_unpack_gw_in(gw_in, f"unpack_gw_in_{l}"), gw_out_parts]
        else:
            (gw_in,), got_out = _inproj_bwd_w(h, dproj, f"inproj_bwd_w_{l}", (True, [gw_out_parts]))
            gw_in_parts = _unpack_gw_in(gw_in, f"unpack_gw_in_{l}")
            (dx, g_norm), got_in = _inproj_bwd_x(dproj, w_in_full[l], xs[l], nw_l, dx, f"inproj_bwd_x_{l}",
                                                 (True, [gw_in_parts]))
            received[l] = [got_in[0], got_out[0]]
        small_rows[l] = [g_norm, g_cb, g_dtb, g_alog, g_dsk, g_nw, g_snk, g_dwb, g_lnw, g_lnb,
                         g_cw_l.reshape(1, -1), g_dww.reshape(1, -1)]
    grad_x = dx.reshape(nb, seq, D_MODEL)

    small = jnp.concatenate([piece for l in range(depth) for piece in small_rows[l]] + [g_fnw], axis=1)
    ssum = _exchange_small(small, "exchange_small")

    big_in, big_out = None, None
    for l in reversed(range(depth)):
        big_in = _reduce_adamw(received[l][0], w_in, m_w_in, v_w_in, l, big_in, 256, f"adamw_w_in_{l}")
        big_out = _reduce_adamw(received[l][1], w_out, m_w_out, v_w_out, l, big_out, 256, f"adamw_w_out_{l}")
    g_w_in, d_w_in, nm_w_in, nv_w_in = big_in
    g_w_out, d_w_out, nm_w_out, nv_w_out = big_out

    per_layer = sum(n for _, n in SMALL_FIELDS) + 4 * 1536 + CONF_KERNEL * 512
    g_small = {}
    for l in range(depth):
        off = l * per_layer
        for fname, n in SMALL_FIELDS:
            g_small[(fname, l)] = ssum[:, off:off + n]
            off += n
        g_small[("conv_w", l)] = lax.dynamic_slice(ssum[:, off:off + 4 * 1536].reshape(4, 1536),
                                                   (0, me_idx * 192), (4, 192)).reshape(1, -1)
        off += 4 * 1536
        g_small[("dw_w", l)] = lax.dynamic_slice(ssum[:, off:off + CONF_KERNEL * 512].reshape(CONF_KERNEL, 512),
                                                 (0, me_idx * 64), (CONF_KERNEL, 64)).reshape(1, -1)
    g_small["final"] = ssum[:, depth * per_layer:depth * per_layer + 1024]

    def small_triplet(w, m, v, width=None):
        if width is None:
            return [a.reshape(1, -1) for a in (w, m, v)]
        return [_pad_lanes(a, width) for a in (w, m, v)]

    order = []
    given = {"norm_w": (norm_w, m_norm_w, v_norm_w), "conv_b": (ssd_conv_b, m_ssd_conv_b, v_ssd_conv_b),
             "dt_bias": (ssd_dt_bias, m_ssd_dt_bias, v_ssd_dt_bias), "a_log": (ssd_a_log, m_ssd_a_log, v_ssd_a_log),
             "d_skip": (ssd_d, m_ssd_d, v_ssd_d), "ssd_norm_w": (ssd_norm_w, m_ssd_norm_w, v_ssd_norm_w),
             "sinks": (attn_sinks, m_attn_sinks, v_attn_sinks), "dw_b": (conf_dw_b, m_conf_dw_b, v_conf_dw_b),
             "ln_w": (conf_ln_w, m_conf_ln_w, v_conf_ln_w), "ln_b": (conf_ln_b, m_conf_ln_b, v_conf_ln_b),
             "conv_w": (ssd_conv_w, m_ssd_conv_w, v_ssd_conv_w), "dw_w": (conf_dw_w, m_conf_dw_w, v_conf_dw_w)}
    for l in range(depth):
        for fname, n in SMALL_FIELDS:
            w, m, v = given[fname]
            order.append(((fname, l), w.shape[1], small_triplet(w[l], m[l], v[l], n)))
        for fname in ("conv_w", "dw_w"):
            w, m, v = given[fname]
            order.append(((fname, l), w[l].size, small_triplet(w[l], m[l], v[l])))
    order.append(("final", 1024, small_triplet(final_norm_w, m_final_norm_w, v_final_norm_w)))
    g_row = jnp.concatenate([g_small[key] for key, _, _ in order], axis=1)
    pad_to = -g_row.shape[1] % 1024
    rows = [jnp.pad(jnp.concatenate([trip[i] for _, _, trip in order], axis=1), ((0, 0), (0, pad_to)))
            for i in range(3)]
    g_row = jnp.pad(g_row, ((0, 0), (0, pad_to)))
    n_small = g_row.shape[1]
    as_tiles = lambda a: a.reshape(n_small // 1024, 1024)
    sm = _reduce_adamw(as_tiles(g_row)[None], as_tiles(rows[0])[None], as_tiles(rows[1])[None],
                       as_tiles(rows[2])[None], 0, None, n_small // 1024, "adamw_small")
    sm = [a.reshape(1, n_small) for a in sm]
    pieces = {}
    off = 0
    for key, n_true, trip in order:
        width = trip[0].shape[1]
        pieces[key] = [a[0, off:off + n_true] for a in sm]
        off += width

    def stacked(fname, like, i):
        return jnp.stack([pieces[(fname, l)][i].reshape(like.shape[1:]) for l in range(depth)], axis=0)

    names = [("norm_w", norm_w), ("conv_w", ssd_conv_w), ("conv_b", ssd_conv_b), ("dt_bias", ssd_dt_bias),
             ("a_log", ssd_a_log), ("d_skip", ssd_d), ("ssd_norm_w", ssd_norm_w), ("sinks", attn_sinks),
             ("dw_w", conf_dw_w), ("dw_b", conf_dw_b), ("ln_w", conf_ln_w), ("ln_b", conf_ln_b)]

    def outputs(i, big_in_i, big_out_i):
        vals = {fname: stacked(fname, like, i) for fname, like in names}
        return [vals["norm_w"], big_in_i, vals["conv_w"], vals["conv_b"], vals["dt_bias"], vals["a_log"],
                vals["d_skip"], vals["ssd_norm_w"], vals["sinks"], vals["dw_w"], vals["dw_b"], vals["ln_w"],
                vals["ln_b"], big_out_i, pieces["final"][i]]

    loss = lax.psum(loss_part[0, 0], MESH_AXES)
    return (loss, grad_x, *outputs(0, g_w_in, g_w_out), *outputs(1, d_w_in, d_w_out),
            *outputs(2, nm_w_in, nm_w_out), *outputs(3, nv_w_in, nv_w_out))
```

```python
import functools

import jax
import jax.numpy as jnp
from jax import lax
from jax.experimental import pallas as pl
from jax.experimental.pallas import tpu as pltpu

F32 = jnp.float32
BF16 = jnp.bfloat16
MESH_AXES = ("x", "y", "c")
N_DEV = 8
EPS = 1e-5

D_MODEL = 1024
CHUNK = 128
SSD_HEADS = 16
SSD_HEAD_DIM = 64
SSD_STATE = 128
ATTN_HEADS = 8
ATTN_HEAD_DIM = 64
CONF_KERNEL = 31
MIX_WIDTH = 2048
D_IN_PROJ = 5392
C_Z = 0
C_CA = 2048
C_XBC = 3072
C_Q = 4608
C_K = 5120
C_V = 5248
C_DT = 5376
PROJ_W = 5632
N_COL_TILES = 4
COL_TILE = PROJ_W // N_COL_TILES
XBC_HALO = 8
CONF_HALO = 32
VMEM_LIMIT = 56 * 1024 * 1024

ADAM_LR = 0.001
ADAM_B1 = 0.9
ADAM_B2 = 0.999
ADAM_EPS = 1e-08
ADAM_WD = 0.01
ADAM_STEP = 10


def _silu(v):
    return v * jax.nn.sigmoid(v)


def _softplus(v):
    return jnp.maximum(v, 0.0) + jnp.log1p(jnp.exp(-jnp.abs(v)))


def _rmsnorm(v, w):
    return v * lax.rsqrt(jnp.mean(v * v, axis=-1, keepdims=True) + EPS) * w


def _dot(a, b):
    return jnp.dot(a.astype(BF16), b.astype(BF16), preferred_element_type=F32)


def _dot_nt(a, b):
    return lax.dot_general(a.astype(BF16), b.astype(BF16), (((1,), (1,)), ((), ())), preferred_element_type=F32)


def _dot_tn(a, b):
    return lax.dot_general(a.astype(BF16), b.astype(BF16), (((0,), (0,)), ((), ())), preferred_element_type=F32)


def _taps(ext, offs, out_len, w=None, g=None):
    n_rows, n_cols = ext.shape
    by_shift = {}
    for t, off in enumerate(offs):
        by_shift.setdefault(off % 8, []).append((t, off))
    for r, taps in by_shift.items():
        assert max(off for _, off in taps) - r + out_len <= n_rows - r
    accs = []
    sums = [[None] * (n_cols // 128) for _ in offs]
    for blk in range(n_cols // 128):
        cs = slice(blk * 128, (blk + 1) * 128)
        e = ext[:, cs]
        acc = None
        for r, taps in by_shift.items():
            shifted = e if r == 0 else pltpu.roll(e, n_rows - r, axis=0)
            for t, off in taps:
                window = shifted[off - r:off - r + out_len, :]
                if w is not None:
                    term = w[t:t + 1, cs] * window
                    acc = term if acc is None else acc + term
                if g is not None:
                    sums[t][blk] = jnp.sum(g[:, cs] * window, axis=0, keepdims=True)
        accs.append(acc)
    if w is not None:
        return jnp.concatenate(accs, axis=1)
    return jnp.concatenate([jnp.concatenate(row, axis=1) for row in sums], axis=0)


@functools.partial(jax.custom_vjp, nondiff_argnums=(3,))
def _dwconv(ext, w, b, halo):
    kk = w.shape[0]
    return b + _taps(ext, [halo - (kk - 1) + t for t in range(kk)], ext.shape[0] - halo, w=w)


def _dwconv_fwd(ext, w, b, halo):
    return _dwconv(ext, w, b, halo), (ext, w)


def _dwconv_bwd(halo, res, g):
    ext, w = res
    kk = w.shape[0]
    offs = [halo - (kk - 1) + t for t in range(kk)]
    dw = _taps(ext, offs, ext.shape[0] - halo, g=g)
    zeros = jnp.zeros((halo, g.shape[1]), g.dtype)
    gp = jnp.concatenate([zeros, g, zeros], axis=0)
    dext = _taps(gp, [halo - off for off in offs], ext.shape[0], w=w)
    return dext, dw, jnp.sum(g, axis=0, keepdims=True)


_dwconv.defvjp(_dwconv_fwd, _dwconv_bwd)


def _ssd_part(z_ssd, xbc, dtr, p_xbc, s_in, cw, cb, dtb, alog, dsk, nw):
    qn = xbc.shape[0]
    nh = SSD_HEADS
    per_group = nh // 2
    n_pair = nh // 2
    xa = _silu(_dwconv(jnp.concatenate([p_xbc, xbc], axis=0), cw, cb, XBC_HALO))
    xs = xa[:, 0:1024]
    dt = _softplus(dtr + dtb)
    a = dt * (-jnp.exp(alog))
    rows = lax.broadcasted_iota(jnp.int32, (qn, qn), 0)
    cols = lax.broadcasted_iota(jnp.int32, (qn, qn), 1)
    causal = rows >= cols
    low = cols < SSD_HEAD_DIM
    a_cs = jnp.dot(causal.astype(F32), a, precision=lax.Precision.HIGHEST, preferred_element_type=F32)
    a_cs_t = a_cs.T
    bgs = [xa[:, 1024 + g * 128:1024 + (g + 1) * 128] for g in range(2)]
    cgs = [xa[:, 1280 + g * 128:1280 + (g + 1) * 128] for g in range(2)]
    cbms = [_dot_nt(cgs[g], bgs[g]) for g in range(2)]
    colb = [jnp.broadcast_to(a_cs[:, h:h + 1], (qn, qn)) for h in range(nh)]
    lastb = [jnp.broadcast_to(colb[h][qn - 1:qn, :], (qn, qn)) for h in range(nh)]
    dtb_wide = [jnp.broadcast_to(dt[:, h:h + 1], (qn, qn)) for h in range(nh)]
    lmats = [jnp.exp(jnp.where(causal, colb[h] - a_cs_t[h:h + 1, :], -jnp.inf)) for h in range(nh)]
    ms = [cbms[h // per_group] * lmats[h] for h in range(nh)]
    x_pair = [xs[:, p * 128:(p + 1) * 128] for p in range(n_pair)]
    x_lo = [x_pair[p] * jnp.where(low, dtb_wide[2 * p], 0.0) for p in range(n_pair)]
    x_hi = [x_pair[p] * jnp.where(low, 0.0, dtb_wide[2 * p + 1]) for p in range(n_pair)]
    y_diag = [_dot(ms[2 * p], x_lo[p]) + _dot(ms[2 * p + 1], x_hi[p]) for p in range(n_pair)]
    col_pair = [jnp.where(low, colb[2 * p], colb[2 * p + 1]) for p in range(n_pair)]
    last_pair = [jnp.where(low, lastb[2 * p], lastb[2 * p + 1]) for p in range(n_pair)]
    ecol = [jnp.exp(col_pair[p]) for p in range(n_pair)]
    xw = [(x_lo[p] + x_hi[p]) * jnp.exp(last_pair[p] - col_pair[p]) for p in range(n_pair)]
    y_off, st = [], []
    for g in range(2):
        ps = range(g * n_pair // 2, (g + 1) * n_pair // 2)
        y_off.append(_dot_nt(cgs[g], s_in[g * 512:(g + 1) * 512, :]) * jnp.concatenate([ecol[p] for p in ps], axis=1))
        st.append(_dot_tn(jnp.concatenate([xw[p] for p in ps], axis=1), bgs[g]))
    e_last = jnp.exp(jnp.broadcast_to(a_cs_t[:, qn - 1:qn], (qn, SSD_STATE)))
    scale = jnp.concatenate([jnp.broadcast_to(e_last[h:h + 1, :], (64, SSD_STATE)) for h in range(nh)], axis=0)
    s_out = scale * s_in + jnp.concatenate(st, axis=0)
    d_wide = jnp.concatenate([jnp.broadcast_to(dsk[:, h:h + 1], (1, 64)) for h in range(nh)], axis=1)
    y = jnp.concatenate(y_diag, axis=1) + jnp.concatenate(y_off, axis=1) + d_wide * xs
    gated = y * _silu(z_ssd)
    halves = []
    for g in range(2):
        gg = gated[:, g * 512:(g + 1) * 512]
        halves.append(gg * lax.rsqrt(jnp.mean(gg * gg, axis=-1, keepdims=True) + EPS))
    return jnp.concatenate(halves, axis=1) * nw, s_out


def _attn_part(z_attn, q, kv, p_kv, snk, kvmask):
    qn = q.shape[0]
    per_group = ATTN_HEADS // 2
    kk = jnp.concatenate([p_kv[:, 0:128], kv[:, 0:128]], axis=0)
    vv = jnp.concatenate([p_kv[:, 128:256], kv[:, 128:256]], axis=0)
    groups = range(2)
    heads = [range(g * per_group, (g + 1) * per_group) for g in groups]
    qs = [jnp.concatenate([q[:, h * 64:(h + 1) * 64] for h in heads[g]], axis=0) for g in groups]
    sk = [jnp.concatenate([jnp.broadcast_to(snk[:, h:h + 1], (qn, 1)) for h in heads[g]], axis=0) for g in groups]
    s = [jnp.where(kvmask, _dot_nt(qs[g], kk[:, g * 64:(g + 1) * 64]) * (ATTN_HEAD_DIM ** -0.5), -jnp.inf)
         for g in groups]
    m = [lax.stop_gradient(jnp.maximum(jnp.max(s[g], axis=-1, keepdims=True), sk[g])) for g in groups]
    e = [jnp.exp(s[g] - m[g]) for g in groups]
    den = [jnp.sum(e[g], axis=-1, keepdims=True) + jnp.exp(sk[g] - m[g]) for g in groups]
    o = [_dot(e[g] / den[g], vv[:, g * 64:(g + 1) * 64]) for g in groups]
    outs = [o[g][i * qn:(i + 1) * qn, :] for g in groups for i in range(per_group)]
    return jnp.concatenate(outs, axis=1) * _silu(z_attn)


def _conf_part(z_conf, cacg, p_cc, dww, dwb, lnw, lnb):
    c0 = cacg[:, 0:512] * jax.nn.sigmoid(cacg[:, 512:1024])
    pc0 = p_cc[:, 0:512] * jax.nn.sigmoid(p_cc[:, 512:1024])
    acc = _dwconv(jnp.concatenate([pc0, c0], axis=0), dww, dwb, CONF_HALO)
    xc = acc - jnp.mean(acc, axis=-1, keepdims=True)
    yln = xc * lax.rsqrt(jnp.mean(xc * xc, axis=-1, keepdims=True) + EPS) * lnw + lnb
    return _silu(yln) * _silu(z_conf)


def _kv_mask(qn, not_first, reps):
    ii = lax.broadcasted_iota(jnp.int32, (reps * qn, 2 * qn), 0) & (qn - 1)
    jj = lax.broadcasted_iota(jnp.int32, (reps * qn, 2 * qn), 1)
    d = jj - ii
    return (d >= 1) & (d <= qn) & (not_first | (jj >= qn))


def _my_place():
    return lax.axis_index("x"), lax.axis_index("y"), lax.axis_index("c")


def _all_gather(arrs, name):
    n = len(arrs)

    def body(*refs):
        ins, outs = refs[:n], refs[n:2 * n]
        send_sems, recv_sems, local_sems = refs[2 * n:]
        x, y, c = _my_place()
        me, sibling = (x, y, c), (x, y, 1 - c)
        chips = [(1 - x, y), (x, 1 - y), (1 - x, 1 - y)]

        def slot(a, p):
            return outs[a].at[4 * p[0] + 2 * p[1] + p[2]]

        def copy(a, kk, block, to, src=None):
            return pltpu.make_async_remote_copy(
                src_ref=slot(a, block) if src is None else src, dst_ref=slot(a, block),
                send_sem=send_sems.at[a, kk], recv_sem=recv_sems.at[a, kk],
                device_id=to, device_id_type=pl.DeviceIdType.MESH)

        mine = [pltpu.make_async_copy(ins[a], slot(a, me), local_sems.at[a]) for a in range(n)]
        for cp in mine:
            cp.start()
        first = []
        for a in range(n):
            first.append(copy(a, 0, me, sibling, src=ins[a]))
            first += [copy(a, 1 + j, me, (*chip, c), src=ins[a]) for j, chip in enumerate(chips)]
        for cp in first:
            cp.start()
        passed = []
        for j, chip in enumerate(chips):
            for a in range(n):
                copy(a, 1 + j, (*chip, c), me).wait_recv()
                fwd = copy(a, 4 + j, (*chip, c), sibling)
                fwd.start()
                passed.append(fwd)
        for a in range(n):
            copy(a, 0, sibling, me).wait_recv()
            for j, chip in enumerate(chips):
                copy(a, 4 + j, (*chip, 1 - c), me).wait_recv()
        for cp in first + passed:
            cp.wait_send()
        for cp in mine:
            cp.wait()

    any_spec = pl.BlockSpec(memory_space=pl.ANY)
    return pl.pallas_call(
        body, name=name,
        out_shape=[jax.ShapeDtypeStruct((N_DEV,) + a.shape, a.dtype) for a in arrs],
        in_specs=[any_spec] * n, out_specs=[any_spec] * n,
        scratch_shapes=[pltpu.SemaphoreType.DMA((n, 7)), pltpu.SemaphoreType.DMA((n, 7)),
                        pltpu.SemaphoreType.DMA((n,))],
    )(*arrs)


def _direct_copies(scatter, ins, outs, send_sems, recv_sems, local_sems):
    x, y, c = _my_place()
    me_idx = 4 * x + 2 * y + c
    n = len(ins)
    local = [pltpu.make_async_copy(ins[a].at[me_idx] if scatter else ins[a], outs[a].at[me_idx], local_sems.at[a])
             for a in range(n)]
    remote = []
    for rel in range(1, N_DEV):
        px = 1 - x if rel & 4 else x
        py = 1 - y if rel & 2 else y
        pc = 1 - c if rel & 1 else c
        peer_idx = 4 * px + 2 * py + pc
        for a in range(n):
            remote.append(pltpu.make_async_remote_copy(
                src_ref=ins[a].at[peer_idx] if scatter else ins[a], dst_ref=outs[a].at[me_idx],
                send_sem=send_sems.at[a, rel - 1], recv_sem=recv_sems.at[a, rel - 1],
                device_id=(px, py, pc), device_id_type=pl.DeviceIdType.MESH))
    return local + remote


def _side_scratch(n):
    return [pltpu.SemaphoreType.DMA((n, 7)), pltpu.SemaphoreType.DMA((n, 7)), pltpu.SemaphoreType.DMA((n,))]


def _side_out_shapes(scatter, arrs):
    return [jax.ShapeDtypeStruct(a.shape if scatter else (N_DEV,) + a.shape, a.dtype) for a in arrs]


def _pallas_with_side(body, side, first, last, n_in, n_out, *, in_specs, out_specs, out_shape, scratch_shapes=(),
                      args, **kwargs):
    side_arrs = [] if side is None else list(side[1])
    ns = len(side_arrs)

    def wrapped(*refs):
        own_in, side_in = refs[:n_in], refs[n_in:n_in + ns]
        o = n_in + ns
        own_out, side_out = refs[o:o + n_out], refs[o + n_out:o + n_out + ns]
        scratch = refs[o + n_out + ns:]
        own_scratch, sems = (scratch[:-3], scratch[-3:]) if ns else (scratch, ())
        if ns:
            @pl.when(first())
            def _():
                for cp in _direct_copies(side[0], side_in, side_out, *sems):
                    cp.start()

        body(*own_in, *own_out, *own_scratch)
        if ns:
            @pl.when(last())
            def _():
                for cp in _direct_copies(side[0], side_in, side_out, *sems):
                    cp.wait()

    any_spec = pl.BlockSpec(memory_space=pl.ANY)
    res = pl.pallas_call(
        wrapped,
        in_specs=list(in_specs) + [any_spec] * ns,
        out_specs=list(out_specs) + [any_spec] * ns,
        out_shape=list(out_shape) + (_side_out_shapes(side[0], side_arrs) if ns else []),
        scratch_shapes=list(scratch_shapes) + (_side_scratch(ns) if ns else []),
        **kwargs,
    )(*args, *side_arrs)
    return res[:n_out], res[n_out:]


def _exchange_small(small, name):
    ns = small.shape[1]

    def body(small_ref, sum_ref, small_all, send_sems, recv_sems, local_sems):
        copies = _direct_copies(False, [small_ref], [small_all], send_sems, recv_sems, local_sems)
        for cp in copies:
            cp.start()
        for cp in copies:
            cp.wait()
        total = small_all[0]
        for i in range(1, N_DEV):
            total = total + small_all[i]
        sum_ref[...] = total

    vmem_spec = pl.BlockSpec(memory_space=pltpu.VMEM)
    return pl.pallas_call(
        body, name=name,
        out_shape=jax.ShapeDtypeStruct((1, ns), F32),
        in_specs=[vmem_spec], out_specs=vmem_spec,
        scratch_shapes=[pltpu.VMEM((N_DEV, 1, ns), F32)] + _side_scratch(1),
    )(small)


def _full(shape):
    return pl.BlockSpec(shape, lambda *_: (0,) * len(shape))


def _inproj_fwd(x, nw, w, name, side=None):
    t = x.shape[0]
    tm = 256

    def body(x_ref, nw_ref, w_ref, proj_ref, h_ref):
        h = _rmsnorm(x_ref[...], nw_ref[...]).astype(BF16)
        h_ref[...] = h
        for j in range(N_COL_TILES):
            sl = slice(j * COL_TILE, (j + 1) * COL_TILE)
            proj_ref[:, sl] = jnp.dot(h, w_ref[:, sl], preferred_element_type=F32)

    grid = (t // tm,)
    return _pallas_with_side(
        body, side, *_grid_ends(grid), 3, 2, name=name, grid=grid,
        out_shape=[jax.ShapeDtypeStruct((t, PROJ_W), F32), jax.ShapeDtypeStruct((t, D_MODEL), BF16)],
        in_specs=[pl.BlockSpec((tm, D_MODEL), lambda i: (i, 0)), _full((1, D_MODEL)), _full((D_MODEL, PROJ_W))],
        out_specs=[pl.BlockSpec((tm, PROJ_W), lambda i: (i, 0)), pl.BlockSpec((tm, D_MODEL), lambda i: (i, 0))],
        compiler_params=pltpu.CompilerParams(dimension_semantics=("arbitrary",), vmem_limit_bytes=VMEM_LIMIT),
        args=(x, nw, w))


def _param_specs():
    return [_full((4, 1536)), _full((1, 1536)), _full((1, 128)), _full((1, 128)), _full((1, 128)),
            _full((1, 1024)), _full((1, 128)), _full((CONF_KERNEL, 512)), _full((1, 512)), _full((1, 512)),
            _full((1, 512))]


def _halo_specs(nc, chunk_of):
    def prev_chunk(b, j):
        return jnp.maximum(b * nc + chunk_of(j) - 1, 0)

    per_xbc = CHUNK // XBC_HALO
    per_cc = CHUNK // CONF_HALO
    return [
        pl.BlockSpec((XBC_HALO, 1536), lambda b, j: (prev_chunk(b, j) * per_xbc + per_xbc - 1, C_XBC // 1536)),
        pl.BlockSpec((CHUNK, 256), lambda b, j: (prev_chunk(b, j), C_K // 256)),
        pl.BlockSpec((CONF_HALO, 1024), lambda b, j: (prev_chunk(b, j) * per_cc + per_cc - 1, C_CA // 1024)),
    ]


def _grid_ends(grid):
    first = lambda: functools.reduce(lambda p, q: p & q, [pl.program_id(i) == 0 for i in range(len(grid))])
    last = lambda: functools.reduce(lambda p, q: p & q, [pl.program_id(i) == n - 1 for i, n in enumerate(grid)])
    return first, last


def _mixer_fwd(x, proj, w_out, params, nb, name, side=None):
    t = x.shape[0]
    nc = t // nb // CHUNK

    def body(x_ref, cur_ref, pxbc_ref, pkv_ref, pcc_ref, wo_ref, *rest):
        prm = [r[...] for r in rest[:11]]
        xn_ref, sall_ref, s_scr = rest[11:]
        c = pl.program_id(1)
        not_first = c > 0
        nf = not_first.astype(F32)

        @pl.when(c == 0)
        def _():
            s_scr[...] = jnp.zeros_like(s_scr)

        cw, cb, dtb, alog, dsk, nw, snk, dww, dwb, lnw, lnb = prm
        s_in = s_scr[...]
        sall_ref[0] = s_in
        y_ssd, s_out = _ssd_part(cur_ref[:, 0:1024], cur_ref[:, C_XBC:C_XBC + 1536], cur_ref[:, C_DT:C_DT + 128],
                                 pxbc_ref[...] * nf, s_in, cw, cb, dtb, alog, dsk, nw)
        s_scr[...] = s_out
        y_attn = _attn_part(cur_ref[:, 1024:1536], cur_ref[:, C_Q:C_Q + 512], cur_ref[:, C_K:C_K + 256],
                            pkv_ref[...] * nf, snk, _kv_mask(CHUNK, not_first, ATTN_HEADS // 2))
        y_conf = _conf_part(cur_ref[:, 1536:2048], cur_ref[:, C_CA:C_CA + 1024], pcc_ref[...] * nf,
                            dww, dwb, lnw, lnb)
        xn_ref[...] = (x_ref[...] + _dot(y_ssd, wo_ref[0:1024, :]) + _dot(y_attn, wo_ref[1024:1536, :])
                       + _dot(y_conf, wo_ref[1536:2048, :]))

    row = lambda b, j: (b * nc + j, 0)
    grid = (nb, nc)
    return _pallas_with_side(
        body, side, *_grid_ends(grid), 17, 2, name=name, grid=grid,
        out_shape=[jax.ShapeDtypeStruct((t, D_MODEL), F32),
                   jax.ShapeDtypeStruct((nb * nc, SSD_HEADS * SSD_HEAD_DIM, SSD_STATE), F32)],
        in_specs=[pl.BlockSpec((CHUNK, D_MODEL), row), pl.BlockSpec((CHUNK, PROJ_W), row)]
                 + _halo_specs(nc, lambda j: j) + [_full((MIX_WIDTH, D_MODEL))] + _param_specs(),
        out_specs=[pl.BlockSpec((CHUNK, D_MODEL), row),
                   pl.BlockSpec((1, SSD_HEADS * SSD_HEAD_DIM, SSD_STATE), lambda b, j: (b * nc + j, 0, 0))],
        scratch_shapes=[pltpu.VMEM((SSD_HEADS * SSD_HEAD_DIM, SSD_STATE), F32)],
        compiler_params=pltpu.CompilerParams(dimension_semantics=("arbitrary", "arbitrary"),
                                             vmem_limit_bytes=VMEM_LIMIT),
        args=(x, proj, proj, proj, proj, w_out, *params))


def _mixer_bwd(dxn, proj, s_all, w_out, params, nb, name, side=None):
    t = dxn.shape[0]
    nc = t // nb // CHUNK
    n_prm = 11

    def body(dxn_ref, cur_ref, pxbc_ref, pkv_ref, pcc_ref, s_ref, wo_ref, *rest):
        prm = [r[...] for r in rest[:n_prm]]
        dproj_ref, gwo_ref = rest[n_prm:n_prm + 2]
        gprm = rest[n_prm + 2:2 * n_prm + 2]
        ds_scr, pend_xbc, pend_kv, pend_cc = rest[2 * n_prm + 2:]
        b, j = pl.program_id(0), pl.program_id(1)
        c = nc - 1 - j
        not_first = c > 0
        nf = not_first.astype(F32)

        @pl.when((b == 0) & (j == 0))
        def _():
            gwo_ref[...] = jnp.zeros_like(gwo_ref)
            for r in gprm:
                r[...] = jnp.zeros_like(r)

        @pl.when(j == 0)
        def _():
            ds_scr[...] = jnp.zeros_like(ds_scr)
            pend_xbc[...] = jnp.zeros_like(pend_xbc)
            pend_kv[...] = jnp.zeros_like(pend_kv)
            pend_cc[...] = jnp.zeros_like(pend_cc)

        cw, cb, dtb, alog, dsk, nw, snk, dww, dwb, lnw, lnb = prm
        g_cw, g_cb, g_dtb, g_alog, g_dsk, g_nw, g_snk, g_dww, g_dwb, g_lnw, g_lnb = gprm
        dxn_v = dxn_ref[...]

        y, vjp = jax.vjp(_conf_part, cur_ref[:, 1536:2048], cur_ref[:, C_CA:C_CA + 1024], pcc_ref[...] * nf,
                         dww, dwb, lnw, lnb)
        gwo_ref[1536:2048, :] += _dot_tn(y, dxn_v)
        dz, dcacg, dpcc, d_dww, d_dwb, d_lnw, d_lnb = vjp(_dot_nt(dxn_v, wo_ref[1536:2048, :]))
        dproj_ref[:, 1536:2048] = dz
        dproj_ref[:, C_CA:C_CA + 1024] = dcacg
        dproj_ref[CHUNK - CONF_HALO:CHUNK, C_CA:C_CA + 1024] += pend_cc[...]
        pend_cc[...] = dpcc
        for r, g in ((g_dww, d_dww), (g_dwb, d_dwb), (g_lnw, d_lnw), (g_lnb, d_lnb)):
            r[...] += g

        attn = functools.partial(_attn_part, kvmask=_kv_mask(CHUNK, not_first, ATTN_HEADS // 2))
        y, vjp = jax.vjp(attn, cur_ref[:, 1024:1536], cur_ref[:, C_Q:C_Q + 512], cur_ref[:, C_K:C_K + 256],
                         pkv_ref[...] * nf, snk)
        gwo_ref[1024:1536, :] += _dot_tn(y, dxn_v)
        dz, dq, dkv, dpkv, d_snk = vjp(_dot_nt(dxn_v, wo_ref[1024:1536, :]))
        dproj_ref[:, 1024:1536] = dz
        dproj_ref[:, C_Q:C_Q + 512] = dq
        dproj_ref[:, C_K:C_K + 256] = dkv + pend_kv[...]
        pend_kv[...] = dpkv
        g_snk[...] += d_snk

        (y, _), vjp = jax.vjp(_ssd_part, cur_ref[:, 0:1024], cur_ref[:, C_XBC:C_XBC + 1536],
                              cur_ref[:, C_DT:C_DT + 128], pxbc_ref[...] * nf, s_ref[0], cw, cb, dtb, alog, dsk, nw)
        gwo_ref[0:1024, :] += _dot_tn(y, dxn_v)
        dz, dxbc, ddtr, dpxbc, ds_in, d_cw, d_cb, d_dtb, d_alog, d_dsk, d_nw = vjp(
            (_dot_nt(dxn_v, wo_ref[0:1024, :]), ds_scr[...]))
        dproj_ref[:, 0:1024] = dz
        dproj_ref[:, C_XBC:C_XBC + 1536] = dxbc
        dproj_ref[CHUNK - XBC_HALO:CHUNK, C_XBC:C_XBC + 1536] += pend_xbc[...]
        dproj_ref[:, C_DT:C_DT + 128] = ddtr
        dproj_ref[:, C_DT + 128:PROJ_W] = jnp.zeros((CHUNK, PROJ_W - C_DT - 128), F32)
        pend_xbc[...] = dpxbc
        ds_scr[...] = ds_in
        for r, g in ((g_cw, d_cw), (g_cb, d_cb), (g_dtb, d_dtb), (g_alog, d_alog), (g_dsk, d_dsk), (g_nw, d_nw)):
            r[...] += g

    row = lambda b, j: (b * nc + nc - 1 - j, 0)
    prm_shapes = [(4, 1536), (1, 1536), (1, 128), (1, 128), (1, 128), (1, 1024), (1, 128), (CONF_KERNEL, 512),
                  (1, 512), (1, 512), (1, 512)]
    grid = (nb, nc)
    return _pallas_with_side(
        body, side, *_grid_ends(grid), 7 + n_prm, 2 + n_prm, name=name, grid=grid,
        out_shape=[jax.ShapeDtypeStruct((t, PROJ_W), F32), jax.ShapeDtypeStruct((MIX_WIDTH, D_MODEL), F32)]
                  + [jax.ShapeDtypeStruct(s, F32) for s in prm_shapes],
        in_specs=[pl.BlockSpec((CHUNK, D_MODEL), row), pl.BlockSpec((CHUNK, PROJ_W), row)]
                 + _halo_specs(nc, lambda j: nc - 1 - j)
                 + [pl.BlockSpec((1, SSD_HEADS * SSD_HEAD_DIM, SSD_STATE), lambda b, j: (b * nc + nc - 1 - j, 0, 0)),
                    _full((MIX_WIDTH, D_MODEL))] + _param_specs(),
        out_specs=[pl.BlockSpec((CHUNK, PROJ_W), row), _full((MIX_WIDTH, D_MODEL))] + [_full(s) for s in prm_shapes],
        scratch_shapes=[pltpu.VMEM((SSD_HEADS * SSD_HEAD_DIM, SSD_STATE), F32), pltpu.VMEM((XBC_HALO, 1536), F32),
                        pltpu.VMEM((CHUNK, 256), F32), pltpu.VMEM((CONF_HALO, 1024), F32)],
        compiler_params=pltpu.CompilerParams(dimension_semantics=("arbitrary", "arbitrary"),
                                             vmem_limit_bytes=VMEM_LIMIT),
        args=(dxn, proj, proj, proj, proj, s_all, w_out, *params))


def _inproj_bwd_x(dproj, w, x, nw, dxn, name, side=None):
    t = x.shape[0]
    tm = 256

    def body(dp_ref, w_ref, x_ref, nw_ref, dxn_ref, dx_ref, gnw_ref):
        @pl.when(pl.program_id(0) == 0)
        def _():
            gnw_ref[...] = jnp.zeros_like(gnw_ref)

        dh = jnp.zeros((tm, D_MODEL), F32)
        for j in range(N_COL_TILES):
            sl = slice(j * COL_TILE, (j + 1) * COL_TILE)
            dh = dh + _dot_nt(dp_ref[:, sl], w_ref[:, sl])
        _, vjp = jax.vjp(_rmsnorm, x_ref[...], nw_ref[...])
        dx, dnw = vjp(dh)
        dx_ref[...] = dxn_ref[...] + dx
        gnw_ref[...] += dnw

    tok = lambda i: (i, 0)
    grid = (t // tm,)
    return _pallas_with_side(
        body, side, *_grid_ends(grid), 5, 2, name=name, grid=grid,
        out_shape=[jax.ShapeDtypeStruct((t, D_MODEL), F32), jax.ShapeDtypeStruct((1, D_MODEL), F32)],
        in_specs=[pl.BlockSpec((tm, PROJ_W), tok), _full((D_MODEL, PROJ_W)), pl.BlockSpec((tm, D_MODEL), tok),
                  _full((1, D_MODEL)), pl.BlockSpec((tm, D_MODEL), tok)],
        out_specs=[pl.BlockSpec((tm, D_MODEL), tok), _full((1, D_MODEL))],
        compiler_params=pltpu.CompilerParams(dimension_semantics=("arbitrary",), vmem_limit_bytes=VMEM_LIMIT),
        args=(dproj, w, x, nw, dxn))


def _inproj_bwd_w(h, dproj, name, side=None):
    t = h.shape[0]
    tk = 512

    def body(h_ref, dp_ref, gw_ref):
        @pl.when(pl.program_id(1) == 0)
        def _():
            gw_ref[...] = jnp.zeros_like(gw_ref)

        gw_ref[...] += _dot_tn(h_ref[...], dp_ref[...])

    grid = (N_COL_TILES, t // tk)
    return _pallas_with_side(
        body, side, *_grid_ends(grid), 2, 1, name=name, grid=grid,
        out_shape=[jax.ShapeDtypeStruct((D_MODEL, PROJ_W), F32)],
        in_specs=[pl.BlockSpec((tk, D_MODEL), lambda n, k: (k, 0)), pl.BlockSpec((tk, COL_TILE), lambda n, k: (k, n))],
        out_specs=[pl.BlockSpec((D_MODEL, COL_TILE), lambda n, k: (0, n))],
        compiler_params=pltpu.CompilerParams(dimension_semantics=("arbitrary", "arbitrary"),
                                             vmem_limit_bytes=VMEM_LIMIT),
        args=(h, dproj))


def _repack_runs():
    pieces = ((0, 2048, C_Z), (2048, 3584, C_XBC), (3584, 3600, C_DT), (3600, 4368, C_Q), (4368, D_IN_PROJ, C_CA))
    per = D_IN_PROJ // N_DEV
    runs = []
    for j in range(N_DEV):
        lo, hi = per * j, per * (j + 1)
        for a, b, dst in pieces:
            s, e = max(lo, a), min(hi, b)
            if s < e:
                runs.append((j, s - lo, e - lo, dst + s - a))
    return runs


def _repack_w_in(g, name):
    tr = 256

    def body(g_ref, o_ref):
        for j, a, b, dst in _repack_runs():
            o_ref[:, dst:dst + b - a] = g_ref[j, :, a:b]
        o_ref[:, C_DT + 16:PROJ_W] = jnp.zeros((tr, PROJ_W - C_DT - 16), g.dtype)

    return pl.pallas_call(
        body, name=name, grid=(D_MODEL // tr,),
        out_shape=jax.ShapeDtypeStruct((D_MODEL, PROJ_W), g.dtype),
        in_specs=[pl.BlockSpec((N_DEV, tr, D_IN_PROJ // N_DEV), lambda i: (0, i, 0))],
        out_specs=pl.BlockSpec((tr, PROJ_W), lambda i: (i, 0)),
        compiler_params=pltpu.CompilerParams(dimension_semantics=("arbitrary",)),
    )(g)


def _unpack_gw_in(g, name):
    tr = 256

    def body(g_ref, o_ref):
        for j, a, b, dst in _repack_runs():
            o_ref[j, :, a:b] = g_ref[:, dst:dst + b - a].astype(BF16)

    return pl.pallas_call(
        body, name=name, grid=(D_MODEL // tr,),
        out_shape=jax.ShapeDtypeStruct((N_DEV, D_MODEL, D_IN_PROJ // N_DEV), BF16),
        in_specs=[pl.BlockSpec((tr, PROJ_W), lambda i: (i, 0))],
        out_specs=pl.BlockSpec((N_DEV, tr, D_IN_PROJ // N_DEV), lambda i: (0, i, 0)),
        compiler_params=pltpu.CompilerParams(dimension_semantics=("arbitrary",)),
    )(g)


def _loss_head(x, fnw, target, name):
    t = x.shape[0]
    tm = 512

    def body(x_ref, w_ref, t_ref, dx_ref, loss_ref, gw_ref):
        @pl.when(pl.program_id(0) == 0)
        def _():
            loss_ref[...] = jnp.zeros_like(loss_ref)
            gw_ref[...] = jnp.zeros_like(gw_ref)

        y, vjp = jax.vjp(_rmsnorm, x_ref[...], w_ref[...])
        err = y - t_ref[...]
        loss_ref[...] += 0.5 * jnp.sum(jnp.mean(err * err, axis=-1, keepdims=True), axis=0, keepdims=True)
        dx, dw = vjp(err * (1.0 / D_MODEL))
        dx_ref[...] = dx
        gw_ref[...] += dw

    tok = lambda i: (i, 0)
    return pl.pallas_call(
        body, name=name, grid=(t // tm,),
        out_shape=[jax.ShapeDtypeStruct((t, D_MODEL), F32), jax.ShapeDtypeStruct((1, 1), F32),
                   jax.ShapeDtypeStruct((1, D_MODEL), F32)],
        in_specs=[pl.BlockSpec((tm, D_MODEL), tok), _full((1, D_MODEL)), pl.BlockSpec((tm, D_MODEL), tok)],
        out_specs=[pl.BlockSpec((tm, D_MODEL), tok), _full((1, 1)), _full((1, D_MODEL))],
        compiler_params=pltpu.CompilerParams(dimension_semantics=("arbitrary",)),
    )(x, fnw, target)


def _adamw(w, g, m, v):
    m = ADAM_B1 * m + (1.0 - ADAM_B1) * g
    v = ADAM_B2 * v + (1.0 - ADAM_B2) * jnp.square(g)
    m_hat = m / (1.0 - ADAM_B1 ** ADAM_STEP)
    v_hat = v / (1.0 - ADAM_B2 ** ADAM_STEP)
    delta = -ADAM_LR * (m_hat / (jnp.sqrt(v_hat) + ADAM_EPS) + ADAM_WD * w)
    return delta, m, v


def _reduce_adamw(parts, w, m, v, tr, name):
    depth = len(parts)
    p, r, cdim = parts[0].shape
    n_blk = r // tr

    def body(*refs):
        p_refs = refs[:depth]
        w_ref, m_ref, v_ref, g_ref, d_ref, nm_ref, nv_ref = refs[depth:]
        for layer in range(depth):
            @pl.when(pl.program_id(0) == layer)
            def _(p_ref=p_refs[layer]):
                g = p_ref[0].astype(F32)
                for i in range(1, p):
                    g = g + p_ref[i].astype(F32)
                g_ref[0] = g
                d_ref[0], nm_ref[0], nv_ref[0] = _adamw(w_ref[0], g, m_ref[0], v_ref[0])

    def parts_spec(layer):
        return pl.BlockSpec((p, tr, cdim), lambda d, i: (0, jnp.clip(i + (d - layer) * n_blk, 0, n_blk - 1), 0))

    blk = pl.BlockSpec((1, tr, cdim), lambda d, i: (d, i, 0))
    return pl.pallas_call(
        body, name=name, grid=(depth, n_blk),
        out_shape=[jax.ShapeDtypeStruct(w.shape, F32)] * 4,
        in_specs=[parts_spec(layer) for layer in range(depth)] + [blk, blk, blk],
        out_specs=[blk] * 4,
        compiler_params=pltpu.CompilerParams(dimension_semantics=("arbitrary", "arbitrary"),
                                             vmem_limit_bytes=VMEM_LIMIT),
    )(*parts, w, m, v)


def _pad_lanes(v, width=128):
    return jnp.pad(v.reshape(1, -1), ((0, 0), (0, width - v.shape[-1])))


SMALL_FIELDS = (("norm_w", 1024), ("conv_b", 1536), ("dt_bias", 128), ("a_log", 128), ("d_skip", 128),
                ("ssd_norm_w", 1024), ("sinks", 128), ("dw_b", 512), ("ln_w", 512), ("ln_b", 512))


def kernel(x, norm_w, w_in, ssd_conv_w, ssd_conv_b, ssd_dt_bias, ssd_a_log, ssd_d, ssd_norm_w, attn_sinks, conf_dw_w, conf_dw_b, conf_ln_w, conf_ln_b, w_out, final_norm_w, loss_target, m_norm_w, m_w_in, m_ssd_conv_w, m_ssd_conv_b, m_ssd_dt_bias, m_ssd_a_log, m_ssd_d, m_ssd_norm_w, m_attn_sinks, m_conf_dw_w, m_conf_dw_b, m_conf_ln_w, m_conf_ln_b, m_w_out, m_final_norm_w, v_norm_w, v_w_in, v_ssd_conv_w, v_ssd_conv_b, v_ssd_dt_bias, v_ssd_a_log, v_ssd_d, v_ssd_norm_w, v_attn_sinks, v_conf_dw_w, v_conf_dw_b, v_conf_ln_w, v_conf_ln_b, v_w_out, v_final_norm_w):
    nb, seq, _ = x.shape
    depth = norm_w.shape[0]
    t = nb * seq
    me_idx = 4 * lax.axis_index("x") + 2 * lax.axis_index("y") + lax.axis_index("c")

    w_in_bf, w_out_bf = w_in.astype(BF16), w_out.astype(BF16)
    g_win0, g_cw, g_dw = _all_gather([w_in_bf[0], ssd_conv_w, conf_dw_w], "gather_weights")
    w_in_full = [_repack_w_in(g_win0, "repack_w_in_0")]
    w_out_full = []
    conv_w_full = [jnp.transpose(g_cw[:, l], (1, 0, 2)).reshape(4, 1536) for l in range(depth)]
    dw_w_full = [jnp.transpose(g_dw[:, l], (1, 0, 2)).reshape(CONF_KERNEL, 512) for l in range(depth)]

    def layer_params(l):
        return [conv_w_full[l], ssd_conv_b[l].reshape(1, -1), _pad_lanes(ssd_dt_bias[l]), _pad_lanes(ssd_a_log[l]),
                _pad_lanes(ssd_d[l]), ssd_norm_w[l].reshape(1, -1), _pad_lanes(attn_sinks[l]), dw_w_full[l],
                conf_dw_b[l].reshape(1, -1), conf_ln_w[l].reshape(1, -1), conf_ln_b[l].reshape(1, -1)]

    xs = [x.reshape(t, D_MODEL)]
    saved = []
    for l in range(depth):
        (proj, h), gathered = _inproj_fwd(xs[l], norm_w[l].reshape(1, -1), w_in_full[l], f"inproj_fwd_{l}",
                                          (False, [w_out_bf[0]]) if l == 0 else None)
        if l == 0:
            w_out_full.append(gathered[0].reshape(MIX_WIDTH, D_MODEL))
        side = (False, [w_in_bf[l + 1], w_out_bf[l + 1]]) if l + 1 < depth else None
        (x_next, s_all), gathered = _mixer_fwd(xs[l], proj, w_out_full[l], layer_params(l), nb, f"mixer_fwd_{l}",
                                               side)
        if side:
            w_in_full.append(_repack_w_in(gathered[0], f"repack_w_in_{l + 1}"))
            w_out_full.append(gathered[1].reshape(MIX_WIDTH, D_MODEL))
        saved.append((proj, h, s_all))
        xs.append(x_next)
    dx, loss_part, g_fnw = _loss_head(xs[depth], final_norm_w.reshape(1, -1), loss_target.reshape(t, D_MODEL),
                                      "loss_head")

    cols_in = D_IN_PROJ // N_DEV
    rows_out = MIX_WIDTH // N_DEV
    small_rows = [None] * depth
    received = [None] * depth
    outgoing = None
    for l in reversed(range(depth)):
        proj, h, s_all = saved[l]
        res, arrived = _mixer_bwd(dx, proj, s_all, w_out_full[l], layer_params(l), nb, f"mixer_bwd_{l}",
                                  (True, outgoing) if outgoing else None)
        if outgoing:
            received[l + 1] = arrived
        dproj, gw_out = res[0], res[1]
        g_cw_l, g_cb, g_dtb, g_alog, g_dsk, g_nw, g_snk, g_dww, g_dwb, g_lnw, g_lnb = res[2:]
        gw_out_parts = gw_out.reshape(N_DEV, rows_out, D_MODEL).astype(BF16)
        nw_l = norm_w[l].reshape(1, -1)
        if l > 0:
            (gw_in,), _ = _inproj_bwd_w(h, dproj, f"inproj_bwd_w_{l}")
            (dx, g_norm), _ = _inproj_bwd_x(dproj, w_in_full[l], xs[l], nw_l, dx, f"inproj_bwd_x_{l}")
            outgoing = [_unpack_gw_in(gw_in, f"unpack_gw_in_{l}"), gw_out_parts]
        else:
            (gw_in,), got_out = _inproj_bwd_w(h, dproj, f"inproj_bwd_w_{l}", (True, [gw_out_parts]))
            gw_in_parts = _unpack_gw_in(gw_in, f"unpack_gw_in_{l}")
            (dx, g_norm), got_in = _inproj_bwd_x(dproj, w_in_full[l], xs[l], nw_l, dx, f"inproj_bwd_x_{l}",
                                                 (True, [gw_in_parts]))
            received[l] = [got_in[0], got_out[0]]
        small_rows[l] = [g_norm, g_cb, g_dtb, g_alog, g_dsk, g_nw, g_snk, g_dwb, g_lnw, g_lnb,
                         g_cw_l.reshape(1, -1), g_dww.reshape(1, -1)]
    grad_x = dx.reshape(nb, seq, D_MODEL)

    small = jnp.concatenate([piece for l in range(depth) for piece in small_rows[l]] + [g_fnw], axis=1)
    ssum = _exchange_small(small, "exchange_small")

    g_w_in, d_w_in, nm_w_in, nv_w_in = _reduce_adamw([received[l][0] for l in range(depth)], w_in, m_w_in, v_w_in,
                                                     256, "adamw_w_in")
    g_w_out, d_w_out, nm_w_out, nv_w_out = _reduce_adamw([received[l][1] for l in range(depth)], w_out, m_w_out,
                                                         v_w_out, 256, "adamw_w_out")

    per_layer = sum(n for _, n in SMALL_FIELDS) + 4 * 1536 + CONF_KERNEL * 512
    g_small = {}
    for l in range(depth):
        off = l * per_layer
        for fname, n in SMALL_FIELDS:
            g_small[(fname, l)] = ssum[:, off:off + n]
            off += n
        g_small[("conv_w", l)] = lax.dynamic_slice(ssum[:, off:off + 4 * 1536].reshape(4, 1536),
                                                   (0, me_idx * 192), (4, 192)).reshape(1, -1)
        off += 4 * 1536
        g_small[("dw_w", l)] = lax.dynamic_slice(ssum[:, off:off + CONF_KERNEL * 512].reshape(CONF_KERNEL, 512),
                                                 (0, me_idx * 64), (CONF_KERNEL, 64)).reshape(1, -1)
    g_small["final"] = ssum[:, depth * per_layer:depth * per_layer + 1024]

    def small_triplet(w, m, v, width=None):
        if width is None:
            return [a.reshape(1, -1) for a in (w, m, v)]
        return [_pad_lanes(a, width) for a in (w, m, v)]

    order = []
    given = {"norm_w": (norm_w, m_norm_w, v_norm_w), "conv_b": (ssd_conv_b, m_ssd_conv_b, v_ssd_conv_b),
             "dt_bias": (ssd_dt_bias, m_ssd_dt_bias, v_ssd_dt_bias), "a_log": (ssd_a_log, m_ssd_a_log, v_ssd_a_log),
             "d_skip": (ssd_d, m_ssd_d, v_ssd_d), "ssd_norm_w": (ssd_norm_w, m_ssd_norm_w, v_ssd_norm_w),
             "sinks": (attn_sinks, m_attn_sinks, v_attn_sinks), "dw_b": (conf_dw_b, m_conf_dw_b, v_conf_dw_b),
             "ln_w": (conf_ln_w, m_conf_ln_w, v_conf_ln_w), "ln_b": (conf_ln_b, m_conf_ln_b, v_conf_ln_b),
             "conv_w": (ssd_conv_w, m_ssd_conv_w, v_ssd_conv_w), "dw_w": (conf_dw_w, m_conf_dw_w, v_conf_dw_w)}
    for l in range(depth):
        for fname, n in SMALL_FIELDS:
            w, m, v = given[fname]
            order.append(((fname, l), w.shape[1], small_triplet(w[l], m[l], v[l], n)))
        for fname in ("conv_w", "dw_w"):
            w, m, v = given[fname]
            order.append(((fname, l), w[l].size, small_triplet(w[l], m[l], v[l])))
    order.append(("final", 1024, small_triplet(final_norm_w, m_final_norm_w, v_final_norm_w)))
    g_row = jnp.concatenate([g_small[key] for key, _, _ in order], axis=1)
    pad_to = -g_row.shape[1] % 1024
    rows = [jnp.pad(jnp.concatenate([trip[i] for _, _, trip in order], axis=1), ((0, 0), (0, pad_to)))
            for i in range(3)]
    g_row = jnp.pad(g_row, ((0, 0), (0, pad_to)))
    n_small = g_row.shape[1]
    as_tiles = lambda a: a.reshape(n_small // 1024, 1024)
    sm = _reduce_adamw([as_tiles(g_row)[None]], as_tiles(rows[0])[None], as_tiles(rows[1])[None],
                       as_tiles(rows[2])[None], n_small // 1024, "adamw_small")
    sm = [a.reshape(1, n_small) for a in sm]
    pieces = {}
    off = 0
    for key, n_true, trip in order:
        width = trip[0].shape[1]
        pieces[key] = [a[0, off:off + n_true] for a in sm]
        off += width

    def stacked(fname, like, i):
        return jnp.stack([pieces[(fname, l)][i].reshape(like.shape[1:]) for l in range(depth)], axis=0)

    names = [("norm_w", norm_w), ("conv_w", ssd_conv_w), ("conv_b", ssd_conv_b), ("dt_bias", ssd_dt_bias),
             ("a_log", ssd_a_log), ("d_skip", ssd_d), ("ssd_norm_w", ssd_norm_w), ("sinks", attn_sinks),
             ("dw_w", conf_dw_w), ("dw_b", conf_dw_b), ("ln_w", conf_ln_w), ("ln_b", conf_ln_b)]

    def outputs(i, big_in_i, big_out_i):
        vals = {fname: stacked(fname, like, i) for fname, like in names}
        return [vals["norm_w"], big_in_i, vals["conv_w"], vals["conv_b"], vals["dt_bias"], vals["a_log"],
                vals["d_skip"], vals["ssd_norm_w"], vals["sinks"], vals["dw_w"], vals["dw_b"], vals["ln_w"],
                vals["ln_b"], big_out_i, pieces["final"][i]]

    loss = lax.psum(loss_part[0, 0], MESH_AXES)
    return (loss, grad_x, *outputs(0, g_w_in, g_w_out), *outputs(1, d_w_in, d_w_out),
            *outputs(2, nm_w_in, nm_w_out), *outputs(3, nv_w_in, nv_w_out))
```

```python
import functools

import jax
import jax.numpy as jnp
from jax import lax
from jax.experimental import pallas as pl
from jax.experimental.pallas import tpu as pltpu

F32 = jnp.float32
BF16 = jnp.bfloat16
MESH_AXES = ("x", "y", "c")
N_DEV = 8
EPS = 1e-5

D_MODEL = 1024
CHUNK = 128
SSD_HEADS = 16
SSD_HEAD_DIM = 64
SSD_STATE = 128
ATTN_HEADS = 8
ATTN_HEAD_DIM = 64
CONF_KERNEL = 31
MIX_WIDTH = 2048
D_IN_PROJ = 5392
C_Z = 0
C_CA = 2048
C_XBC = 3072
C_Q = 4608
C_K = 5120
C_V = 5248
C_DT = 5376
PROJ_W = 5632
N_COL_TILES = 4
COL_TILE = PROJ_W // N_COL_TILES
XBC_HALO = 8
CONF_HALO = 32
VMEM_LIMIT = 56 * 1024 * 1024

ADAM_LR = 0.001
ADAM_B1 = 0.9
ADAM_B2 = 0.999
ADAM_EPS = 1e-08
ADAM_WD = 0.01
ADAM_STEP = 10


def _silu(v):
    return v * jax.nn.sigmoid(v)


def _softplus(v):
    return jnp.maximum(v, 0.0) + jnp.log1p(jnp.exp(-jnp.abs(v)))


def _rmsnorm(v, w):
    return v * lax.rsqrt(jnp.mean(v * v, axis=-1, keepdims=True) + EPS) * w


def _dot(a, b):
    return jnp.dot(a.astype(BF16), b.astype(BF16), preferred_element_type=F32)


def _dot_nt(a, b):
    return lax.dot_general(a.astype(BF16), b.astype(BF16), (((1,), (1,)), ((), ())), preferred_element_type=F32)


def _dot_tn(a, b):
    return lax.dot_general(a.astype(BF16), b.astype(BF16), (((0,), (0,)), ((), ())), preferred_element_type=F32)


def _taps(ext, offs, out_len, w=None, g=None):
    n_rows, n_cols = ext.shape
    by_shift = {}
    for t, off in enumerate(offs):
        by_shift.setdefault(off % 8, []).append((t, off))
    for r, taps in by_shift.items():
        assert max(off for _, off in taps) - r + out_len <= n_rows - r
    accs = []
    sums = [[None] * (n_cols // 128) for _ in offs]
    for blk in range(n_cols // 128):
        cs = slice(blk * 128, (blk + 1) * 128)
        e = ext[:, cs]
        acc = None
        for r, taps in by_shift.items():
            shifted = e if r == 0 else pltpu.roll(e, n_rows - r, axis=0)
            for t, off in taps:
                window = shifted[off - r:off - r + out_len, :]
                if w is not None:
                    term = w[t:t + 1, cs] * window
                    acc = term if acc is None else acc + term
                if g is not None:
                    sums[t][blk] = jnp.sum(g[:, cs] * window, axis=0, keepdims=True)
        accs.append(acc)
    if w is not None:
        return jnp.concatenate(accs, axis=1)
    return jnp.concatenate([jnp.concatenate(row, axis=1) for row in sums], axis=0)


@functools.partial(jax.custom_vjp, nondiff_argnums=(3,))
def _dwconv(ext, w, b, halo):
    kk = w.shape[0]
    return b + _taps(ext, [halo - (kk - 1) + t for t in range(kk)], ext.shape[0] - halo, w=w)


def _dwconv_fwd(ext, w, b, halo):
    return _dwconv(ext, w, b, halo), (ext, w)


def _dwconv_bwd(halo, res, g):
    ext, w = res
    kk = w.shape[0]
    offs = [halo - (kk - 1) + t for t in range(kk)]
    dw = _taps(ext, offs, ext.shape[0] - halo, g=g)
    zeros = jnp.zeros((halo, g.shape[1]), g.dtype)
    gp = jnp.concatenate([zeros, g, zeros], axis=0)
    dext = _taps(gp, [halo - off for off in offs], ext.shape[0], w=w)
    return dext, dw, jnp.sum(g, axis=0, keepdims=True)


_dwconv.defvjp(_dwconv_fwd, _dwconv_bwd)


def _ssd_part(z_ssd, xbc, dtr, p_xbc, s_in, cw, cb, dtb, alog, dsk, nw):
    qn = xbc.shape[0]
    nh = SSD_HEADS
    per_group = nh // 2
    n_pair = nh // 2
    xa = _silu(_dwconv(jnp.concatenate([p_xbc, xbc], axis=0), cw, cb, XBC_HALO))
    xs = xa[:, 0:1024]
    dt = _softplus(dtr + dtb)
    a = dt * (-jnp.exp(alog))
    rows = lax.broadcasted_iota(jnp.int32, (qn, qn), 0)
    cols = lax.broadcasted_iota(jnp.int32, (qn, qn), 1)
    causal = rows >= cols
    low = cols < SSD_HEAD_DIM
    a_cs = jnp.dot(causal.astype(F32), a, precision=lax.Precision.HIGHEST, preferred_element_type=F32)
    a_cs_t = a_cs.T
    bgs = [xa[:, 1024 + g * 128:1024 + (g + 1) * 128] for g in range(2)]
    cgs = [xa[:, 1280 + g * 128:1280 + (g + 1) * 128] for g in range(2)]
    cbms = [_dot_nt(cgs[g], bgs[g]) for g in range(2)]
    colb = [jnp.broadcast_to(a_cs[:, h:h + 1], (qn, qn)) for h in range(nh)]
    lastb = [jnp.broadcast_to(colb[h][qn - 1:qn, :], (qn, qn)) for h in range(nh)]
    dtb_wide = [jnp.broadcast_to(dt[:, h:h + 1], (qn, qn)) for h in range(nh)]
    lmats = [jnp.exp(jnp.where(causal, colb[h] - a_cs_t[h:h + 1, :], -jnp.inf)) for h in range(nh)]
    ms = [cbms[h // per_group] * lmats[h] for h in range(nh)]
    x_pair = [xs[:, p * 128:(p + 1) * 128] for p in range(n_pair)]
    x_lo = [x_pair[p] * jnp.where(low, dtb_wide[2 * p], 0.0) for p in range(n_pair)]
    x_hi = [x_pair[p] * jnp.where(low, 0.0, dtb_wide[2 * p + 1]) for p in range(n_pair)]
    y_diag = [_dot(ms[2 * p], x_lo[p]) + _dot(ms[2 * p + 1], x_hi[p]) for p in range(n_pair)]
    col_pair = [jnp.where(low, colb[2 * p], colb[2 * p + 1]) for p in range(n_pair)]
    last_pair = [jnp.where(low, lastb[2 * p], lastb[2 * p + 1]) for p in range(n_pair)]
    ecol = [jnp.exp(col_pair[p]) for p in range(n_pair)]
    xw = [(x_lo[p] + x_hi[p]) * jnp.exp(last_pair[p] - col_pair[p]) for p in range(n_pair)]
    y_off, st = [], []
    for g in range(2):
        ps = range(g * n_pair // 2, (g + 1) * n_pair // 2)
        y_off.append(_dot_nt(cgs[g], s_in[g * 512:(g + 1) * 512, :]) * jnp.concatenate([ecol[p] for p in ps], axis=1))
        st.append(_dot_tn(jnp.concatenate([xw[p] for p in ps], axis=1), bgs[g]))
    e_last = jnp.exp(jnp.broadcast_to(a_cs_t[:, qn - 1:qn], (qn, SSD_STATE)))
    scale = jnp.concatenate([jnp.broadcast_to(e_last[h:h + 1, :], (64, SSD_STATE)) for h in range(nh)], axis=0)
    s_out = scale * s_in + jnp.concatenate(st, axis=0)
    d_wide = jnp.concatenate([jnp.broadcast_to(dsk[:, h:h + 1], (1, 64)) for h in range(nh)], axis=1)
    y = jnp.concatenate(y_diag, axis=1) + jnp.concatenate(y_off, axis=1) + d_wide * xs
    gated = y * _silu(z_ssd)
    halves = []
    for g in range(2):
        gg = gated[:, g * 512:(g + 1) * 512]
        halves.append(gg * lax.rsqrt(jnp.mean(gg * gg, axis=-1, keepdims=True) + EPS))
    return jnp.concatenate(halves, axis=1) * nw, s_out


def _attn_part(z_attn, q, kv, p_kv, snk, kvmask):
    qn = q.shape[0]
    per_group = ATTN_HEADS // 2
    kk = jnp.concatenate([p_kv[:, 0:128], kv[:, 0:128]], axis=0)
    vv = jnp.concatenate([p_kv[:, 128:256], kv[:, 128:256]], axis=0)
    groups = range(2)
    heads = [range(g * per_group, (g + 1) * per_group) for g in groups]
    qs = [jnp.concatenate([q[:, h * 64:(h + 1) * 64] for h in heads[g]], axis=0) for g in groups]
    sk = [jnp.concatenate([jnp.broadcast_to(snk[:, h:h + 1], (qn, 1)) for h in heads[g]], axis=0) for g in groups]
    s = [jnp.where(kvmask, _dot_nt(qs[g], kk[:, g * 64:(g + 1) * 64]) * (ATTN_HEAD_DIM ** -0.5), -jnp.inf)
         for g in groups]
    m = [lax.stop_gradient(jnp.maximum(jnp.max(s[g], axis=-1, keepdims=True), sk[g])) for g in groups]
    e = [jnp.exp(s[g] - m[g]) for g in groups]
    den = [jnp.sum(e[g], axis=-1, keepdims=True) + jnp.exp(sk[g] - m[g]) for g in groups]
    o = [_dot(e[g] / den[g], vv[:, g * 64:(g + 1) * 64]) for g in groups]
    outs = [o[g][i * qn:(i + 1) * qn, :] for g in groups for i in range(per_group)]
    return jnp.concatenate(outs, axis=1) * _silu(z_attn)


def _conf_part(z_conf, cacg, p_cc, dww, dwb, lnw, lnb):
    c0 = cacg[:, 0:512] * jax.nn.sigmoid(cacg[:, 512:1024])
    pc0 = p_cc[:, 0:512] * jax.nn.sigmoid(p_cc[:, 512:1024])
    acc = _dwconv(jnp.concatenate([pc0, c0], axis=0), dww, dwb, CONF_HALO)
    xc = acc - jnp.mean(acc, axis=-1, keepdims=True)
    yln = xc * lax.rsqrt(jnp.mean(xc * xc, axis=-1, keepdims=True) + EPS) * lnw + lnb
    return _silu(yln) * _silu(z_conf)


def _kv_mask(qn, not_first, reps):
    ii = lax.broadcasted_iota(jnp.int32, (reps * qn, 2 * qn), 0) & (qn - 1)
    jj = lax.broadcasted_iota(jnp.int32, (reps * qn, 2 * qn), 1)
    d = jj - ii
    return (d >= 1) & (d <= qn) & (not_first | (jj >= qn))


def _my_place():
    return lax.axis_index("x"), lax.axis_index("y"), lax.axis_index("c")


def _all_gather(arrs, name):
    n = len(arrs)

    def body(*refs):
        ins, outs = refs[:n], refs[n:2 * n]
        send_sems, recv_sems, local_sems = refs[2 * n:]
        x, y, c = _my_place()
        me, sibling = (x, y, c), (x, y, 1 - c)
        chips = [(1 - x, y), (x, 1 - y), (1 - x, 1 - y)]

        def slot(a, p):
            return outs[a].at[4 * p[0] + 2 * p[1] + p[2]]

        def copy(a, kk, block, to, src=None):
            return pltpu.make_async_remote_copy(
                src_ref=slot(a, block) if src is None else src, dst_ref=slot(a, block),
                send_sem=send_sems.at[a, kk], recv_sem=recv_sems.at[a, kk],
                device_id=to, device_id_type=pl.DeviceIdType.MESH)

        mine = [pltpu.make_async_copy(ins[a], slot(a, me), local_sems.at[a]) for a in range(n)]
        for cp in mine:
            cp.start()
        first = []
        for a in range(n):
            first.append(copy(a, 0, me, sibling, src=ins[a]))
            first += [copy(a, 1 + j, me, (*chip, c), src=ins[a]) for j, chip in enumerate(chips)]
        for cp in first:
            cp.start()
        passed = []
        for j, chip in enumerate(chips):
            for a in range(n):
                copy(a, 1 + j, (*chip, c), me).wait_recv()
                fwd = copy(a, 4 + j, (*chip, c), sibling)
                fwd.start()
                passed.append(fwd)
        for a in range(n):
            copy(a, 0, sibling, me).wait_recv()
            for j, chip in enumerate(chips):
                copy(a, 4 + j, (*chip, 1 - c), me).wait_recv()
        for cp in first + passed:
            cp.wait_send()
        for cp in mine:
            cp.wait()

    any_spec = pl.BlockSpec(memory_space=pl.ANY)
    return pl.pallas_call(
        body, name=name,
        out_shape=[jax.ShapeDtypeStruct((N_DEV,) + a.shape, a.dtype) for a in arrs],
        in_specs=[any_spec] * n, out_specs=[any_spec] * n,
        scratch_shapes=[pltpu.SemaphoreType.DMA((n, 7)), pltpu.SemaphoreType.DMA((n, 7)),
                        pltpu.SemaphoreType.DMA((n,))],
    )(*arrs)


def _direct_copies(scatter, ins, outs, send_sems, recv_sems, local_sems):
    x, y, c = _my_place()
    me_idx = 4 * x + 2 * y + c
    n = len(ins)
    local = [pltpu.make_async_copy(ins[a].at[me_idx] if scatter else ins[a], outs[a].at[me_idx], local_sems.at[a])
             for a in range(n)]
    remote = []
    for rel in range(1, N_DEV):
        px = 1 - x if rel & 4 else x
        py = 1 - y if rel & 2 else y
        pc = 1 - c if rel & 1 else c
        peer_idx = 4 * px + 2 * py + pc
        for a in range(n):
            remote.append(pltpu.make_async_remote_copy(
                src_ref=ins[a].at[peer_idx] if scatter else ins[a], dst_ref=outs[a].at[me_idx],
                send_sem=send_sems.at[a, rel - 1], recv_sem=recv_sems.at[a, rel - 1],
                device_id=(px, py, pc), device_id_type=pl.DeviceIdType.MESH))
    return local + remote


def _side_scratch(n):
    return [pltpu.SemaphoreType.DMA((n, 7)), pltpu.SemaphoreType.DMA((n, 7)), pltpu.SemaphoreType.DMA((n,))]


def _side_out_shapes(scatter, arrs):
    return [jax.ShapeDtypeStruct(a.shape if scatter else (N_DEV,) + a.shape, a.dtype) for a in arrs]


def _pallas_with_side(body, side, first, last, n_in, n_out, *, in_specs, out_specs, out_shape, scratch_shapes=(),
                      args, **kwargs):
    side_arrs = [] if side is None else list(side[1])
    ns = len(side_arrs)

    def wrapped(*refs):
        own_in, side_in = refs[:n_in], refs[n_in:n_in + ns]
        o = n_in + ns
        own_out, side_out = refs[o:o + n_out], refs[o + n_out:o + n_out + ns]
        scratch = refs[o + n_out + ns:]
        own_scratch, sems = (scratch[:-3], scratch[-3:]) if ns else (scratch, ())
        if ns:
            @pl.when(first())
            def _():
                for cp in _direct_copies(side[0], side_in, side_out, *sems):
                    cp.start()

        body(*own_in, *own_out, *own_scratch)
        if ns:
            @pl.when(last())
            def _():
                for cp in _direct_copies(side[0], side_in, side_out, *sems):
                    cp.wait()

    any_spec = pl.BlockSpec(memory_space=pl.ANY)
    res = pl.pallas_call(
        wrapped,
        in_specs=list(in_specs) + [any_spec] * ns,
        out_specs=list(out_specs) + [any_spec] * ns,
        out_shape=list(out_shape) + (_side_out_shapes(side[0], side_arrs) if ns else []),
        scratch_shapes=list(scratch_shapes) + (_side_scratch(ns) if ns else []),
        **kwargs,
    )(*args, *side_arrs)
    return res[:n_out], res[n_out:]


def _exchange_small(small, name):
    ns = small.shape[1]

    def body(small_ref, sum_ref, small_all, send_sems, recv_sems, local_sems):
        copies = _direct_copies(False, [small_ref], [small_all], send_sems, recv_sems, local_sems)
        for cp in copies:
            cp.start()
        for cp in copies:
            cp.wait()
        total = small_all[0]
        for i in range(1, N_DEV):
            total = total + small_all[i]
        sum_ref[...] = total

    vmem_spec = pl.BlockSpec(memory_space=pltpu.VMEM)
    return pl.pallas_call(
        body, name=name,
        out_shape=jax.ShapeDtypeStruct((1, ns), F32),
        in_specs=[vmem_spec], out_specs=vmem_spec,
        scratch_shapes=[pltpu.VMEM((N_DEV, 1, ns), F32)] + _side_scratch(1),
    )(small)


def _full(shape):
    return pl.BlockSpec(shape, lambda *_: (0,) * len(shape))


def _inproj_fwd(x, nw, w, name, side=None):
    t = x.shape[0]
    tm = 256

    def body(x_ref, nw_ref, w_ref, proj_ref, h_ref):
        h = _rmsnorm(x_ref[...], nw_ref[...]).astype(BF16)
        h_ref[...] = h
        for j in range(N_COL_TILES):
            sl = slice(j * COL_TILE, (j + 1) * COL_TILE)
            proj_ref[:, sl] = jnp.dot(h, w_ref[:, sl], preferred_element_type=F32)

    grid = (t // tm,)
    return _pallas_with_side(
        body, side, *_grid_ends(grid), 3, 2, name=name, grid=grid,
        out_shape=[jax.ShapeDtypeStruct((t, PROJ_W), F32), jax.ShapeDtypeStruct((t, D_MODEL), BF16)],
        in_specs=[pl.BlockSpec((tm, D_MODEL), lambda i: (i, 0)), _full((1, D_MODEL)), _full((D_MODEL, PROJ_W))],
        out_specs=[pl.BlockSpec((tm, PROJ_W), lambda i: (i, 0)), pl.BlockSpec((tm, D_MODEL), lambda i: (i, 0))],
        compiler_params=pltpu.CompilerParams(dimension_semantics=("arbitrary",), vmem_limit_bytes=VMEM_LIMIT),
        args=(x, nw, w))


def _param_specs():
    return [_full((4, 1536)), _full((1, 1536)), _full((1, 128)), _full((1, 128)), _full((1, 128)),
            _full((1, 1024)), _full((1, 128)), _full((CONF_KERNEL, 512)), _full((1, 512)), _full((1, 512)),
            _full((1, 512))]


def _halo_specs(nc, chunk_of):
    def prev_chunk(b, j):
        return jnp.maximum(b * nc + chunk_of(j) - 1, 0)

    per_xbc = CHUNK // XBC_HALO
    per_cc = CHUNK // CONF_HALO
    return [
        pl.BlockSpec((XBC_HALO, 1536), lambda b, j: (prev_chunk(b, j) * per_xbc + per_xbc - 1, C_XBC // 1536)),
        pl.BlockSpec((CHUNK, 256), lambda b, j: (prev_chunk(b, j), C_K // 256)),
        pl.BlockSpec((CONF_HALO, 1024), lambda b, j: (prev_chunk(b, j) * per_cc + per_cc - 1, C_CA // 1024)),
    ]


def _grid_ends(grid):
    first = lambda: functools.reduce(lambda p, q: p & q, [pl.program_id(i) == 0 for i in range(len(grid))])
    last = lambda: functools.reduce(lambda p, q: p & q, [pl.program_id(i) == n - 1 for i, n in enumerate(grid)])
    return first, last


def _mixer_fwd(x, proj, w_out, params, nb, name, side=None):
    t = x.shape[0]
    nc = t // nb // CHUNK

    def body(x_ref, cur_ref, pxbc_ref, pkv_ref, pcc_ref, wo_ref, *rest):
        prm = [r[...] for r in rest[:11]]
        xn_ref, sall_ref, s_scr = rest[11:]
        c = pl.program_id(1)
        not_first = c > 0
        nf = not_first.astype(F32)

        @pl.when(c == 0)
        def _():
            s_scr[...] = jnp.zeros_like(s_scr)

        cw, cb, dtb, alog, dsk, nw, snk, dww, dwb, lnw, lnb = prm
        s_in = s_scr[...]
        sall_ref[0] = s_in
        y_ssd, s_out = _ssd_part(cur_ref[:, 0:1024], cur_ref[:, C_XBC:C_XBC + 1536], cur_ref[:, C_DT:C_DT + 128],
                                 pxbc_ref[...] * nf, s_in, cw, cb, dtb, alog, dsk, nw)
        s_scr[...] = s_out
        y_attn = _attn_part(cur_ref[:, 1024:1536], cur_ref[:, C_Q:C_Q + 512], cur_ref[:, C_K:C_K + 256],
                            pkv_ref[...] * nf, snk, _kv_mask(CHUNK, not_first, ATTN_HEADS // 2))
        y_conf = _conf_part(cur_ref[:, 1536:2048], cur_ref[:, C_CA:C_CA + 1024], pcc_ref[...] * nf,
                            dww, dwb, lnw, lnb)
        xn_ref[...] = (x_ref[...] + _dot(y_ssd, wo_ref[0:1024, :]) + _dot(y_attn, wo_ref[1024:1536, :])
                       + _dot(y_conf, wo_ref[1536:2048, :]))

    row = lambda b, j: (b * nc + j, 0)
    grid = (nb, nc)
    return _pallas_with_side(
        body, side, *_grid_ends(grid), 17, 2, name=name, grid=grid,
        out_shape=[jax.ShapeDtypeStruct((t, D_MODEL), F32),
                   jax.ShapeDtypeStruct((nb * nc, SSD_HEADS * SSD_HEAD_DIM, SSD_STATE), F32)],
        in_specs=[pl.BlockSpec((CHUNK, D_MODEL), row), pl.BlockSpec((CHUNK, PROJ_W), row)]
                 + _halo_specs(nc, lambda j: j) + [_full((MIX_WIDTH, D_MODEL))] + _param_specs(),
        out_specs=[pl.BlockSpec((CHUNK, D_MODEL), row),
                   pl.BlockSpec((1, SSD_HEADS * SSD_HEAD_DIM, SSD_STATE), lambda b, j: (b * nc + j, 0, 0))],
        scratch_shapes=[pltpu.VMEM((SSD_HEADS * SSD_HEAD_DIM, SSD_STATE), F32)],
        compiler_params=pltpu.CompilerParams(dimension_semantics=("arbitrary", "arbitrary"),
                                             vmem_limit_bytes=VMEM_LIMIT),
        args=(x, proj, proj, proj, proj, w_out, *params))


def _mixer_bwd(dxn, proj, s_all, w_out, params, nb, name, side=None):
    t = dxn.shape[0]
    nc = t // nb // CHUNK
    n_prm = 11

    def body(dxn_ref, cur_ref, pxbc_ref, pkv_ref, pcc_ref, s_ref, wo_ref, *rest):
        prm = [r[...] for r in rest[:n_prm]]
        dproj_ref, gwo_ref = rest[n_prm:n_prm + 2]
        gprm = rest[n_prm + 2:2 * n_prm + 2]
        ds_scr, pend_xbc, pend_kv, pend_cc = rest[2 * n_prm + 2:]
        b, j = pl.program_id(0), pl.program_id(1)
        c = nc - 1 - j
        not_first = c > 0
        nf = not_first.astype(F32)

        @pl.when((b == 0) & (j == 0))
        def _():
            gwo_ref[...] = jnp.zeros_like(gwo_ref)
            for r in gprm:
                r[...] = jnp.zeros_like(r)

        @pl.when(j == 0)
        def _():
            ds_scr[...] = jnp.zeros_like(ds_scr)
            pend_xbc[...] = jnp.zeros_like(pend_xbc)
            pend_kv[...] = jnp.zeros_like(pend_kv)
            pend_cc[...] = jnp.zeros_like(pend_cc)

        cw, cb, dtb, alog, dsk, nw, snk, dww, dwb, lnw, lnb = prm
        g_cw, g_cb, g_dtb, g_alog, g_dsk, g_nw, g_snk, g_dww, g_dwb, g_lnw, g_lnb = gprm
        dxn_v = dxn_ref[...]

        def add_tail(d_cur, pending):
            lead = jnp.zeros((CHUNK - pending.shape[0], pending.shape[1]), F32)
            return d_cur + jnp.concatenate([lead, pending], axis=0)

        y, vjp = jax.vjp(_conf_part, cur_ref[:, 1536:2048], cur_ref[:, C_CA:C_CA + 1024], pcc_ref[...] * nf,
                         dww, dwb, lnw, lnb)
        gwo_ref[1536:2048, :] += _dot_tn(y, dxn_v)
        dz, dcacg, dpcc, d_dww, d_dwb, d_lnw, d_lnb = vjp(_dot_nt(dxn_v, wo_ref[1536:2048, :]))
        dproj_ref[:, 1536:2048] = dz.astype(BF16)
        dproj_ref[:, C_CA:C_CA + 1024] = add_tail(dcacg, pend_cc[...]).astype(BF16)
        pend_cc[...] = dpcc
        for r, g in ((g_dww, d_dww), (g_dwb, d_dwb), (g_lnw, d_lnw), (g_lnb, d_lnb)):
            r[...] += g

        attn = functools.partial(_attn_part, kvmask=_kv_mask(CHUNK, not_first, ATTN_HEADS // 2))
        y, vjp = jax.vjp(attn, cur_ref[:, 1024:1536], cur_ref[:, C_Q:C_Q + 512], cur_ref[:, C_K:C_K + 256],
                         pkv_ref[...] * nf, snk)
        gwo_ref[1024:1536, :] += _dot_tn(y, dxn_v)
        dz, dq, dkv, dpkv, d_snk = vjp(_dot_nt(dxn_v, wo_ref[1024:1536, :]))
        dproj_ref[:, 1024:1536] = dz.astype(BF16)
        dproj_ref[:, C_Q:C_Q + 512] = dq.astype(BF16)
        dproj_ref[:, C_K:C_K + 256] = (dkv + pend_kv[...]).astype(BF16)
        pend_kv[...] = dpkv
        g_snk[...] += d_snk

        (y, _), vjp = jax.vjp(_ssd_part, cur_ref[:, 0:1024], cur_ref[:, C_XBC:C_XBC + 1536],
                              cur_ref[:, C_DT:C_DT + 128], pxbc_ref[...] * nf, s_ref[0], cw, cb, dtb, alog, dsk, nw)
        gwo_ref[0:1024, :] += _dot_tn(y, dxn_v)
        dz, dxbc, ddtr, dpxbc, ds_in, d_cw, d_cb, d_dtb, d_alog, d_dsk, d_nw = vjp(
            (_dot_nt(dxn_v, wo_ref[0:1024, :]), ds_scr[...]))
        dproj_ref[:, 0:1024] = dz.astype(BF16)
        dproj_ref[:, C_XBC:C_XBC + 1536] = add_tail(dxbc, pend_xbc[...]).astype(BF16)
        dproj_ref[:, C_DT:C_DT + 128] = ddtr.astype(BF16)
        dproj_ref[:, C_DT + 128:PROJ_W] = jnp.zeros((CHUNK, PROJ_W - C_DT - 128), BF16)
        pend_xbc[...] = dpxbc
        ds_scr[...] = ds_in
        for r, g in ((g_cw, d_cw), (g_cb, d_cb), (g_dtb, d_dtb), (g_alog, d_alog), (g_dsk, d_dsk), (g_nw, d_nw)):
            r[...] += g

    row = lambda b, j: (b * nc + nc - 1 - j, 0)
    prm_shapes = [(4, 1536), (1, 1536), (1, 128), (1, 128), (1, 128), (1, 1024), (1, 128), (CONF_KERNEL, 512),
                  (1, 512), (1, 512), (1, 512)]
    grid = (nb, nc)
    return _pallas_with_side(
        body, side, *_grid_ends(grid), 7 + n_prm, 2 + n_prm, name=name, grid=grid,
        out_shape=[jax.ShapeDtypeStruct((t, PROJ_W), BF16), jax.ShapeDtypeStruct((MIX_WIDTH, D_MODEL), F32)]
                  + [jax.ShapeDtypeStruct(s, F32) for s in prm_shapes],
        in_specs=[pl.BlockSpec((CHUNK, D_MODEL), row), pl.BlockSpec((CHUNK, PROJ_W), row)]
                 + _halo_specs(nc, lambda j: nc - 1 - j)
                 + [pl.BlockSpec((1, SSD_HEADS * SSD_HEAD_DIM, SSD_STATE), lambda b, j: (b * nc + nc - 1 - j, 0, 0)),
                    _full((MIX_WIDTH, D_MODEL))] + _param_specs(),
        out_specs=[pl.BlockSpec((CHUNK, PROJ_W), row), _full((MIX_WIDTH, D_MODEL))] + [_full(s) for s in prm_shapes],
        scratch_shapes=[pltpu.VMEM((SSD_HEADS * SSD_HEAD_DIM, SSD_STATE), F32), pltpu.VMEM((XBC_HALO, 1536), F32),
                        pltpu.VMEM((CHUNK, 256), F32), pltpu.VMEM((CONF_HALO, 1024), F32)],
        compiler_params=pltpu.CompilerParams(dimension_semantics=("arbitrary", "arbitrary"),
                                             vmem_limit_bytes=VMEM_LIMIT),
        args=(dxn, proj, proj, proj, proj, s_all, w_out, *params))


def _inproj_bwd_x(dproj, w, x, nw, dxn, name, side=None):
    t = x.shape[0]
    tm = 256

    def body(dp_ref, w_ref, x_ref, nw_ref, dxn_ref, dx_ref, gnw_ref):
        @pl.when(pl.program_id(0) == 0)
        def _():
            gnw_ref[...] = jnp.zeros_like(gnw_ref)

        dh = jnp.zeros((tm, D_MODEL), F32)
        for j in range(N_COL_TILES):
            sl = slice(j * COL_TILE, (j + 1) * COL_TILE)
            dh = dh + _dot_nt(dp_ref[:, sl], w_ref[:, sl])
        _, vjp = jax.vjp(_rmsnorm, x_ref[...], nw_ref[...])
        dx, dnw = vjp(dh)
        dx_ref[...] = dxn_ref[...] + dx
        gnw_ref[...] += dnw

    tok = lambda i: (i, 0)
    grid = (t // tm,)
    return _pallas_with_side(
        body, side, *_grid_ends(grid), 5, 2, name=name, grid=grid,
        out_shape=[jax.ShapeDtypeStruct((t, D_MODEL), F32), jax.ShapeDtypeStruct((1, D_MODEL), F32)],
        in_specs=[pl.BlockSpec((tm, PROJ_W), tok), _full((D_MODEL, PROJ_W)), pl.BlockSpec((tm, D_MODEL), tok),
                  _full((1, D_MODEL)), pl.BlockSpec((tm, D_MODEL), tok)],
        out_specs=[pl.BlockSpec((tm, D_MODEL), tok), _full((1, D_MODEL))],
        compiler_params=pltpu.CompilerParams(dimension_semantics=("arbitrary",), vmem_limit_bytes=VMEM_LIMIT),
        args=(dproj, w, x, nw, dxn))


def _inproj_bwd_w(h, dproj, name, side=None):
    t = h.shape[0]
    tk = 512

    def body(h_ref, dp_ref, gw_ref):
        @pl.when(pl.program_id(1) == 0)
        def _():
            gw_ref[...] = jnp.zeros_like(gw_ref)

        gw_ref[...] += _dot_tn(h_ref[...], dp_ref[...])

    grid = (N_COL_TILES, t // tk)
    return _pallas_with_side(
        body, side, *_grid_ends(grid), 2, 1, name=name, grid=grid,
        out_shape=[jax.ShapeDtypeStruct((D_MODEL, PROJ_W), F32)],
        in_specs=[pl.BlockSpec((tk, D_MODEL), lambda n, k: (k, 0)), pl.BlockSpec((tk, COL_TILE), lambda n, k: (k, n))],
        out_specs=[pl.BlockSpec((D_MODEL, COL_TILE), lambda n, k: (0, n))],
        compiler_params=pltpu.CompilerParams(dimension_semantics=("arbitrary", "arbitrary"),
                                             vmem_limit_bytes=VMEM_LIMIT),
        args=(h, dproj))


def _repack_runs():
    pieces = ((0, 2048, C_Z), (2048, 3584, C_XBC), (3584, 3600, C_DT), (3600, 4368, C_Q), (4368, D_IN_PROJ, C_CA))
    per = D_IN_PROJ // N_DEV
    runs = []
    for j in range(N_DEV):
        lo, hi = per * j, per * (j + 1)
        for a, b, dst in pieces:
            s, e = max(lo, a), min(hi, b)
            if s < e:
                runs.append((j, s - lo, e - lo, dst + s - a))
    return runs


def _repack_w_in(g, name):
    tr = 256

    def body(g_ref, o_ref):
        for j, a, b, dst in _repack_runs():
            o_ref[:, dst:dst + b - a] = g_ref[j, :, a:b]
        o_ref[:, C_DT + 16:PROJ_W] = jnp.zeros((tr, PROJ_W - C_DT - 16), g.dtype)

    return pl.pallas_call(
        body, name=name, grid=(D_MODEL // tr,),
        out_shape=jax.ShapeDtypeStruct((D_MODEL, PROJ_W), g.dtype),
        in_specs=[pl.BlockSpec((N_DEV, tr, D_IN_PROJ // N_DEV), lambda i: (0, i, 0))],
        out_specs=pl.BlockSpec((tr, PROJ_W), lambda i: (i, 0)),
        compiler_params=pltpu.CompilerParams(dimension_semantics=("arbitrary",)),
    )(g)


def _unpack_gw_in(g, name):
    tr = 256

    def body(g_ref, o_ref):
        for j, a, b, dst in _repack_runs():
            o_ref[j, :, a:b] = g_ref[:, dst:dst + b - a].astype(BF16)

    return pl.pallas_call(
        body, name=name, grid=(D_MODEL // tr,),
        out_shape=jax.ShapeDtypeStruct((N_DEV, D_MODEL, D_IN_PROJ // N_DEV), BF16),
        in_specs=[pl.BlockSpec((tr, PROJ_W), lambda i: (i, 0))],
        out_specs=pl.BlockSpec((N_DEV, tr, D_IN_PROJ // N_DEV), lambda i: (0, i, 0)),
        compiler_params=pltpu.CompilerParams(dimension_semantics=("arbitrary",)),
    )(g)


def _loss_head(x, fnw, target, name):
    t = x.shape[0]
    tm = 512

    def body(x_ref, w_ref, t_ref, dx_ref, loss_ref, gw_ref):
        @pl.when(pl.program_id(0) == 0)
        def _():
            loss_ref[...] = jnp.zeros_like(loss_ref)
            gw_ref[...] = jnp.zeros_like(gw_ref)

        y, vjp = jax.vjp(_rmsnorm, x_ref[...], w_ref[...])
        err = y - t_ref[...]
        loss_ref[...] += 0.5 * jnp.sum(jnp.mean(err * err, axis=-1, keepdims=True), axis=0, keepdims=True)
        dx, dw = vjp(err * (1.0 / D_MODEL))
        dx_ref[...] = dx
        gw_ref[...] += dw

    tok = lambda i: (i, 0)
    return pl.pallas_call(
        body, name=name, grid=(t // tm,),
        out_shape=[jax.ShapeDtypeStruct((t, D_MODEL), F32), jax.ShapeDtypeStruct((1, 1), F32),
                   jax.ShapeDtypeStruct((1, D_MODEL), F32)],
        in_specs=[pl.BlockSpec((tm, D_MODEL), tok), _full((1, D_MODEL)), pl.BlockSpec((tm, D_MODEL), tok)],
        out_specs=[pl.BlockSpec((tm, D_MODEL), tok), _full((1, 1)), _full((1, D_MODEL))],
        compiler_params=pltpu.CompilerParams(dimension_semantics=("arbitrary",)),
    )(x, fnw, target)


def _adamw(w, g, m, v):
    m = ADAM_B1 * m + (1.0 - ADAM_B1) * g
    v = ADAM_B2 * v + (1.0 - ADAM_B2) * jnp.square(g)
    m_hat = m / (1.0 - ADAM_B1 ** ADAM_STEP)
    v_hat = v / (1.0 - ADAM_B2 ** ADAM_STEP)
    delta = -ADAM_LR * (m_hat / (jnp.sqrt(v_hat) + ADAM_EPS) + ADAM_WD * w)
    return delta, m, v


def _reduce_adamw(parts, w, m, v, tr, name):
    depth = len(parts)
    p, r, cdim = parts[0].shape
    n_blk = r // tr

    def body(*refs):
        p_refs = refs[:depth]
        w_ref, m_ref, v_ref, g_ref, d_ref, nm_ref, nv_ref = refs[depth:]
        for layer in range(depth):
            @pl.when(pl.program_id(0) == layer)
            def _(p_ref=p_refs[layer]):
                g = p_ref[0].astype(F32)
                for i in range(1, p):
                    g = g + p_ref[i].astype(F32)
                g_ref[0] = g
                d_ref[0], nm_ref[0], nv_ref[0] = _adamw(w_ref[0], g, m_ref[0], v_ref[0])

    def parts_spec(layer):
        return pl.BlockSpec((p, tr, cdim), lambda d, i: (0, jnp.clip(i + (d - layer) * n_blk, 0, n_blk - 1), 0))

    blk = pl.BlockSpec((1, tr, cdim), lambda d, i: (d, i, 0))
    return pl.pallas_call(
        body, name=name, grid=(depth, n_blk),
        out_shape=[jax.ShapeDtypeStruct(w.shape, F32)] * 4,
        in_specs=[parts_spec(layer) for layer in range(depth)] + [blk, blk, blk],
        out_specs=[blk] * 4,
        compiler_params=pltpu.CompilerParams(dimension_semantics=("arbitrary", "arbitrary"),
                                             vmem_limit_bytes=VMEM_LIMIT),
    )(*parts, w, m, v)


def _adamw_small(ssum, entries, name):
    direct = [e[3] for e in entries if not isinstance(e[3], list)]
    n_direct = len(direct)

    def body(*refs):
        ssum_ref, direct_refs = refs[0], list(refs[1:1 + n_direct])
        ins = refs[1 + n_direct:1 + n_direct + 3 * len(entries)]
        outs = refs[1 + n_direct + 3 * len(entries):]
        for k, (w, _, _, grad) in enumerate(entries):
            w_ref, m_ref, v_ref = ins[3 * k:3 * k + 3]
            g_ref, d_ref, nm_ref, nv_ref = outs[4 * k:4 * k + 4]
            if isinstance(grad, list):
                for row, off in enumerate(grad):
                    rows = slice(row, row + 1)
                    g = ssum_ref[:, off:off + w.shape[1]]
                    g_ref[rows, :] = g
                    d_ref[rows, :], nm_ref[rows, :], nv_ref[rows, :] = _adamw(w_ref[rows, :], g, m_ref[rows, :],
                                                                              v_ref[rows, :])
            else:
                g = direct_refs.pop(0)[...]
                g_ref[...] = g
                d_ref[...], nm_ref[...], nv_ref[...] = _adamw(w_ref[...], g, m_ref[...], v_ref[...])

    vmem = pl.BlockSpec(memory_space=pltpu.VMEM)
    args = [ssum] + direct + [a for e in entries for a in e[:3]]
    res = pl.pallas_call(
        body, name=name,
        out_shape=[jax.ShapeDtypeStruct(e[0].shape, F32) for e in entries for _ in range(4)],
        in_specs=[vmem] * len(args), out_specs=[vmem] * (4 * len(entries)),
    )(*args)
    return [res[4 * k:4 * k + 4] for k in range(len(entries))]


def _reduce_adamw_cols(parts, w, m, v, name):
    depth = len(parts)
    p, r, cdim = parts[0].shape
    tc = 256

    def body(*refs):
        p_refs = refs[:depth]
        w_ref, m_ref, v_ref, g_ref, d_ref, nm_ref, nv_ref = refs[depth:]
        for layer in range(depth):
            g = p_refs[layer][0].astype(F32)
            for i in range(1, p):
                g = g + p_refs[layer][i].astype(F32)
            g = g.T
            g_ref[:, layer, :] = g
            d_ref[:, layer, :], nm_ref[:, layer, :], nv_ref[:, layer, :] = _adamw(
                w_ref[:, layer, :], g, m_ref[:, layer, :], v_ref[:, layer, :])

    view = lambda a: jnp.transpose(a, (2, 0, 1))
    blk = pl.BlockSpec((cdim, depth, tc), lambda i: (0, 0, i))
    outs = pl.pallas_call(
        body, name=name, grid=(r // tc,),
        out_shape=[jax.ShapeDtypeStruct((cdim, depth, r), F32)] * 4,
        in_specs=[pl.BlockSpec((p, tc, cdim), lambda i: (0, i, 0))] * depth + [blk, blk, blk],
        out_specs=[blk] * 4,
        compiler_params=pltpu.CompilerParams(dimension_semantics=("arbitrary",), vmem_limit_bytes=VMEM_LIMIT),
    )(*parts, view(w), view(m), view(v))
    return [jnp.transpose(o, (1, 2, 0)) for o in outs]


def _pad_lanes(v, width=128):
    return jnp.pad(v.reshape(1, -1), ((0, 0), (0, width - v.shape[-1])))


SMALL_FIELDS = (("norm_w", 1024), ("conv_b", 1536), ("dt_bias", 128), ("a_log", 128), ("d_skip", 128),
                ("ssd_norm_w", 1024), ("sinks", 128), ("dw_b", 512), ("ln_w", 512), ("ln_b", 512))


def kernel(x, norm_w, w_in, ssd_conv_w, ssd_conv_b, ssd_dt_bias, ssd_a_log, ssd_d, ssd_norm_w, attn_sinks, conf_dw_w, conf_dw_b, conf_ln_w, conf_ln_b, w_out, final_norm_w, loss_target, m_norm_w, m_w_in, m_ssd_conv_w, m_ssd_conv_b, m_ssd_dt_bias, m_ssd_a_log, m_ssd_d, m_ssd_norm_w, m_attn_sinks, m_conf_dw_w, m_conf_dw_b, m_conf_ln_w, m_conf_ln_b, m_w_out, m_final_norm_w, v_norm_w, v_w_in, v_ssd_conv_w, v_ssd_conv_b, v_ssd_dt_bias, v_ssd_a_log, v_ssd_d, v_ssd_norm_w, v_attn_sinks, v_conf_dw_w, v_conf_dw_b, v_conf_ln_w, v_conf_ln_b, v_w_out, v_final_norm_w):
    nb, seq, _ = x.shape
    depth = norm_w.shape[0]
    t = nb * seq
    me_idx = 4 * lax.axis_index("x") + 2 * lax.axis_index("y") + lax.axis_index("c")

    w_in_bf, w_out_bf = w_in.astype(BF16), w_out.astype(BF16)
    g_win0, g_cw, g_dw = _all_gather([w_in_bf[0], ssd_conv_w, conf_dw_w], "gather_weights")
    w_in_full = [_repack_w_in(g_win0, "repack_w_in_0")]
    w_out_full = []
    conv_w_full = [jnp.transpose(g_cw[:, l], (1, 0, 2)).reshape(4, 1536) for l in range(depth)]
    dw_w_full = [jnp.transpose(g_dw[:, l], (1, 0, 2)).reshape(CONF_KERNEL, 512) for l in range(depth)]

    def layer_params(l):
        return [conv_w_full[l], ssd_conv_b[l].reshape(1, -1), _pad_lanes(ssd_dt_bias[l]), _pad_lanes(ssd_a_log[l]),
                _pad_lanes(ssd_d[l]), ssd_norm_w[l].reshape(1, -1), _pad_lanes(attn_sinks[l]), dw_w_full[l],
                conf_dw_b[l].reshape(1, -1), conf_ln_w[l].reshape(1, -1), conf_ln_b[l].reshape(1, -1)]

    xs = [x.reshape(t, D_MODEL)]
    saved = []
    for l in range(depth):
        (proj, h), gathered = _inproj_fwd(xs[l], norm_w[l].reshape(1, -1), w_in_full[l], f"inproj_fwd_{l}",
                                          (False, [w_out_bf[0]]) if l == 0 else None)
        if l == 0:
            w_out_full.append(gathered[0].reshape(MIX_WIDTH, D_MODEL))
        side = (False, [w_in_bf[l + 1], w_out_bf[l + 1]]) if l + 1 < depth else None
        (x_next, s_all), gathered = _mixer_fwd(xs[l], proj, w_out_full[l], layer_params(l), nb, f"mixer_fwd_{l}",
                                               side)
        if side:
            w_in_full.append(_repack_w_in(gathered[0], f"repack_w_in_{l + 1}"))
            w_out_full.append(gathered[1].reshape(MIX_WIDTH, D_MODEL))
        saved.append((proj, h, s_all))
        xs.append(x_next)
    dx, loss_part, g_fnw = _loss_head(xs[depth], final_norm_w.reshape(1, -1), loss_target.reshape(t, D_MODEL),
                                      "loss_head")

    cols_in = D_IN_PROJ // N_DEV
    rows_out = MIX_WIDTH // N_DEV
    small_rows = [None] * depth
    received = [None] * depth
    outgoing = None
    for l in reversed(range(depth)):
        proj, h, s_all = saved[l]
        res, arrived = _mixer_bwd(dx, proj, s_all, w_out_full[l], layer_params(l), nb, f"mixer_bwd_{l}",
                                  (True, outgoing) if outgoing else None)
        if outgoing:
            received[l + 1] = arrived
        dproj, gw_out = res[0], res[1]
        g_cw_l, g_cb, g_dtb, g_alog, g_dsk, g_nw, g_snk, g_dww, g_dwb, g_lnw, g_lnb = res[2:]
        gw_out_parts = gw_out.reshape(N_DEV, rows_out, D_MODEL).astype(BF16)
        nw_l = norm_w[l].reshape(1, -1)
        if l > 0:
            (gw_in,), _ = _inproj_bwd_w(h, dproj, f"inproj_bwd_w_{l}")
            (dx, g_norm), _ = _inproj_bwd_x(dproj, w_in_full[l], xs[l], nw_l, dx, f"inproj_bwd_x_{l}")
            outgoing = [_unpack_gw_in(gw_in, f"unpack_gw_in_{l}"), gw_out_parts]
        else:
            (gw_in,), got_out = _inproj_bwd_w(h, dproj, f"inproj_bwd_w_{l}", (True, [gw_out_parts]))
            gw_in_parts = _unpack_gw_in(gw_in, f"unpack_gw_in_{l}")
            (dx, g_norm), got_in = _inproj_bwd_x(dproj, w_in_full[l], xs[l], nw_l, dx, f"inproj_bwd_x_{l}",
                                                 (True, [gw_in_parts]))
            received[l] = [got_in[0], got_out[0]]
        small_rows[l] = [g_norm, g_cb, g_dtb, g_alog, g_dsk, g_nw, g_snk, g_dwb, g_lnw, g_lnb,
                         g_cw_l.reshape(1, -1), g_dww.reshape(1, -1)]
    grad_x = dx.reshape(nb, seq, D_MODEL)

    small = jnp.concatenate([piece for l in range(depth) for piece in small_rows[l]] + [g_fnw], axis=1)
    ssum = _exchange_small(small, "exchange_small")

    g_w_in, d_w_in, nm_w_in, nv_w_in = _reduce_adamw_cols([received[l][0] for l in range(depth)], w_in, m_w_in,
                                                          v_w_in, "adamw_w_in")
    g_w_out, d_w_out, nm_w_out, nv_w_out = _reduce_adamw([received[l][1] for l in range(depth)], w_out, m_w_out,
                                                         v_w_out, 256, "adamw_w_out")

    per_layer = sum(n for _, n in SMALL_FIELDS) + 4 * 1536 + CONF_KERNEL * 512
    given = {"norm_w": (norm_w, m_norm_w, v_norm_w), "conv_b": (ssd_conv_b, m_ssd_conv_b, v_ssd_conv_b),
             "dt_bias": (ssd_dt_bias, m_ssd_dt_bias, v_ssd_dt_bias), "a_log": (ssd_a_log, m_ssd_a_log, v_ssd_a_log),
             "d_skip": (ssd_d, m_ssd_d, v_ssd_d), "ssd_norm_w": (ssd_norm_w, m_ssd_norm_w, v_ssd_norm_w),
             "sinks": (attn_sinks, m_attn_sinks, v_attn_sinks), "dw_b": (conf_dw_b, m_conf_dw_b, v_conf_dw_b),
             "ln_w": (conf_ln_w, m_conf_ln_w, v_conf_ln_w), "ln_b": (conf_ln_b, m_conf_ln_b, v_conf_ln_b)}
    entries = []
    off = 0
    for fname, n in SMALL_FIELDS:
        entries.append((*given[fname], [l * per_layer + off for l in range(depth)]))
        off += n
    shard_grads = []
    for width, shard, kk in ((1536, 192, 4), (512, 64, CONF_KERNEL)):
        full = [ssum[:, l * per_layer + off:l * per_layer + off + kk * width].reshape(kk, width) for l in range(depth)]
        shard_grads.append(jnp.stack([lax.dynamic_slice(f, (0, me_idx * shard), (kk, shard)) for f in full], axis=0))
        off += kk * width
    entries.append((ssd_conv_w, m_ssd_conv_w, v_ssd_conv_w, shard_grads[0]))
    entries.append((conf_dw_w, m_conf_dw_w, v_conf_dw_w, shard_grads[1]))
    entries.append((final_norm_w.reshape(1, -1), m_final_norm_w.reshape(1, -1), v_final_norm_w.reshape(1, -1),
                    [depth * per_layer]))
    sm = _adamw_small(ssum, entries, "adamw_small")
    sm = {k: quad for k, quad in zip([f for f, _ in SMALL_FIELDS] + ["conv_w", "dw_w", "final"], sm)}

    def outputs(i, big_in_i, big_out_i):
        return [sm["norm_w"][i], big_in_i, sm["conv_w"][i], sm["conv_b"][i], sm["dt_bias"][i], sm["a_log"][i],
                sm["d_skip"][i], sm["ssd_norm_w"][i], sm["sinks"][i], sm["dw_w"][i], sm["dw_b"][i], sm["ln_w"][i],
                sm["ln_b"][i], big_out_i, sm["final"][i].reshape(-1)]

    loss = lax.psum(loss_part[0, 0], MESH_AXES)
    return (loss, grad_x, *outputs(0, g_w_in, g_w_out), *outputs(1, d_w_in, d_w_out),
            *outputs(2, nm_w_in, nm_w_out), *outputs(3, nv_w_in, nv_w_out))
```

```python
import functools

import jax
import jax.numpy as jnp
from jax import lax
from jax.experimental import pallas as pl
from jax.experimental.pallas import tpu as pltpu

F32 = jnp.float32
BF16 = jnp.bfloat16
MESH_AXES = ("x", "y", "c")
N_DEV = 8
EPS = 1e-5

D_MODEL = 1024
CHUNK = 128
SSD_HEADS = 16
SSD_HEAD_DIM = 64
SSD_STATE = 128
ATTN_HEADS = 8
ATTN_HEAD_DIM = 64
CONF_KERNEL = 31
MIX_WIDTH = 2048
D_IN_PROJ = 5392
C_Z = 0
C_CA = 2048
C_XBC = 3072
C_Q = 4608
C_K = 5120
C_V = 5248
C_DT = 5376
PROJ_W = 5632
N_COL_TILES = 4
COL_TILE = PROJ_W // N_COL_TILES
XBC_HALO = 8
CONF_HALO = 32
VMEM_LIMIT = 56 * 1024 * 1024

ADAM_LR = 0.001
ADAM_B1 = 0.9
ADAM_B2 = 0.999
ADAM_EPS = 1e-08
ADAM_WD = 0.01
ADAM_STEP = 10


def _silu(v):
    return v * jax.nn.sigmoid(v)


def _softplus(v):
    return jnp.maximum(v, 0.0) + jnp.log1p(jnp.exp(-jnp.abs(v)))


def _rmsnorm(v, w):
    return v * lax.rsqrt(jnp.mean(v * v, axis=-1, keepdims=True) + EPS) * w


def _dot(a, b):
    return jnp.dot(a.astype(BF16), b.astype(BF16), preferred_element_type=F32)


def _dot_nt(a, b):
    return lax.dot_general(a.astype(BF16), b.astype(BF16), (((1,), (1,)), ((), ())), preferred_element_type=F32)


def _dot_tn(a, b):
    return lax.dot_general(a.astype(BF16), b.astype(BF16), (((0,), (0,)), ((), ())), preferred_element_type=F32)


def _taps(ext, offs, out_len, w=None, g=None):
    n_rows, n_cols = ext.shape
    by_shift = {}
    for t, off in enumerate(offs):
        by_shift.setdefault(off % 8, []).append((t, off))
    for r, taps in by_shift.items():
        assert max(off for _, off in taps) - r + out_len <= n_rows - r
    accs = []
    sums = [[None] * (n_cols // 128) for _ in offs]
    for blk in range(n_cols // 128):
        cs = slice(blk * 128, (blk + 1) * 128)
        e = ext[:, cs]
        acc = None
        for r, taps in by_shift.items():
            shifted = e if r == 0 else pltpu.roll(e, n_rows - r, axis=0)
            for t, off in taps:
                window = shifted[off - r:off - r + out_len, :]
                if w is not None:
                    term = w[t:t + 1, cs] * window
                    acc = term if acc is None else acc + term
                if g is not None:
                    sums[t][blk] = jnp.sum(g[:, cs] * window, axis=0, keepdims=True)
        accs.append(acc)
    if w is not None:
        return jnp.concatenate(accs, axis=1)
    return jnp.concatenate([jnp.concatenate(row, axis=1) for row in sums], axis=0)


@functools.partial(jax.custom_vjp, nondiff_argnums=(3,))
def _dwconv(ext, w, b, halo):
    kk = w.shape[0]
    return b + _taps(ext, [halo - (kk - 1) + t for t in range(kk)], ext.shape[0] - halo, w=w)


def _dwconv_fwd(ext, w, b, halo):
    return _dwconv(ext, w, b, halo), (ext, w)


def _dwconv_bwd(halo, res, g):
    ext, w = res
    kk = w.shape[0]
    offs = [halo - (kk - 1) + t for t in range(kk)]
    dw = _taps(ext, offs, ext.shape[0] - halo, g=g)
    zeros = jnp.zeros((halo, g.shape[1]), g.dtype)
    gp = jnp.concatenate([zeros, g, zeros], axis=0)
    dext = _taps(gp, [halo - off for off in offs], ext.shape[0], w=w)
    return dext, dw, jnp.sum(g, axis=0, keepdims=True)


_dwconv.defvjp(_dwconv_fwd, _dwconv_bwd)


def _ssd_part(z_ssd, xbc, dtr, p_xbc, s_in, cw, cb, dtb, alog, dsk, nw):
    qn = xbc.shape[0]
    nh = SSD_HEADS
    per_group = nh // 2
    n_pair = nh // 2
    xa = _silu(_dwconv(jnp.concatenate([p_xbc, xbc], axis=0), cw, cb, XBC_HALO))
    xs = xa[:, 0:1024]
    dt = _softplus(dtr + dtb)
    a = dt * (-jnp.exp(alog))
    rows = lax.broadcasted_iota(jnp.int32, (qn, qn), 0)
    cols = lax.broadcasted_iota(jnp.int32, (qn, qn), 1)
    causal = rows >= cols
    low = cols < SSD_HEAD_DIM
    a_cs = jnp.dot(causal.astype(F32), a, precision=lax.Precision.HIGHEST, preferred_element_type=F32)
    a_cs_t = a_cs.T
    bgs = [xa[:, 1024 + g * 128:1024 + (g + 1) * 128] for g in range(2)]
    cgs = [xa[:, 1280 + g * 128:1280 + (g + 1) * 128] for g in range(2)]
    cbms = [_dot_nt(cgs[g], bgs[g]) for g in range(2)]
    colb = [jnp.broadcast_to(a_cs[:, h:h + 1], (qn, qn)) for h in range(nh)]
    lastb = [jnp.broadcast_to(colb[h][qn - 1:qn, :], (qn, qn)) for h in range(nh)]
    dtb_wide = [jnp.broadcast_to(dt[:, h:h + 1], (qn, qn)) for h in range(nh)]
    lmats = [jnp.exp(jnp.where(causal, colb[h] - a_cs_t[h:h + 1, :], -jnp.inf)) for h in range(nh)]
    ms = [cbms[h // per_group] * lmats[h] for h in range(nh)]
    x_pair = [xs[:, p * 128:(p + 1) * 128] for p in range(n_pair)]
    x_lo = [x_pair[p] * jnp.where(low, dtb_wide[2 * p], 0.0) for p in range(n_pair)]
    x_hi = [x_pair[p] * jnp.where(low, 0.0, dtb_wide[2 * p + 1]) for p in range(n_pair)]
    y_diag = [_dot(ms[2 * p], x_lo[p]) + _dot(ms[2 * p + 1], x_hi[p]) for p in range(n_pair)]
    col_pair = [jnp.where(low, colb[2 * p], colb[2 * p + 1]) for p in range(n_pair)]
    last_pair = [jnp.where(low, lastb[2 * p], lastb[2 * p + 1]) for p in range(n_pair)]
    ecol = [jnp.exp(col_pair[p]) for p in range(n_pair)]
    xw = [(x_lo[p] + x_hi[p]) * jnp.exp(last_pair[p] - col_pair[p]) for p in range(n_pair)]
    y_off, st = [], []
    for g in range(2):
        ps = range(g * n_pair // 2, (g + 1) * n_pair // 2)
        y_off.append(_dot_nt(cgs[g], s_in[g * 512:(g + 1) * 512, :]) * jnp.concatenate([ecol[p] for p in ps], axis=1))
        st.append(_dot_tn(jnp.concatenate([xw[p] for p in ps], axis=1), bgs[g]))
    e_last = jnp.exp(jnp.broadcast_to(a_cs_t[:, qn - 1:qn], (qn, SSD_STATE)))
    scale = jnp.concatenate([jnp.broadcast_to(e_last[h:h + 1, :], (64, SSD_STATE)) for h in range(nh)], axis=0)
    s_out = scale * s_in + jnp.concatenate(st, axis=0)
    d_wide = jnp.concatenate([jnp.broadcast_to(dsk[:, h:h + 1], (1, 64)) for h in range(nh)], axis=1)
    y = jnp.concatenate(y_diag, axis=1) + jnp.concatenate(y_off, axis=1) + d_wide * xs
    gated = y * _silu(z_ssd)
    halves = []
    for g in range(2):
        gg = gated[:, g * 512:(g + 1) * 512]
        halves.append(gg * lax.rsqrt(jnp.mean(gg * gg, axis=-1, keepdims=True) + EPS))
    return jnp.concatenate(halves, axis=1) * nw, s_out


def _attn_part(z_attn, q, kv, p_kv, snk, kvmask):
    qn = q.shape[0]
    per_group = ATTN_HEADS // 2
    kk = jnp.concatenate([p_kv[:, 0:128], kv[:, 0:128]], axis=0)
    vv = jnp.concatenate([p_kv[:, 128:256], kv[:, 128:256]], axis=0)
    groups = range(2)
    heads = [range(g * per_group, (g + 1) * per_group) for g in groups]
    qs = [jnp.concatenate([q[:, h * 64:(h + 1) * 64] for h in heads[g]], axis=0) for g in groups]
    sk = [jnp.concatenate([jnp.broadcast_to(snk[:, h:h + 1], (qn, 1)) for h in heads[g]], axis=0) for g in groups]
    s = [jnp.where(kvmask, _dot_nt(qs[g], kk[:, g * 64:(g + 1) * 64]) * (ATTN_HEAD_DIM ** -0.5), -jnp.inf)
         for g in groups]
    m = [lax.stop_gradient(jnp.maximum(jnp.max(s[g], axis=-1, keepdims=True), sk[g])) for g in groups]
    e = [jnp.exp(s[g] - m[g]) for g in groups]
    r_den = [1.0 / (jnp.sum(e[g], axis=-1, keepdims=True) + jnp.exp(sk[g] - m[g])) for g in groups]
    o = [_dot(e[g], vv[:, g * 64:(g + 1) * 64]) * r_den[g] for g in groups]
    outs = [o[g][i * qn:(i + 1) * qn, :] for g in groups for i in range(per_group)]
    return jnp.concatenate(outs, axis=1) * _silu(z_attn)


def _conf_part(z_conf, cacg, p_cc, dww, dwb, lnw, lnb):
    c0 = cacg[:, 0:512] * jax.nn.sigmoid(cacg[:, 512:1024])
    pc0 = p_cc[:, 0:512] * jax.nn.sigmoid(p_cc[:, 512:1024])
    acc = _dwconv(jnp.concatenate([pc0, c0], axis=0), dww, dwb, CONF_HALO)
    xc = acc - jnp.mean(acc, axis=-1, keepdims=True)
    yln = xc * lax.rsqrt(jnp.mean(xc * xc, axis=-1, keepdims=True) + EPS) * lnw + lnb
    return _silu(yln) * _silu(z_conf)


def _kv_mask(qn, not_first, reps):
    ii = lax.broadcasted_iota(jnp.int32, (reps * qn, 2 * qn), 0) & (qn - 1)
    jj = lax.broadcasted_iota(jnp.int32, (reps * qn, 2 * qn), 1)
    d = jj - ii
    return (d >= 1) & (d <= qn) & (not_first | (jj >= qn))


def _my_place():
    return lax.axis_index("x"), lax.axis_index("y"), lax.axis_index("c")


def _all_gather(arrs, name):
    n = len(arrs)

    def body(*refs):
        ins, outs = refs[:n], refs[n:2 * n]
        send_sems, recv_sems, local_sems = refs[2 * n:]
        x, y, c = _my_place()
        me, sibling = (x, y, c), (x, y, 1 - c)
        chips = [(1 - x, y), (x, 1 - y), (1 - x, 1 - y)]

        def slot(a, p):
            return outs[a].at[4 * p[0] + 2 * p[1] + p[2]]

        def copy(a, kk, block, to, src=None):
            return pltpu.make_async_remote_copy(
                src_ref=slot(a, block) if src is None else src, dst_ref=slot(a, block),
                send_sem=send_sems.at[a, kk], recv_sem=recv_sems.at[a, kk],
                device_id=to, device_id_type=pl.DeviceIdType.MESH)

        mine = [pltpu.make_async_copy(ins[a], slot(a, me), local_sems.at[a]) for a in range(n)]
        for cp in mine:
            cp.start()
        first = []
        for a in range(n):
            first.append(copy(a, 0, me, sibling, src=ins[a]))
            first += [copy(a, 1 + j, me, (*chip, c), src=ins[a]) for j, chip in enumerate(chips)]
        for cp in first:
            cp.start()
        passed = []
        for j, chip in enumerate(chips):
            for a in range(n):
                copy(a, 1 + j, (*chip, c), me).wait_recv()
                fwd = copy(a, 4 + j, (*chip, c), sibling)
                fwd.start()
                passed.append(fwd)
        for a in range(n):
            copy(a, 0, sibling, me).wait_recv()
            for j, chip in enumerate(chips):
                copy(a, 4 + j, (*chip, 1 - c), me).wait_recv()
        for cp in first + passed:
            cp.wait_send()
        for cp in mine:
            cp.wait()

    any_spec = pl.BlockSpec(memory_space=pl.ANY)
    return pl.pallas_call(
        body, name=name,
        out_shape=[jax.ShapeDtypeStruct((N_DEV,) + a.shape, a.dtype) for a in arrs],
        in_specs=[any_spec] * n, out_specs=[any_spec] * n,
        scratch_shapes=[pltpu.SemaphoreType.DMA((n, 7)), pltpu.SemaphoreType.DMA((n, 7)),
                        pltpu.SemaphoreType.DMA((n,))],
    )(*arrs)


def _direct_copies(scatter, ins, outs, send_sems, recv_sems, local_sems):
    x, y, c = _my_place()
    me_idx = 4 * x + 2 * y + c
    n = len(ins)
    local = [pltpu.make_async_copy(ins[a].at[me_idx] if scatter else ins[a], outs[a].at[me_idx], local_sems.at[a])
             for a in range(n)]
    remote = []
    for rel in range(1, N_DEV):
        px = 1 - x if rel & 4 else x
        py = 1 - y if rel & 2 else y
        pc = 1 - c if rel & 1 else c
        peer_idx = 4 * px + 2 * py + pc
        for a in range(n):
            remote.append(pltpu.make_async_remote_copy(
                src_ref=ins[a].at[peer_idx] if scatter else ins[a], dst_ref=outs[a].at[me_idx],
                send_sem=send_sems.at[a, rel - 1], recv_sem=recv_sems.at[a, rel - 1],
                device_id=(px, py, pc), device_id_type=pl.DeviceIdType.MESH))
    return local + remote


def _side_scratch(n):
    return [pltpu.SemaphoreType.DMA((n, 7)), pltpu.SemaphoreType.DMA((n, 7)), pltpu.SemaphoreType.DMA((n,))]


def _side_out_shapes(scatter, arrs):
    return [jax.ShapeDtypeStruct(a.shape if scatter else (N_DEV,) + a.shape, a.dtype) for a in arrs]


def _pallas_with_side(body, side, first, last, n_in, n_out, *, in_specs, out_specs, out_shape, scratch_shapes=(),
                      args, **kwargs):
    side_arrs = [] if side is None else list(side[1])
    ns = len(side_arrs)

    def wrapped(*refs):
        own_in, side_in = refs[:n_in], refs[n_in:n_in + ns]
        o = n_in + ns
        own_out, side_out = refs[o:o + n_out], refs[o + n_out:o + n_out + ns]
        scratch = refs[o + n_out + ns:]
        own_scratch, sems = (scratch[:-3], scratch[-3:]) if ns else (scratch, ())
        if ns:
            @pl.when(first())
            def _():
                for cp in _direct_copies(side[0], side_in, side_out, *sems):
                    cp.start()

        body(*own_in, *own_out, *own_scratch)
        if ns:
            @pl.when(last())
            def _():
                for cp in _direct_copies(side[0], side_in, side_out, *sems):
                    cp.wait()

    any_spec = pl.BlockSpec(memory_space=pl.ANY)
    res = pl.pallas_call(
        wrapped,
        in_specs=list(in_specs) + [any_spec] * ns,
        out_specs=list(out_specs) + [any_spec] * ns,
        out_shape=list(out_shape) + (_side_out_shapes(side[0], side_arrs) if ns else []),
        scratch_shapes=list(scratch_shapes) + (_side_scratch(ns) if ns else []),
        **kwargs,
    )(*args, *side_arrs)
    return res[:n_out], res[n_out:]


def _exchange_small(small, name):
    ns = small.shape[1]

    def body(small_ref, sum_ref, small_all, send_sems, recv_sems, local_sems):
        copies = _direct_copies(False, [small_ref], [small_all], send_sems, recv_sems, local_sems)
        for cp in copies:
            cp.start()
        for cp in copies:
            cp.wait()
        total = small_all[0]
        for i in range(1, N_DEV):
            total = total + small_all[i]
        sum_ref[...] = total

    vmem_spec = pl.BlockSpec(memory_space=pltpu.VMEM)
    return pl.pallas_call(
        body, name=name,
        out_shape=jax.ShapeDtypeStruct((1, ns), F32),
        in_specs=[vmem_spec], out_specs=vmem_spec,
        scratch_shapes=[pltpu.VMEM((N_DEV, 1, ns), F32)] + _side_scratch(1),
    )(small)


def _full(shape):
    return pl.BlockSpec(shape, lambda *_: (0,) * len(shape))


def _inproj_fwd(x, nw, w, name, side=None):
    t = x.shape[0]
    tm = 256

    def body(x_ref, nw_ref, w_ref, proj_ref, h_ref):
        h = _rmsnorm(x_ref[...], nw_ref[...]).astype(BF16)
        h_ref[...] = h
        for j in range(N_COL_TILES):
            sl = slice(j * COL_TILE, (j + 1) * COL_TILE)
            proj_ref[:, sl] = jnp.dot(h, w_ref[:, sl], preferred_element_type=F32)

    grid = (t // tm,)
    return _pallas_with_side(
        body, side, *_grid_ends(grid), 3, 2, name=name, grid=grid,
        out_shape=[jax.ShapeDtypeStruct((t, PROJ_W), F32), jax.ShapeDtypeStruct((t, D_MODEL), BF16)],
        in_specs=[pl.BlockSpec((tm, D_MODEL), lambda i: (i, 0)), _full((1, D_MODEL)), _full((D_MODEL, PROJ_W))],
        out_specs=[pl.BlockSpec((tm, PROJ_W), lambda i: (i, 0)), pl.BlockSpec((tm, D_MODEL), lambda i: (i, 0))],
        compiler_params=pltpu.CompilerParams(dimension_semantics=("arbitrary",), vmem_limit_bytes=VMEM_LIMIT),
        args=(x, nw, w))


def _param_specs():
    return [_full((4, 1536)), _full((1, 1536)), _full((1, 128)), _full((1, 128)), _full((1, 128)),
            _full((1, 1024)), _full((1, 128)), _full((CONF_KERNEL, 512)), _full((1, 512)), _full((1, 512)),
            _full((1, 512))]


def _halo_specs(nc, chunk_of):
    def prev_chunk(b, j):
        return jnp.maximum(b * nc + chunk_of(j) - 1, 0)

    per_xbc = CHUNK // XBC_HALO
    per_cc = CHUNK // CONF_HALO
    return [
        pl.BlockSpec((XBC_HALO, 1536), lambda b, j: (prev_chunk(b, j) * per_xbc + per_xbc - 1, C_XBC // 1536)),
        pl.BlockSpec((CHUNK, 256), lambda b, j: (prev_chunk(b, j), C_K // 256)),
        pl.BlockSpec((CONF_HALO, 1024), lambda b, j: (prev_chunk(b, j) * per_cc + per_cc - 1, C_CA // 1024)),
    ]


def _grid_ends(grid):
    first = lambda: functools.reduce(lambda p, q: p & q, [pl.program_id(i) == 0 for i in range(len(grid))])
    last = lambda: functools.reduce(lambda p, q: p & q, [pl.program_id(i) == n - 1 for i, n in enumerate(grid)])
    return first, last


def _mixer_fwd(x, proj, w_out, params, nb, name, side=None):
    t = x.shape[0]
    nc = t // nb // CHUNK

    def body(x_ref, cur_ref, pxbc_ref, pkv_ref, pcc_ref, wo_ref, *rest):
        prm = [r[...] for r in rest[:11]]
        xn_ref, sall_ref, s_scr = rest[11:]
        c = pl.program_id(1)
        not_first = c > 0
        nf = not_first.astype(F32)

        @pl.when(c == 0)
        def _():
            s_scr[...] = jnp.zeros_like(s_scr)

        cw, cb, dtb, alog, dsk, nw, snk, dww, dwb, lnw, lnb = prm
        s_in = s_scr[...]
        sall_ref[0] = s_in
        y_ssd, s_out = _ssd_part(cur_ref[:, 0:1024], cur_ref[:, C_XBC:C_XBC + 1536], cur_ref[:, C_DT:C_DT + 128],
                                 pxbc_ref[...] * nf, s_in, cw, cb, dtb, alog, dsk, nw)
        s_scr[...] = s_out
        y_attn = _attn_part(cur_ref[:, 1024:1536], cur_ref[:, C_Q:C_Q + 512], cur_ref[:, C_K:C_K + 256],
                            pkv_ref[...] * nf, snk, _kv_mask(CHUNK, not_first, ATTN_HEADS // 2))
        y_conf = _conf_part(cur_ref[:, 1536:2048], cur_ref[:, C_CA:C_CA + 1024], pcc_ref[...] * nf,
                            dww, dwb, lnw, lnb)
        xn_ref[...] = (x_ref[...] + _dot(y_ssd, wo_ref[0:1024, :]) + _dot(y_attn, wo_ref[1024:1536, :])
                       + _dot(y_conf, wo_ref[1536:2048, :]))

    row = lambda b, j: (b * nc + j, 0)
    grid = (nb, nc)
    return _pallas_with_side(
        body, side, *_grid_ends(grid), 17, 2, name=name, grid=grid,
        out_shape=[jax.ShapeDtypeStruct((t, D_MODEL), F32),
                   jax.ShapeDtypeStruct((nb * nc, SSD_HEADS * SSD_HEAD_DIM, SSD_STATE), F32)],
        in_specs=[pl.BlockSpec((CHUNK, D_MODEL), row), pl.BlockSpec((CHUNK, PROJ_W), row)]
                 + _halo_specs(nc, lambda j: j) + [_full((MIX_WIDTH, D_MODEL))] + _param_specs(),
        out_specs=[pl.BlockSpec((CHUNK, D_MODEL), row),
                   pl.BlockSpec((1, SSD_HEADS * SSD_HEAD_DIM, SSD_STATE), lambda b, j: (b * nc + j, 0, 0))],
        scratch_shapes=[pltpu.VMEM((SSD_HEADS * SSD_HEAD_DIM, SSD_STATE), F32)],
        compiler_params=pltpu.CompilerParams(dimension_semantics=("arbitrary", "arbitrary"),
                                             vmem_limit_bytes=VMEM_LIMIT),
        args=(x, proj, proj, proj, proj, w_out, *params))


def _mixer_bwd(dxn, proj, s_all, w_out, params, nb, name, side=None):
    t = dxn.shape[0]
    nc = t // nb // CHUNK
    n_prm = 11

    def body(dxn_ref, cur_ref, pxbc_ref, pkv_ref, pcc_ref, s_ref, wo_ref, *rest):
        prm = [r[...] for r in rest[:n_prm]]
        dproj_ref, ycat_ref = rest[n_prm:n_prm + 2]
        gprm = rest[n_prm + 2:2 * n_prm + 2]
        ds_scr, pend_xbc, pend_kv, pend_cc = rest[2 * n_prm + 2:]
        b, j = pl.program_id(0), pl.program_id(1)
        c = nc - 1 - j
        not_first = c > 0
        nf = not_first.astype(F32)

        @pl.when((b == 0) & (j == 0))
        def _():
            for r in gprm:
                r[...] = jnp.zeros_like(r)

        @pl.when(j == 0)
        def _():
            ds_scr[...] = jnp.zeros_like(ds_scr)
            pend_xbc[...] = jnp.zeros_like(pend_xbc)
            pend_kv[...] = jnp.zeros_like(pend_kv)
            pend_cc[...] = jnp.zeros_like(pend_cc)

        cw, cb, dtb, alog, dsk, nw, snk, dww, dwb, lnw, lnb = prm
        g_cw, g_cb, g_dtb, g_alog, g_dsk, g_nw, g_snk, g_dww, g_dwb, g_lnw, g_lnb = gprm
        dxn_v = dxn_ref[...]

        def add_tail(d_cur, pending):
            lead = jnp.zeros((CHUNK - pending.shape[0], pending.shape[1]), F32)
            return d_cur + jnp.concatenate([lead, pending], axis=0)

        y, vjp = jax.vjp(_conf_part, cur_ref[:, 1536:2048], cur_ref[:, C_CA:C_CA + 1024], pcc_ref[...] * nf,
                         dww, dwb, lnw, lnb)
        ycat_ref[:, 1536:2048] = y.astype(BF16)
        dz, dcacg, dpcc, d_dww, d_dwb, d_lnw, d_lnb = vjp(_dot_nt(dxn_v, wo_ref[1536:2048, :]))
        dproj_ref[:, 1536:2048] = dz.astype(BF16)
        dproj_ref[:, C_CA:C_CA + 1024] = add_tail(dcacg, pend_cc[...]).astype(BF16)
        pend_cc[...] = dpcc
        for r, g in ((g_dww, d_dww), (g_dwb, d_dwb), (g_lnw, d_lnw), (g_lnb, d_lnb)):
            r[...] += g

        attn = functools.partial(_attn_part, kvmask=_kv_mask(CHUNK, not_first, ATTN_HEADS // 2))
        y, vjp = jax.vjp(attn, cur_ref[:, 1024:1536], cur_ref[:, C_Q:C_Q + 512], cur_ref[:, C_K:C_K + 256],
                         pkv_ref[...] * nf, snk)
        ycat_ref[:, 1024:1536] = y.astype(BF16)
        dz, dq, dkv, dpkv, d_snk = vjp(_dot_nt(dxn_v, wo_ref[1024:1536, :]))
        dproj_ref[:, 1024:1536] = dz.astype(BF16)
        dproj_ref[:, C_Q:C_Q + 512] = dq.astype(BF16)
        dproj_ref[:, C_K:C_K + 256] = (dkv + pend_kv[...]).astype(BF16)
        pend_kv[...] = dpkv
        g_snk[...] += d_snk

        (y, _), vjp = jax.vjp(_ssd_part, cur_ref[:, 0:1024], cur_ref[:, C_XBC:C_XBC + 1536],
                              cur_ref[:, C_DT:C_DT + 128], pxbc_ref[...] * nf, s_ref[0], cw, cb, dtb, alog, dsk, nw)
        ycat_ref[:, 0:1024] = y.astype(BF16)
        dz, dxbc, ddtr, dpxbc, ds_in, d_cw, d_cb, d_dtb, d_alog, d_dsk, d_nw = vjp(
            (_dot_nt(dxn_v, wo_ref[0:1024, :]), ds_scr[...]))
        dproj_ref[:, 0:1024] = dz.astype(BF16)
        dproj_ref[:, C_XBC:C_XBC + 1536] = add_tail(dxbc, pend_xbc[...]).astype(BF16)
        dproj_ref[:, C_DT:C_DT + 128] = ddtr.astype(BF16)
        dproj_ref[:, C_DT + 128:PROJ_W] = jnp.zeros((CHUNK, PROJ_W - C_DT - 128), BF16)
        pend_xbc[...] = dpxbc
        ds_scr[...] = ds_in
        for r, g in ((g_cw, d_cw), (g_cb, d_cb), (g_dtb, d_dtb), (g_alog, d_alog), (g_dsk, d_dsk), (g_nw, d_nw)):
            r[...] += g

    row = lambda b, j: (b * nc + nc - 1 - j, 0)
    prm_shapes = [(4, 1536), (1, 1536), (1, 128), (1, 128), (1, 128), (1, 1024), (1, 128), (CONF_KERNEL, 512),
                  (1, 512), (1, 512), (1, 512)]
    grid = (nb, nc)
    return _pallas_with_side(
        body, side, *_grid_ends(grid), 7 + n_prm, 2 + n_prm, name=name, grid=grid,
        out_shape=[jax.ShapeDtypeStruct((t, PROJ_W), BF16), jax.ShapeDtypeStruct((t, MIX_WIDTH), BF16)]
                  + [jax.ShapeDtypeStruct(s, F32) for s in prm_shapes],
        in_specs=[pl.BlockSpec((CHUNK, D_MODEL), row), pl.BlockSpec((CHUNK, PROJ_W), row)]
                 + _halo_specs(nc, lambda j: nc - 1 - j)
                 + [pl.BlockSpec((1, SSD_HEADS * SSD_HEAD_DIM, SSD_STATE), lambda b, j: (b * nc + nc - 1 - j, 0, 0)),
                    _full((MIX_WIDTH, D_MODEL))] + _param_specs(),
        out_specs=[pl.BlockSpec((CHUNK, PROJ_W), row), pl.BlockSpec((CHUNK, MIX_WIDTH), row)]
                  + [_full(s) for s in prm_shapes],
        scratch_shapes=[pltpu.VMEM((SSD_HEADS * SSD_HEAD_DIM, SSD_STATE), F32), pltpu.VMEM((XBC_HALO, 1536), F32),
                        pltpu.VMEM((CHUNK, 256), F32), pltpu.VMEM((CONF_HALO, 1024), F32)],
        compiler_params=pltpu.CompilerParams(dimension_semantics=("arbitrary", "arbitrary"),
                                             vmem_limit_bytes=VMEM_LIMIT),
        args=(dxn, proj, proj, proj, proj, s_all, w_out, *params))


def _gw_out(y_cat, dxn, name):
    t = y_cat.shape[0]
    tk = 512

    def body(y_ref, dxn_ref, out_ref, acc):
        k = pl.program_id(0)

        @pl.when(k == 0)
        def _():
            acc[...] = jnp.zeros_like(acc)

        acc[...] += _dot_tn(y_ref[...], dxn_ref[...])

        @pl.when(k == t // tk - 1)
        def _():
            out_ref[...] = acc[...].astype(BF16)

    out = pl.pallas_call(
        body, name=name, grid=(t // tk,),
        out_shape=jax.ShapeDtypeStruct((MIX_WIDTH, D_MODEL), BF16),
        in_specs=[pl.BlockSpec((tk, MIX_WIDTH), lambda k: (k, 0)), pl.BlockSpec((tk, D_MODEL), lambda k: (k, 0))],
        out_specs=_full((MIX_WIDTH, D_MODEL)),
        scratch_shapes=[pltpu.VMEM((MIX_WIDTH, D_MODEL), F32)],
        compiler_params=pltpu.CompilerParams(dimension_semantics=("arbitrary",), vmem_limit_bytes=VMEM_LIMIT),
    )(y_cat, dxn)
    return out.reshape(N_DEV, MIX_WIDTH // N_DEV, D_MODEL)


def _inproj_bwd_x(dproj, w, x, nw, dxn, name, side=None):
    t = x.shape[0]
    tm = 256

    def body(dp_ref, w_ref, x_ref, nw_ref, dxn_ref, dx_ref, gnw_ref):
        @pl.when(pl.program_id(0) == 0)
        def _():
            gnw_ref[...] = jnp.zeros_like(gnw_ref)

        dh = jnp.zeros((tm, D_MODEL), F32)
        for j in range(N_COL_TILES):
            sl = slice(j * COL_TILE, (j + 1) * COL_TILE)
            dh = dh + _dot_nt(dp_ref[:, sl], w_ref[:, sl])
        _, vjp = jax.vjp(_rmsnorm, x_ref[...], nw_ref[...])
        dx, dnw = vjp(dh)
        dx_ref[...] = dxn_ref[...] + dx
        gnw_ref[...] += dnw

    tok = lambda i: (i, 0)
    grid = (t // tm,)
    return _pallas_with_side(
        body, side, *_grid_ends(grid), 5, 2, name=name, grid=grid,
        out_shape=[jax.ShapeDtypeStruct((t, D_MODEL), F32), jax.ShapeDtypeStruct((1, D_MODEL), F32)],
        in_specs=[pl.BlockSpec((tm, PROJ_W), tok), _full((D_MODEL, PROJ_W)), pl.BlockSpec((tm, D_MODEL), tok),
                  _full((1, D_MODEL)), pl.BlockSpec((tm, D_MODEL), tok)],
        out_specs=[pl.BlockSpec((tm, D_MODEL), tok), _full((1, D_MODEL))],
        compiler_params=pltpu.CompilerParams(dimension_semantics=("arbitrary",), vmem_limit_bytes=VMEM_LIMIT),
        args=(dproj, w, x, nw, dxn))


def _inproj_bwd_w(h, dproj, name, side=None):
    t = h.shape[0]
    tk = 512

    def body(h_ref, dp_ref, gw_ref):
        @pl.when(pl.program_id(1) == 0)
        def _():
            gw_ref[...] = jnp.zeros_like(gw_ref)

        gw_ref[...] += _dot_tn(h_ref[...], dp_ref[...])

    grid = (N_COL_TILES, t // tk)
    return _pallas_with_side(
        body, side, *_grid_ends(grid), 2, 1, name=name, grid=grid,
        out_shape=[jax.ShapeDtypeStruct((D_MODEL, PROJ_W), F32)],
        in_specs=[pl.BlockSpec((tk, D_MODEL), lambda n, k: (k, 0)), pl.BlockSpec((tk, COL_TILE), lambda n, k: (k, n))],
        out_specs=[pl.BlockSpec((D_MODEL, COL_TILE), lambda n, k: (0, n))],
        compiler_params=pltpu.CompilerParams(dimension_semantics=("arbitrary", "arbitrary"),
                                             vmem_limit_bytes=VMEM_LIMIT),
        args=(h, dproj))


def _repack_runs():
    pieces = ((0, 2048, C_Z), (2048, 3584, C_XBC), (3584, 3600, C_DT), (3600, 4368, C_Q), (4368, D_IN_PROJ, C_CA))
    per = D_IN_PROJ // N_DEV
    runs = []
    for j in range(N_DEV):
        lo, hi = per * j, per * (j + 1)
        for a, b, dst in pieces:
            s, e = max(lo, a), min(hi, b)
            if s < e:
                runs.append((j, s - lo, e - lo, dst + s - a))
    return runs


def _repack_w_in(g, name):
    tr = 256

    def body(g_ref, o_ref):
        for j, a, b, dst in _repack_runs():
            o_ref[:, dst:dst + b - a] = g_ref[j, :, a:b]
        o_ref[:, C_DT + 16:PROJ_W] = jnp.zeros((tr, PROJ_W - C_DT - 16), g.dtype)

    return pl.pallas_call(
        body, name=name, grid=(D_MODEL // tr,),
        out_shape=jax.ShapeDtypeStruct((D_MODEL, PROJ_W), g.dtype),
        in_specs=[pl.BlockSpec((N_DEV, tr, D_IN_PROJ // N_DEV), lambda i: (0, i, 0))],
        out_specs=pl.BlockSpec((tr, PROJ_W), lambda i: (i, 0)),
        compiler_params=pltpu.CompilerParams(dimension_semantics=("arbitrary",)),
    )(g)


def _unpack_gw_in(g, name):
    tr = 256

    def body(g_ref, o_ref):
        for j, a, b, dst in _repack_runs():
            o_ref[j, :, a:b] = g_ref[:, dst:dst + b - a].astype(BF16)

    return pl.pallas_call(
        body, name=name, grid=(D_MODEL // tr,),
        out_shape=jax.ShapeDtypeStruct((N_DEV, D_MODEL, D_IN_PROJ // N_DEV), BF16),
        in_specs=[pl.BlockSpec((tr, PROJ_W), lambda i: (i, 0))],
        out_specs=pl.BlockSpec((N_DEV, tr, D_IN_PROJ // N_DEV), lambda i: (0, i, 0)),
        compiler_params=pltpu.CompilerParams(dimension_semantics=("arbitrary",)),
    )(g)


def _loss_head(x, fnw, target, name):
    t = x.shape[0]
    tm = 512

    def body(x_ref, w_ref, t_ref, dx_ref, loss_ref, gw_ref):
        @pl.when(pl.program_id(0) == 0)
        def _():
            loss_ref[...] = jnp.zeros_like(loss_ref)
            gw_ref[...] = jnp.zeros_like(gw_ref)

        y, vjp = jax.vjp(_rmsnorm, x_ref[...], w_ref[...])
        err = y - t_ref[...]
        loss_ref[...] += 0.5 * jnp.sum(jnp.mean(err * err, axis=-1, keepdims=True), axis=0, keepdims=True)
        dx, dw = vjp(err * (1.0 / D_MODEL))
        dx_ref[...] = dx
        gw_ref[...] += dw

    tok = lambda i: (i, 0)
    return pl.pallas_call(
        body, name=name, grid=(t // tm,),
        out_shape=[jax.ShapeDtypeStruct((t, D_MODEL), F32), jax.ShapeDtypeStruct((1, 1), F32),
                   jax.ShapeDtypeStruct((1, D_MODEL), F32)],
        in_specs=[pl.BlockSpec((tm, D_MODEL), tok), _full((1, D_MODEL)), pl.BlockSpec((tm, D_MODEL), tok)],
        out_specs=[pl.BlockSpec((tm, D_MODEL), tok), _full((1, 1)), _full((1, D_MODEL))],
        compiler_params=pltpu.CompilerParams(dimension_semantics=("arbitrary",)),
    )(x, fnw, target)


def _adamw(w, g, m, v):
    m = ADAM_B1 * m + (1.0 - ADAM_B1) * g
    v = ADAM_B2 * v + (1.0 - ADAM_B2) * jnp.square(g)
    m_hat = m / (1.0 - ADAM_B1 ** ADAM_STEP)
    v_hat = v / (1.0 - ADAM_B2 ** ADAM_STEP)
    delta = -ADAM_LR * (m_hat / (jnp.sqrt(v_hat) + ADAM_EPS) + ADAM_WD * w)
    return delta, m, v


def _reduce_adamw(parts, w, m, v, tr, name):
    depth = len(parts)
    p, r, cdim = parts[0].shape
    n_blk = r // tr

    def body(*refs):
        p_refs = refs[:depth]
        w_ref, m_ref, v_ref, g_ref, d_ref, nm_ref, nv_ref = refs[depth:]
        for layer in range(depth):
            @pl.when(pl.program_id(0) == layer)
            def _(p_ref=p_refs[layer]):
                g = p_ref[0].astype(F32)
                for i in range(1, p):
                    g = g + p_ref[i].astype(F32)
                g_ref[0] = g
                d_ref[0], nm_ref[0], nv_ref[0] = _adamw(w_ref[0], g, m_ref[0], v_ref[0])

    def parts_spec(layer):
        return pl.BlockSpec((p, tr, cdim), lambda d, i: (0, jnp.clip(i + (d - layer) * n_blk, 0, n_blk - 1), 0))

    blk = pl.BlockSpec((1, tr, cdim), lambda d, i: (d, i, 0))
    return pl.pallas_call(
        body, name=name, grid=(depth, n_blk),
        out_shape=[jax.ShapeDtypeStruct(w.shape, F32)] * 4,
        in_specs=[parts_spec(layer) for layer in range(depth)] + [blk, blk, blk],
        out_specs=[blk] * 4,
        compiler_params=pltpu.CompilerParams(dimension_semantics=("arbitrary", "arbitrary"),
                                             vmem_limit_bytes=VMEM_LIMIT),
    )(*parts, w, m, v)


def _adamw_small(ssum, entries, name):
    direct = [e[3] for e in entries if not isinstance(e[3], list)]
    n_direct = len(direct)

    def body(*refs):
        ssum_ref, direct_refs = refs[0], list(refs[1:1 + n_direct])
        ins = refs[1 + n_direct:1 + n_direct + 3 * len(entries)]
        outs = refs[1 + n_direct + 3 * len(entries):]
        for k, (w, _, _, grad) in enumerate(entries):
            w_ref, m_ref, v_ref = ins[3 * k:3 * k + 3]
            g_ref, d_ref, nm_ref, nv_ref = outs[4 * k:4 * k + 4]
            if isinstance(grad, list):
                for row, off in enumerate(grad):
                    rows = slice(row, row + 1)
                    g = ssum_ref[:, off:off + w.shape[1]]
                    g_ref[rows, :] = g
                    d_ref[rows, :], nm_ref[rows, :], nv_ref[rows, :] = _adamw(w_ref[rows, :], g, m_ref[rows, :],
                                                                              v_ref[rows, :])
            else:
                g = direct_refs.pop(0)[...]
                g_ref[...] = g
                d_ref[...], nm_ref[...], nv_ref[...] = _adamw(w_ref[...], g, m_ref[...], v_ref[...])

    vmem = pl.BlockSpec(memory_space=pltpu.VMEM)
    args = [ssum] + direct + [a for e in entries for a in e[:3]]
    res = pl.pallas_call(
        body, name=name,
        out_shape=[jax.ShapeDtypeStruct(e[0].shape, F32) for e in entries for _ in range(4)],
        in_specs=[vmem] * len(args), out_specs=[vmem] * (4 * len(entries)),
    )(*args)
    return [res[4 * k:4 * k + 4] for k in range(len(entries))]


def _reduce_adamw_cols(parts, w, m, v, name):
    depth = len(parts)
    p, r, cdim = parts[0].shape
    tc = 256

    def body(*refs):
        p_refs = refs[:depth]
        w_ref, m_ref, v_ref, g_ref, d_ref, nm_ref, nv_ref = refs[depth:]
        for layer in range(depth):
            g = p_refs[layer][0].astype(F32)
            for i in range(1, p):
                g = g + p_refs[layer][i].astype(F32)
            g = g.T
            g_ref[:, layer, :] = g
            d_ref[:, layer, :], nm_ref[:, layer, :], nv_ref[:, layer, :] = _adamw(
                w_ref[:, layer, :], g, m_ref[:, layer, :], v_ref[:, layer, :])

    view = lambda a: jnp.transpose(a, (2, 0, 1))
    blk = pl.BlockSpec((cdim, depth, tc), lambda i: (0, 0, i))
    outs = pl.pallas_call(
        body, name=name, grid=(r // tc,),
        out_shape=[jax.ShapeDtypeStruct((cdim, depth, r), F32)] * 4,
        in_specs=[pl.BlockSpec((p, tc, cdim), lambda i: (0, i, 0))] * depth + [blk, blk, blk],
        out_specs=[blk] * 4,
        compiler_params=pltpu.CompilerParams(dimension_semantics=("arbitrary",), vmem_limit_bytes=VMEM_LIMIT),
    )(*parts, view(w), view(m), view(v))
    return [jnp.transpose(o, (1, 2, 0)) for o in outs]


def _pad_lanes(v, width=128):
    return jnp.pad(v.reshape(1, -1), ((0, 0), (0, width - v.shape[-1])))


SMALL_FIELDS = (("norm_w", 1024), ("conv_b", 1536), ("dt_bias", 128), ("a_log", 128), ("d_skip", 128),
                ("ssd_norm_w", 1024), ("sinks", 128), ("dw_b", 512), ("ln_w", 512), ("ln_b", 512))


def kernel(x, norm_w, w_in, ssd_conv_w, ssd_conv_b, ssd_dt_bias, ssd_a_log, ssd_d, ssd_norm_w, attn_sinks, conf_dw_w, conf_dw_b, conf_ln_w, conf_ln_b, w_out, final_norm_w, loss_target, m_norm_w, m_w_in, m_ssd_conv_w, m_ssd_conv_b, m_ssd_dt_bias, m_ssd_a_log, m_ssd_d, m_ssd_norm_w, m_attn_sinks, m_conf_dw_w, m_conf_dw_b, m_conf_ln_w, m_conf_ln_b, m_w_out, m_final_norm_w, v_norm_w, v_w_in, v_ssd_conv_w, v_ssd_conv_b, v_ssd_dt_bias, v_ssd_a_log, v_ssd_d, v_ssd_norm_w, v_attn_sinks, v_conf_dw_w, v_conf_dw_b, v_conf_ln_w, v_conf_ln_b, v_w_out, v_final_norm_w):
    nb, seq, _ = x.shape
    depth = norm_w.shape[0]
    t = nb * seq
    me_idx = 4 * lax.axis_index("x") + 2 * lax.axis_index("y") + lax.axis_index("c")

    w_in_bf, w_out_bf = w_in.astype(BF16), w_out.astype(BF16)
    g_win0, g_cw, g_dw = _all_gather([w_in_bf[0], ssd_conv_w, conf_dw_w], "gather_weights")
    w_in_full = [_repack_w_in(g_win0, "repack_w_in_0")]
    w_out_full = []
    conv_w_full = [jnp.transpose(g_cw[:, l], (1, 0, 2)).reshape(4, 1536) for l in range(depth)]
    dw_w_full = [jnp.transpose(g_dw[:, l], (1, 0, 2)).reshape(CONF_KERNEL, 512) for l in range(depth)]

    def layer_params(l):
        return [conv_w_full[l], ssd_conv_b[l].reshape(1, -1), _pad_lanes(ssd_dt_bias[l]), _pad_lanes(ssd_a_log[l]),
                _pad_lanes(ssd_d[l]), ssd_norm_w[l].reshape(1, -1), _pad_lanes(attn_sinks[l]), dw_w_full[l],
                conf_dw_b[l].reshape(1, -1), conf_ln_w[l].reshape(1, -1), conf_ln_b[l].reshape(1, -1)]

    xs = [x.reshape(t, D_MODEL)]
    saved = []
    for l in range(depth):
        (proj, h), gathered = _inproj_fwd(xs[l], norm_w[l].reshape(1, -1), w_in_full[l], f"inproj_fwd_{l}",
                                          (False, [w_out_bf[0]]) if l == 0 else None)
        if l == 0:
            w_out_full.append(gathered[0].reshape(MIX_WIDTH, D_MODEL))
        side = (False, [w_in_bf[l + 1], w_out_bf[l + 1]]) if l + 1 < depth else None
        (x_next, s_all), gathered = _mixer_fwd(xs[l], proj, w_out_full[l], layer_params(l), nb, f"mixer_fwd_{l}",
                                               side)
        if side:
            w_in_full.append(_repack_w_in(gathered[0], f"repack_w_in_{l + 1}"))
            w_out_full.append(gathered[1].reshape(MIX_WIDTH, D_MODEL))
        saved.append((proj, h, s_all))
        xs.append(x_next)
    dx, loss_part, g_fnw = _loss_head(xs[depth], final_norm_w.reshape(1, -1), loss_target.reshape(t, D_MODEL),
                                      "loss_head")

    cols_in = D_IN_PROJ // N_DEV
    rows_out = MIX_WIDTH // N_DEV
    small_rows = [None] * depth
    received = [None] * depth
    outgoing = None
    for l in reversed(range(depth)):
        proj, h, s_all = saved[l]
        res, arrived = _mixer_bwd(dx, proj, s_all, w_out_full[l], layer_params(l), nb, f"mixer_bwd_{l}",
                                  (True, outgoing) if outgoing else None)
        if outgoing:
            received[l + 1] = arrived
        dproj, y_cat = res[0], res[1]
        g_cw_l, g_cb, g_dtb, g_alog, g_dsk, g_nw, g_snk, g_dww, g_dwb, g_lnw, g_lnb = res[2:]
        gw_out_parts = _gw_out(y_cat, dx, f"gw_out_{l}")
        nw_l = norm_w[l].reshape(1, -1)
        if l > 0:
            (gw_in,), _ = _inproj_bwd_w(h, dproj, f"inproj_bwd_w_{l}")
            (dx, g_norm), _ = _inproj_bwd_x(dproj, w_in_full[l], xs[l], nw_l, dx, f"inproj_bwd_x_{l}")
            outgoing = [_unpack_gw_in(gw_in, f"unpack_gw_in_{l}"), gw_out_parts]
        else:
            (gw_in,), got_out = _inproj_bwd_w(h, dproj, f"inproj_bwd_w_{l}", (True, [gw_out_parts]))
            gw_in_parts = _unpack_gw_in(gw_in, f"unpack_gw_in_{l}")
            (dx, g_norm), got_in = _inproj_bwd_x(dproj, w_in_full[l], xs[l], nw_l, dx, f"inproj_bwd_x_{l}",
                                                 (True, [gw_in_parts]))
            received[l] = [got_in[0], got_out[0]]
        small_rows[l] = [g_norm, g_cb, g_dtb, g_alog, g_dsk, g_nw, g_snk, g_dwb, g_lnw, g_lnb,
                         g_cw_l.reshape(1, -1), g_dww.reshape(1, -1)]
    grad_x = dx.reshape(nb, seq, D_MODEL)

    small = jnp.concatenate([piece for l in range(depth) for piece in small_rows[l]] + [g_fnw], axis=1)
    ssum = _exchange_small(small, "exchange_small")

    g_w_in, d_w_in, nm_w_in, nv_w_in = _reduce_adamw_cols([received[l][0] for l in range(depth)], w_in, m_w_in,
                                                          v_w_in, "adamw_w_in")
    g_w_out, d_w_out, nm_w_out, nv_w_out = _reduce_adamw([received[l][1] for l in range(depth)], w_out, m_w_out,
                                                         v_w_out, 256, "adamw_w_out")

    per_layer = sum(n for _, n in SMALL_FIELDS) + 4 * 1536 + CONF_KERNEL * 512
    given = {"norm_w": (norm_w, m_norm_w, v_norm_w), "conv_b": (ssd_conv_b, m_ssd_conv_b, v_ssd_conv_b),
             "dt_bias": (ssd_dt_bias, m_ssd_dt_bias, v_ssd_dt_bias), "a_log": (ssd_a_log, m_ssd_a_log, v_ssd_a_log),
             "d_skip": (ssd_d, m_ssd_d, v_ssd_d), "ssd_norm_w": (ssd_norm_w, m_ssd_norm_w, v_ssd_norm_w),
             "sinks": (attn_sinks, m_attn_sinks, v_attn_sinks), "dw_b": (conf_dw_b, m_conf_dw_b, v_conf_dw_b),
             "ln_w": (conf_ln_w, m_conf_ln_w, v_conf_ln_w), "ln_b": (conf_ln_b, m_conf_ln_b, v_conf_ln_b)}
    entries = []
    off = 0
    for fname, n in SMALL_FIELDS:
        entries.append((*given[fname], [l * per_layer + off for l in range(depth)]))
        off += n
    shard_grads = []
    for width, shard, kk in ((1536, 192, 4), (512, 64, CONF_KERNEL)):
        full = [ssum[:, l * per_layer + off:l * per_layer + off + kk * width].reshape(kk, width) for l in range(depth)]
        shard_grads.append(jnp.stack([lax.dynamic_slice(f, (0, me_idx * shard), (kk, shard)) for f in full], axis=0))
        off += kk * width
    entries.append((ssd_conv_w, m_ssd_conv_w, v_ssd_conv_w, shard_grads[0]))
    entries.append((conf_dw_w, m_conf_dw_w, v_conf_dw_w, shard_grads[1]))
    entries.append((final_norm_w.reshape(1, -1), m_final_norm_w.reshape(1, -1), v_final_norm_w.reshape(1, -1),
                    [depth * per_layer]))
    sm = _adamw_small(ssum, entries, "adamw_small")
    sm = {k: quad for k, quad in zip([f for f, _ in SMALL_FIELDS] + ["conv_w", "dw_w", "final"], sm)}

    def outputs(i, big_in_i, big_out_i):
        return [sm["norm_w"][i], big_in_i, sm["conv_w"][i], sm["conv_b"][i], sm["dt_bias"][i], sm["a_log"][i],
                sm["d_skip"][i], sm["ssd_norm_w"][i], sm["sinks"][i], sm["dw_w"][i], sm["dw_b"][i], sm["ln_w"][i],
                sm["ln_b"][i], big_out_i, sm["final"][i].reshape(-1)]

    loss = lax.psum(loss_part[0, 0], MESH_AXES)
    return (loss, grad_x, *outputs(0, g_w_in, g_w_out), *outputs(1, d_w_in, d_w_out),
            *outputs(2, nm_w_in, nm_w_out), *outputs(3, nv_w_in, nv_w_out))
```

```python
import functools

import jax
import jax.numpy as jnp
from jax import lax
from jax.experimental import pallas as pl
from jax.experimental.pallas import tpu as pltpu

F32 = jnp.float32
BF16 = jnp.bfloat16
MESH_AXES = ("x", "y", "c")
N_DEV = 8
EPS = 1e-5

D_MODEL = 1024
CHUNK = 128
SSD_HEADS = 16
SSD_HEAD_DIM = 64
SSD_STATE = 128
ATTN_HEADS = 8
ATTN_HEAD_DIM = 64
CONF_KERNEL = 31
MIX_WIDTH = 2048
D_IN_PROJ = 5392
C_Z = 0
C_CA = 2048
C_XBC = 3072
C_Q = 4608
C_K = 5120
C_V = 5248
C_DT = 5376
PROJ_W = 5632
N_COL_TILES = 4
COL_TILE = PROJ_W // N_COL_TILES
XBC_HALO = 8
CONF_HALO = 32
ATTN_STACK_FWD = 2
ATTN_STACK_BWD = 4
VMEM_LIMIT = 56 * 1024 * 1024

ADAM_LR = 0.001
ADAM_B1 = 0.9
ADAM_B2 = 0.999
ADAM_EPS = 1e-08
ADAM_WD = 0.01
ADAM_STEP = 10


def _silu(v):
    return v * jax.nn.sigmoid(v)


def _softplus(v):
    return jnp.maximum(v, 0.0) + jnp.log1p(jnp.exp(-jnp.abs(v)))


def _rmsnorm(v, w):
    return v * lax.rsqrt(jnp.mean(v * v, axis=-1, keepdims=True) + EPS) * w


def _dot(a, b):
    return jnp.dot(a.astype(BF16), b.astype(BF16), preferred_element_type=F32)


def _dot_nt(a, b):
    return lax.dot_general(a.astype(BF16), b.astype(BF16), (((1,), (1,)), ((), ())), preferred_element_type=F32)


def _dot_tn(a, b):
    return lax.dot_general(a.astype(BF16), b.astype(BF16), (((0,), (0,)), ((), ())), preferred_element_type=F32)


def _taps(ext, offs, out_len, w=None, g=None):
    n_rows, n_cols = ext.shape
    by_shift = {}
    for t, off in enumerate(offs):
        by_shift.setdefault(off % 8, []).append((t, off))
    for r, taps in by_shift.items():
        assert max(off for _, off in taps) - r + out_len <= n_rows - r
    accs = []
    sums = [[None] * (n_cols // 128) for _ in offs]
    for blk in range(n_cols // 128):
        cs = slice(blk * 128, (blk + 1) * 128)
        e = ext[:, cs]
        acc = None
        for r, taps in by_shift.items():
            shifted = e if r == 0 else pltpu.roll(e, n_rows - r, axis=0)
            for t, off in taps:
                window = shifted[off - r:off - r + out_len, :]
                if w is not None:
                    term = w[t:t + 1, cs] * window
                    acc = term if acc is None else acc + term
                if g is not None:
                    sums[t][blk] = jnp.sum(g[:, cs] * window, axis=0, keepdims=True)
        accs.append(acc)
    if w is not None:
        return jnp.concatenate(accs, axis=1)
    return jnp.concatenate([jnp.concatenate(row, axis=1) for row in sums], axis=0)


@functools.partial(jax.custom_vjp, nondiff_argnums=(3,))
def _dwconv(ext, w, b, halo):
    kk = w.shape[0]
    return b + _taps(ext, [halo - (kk - 1) + t for t in range(kk)], ext.shape[0] - halo, w=w)


def _dwconv_fwd(ext, w, b, halo):
    return _dwconv(ext, w, b, halo), (ext, w)


def _dwconv_bwd(halo, res, g):
    ext, w = res
    kk = w.shape[0]
    offs = [halo - (kk - 1) + t for t in range(kk)]
    dw = _taps(ext, offs, ext.shape[0] - halo, g=g)
    zeros = jnp.zeros((halo, g.shape[1]), g.dtype)
    gp = jnp.concatenate([zeros, g, zeros], axis=0)
    dext = _taps(gp, [halo - off for off in offs], ext.shape[0], w=w)
    return dext, dw, jnp.sum(g, axis=0, keepdims=True)


_dwconv.defvjp(_dwconv_fwd, _dwconv_bwd)


def _ssd_part(z_ssd, xbc, dtr, p_xbc, s_in, cw, cb, dtb, alog, dsk, nw):
    qn = xbc.shape[0]
    nh = SSD_HEADS
    per_group = nh // 2
    n_pair = nh // 2
    xa = _silu(_dwconv(jnp.concatenate([p_xbc, xbc], axis=0), cw, cb, XBC_HALO))
    xs = xa[:, 0:1024]
    dt = _softplus(dtr + dtb)
    a = dt * (-jnp.exp(alog))
    rows = lax.broadcasted_iota(jnp.int32, (qn, qn), 0)
    cols = lax.broadcasted_iota(jnp.int32, (qn, qn), 1)
    causal = rows >= cols
    low = cols < SSD_HEAD_DIM
    a_cs = jnp.dot(causal.astype(F32), a, precision=lax.Precision.HIGHEST, preferred_element_type=F32)
    a_cs_t = a_cs.T
    bgs = [xa[:, 1024 + g * 128:1024 + (g + 1) * 128] for g in range(2)]
    cgs = [xa[:, 1280 + g * 128:1280 + (g + 1) * 128] for g in range(2)]
    cbms = [_dot_nt(cgs[g], bgs[g]) for g in range(2)]
    colb = [jnp.broadcast_to(a_cs[:, h:h + 1], (qn, qn)) for h in range(nh)]
    lastb = [jnp.broadcast_to(colb[h][qn - 1:qn, :], (qn, qn)) for h in range(nh)]
    dtb_wide = [jnp.broadcast_to(dt[:, h:h + 1], (qn, qn)) for h in range(nh)]
    lmats = [jnp.exp(jnp.where(causal, colb[h] - a_cs_t[h:h + 1, :], -jnp.inf)) for h in range(nh)]
    ms = [cbms[h // per_group] * lmats[h] for h in range(nh)]
    x_pair = [xs[:, p * 128:(p + 1) * 128] for p in range(n_pair)]
    x_lo = [x_pair[p] * jnp.where(low, dtb_wide[2 * p], 0.0) for p in range(n_pair)]
    x_hi = [x_pair[p] * jnp.where(low, 0.0, dtb_wide[2 * p + 1]) for p in range(n_pair)]
    y_diag = [_dot(ms[2 * p], x_lo[p]) + _dot(ms[2 * p + 1], x_hi[p]) for p in range(n_pair)]
    col_pair = [jnp.where(low, colb[2 * p], colb[2 * p + 1]) for p in range(n_pair)]
    last_pair = [jnp.where(low, lastb[2 * p], lastb[2 * p + 1]) for p in range(n_pair)]
    ecol = [jnp.exp(col_pair[p]) for p in range(n_pair)]
    xw = [(x_lo[p] + x_hi[p]) * jnp.exp(last_pair[p] - col_pair[p]) for p in range(n_pair)]
    y_off, st = [], []
    for g in range(2):
        ps = range(g * n_pair // 2, (g + 1) * n_pair // 2)
        y_off.append(_dot_nt(cgs[g], s_in[g * 512:(g + 1) * 512, :]) * jnp.concatenate([ecol[p] for p in ps], axis=1))
        st.append(_dot_tn(jnp.concatenate([xw[p] for p in ps], axis=1), bgs[g]))
    e_last = jnp.exp(jnp.broadcast_to(a_cs_t[:, qn - 1:qn], (qn, SSD_STATE)))
    scale = jnp.concatenate([jnp.broadcast_to(e_last[h:h + 1, :], (64, SSD_STATE)) for h in range(nh)], axis=0)
    s_out = scale * s_in + jnp.concatenate(st, axis=0)
    d_wide = jnp.concatenate([jnp.broadcast_to(dsk[:, h:h + 1], (1, 64)) for h in range(nh)], axis=1)
    y = jnp.concatenate(y_diag, axis=1) + jnp.concatenate(y_off, axis=1) + d_wide * xs
    gated = y * _silu(z_ssd)
    halves = []
    for g in range(2):
        gg = gated[:, g * 512:(g + 1) * 512]
        halves.append(gg * lax.rsqrt(jnp.mean(gg * gg, axis=-1, keepdims=True) + EPS))
    return jnp.concatenate(halves, axis=1) * nw, s_out


def _attn_part(z_attn, q, kv, p_kv, snk, kvmask, stack):
    qn = q.shape[0]
    kk = jnp.concatenate([p_kv[:, 0:128], kv[:, 0:128]], axis=0)
    vv = jnp.concatenate([p_kv[:, 128:256], kv[:, 128:256]], axis=0)
    units = range(ATTN_HEADS // stack)
    heads = [range(u * stack, (u + 1) * stack) for u in units]
    kv_of = [u * stack // (ATTN_HEADS // 2) for u in units]
    k_of = [kk[:, g * 64:(g + 1) * 64] for g in kv_of]
    v_of = [vv[:, g * 64:(g + 1) * 64] for g in kv_of]
    qs = [jnp.concatenate([q[:, h * 64:(h + 1) * 64] for h in heads[u]], axis=0) for u in units]
    sk = [jnp.concatenate([jnp.broadcast_to(snk[:, h:h + 1], (qn, 1)) for h in heads[u]], axis=0) for u in units]
    s = [jnp.where(kvmask, _dot_nt(qs[u], k_of[u]) * (ATTN_HEAD_DIM ** -0.5), -jnp.inf) for u in units]
    m = [lax.stop_gradient(jnp.maximum(jnp.max(s[u], axis=-1, keepdims=True), sk[u])) for u in units]
    e = [jnp.exp(s[u] - m[u]) for u in units]
    r_den = [1.0 / (jnp.sum(e[u], axis=-1, keepdims=True) + jnp.exp(sk[u] - m[u])) for u in units]
    o = [_dot(e[u], v_of[u]) * r_den[u] for u in units]
    outs = [o[u][i * qn:(i + 1) * qn, :] for u in units for i in range(stack)]
    return jnp.concatenate(outs, axis=1) * _silu(z_attn)


def _conf_part(z_conf, cacg, p_cc, dww, dwb, lnw, lnb):
    c0 = cacg[:, 0:512] * jax.nn.sigmoid(cacg[:, 512:1024])
    pc0 = p_cc[:, 0:512] * jax.nn.sigmoid(p_cc[:, 512:1024])
    acc = _dwconv(jnp.concatenate([pc0, c0], axis=0), dww, dwb, CONF_HALO)
    xc = acc - jnp.mean(acc, axis=-1, keepdims=True)
    yln = xc * lax.rsqrt(jnp.mean(xc * xc, axis=-1, keepdims=True) + EPS) * lnw + lnb
    return _silu(yln) * _silu(z_conf)


def _kv_mask(qn, not_first, reps):
    ii = lax.broadcasted_iota(jnp.int32, (reps * qn, 2 * qn), 0) & (qn - 1)
    jj = lax.broadcasted_iota(jnp.int32, (reps * qn, 2 * qn), 1)
    d = jj - ii
    return (d >= 1) & (d <= qn) & (not_first | (jj >= qn))


def _my_place():
    return lax.axis_index("x"), lax.axis_index("y"), lax.axis_index("c")


def _all_gather(arrs, name):
    n = len(arrs)

    def body(*refs):
        ins, outs = refs[:n], refs[n:2 * n]
        send_sems, recv_sems, local_sems = refs[2 * n:]
        x, y, c = _my_place()
        me, sibling = (x, y, c), (x, y, 1 - c)
        chips = [(1 - x, y), (x, 1 - y), (1 - x, 1 - y)]

        def slot(a, p):
            return outs[a].at[4 * p[0] + 2 * p[1] + p[2]]

        def copy(a, kk, block, to, src=None):
            return pltpu.make_async_remote_copy(
                src_ref=slot(a, block) if src is None else src, dst_ref=slot(a, block),
                send_sem=send_sems.at[a, kk], recv_sem=recv_sems.at[a, kk],
                device_id=to, device_id_type=pl.DeviceIdType.MESH)

        mine = [pltpu.make_async_copy(ins[a], slot(a, me), local_sems.at[a]) for a in range(n)]
        for cp in mine:
            cp.start()
        first = []
        for a in range(n):
            first.append(copy(a, 0, me, sibling, src=ins[a]))
            first += [copy(a, 1 + j, me, (*chip, c), src=ins[a]) for j, chip in enumerate(chips)]
        for cp in first:
            cp.start()
        passed = []
        for j, chip in enumerate(chips):
            for a in range(n):
                copy(a, 1 + j, (*chip, c), me).wait_recv()
                fwd = copy(a, 4 + j, (*chip, c), sibling)
                fwd.start()
                passed.append(fwd)
        for a in range(n):
            copy(a, 0, sibling, me).wait_recv()
            for j, chip in enumerate(chips):
                copy(a, 4 + j, (*chip, 1 - c), me).wait_recv()
        for cp in first + passed:
            cp.wait_send()
        for cp in mine:
            cp.wait()

    any_spec = pl.BlockSpec(memory_space=pl.ANY)
    return pl.pallas_call(
        body, name=name,
        out_shape=[jax.ShapeDtypeStruct((N_DEV,) + a.shape, a.dtype) for a in arrs],
        in_specs=[any_spec] * n, out_specs=[any_spec] * n,
        scratch_shapes=[pltpu.SemaphoreType.DMA((n, 7)), pltpu.SemaphoreType.DMA((n, 7)),
                        pltpu.SemaphoreType.DMA((n,))],
    )(*arrs)


GATHER, SCATTER, CHIP_SCATTER = "gather", "scatter", "chip_scatter"


def _direct_copies(mode, ins, outs, send_sems, recv_sems, local_sems):
    x, y, c = _my_place()
    by_chip = mode == CHIP_SCATTER
    place = (lambda px, py, pc: 2 * px + py) if by_chip else (lambda px, py, pc: 4 * px + 2 * py + pc)
    me_idx = place(x, y, c)
    n = len(ins)
    local = [pltpu.make_async_copy(ins[a] if mode == GATHER else ins[a].at[me_idx], outs[a].at[me_idx],
                                   local_sems.at[a]) for a in range(n)]
    remote = []
    for rel in range(1, N_DEV):
        if by_chip and rel & 1:
            continue
        px = 1 - x if rel & 4 else x
        py = 1 - y if rel & 2 else y
        pc = 1 - c if rel & 1 else c
        for a in range(n):
            remote.append(pltpu.make_async_remote_copy(
                src_ref=ins[a] if mode == GATHER else ins[a].at[place(px, py, pc)], dst_ref=outs[a].at[me_idx],
                send_sem=send_sems.at[a, rel - 1], recv_sem=recv_sems.at[a, rel - 1],
                device_id=(px, py, pc), device_id_type=pl.DeviceIdType.MESH))
    return local + remote


def _side_scratch(n):
    return [pltpu.SemaphoreType.DMA((n, 7)), pltpu.SemaphoreType.DMA((n, 7)), pltpu.SemaphoreType.DMA((n,))]


def _side_out_shapes(mode, arrs):
    return [jax.ShapeDtypeStruct((N_DEV,) + a.shape if mode == GATHER else a.shape, a.dtype) for a in arrs]


def _pallas_with_side(body, side, first, last, n_in, n_out, *, in_specs, out_specs, out_shape, scratch_shapes=(),
                      args, **kwargs):
    side_arrs = [] if side is None else list(side[1])
    ns = len(side_arrs)

    def wrapped(*refs):
        own_in, side_in = refs[:n_in], refs[n_in:n_in + ns]
        o = n_in + ns
        own_out, side_out = refs[o:o + n_out], refs[o + n_out:o + n_out + ns]
        scratch = refs[o + n_out + ns:]
        own_scratch, sems = (scratch[:-3], scratch[-3:]) if ns else (scratch, ())
        if ns:
            @pl.when(first())
            def _():
                for cp in _direct_copies(side[0], side_in, side_out, *sems):
                    cp.start()

        body(*own_in, *own_out, *own_scratch)
        if ns:
            @pl.when(last())
            def _():
                for cp in _direct_copies(side[0], side_in, side_out, *sems):
                    cp.wait()

    any_spec = pl.BlockSpec(memory_space=pl.ANY)
    res = pl.pallas_call(
        wrapped,
        in_specs=list(in_specs) + [any_spec] * ns,
        out_specs=list(out_specs) + [any_spec] * ns,
        out_shape=list(out_shape) + (_side_out_shapes(side[0], side_arrs) if ns else []),
        scratch_shapes=list(scratch_shapes) + (_side_scratch(ns) if ns else []),
        **kwargs,
    )(*args, *side_arrs)
    return res[:n_out], res[n_out:]


def _exchange_small(small, name):
    ns = small.shape[1]

    def body(small_ref, sum_ref, small_all, send_sems, recv_sems, local_sems):
        copies = _direct_copies(GATHER, [small_ref], [small_all], send_sems, recv_sems, local_sems)
        for cp in copies:
            cp.start()
        for cp in copies:
            cp.wait()
        total = small_all[0]
        for i in range(1, N_DEV):
            total = total + small_all[i]
        sum_ref[...] = total

    vmem_spec = pl.BlockSpec(memory_space=pltpu.VMEM)
    return pl.pallas_call(
        body, name=name,
        out_shape=jax.ShapeDtypeStruct((1, ns), F32),
        in_specs=[vmem_spec], out_specs=vmem_spec,
        scratch_shapes=[pltpu.VMEM((N_DEV, 1, ns), F32)] + _side_scratch(1),
    )(small)


def _full(shape):
    return pl.BlockSpec(shape, lambda *_: (0,) * len(shape))


def _inproj_fwd(x, nw, w, name, side=None):
    t = x.shape[0]
    tm = 256

    def body(x_ref, nw_ref, w_ref, proj_ref, h_ref):
        h = _rmsnorm(x_ref[...], nw_ref[...]).astype(BF16)
        h_ref[...] = h
        for j in range(N_COL_TILES):
            sl = slice(j * COL_TILE, (j + 1) * COL_TILE)
            proj_ref[:, sl] = jnp.dot(h, w_ref[:, sl], preferred_element_type=F32)

    grid = (t // tm,)
    return _pallas_with_side(
        body, side, *_grid_ends(grid), 3, 2, name=name, grid=grid,
        out_shape=[jax.ShapeDtypeStruct((t, PROJ_W), F32), jax.ShapeDtypeStruct((t, D_MODEL), BF16)],
        in_specs=[pl.BlockSpec((tm, D_MODEL), lambda i: (i, 0)), _full((1, D_MODEL)), _full((D_MODEL, PROJ_W))],
        out_specs=[pl.BlockSpec((tm, PROJ_W), lambda i: (i, 0)), pl.BlockSpec((tm, D_MODEL), lambda i: (i, 0))],
        compiler_params=pltpu.CompilerParams(dimension_semantics=("arbitrary",), vmem_limit_bytes=VMEM_LIMIT),
        args=(x, nw, w))


def _param_specs():
    return [_full((4, 1536)), _full((1, 1536)), _full((1, 128)), _full((1, 128)), _full((1, 128)),
            _full((1, 1024)), _full((1, 128)), _full((CONF_KERNEL, 512)), _full((1, 512)), _full((1, 512)),
            _full((1, 512))]


def _halo_specs(nc, chunk_of):
    def prev_chunk(b, j):
        return jnp.maximum(b * nc + chunk_of(j) - 1, 0)

    per_xbc = CHUNK // XBC_HALO
    per_cc = CHUNK // CONF_HALO
    return [
        pl.BlockSpec((XBC_HALO, 1536), lambda b, j: (prev_chunk(b, j) * per_xbc + per_xbc - 1, C_XBC // 1536)),
        pl.BlockSpec((CHUNK, 256), lambda b, j: (prev_chunk(b, j), C_K // 256)),
        pl.BlockSpec((CONF_HALO, 1024), lambda b, j: (prev_chunk(b, j) * per_cc + per_cc - 1, C_CA // 1024)),
    ]


def _grid_ends(grid):
    first = lambda: functools.reduce(lambda p, q: p & q, [pl.program_id(i) == 0 for i in range(len(grid))])
    last = lambda: functools.reduce(lambda p, q: p & q, [pl.program_id(i) == n - 1 for i, n in enumerate(grid)])
    return first, last


def _mixer_fwd(x, proj, w_out, params, nb, name, side=None):
    t = x.shape[0]
    nc = t // nb // CHUNK

    def body(x_ref, cur_ref, pxbc_ref, pkv_ref, pcc_ref, wo_ref, *rest):
        prm = [r[...] for r in rest[:11]]
        xn_ref, sall_ref, s_scr = rest[11:]
        c = pl.program_id(1)
        not_first = c > 0
        nf = not_first.astype(F32)

        @pl.when(c == 0)
        def _():
            s_scr[...] = jnp.zeros_like(s_scr)

        cw, cb, dtb, alog, dsk, nw, snk, dww, dwb, lnw, lnb = prm
        s_in = s_scr[...]
        sall_ref[0] = s_in
        y_ssd, s_out = _ssd_part(cur_ref[:, 0:1024], cur_ref[:, C_XBC:C_XBC + 1536], cur_ref[:, C_DT:C_DT + 128],
                                 pxbc_ref[...] * nf, s_in, cw, cb, dtb, alog, dsk, nw)
        s_scr[...] = s_out
        y_attn = _attn_part(cur_ref[:, 1024:1536], cur_ref[:, C_Q:C_Q + 512], cur_ref[:, C_K:C_K + 256],
                            pkv_ref[...] * nf, snk, _kv_mask(CHUNK, not_first, ATTN_STACK_FWD), ATTN_STACK_FWD)
        y_conf = _conf_part(cur_ref[:, 1536:2048], cur_ref[:, C_CA:C_CA + 1024], pcc_ref[...] * nf,
                            dww, dwb, lnw, lnb)
        xn_ref[...] = (x_ref[...] + _dot(y_ssd, wo_ref[0:1024, :]) + _dot(y_attn, wo_ref[1024:1536, :])
                       + _dot(y_conf, wo_ref[1536:2048, :]))

    row = lambda b, j: (b * nc + j, 0)
    grid = (nb, nc)
    return _pallas_with_side(
        body, side, *_grid_ends(grid), 17, 2, name=name, grid=grid,
        out_shape=[jax.ShapeDtypeStruct((t, D_MODEL), F32),
                   jax.ShapeDtypeStruct((nb * nc, SSD_HEADS * SSD_HEAD_DIM, SSD_STATE), F32)],
        in_specs=[pl.BlockSpec((CHUNK, D_MODEL), row), pl.BlockSpec((CHUNK, PROJ_W), row)]
                 + _halo_specs(nc, lambda j: j) + [_full((MIX_WIDTH, D_MODEL))] + _param_specs(),
        out_specs=[pl.BlockSpec((CHUNK, D_MODEL), row),
                   pl.BlockSpec((1, SSD_HEADS * SSD_HEAD_DIM, SSD_STATE), lambda b, j: (b * nc + j, 0, 0))],
        scratch_shapes=[pltpu.VMEM((SSD_HEADS * SSD_HEAD_DIM, SSD_STATE), F32)],
        compiler_params=pltpu.CompilerParams(dimension_semantics=("arbitrary", "arbitrary"),
                                             vmem_limit_bytes=VMEM_LIMIT),
        args=(x, proj, proj, proj, proj, w_out, *params))


def _mixer_bwd(dxn, proj, s_all, w_out, params, nb, name, side=None):
    t = dxn.shape[0]
    nc = t // nb // CHUNK
    n_prm = 11

    def body(dxn_ref, cur_ref, pxbc_ref, pkv_ref, pcc_ref, s_ref, wo_ref, *rest):
        prm = [r[...] for r in rest[:n_prm]]
        dproj_ref, ycat_ref = rest[n_prm:n_prm + 2]
        gprm = rest[n_prm + 2:2 * n_prm + 2]
        ds_scr, pend_xbc, pend_kv, pend_cc = rest[2 * n_prm + 2:]
        b, j = pl.program_id(0), pl.program_id(1)
        c = nc - 1 - j
        not_first = c > 0
        nf = not_first.astype(F32)

        @pl.when((b == 0) & (j == 0))
        def _():
            for r in gprm:
                r[...] = jnp.zeros_like(r)

        @pl.when(j == 0)
        def _():
            ds_scr[...] = jnp.zeros_like(ds_scr)
            pend_xbc[...] = jnp.zeros_like(pend_xbc)
            pend_kv[...] = jnp.zeros_like(pend_kv)
            pend_cc[...] = jnp.zeros_like(pend_cc)

        cw, cb, dtb, alog, dsk, nw, snk, dww, dwb, lnw, lnb = prm
        g_cw, g_cb, g_dtb, g_alog, g_dsk, g_nw, g_snk, g_dww, g_dwb, g_lnw, g_lnb = gprm
        dxn_v = dxn_ref[...]

        def add_tail(d_cur, pending):
            lead = jnp.zeros((CHUNK - pending.shape[0], pending.shape[1]), F32)
            return d_cur + jnp.concatenate([lead, pending], axis=0)

        y, vjp = jax.vjp(_conf_part, cur_ref[:, 1536:2048], cur_ref[:, C_CA:C_CA + 1024], pcc_ref[...] * nf,
                         dww, dwb, lnw, lnb)
        ycat_ref[:, 1536:2048] = y.astype(BF16)
        dz, dcacg, dpcc, d_dww, d_dwb, d_lnw, d_lnb = vjp(_dot_nt(dxn_v, wo_ref[1536:2048, :]))
        dproj_ref[:, 1536:2048] = dz.astype(BF16)
        dproj_ref[:, C_CA:C_CA + 1024] = add_tail(dcacg, pend_cc[...]).astype(BF16)
        pend_cc[...] = dpcc
        for r, g in ((g_dww, d_dww), (g_dwb, d_dwb), (g_lnw, d_lnw), (g_lnb, d_lnb)):
            r[...] += g

        attn = functools.partial(_attn_part, kvmask=_kv_mask(CHUNK, not_first, ATTN_STACK_BWD),
                                 stack=ATTN_STACK_BWD)
        y, vjp = jax.vjp(attn, cur_ref[:, 1024:1536], cur_ref[:, C_Q:C_Q + 512], cur_ref[:, C_K:C_K + 256],
                         pkv_ref[...] * nf, snk)
        ycat_ref[:, 1024:1536] = y.astype(BF16)
        dz, dq, dkv, dpkv, d_snk = vjp(_dot_nt(dxn_v, wo_ref[1024:1536, :]))
        dproj_ref[:, 1024:1536] = dz.astype(BF16)
        dproj_ref[:, C_Q:C_Q + 512] = dq.astype(BF16)
        dproj_ref[:, C_K:C_K + 256] = (dkv + pend_kv[...]).astype(BF16)
        pend_kv[...] = dpkv
        g_snk[...] += d_snk

        (y, _), vjp = jax.vjp(_ssd_part, cur_ref[:, 0:1024], cur_ref[:, C_XBC:C_XBC + 1536],
                              cur_ref[:, C_DT:C_DT + 128], pxbc_ref[...] * nf, s_ref[0], cw, cb, dtb, alog, dsk, nw)
        ycat_ref[:, 0:1024] = y.astype(BF16)
        dz, dxbc, ddtr, dpxbc, ds_in, d_cw, d_cb, d_dtb, d_alog, d_dsk, d_nw = vjp(
            (_dot_nt(dxn_v, wo_ref[0:1024, :]), ds_scr[...]))
        dproj_ref[:, 0:1024] = dz.astype(BF16)
        dproj_ref[:, C_XBC:C_XBC + 1536] = add_tail(dxbc, pend_xbc[...]).astype(BF16)
        dproj_ref[:, C_DT:C_DT + 128] = ddtr.astype(BF16)
        dproj_ref[:, C_DT + 128:PROJ_W] = jnp.zeros((CHUNK, PROJ_W - C_DT - 128), BF16)
        pend_xbc[...] = dpxbc
        ds_scr[...] = ds_in
        for r, g in ((g_cw, d_cw), (g_cb, d_cb), (g_dtb, d_dtb), (g_alog, d_alog), (g_dsk, d_dsk), (g_nw, d_nw)):
            r[...] += g

    row = lambda b, j: (b * nc + nc - 1 - j, 0)
    prm_shapes = [(4, 1536), (1, 1536), (1, 128), (1, 128), (1, 128), (1, 1024), (1, 128), (CONF_KERNEL, 512),
                  (1, 512), (1, 512), (1, 512)]
    grid = (nb, nc)
    return _pallas_with_side(
        body, side, *_grid_ends(grid), 7 + n_prm, 2 + n_prm, name=name, grid=grid,
        out_shape=[jax.ShapeDtypeStruct((t, PROJ_W), BF16), jax.ShapeDtypeStruct((t, MIX_WIDTH), BF16)]
                  + [jax.ShapeDtypeStruct(s, F32) for s in prm_shapes],
        in_specs=[pl.BlockSpec((CHUNK, D_MODEL), row), pl.BlockSpec((CHUNK, PROJ_W), row)]
                 + _halo_specs(nc, lambda j: nc - 1 - j)
                 + [pl.BlockSpec((1, SSD_HEADS * SSD_HEAD_DIM, SSD_STATE), lambda b, j: (b * nc + nc - 1 - j, 0, 0)),
                    _full((MIX_WIDTH, D_MODEL))] + _param_specs(),
        out_specs=[pl.BlockSpec((CHUNK, PROJ_W), row), pl.BlockSpec((CHUNK, MIX_WIDTH), row)]
                  + [_full(s) for s in prm_shapes],
        scratch_shapes=[pltpu.VMEM((SSD_HEADS * SSD_HEAD_DIM, SSD_STATE), F32), pltpu.VMEM((XBC_HALO, 1536), F32),
                        pltpu.VMEM((CHUNK, 256), F32), pltpu.VMEM((CONF_HALO, 1024), F32)],
        compiler_params=pltpu.CompilerParams(dimension_semantics=("arbitrary", "arbitrary"),
                                             vmem_limit_bytes=VMEM_LIMIT),
        args=(dxn, proj, proj, proj, proj, s_all, w_out, *params))


def _gw_out(y_cat, dxn, name):
    t = y_cat.shape[0]
    tk = 512

    def body(y_ref, dxn_ref, out_ref, acc):
        k = pl.program_id(0)

        @pl.when(k == 0)
        def _():
            acc[...] = jnp.zeros_like(acc)

        acc[...] += _dot_tn(y_ref[...], dxn_ref[...])

        @pl.when(k == t // tk - 1)
        def _():
            out_ref[...] = acc[...].astype(BF16)

    out = pl.pallas_call(
        body, name=name, grid=(t // tk,),
        out_shape=jax.ShapeDtypeStruct((MIX_WIDTH, D_MODEL), BF16),
        in_specs=[pl.BlockSpec((tk, MIX_WIDTH), lambda k: (k, 0)), pl.BlockSpec((tk, D_MODEL), lambda k: (k, 0))],
        out_specs=_full((MIX_WIDTH, D_MODEL)),
        scratch_shapes=[pltpu.VMEM((MIX_WIDTH, D_MODEL), F32)],
        compiler_params=pltpu.CompilerParams(dimension_semantics=("arbitrary",), vmem_limit_bytes=VMEM_LIMIT),
    )(y_cat, dxn)
    return out.reshape(N_DEV, MIX_WIDTH // N_DEV, D_MODEL)


def _inproj_bwd_x(dproj, w, x, nw, dxn, name, side=None):
    t = x.shape[0]
    tm = 256

    def body(dp_ref, w_ref, x_ref, nw_ref, dxn_ref, dx_ref, gnw_ref):
        @pl.when(pl.program_id(0) == 0)
        def _():
            gnw_ref[...] = jnp.zeros_like(gnw_ref)

        dh = jnp.zeros((tm, D_MODEL), F32)
        for j in range(N_COL_TILES):
            sl = slice(j * COL_TILE, (j + 1) * COL_TILE)
            dh = dh + _dot_nt(dp_ref[:, sl], w_ref[:, sl])
        _, vjp = jax.vjp(_rmsnorm, x_ref[...], nw_ref[...])
        dx, dnw = vjp(dh)
        dx_ref[...] = dxn_ref[...] + dx
        gnw_ref[...] += dnw

    tok = lambda i: (i, 0)
    grid = (t // tm,)
    return _pallas_with_side(
        body, side, *_grid_ends(grid), 5, 2, name=name, grid=grid,
        out_shape=[jax.ShapeDtypeStruct((t, D_MODEL), F32), jax.ShapeDtypeStruct((1, D_MODEL), F32)],
        in_specs=[pl.BlockSpec((tm, PROJ_W), tok), _full((D_MODEL, PROJ_W)), pl.BlockSpec((tm, D_MODEL), tok),
                  _full((1, D_MODEL)), pl.BlockSpec((tm, D_MODEL), tok)],
        out_specs=[pl.BlockSpec((tm, D_MODEL), tok), _full((1, D_MODEL))],
        compiler_params=pltpu.CompilerParams(dimension_semantics=("arbitrary",), vmem_limit_bytes=VMEM_LIMIT),
        args=(dproj, w, x, nw, dxn))


def _inproj_bwd_w(h, dproj, name, side=None):
    t = h.shape[0]
    tk = 512

    def body(h_ref, dp_ref, gw_ref):
        @pl.when(pl.program_id(1) == 0)
        def _():
            gw_ref[...] = jnp.zeros_like(gw_ref)

        gw_ref[...] += _dot_tn(h_ref[...], dp_ref[...])

    grid = (N_COL_TILES, t // tk)
    return _pallas_with_side(
        body, side, *_grid_ends(grid), 2, 1, name=name, grid=grid,
        out_shape=[jax.ShapeDtypeStruct((D_MODEL, PROJ_W), F32)],
        in_specs=[pl.BlockSpec((tk, D_MODEL), lambda n, k: (k, 0)), pl.BlockSpec((tk, COL_TILE), lambda n, k: (k, n))],
        out_specs=[pl.BlockSpec((D_MODEL, COL_TILE), lambda n, k: (0, n))],
        compiler_params=pltpu.CompilerParams(dimension_semantics=("arbitrary", "arbitrary"),
                                             vmem_limit_bytes=VMEM_LIMIT),
        args=(h, dproj))


def _repack_runs():
    pieces = ((0, 2048, C_Z), (2048, 3584, C_XBC), (3584, 3600, C_DT), (3600, 4368, C_Q), (4368, D_IN_PROJ, C_CA))
    per = D_IN_PROJ // N_DEV
    runs = []
    for j in range(N_DEV):
        lo, hi = per * j, per * (j + 1)
        for a, b, dst in pieces:
            s, e = max(lo, a), min(hi, b)
            if s < e:
                runs.append((j, s - lo, e - lo, dst + s - a))
    return runs


def _repack_w_in(g, name):
    tr = 256

    def body(g_ref, o_ref):
        for j, a, b, dst in _repack_runs():
            o_ref[:, dst:dst + b - a] = g_ref[j, :, a:b]
        o_ref[:, C_DT + 16:PROJ_W] = jnp.zeros((tr, PROJ_W - C_DT - 16), g.dtype)

    return pl.pallas_call(
        body, name=name, grid=(D_MODEL // tr,),
        out_shape=jax.ShapeDtypeStruct((D_MODEL, PROJ_W), g.dtype),
        in_specs=[pl.BlockSpec((N_DEV, tr, D_IN_PROJ // N_DEV), lambda i: (0, i, 0))],
        out_specs=pl.BlockSpec((tr, PROJ_W), lambda i: (i, 0)),
        compiler_params=pltpu.CompilerParams(dimension_semantics=("arbitrary",)),
    )(g)


def _unpack_gw_in(g, name):
    tr = 256

    def body(g_ref, o_ref):
        for j, a, b, dst in _repack_runs():
            o_ref[j, :, a:b] = g_ref[:, dst:dst + b - a].astype(BF16)

    return pl.pallas_call(
        body, name=name, grid=(D_MODEL // tr,),
        out_shape=jax.ShapeDtypeStruct((N_DEV, D_MODEL, D_IN_PROJ // N_DEV), BF16),
        in_specs=[pl.BlockSpec((tr, PROJ_W), lambda i: (i, 0))],
        out_specs=pl.BlockSpec((N_DEV, tr, D_IN_PROJ // N_DEV), lambda i: (0, i, 0)),
        compiler_params=pltpu.CompilerParams(dimension_semantics=("arbitrary",)),
    )(g)


def _pair_sum(parts, name):
    n_dev, r, cdim = parts.shape
    n_chip = n_dev // 2
    by_chip = parts.reshape(n_chip, 2, r, cdim)

    def swap_body(p_ref, got_ref, send_sem, recv_sem):
        x, y, c = _my_place()
        cp = pltpu.make_async_remote_copy(
            src_ref=p_ref.at[:, pl.ds(1 - c, 1)], dst_ref=got_ref, send_sem=send_sem, recv_sem=recv_sem,
            device_id=(x, y, 1 - c), device_id_type=pl.DeviceIdType.MESH)
        cp.start()
        cp.wait()

    any_spec = pl.BlockSpec(memory_space=pl.ANY)
    got = pl.pallas_call(
        swap_body, name=name + "_swap",
        out_shape=jax.ShapeDtypeStruct((n_chip, 1, r, cdim), parts.dtype),
        in_specs=[any_spec], out_specs=any_spec,
        scratch_shapes=[pltpu.SemaphoreType.DMA, pltpu.SemaphoreType.DMA],
    )(by_chip)

    tr = 256

    def add_body(p_ref, got_ref, o_ref):
        mine = jnp.where(lax.axis_index("c") == 0, p_ref[0, 0], p_ref[0, 1])
        o_ref[0] = (mine.astype(F32) + got_ref[0, 0].astype(F32)).astype(o_ref.dtype)

    return pl.pallas_call(
        add_body, name=name + "_add", grid=(n_chip, r // tr),
        out_shape=jax.ShapeDtypeStruct((n_chip, r, cdim), parts.dtype),
        in_specs=[pl.BlockSpec((1, 2, tr, cdim), lambda k, i: (k, 0, i, 0)),
                  pl.BlockSpec((1, 1, tr, cdim), lambda k, i: (k, 0, i, 0))],
        out_specs=pl.BlockSpec((1, tr, cdim), lambda k, i: (k, i, 0)),
        compiler_params=pltpu.CompilerParams(dimension_semantics=("arbitrary", "arbitrary")),
    )(by_chip, got)


def _loss_head(x, fnw, target, name):
    t = x.shape[0]
    tm = 512

    def body(x_ref, w_ref, t_ref, dx_ref, loss_ref, gw_ref):
        @pl.when(pl.program_id(0) == 0)
        def _():
            loss_ref[...] = jnp.zeros_like(loss_ref)
            gw_ref[...] = jnp.zeros_like(gw_ref)

        y, vjp = jax.vjp(_rmsnorm, x_ref[...], w_ref[...])
        err = y - t_ref[...]
        loss_ref[...] += 0.5 * jnp.sum(jnp.mean(err * err, axis=-1, keepdims=True), axis=0, keepdims=True)
        dx, dw = vjp(err * (1.0 / D_MODEL))
        dx_ref[...] = dx
        gw_ref[...] += dw

    tok = lambda i: (i, 0)
    return pl.pallas_call(
        body, name=name, grid=(t // tm,),
        out_shape=[jax.ShapeDtypeStruct((t, D_MODEL), F32), jax.ShapeDtypeStruct((1, 1), F32),
                   jax.ShapeDtypeStruct((1, D_MODEL), F32)],
        in_specs=[pl.BlockSpec((tm, D_MODEL), tok), _full((1, D_MODEL)), pl.BlockSpec((tm, D_MODEL), tok)],
        out_specs=[pl.BlockSpec((tm, D_MODEL), tok), _full((1, 1)), _full((1, D_MODEL))],
        compiler_params=pltpu.CompilerParams(dimension_semantics=("arbitrary",)),
    )(x, fnw, target)


def _adamw(w, g, m, v):
    m = ADAM_B1 * m + (1.0 - ADAM_B1) * g
    v = ADAM_B2 * v + (1.0 - ADAM_B2) * jnp.square(g)
    m_hat = m / (1.0 - ADAM_B1 ** ADAM_STEP)
    v_hat = v / (1.0 - ADAM_B2 ** ADAM_STEP)
    delta = -ADAM_LR * (m_hat / (jnp.sqrt(v_hat) + ADAM_EPS) + ADAM_WD * w)
    return delta, m, v


def _reduce_adamw(parts, w, m, v, tr, name):
    depth = len(parts)
    p, r, cdim = parts[0].shape
    n_blk = r // tr

    def body(*refs):
        p_refs = refs[:depth]
        w_ref, m_ref, v_ref, g_ref, d_ref, nm_ref, nv_ref = refs[depth:]
        for layer in range(depth):
            @pl.when(pl.program_id(0) == layer)
            def _(p_ref=p_refs[layer]):
                g = p_ref[0].astype(F32)
                for i in range(1, p):
                    g = g + p_ref[i].astype(F32)
                g_ref[0] = g
                d_ref[0], nm_ref[0], nv_ref[0] = _adamw(w_ref[0], g, m_ref[0], v_ref[0])

    def parts_spec(layer):
        return pl.BlockSpec((p, tr, cdim), lambda d, i: (0, jnp.clip(i + (d - layer) * n_blk, 0, n_blk - 1), 0))

    blk = pl.BlockSpec((1, tr, cdim), lambda d, i: (d, i, 0))
    return pl.pallas_call(
        body, name=name, grid=(depth, n_blk),
        out_shape=[jax.ShapeDtypeStruct(w.shape, F32)] * 4,
        in_specs=[parts_spec(layer) for layer in range(depth)] + [blk, blk, blk],
        out_specs=[blk] * 4,
        compiler_params=pltpu.CompilerParams(dimension_semantics=("arbitrary", "arbitrary"),
                                             vmem_limit_bytes=VMEM_LIMIT),
    )(*parts, w, m, v)


def _adamw_small(ssum, entries, name):
    direct = [e[3] for e in entries if not isinstance(e[3], list)]
    n_direct = len(direct)

    def body(*refs):
        ssum_ref, direct_refs = refs[0], list(refs[1:1 + n_direct])
        ins = refs[1 + n_direct:1 + n_direct + 3 * len(entries)]
        outs = refs[1 + n_direct + 3 * len(entries):]
        for k, (w, _, _, grad) in enumerate(entries):
            w_ref, m_ref, v_ref = ins[3 * k:3 * k + 3]
            g_ref, d_ref, nm_ref, nv_ref = outs[4 * k:4 * k + 4]
            if isinstance(grad, list):
                for row, off in enumerate(grad):
                    rows = slice(row, row + 1)
                    g = ssum_ref[:, off:off + w.shape[1]]
                    g_ref[rows, :] = g
                    d_ref[rows, :], nm_ref[rows, :], nv_ref[rows, :] = _adamw(w_ref[rows, :], g, m_ref[rows, :],
                                                                              v_ref[rows, :])
            else:
                g = direct_refs.pop(0)[...]
                g_ref[...] = g
                d_ref[...], nm_ref[...], nv_ref[...] = _adamw(w_ref[...], g, m_ref[...], v_ref[...])

    vmem = pl.BlockSpec(memory_space=pltpu.VMEM)
    args = [ssum] + direct + [a for e in entries for a in e[:3]]
    res = pl.pallas_call(
        body, name=name,
        out_shape=[jax.ShapeDtypeStruct(e[0].shape, F32) for e in entries for _ in range(4)],
        in_specs=[vmem] * len(args), out_specs=[vmem] * (4 * len(entries)),
    )(*args)
    return [res[4 * k:4 * k + 4] for k in range(len(entries))]


def _reduce_adamw_cols(parts, w, m, v, name):
    depth = len(parts)
    _, r, cdim = parts[0].shape
    tc = 256

    def body(*refs):
        p_refs = refs[:depth]
        w_ref, m_ref, v_ref, g_ref, d_ref, nm_ref, nv_ref = refs[depth:]
        for layer in range(depth):
            g = p_refs[layer][0].astype(F32)
            for i in range(1, parts[layer].shape[0]):
                g = g + p_refs[layer][i].astype(F32)
            g = g.T
            g_ref[:, layer, :] = g
            d_ref[:, layer, :], nm_ref[:, layer, :], nv_ref[:, layer, :] = _adamw(
                w_ref[:, layer, :], g, m_ref[:, layer, :], v_ref[:, layer, :])

    view = lambda a: jnp.transpose(a, (2, 0, 1))
    blk = pl.BlockSpec((cdim, depth, tc), lambda i: (0, 0, i))
    outs = pl.pallas_call(
        body, name=name, grid=(r // tc,),
        out_shape=[jax.ShapeDtypeStruct((cdim, depth, r), F32)] * 4,
        in_specs=[pl.BlockSpec((a.shape[0], tc, cdim), lambda i: (0, i, 0)) for a in parts] + [blk, blk, blk],
        out_specs=[blk] * 4,
        compiler_params=pltpu.CompilerParams(dimension_semantics=("arbitrary",), vmem_limit_bytes=VMEM_LIMIT),
    )(*parts, view(w), view(m), view(v))
    return [jnp.transpose(o, (1, 2, 0)) for o in outs]


def _pad_lanes(v, width=128):
    return jnp.pad(v.reshape(1, -1), ((0, 0), (0, width - v.shape[-1])))


SMALL_FIELDS = (("norm_w", 1024), ("conv_b", 1536), ("dt_bias", 128), ("a_log", 128), ("d_skip", 128),
                ("ssd_norm_w", 1024), ("sinks", 128), ("dw_b", 512), ("ln_w", 512), ("ln_b", 512))


def kernel(x, norm_w, w_in, ssd_conv_w, ssd_conv_b, ssd_dt_bias, ssd_a_log, ssd_d, ssd_norm_w, attn_sinks, conf_dw_w, conf_dw_b, conf_ln_w, conf_ln_b, w_out, final_norm_w, loss_target, m_norm_w, m_w_in, m_ssd_conv_w, m_ssd_conv_b, m_ssd_dt_bias, m_ssd_a_log, m_ssd_d, m_ssd_norm_w, m_attn_sinks, m_conf_dw_w, m_conf_dw_b, m_conf_ln_w, m_conf_ln_b, m_w_out, m_final_norm_w, v_norm_w, v_w_in, v_ssd_conv_w, v_ssd_conv_b, v_ssd_dt_bias, v_ssd_a_log, v_ssd_d, v_ssd_norm_w, v_attn_sinks, v_conf_dw_w, v_conf_dw_b, v_conf_ln_w, v_conf_ln_b, v_w_out, v_final_norm_w):
    nb, seq, _ = x.shape
    depth = norm_w.shape[0]
    t = nb * seq
    me_idx = 4 * lax.axis_index("x") + 2 * lax.axis_index("y") + lax.axis_index("c")

    w_in_bf, w_out_bf = w_in.astype(BF16), w_out.astype(BF16)
    g_win0, g_cw, g_dw = _all_gather([w_in_bf[0], ssd_conv_w, conf_dw_w], "gather_weights")
    w_in_full = [_repack_w_in(g_win0, "repack_w_in_0")]
    w_out_full = []
    conv_w_full = [jnp.transpose(g_cw[:, l], (1, 0, 2)).reshape(4, 1536) for l in range(depth)]
    dw_w_full = [jnp.transpose(g_dw[:, l], (1, 0, 2)).reshape(CONF_KERNEL, 512) for l in range(depth)]

    def layer_params(l):
        return [conv_w_full[l], ssd_conv_b[l].reshape(1, -1), _pad_lanes(ssd_dt_bias[l]), _pad_lanes(ssd_a_log[l]),
                _pad_lanes(ssd_d[l]), ssd_norm_w[l].reshape(1, -1), _pad_lanes(attn_sinks[l]), dw_w_full[l],
                conf_dw_b[l].reshape(1, -1), conf_ln_w[l].reshape(1, -1), conf_ln_b[l].reshape(1, -1)]

    xs = [x.reshape(t, D_MODEL)]
    saved = []
    for l in range(depth):
        (proj, h), gathered = _inproj_fwd(xs[l], norm_w[l].reshape(1, -1), w_in_full[l], f"inproj_fwd_{l}",
                                          (GATHER, [w_out_bf[0]]) if l == 0 else None)
        if l == 0:
            w_out_full.append(gathered[0].reshape(MIX_WIDTH, D_MODEL))
        side = (GATHER, [w_in_bf[l + 1], w_out_bf[l + 1]]) if l + 1 < depth else None
        (x_next, s_all), gathered = _mixer_fwd(xs[l], proj, w_out_full[l], layer_params(l), nb, f"mixer_fwd_{l}",
                                               side)
        if side:
            w_in_full.append(_repack_w_in(gathered[0], f"repack_w_in_{l + 1}"))
            w_out_full.append(gathered[1].reshape(MIX_WIDTH, D_MODEL))
        saved.append((proj, h, s_all))
        xs.append(x_next)
    dx, loss_part, g_fnw = _loss_head(xs[depth], final_norm_w.reshape(1, -1), loss_target.reshape(t, D_MODEL),
                                      "loss_head")

    cols_in = D_IN_PROJ // N_DEV
    rows_out = MIX_WIDTH // N_DEV
    small_rows = [None] * depth
    received = [None] * depth
    outgoing = None
    for l in reversed(range(depth)):
        proj, h, s_all = saved[l]
        res, arrived = _mixer_bwd(dx, proj, s_all, w_out_full[l], layer_params(l), nb, f"mixer_bwd_{l}",
                                  (SCATTER, outgoing) if outgoing else None)
        if outgoing:
            received[l + 1] = arrived
        dproj, y_cat = res[0], res[1]
        g_cw_l, g_cb, g_dtb, g_alog, g_dsk, g_nw, g_snk, g_dww, g_dwb, g_lnw, g_lnb = res[2:]
        gw_out_parts = _gw_out(y_cat, dx, f"gw_out_{l}")
        nw_l = norm_w[l].reshape(1, -1)
        if l > 0:
            (gw_in,), _ = _inproj_bwd_w(h, dproj, f"inproj_bwd_w_{l}")
            (dx, g_norm), _ = _inproj_bwd_x(dproj, w_in_full[l], xs[l], nw_l, dx, f"inproj_bwd_x_{l}")
            outgoing = [_unpack_gw_in(gw_in, f"unpack_gw_in_{l}"), gw_out_parts]
        else:
            (gw_in,), got_out = _inproj_bwd_w(h, dproj, f"inproj_bwd_w_{l}", (SCATTER, [gw_out_parts]))
            chip_parts = _pair_sum(_unpack_gw_in(gw_in, f"unpack_gw_in_{l}"), f"pair_sum_{l}")
            (dx, g_norm), got_in = _inproj_bwd_x(dproj, w_in_full[l], xs[l], nw_l, dx, f"inproj_bwd_x_{l}",
                                                 (CHIP_SCATTER, [chip_parts]))
            received[l] = [got_in[0], got_out[0]]
        small_rows[l] = [g_norm, g_cb, g_dtb, g_alog, g_dsk, g_nw, g_snk, g_dwb, g_lnw, g_lnb,
                         g_cw_l.reshape(1, -1), g_dww.reshape(1, -1)]
    grad_x = dx.reshape(nb, seq, D_MODEL)

    small = jnp.concatenate([piece for l in range(depth) for piece in small_rows[l]] + [g_fnw], axis=1)
    ssum = _exchange_small(small, "exchange_small")

    g_w_in, d_w_in, nm_w_in, nv_w_in = _reduce_adamw_cols([received[l][0] for l in range(depth)], w_in, m_w_in,
                                                          v_w_in, "adamw_w_in")
    g_w_out, d_w_out, nm_w_out, nv_w_out = _reduce_adamw([received[l][1] for l in range(depth)], w_out, m_w_out,
                                                         v_w_out, 256, "adamw_w_out")

    per_layer = sum(n for _, n in SMALL_FIELDS) + 4 * 1536 + CONF_KERNEL * 512
    given = {"norm_w": (norm_w, m_norm_w, v_norm_w), "conv_b": (ssd_conv_b, m_ssd_conv_b, v_ssd_conv_b),
             "dt_bias": (ssd_dt_bias, m_ssd_dt_bias, v_ssd_dt_bias), "a_log": (ssd_a_log, m_ssd_a_log, v_ssd_a_log),
             "d_skip": (ssd_d, m_ssd_d, v_ssd_d), "ssd_norm_w": (ssd_norm_w, m_ssd_norm_w, v_ssd_norm_w),
             "sinks": (attn_sinks, m_attn_sinks, v_attn_sinks), "dw_b": (conf_dw_b, m_conf_dw_b, v_conf_dw_b),
             "ln_w": (conf_ln_w, m_conf_ln_w, v_conf_ln_w), "ln_b": (conf_ln_b, m_conf_ln_b, v_conf_ln_b)}
    entries = []
    off = 0
    for fname, n in SMALL_FIELDS:
        entries.append((*given[fname], [l * per_layer + off for l in range(depth)]))
        off += n
    shard_grads = []
    for width, shard, kk in ((1536, 192, 4), (512, 64, CONF_KERNEL)):
        full = [ssum[:, l * per_layer + off:l * per_layer + off + kk * width].reshape(kk, width) for l in range(depth)]
        shard_grads.append(jnp.stack([lax.dynamic_slice(f, (0, me_idx * shard), (kk, shard)) for f in full], axis=0))
        off += kk * width
    entries.append((ssd_conv_w, m_ssd_conv_w, v_ssd_conv_w, shard_grads[0]))
    entries.append((conf_dw_w, m_conf_dw_w, v_conf_dw_w, shard_grads[1]))
    entries.append((final_norm_w.reshape(1, -1), m_final_norm_w.reshape(1, -1), v_final_norm_w.reshape(1, -1),
                    [depth * per_layer]))
    sm = _adamw_small(ssum, entries, "adamw_small")
    sm = {k: quad for k, quad in zip([f for f, _ in SMALL_FIELDS] + ["conv_w", "dw_w", "final"], sm)}

    def outputs(i, big_in_i, big_out_i):
        return [sm["norm_w"][i], big_in_i, sm["conv_w"][i], sm["conv_b"][i], sm["dt_bias"][i], sm["a_log"][i],
                sm["d_skip"][i], sm["ssd_norm_w"][i], sm["sinks"][i], sm["dw_w"][i], sm["dw_b"][i], sm["ln_w"][i],
                sm["ln_b"][i], big_out_i, sm["final"][i].reshape(-1)]

    loss = lax.psum(loss_part[0, 0], MESH_AXES)
    return (loss, grad_x, *outputs(0, g_w_in, g_w_out), *outputs(1, d_w_in, d_w_out),
            *outputs(2, nm_w_in, nm_w_out), *outputs(3, nv_w_in, nv_w_out))
```

```python
import functools

import jax
import jax.numpy as jnp
from jax import lax
from jax.experimental import pallas as pl
from jax.experimental.pallas import tpu as pltpu

F32 = jnp.float32
BF16 = jnp.bfloat16
MESH_AXES = ("x", "y", "c")
N_DEV = 8
EPS = 1e-5

D_MODEL = 1024
CHUNK = 128
SSD_HEADS = 16
SSD_HEAD_DIM = 64
SSD_STATE = 128
ATTN_HEADS = 8
ATTN_HEAD_DIM = 64
CONF_KERNEL = 31
MIX_WIDTH = 2048
D_IN_PROJ = 5392
C_Z = 0
C_CA = 2048
C_XBC = 3072
C_Q = 4608
C_K = 5120
C_V = 5248
C_DT = 5376
PROJ_W = 5632
N_COL_TILES = 4
COL_TILE = PROJ_W // N_COL_TILES
XBC_HALO = 8
CONF_HALO = 32
ATTN_STACK_FWD = 2
ATTN_STACK_BWD = 4
VMEM_LIMIT = 56 * 1024 * 1024

ADAM_LR = 0.001
ADAM_B1 = 0.9
ADAM_B2 = 0.999
ADAM_EPS = 1e-08
ADAM_WD = 0.01
ADAM_STEP = 10


def _silu(v):
    return v * jax.nn.sigmoid(v)


def _softplus(v):
    return jnp.maximum(v, 0.0) + jnp.log1p(jnp.exp(-jnp.abs(v)))


def _rmsnorm(v, w):
    return v * lax.rsqrt(jnp.mean(v * v, axis=-1, keepdims=True) + EPS) * w


def _dot(a, b):
    return jnp.dot(a.astype(BF16), b.astype(BF16), preferred_element_type=F32)


def _dot_nt(a, b):
    return lax.dot_general(a.astype(BF16), b.astype(BF16), (((1,), (1,)), ((), ())), preferred_element_type=F32)


def _dot_tn(a, b):
    return lax.dot_general(a.astype(BF16), b.astype(BF16), (((0,), (0,)), ((), ())), preferred_element_type=F32)


def _taps(ext, offs, out_len, w=None, g=None):
    n_rows, n_cols = ext.shape
    by_shift = {}
    for t, off in enumerate(offs):
        by_shift.setdefault(off % 8, []).append((t, off))
    for r, taps in by_shift.items():
        assert max(off for _, off in taps) - r + out_len <= n_rows - r
    accs = []
    sums = [[None] * (n_cols // 128) for _ in offs]
    for blk in range(n_cols // 128):
        cs = slice(blk * 128, (blk + 1) * 128)
        e = ext[:, cs]
        acc = None
        for r, taps in by_shift.items():
            shifted = e if r == 0 else pltpu.roll(e, n_rows - r, axis=0)
            for t, off in taps:
                window = shifted[off - r:off - r + out_len, :]
                if w is not None:
                    term = w[t:t + 1, cs] * window
                    acc = term if acc is None else acc + term
                if g is not None:
                    sums[t][blk] = jnp.sum(g[:, cs] * window, axis=0, keepdims=True)
        accs.append(acc)
    if w is not None:
        return jnp.concatenate(accs, axis=1)
    return jnp.concatenate([jnp.concatenate(row, axis=1) for row in sums], axis=0)


@functools.partial(jax.custom_vjp, nondiff_argnums=(3,))
def _dwconv(ext, w, b, halo):
    kk = w.shape[0]
    return b + _taps(ext, [halo - (kk - 1) + t for t in range(kk)], ext.shape[0] - halo, w=w)


def _dwconv_fwd(ext, w, b, halo):
    return _dwconv(ext, w, b, halo), (ext, w)


def _dwconv_bwd(halo, res, g):
    ext, w = res
    kk = w.shape[0]
    offs = [halo - (kk - 1) + t for t in range(kk)]
    dw = _taps(ext, offs, ext.shape[0] - halo, g=g)
    zeros = jnp.zeros((halo, g.shape[1]), g.dtype)
    gp = jnp.concatenate([zeros, g, zeros], axis=0)
    dext = _taps(gp, [halo - off for off in offs], ext.shape[0], w=w)
    return dext, dw, jnp.sum(g, axis=0, keepdims=True)


_dwconv.defvjp(_dwconv_fwd, _dwconv_bwd)


def _ssd_part(z_ssd, conv_out, dtr, s_in, dtb, alog, dsk, nw):
    qn = conv_out.shape[0]
    nh = SSD_HEADS
    per_group = nh // 2
    n_pair = nh // 2
    xa = _silu(conv_out)
    xs = xa[:, 0:1024]
    dt = _softplus(dtr + dtb)
    a = dt * (-jnp.exp(alog))
    rows = lax.broadcasted_iota(jnp.int32, (qn, qn), 0)
    cols = lax.broadcasted_iota(jnp.int32, (qn, qn), 1)
    causal = rows >= cols
    low = cols < SSD_HEAD_DIM
    a_cs = jnp.dot(causal.astype(F32), a, precision=lax.Precision.HIGHEST, preferred_element_type=F32)
    a_cs_t = a_cs.T
    bgs = [xa[:, 1024 + g * 128:1024 + (g + 1) * 128] for g in range(2)]
    cgs = [xa[:, 1280 + g * 128:1280 + (g + 1) * 128] for g in range(2)]
    cbms = [_dot_nt(cgs[g], bgs[g]) for g in range(2)]
    colb = [jnp.broadcast_to(a_cs[:, h:h + 1], (qn, qn)) for h in range(nh)]
    lastb = [jnp.broadcast_to(colb[h][qn - 1:qn, :], (qn, qn)) for h in range(nh)]
    dtb_wide = [jnp.broadcast_to(dt[:, h:h + 1], (qn, qn)) for h in range(nh)]
    lmats = [jnp.exp(jnp.where(causal, colb[h] - a_cs_t[h:h + 1, :], -jnp.inf)) for h in range(nh)]
    ms = [cbms[h // per_group] * lmats[h] for h in range(nh)]
    x_pair = [xs[:, p * 128:(p + 1) * 128] for p in range(n_pair)]
    x_lo = [x_pair[p] * jnp.where(low, dtb_wide[2 * p], 0.0) for p in range(n_pair)]
    x_hi = [x_pair[p] * jnp.where(low, 0.0, dtb_wide[2 * p + 1]) for p in range(n_pair)]
    y_diag = [_dot(ms[2 * p], x_lo[p]) + _dot(ms[2 * p + 1], x_hi[p]) for p in range(n_pair)]
    col_pair = [jnp.where(low, colb[2 * p], colb[2 * p + 1]) for p in range(n_pair)]
    last_pair = [jnp.where(low, lastb[2 * p], lastb[2 * p + 1]) for p in range(n_pair)]
    ecol = [jnp.exp(col_pair[p]) for p in range(n_pair)]
    xw = [(x_lo[p] + x_hi[p]) * jnp.exp(last_pair[p] - col_pair[p]) for p in range(n_pair)]
    y_off, st = [], []
    for g in range(2):
        ps = range(g * n_pair // 2, (g + 1) * n_pair // 2)
        y_off.append(_dot_nt(cgs[g], s_in[g * 512:(g + 1) * 512, :]) * jnp.concatenate([ecol[p] for p in ps], axis=1))
        st.append(_dot_tn(jnp.concatenate([xw[p] for p in ps], axis=1), bgs[g]))
    e_last = jnp.exp(jnp.broadcast_to(a_cs_t[:, qn - 1:qn], (qn, SSD_STATE)))
    scale = jnp.concatenate([jnp.broadcast_to(e_last[h:h + 1, :], (64, SSD_STATE)) for h in range(nh)], axis=0)
    s_out = scale * s_in + jnp.concatenate(st, axis=0)
    d_wide = jnp.concatenate([jnp.broadcast_to(dsk[:, h:h + 1], (1, 64)) for h in range(nh)], axis=1)
    y = jnp.concatenate(y_diag, axis=1) + jnp.concatenate(y_off, axis=1) + d_wide * xs
    gated = y * _silu(z_ssd)
    halves = []
    for g in range(2):
        gg = gated[:, g * 512:(g + 1) * 512]
        halves.append(gg * lax.rsqrt(jnp.mean(gg * gg, axis=-1, keepdims=True) + EPS))
    return jnp.concatenate(halves, axis=1) * nw, s_out


def _attn_part(z_attn, q, kv, p_kv, snk, kvmask, stack):
    qn = q.shape[0]
    kk = jnp.concatenate([p_kv[:, 0:128], kv[:, 0:128]], axis=0)
    vv = jnp.concatenate([p_kv[:, 128:256], kv[:, 128:256]], axis=0)
    units = range(ATTN_HEADS // stack)
    heads = [range(u * stack, (u + 1) * stack) for u in units]
    kv_of = [u * stack // (ATTN_HEADS // 2) for u in units]
    k_of = [kk[:, g * 64:(g + 1) * 64] for g in kv_of]
    v_of = [vv[:, g * 64:(g + 1) * 64] for g in kv_of]
    qs = [jnp.concatenate([q[:, h * 64:(h + 1) * 64] for h in heads[u]], axis=0) for u in units]
    sk = [jnp.concatenate([jnp.broadcast_to(snk[:, h:h + 1], (qn, 1)) for h in heads[u]], axis=0) for u in units]
    s = [jnp.where(kvmask, _dot_nt(qs[u], k_of[u]) * (ATTN_HEAD_DIM ** -0.5), -jnp.inf) for u in units]
    m = [lax.stop_gradient(jnp.maximum(jnp.max(s[u], axis=-1, keepdims=True), sk[u])) for u in units]
    e = [jnp.exp(s[u] - m[u]) for u in units]
    r_den = [1.0 / (jnp.sum(e[u], axis=-1, keepdims=True) + jnp.exp(sk[u] - m[u])) for u in units]
    o = [_dot(e[u], v_of[u]) * r_den[u] for u in units]
    outs = [o[u][i * qn:(i + 1) * qn, :] for u in units for i in range(stack)]
    return jnp.concatenate(outs, axis=1) * _silu(z_attn)


def _conf_glu(cacg, p_cc):
    c0 = cacg[:, 0:512] * jax.nn.sigmoid(cacg[:, 512:1024])
    pc0 = p_cc[:, 0:512] * jax.nn.sigmoid(p_cc[:, 512:1024])
    return jnp.concatenate([pc0, c0], axis=0)


def _conf_tail(conv_out, z_conf, lnw, lnb):
    xc = conv_out - jnp.mean(conv_out, axis=-1, keepdims=True)
    yln = xc * lax.rsqrt(jnp.mean(xc * xc, axis=-1, keepdims=True) + EPS) * lnw + lnb
    return _silu(yln) * _silu(z_conf)


def _kv_mask(qn, not_first, reps):
    ii = lax.broadcasted_iota(jnp.int32, (reps * qn, 2 * qn), 0) & (qn - 1)
    jj = lax.broadcasted_iota(jnp.int32, (reps * qn, 2 * qn), 1)
    d = jj - ii
    return (d >= 1) & (d <= qn) & (not_first | (jj >= qn))


def _my_place():
    return lax.axis_index("x"), lax.axis_index("y"), lax.axis_index("c")


def _all_gather(arrs, name):
    n = len(arrs)

    def body(*refs):
        ins, outs = refs[:n], refs[n:2 * n]
        send_sems, recv_sems, local_sems = refs[2 * n:]
        x, y, c = _my_place()
        me, sibling = (x, y, c), (x, y, 1 - c)
        chips = [(1 - x, y), (x, 1 - y), (1 - x, 1 - y)]

        def slot(a, p):
            return outs[a].at[4 * p[0] + 2 * p[1] + p[2]]

        def copy(a, kk, block, to, src=None):
            return pltpu.make_async_remote_copy(
                src_ref=slot(a, block) if src is None else src, dst_ref=slot(a, block),
                send_sem=send_sems.at[a, kk], recv_sem=recv_sems.at[a, kk],
                device_id=to, device_id_type=pl.DeviceIdType.MESH)

        mine = [pltpu.make_async_copy(ins[a], slot(a, me), local_sems.at[a]) for a in range(n)]
        for cp in mine:
            cp.start()
        first = []
        for a in range(n):
            first.append(copy(a, 0, me, sibling, src=ins[a]))
            first += [copy(a, 1 + j, me, (*chip, c), src=ins[a]) for j, chip in enumerate(chips)]
        for cp in first:
            cp.start()
        passed = []
        for j, chip in enumerate(chips):
            for a in range(n):
                copy(a, 1 + j, (*chip, c), me).wait_recv()
                fwd = copy(a, 4 + j, (*chip, c), sibling)
                fwd.start()
                passed.append(fwd)
        for a in range(n):
            copy(a, 0, sibling, me).wait_recv()
            for j, chip in enumerate(chips):
                copy(a, 4 + j, (*chip, 1 - c), me).wait_recv()
        for cp in first + passed:
            cp.wait_send()
        for cp in mine:
            cp.wait()

    any_spec = pl.BlockSpec(memory_space=pl.ANY)
    return pl.pallas_call(
        body, name=name,
        out_shape=[jax.ShapeDtypeStruct((N_DEV,) + a.shape, a.dtype) for a in arrs],
        in_specs=[any_spec] * n, out_specs=[any_spec] * n,
        scratch_shapes=[pltpu.SemaphoreType.DMA((n, 7)), pltpu.SemaphoreType.DMA((n, 7)),
                        pltpu.SemaphoreType.DMA((n,))],
    )(*arrs)


GATHER, SCATTER, CHIP_SCATTER = "gather", "scatter", "chip_scatter"


def _direct_copies(mode, ins, outs, send_sems, recv_sems, local_sems):
    x, y, c = _my_place()
    by_chip = mode == CHIP_SCATTER
    place = (lambda px, py, pc: 2 * px + py) if by_chip else (lambda px, py, pc: 4 * px + 2 * py + pc)
    me_idx = place(x, y, c)
    n = len(ins)
    local = [pltpu.make_async_copy(ins[a] if mode == GATHER else ins[a].at[me_idx], outs[a].at[me_idx],
                                   local_sems.at[a]) for a in range(n)]
    remote = []
    for rel in range(1, N_DEV):
        if by_chip and rel & 1:
            continue
        px = 1 - x if rel & 4 else x
        py = 1 - y if rel & 2 else y
        pc = 1 - c if rel & 1 else c
        for a in range(n):
            remote.append(pltpu.make_async_remote_copy(
                src_ref=ins[a] if mode == GATHER else ins[a].at[place(px, py, pc)], dst_ref=outs[a].at[me_idx],
                send_sem=send_sems.at[a, rel - 1], recv_sem=recv_sems.at[a, rel - 1],
                device_id=(px, py, pc), device_id_type=pl.DeviceIdType.MESH))
    return local + remote


def _side_scratch(n):
    return [pltpu.SemaphoreType.DMA((n, 7)), pltpu.SemaphoreType.DMA((n, 7)), pltpu.SemaphoreType.DMA((n,))]


def _side_out_shapes(mode, arrs):
    return [jax.ShapeDtypeStruct((N_DEV,) + a.shape if mode == GATHER else a.shape, a.dtype) for a in arrs]


def _pallas_with_side(body, side, first, last, n_in, n_out, *, in_specs, out_specs, out_shape, scratch_shapes=(),
                      args, **kwargs):
    side_arrs = [] if side is None else list(side[1])
    ns = len(side_arrs)

    def wrapped(*refs):
        own_in, side_in = refs[:n_in], refs[n_in:n_in + ns]
        o = n_in + ns
        own_out, side_out = refs[o:o + n_out], refs[o + n_out:o + n_out + ns]
        scratch = refs[o + n_out + ns:]
        own_scratch, sems = (scratch[:-3], scratch[-3:]) if ns else (scratch, ())
        if ns:
            @pl.when(first())
            def _():
                for cp in _direct_copies(side[0], side_in, side_out, *sems):
                    cp.start()

        body(*own_in, *own_out, *own_scratch)
        if ns:
            @pl.when(last())
            def _():
                for cp in _direct_copies(side[0], side_in, side_out, *sems):
                    cp.wait()

    any_spec = pl.BlockSpec(memory_space=pl.ANY)
    res = pl.pallas_call(
        wrapped,
        in_specs=list(in_specs) + [any_spec] * ns,
        out_specs=list(out_specs) + [any_spec] * ns,
        out_shape=list(out_shape) + (_side_out_shapes(side[0], side_arrs) if ns else []),
        scratch_shapes=list(scratch_shapes) + (_side_scratch(ns) if ns else []),
        **kwargs,
    )(*args, *side_arrs)
    return res[:n_out], res[n_out:]


def _exchange_small(small, name):
    ns = small.shape[1]

    def body(small_ref, sum_ref, small_all, send_sems, recv_sems, local_sems):
        copies = _direct_copies(GATHER, [small_ref], [small_all], send_sems, recv_sems, local_sems)
        for cp in copies:
            cp.start()
        for cp in copies:
            cp.wait()
        total = small_all[0]
        for i in range(1, N_DEV):
            total = total + small_all[i]
        sum_ref[...] = total

    vmem_spec = pl.BlockSpec(memory_space=pltpu.VMEM)
    return pl.pallas_call(
        body, name=name,
        out_shape=jax.ShapeDtypeStruct((1, ns), F32),
        in_specs=[vmem_spec], out_specs=vmem_spec,
        scratch_shapes=[pltpu.VMEM((N_DEV, 1, ns), F32)] + _side_scratch(1),
    )(small)


def _full(shape):
    return pl.BlockSpec(shape, lambda *_: (0,) * len(shape))


def _inproj_fwd(x, nw, w, name, side=None):
    t = x.shape[0]
    tm = 256

    def body(x_ref, nw_ref, w_ref, proj_ref, h_ref):
        h = _rmsnorm(x_ref[...], nw_ref[...]).astype(BF16)
        h_ref[...] = h
        for j in range(N_COL_TILES):
            sl = slice(j * COL_TILE, (j + 1) * COL_TILE)
            proj_ref[:, sl] = jnp.dot(h, w_ref[:, sl], preferred_element_type=F32)

    grid = (t // tm,)
    return _pallas_with_side(
        body, side, *_grid_ends(grid), 3, 2, name=name, grid=grid,
        out_shape=[jax.ShapeDtypeStruct((t, PROJ_W), F32), jax.ShapeDtypeStruct((t, D_MODEL), BF16)],
        in_specs=[pl.BlockSpec((tm, D_MODEL), lambda i: (i, 0)), _full((1, D_MODEL)), _full((D_MODEL, PROJ_W))],
        out_specs=[pl.BlockSpec((tm, PROJ_W), lambda i: (i, 0)), pl.BlockSpec((tm, D_MODEL), lambda i: (i, 0))],
        compiler_params=pltpu.CompilerParams(dimension_semantics=("arbitrary",), vmem_limit_bytes=VMEM_LIMIT),
        args=(x, nw, w))


def _param_specs():
    return [_full((4, 1536)), _full((1, 1536)), _full((1, 128)), _full((1, 128)), _full((1, 128)),
            _full((1, 1024)), _full((1, 128)), _full((CONF_KERNEL, 512)), _full((1, 512)), _full((1, 512)),
            _full((1, 512))]


def _halo_specs(nc, chunk_of):
    def prev_chunk(b, j):
        return jnp.maximum(b * nc + chunk_of(j) - 1, 0)

    per_xbc = CHUNK // XBC_HALO
    per_cc = CHUNK // CONF_HALO
    return [
        pl.BlockSpec((XBC_HALO, 1536), lambda b, j: (prev_chunk(b, j) * per_xbc + per_xbc - 1, C_XBC // 1536)),
        pl.BlockSpec((CHUNK, 256), lambda b, j: (prev_chunk(b, j), C_K // 256)),
        pl.BlockSpec((CONF_HALO, 1024), lambda b, j: (prev_chunk(b, j) * per_cc + per_cc - 1, C_CA // 1024)),
    ]


def _grid_ends(grid):
    first = lambda: functools.reduce(lambda p, q: p & q, [pl.program_id(i) == 0 for i in range(len(grid))])
    last = lambda: functools.reduce(lambda p, q: p & q, [pl.program_id(i) == n - 1 for i, n in enumerate(grid)])
    return first, last


def _mixer_fwd(x, proj, w_out, params, nb, name, side=None):
    t = x.shape[0]
    nc = t // nb // CHUNK

    def body(x_ref, cur_ref, pxbc_ref, pkv_ref, pcc_ref, wo_ref, *rest):
        prm = [r[...] for r in rest[:11]]
        xn_ref, sall_ref, conv_ref, s_scr = rest[11:]
        c = pl.program_id(1)
        not_first = c > 0
        nf = not_first.astype(F32)

        @pl.when(c == 0)
        def _():
            s_scr[...] = jnp.zeros_like(s_scr)

        cw, cb, dtb, alog, dsk, nw, snk, dww, dwb, lnw, lnb = prm
        s_in = s_scr[...]
        sall_ref[0] = s_in
        ssd_conv = _dwconv(jnp.concatenate([pxbc_ref[...] * nf, cur_ref[:, C_XBC:C_XBC + 1536]], axis=0), cw, cb,
                           XBC_HALO)
        y_ssd, s_out = _ssd_part(cur_ref[:, 0:1024], ssd_conv, cur_ref[:, C_DT:C_DT + 128], s_in, dtb, alog, dsk, nw)
        s_scr[...] = s_out
        y_attn = _attn_part(cur_ref[:, 1024:1536], cur_ref[:, C_Q:C_Q + 512], cur_ref[:, C_K:C_K + 256],
                            pkv_ref[...] * nf, snk, _kv_mask(CHUNK, not_first, ATTN_STACK_FWD), ATTN_STACK_FWD)
        conv_out = _dwconv(_conf_glu(cur_ref[:, C_CA:C_CA + 1024], pcc_ref[...] * nf), dww, dwb, CONF_HALO)
        conv_ref[...] = conv_out
        y_conf = _conf_tail(conv_out, cur_ref[:, 1536:2048], lnw, lnb)
        xn_ref[...] = (x_ref[...] + _dot(y_ssd, wo_ref[0:1024, :]) + _dot(y_attn, wo_ref[1024:1536, :])
                       + _dot(y_conf, wo_ref[1536:2048, :]))

    row = lambda b, j: (b * nc + j, 0)
    grid = (nb, nc)
    return _pallas_with_side(
        body, side, *_grid_ends(grid), 17, 3, name=name, grid=grid,
        out_shape=[jax.ShapeDtypeStruct((t, D_MODEL), F32),
                   jax.ShapeDtypeStruct((nb * nc, SSD_HEADS * SSD_HEAD_DIM, SSD_STATE), F32),
                   jax.ShapeDtypeStruct((t, 512), F32)],
        in_specs=[pl.BlockSpec((CHUNK, D_MODEL), row), pl.BlockSpec((CHUNK, PROJ_W), row)]
                 + _halo_specs(nc, lambda j: j) + [_full((MIX_WIDTH, D_MODEL))] + _param_specs(),
        out_specs=[pl.BlockSpec((CHUNK, D_MODEL), row),
                   pl.BlockSpec((1, SSD_HEADS * SSD_HEAD_DIM, SSD_STATE), lambda b, j: (b * nc + j, 0, 0)),
                   pl.BlockSpec((CHUNK, 512), row)],
        scratch_shapes=[pltpu.VMEM((SSD_HEADS * SSD_HEAD_DIM, SSD_STATE), F32)],
        compiler_params=pltpu.CompilerParams(dimension_semantics=("arbitrary", "arbitrary"),
                                             vmem_limit_bytes=VMEM_LIMIT),
        args=(x, proj, proj, proj, proj, w_out, *params))


def _mixer_bwd(dxn, proj, s_all, conv_all, w_out, params, nb, name, side=None):
    t = dxn.shape[0]
    nc = t // nb // CHUNK
    n_prm = 11

    def body(dxn_ref, cur_ref, pxbc_ref, pkv_ref, pcc_ref, s_ref, conv_ref, wo_ref, *rest):
        prm = [r[...] for r in rest[:n_prm]]
        dproj_ref, ycat_ref = rest[n_prm:n_prm + 2]
        gprm = rest[n_prm + 2:2 * n_prm + 2]
        ds_scr, pend_xbc, pend_kv, pend_cc = rest[2 * n_prm + 2:]
        b, j = pl.program_id(0), pl.program_id(1)
        c = nc - 1 - j
        not_first = c > 0
        nf = not_first.astype(F32)

        @pl.when((b == 0) & (j == 0))
        def _():
            for r in gprm:
                r[...] = jnp.zeros_like(r)

        @pl.when(j == 0)
        def _():
            ds_scr[...] = jnp.zeros_like(ds_scr)
            pend_xbc[...] = jnp.zeros_like(pend_xbc)
            pend_kv[...] = jnp.zeros_like(pend_kv)
            pend_cc[...] = jnp.zeros_like(pend_cc)

        cw, cb, dtb, alog, dsk, nw, snk, dww, dwb, lnw, lnb = prm
        g_cw, g_cb, g_dtb, g_alog, g_dsk, g_nw, g_snk, g_dww, g_dwb, g_lnw, g_lnb = gprm
        dxn_v = dxn_ref[...]

        def add_tail(d_cur, pending):
            lead = jnp.zeros((CHUNK - pending.shape[0], pending.shape[1]), F32)
            return d_cur + jnp.concatenate([lead, pending], axis=0)

        y, vjp = jax.vjp(_conf_tail, conv_ref[...], cur_ref[:, 1536:2048], lnw, lnb)
        ycat_ref[:, 1536:2048] = y.astype(BF16)
        d_conv, dz, d_lnw, d_lnb = vjp(_dot_nt(dxn_v, wo_ref[1536:2048, :]))
        ext, vjp = jax.vjp(_conf_glu, cur_ref[:, C_CA:C_CA + 1024], pcc_ref[...] * nf)
        d_ext, d_dww, d_dwb = _dwconv_bwd(CONF_HALO, (ext, dww), d_conv)
        dcacg, dpcc = vjp(d_ext)
        dproj_ref[:, 1536:2048] = dz.astype(BF16)
        dproj_ref[:, C_CA:C_CA + 1024] = add_tail(dcacg, pend_cc[...]).astype(BF16)
        pend_cc[...] = dpcc
        for r, g in ((g_dww, d_dww), (g_dwb, d_dwb), (g_lnw, d_lnw), (g_lnb, d_lnb)):
            r[...] += g

        attn = functools.partial(_attn_part, kvmask=_kv_mask(CHUNK, not_first, ATTN_STACK_BWD),
                                 stack=ATTN_STACK_BWD)
        y, vjp = jax.vjp(attn, cur_ref[:, 1024:1536], cur_ref[:, C_Q:C_Q + 512], cur_ref[:, C_K:C_K + 256],
                         pkv_ref[...] * nf, snk)
        ycat_ref[:, 1024:1536] = y.astype(BF16)
        dz, dq, dkv, dpkv, d_snk = vjp(_dot_nt(dxn_v, wo_ref[1024:1536, :]))
        dproj_ref[:, 1024:1536] = dz.astype(BF16)
        dproj_ref[:, C_Q:C_Q + 512] = dq.astype(BF16)
        dproj_ref[:, C_K:C_K + 256] = (dkv + pend_kv[...]).astype(BF16)
        pend_kv[...] = dpkv
        g_snk[...] += d_snk

        ext = jnp.concatenate([pxbc_ref[...] * nf, cur_ref[:, C_XBC:C_XBC + 1536]], axis=0)
        (y, _), vjp = jax.vjp(_ssd_part, cur_ref[:, 0:1024], _dwconv(ext, cw, cb, XBC_HALO),
                              cur_ref[:, C_DT:C_DT + 128], s_ref[0], dtb, alog, dsk, nw)
        ycat_ref[:, 0:1024] = y.astype(BF16)
        dz, d_conv, ddtr, ds_in, d_dtb, d_alog, d_dsk, d_nw = vjp((_dot_nt(dxn_v, wo_ref[0:1024, :]), ds_scr[...]))
        d_ext, d_cw, d_cb = _dwconv_bwd(XBC_HALO, (ext, cw), d_conv)
        dpxbc, dxbc = d_ext[0:XBC_HALO, :], d_ext[XBC_HALO:, :]
        dproj_ref[:, 0:1024] = dz.astype(BF16)
        dproj_ref[:, C_XBC:C_XBC + 1536] = add_tail(dxbc, pend_xbc[...]).astype(BF16)
        dproj_ref[:, C_DT:C_DT + 128] = ddtr.astype(BF16)
        dproj_ref[:, C_DT + 128:PROJ_W] = jnp.zeros((CHUNK, PROJ_W - C_DT - 128), BF16)
        pend_xbc[...] = dpxbc
        ds_scr[...] = ds_in
        for r, g in ((g_cw, d_cw), (g_cb, d_cb), (g_dtb, d_dtb), (g_alog, d_alog), (g_dsk, d_dsk), (g_nw, d_nw)):
            r[...] += g

    row = lambda b, j: (b * nc + nc - 1 - j, 0)
    prm_shapes = [(4, 1536), (1, 1536), (1, 128), (1, 128), (1, 128), (1, 1024), (1, 128), (CONF_KERNEL, 512),
                  (1, 512), (1, 512), (1, 512)]
    grid = (nb, nc)
    return _pallas_with_side(
        body, side, *_grid_ends(grid), 8 + n_prm, 2 + n_prm, name=name, grid=grid,
        out_shape=[jax.ShapeDtypeStruct((t, PROJ_W), BF16), jax.ShapeDtypeStruct((t, MIX_WIDTH), BF16)]
                  + [jax.ShapeDtypeStruct(s, F32) for s in prm_shapes],
        in_specs=[pl.BlockSpec((CHUNK, D_MODEL), row), pl.BlockSpec((CHUNK, PROJ_W), row)]
                 + _halo_specs(nc, lambda j: nc - 1 - j)
                 + [pl.BlockSpec((1, SSD_HEADS * SSD_HEAD_DIM, SSD_STATE), lambda b, j: (b * nc + nc - 1 - j, 0, 0)),
                    pl.BlockSpec((CHUNK, 512), row), _full((MIX_WIDTH, D_MODEL))] + _param_specs(),
        out_specs=[pl.BlockSpec((CHUNK, PROJ_W), row), pl.BlockSpec((CHUNK, MIX_WIDTH), row)]
                  + [_full(s) for s in prm_shapes],
        scratch_shapes=[pltpu.VMEM((SSD_HEADS * SSD_HEAD_DIM, SSD_STATE), F32), pltpu.VMEM((XBC_HALO, 1536), F32),
                        pltpu.VMEM((CHUNK, 256), F32), pltpu.VMEM((CONF_HALO, 1024), F32)],
        compiler_params=pltpu.CompilerParams(dimension_semantics=("arbitrary", "arbitrary"),
                                             vmem_limit_bytes=VMEM_LIMIT),
        args=(dxn, proj, proj, proj, proj, s_all, conv_all, w_out, *params))


def _gw_out(y_cat, dxn, name):
    t = y_cat.shape[0]
    tk = 512

    def body(y_ref, dxn_ref, out_ref, acc):
        k = pl.program_id(0)

        @pl.when(k == 0)
        def _():
            acc[...] = jnp.zeros_like(acc)

        acc[...] += _dot_tn(y_ref[...], dxn_ref[...])

        @pl.when(k == t // tk - 1)
        def _():
            out_ref[...] = acc[...].astype(BF16)

    out = pl.pallas_call(
        body, name=name, grid=(t // tk,),
        out_shape=jax.ShapeDtypeStruct((MIX_WIDTH, D_MODEL), BF16),
        in_specs=[pl.BlockSpec((tk, MIX_WIDTH), lambda k: (k, 0)), pl.BlockSpec((tk, D_MODEL), lambda k: (k, 0))],
        out_specs=_full((MIX_WIDTH, D_MODEL)),
        scratch_shapes=[pltpu.VMEM((MIX_WIDTH, D_MODEL), F32)],
        compiler_params=pltpu.CompilerParams(dimension_semantics=("arbitrary",), vmem_limit_bytes=VMEM_LIMIT),
    )(y_cat, dxn)
    return out.reshape(N_DEV, MIX_WIDTH // N_DEV, D_MODEL)


def _inproj_bwd_x(dproj, w, x, nw, dxn, name, side=None):
    t = x.shape[0]
    tm = 256

    def body(dp_ref, w_ref, x_ref, nw_ref, dxn_ref, dx_ref, gnw_ref):
        @pl.when(pl.program_id(0) == 0)
        def _():
            gnw_ref[...] = jnp.zeros_like(gnw_ref)

        dh = jnp.zeros((tm, D_MODEL), F32)
        for j in range(N_COL_TILES):
            sl = slice(j * COL_TILE, (j + 1) * COL_TILE)
            dh = dh + _dot_nt(dp_ref[:, sl], w_ref[:, sl])
        _, vjp = jax.vjp(_rmsnorm, x_ref[...], nw_ref[...])
        dx, dnw = vjp(dh)
        dx_ref[...] = dxn_ref[...] + dx
        gnw_ref[...] += dnw

    tok = lambda i: (i, 0)
    grid = (t // tm,)
    return _pallas_with_side(
        body, side, *_grid_ends(grid), 5, 2, name=name, grid=grid,
        out_shape=[jax.ShapeDtypeStruct((t, D_MODEL), F32), jax.ShapeDtypeStruct((1, D_MODEL), F32)],
        in_specs=[pl.BlockSpec((tm, PROJ_W), tok), _full((D_MODEL, PROJ_W)), pl.BlockSpec((tm, D_MODEL), tok),
                  _full((1, D_MODEL)), pl.BlockSpec((tm, D_MODEL), tok)],
        out_specs=[pl.BlockSpec((tm, D_MODEL), tok), _full((1, D_MODEL))],
        compiler_params=pltpu.CompilerParams(dimension_semantics=("arbitrary",), vmem_limit_bytes=VMEM_LIMIT),
        args=(dproj, w, x, nw, dxn))


def _inproj_bwd_w(h, dproj, name, side=None):
    t = h.shape[0]
    tk = 512

    def body(h_ref, dp_ref, gw_ref):
        @pl.when(pl.program_id(1) == 0)
        def _():
            gw_ref[...] = jnp.zeros_like(gw_ref)

        gw_ref[...] += _dot_tn(h_ref[...], dp_ref[...])

    grid = (N_COL_TILES, t // tk)
    return _pallas_with_side(
        body, side, *_grid_ends(grid), 2, 1, name=name, grid=grid,
        out_shape=[jax.ShapeDtypeStruct((D_MODEL, PROJ_W), F32)],
        in_specs=[pl.BlockSpec((tk, D_MODEL), lambda n, k: (k, 0)), pl.BlockSpec((tk, COL_TILE), lambda n, k: (k, n))],
        out_specs=[pl.BlockSpec((D_MODEL, COL_TILE), lambda n, k: (0, n))],
        compiler_params=pltpu.CompilerParams(dimension_semantics=("arbitrary", "arbitrary"),
                                             vmem_limit_bytes=VMEM_LIMIT),
        args=(h, dproj))


def _repack_runs():
    pieces = ((0, 2048, C_Z), (2048, 3584, C_XBC), (3584, 3600, C_DT), (3600, 4368, C_Q), (4368, D_IN_PROJ, C_CA))
    per = D_IN_PROJ // N_DEV
    runs = []
    for j in range(N_DEV):
        lo, hi = per * j, per * (j + 1)
        for a, b, dst in pieces:
            s, e = max(lo, a), min(hi, b)
            if s < e:
                runs.append((j, s - lo, e - lo, dst + s - a))
    return runs


def _repack_w_in(g, name):
    tr = 256

    def body(g_ref, o_ref):
        for j, a, b, dst in _repack_runs():
            o_ref[:, dst:dst + b - a] = g_ref[j, :, a:b]
        o_ref[:, C_DT + 16:PROJ_W] = jnp.zeros((tr, PROJ_W - C_DT - 16), g.dtype)

    return pl.pallas_call(
        body, name=name, grid=(D_MODEL // tr,),
        out_shape=jax.ShapeDtypeStruct((D_MODEL, PROJ_W), g.dtype),
        in_specs=[pl.BlockSpec((N_DEV, tr, D_IN_PROJ // N_DEV), lambda i: (0, i, 0))],
        out_specs=pl.BlockSpec((tr, PROJ_W), lambda i: (i, 0)),
        compiler_params=pltpu.CompilerParams(dimension_semantics=("arbitrary",)),
    )(g)


def _unpack_gw_in(g, name):
    tr = 256

    def body(g_ref, o_ref):
        for j, a, b, dst in _repack_runs():
            o_ref[j, :, a:b] = g_ref[:, dst:dst + b - a].astype(BF16)

    return pl.pallas_call(
        body, name=name, grid=(D_MODEL // tr,),
        out_shape=jax.ShapeDtypeStruct((N_DEV, D_MODEL, D_IN_PROJ // N_DEV), BF16),
        in_specs=[pl.BlockSpec((tr, PROJ_W), lambda i: (i, 0))],
        out_specs=pl.BlockSpec((N_DEV, tr, D_IN_PROJ // N_DEV), lambda i: (0, i, 0)),
        compiler_params=pltpu.CompilerParams(dimension_semantics=("arbitrary",)),
    )(g)


def _pair_sum(parts, name):
    n_dev, r, cdim = parts.shape
    n_chip = n_dev // 2
    by_chip = parts.reshape(n_chip, 2, r, cdim)

    def swap_body(p_ref, got_ref, send_sem, recv_sem):
        x, y, c = _my_place()
        cp = pltpu.make_async_remote_copy(
            src_ref=p_ref.at[:, pl.ds(1 - c, 1)], dst_ref=got_ref, send_sem=send_sem, recv_sem=recv_sem,
            device_id=(x, y, 1 - c), device_id_type=pl.DeviceIdType.MESH)
        cp.start()
        cp.wait()

    any_spec = pl.BlockSpec(memory_space=pl.ANY)
    got = pl.pallas_call(
        swap_body, name=name + "_swap",
        out_shape=jax.ShapeDtypeStruct((n_chip, 1, r, cdim), parts.dtype),
        in_specs=[any_spec], out_specs=any_spec,
        scratch_shapes=[pltpu.SemaphoreType.DMA, pltpu.SemaphoreType.DMA],
    )(by_chip)

    tr = 256

    def add_body(p_ref, got_ref, o_ref):
        mine = jnp.where(lax.axis_index("c") == 0, p_ref[0, 0], p_ref[0, 1])
        o_ref[0] = (mine.astype(F32) + got_ref[0, 0].astype(F32)).astype(o_ref.dtype)

    return pl.pallas_call(
        add_body, name=name + "_add", grid=(n_chip, r // tr),
        out_shape=jax.ShapeDtypeStruct((n_chip, r, cdim), parts.dtype),
        in_specs=[pl.BlockSpec((1, 2, tr, cdim), lambda k, i: (k, 0, i, 0)),
                  pl.BlockSpec((1, 1, tr, cdim), lambda k, i: (k, 0, i, 0))],
        out_specs=pl.BlockSpec((1, tr, cdim), lambda k, i: (k, i, 0)),
        compiler_params=pltpu.CompilerParams(dimension_semantics=("arbitrary", "arbitrary")),
    )(by_chip, got)


def _loss_head(x, fnw, target, name):
    t = x.shape[0]
    tm = 512

    def body(x_ref, w_ref, t_ref, dx_ref, loss_ref, gw_ref):
        @pl.when(pl.program_id(0) == 0)
        def _():
            loss_ref[...] = jnp.zeros_like(loss_ref)
            gw_ref[...] = jnp.zeros_like(gw_ref)

        y, vjp = jax.vjp(_rmsnorm, x_ref[...], w_ref[...])
        err = y - t_ref[...]
        loss_ref[...] += 0.5 * jnp.sum(jnp.mean(err * err, axis=-1, keepdims=True), axis=0, keepdims=True)
        dx, dw = vjp(err * (1.0 / D_MODEL))
        dx_ref[...] = dx
        gw_ref[...] += dw

    tok = lambda i: (i, 0)
    return pl.pallas_call(
        body, name=name, grid=(t // tm,),
        out_shape=[jax.ShapeDtypeStruct((t, D_MODEL), F32), jax.ShapeDtypeStruct((1, 1), F32),
                   jax.ShapeDtypeStruct((1, D_MODEL), F32)],
        in_specs=[pl.BlockSpec((tm, D_MODEL), tok), _full((1, D_MODEL)), pl.BlockSpec((tm, D_MODEL), tok)],
        out_specs=[pl.BlockSpec((tm, D_MODEL), tok), _full((1, 1)), _full((1, D_MODEL))],
        compiler_params=pltpu.CompilerParams(dimension_semantics=("arbitrary",)),
    )(x, fnw, target)


def _adamw(w, g, m, v):
    m = ADAM_B1 * m + (1.0 - ADAM_B1) * g
    v = ADAM_B2 * v + (1.0 - ADAM_B2) * jnp.square(g)
    m_hat = m / (1.0 - ADAM_B1 ** ADAM_STEP)
    v_hat = v / (1.0 - ADAM_B2 ** ADAM_STEP)
    delta = -ADAM_LR * (m_hat / (jnp.sqrt(v_hat) + ADAM_EPS) + ADAM_WD * w)
    return delta, m, v


def _reduce_adamw(parts, w, m, v, tr, name):
    depth = len(parts)
    p, r, cdim = parts[0].shape
    n_blk = r // tr

    def body(*refs):
        p_refs = refs[:depth]
        w_ref, m_ref, v_ref, g_ref, d_ref, nm_ref, nv_ref = refs[depth:]
        for layer in range(depth):
            @pl.when(pl.program_id(0) == layer)
            def _(p_ref=p_refs[layer]):
                g = p_ref[0].astype(F32)
                for i in range(1, p):
                    g = g + p_ref[i].astype(F32)
                g_ref[0] = g
                d_ref[0], nm_ref[0], nv_ref[0] = _adamw(w_ref[0], g, m_ref[0], v_ref[0])

    def parts_spec(layer):
        return pl.BlockSpec((p, tr, cdim), lambda d, i: (0, jnp.clip(i + (d - layer) * n_blk, 0, n_blk - 1), 0))

    blk = pl.BlockSpec((1, tr, cdim), lambda d, i: (d, i, 0))
    return pl.pallas_call(
        body, name=name, grid=(depth, n_blk),
        out_shape=[jax.ShapeDtypeStruct(w.shape, F32)] * 4,
        in_specs=[parts_spec(layer) for layer in range(depth)] + [blk, blk, blk],
        out_specs=[blk] * 4,
        compiler_params=pltpu.CompilerParams(dimension_semantics=("arbitrary", "arbitrary"),
                                             vmem_limit_bytes=VMEM_LIMIT),
    )(*parts, w, m, v)


def _adamw_small(ssum, entries, name):
    direct = [e[3] for e in entries if not isinstance(e[3], list)]
    n_direct = len(direct)

    def body(*refs):
        ssum_ref, direct_refs = refs[0], list(refs[1:1 + n_direct])
        ins = refs[1 + n_direct:1 + n_direct + 3 * len(entries)]
        outs = refs[1 + n_direct + 3 * len(entries):]
        for k, (w, _, _, grad) in enumerate(entries):
            w_ref, m_ref, v_ref = ins[3 * k:3 * k + 3]
            g_ref, d_ref, nm_ref, nv_ref = outs[4 * k:4 * k + 4]
            if isinstance(grad, list):
                for row, off in enumerate(grad):
                    rows = slice(row, row + 1)
                    g = ssum_ref[:, off:off + w.shape[1]]
                    g_ref[rows, :] = g
                    d_ref[rows, :], nm_ref[rows, :], nv_ref[rows, :] = _adamw(w_ref[rows, :], g, m_ref[rows, :],
                                                                              v_ref[rows, :])
            else:
                g = direct_refs.pop(0)[...]
                g_ref[...] = g
                d_ref[...], nm_ref[...], nv_ref[...] = _adamw(w_ref[...], g, m_ref[...], v_ref[...])

    vmem = pl.BlockSpec(memory_space=pltpu.VMEM)
    args = [ssum] + direct + [a for e in entries for a in e[:3]]
    res = pl.pallas_call(
        body, name=name,
        out_shape=[jax.ShapeDtypeStruct(e[0].shape, F32) for e in entries for _ in range(4)],
        in_specs=[vmem] * len(args), out_specs=[vmem] * (4 * len(entries)),
    )(*args)
    return [res[4 * k:4 * k + 4] for k in range(len(entries))]


def _reduce_adamw_cols(parts, w, m, v, name):
    depth = len(parts)
    _, r, cdim = parts[0].shape
    tc = 256

    def body(*refs):
        p_refs = refs[:depth]
        w_ref, m_ref, v_ref, g_ref, d_ref, nm_ref, nv_ref = refs[depth:]
        for layer in range(depth):
            g = p_refs[layer][0].astype(F32)
            for i in range(1, parts[layer].shape[0]):
                g = g + p_refs[layer][i].astype(F32)
            g = g.T
            g_ref[:, layer, :] = g
            d_ref[:, layer, :], nm_ref[:, layer, :], nv_ref[:, layer, :] = _adamw(
                w_ref[:, layer, :], g, m_ref[:, layer, :], v_ref[:, layer, :])

    view = lambda a: jnp.transpose(a, (2, 0, 1))
    blk = pl.BlockSpec((cdim, depth, tc), lambda i: (0, 0, i))
    outs = pl.pallas_call(
        body, name=name, grid=(r // tc,),
        out_shape=[jax.ShapeDtypeStruct((cdim, depth, r), F32)] * 4,
        in_specs=[pl.BlockSpec((a.shape[0], tc, cdim), lambda i: (0, i, 0)) for a in parts] + [blk, blk, blk],
        out_specs=[blk] * 4,
        compiler_params=pltpu.CompilerParams(dimension_semantics=("arbitrary",), vmem_limit_bytes=VMEM_LIMIT),
    )(*parts, view(w), view(m), view(v))
    return [jnp.transpose(o, (1, 2, 0)) for o in outs]


def _pad_lanes(v, width=128):
    return jnp.pad(v.reshape(1, -1), ((0, 0), (0, width - v.shape[-1])))


SMALL_FIELDS = (("norm_w", 1024), ("conv_b", 1536), ("dt_bias", 128), ("a_log", 128), ("d_skip", 128),
                ("ssd_norm_w", 1024), ("sinks", 128), ("dw_b", 512), ("ln_w", 512), ("ln_b", 512))


def kernel(x, norm_w, w_in, ssd_conv_w, ssd_conv_b, ssd_dt_bias, ssd_a_log, ssd_d, ssd_norm_w, attn_sinks, conf_dw_w, conf_dw_b, conf_ln_w, conf_ln_b, w_out, final_norm_w, loss_target, m_norm_w, m_w_in, m_ssd_conv_w, m_ssd_conv_b, m_ssd_dt_bias, m_ssd_a_log, m_ssd_d, m_ssd_norm_w, m_attn_sinks, m_conf_dw_w, m_conf_dw_b, m_conf_ln_w, m_conf_ln_b, m_w_out, m_final_norm_w, v_norm_w, v_w_in, v_ssd_conv_w, v_ssd_conv_b, v_ssd_dt_bias, v_ssd_a_log, v_ssd_d, v_ssd_norm_w, v_attn_sinks, v_conf_dw_w, v_conf_dw_b, v_conf_ln_w, v_conf_ln_b, v_w_out, v_final_norm_w):
    nb, seq, _ = x.shape
    depth = norm_w.shape[0]
    t = nb * seq
    me_idx = 4 * lax.axis_index("x") + 2 * lax.axis_index("y") + lax.axis_index("c")

    w_in_bf, w_out_bf = w_in.astype(BF16), w_out.astype(BF16)
    g_win0, g_cw, g_dw = _all_gather([w_in_bf[0], ssd_conv_w, conf_dw_w], "gather_weights")
    w_in_full = [_repack_w_in(g_win0, "repack_w_in_0")]
    w_out_full = []
    conv_w_full = [jnp.transpose(g_cw[:, l], (1, 0, 2)).reshape(4, 1536) for l in range(depth)]
    dw_w_full = [jnp.transpose(g_dw[:, l], (1, 0, 2)).reshape(CONF_KERNEL, 512) for l in range(depth)]

    def layer_params(l):
        return [conv_w_full[l], ssd_conv_b[l].reshape(1, -1), _pad_lanes(ssd_dt_bias[l]), _pad_lanes(ssd_a_log[l]),
                _pad_lanes(ssd_d[l]), ssd_norm_w[l].reshape(1, -1), _pad_lanes(attn_sinks[l]), dw_w_full[l],
                conf_dw_b[l].reshape(1, -1), conf_ln_w[l].reshape(1, -1), conf_ln_b[l].reshape(1, -1)]

    xs = [x.reshape(t, D_MODEL)]
    saved = []
    for l in range(depth):
        (proj, h), gathered = _inproj_fwd(xs[l], norm_w[l].reshape(1, -1), w_in_full[l], f"inproj_fwd_{l}",
                                          (GATHER, [w_out_bf[l]]))
        w_out_full.append(gathered[0].reshape(MIX_WIDTH, D_MODEL))
        side = (GATHER, [w_in_bf[l + 1]]) if l + 1 < depth else None
        (x_next, *kept), gathered = _mixer_fwd(xs[l], proj, w_out_full[l], layer_params(l), nb, f"mixer_fwd_{l}",
                                               side)
        if side:
            w_in_full.append(_repack_w_in(gathered[0], f"repack_w_in_{l + 1}"))
        saved.append((proj, h, kept))
        xs.append(x_next)
    dx, loss_part, g_fnw = _loss_head(xs[depth], final_norm_w.reshape(1, -1), loss_target.reshape(t, D_MODEL),
                                      "loss_head")

    cols_in = D_IN_PROJ // N_DEV
    rows_out = MIX_WIDTH // N_DEV
    small_rows = [None] * depth
    received = [None] * depth
    outgoing = None
    for l in reversed(range(depth)):
        proj, h, kept = saved[l]
        res, arrived = _mixer_bwd(dx, proj, *kept, w_out_full[l], layer_params(l), nb, f"mixer_bwd_{l}",
                                  (SCATTER, outgoing) if outgoing else None)
        if outgoing:
            received[l + 1] = arrived
        dproj, y_cat = res[0], res[1]
        g_cw_l, g_cb, g_dtb, g_alog, g_dsk, g_nw, g_snk, g_dww, g_dwb, g_lnw, g_lnb = res[2:]
        gw_out_parts = _gw_out(y_cat, dx, f"gw_out_{l}")
        nw_l = norm_w[l].reshape(1, -1)
        if l > 0:
            (gw_in,), _ = _inproj_bwd_w(h, dproj, f"inproj_bwd_w_{l}")
            (dx, g_norm), _ = _inproj_bwd_x(dproj, w_in_full[l], xs[l], nw_l, dx, f"inproj_bwd_x_{l}")
            outgoing = [_unpack_gw_in(gw_in, f"unpack_gw_in_{l}"), gw_out_parts]
        else:
            (gw_in,), got_out = _inproj_bwd_w(h, dproj, f"inproj_bwd_w_{l}", (SCATTER, [gw_out_parts]))
            chip_parts = _pair_sum(_unpack_gw_in(gw_in, f"unpack_gw_in_{l}"), f"pair_sum_{l}")
            (dx, g_norm), got_in = _inproj_bwd_x(dproj, w_in_full[l], xs[l], nw_l, dx, f"inproj_bwd_x_{l}",
                                                 (CHIP_SCATTER, [chip_parts]))
            received[l] = [got_in[0], got_out[0]]
        small_rows[l] = [g_norm, g_cb, g_dtb, g_alog, g_dsk, g_nw, g_snk, g_dwb, g_lnw, g_lnb,
                         g_cw_l.reshape(1, -1), g_dww.reshape(1, -1)]
    grad_x = dx.reshape(nb, seq, D_MODEL)

    small = jnp.concatenate([piece for l in range(depth) for piece in small_rows[l]] + [g_fnw], axis=1)
    ssum = _exchange_small(small, "exchange_small")

    g_w_in, d_w_in, nm_w_in, nv_w_in = _reduce_adamw_cols([received[l][0] for l in range(depth)], w_in, m_w_in,
                                                          v_w_in, "adamw_w_in")
    g_w_out, d_w_out, nm_w_out, nv_w_out = _reduce_adamw([received[l][1] for l in range(depth)], w_out, m_w_out,
                                                         v_w_out, 256, "adamw_w_out")

    per_layer = sum(n for _, n in SMALL_FIELDS) + 4 * 1536 + CONF_KERNEL * 512
    given = {"norm_w": (norm_w, m_norm_w, v_norm_w), "conv_b": (ssd_conv_b, m_ssd_conv_b, v_ssd_conv_b),
             "dt_bias": (ssd_dt_bias, m_ssd_dt_bias, v_ssd_dt_bias), "a_log": (ssd_a_log, m_ssd_a_log, v_ssd_a_log),
             "d_skip": (ssd_d, m_ssd_d, v_ssd_d), "ssd_norm_w": (ssd_norm_w, m_ssd_norm_w, v_ssd_norm_w),
             "sinks": (attn_sinks, m_attn_sinks, v_attn_sinks), "dw_b": (conf_dw_b, m_conf_dw_b, v_conf_dw_b),
             "ln_w": (conf_ln_w, m_conf_ln_w, v_conf_ln_w), "ln_b": (conf_ln_b, m_conf_ln_b, v_conf_ln_b)}
    entries = []
    off = 0
    for fname, n in SMALL_FIELDS:
        entries.append((*given[fname], [l * per_layer + off for l in range(depth)]))
        off += n
    shard_grads = []
    for width, shard, kk in ((1536, 192, 4), (512, 64, CONF_KERNEL)):
        full = [ssum[:, l * per_layer + off:l * per_layer + off + kk * width].reshape(kk, width) for l in range(depth)]
        shard_grads.append(jnp.stack([lax.dynamic_slice(f, (0, me_idx * shard), (kk, shard)) for f in full], axis=0))
        off += kk * width
    entries.append((ssd_conv_w, m_ssd_conv_w, v_ssd_conv_w, shard_grads[0]))
    entries.append((conf_dw_w, m_conf_dw_w, v_conf_dw_w, shard_grads[1]))
    entries.append((final_norm_w.reshape(1, -1), m_final_norm_w.reshape(1, -1), v_final_norm_w.reshape(1, -1),
                    [depth * per_layer]))
    sm = _adamw_small(ssum, entries, "adamw_small")
    sm = {k: quad for k, quad in zip([f for f, _ in SMALL_FIELDS] + ["conv_w", "dw_w", "final"], sm)}

    def outputs(i, big_in_i, big_out_i):
        return [sm["norm_w"][i], big_in_i, sm["conv_w"][i], sm["conv_b"][i], sm["dt_bias"][i], sm["a_log"][i],
                sm["d_skip"][i], sm["ssd_norm_w"][i], sm["sinks"][i], sm["dw_w"][i], sm["dw_b"][i], sm["ln_w"][i],
                sm["ln_b"][i], big_out_i, sm["final"][i].reshape(-1)]

    loss = lax.psum(loss_part[0, 0], MESH_AXES)
    return (loss, grad_x, *outputs(0, g_w_in, g_w_out), *outputs(1, d_w_in, d_w_out),
            *outputs(2, nm_w_in, nm_w_out), *outputs(3, nv_w_in, nv_w_out))
```

```python
import functools

import jax
import jax.numpy as jnp
from jax import lax
from jax.experimental import pallas as pl
from jax.experimental.pallas import tpu as pltpu

F32 = jnp.float32
BF16 = jnp.bfloat16
MESH_AXES = ("x", "y", "c")
N_DEV = 8
EPS = 1e-5

D_MODEL = 1024
CHUNK = 128
SSD_HEADS = 16
SSD_HEAD_DIM = 64
SSD_STATE = 128
ATTN_HEADS = 8
ATTN_HEAD_DIM = 64
CONF_KERNEL = 31
MIX_WIDTH = 2048
D_IN_PROJ = 5392
C_Z = 0
C_CA = 2048
C_XBC = 3072
C_Q = 4608
C_K = 5120
C_V = 5248
C_DT = 5376
PROJ_W = 5632
N_COL_TILES = 4
COL_TILE = PROJ_W // N_COL_TILES
XBC_HALO = 8
CONF_HALO = 32
ATTN_STACK_FWD = 2
ATTN_STACK_BWD = 4
VMEM_LIMIT = 56 * 1024 * 1024

ADAM_LR = 0.001
ADAM_B1 = 0.9
ADAM_B2 = 0.999
ADAM_EPS = 1e-08
ADAM_WD = 0.01
ADAM_STEP = 10


def _silu(v):
    return v * jax.nn.sigmoid(v)


def _softplus(v):
    return jnp.maximum(v, 0.0) + jnp.log1p(jnp.exp(-jnp.abs(v)))


def _rmsnorm(v, w):
    return v * lax.rsqrt(jnp.mean(v * v, axis=-1, keepdims=True) + EPS) * w


def _dot(a, b):
    return jnp.dot(a.astype(BF16), b.astype(BF16), preferred_element_type=F32)


def _dot_nt(a, b):
    return lax.dot_general(a.astype(BF16), b.astype(BF16), (((1,), (1,)), ((), ())), preferred_element_type=F32)


def _dot_tn(a, b):
    return lax.dot_general(a.astype(BF16), b.astype(BF16), (((0,), (0,)), ((), ())), preferred_element_type=F32)


def _taps(ext, offs, out_len, w=None, g=None):
    n_rows, n_cols = ext.shape
    by_shift = {}
    for t, off in enumerate(offs):
        by_shift.setdefault(off % 8, []).append((t, off))
    for r, taps in by_shift.items():
        assert max(off for _, off in taps) - r + out_len <= n_rows - r
    accs = []
    sums = [[None] * (n_cols // 128) for _ in offs]
    for blk in range(n_cols // 128):
        cs = slice(blk * 128, (blk + 1) * 128)
        e = ext[:, cs]
        acc = None
        for r, taps in by_shift.items():
            shifted = e if r == 0 else pltpu.roll(e, n_rows - r, axis=0)
            for t, off in taps:
                window = shifted[off - r:off - r + out_len, :]
                if w is not None:
                    term = w[t:t + 1, cs] * window
                    acc = term if acc is None else acc + term
                if g is not None:
                    sums[t][blk] = jnp.sum(g[:, cs] * window, axis=0, keepdims=True)
        accs.append(acc)
    if w is not None:
        return jnp.concatenate(accs, axis=1)
    return jnp.concatenate([jnp.concatenate(row, axis=1) for row in sums], axis=0)


@functools.partial(jax.custom_vjp, nondiff_argnums=(3,))
def _dwconv(ext, w, b, halo):
    kk = w.shape[0]
    return b + _taps(ext, [halo - (kk - 1) + t for t in range(kk)], ext.shape[0] - halo, w=w)


def _dwconv_fwd(ext, w, b, halo):
    return _dwconv(ext, w, b, halo), (ext, w)


def _dwconv_bwd(halo, res, g):
    ext, w = res
    kk = w.shape[0]
    offs = [halo - (kk - 1) + t for t in range(kk)]
    dw = _taps(ext, offs, ext.shape[0] - halo, g=g)
    zeros = jnp.zeros((halo, g.shape[1]), g.dtype)
    gp = jnp.concatenate([zeros, g, zeros], axis=0)
    dext = _taps(gp, [halo - off for off in offs], ext.shape[0], w=w)
    return dext, dw, jnp.sum(g, axis=0, keepdims=True)


_dwconv.defvjp(_dwconv_fwd, _dwconv_bwd)


def _ssd_part(z_ssd, conv_out, dtr, s_in, dtb, alog, dsk, nw):
    qn = conv_out.shape[0]
    nh = SSD_HEADS
    per_group = nh // 2
    n_pair = nh // 2
    xa = _silu(conv_out)
    xs = xa[:, 0:1024]
    dt = _softplus(dtr + dtb)
    a = dt * (-jnp.exp(alog))
    rows = lax.broadcasted_iota(jnp.int32, (qn, qn), 0)
    cols = lax.broadcasted_iota(jnp.int32, (qn, qn), 1)
    causal = rows >= cols
    low = cols < SSD_HEAD_DIM
    a_cs = jnp.dot(causal.astype(F32), a, precision=lax.Precision.HIGHEST, preferred_element_type=F32)
    a_cs_t = a_cs.T
    bgs = [xa[:, 1024 + g * 128:1024 + (g + 1) * 128] for g in range(2)]
    cgs = [xa[:, 1280 + g * 128:1280 + (g + 1) * 128] for g in range(2)]
    cbms = [_dot_nt(cgs[g], bgs[g]) for g in range(2)]
    colb = [jnp.broadcast_to(a_cs[:, h:h + 1], (qn, qn)) for h in range(nh)]
    lastb = [jnp.broadcast_to(colb[h][qn - 1:qn, :], (qn, qn)) for h in range(nh)]
    dtb_wide = [jnp.broadcast_to(dt[:, h:h + 1], (qn, qn)) for h in range(nh)]
    lmats = [jnp.exp(jnp.where(causal, colb[h] - a_cs_t[h:h + 1, :], -jnp.inf)) for h in range(nh)]
    ms = [cbms[h // per_group] * lmats[h] for h in range(nh)]
    x_pair = [xs[:, p * 128:(p + 1) * 128] for p in range(n_pair)]
    x_lo = [x_pair[p] * jnp.where(low, dtb_wide[2 * p], 0.0) for p in range(n_pair)]
    x_hi = [x_pair[p] * jnp.where(low, 0.0, dtb_wide[2 * p + 1]) for p in range(n_pair)]
    y_diag = [_dot(ms[2 * p], x_lo[p]) + _dot(ms[2 * p + 1], x_hi[p]) for p in range(n_pair)]
    col_pair = [jnp.where(low, colb[2 * p], colb[2 * p + 1]) for p in range(n_pair)]
    last_pair = [jnp.where(low, lastb[2 * p], lastb[2 * p + 1]) for p in range(n_pair)]
    ecol = [jnp.exp(col_pair[p]) for p in range(n_pair)]
    xw = [(x_lo[p] + x_hi[p]) * jnp.exp(last_pair[p] - col_pair[p]) for p in range(n_pair)]
    y_off, st = [], []
    for g in range(2):
        ps = range(g * n_pair // 2, (g + 1) * n_pair // 2)
        y_off.append(_dot_nt(cgs[g], s_in[g * 512:(g + 1) * 512, :]) * jnp.concatenate([ecol[p] for p in ps], axis=1))
        st.append(_dot_tn(jnp.concatenate([xw[p] for p in ps], axis=1), bgs[g]))
    e_last = jnp.exp(jnp.broadcast_to(a_cs_t[:, qn - 1:qn], (qn, SSD_STATE)))
    scale = jnp.concatenate([jnp.broadcast_to(e_last[h:h + 1, :], (64, SSD_STATE)) for h in range(nh)], axis=0)
    s_out = scale * s_in + jnp.concatenate(st, axis=0)
    d_wide = jnp.concatenate([jnp.broadcast_to(dsk[:, h:h + 1], (1, 64)) for h in range(nh)], axis=1)
    y = jnp.concatenate(y_diag, axis=1) + jnp.concatenate(y_off, axis=1) + d_wide * xs
    gated = y * _silu(z_ssd)
    halves = []
    for g in range(2):
        gg = gated[:, g * 512:(g + 1) * 512]
        halves.append(gg * lax.rsqrt(jnp.mean(gg * gg, axis=-1, keepdims=True) + EPS))
    return jnp.concatenate(halves, axis=1) * nw, s_out


def _attn_part(z_attn, q, kv, p_kv, snk, kvmask, stack):
    qn = q.shape[0]
    kk = jnp.concatenate([p_kv[:, 0:128], kv[:, 0:128]], axis=0)
    vv = jnp.concatenate([p_kv[:, 128:256], kv[:, 128:256]], axis=0)
    units = range(ATTN_HEADS // stack)
    heads = [range(u * stack, (u + 1) * stack) for u in units]
    kv_of = [u * stack // (ATTN_HEADS // 2) for u in units]
    k_of = [kk[:, g * 64:(g + 1) * 64] for g in kv_of]
    v_of = [vv[:, g * 64:(g + 1) * 64] for g in kv_of]
    qs = [jnp.concatenate([q[:, h * 64:(h + 1) * 64] for h in heads[u]], axis=0) for u in units]
    sk = [jnp.concatenate([jnp.broadcast_to(snk[:, h:h + 1], (qn, 1)) for h in heads[u]], axis=0) for u in units]
    s = [jnp.where(kvmask, _dot_nt(qs[u], k_of[u]) * (ATTN_HEAD_DIM ** -0.5), -jnp.inf) for u in units]
    m = [lax.stop_gradient(jnp.maximum(jnp.max(s[u], axis=-1, keepdims=True), sk[u])) for u in units]
    e = [jnp.exp(s[u] - m[u]) for u in units]
    r_den = [1.0 / (jnp.sum(e[u], axis=-1, keepdims=True) + jnp.exp(sk[u] - m[u])) for u in units]
    o = [_dot(e[u], v_of[u]) * r_den[u] for u in units]
    outs = [o[u][i * qn:(i + 1) * qn, :] for u in units for i in range(stack)]
    return jnp.concatenate(outs, axis=1) * _silu(z_attn)


def _conf_glu(cacg, p_cc):
    c0 = cacg[:, 0:512] * jax.nn.sigmoid(cacg[:, 512:1024])
    pc0 = p_cc[:, 0:512] * jax.nn.sigmoid(p_cc[:, 512:1024])
    return jnp.concatenate([pc0, c0], axis=0)


def _conf_tail(conv_out, z_conf, lnw, lnb):
    xc = conv_out - jnp.mean(conv_out, axis=-1, keepdims=True)
    yln = xc * lax.rsqrt(jnp.mean(xc * xc, axis=-1, keepdims=True) + EPS) * lnw + lnb
    return _silu(yln) * _silu(z_conf)


def _kv_mask(qn, not_first, reps):
    ii = lax.broadcasted_iota(jnp.int32, (reps * qn, 2 * qn), 0) & (qn - 1)
    jj = lax.broadcasted_iota(jnp.int32, (reps * qn, 2 * qn), 1)
    d = jj - ii
    return (d >= 1) & (d <= qn) & (not_first | (jj >= qn))


def _my_place():
    return lax.axis_index("x"), lax.axis_index("y"), lax.axis_index("c")


def _all_gather(arrs, name):
    n = len(arrs)

    def body(*refs):
        ins, outs = refs[:n], refs[n:2 * n]
        send_sems, recv_sems, local_sems = refs[2 * n:]
        x, y, c = _my_place()
        me, sibling = (x, y, c), (x, y, 1 - c)
        chips = [(1 - x, y), (x, 1 - y), (1 - x, 1 - y)]

        def slot(a, p):
            return outs[a].at[4 * p[0] + 2 * p[1] + p[2]]

        def copy(a, kk, block, to, src=None):
            return pltpu.make_async_remote_copy(
                src_ref=slot(a, block) if src is None else src, dst_ref=slot(a, block),
                send_sem=send_sems.at[a, kk], recv_sem=recv_sems.at[a, kk],
                device_id=to, device_id_type=pl.DeviceIdType.MESH)

        mine = [pltpu.make_async_copy(ins[a], slot(a, me), local_sems.at[a]) for a in range(n)]
        for cp in mine:
            cp.start()
        first = []
        for a in range(n):
            first.append(copy(a, 0, me, sibling, src=ins[a]))
            first += [copy(a, 1 + j, me, (*chip, c), src=ins[a]) for j, chip in enumerate(chips)]
        for cp in first:
            cp.start()
        passed = []
        for j, chip in enumerate(chips):
            for a in range(n):
                copy(a, 1 + j, (*chip, c), me).wait_recv()
                fwd = copy(a, 4 + j, (*chip, c), sibling)
                fwd.start()
                passed.append(fwd)
        for a in range(n):
            copy(a, 0, sibling, me).wait_recv()
            for j, chip in enumerate(chips):
                copy(a, 4 + j, (*chip, 1 - c), me).wait_recv()
        for cp in first + passed:
            cp.wait_send()
        for cp in mine:
            cp.wait()

    any_spec = pl.BlockSpec(memory_space=pl.ANY)
    return pl.pallas_call(
        body, name=name,
        out_shape=[jax.ShapeDtypeStruct((N_DEV,) + a.shape, a.dtype) for a in arrs],
        in_specs=[any_spec] * n, out_specs=[any_spec] * n,
        scratch_shapes=[pltpu.SemaphoreType.DMA((n, 7)), pltpu.SemaphoreType.DMA((n, 7)),
                        pltpu.SemaphoreType.DMA((n,))],
    )(*arrs)


GATHER, SCATTER, CHIP_SCATTER = "gather", "scatter", "chip_scatter"


def _direct_copies(mode, ins, outs, send_sems, recv_sems, local_sems):
    x, y, c = _my_place()
    by_chip = mode == CHIP_SCATTER
    place = (lambda px, py, pc: 2 * px + py) if by_chip else (lambda px, py, pc: 4 * px + 2 * py + pc)
    me_idx = place(x, y, c)
    n = len(ins)
    local = [pltpu.make_async_copy(ins[a] if mode == GATHER else ins[a].at[me_idx], outs[a].at[me_idx],
                                   local_sems.at[a]) for a in range(n)]
    remote = []
    for rel in range(1, N_DEV):
        if by_chip and rel & 1:
            continue
        px = 1 - x if rel & 4 else x
        py = 1 - y if rel & 2 else y
        pc = 1 - c if rel & 1 else c
        for a in range(n):
            remote.append(pltpu.make_async_remote_copy(
                src_ref=ins[a] if mode == GATHER else ins[a].at[place(px, py, pc)], dst_ref=outs[a].at[me_idx],
                send_sem=send_sems.at[a, rel - 1], recv_sem=recv_sems.at[a, rel - 1],
                device_id=(px, py, pc), device_id_type=pl.DeviceIdType.MESH))
    return local + remote


def _side_scratch(n):
    return [pltpu.SemaphoreType.DMA((n, 7)), pltpu.SemaphoreType.DMA((n, 7)), pltpu.SemaphoreType.DMA((n,))]


def _side_out_shapes(mode, arrs):
    return [jax.ShapeDtypeStruct((N_DEV,) + a.shape if mode == GATHER else a.shape, a.dtype) for a in arrs]


def _pallas_with_side(body, side, first, last, n_in, n_out, *, in_specs, out_specs, out_shape, scratch_shapes=(),
                      args, **kwargs):
    side_arrs = [] if side is None else list(side[1])
    ns = len(side_arrs)

    def wrapped(*refs):
        own_in, side_in = refs[:n_in], refs[n_in:n_in + ns]
        o = n_in + ns
        own_out, side_out = refs[o:o + n_out], refs[o + n_out:o + n_out + ns]
        scratch = refs[o + n_out + ns:]
        own_scratch, sems = (scratch[:-3], scratch[-3:]) if ns else (scratch, ())
        if ns:
            @pl.when(first())
            def _():
                for cp in _direct_copies(side[0], side_in, side_out, *sems):
                    cp.start()

        body(*own_in, *own_out, *own_scratch)
        if ns:
            @pl.when(last())
            def _():
                for cp in _direct_copies(side[0], side_in, side_out, *sems):
                    cp.wait()

    any_spec = pl.BlockSpec(memory_space=pl.ANY)
    res = pl.pallas_call(
        wrapped,
        in_specs=list(in_specs) + [any_spec] * ns,
        out_specs=list(out_specs) + [any_spec] * ns,
        out_shape=list(out_shape) + (_side_out_shapes(side[0], side_arrs) if ns else []),
        scratch_shapes=list(scratch_shapes) + (_side_scratch(ns) if ns else []),
        **kwargs,
    )(*args, *side_arrs)
    return res[:n_out], res[n_out:]


def _exchange_small(small, name):
    ns = small.shape[1]

    def body(small_ref, sum_ref, small_all, send_sems, recv_sems, local_sems):
        copies = _direct_copies(GATHER, [small_ref], [small_all], send_sems, recv_sems, local_sems)
        for cp in copies:
            cp.start()
        for cp in copies:
            cp.wait()
        total = small_all[0]
        for i in range(1, N_DEV):
            total = total + small_all[i]
        sum_ref[...] = total

    vmem_spec = pl.BlockSpec(memory_space=pltpu.VMEM)
    return pl.pallas_call(
        body, name=name,
        out_shape=jax.ShapeDtypeStruct((1, ns), F32),
        in_specs=[vmem_spec], out_specs=vmem_spec,
        scratch_shapes=[pltpu.VMEM((N_DEV, 1, ns), F32)] + _side_scratch(1),
    )(small)


def _full(shape):
    return pl.BlockSpec(shape, lambda *_: (0,) * len(shape))


def _inproj_fwd(x, nw, w, name, side=None):
    t = x.shape[0]
    tm = 256

    def body(x_ref, nw_ref, w_ref, proj_ref, h_ref):
        h = _rmsnorm(x_ref[...], nw_ref[...]).astype(BF16)
        h_ref[...] = h
        for j in range(N_COL_TILES):
            sl = slice(j * COL_TILE, (j + 1) * COL_TILE)
            proj_ref[:, sl] = jnp.dot(h, w_ref[:, sl], preferred_element_type=F32)

    grid = (t // tm,)
    return _pallas_with_side(
        body, side, *_grid_ends(grid), 3, 2, name=name, grid=grid,
        out_shape=[jax.ShapeDtypeStruct((t, PROJ_W), F32), jax.ShapeDtypeStruct((t, D_MODEL), BF16)],
        in_specs=[pl.BlockSpec((tm, D_MODEL), lambda i: (i, 0)), _full((1, D_MODEL)), _full((D_MODEL, PROJ_W))],
        out_specs=[pl.BlockSpec((tm, PROJ_W), lambda i: (i, 0)), pl.BlockSpec((tm, D_MODEL), lambda i: (i, 0))],
        compiler_params=pltpu.CompilerParams(dimension_semantics=("arbitrary",), vmem_limit_bytes=VMEM_LIMIT),
        args=(x, nw, w))


def _param_specs():
    return [_full((4, 1536)), _full((1, 1536)), _full((1, 128)), _full((1, 128)), _full((1, 128)),
            _full((1, 1024)), _full((1, 128)), _full((CONF_KERNEL, 512)), _full((1, 512)), _full((1, 512)),
            _full((1, 512))]


def _halo_specs(nc, chunk_of):
    def prev_chunk(b, j):
        return jnp.maximum(b * nc + chunk_of(j) - 1, 0)

    per_xbc = CHUNK // XBC_HALO
    per_cc = CHUNK // CONF_HALO
    return [
        pl.BlockSpec((XBC_HALO, 1536), lambda b, j: (prev_chunk(b, j) * per_xbc + per_xbc - 1, C_XBC // 1536)),
        pl.BlockSpec((CHUNK, 256), lambda b, j: (prev_chunk(b, j), C_K // 256)),
        pl.BlockSpec((CONF_HALO, 1024), lambda b, j: (prev_chunk(b, j) * per_cc + per_cc - 1, C_CA // 1024)),
    ]


def _grid_ends(grid):
    first = lambda: functools.reduce(lambda p, q: p & q, [pl.program_id(i) == 0 for i in range(len(grid))])
    last = lambda: functools.reduce(lambda p, q: p & q, [pl.program_id(i) == n - 1 for i, n in enumerate(grid)])
    return first, last


def _mixer_fwd(x, proj, w_out, params, nb, name, side=None):
    t = x.shape[0]
    nc = t // nb // CHUNK

    def body(x_ref, cur_ref, pxbc_ref, pkv_ref, pcc_ref, wo_ref, *rest):
        prm = [r[...] for r in rest[:11]]
        xn_ref, sall_ref, conv_ref, s_scr = rest[11:]
        c = pl.program_id(1)
        not_first = c > 0
        nf = not_first.astype(F32)

        @pl.when(c == 0)
        def _():
            s_scr[...] = jnp.zeros_like(s_scr)

        cw, cb, dtb, alog, dsk, nw, snk, dww, dwb, lnw, lnb = prm
        s_in = s_scr[...]
        sall_ref[0] = s_in
        ssd_conv = _dwconv(jnp.concatenate([pxbc_ref[...] * nf, cur_ref[:, C_XBC:C_XBC + 1536]], axis=0), cw, cb,
                           XBC_HALO)
        y_ssd, s_out = _ssd_part(cur_ref[:, 0:1024], ssd_conv, cur_ref[:, C_DT:C_DT + 128], s_in, dtb, alog, dsk, nw)
        s_scr[...] = s_out
        y_attn = _attn_part(cur_ref[:, 1024:1536], cur_ref[:, C_Q:C_Q + 512], cur_ref[:, C_K:C_K + 256],
                            pkv_ref[...] * nf, snk, _kv_mask(CHUNK, not_first, ATTN_STACK_FWD), ATTN_STACK_FWD)
        conv_out = _dwconv(_conf_glu(cur_ref[:, C_CA:C_CA + 1024], pcc_ref[...] * nf), dww, dwb, CONF_HALO)
        conv_ref[...] = conv_out
        y_conf = _conf_tail(conv_out, cur_ref[:, 1536:2048], lnw, lnb)
        xn_ref[...] = (x_ref[...] + _dot(y_ssd, wo_ref[0:1024, :]) + _dot(y_attn, wo_ref[1024:1536, :])
                       + _dot(y_conf, wo_ref[1536:2048, :]))

    row = lambda b, j: (b * nc + j, 0)
    grid = (nb, nc)
    return _pallas_with_side(
        body, side, *_grid_ends(grid), 17, 3, name=name, grid=grid,
        out_shape=[jax.ShapeDtypeStruct((t, D_MODEL), F32),
                   jax.ShapeDtypeStruct((nb * nc, SSD_HEADS * SSD_HEAD_DIM, SSD_STATE), F32),
                   jax.ShapeDtypeStruct((t, 512), F32)],
        in_specs=[pl.BlockSpec((CHUNK, D_MODEL), row), pl.BlockSpec((CHUNK, PROJ_W), row)]
                 + _halo_specs(nc, lambda j: j) + [_full((MIX_WIDTH, D_MODEL))] + _param_specs(),
        out_specs=[pl.BlockSpec((CHUNK, D_MODEL), row),
                   pl.BlockSpec((1, SSD_HEADS * SSD_HEAD_DIM, SSD_STATE), lambda b, j: (b * nc + j, 0, 0)),
                   pl.BlockSpec((CHUNK, 512), row)],
        scratch_shapes=[pltpu.VMEM((SSD_HEADS * SSD_HEAD_DIM, SSD_STATE), F32)],
        compiler_params=pltpu.CompilerParams(dimension_semantics=("arbitrary", "arbitrary"),
                                             vmem_limit_bytes=VMEM_LIMIT),
        args=(x, proj, proj, proj, proj, w_out, *params))


def _mixer_bwd(dxn, proj, s_all, conv_all, w_out, params, nb, name, side=None):
    t = dxn.shape[0]
    nc = t // nb // CHUNK
    n_prm = 11

    def body(dxn_ref, cur_ref, pxbc_ref, pkv_ref, pcc_ref, s_ref, conv_ref, wo_ref, *rest):
        prm = [r[...] for r in rest[:n_prm]]
        dproj_ref, ycat_ref = rest[n_prm:n_prm + 2]
        gprm = rest[n_prm + 2:2 * n_prm + 2]
        ds_scr, pend_xbc, pend_kv, pend_cc = rest[2 * n_prm + 2:]
        b, j = pl.program_id(0), pl.program_id(1)
        c = nc - 1 - j
        not_first = c > 0
        nf = not_first.astype(F32)

        @pl.when((b == 0) & (j == 0))
        def _():
            for r in gprm:
                r[...] = jnp.zeros_like(r)

        @pl.when(j == 0)
        def _():
            ds_scr[...] = jnp.zeros_like(ds_scr)
            pend_xbc[...] = jnp.zeros_like(pend_xbc)
            pend_kv[...] = jnp.zeros_like(pend_kv)
            pend_cc[...] = jnp.zeros_like(pend_cc)

        cw, cb, dtb, alog, dsk, nw, snk, dww, dwb, lnw, lnb = prm
        g_cw, g_cb, g_dtb, g_alog, g_dsk, g_nw, g_snk, g_dww, g_dwb, g_lnw, g_lnb = gprm
        dxn_v = dxn_ref[...]

        def add_tail(d_cur, pending):
            lead = jnp.zeros((CHUNK - pending.shape[0], pending.shape[1]), F32)
            return d_cur + jnp.concatenate([lead, pending], axis=0)

        y, vjp = jax.vjp(_conf_tail, conv_ref[...], cur_ref[:, 1536:2048], lnw, lnb)
        ycat_ref[:, 1536:2048] = y.astype(BF16)
        d_conv, dz, d_lnw, d_lnb = vjp(_dot_nt(dxn_v, wo_ref[1536:2048, :]))
        ext, vjp = jax.vjp(_conf_glu, cur_ref[:, C_CA:C_CA + 1024], pcc_ref[...] * nf)
        d_ext, d_dww, d_dwb = _dwconv_bwd(CONF_HALO, (ext, dww), d_conv)
        dcacg, dpcc = vjp(d_ext)
        dproj_ref[:, 1536:2048] = dz.astype(BF16)
        dproj_ref[:, C_CA:C_CA + 1024] = add_tail(dcacg, pend_cc[...]).astype(BF16)
        pend_cc[...] = dpcc
        for r, g in ((g_dww, d_dww), (g_dwb, d_dwb), (g_lnw, d_lnw), (g_lnb, d_lnb)):
            r[...] += g

        attn = functools.partial(_attn_part, kvmask=_kv_mask(CHUNK, not_first, ATTN_STACK_BWD),
                                 stack=ATTN_STACK_BWD)
        y, vjp = jax.vjp(attn, cur_ref[:, 1024:1536], cur_ref[:, C_Q:C_Q + 512], cur_ref[:, C_K:C_K + 256],
                         pkv_ref[...] * nf, snk)
        ycat_ref[:, 1024:1536] = y.astype(BF16)
        dz, dq, dkv, dpkv, d_snk = vjp(_dot_nt(dxn_v, wo_ref[1024:1536, :]))
        dproj_ref[:, 1024:1536] = dz.astype(BF16)
        dproj_ref[:, C_Q:C_Q + 512] = dq.astype(BF16)
        dproj_ref[:, C_K:C_K + 256] = (dkv + pend_kv[...]).astype(BF16)
        pend_kv[...] = dpkv
        g_snk[...] += d_snk

        ext = jnp.concatenate([pxbc_ref[...] * nf, cur_ref[:, C_XBC:C_XBC + 1536]], axis=0)
        (y, _), vjp = jax.vjp(_ssd_part, cur_ref[:, 0:1024], _dwconv(ext, cw, cb, XBC_HALO),
                              cur_ref[:, C_DT:C_DT + 128], s_ref[0], dtb, alog, dsk, nw)
        ycat_ref[:, 0:1024] = y.astype(BF16)
        dz, d_conv, ddtr, ds_in, d_dtb, d_alog, d_dsk, d_nw = vjp((_dot_nt(dxn_v, wo_ref[0:1024, :]), ds_scr[...]))
        d_ext, d_cw, d_cb = _dwconv_bwd(XBC_HALO, (ext, cw), d_conv)
        dpxbc, dxbc = d_ext[0:XBC_HALO, :], d_ext[XBC_HALO:, :]
        dproj_ref[:, 0:1024] = dz.astype(BF16)
        dproj_ref[:, C_XBC:C_XBC + 1536] = add_tail(dxbc, pend_xbc[...]).astype(BF16)
        dproj_ref[:, C_DT:C_DT + 128] = ddtr.astype(BF16)
        dproj_ref[:, C_DT + 128:PROJ_W] = jnp.zeros((CHUNK, PROJ_W - C_DT - 128), BF16)
        pend_xbc[...] = dpxbc
        ds_scr[...] = ds_in
        for r, g in ((g_cw, d_cw), (g_cb, d_cb), (g_dtb, d_dtb), (g_alog, d_alog), (g_dsk, d_dsk), (g_nw, d_nw)):
            r[...] += g

    row = lambda b, j: (b * nc + nc - 1 - j, 0)
    prm_shapes = [(4, 1536), (1, 1536), (1, 128), (1, 128), (1, 128), (1, 1024), (1, 128), (CONF_KERNEL, 512),
                  (1, 512), (1, 512), (1, 512)]
    grid = (nb, nc)
    return _pallas_with_side(
        body, side, *_grid_ends(grid), 8 + n_prm, 2 + n_prm, name=name, grid=grid,
        out_shape=[jax.ShapeDtypeStruct((t, PROJ_W), BF16), jax.ShapeDtypeStruct((t, MIX_WIDTH), BF16)]
                  + [jax.ShapeDtypeStruct(s, F32) for s in prm_shapes],
        in_specs=[pl.BlockSpec((CHUNK, D_MODEL), row), pl.BlockSpec((CHUNK, PROJ_W), row)]
                 + _halo_specs(nc, lambda j: nc - 1 - j)
                 + [pl.BlockSpec((1, SSD_HEADS * SSD_HEAD_DIM, SSD_STATE), lambda b, j: (b * nc + nc - 1 - j, 0, 0)),
                    pl.BlockSpec((CHUNK, 512), row), _full((MIX_WIDTH, D_MODEL))] + _param_specs(),
        out_specs=[pl.BlockSpec((CHUNK, PROJ_W), row), pl.BlockSpec((CHUNK, MIX_WIDTH), row)]
                  + [_full(s) for s in prm_shapes],
        scratch_shapes=[pltpu.VMEM((SSD_HEADS * SSD_HEAD_DIM, SSD_STATE), F32), pltpu.VMEM((XBC_HALO, 1536), F32),
                        pltpu.VMEM((CHUNK, 256), F32), pltpu.VMEM((CONF_HALO, 1024), F32)],
        compiler_params=pltpu.CompilerParams(dimension_semantics=("arbitrary", "arbitrary"),
                                             vmem_limit_bytes=VMEM_LIMIT),
        args=(dxn, proj, proj, proj, proj, s_all, conv_all, w_out, *params))


def _gw_out(y_cat, dxn, name):
    t = y_cat.shape[0]
    tk = 512

    def body(y_ref, dxn_ref, out_ref, acc):
        k = pl.program_id(0)

        @pl.when(k == 0)
        def _():
            acc[...] = jnp.zeros_like(acc)

        acc[...] += _dot_tn(y_ref[...], dxn_ref[...])

        @pl.when(k == t // tk - 1)
        def _():
            out_ref[...] = acc[...].astype(BF16)

    out = pl.pallas_call(
        body, name=name, grid=(t // tk,),
        out_shape=jax.ShapeDtypeStruct((MIX_WIDTH, D_MODEL), BF16),
        in_specs=[pl.BlockSpec((tk, MIX_WIDTH), lambda k: (k, 0)), pl.BlockSpec((tk, D_MODEL), lambda k: (k, 0))],
        out_specs=_full((MIX_WIDTH, D_MODEL)),
        scratch_shapes=[pltpu.VMEM((MIX_WIDTH, D_MODEL), F32)],
        compiler_params=pltpu.CompilerParams(dimension_semantics=("arbitrary",), vmem_limit_bytes=VMEM_LIMIT),
    )(y_cat, dxn)
    return out.reshape(N_DEV, MIX_WIDTH // N_DEV, D_MODEL)


def _inproj_bwd_x(dproj, w, x, nw, dxn, name, side=None):
    t = x.shape[0]
    tm = 256

    def body(dp_ref, w_ref, x_ref, nw_ref, dxn_ref, dx_ref, gnw_ref):
        @pl.when(pl.program_id(0) == 0)
        def _():
            gnw_ref[...] = jnp.zeros_like(gnw_ref)

        dh = jnp.zeros((tm, D_MODEL), F32)
        for j in range(N_COL_TILES):
            sl = slice(j * COL_TILE, (j + 1) * COL_TILE)
            dh = dh + _dot_nt(dp_ref[:, sl], w_ref[:, sl])
        _, vjp = jax.vjp(_rmsnorm, x_ref[...], nw_ref[...])
        dx, dnw = vjp(dh)
        dx_ref[...] = dxn_ref[...] + dx
        gnw_ref[...] += dnw

    tok = lambda i: (i, 0)
    grid = (t // tm,)
    return _pallas_with_side(
        body, side, *_grid_ends(grid), 5, 2, name=name, grid=grid,
        out_shape=[jax.ShapeDtypeStruct((t, D_MODEL), F32), jax.ShapeDtypeStruct((1, D_MODEL), F32)],
        in_specs=[pl.BlockSpec((tm, PROJ_W), tok), _full((D_MODEL, PROJ_W)), pl.BlockSpec((tm, D_MODEL), tok),
                  _full((1, D_MODEL)), pl.BlockSpec((tm, D_MODEL), tok)],
        out_specs=[pl.BlockSpec((tm, D_MODEL), tok), _full((1, D_MODEL))],
        compiler_params=pltpu.CompilerParams(dimension_semantics=("arbitrary",), vmem_limit_bytes=VMEM_LIMIT),
        args=(dproj, w, x, nw, dxn))


def _inproj_bwd_w(h, dproj, name, side=None):
    t = h.shape[0]
    tk = 512

    def body(h_ref, dp_ref, gw_ref):
        @pl.when(pl.program_id(1) == 0)
        def _():
            gw_ref[...] = jnp.zeros_like(gw_ref)

        gw_ref[...] += _dot_tn(h_ref[...], dp_ref[...])

    grid = (N_COL_TILES, t // tk)
    return _pallas_with_side(
        body, side, *_grid_ends(grid), 2, 1, name=name, grid=grid,
        out_shape=[jax.ShapeDtypeStruct((D_MODEL, PROJ_W), F32)],
        in_specs=[pl.BlockSpec((tk, D_MODEL), lambda n, k: (k, 0)), pl.BlockSpec((tk, COL_TILE), lambda n, k: (k, n))],
        out_specs=[pl.BlockSpec((D_MODEL, COL_TILE), lambda n, k: (0, n))],
        compiler_params=pltpu.CompilerParams(dimension_semantics=("arbitrary", "arbitrary"),
                                             vmem_limit_bytes=VMEM_LIMIT),
        args=(h, dproj))


def _repack_runs():
    pieces = ((0, 2048, C_Z), (2048, 3584, C_XBC), (3584, 3600, C_DT), (3600, 4368, C_Q), (4368, D_IN_PROJ, C_CA))
    per = D_IN_PROJ // N_DEV
    runs = []
    for j in range(N_DEV):
        lo, hi = per * j, per * (j + 1)
        for a, b, dst in pieces:
            s, e = max(lo, a), min(hi, b)
            if s < e:
                runs.append((j, s - lo, e - lo, dst + s - a))
    return runs


def _repack_w_in(g, name):
    tr = 256

    def body(g_ref, o_ref):
        for j, a, b, dst in _repack_runs():
            o_ref[:, dst:dst + b - a] = g_ref[j, :, a:b]
        o_ref[:, C_DT + 16:PROJ_W] = jnp.zeros((tr, PROJ_W - C_DT - 16), g.dtype)

    return pl.pallas_call(
        body, name=name, grid=(D_MODEL // tr,),
        out_shape=jax.ShapeDtypeStruct((D_MODEL, PROJ_W), g.dtype),
        in_specs=[pl.BlockSpec((N_DEV, tr, D_IN_PROJ // N_DEV), lambda i: (0, i, 0))],
        out_specs=pl.BlockSpec((tr, PROJ_W), lambda i: (i, 0)),
        compiler_params=pltpu.CompilerParams(dimension_semantics=("arbitrary",)),
    )(g)


def _unpack_gw_in(g, name):
    tr = 256

    def body(g_ref, o_ref):
        for j, a, b, dst in _repack_runs():
            o_ref[j, :, a:b] = g_ref[:, dst:dst + b - a].astype(BF16)

    return pl.pallas_call(
        body, name=name, grid=(D_MODEL // tr,),
        out_shape=jax.ShapeDtypeStruct((N_DEV, D_MODEL, D_IN_PROJ // N_DEV), BF16),
        in_specs=[pl.BlockSpec((tr, PROJ_W), lambda i: (i, 0))],
        out_specs=pl.BlockSpec((N_DEV, tr, D_IN_PROJ // N_DEV), lambda i: (0, i, 0)),
        compiler_params=pltpu.CompilerParams(dimension_semantics=("arbitrary",)),
    )(g)


def _pair_sum(parts, name):
    n_dev, r, cdim = parts.shape
    n_chip = n_dev // 2
    by_chip = parts.reshape(n_chip, 2, r, cdim)

    def swap_body(p_ref, got_ref, send_sem, recv_sem):
        x, y, c = _my_place()
        cp = pltpu.make_async_remote_copy(
            src_ref=p_ref.at[:, pl.ds(1 - c, 1)], dst_ref=got_ref, send_sem=send_sem, recv_sem=recv_sem,
            device_id=(x, y, 1 - c), device_id_type=pl.DeviceIdType.MESH)
        cp.start()
        cp.wait()

    any_spec = pl.BlockSpec(memory_space=pl.ANY)
    got = pl.pallas_call(
        swap_body, name=name + "_swap",
        out_shape=jax.ShapeDtypeStruct((n_chip, 1, r, cdim), parts.dtype),
        in_specs=[any_spec], out_specs=any_spec,
        scratch_shapes=[pltpu.SemaphoreType.DMA, pltpu.SemaphoreType.DMA],
    )(by_chip)

    tr = 256

    def add_body(p_ref, got_ref, o_ref):
        mine = jnp.where(lax.axis_index("c") == 0, p_ref[0, 0], p_ref[0, 1])
        o_ref[0] = (mine.astype(F32) + got_ref[0, 0].astype(F32)).astype(o_ref.dtype)

    return pl.pallas_call(
        add_body, name=name + "_add", grid=(n_chip, r // tr),
        out_shape=jax.ShapeDtypeStruct((n_chip, r, cdim), parts.dtype),
        in_specs=[pl.BlockSpec((1, 2, tr, cdim), lambda k, i: (k, 0, i, 0)),
                  pl.BlockSpec((1, 1, tr, cdim), lambda k, i: (k, 0, i, 0))],
        out_specs=pl.BlockSpec((1, tr, cdim), lambda k, i: (k, i, 0)),
        compiler_params=pltpu.CompilerParams(dimension_semantics=("arbitrary", "arbitrary")),
    )(by_chip, got)


def _loss_head(x, fnw, target, name):
    t = x.shape[0]
    tm = 512

    def body(x_ref, w_ref, t_ref, dx_ref, loss_ref, gw_ref):
        @pl.when(pl.program_id(0) == 0)
        def _():
            loss_ref[...] = jnp.zeros_like(loss_ref)
            gw_ref[...] = jnp.zeros_like(gw_ref)

        y, vjp = jax.vjp(_rmsnorm, x_ref[...], w_ref[...])
        err = y - t_ref[...]
        loss_ref[...] += 0.5 * jnp.sum(jnp.mean(err * err, axis=-1, keepdims=True), axis=0, keepdims=True)
        dx, dw = vjp(err * (1.0 / D_MODEL))
        dx_ref[...] = dx
        gw_ref[...] += dw

    tok = lambda i: (i, 0)
    return pl.pallas_call(
        body, name=name, grid=(t // tm,),
        out_shape=[jax.ShapeDtypeStruct((t, D_MODEL), F32), jax.ShapeDtypeStruct((1, 1), F32),
                   jax.ShapeDtypeStruct((1, D_MODEL), F32)],
        in_specs=[pl.BlockSpec((tm, D_MODEL), tok), _full((1, D_MODEL)), pl.BlockSpec((tm, D_MODEL), tok)],
        out_specs=[pl.BlockSpec((tm, D_MODEL), tok), _full((1, 1)), _full((1, D_MODEL))],
        compiler_params=pltpu.CompilerParams(dimension_semantics=("arbitrary",)),
    )(x, fnw, target)


def _adamw(w, g, m, v):
    m = ADAM_B1 * m + (1.0 - ADAM_B1) * g
    v = ADAM_B2 * v + (1.0 - ADAM_B2) * jnp.square(g)
    m_hat = m / (1.0 - ADAM_B1 ** ADAM_STEP)
    v_hat = v / (1.0 - ADAM_B2 ** ADAM_STEP)
    delta = -ADAM_LR * (m_hat / (jnp.sqrt(v_hat) + ADAM_EPS) + ADAM_WD * w)
    return delta, m, v


def _reduce_adamw(parts, w, m, v, tr, name):
    depth = len(parts)
    p, r, cdim = parts[0].shape
    n_blk = r // tr

    def body(*refs):
        p_refs = refs[:depth]
        w_ref, m_ref, v_ref, g_ref, d_ref, nm_ref, nv_ref = refs[depth:]
        for layer in range(depth):
            @pl.when(pl.program_id(0) == layer)
            def _(p_ref=p_refs[layer]):
                g = p_ref[0].astype(F32)
                for i in range(1, p):
                    g = g + p_ref[i].astype(F32)
                g_ref[0] = g
                d_ref[0], nm_ref[0], nv_ref[0] = _adamw(w_ref[0], g, m_ref[0], v_ref[0])

    def parts_spec(layer):
        return pl.BlockSpec((p, tr, cdim), lambda d, i: (0, jnp.clip(i + (d - layer) * n_blk, 0, n_blk - 1), 0))

    blk = pl.BlockSpec((1, tr, cdim), lambda d, i: (d, i, 0))
    return pl.pallas_call(
        body, name=name, grid=(depth, n_blk),
        out_shape=[jax.ShapeDtypeStruct(w.shape, F32)] * 4,
        in_specs=[parts_spec(layer) for layer in range(depth)] + [blk, blk, blk],
        out_specs=[blk] * 4,
        compiler_params=pltpu.CompilerParams(dimension_semantics=("arbitrary", "arbitrary"),
                                             vmem_limit_bytes=VMEM_LIMIT),
    )(*parts, w, m, v)


def _adamw_small(ssum, entries, name):
    direct = [e[3] for e in entries if not isinstance(e[3], list)]
    n_direct = len(direct)

    def body(*refs):
        ssum_ref, direct_refs = refs[0], list(refs[1:1 + n_direct])
        ins = refs[1 + n_direct:1 + n_direct + 3 * len(entries)]
        outs = refs[1 + n_direct + 3 * len(entries):]
        for k, (w, _, _, grad) in enumerate(entries):
            w_ref, m_ref, v_ref = ins[3 * k:3 * k + 3]
            g_ref, d_ref, nm_ref, nv_ref = outs[4 * k:4 * k + 4]
            if isinstance(grad, list):
                for row, off in enumerate(grad):
                    rows = slice(row, row + 1)
                    g = ssum_ref[:, off:off + w.shape[1]]
                    g_ref[rows, :] = g
                    d_ref[rows, :], nm_ref[rows, :], nv_ref[rows, :] = _adamw(w_ref[rows, :], g, m_ref[rows, :],
                                                                              v_ref[rows, :])
            else:
                g = direct_refs.pop(0)[...]
                g_ref[...] = g
                d_ref[...], nm_ref[...], nv_ref[...] = _adamw(w_ref[...], g, m_ref[...], v_ref[...])

    vmem = pl.BlockSpec(memory_space=pltpu.VMEM)
    args = [ssum] + direct + [a for e in entries for a in e[:3]]
    res = pl.pallas_call(
        body, name=name,
        out_shape=[jax.ShapeDtypeStruct(e[0].shape, F32) for e in entries for _ in range(4)],
        in_specs=[vmem] * len(args), out_specs=[vmem] * (4 * len(entries)),
    )(*args)
    return [res[4 * k:4 * k + 4] for k in range(len(entries))]


def _reduce_adamw_cols(parts, w, m, v, name):
    depth = len(parts)
    _, r, cdim = parts[0].shape
    tc = 256

    def body(*refs):
        p_refs = refs[:depth]
        w_ref, m_ref, v_ref, g_ref, d_ref, nm_ref, nv_ref = refs[depth:]
        for layer in range(depth):
            g = p_refs[layer][0].astype(F32)
            for i in range(1, parts[layer].shape[0]):
                g = g + p_refs[layer][i].astype(F32)
            g = g.T
            g_ref[:, layer, :] = g
            d_ref[:, layer, :], nm_ref[:, layer, :], nv_ref[:, layer, :] = _adamw(
                w_ref[:, layer, :], g, m_ref[:, layer, :], v_ref[:, layer, :])

    view = lambda a: jnp.transpose(a, (2, 0, 1))
    blk = pl.BlockSpec((cdim, depth, tc), lambda i: (0, 0, i))
    outs = pl.pallas_call(
        body, name=name, grid=(r // tc,),
        out_shape=[jax.ShapeDtypeStruct((cdim, depth, r), F32)] * 4,
        in_specs=[pl.BlockSpec((a.shape[0], tc, cdim), lambda i: (0, i, 0)) for a in parts] + [blk, blk, blk],
        out_specs=[blk] * 4,
        compiler_params=pltpu.CompilerParams(dimension_semantics=("arbitrary",), vmem_limit_bytes=VMEM_LIMIT),
    )(*parts, view(w), view(m), view(v))
    return [jnp.transpose(o, (1, 2, 0)) for o in outs]


def _pad_lanes(v, width=128):
    return jnp.pad(v.reshape(1, -1), ((0, 0), (0, width - v.shape[-1])))


SMALL_FIELDS = (("norm_w", 1024), ("conv_b", 1536), ("dt_bias", 128), ("a_log", 128), ("d_skip", 128),
                ("ssd_norm_w", 1024), ("sinks", 128), ("dw_b", 512), ("ln_w", 512), ("ln_b", 512))


def kernel(x, norm_w, w_in, ssd_conv_w, ssd_conv_b, ssd_dt_bias, ssd_a_log, ssd_d, ssd_norm_w, attn_sinks, conf_dw_w, conf_dw_b, conf_ln_w, conf_ln_b, w_out, final_norm_w, loss_target, m_norm_w, m_w_in, m_ssd_conv_w, m_ssd_conv_b, m_ssd_dt_bias, m_ssd_a_log, m_ssd_d, m_ssd_norm_w, m_attn_sinks, m_conf_dw_w, m_conf_dw_b, m_conf_ln_w, m_conf_ln_b, m_w_out, m_final_norm_w, v_norm_w, v_w_in, v_ssd_conv_w, v_ssd_conv_b, v_ssd_dt_bias, v_ssd_a_log, v_ssd_d, v_ssd_norm_w, v_attn_sinks, v_conf_dw_w, v_conf_dw_b, v_conf_ln_w, v_conf_ln_b, v_w_out, v_final_norm_w):
    nb, seq, _ = x.shape
    depth = norm_w.shape[0]
    t = nb * seq
    me_idx = 4 * lax.axis_index("x") + 2 * lax.axis_index("y") + lax.axis_index("c")

    w_in_bf, w_out_bf = w_in.astype(BF16), w_out.astype(BF16)
    g_win0, g_cw, g_dw = _all_gather([w_in_bf[0], ssd_conv_w, conf_dw_w], "gather_weights")
    w_in_full = [_repack_w_in(g_win0, "repack_w_in_0")]
    w_out_full = []
    conv_w_full = [jnp.transpose(g_cw[:, l], (1, 0, 2)).reshape(4, 1536) for l in range(depth)]
    dw_w_full = [jnp.transpose(g_dw[:, l], (1, 0, 2)).reshape(CONF_KERNEL, 512) for l in range(depth)]

    def layer_params(l):
        return [conv_w_full[l], ssd_conv_b[l].reshape(1, -1), _pad_lanes(ssd_dt_bias[l]), _pad_lanes(ssd_a_log[l]),
                _pad_lanes(ssd_d[l]), ssd_norm_w[l].reshape(1, -1), _pad_lanes(attn_sinks[l]), dw_w_full[l],
                conf_dw_b[l].reshape(1, -1), conf_ln_w[l].reshape(1, -1), conf_ln_b[l].reshape(1, -1)]

    xs = [x.reshape(t, D_MODEL)]
    saved = []
    for l in range(depth):
        (proj, h), gathered = _inproj_fwd(xs[l], norm_w[l].reshape(1, -1), w_in_full[l], f"inproj_fwd_{l}",
                                          (GATHER, [w_out_bf[l]]))
        w_out_full.append(gathered[0].reshape(MIX_WIDTH, D_MODEL))
        side = (GATHER, [w_in_bf[l + 1]]) if l + 1 < depth else None
        (x_next, *kept), gathered = _mixer_fwd(xs[l], proj, w_out_full[l], layer_params(l), nb, f"mixer_fwd_{l}",
                                               side)
        if side:
            w_in_full.append(_repack_w_in(gathered[0], f"repack_w_in_{l + 1}"))
        saved.append((proj, h, kept))
        xs.append(x_next)
    dx, loss_part, g_fnw = _loss_head(xs[depth], final_norm_w.reshape(1, -1), loss_target.reshape(t, D_MODEL),
                                      "loss_head")

    cols_in = D_IN_PROJ // N_DEV
    rows_out = MIX_WIDTH // N_DEV
    small_rows = [None] * depth
    received = [None] * depth
    outgoing = None
    for l in reversed(range(depth)):
        proj, h, kept = saved[l]
        res, arrived = _mixer_bwd(dx, proj, *kept, w_out_full[l], layer_params(l), nb, f"mixer_bwd_{l}",
                                  (SCATTER, outgoing) if outgoing else None)
        if outgoing:
            received[l + 1][0] = arrived[0]
        dproj, y_cat = res[0], res[1]
        g_cw_l, g_cb, g_dtb, g_alog, g_dsk, g_nw, g_snk, g_dww, g_dwb, g_lnw, g_lnb = res[2:]
        gw_out_parts = _gw_out(y_cat, dx, f"gw_out_{l}")
        nw_l = norm_w[l].reshape(1, -1)
        (gw_in,), got_out = _inproj_bwd_w(h, dproj, f"inproj_bwd_w_{l}", (SCATTER, [gw_out_parts]))
        received[l] = [None, got_out[0]]
        if l > 0:
            (dx, g_norm), _ = _inproj_bwd_x(dproj, w_in_full[l], xs[l], nw_l, dx, f"inproj_bwd_x_{l}")
            outgoing = [_unpack_gw_in(gw_in, f"unpack_gw_in_{l}")]
        else:
            chip_parts = _pair_sum(_unpack_gw_in(gw_in, f"unpack_gw_in_{l}"), f"pair_sum_{l}")
            (dx, g_norm), got_in = _inproj_bwd_x(dproj, w_in_full[l], xs[l], nw_l, dx, f"inproj_bwd_x_{l}",
                                                 (CHIP_SCATTER, [chip_parts]))
            received[l][0] = got_in[0]
        small_rows[l] = [g_norm, g_cb, g_dtb, g_alog, g_dsk, g_nw, g_snk, g_dwb, g_lnw, g_lnb,
                         g_cw_l.reshape(1, -1), g_dww.reshape(1, -1)]
    grad_x = dx.reshape(nb, seq, D_MODEL)

    small = jnp.concatenate([piece for l in range(depth) for piece in small_rows[l]] + [g_fnw], axis=1)
    ssum = _exchange_small(small, "exchange_small")

    g_w_in, d_w_in, nm_w_in, nv_w_in = _reduce_adamw_cols([received[l][0] for l in range(depth)], w_in, m_w_in,
                                                          v_w_in, "adamw_w_in")
    g_w_out, d_w_out, nm_w_out, nv_w_out = _reduce_adamw([received[l][1] for l in range(depth)], w_out, m_w_out,
                                                         v_w_out, 256, "adamw_w_out")

    per_layer = sum(n for _, n in SMALL_FIELDS) + 4 * 1536 + CONF_KERNEL * 512
    given = {"norm_w": (norm_w, m_norm_w, v_norm_w), "conv_b": (ssd_conv_b, m_ssd_conv_b, v_ssd_conv_b),
             "dt_bias": (ssd_dt_bias, m_ssd_dt_bias, v_ssd_dt_bias), "a_log": (ssd_a_log, m_ssd_a_log, v_ssd_a_log),
             "d_skip": (ssd_d, m_ssd_d, v_ssd_d), "ssd_norm_w": (ssd_norm_w, m_ssd_norm_w, v_ssd_norm_w),
             "sinks": (attn_sinks, m_attn_sinks, v_attn_sinks), "dw_b": (conf_dw_b, m_conf_dw_b, v_conf_dw_b),
             "ln_w": (conf_ln_w, m_conf_ln_w, v_conf_ln_w), "ln_b": (conf_ln_b, m_conf_ln_b, v_conf_ln_b)}
    entries = []
    off = 0
    for fname, n in SMALL_FIELDS:
        entries.append((*given[fname], [l * per_layer + off for l in range(depth)]))
        off += n
    shard_grads = []
    for width, shard, kk in ((1536, 192, 4), (512, 64, CONF_KERNEL)):
        full = [ssum[:, l * per_layer + off:l * per_layer + off + kk * width].reshape(kk, width) for l in range(depth)]
        shard_grads.append(jnp.stack([lax.dynamic_slice(f, (0, me_idx * shard), (kk, shard)) for f in full], axis=0))
        off += kk * width
    entries.append((ssd_conv_w, m_ssd_conv_w, v_ssd_conv_w, shard_grads[0]))
    entries.append((conf_dw_w, m_conf_dw_w, v_conf_dw_w, shard_grads[1]))
    entries.append((final_norm_w.reshape(1, -1), m_final_norm_w.reshape(1, -1), v_final_norm_w.reshape(1, -1),
                    [depth * per_layer]))
    sm = _adamw_small(ssum, entries, "adamw_small")
    sm = {k: quad for k, quad in zip([f for f, _ in SMALL_FIELDS] + ["conv_w", "dw_w", "final"], sm)}

    def outputs(i, big_in_i, big_out_i):
        return [sm["norm_w"][i], big_in_i, sm["conv_w"][i], sm["conv_b"][i], sm["dt_bias"][i], sm["a_log"][i],
                sm["d_skip"][i], sm["ssd_norm_w"][i], sm["sinks"][i], sm["dw_w"][i], sm["dw_b"][i], sm["ln_w"][i],
                sm["ln_b"][i], big_out_i, sm["final"][i].reshape(-1)]

    loss = lax.psum(loss_part[0, 0], MESH_AXES)
    return (loss, grad_x, *outputs(0, g_w_in, g_w_out), *outputs(1, d_w_in, d_w_out),
            *outputs(2, nm_w_in, nm_w_out), *outputs(3, nv_w_in, nv_w_out))
```

```python
import functools

import jax
import jax.numpy as jnp
from jax import lax
from jax.experimental import pallas as pl
from jax.experimental.pallas import tpu as pltpu

F32 = jnp.float32
BF16 = jnp.bfloat16
MESH_AXES = ("x", "y", "c")
N_DEV = 8
EPS = 1e-5

D_MODEL = 1024
CHUNK = 128
SSD_HEADS = 16
SSD_HEAD_DIM = 64
SSD_STATE = 128
ATTN_HEADS = 8
ATTN_HEAD_DIM = 64
CONF_KERNEL = 31
MIX_WIDTH = 2048
D_IN_PROJ = 5392
C_Z = 0
C_CA = 2048
C_XBC = 3072
C_Q = 4608
C_K = 5120
C_V = 5248
C_DT = 5376
PROJ_W = 5632
N_COL_TILES = 4
COL_TILE = PROJ_W // N_COL_TILES
XBC_HALO = 8
CONF_HALO = 32
ATTN_STACK_FWD = 2
ATTN_STACK_BWD = 4
VMEM_LIMIT = 56 * 1024 * 1024

ADAM_LR = 0.001
ADAM_B1 = 0.9
ADAM_B2 = 0.999
ADAM_EPS = 1e-08
ADAM_WD = 0.01
ADAM_STEP = 10


def _silu(v):
    return v * jax.nn.sigmoid(v)


def _softplus(v):
    return jnp.maximum(v, 0.0) + jnp.log1p(jnp.exp(-jnp.abs(v)))


def _rmsnorm(v, w):
    return v * lax.rsqrt(jnp.mean(v * v, axis=-1, keepdims=True) + EPS) * w


def _dot(a, b):
    return jnp.dot(a.astype(BF16), b.astype(BF16), preferred_element_type=F32)


def _dot_nt(a, b):
    return lax.dot_general(a.astype(BF16), b.astype(BF16), (((1,), (1,)), ((), ())), preferred_element_type=F32)


def _dot_tn(a, b):
    return lax.dot_general(a.astype(BF16), b.astype(BF16), (((0,), (0,)), ((), ())), preferred_element_type=F32)


def _taps(ext, offs, out_len, w=None, g=None):
    n_rows, n_cols = ext.shape
    by_shift = {}
    for t, off in enumerate(offs):
        by_shift.setdefault(off % 8, []).append((t, off))
    for r, taps in by_shift.items():
        assert max(off for _, off in taps) - r + out_len <= n_rows - r
    accs = []
    sums = [[None] * (n_cols // 128) for _ in offs]
    for blk in range(n_cols // 128):
        cs = slice(blk * 128, (blk + 1) * 128)
        e = ext[:, cs]
        acc = None
        for r, taps in by_shift.items():
            shifted = e if r == 0 else pltpu.roll(e, n_rows - r, axis=0)
            for t, off in taps:
                window = shifted[off - r:off - r + out_len, :]
                if w is not None:
                    term = w[t:t + 1, cs] * window
                    acc = term if acc is None else acc + term
                if g is not None:
                    sums[t][blk] = jnp.sum(g[:, cs] * window, axis=0, keepdims=True)
        accs.append(acc)
    if w is not None:
        return jnp.concatenate(accs, axis=1)
    return jnp.concatenate([jnp.concatenate(row, axis=1) for row in sums], axis=0)


@functools.partial(jax.custom_vjp, nondiff_argnums=(3,))
def _dwconv(ext, w, b, halo):
    kk = w.shape[0]
    return b + _taps(ext, [halo - (kk - 1) + t for t in range(kk)], ext.shape[0] - halo, w=w)


def _dwconv_fwd(ext, w, b, halo):
    return _dwconv(ext, w, b, halo), (ext, w)


def _dwconv_bwd(halo, res, g):
    ext, w = res
    kk = w.shape[0]
    offs = [halo - (kk - 1) + t for t in range(kk)]
    dw = _taps(ext, offs, ext.shape[0] - halo, g=g)
    zeros = jnp.zeros((halo, g.shape[1]), g.dtype)
    gp = jnp.concatenate([zeros, g, zeros], axis=0)
    dext = _taps(gp, [halo - off for off in offs], ext.shape[0], w=w)
    return dext, dw, jnp.sum(g, axis=0, keepdims=True)


_dwconv.defvjp(_dwconv_fwd, _dwconv_bwd)


def _ssd_part(z_ssd, conv_out, dtr, s_in, dtb, alog, dsk, nw):
    qn = conv_out.shape[0]
    nh = SSD_HEADS
    per_group = nh // 2
    n_pair = nh // 2
    xa = _silu(conv_out)
    xs = xa[:, 0:1024]
    dt = _softplus(dtr + dtb)
    a = dt * (-jnp.exp(alog))
    rows = lax.broadcasted_iota(jnp.int32, (qn, qn), 0)
    cols = lax.broadcasted_iota(jnp.int32, (qn, qn), 1)
    causal = rows >= cols
    low = cols < SSD_HEAD_DIM
    a_cs = jnp.dot(causal.astype(F32), a, precision=lax.Precision.HIGHEST, preferred_element_type=F32)
    a_cs_t = a_cs.T
    bgs = [xa[:, 1024 + g * 128:1024 + (g + 1) * 128] for g in range(2)]
    cgs = [xa[:, 1280 + g * 128:1280 + (g + 1) * 128] for g in range(2)]
    cbms = [_dot_nt(cgs[g], bgs[g]) for g in range(2)]
    colb = [jnp.broadcast_to(a_cs[:, h:h + 1], (qn, qn)) for h in range(nh)]
    lastb = [jnp.broadcast_to(colb[h][qn - 1:qn, :], (qn, qn)) for h in range(nh)]
    dtb_wide = [jnp.broadcast_to(dt[:, h:h + 1], (qn, qn)) for h in range(nh)]
    lmats = [jnp.exp(jnp.where(causal, colb[h] - a_cs_t[h:h + 1, :], -jnp.inf)) for h in range(nh)]
    ms = [cbms[h // per_group] * lmats[h] for h in range(nh)]
    x_pair = [xs[:, p * 128:(p + 1) * 128] for p in range(n_pair)]
    x_lo = [x_pair[p] * jnp.where(low, dtb_wide[2 * p], 0.0) for p in range(n_pair)]
    x_hi = [x_pair[p] * jnp.where(low, 0.0, dtb_wide[2 * p + 1]) for p in range(n_pair)]
    y_diag = [_dot(ms[2 * p], x_lo[p]) + _dot(ms[2 * p + 1], x_hi[p]) for p in range(n_pair)]
    col_pair = [jnp.where(low, colb[2 * p], colb[2 * p + 1]) for p in range(n_pair)]
    last_pair = [jnp.where(low, lastb[2 * p], lastb[2 * p + 1]) for p in range(n_pair)]
    ecol = [jnp.exp(col_pair[p]) for p in range(n_pair)]
    xw = [(x_lo[p] + x_hi[p]) * jnp.exp(last_pair[p] - col_pair[p]) for p in range(n_pair)]
    y_off, st = [], []
    for g in range(2):
        ps = range(g * n_pair // 2, (g + 1) * n_pair // 2)
        y_off.append(_dot_nt(cgs[g], s_in[g * 512:(g + 1) * 512, :]) * jnp.concatenate([ecol[p] for p in ps], axis=1))
        st.append(_dot_tn(jnp.concatenate([xw[p] for p in ps], axis=1), bgs[g]))
    e_last = jnp.exp(jnp.broadcast_to(a_cs_t[:, qn - 1:qn], (qn, SSD_STATE)))
    scale = jnp.concatenate([jnp.broadcast_to(e_last[h:h + 1, :], (64, SSD_STATE)) for h in range(nh)], axis=0)
    s_out = scale * s_in + jnp.concatenate(st, axis=0)
    d_wide = jnp.concatenate([jnp.broadcast_to(dsk[:, h:h + 1], (1, 64)) for h in range(nh)], axis=1)
    y = jnp.concatenate(y_diag, axis=1) + jnp.concatenate(y_off, axis=1) + d_wide * xs
    gated = y * _silu(z_ssd)
    halves = []
    for g in range(2):
        gg = gated[:, g * 512:(g + 1) * 512]
        halves.append(gg * lax.rsqrt(jnp.mean(gg * gg, axis=-1, keepdims=True) + EPS))
    return jnp.concatenate(halves, axis=1) * nw, s_out


def _attn_part(z_attn, q, kv, p_kv, snk, kvmask, stack):
    qn = q.shape[0]
    kk = jnp.concatenate([p_kv[:, 0:128], kv[:, 0:128]], axis=0)
    vv = jnp.concatenate([p_kv[:, 128:256], kv[:, 128:256]], axis=0)
    units = range(ATTN_HEADS // stack)
    heads = [range(u * stack, (u + 1) * stack) for u in units]
    kv_of = [u * stack // (ATTN_HEADS // 2) for u in units]
    k_of = [kk[:, g * 64:(g + 1) * 64] for g in kv_of]
    v_of = [vv[:, g * 64:(g + 1) * 64] for g in kv_of]
    qs = [jnp.concatenate([q[:, h * 64:(h + 1) * 64] for h in heads[u]], axis=0) for u in units]
    sk = [jnp.concatenate([jnp.broadcast_to(snk[:, h:h + 1], (qn, 1)) for h in heads[u]], axis=0) for u in units]
    s = [jnp.where(kvmask, _dot_nt(qs[u], k_of[u]) * (ATTN_HEAD_DIM ** -0.5), -jnp.inf) for u in units]
    m = [lax.stop_gradient(jnp.maximum(jnp.max(s[u], axis=-1, keepdims=True), sk[u])) for u in units]
    e = [jnp.exp(s[u] - m[u]) for u in units]
    r_den = [1.0 / (jnp.sum(e[u], axis=-1, keepdims=True) + jnp.exp(sk[u] - m[u])) for u in units]
    o = [_dot(e[u], v_of[u]) * r_den[u] for u in units]
    outs = [o[u][i * qn:(i + 1) * qn, :] for u in units for i in range(stack)]
    return jnp.concatenate(outs, axis=1) * _silu(z_attn)


def _conf_glu(cacg, p_cc):
    c0 = cacg[:, 0:512] * jax.nn.sigmoid(cacg[:, 512:1024])
    pc0 = p_cc[:, 0:512] * jax.nn.sigmoid(p_cc[:, 512:1024])
    return jnp.concatenate([pc0, c0], axis=0)


def _conf_tail(conv_out, z_conf, lnw, lnb):
    xc = conv_out - jnp.mean(conv_out, axis=-1, keepdims=True)
    yln = xc * lax.rsqrt(jnp.mean(xc * xc, axis=-1, keepdims=True) + EPS) * lnw + lnb
    return _silu(yln) * _silu(z_conf)


def _kv_mask(qn, not_first, reps):
    ii = lax.broadcasted_iota(jnp.int32, (reps * qn, 2 * qn), 0) & (qn - 1)
    jj = lax.broadcasted_iota(jnp.int32, (reps * qn, 2 * qn), 1)
    d = jj - ii
    return (d >= 1) & (d <= qn) & (not_first | (jj >= qn))


def _my_place():
    return lax.axis_index("x"), lax.axis_index("y"), lax.axis_index("c")


def _all_gather(arrs, name):
    n = len(arrs)

    def body(*refs):
        ins, outs = refs[:n], refs[n:2 * n]
        send_sems, recv_sems, local_sems = refs[2 * n:]
        x, y, c = _my_place()
        me, sibling = (x, y, c), (x, y, 1 - c)
        chips = [(1 - x, y), (x, 1 - y), (1 - x, 1 - y)]

        def slot(a, p):
            return outs[a].at[4 * p[0] + 2 * p[1] + p[2]]

        def copy(a, kk, block, to, src=None):
            return pltpu.make_async_remote_copy(
                src_ref=slot(a, block) if src is None else src, dst_ref=slot(a, block),
                send_sem=send_sems.at[a, kk], recv_sem=recv_sems.at[a, kk],
                device_id=to, device_id_type=pl.DeviceIdType.MESH)

        mine = [pltpu.make_async_copy(ins[a], slot(a, me), local_sems.at[a]) for a in range(n)]
        for cp in mine:
            cp.start()
        first = []
        for a in range(n):
            first.append(copy(a, 0, me, sibling, src=ins[a]))
            first += [copy(a, 1 + j, me, (*chip, c), src=ins[a]) for j, chip in enumerate(chips)]
        for cp in first:
            cp.start()
        passed = []
        for j, chip in enumerate(chips):
            for a in range(n):
                copy(a, 1 + j, (*chip, c), me).wait_recv()
                fwd = copy(a, 4 + j, (*chip, c), sibling)
                fwd.start()
                passed.append(fwd)
        for a in range(n):
            copy(a, 0, sibling, me).wait_recv()
            for j, chip in enumerate(chips):
                copy(a, 4 + j, (*chip, 1 - c), me).wait_recv()
        for cp in first + passed:
            cp.wait_send()
        for cp in mine:
            cp.wait()

    any_spec = pl.BlockSpec(memory_space=pl.ANY)
    return pl.pallas_call(
        body, name=name,
        out_shape=[jax.ShapeDtypeStruct((N_DEV,) + a.shape, a.dtype) for a in arrs],
        in_specs=[any_spec] * n, out_specs=[any_spec] * n,
        scratch_shapes=[pltpu.SemaphoreType.DMA((n, 7)), pltpu.SemaphoreType.DMA((n, 7)),
                        pltpu.SemaphoreType.DMA((n,))],
    )(*arrs)


GATHER, SCATTER, CHIP_SCATTER = "gather", "scatter", "chip_scatter"


def _direct_copies(mode, ins, outs, send_sems, recv_sems, local_sems):
    x, y, c = _my_place()
    by_chip = mode == CHIP_SCATTER
    place = (lambda px, py, pc: 2 * px + py) if by_chip else (lambda px, py, pc: 4 * px + 2 * py + pc)
    me_idx = place(x, y, c)
    n = len(ins)
    local = [pltpu.make_async_copy(ins[a] if mode == GATHER else ins[a].at[me_idx], outs[a].at[me_idx],
                                   local_sems.at[a]) for a in range(n)]
    remote = []
    for rel in range(1, N_DEV):
        if by_chip and rel & 1:
            continue
        px = 1 - x if rel & 4 else x
        py = 1 - y if rel & 2 else y
        pc = 1 - c if rel & 1 else c
        for a in range(n):
            remote.append(pltpu.make_async_remote_copy(
                src_ref=ins[a] if mode == GATHER else ins[a].at[place(px, py, pc)], dst_ref=outs[a].at[me_idx],
                send_sem=send_sems.at[a, rel - 1], recv_sem=recv_sems.at[a, rel - 1],
                device_id=(px, py, pc), device_id_type=pl.DeviceIdType.MESH))
    return local + remote


def _side_scratch(n):
    return [pltpu.SemaphoreType.DMA((n, 7)), pltpu.SemaphoreType.DMA((n, 7)), pltpu.SemaphoreType.DMA((n,))]


def _side_out_shapes(mode, arrs):
    return [jax.ShapeDtypeStruct((N_DEV,) + a.shape if mode == GATHER else a.shape, a.dtype) for a in arrs]


def _pallas_with_side(body, side, first, last, n_in, n_out, *, in_specs, out_specs, out_shape, scratch_shapes=(),
                      args, **kwargs):
    side_arrs = [] if side is None else list(side[1])
    ns = len(side_arrs)

    def wrapped(*refs):
        own_in, side_in = refs[:n_in], refs[n_in:n_in + ns]
        o = n_in + ns
        own_out, side_out = refs[o:o + n_out], refs[o + n_out:o + n_out + ns]
        scratch = refs[o + n_out + ns:]
        own_scratch, sems = (scratch[:-3], scratch[-3:]) if ns else (scratch, ())
        body(*own_in, *own_out, *own_scratch)
        if ns:
            @pl.when(first())
            def _():
                for cp in _direct_copies(side[0], side_in, side_out, *sems):
                    cp.start()

            @pl.when(last())
            def _():
                for cp in _direct_copies(side[0], side_in, side_out, *sems):
                    cp.wait()

    any_spec = pl.BlockSpec(memory_space=pl.ANY)
    res = pl.pallas_call(
        wrapped,
        in_specs=list(in_specs) + [any_spec] * ns,
        out_specs=list(out_specs) + [any_spec] * ns,
        out_shape=list(out_shape) + (_side_out_shapes(side[0], side_arrs) if ns else []),
        scratch_shapes=list(scratch_shapes) + (_side_scratch(ns) if ns else []),
        **kwargs,
    )(*args, *side_arrs)
    return res[:n_out], res[n_out:]


def _exchange_small(small, name):
    ns = small.shape[1]

    def body(small_ref, sum_ref, small_all, send_sems, recv_sems, local_sems):
        copies = _direct_copies(GATHER, [small_ref], [small_all], send_sems, recv_sems, local_sems)
        for cp in copies:
            cp.start()
        for cp in copies:
            cp.wait()
        total = small_all[0]
        for i in range(1, N_DEV):
            total = total + small_all[i]
        sum_ref[...] = total

    vmem_spec = pl.BlockSpec(memory_space=pltpu.VMEM)
    return pl.pallas_call(
        body, name=name,
        out_shape=jax.ShapeDtypeStruct((1, ns), F32),
        in_specs=[vmem_spec], out_specs=vmem_spec,
        scratch_shapes=[pltpu.VMEM((N_DEV, 1, ns), F32)] + _side_scratch(1),
    )(small)


def _full(shape):
    return pl.BlockSpec(shape, lambda *_: (0,) * len(shape))


def _inproj_fwd(x, nw, w, name, side=None):
    t = x.shape[0]
    tm = 256

    def body(x_ref, nw_ref, w_ref, proj_ref, h_ref):
        h = _rmsnorm(x_ref[...], nw_ref[...]).astype(BF16)
        h_ref[...] = h
        for j in range(N_COL_TILES):
            sl = slice(j * COL_TILE, (j + 1) * COL_TILE)
            proj_ref[:, sl] = jnp.dot(h, w_ref[:, sl], preferred_element_type=F32)

    grid = (t // tm,)
    return _pallas_with_side(
        body, side, *_grid_ends(grid), 3, 2, name=name, grid=grid,
        out_shape=[jax.ShapeDtypeStruct((t, PROJ_W), F32), jax.ShapeDtypeStruct((t, D_MODEL), BF16)],
        in_specs=[pl.BlockSpec((tm, D_MODEL), lambda i: (i, 0)), _full((1, D_MODEL)), _full((D_MODEL, PROJ_W))],
        out_specs=[pl.BlockSpec((tm, PROJ_W), lambda i: (i, 0)), pl.BlockSpec((tm, D_MODEL), lambda i: (i, 0))],
        compiler_params=pltpu.CompilerParams(dimension_semantics=("arbitrary",), vmem_limit_bytes=VMEM_LIMIT),
        args=(x, nw, w))


def _param_specs():
    return [_full((4, 1536)), _full((1, 1536)), _full((1, 128)), _full((1, 128)), _full((1, 128)),
            _full((1, 1024)), _full((1, 128)), _full((CONF_KERNEL, 512)), _full((1, 512)), _full((1, 512)),
            _full((1, 512))]


def _halo_specs(nc, chunk_of):
    def prev_chunk(b, j):
        return jnp.maximum(b * nc + chunk_of(j) - 1, 0)

    per_xbc = CHUNK // XBC_HALO
    per_cc = CHUNK // CONF_HALO
    return [
        pl.BlockSpec((XBC_HALO, 1536), lambda b, j: (prev_chunk(b, j) * per_xbc + per_xbc - 1, C_XBC // 1536)),
        pl.BlockSpec((CHUNK, 256), lambda b, j: (prev_chunk(b, j), C_K // 256)),
        pl.BlockSpec((CONF_HALO, 1024), lambda b, j: (prev_chunk(b, j) * per_cc + per_cc - 1, C_CA // 1024)),
    ]


def _grid_ends(grid):
    first = lambda: functools.reduce(lambda p, q: p & q, [pl.program_id(i) == 0 for i in range(len(grid))])
    last = lambda: functools.reduce(lambda p, q: p & q, [pl.program_id(i) == n - 1 for i, n in enumerate(grid)])
    return first, last


def _mixer_fwd(x, proj, w_out, params, nb, name, side=None):
    t = x.shape[0]
    nc = t // nb // CHUNK

    def body(x_ref, cur_ref, pxbc_ref, pkv_ref, pcc_ref, wo_ref, *rest):
        prm = [r[...] for r in rest[:11]]
        xn_ref, sall_ref, conv_ref, s_scr = rest[11:]
        c = pl.program_id(1)
        not_first = c > 0
        nf = not_first.astype(F32)

        @pl.when(c == 0)
        def _():
            s_scr[...] = jnp.zeros_like(s_scr)

        cw, cb, dtb, alog, dsk, nw, snk, dww, dwb, lnw, lnb = prm
        s_in = s_scr[...]
        sall_ref[0] = s_in
        ssd_conv = _dwconv(jnp.concatenate([pxbc_ref[...] * nf, cur_ref[:, C_XBC:C_XBC + 1536]], axis=0), cw, cb,
                           XBC_HALO)
        y_ssd, s_out = _ssd_part(cur_ref[:, 0:1024], ssd_conv, cur_ref[:, C_DT:C_DT + 128], s_in, dtb, alog, dsk, nw)
        s_scr[...] = s_out
        y_attn = _attn_part(cur_ref[:, 1024:1536], cur_ref[:, C_Q:C_Q + 512], cur_ref[:, C_K:C_K + 256],
                            pkv_ref[...] * nf, snk, _kv_mask(CHUNK, not_first, ATTN_STACK_FWD), ATTN_STACK_FWD)
        conv_out = _dwconv(_conf_glu(cur_ref[:, C_CA:C_CA + 1024], pcc_ref[...] * nf), dww, dwb, CONF_HALO)
        conv_ref[...] = conv_out
        y_conf = _conf_tail(conv_out, cur_ref[:, 1536:2048], lnw, lnb)
        xn_ref[...] = (x_ref[...] + _dot(y_ssd, wo_ref[0:1024, :]) + _dot(y_attn, wo_ref[1024:1536, :])
                       + _dot(y_conf, wo_ref[1536:2048, :]))

    row = lambda b, j: (b * nc + j, 0)
    grid = (nb, nc)
    return _pallas_with_side(
        body, side, *_grid_ends(grid), 17, 3, name=name, grid=grid,
        out_shape=[jax.ShapeDtypeStruct((t, D_MODEL), F32),
                   jax.ShapeDtypeStruct((nb * nc, SSD_HEADS * SSD_HEAD_DIM, SSD_STATE), F32),
                   jax.ShapeDtypeStruct((t, 512), F32)],
        in_specs=[pl.BlockSpec((CHUNK, D_MODEL), row), pl.BlockSpec((CHUNK, PROJ_W), row)]
                 + _halo_specs(nc, lambda j: j) + [_full((MIX_WIDTH, D_MODEL))] + _param_specs(),
        out_specs=[pl.BlockSpec((CHUNK, D_MODEL), row),
                   pl.BlockSpec((1, SSD_HEADS * SSD_HEAD_DIM, SSD_STATE), lambda b, j: (b * nc + j, 0, 0)),
                   pl.BlockSpec((CHUNK, 512), row)],
        scratch_shapes=[pltpu.VMEM((SSD_HEADS * SSD_HEAD_DIM, SSD_STATE), F32)],
        compiler_params=pltpu.CompilerParams(dimension_semantics=("arbitrary", "arbitrary"),
                                             vmem_limit_bytes=VMEM_LIMIT),
        args=(x, proj, proj, proj, proj, w_out, *params))


def _mixer_bwd(dxn, proj, s_all, conv_all, w_out, params, nb, name):
    t = dxn.shape[0]
    nc = t // nb // CHUNK
    n_prm = 11

    def body(dxn_ref, cur_ref, pxbc_ref, pkv_ref, pcc_ref, s_ref, conv_ref, wo_ref, *rest):
        prm = [r[...] for r in rest[:n_prm]]
        dproj_ref, ycat_ref = rest[n_prm:n_prm + 2]
        gprm = rest[n_prm + 2:2 * n_prm + 2]
        ds_scr, pend_xbc, pend_kv, pend_cc = rest[2 * n_prm + 2:]
        b, j = pl.program_id(0), pl.program_id(1)
        c = nc - 1 - j
        not_first = c > 0
        nf = not_first.astype(F32)

        @pl.when((b == 0) & (j == 0))
        def _():
            for r in gprm:
                r[...] = jnp.zeros_like(r)

        @pl.when(j == 0)
        def _():
            ds_scr[...] = jnp.zeros_like(ds_scr)
            pend_xbc[...] = jnp.zeros_like(pend_xbc)
            pend_kv[...] = jnp.zeros_like(pend_kv)
            pend_cc[...] = jnp.zeros_like(pend_cc)

        cw, cb, dtb, alog, dsk, nw, snk, dww, dwb, lnw, lnb = prm
        g_cw, g_cb, g_dtb, g_alog, g_dsk, g_nw, g_snk, g_dww, g_dwb, g_lnw, g_lnb = gprm
        dxn_v = dxn_ref[...]

        def add_tail(d_cur, pending):
            lead = jnp.zeros((CHUNK - pending.shape[0], pending.shape[1]), F32)
            return d_cur + jnp.concatenate([lead, pending], axis=0)

        y, vjp = jax.vjp(_conf_tail, conv_ref[...], cur_ref[:, 1536:2048], lnw, lnb)
        ycat_ref[:, 1536:2048] = y.astype(BF16)
        d_conv, dz, d_lnw, d_lnb = vjp(_dot_nt(dxn_v, wo_ref[1536:2048, :]))
        ext, vjp = jax.vjp(_conf_glu, cur_ref[:, C_CA:C_CA + 1024], pcc_ref[...] * nf)
        d_ext, d_dww, d_dwb = _dwconv_bwd(CONF_HALO, (ext, dww), d_conv)
        dcacg, dpcc = vjp(d_ext)
        dproj_ref[:, 1536:2048] = dz.astype(BF16)
        dproj_ref[:, C_CA:C_CA + 1024] = add_tail(dcacg, pend_cc[...]).astype(BF16)
        pend_cc[...] = dpcc
        for r, g in ((g_dww, d_dww), (g_dwb, d_dwb), (g_lnw, d_lnw), (g_lnb, d_lnb)):
            r[...] += g

        attn = functools.partial(_attn_part, kvmask=_kv_mask(CHUNK, not_first, ATTN_STACK_BWD),
                                 stack=ATTN_STACK_BWD)
        y, vjp = jax.vjp(attn, cur_ref[:, 1024:1536], cur_ref[:, C_Q:C_Q + 512], cur_ref[:, C_K:C_K + 256],
                         pkv_ref[...] * nf, snk)
        ycat_ref[:, 1024:1536] = y.astype(BF16)
        dz, dq, dkv, dpkv, d_snk = vjp(_dot_nt(dxn_v, wo_ref[1024:1536, :]))
        dproj_ref[:, 1024:1536] = dz.astype(BF16)
        dproj_ref[:, C_Q:C_Q + 512] = dq.astype(BF16)
        dproj_ref[:, C_K:C_K + 256] = (dkv + pend_kv[...]).astype(BF16)
        pend_kv[...] = dpkv
        g_snk[...] += d_snk

        ext = jnp.concatenate([pxbc_ref[...] * nf, cur_ref[:, C_XBC:C_XBC + 1536]], axis=0)
        (y, _), vjp = jax.vjp(_ssd_part, cur_ref[:, 0:1024], _dwconv(ext, cw, cb, XBC_HALO),
                              cur_ref[:, C_DT:C_DT + 128], s_ref[0], dtb, alog, dsk, nw)
        ycat_ref[:, 0:1024] = y.astype(BF16)
        dz, d_conv, ddtr, ds_in, d_dtb, d_alog, d_dsk, d_nw = vjp((_dot_nt(dxn_v, wo_ref[0:1024, :]), ds_scr[...]))
        d_ext, d_cw, d_cb = _dwconv_bwd(XBC_HALO, (ext, cw), d_conv)
        dpxbc, dxbc = d_ext[0:XBC_HALO, :], d_ext[XBC_HALO:, :]
        dproj_ref[:, 0:1024] = dz.astype(BF16)
        dproj_ref[:, C_XBC:C_XBC + 1536] = add_tail(dxbc, pend_xbc[...]).astype(BF16)
        dproj_ref[:, C_DT:C_DT + 128] = ddtr.astype(BF16)
        dproj_ref[:, C_DT + 128:PROJ_W] = jnp.zeros((CHUNK, PROJ_W - C_DT - 128), BF16)
        pend_xbc[...] = dpxbc
        ds_scr[...] = ds_in
        for r, g in ((g_cw, d_cw), (g_cb, d_cb), (g_dtb, d_dtb), (g_alog, d_alog), (g_dsk, d_dsk), (g_nw, d_nw)):
            r[...] += g

    row = lambda b, j: (b * nc + nc - 1 - j, 0)
    prm_shapes = [(4, 1536), (1, 1536), (1, 128), (1, 128), (1, 128), (1, 1024), (1, 128), (CONF_KERNEL, 512),
                  (1, 512), (1, 512), (1, 512)]
    grid = (nb, nc)
    return _pallas_with_side(
        body, None, *_grid_ends(grid), 8 + n_prm, 2 + n_prm, name=name, grid=grid,
        out_shape=[jax.ShapeDtypeStruct((t, PROJ_W), BF16), jax.ShapeDtypeStruct((t, MIX_WIDTH), BF16)]
                  + [jax.ShapeDtypeStruct(s, F32) for s in prm_shapes],
        in_specs=[pl.BlockSpec((CHUNK, D_MODEL), row), pl.BlockSpec((CHUNK, PROJ_W), row)]
                 + _halo_specs(nc, lambda j: nc - 1 - j)
                 + [pl.BlockSpec((1, SSD_HEADS * SSD_HEAD_DIM, SSD_STATE), lambda b, j: (b * nc + nc - 1 - j, 0, 0)),
                    pl.BlockSpec((CHUNK, 512), row), _full((MIX_WIDTH, D_MODEL))] + _param_specs(),
        out_specs=[pl.BlockSpec((CHUNK, PROJ_W), row), pl.BlockSpec((CHUNK, MIX_WIDTH), row)]
                  + [_full(s) for s in prm_shapes],
        scratch_shapes=[pltpu.VMEM((SSD_HEADS * SSD_HEAD_DIM, SSD_STATE), F32), pltpu.VMEM((XBC_HALO, 1536), F32),
                        pltpu.VMEM((CHUNK, 256), F32), pltpu.VMEM((CONF_HALO, 1024), F32)],
        compiler_params=pltpu.CompilerParams(dimension_semantics=("arbitrary", "arbitrary"),
                                             vmem_limit_bytes=VMEM_LIMIT),
        args=(dxn, proj, proj, proj, proj, s_all, conv_all, w_out, *params))


def _gw_out(y_cat, dxn, name):
    t = y_cat.shape[0]
    tk = 512

    def body(y_ref, dxn_ref, out_ref, acc):
        k = pl.program_id(0)

        @pl.when(k == 0)
        def _():
            acc[...] = jnp.zeros_like(acc)

        acc[...] += _dot_tn(y_ref[...], dxn_ref[...])

        @pl.when(k == t // tk - 1)
        def _():
            out_ref[...] = acc[...].astype(BF16)

    out = pl.pallas_call(
        body, name=name, grid=(t // tk,),
        out_shape=jax.ShapeDtypeStruct((MIX_WIDTH, D_MODEL), BF16),
        in_specs=[pl.BlockSpec((tk, MIX_WIDTH), lambda k: (k, 0)), pl.BlockSpec((tk, D_MODEL), lambda k: (k, 0))],
        out_specs=_full((MIX_WIDTH, D_MODEL)),
        scratch_shapes=[pltpu.VMEM((MIX_WIDTH, D_MODEL), F32)],
        compiler_params=pltpu.CompilerParams(dimension_semantics=("arbitrary",), vmem_limit_bytes=VMEM_LIMIT),
    )(y_cat, dxn)
    return out.reshape(N_DEV, MIX_WIDTH // N_DEV, D_MODEL)


def _inproj_bwd_x(dproj, w, x, nw, dxn, name, side=None):
    t = x.shape[0]
    tm = 256

    def body(dp_ref, w_ref, x_ref, nw_ref, dxn_ref, dx_ref, gnw_ref):
        @pl.when(pl.program_id(0) == 0)
        def _():
            gnw_ref[...] = jnp.zeros_like(gnw_ref)

        dh = jnp.zeros((tm, D_MODEL), F32)
        for j in range(N_COL_TILES):
            sl = slice(j * COL_TILE, (j + 1) * COL_TILE)
            dh = dh + _dot_nt(dp_ref[:, sl], w_ref[:, sl])
        _, vjp = jax.vjp(_rmsnorm, x_ref[...], nw_ref[...])
        dx, dnw = vjp(dh)
        dx_ref[...] = dxn_ref[...] + dx
        gnw_ref[...] += dnw

    tok = lambda i: (i, 0)
    grid = (t // tm,)
    return _pallas_with_side(
        body, side, *_grid_ends(grid), 5, 2, name=name, grid=grid,
        out_shape=[jax.ShapeDtypeStruct((t, D_MODEL), F32), jax.ShapeDtypeStruct((1, D_MODEL), F32)],
        in_specs=[pl.BlockSpec((tm, PROJ_W), tok), _full((D_MODEL, PROJ_W)), pl.BlockSpec((tm, D_MODEL), tok),
                  _full((1, D_MODEL)), pl.BlockSpec((tm, D_MODEL), tok)],
        out_specs=[pl.BlockSpec((tm, D_MODEL), tok), _full((1, D_MODEL))],
        compiler_params=pltpu.CompilerParams(dimension_semantics=("arbitrary",), vmem_limit_bytes=VMEM_LIMIT),
        args=(dproj, w, x, nw, dxn))


def _inproj_bwd_w(h, dproj, name, side=None):
    t = h.shape[0]
    tk = 512

    def body(h_ref, dp_ref, gw_ref):
        @pl.when(pl.program_id(1) == 0)
        def _():
            gw_ref[...] = jnp.zeros_like(gw_ref)

        gw_ref[...] += _dot_tn(h_ref[...], dp_ref[...])

    grid = (N_COL_TILES, t // tk)
    return _pallas_with_side(
        body, side, *_grid_ends(grid), 2, 1, name=name, grid=grid,
        out_shape=[jax.ShapeDtypeStruct((D_MODEL, PROJ_W), F32)],
        in_specs=[pl.BlockSpec((tk, D_MODEL), lambda n, k: (k, 0)), pl.BlockSpec((tk, COL_TILE), lambda n, k: (k, n))],
        out_specs=[pl.BlockSpec((D_MODEL, COL_TILE), lambda n, k: (0, n))],
        compiler_params=pltpu.CompilerParams(dimension_semantics=("arbitrary", "arbitrary"),
                                             vmem_limit_bytes=VMEM_LIMIT),
        args=(h, dproj))


def _repack_runs():
    pieces = ((0, 2048, C_Z), (2048, 3584, C_XBC), (3584, 3600, C_DT), (3600, 4368, C_Q), (4368, D_IN_PROJ, C_CA))
    per = D_IN_PROJ // N_DEV
    runs = []
    for j in range(N_DEV):
        lo, hi = per * j, per * (j + 1)
        for a, b, dst in pieces:
            s, e = max(lo, a), min(hi, b)
            if s < e:
                runs.append((j, s - lo, e - lo, dst + s - a))
    return runs


def _repack_w_in(g, name):
    tr = 256

    def body(g_ref, o_ref):
        for j, a, b, dst in _repack_runs():
            o_ref[:, dst:dst + b - a] = g_ref[j, :, a:b]
        o_ref[:, C_DT + 16:PROJ_W] = jnp.zeros((tr, PROJ_W - C_DT - 16), g.dtype)

    return pl.pallas_call(
        body, name=name, grid=(D_MODEL // tr,),
        out_shape=jax.ShapeDtypeStruct((D_MODEL, PROJ_W), g.dtype),
        in_specs=[pl.BlockSpec((N_DEV, tr, D_IN_PROJ // N_DEV), lambda i: (0, i, 0))],
        out_specs=pl.BlockSpec((tr, PROJ_W), lambda i: (i, 0)),
        compiler_params=pltpu.CompilerParams(dimension_semantics=("arbitrary",)),
    )(g)


def _unpack_gw_in(g, name):
    tr = 256

    def body(g_ref, o_ref):
        for j, a, b, dst in _repack_runs():
            o_ref[j, :, a:b] = g_ref[:, dst:dst + b - a].astype(BF16)

    return pl.pallas_call(
        body, name=name, grid=(D_MODEL // tr,),
        out_shape=jax.ShapeDtypeStruct((N_DEV, D_MODEL, D_IN_PROJ // N_DEV), BF16),
        in_specs=[pl.BlockSpec((tr, PROJ_W), lambda i: (i, 0))],
        out_specs=pl.BlockSpec((N_DEV, tr, D_IN_PROJ // N_DEV), lambda i: (0, i, 0)),
        compiler_params=pltpu.CompilerParams(dimension_semantics=("arbitrary",)),
    )(g)


def _pair_sum(parts, name):
    n_dev, r, cdim = parts.shape
    n_chip = n_dev // 2
    by_chip = parts.reshape(n_chip, 2, r, cdim)

    def swap_body(p_ref, got_ref, send_sem, recv_sem):
        x, y, c = _my_place()
        cp = pltpu.make_async_remote_copy(
            src_ref=p_ref.at[:, pl.ds(1 - c, 1)], dst_ref=got_ref, send_sem=send_sem, recv_sem=recv_sem,
            device_id=(x, y, 1 - c), device_id_type=pl.DeviceIdType.MESH)
        cp.start()
        cp.wait()

    any_spec = pl.BlockSpec(memory_space=pl.ANY)
    got = pl.pallas_call(
        swap_body, name=name + "_swap",
        out_shape=jax.ShapeDtypeStruct((n_chip, 1, r, cdim), parts.dtype),
        in_specs=[any_spec], out_specs=any_spec,
        scratch_shapes=[pltpu.SemaphoreType.DMA, pltpu.SemaphoreType.DMA],
    )(by_chip)

    tr = 256

    def add_body(p_ref, got_ref, o_ref):
        mine = jnp.where(lax.axis_index("c") == 0, p_ref[0, 0], p_ref[0, 1])
        o_ref[0] = (mine.astype(F32) + got_ref[0, 0].astype(F32)).astype(o_ref.dtype)

    return pl.pallas_call(
        add_body, name=name + "_add", grid=(n_chip, r // tr),
        out_shape=jax.ShapeDtypeStruct((n_chip, r, cdim), parts.dtype),
        in_specs=[pl.BlockSpec((1, 2, tr, cdim), lambda k, i: (k, 0, i, 0)),
                  pl.BlockSpec((1, 1, tr, cdim), lambda k, i: (k, 0, i, 0))],
        out_specs=pl.BlockSpec((1, tr, cdim), lambda k, i: (k, i, 0)),
        compiler_params=pltpu.CompilerParams(dimension_semantics=("arbitrary", "arbitrary")),
    )(by_chip, got)


def _loss_head(x, fnw, target, name):
    t = x.shape[0]
    tm = 512

    def body(x_ref, w_ref, t_ref, dx_ref, loss_ref, gw_ref):
        @pl.when(pl.program_id(0) == 0)
        def _():
            loss_ref[...] = jnp.zeros_like(loss_ref)
            gw_ref[...] = jnp.zeros_like(gw_ref)

        y, vjp = jax.vjp(_rmsnorm, x_ref[...], w_ref[...])
        err = y - t_ref[...]
        loss_ref[...] += 0.5 * jnp.sum(jnp.mean(err * err, axis=-1, keepdims=True), axis=0, keepdims=True)
        dx, dw = vjp(err * (1.0 / D_MODEL))
        dx_ref[...] = dx
        gw_ref[...] += dw

    tok = lambda i: (i, 0)
    return pl.pallas_call(
        body, name=name, grid=(t // tm,),
        out_shape=[jax.ShapeDtypeStruct((t, D_MODEL), F32), jax.ShapeDtypeStruct((1, 1), F32),
                   jax.ShapeDtypeStruct((1, D_MODEL), F32)],
        in_specs=[pl.BlockSpec((tm, D_MODEL), tok), _full((1, D_MODEL)), pl.BlockSpec((tm, D_MODEL), tok)],
        out_specs=[pl.BlockSpec((tm, D_MODEL), tok), _full((1, 1)), _full((1, D_MODEL))],
        compiler_params=pltpu.CompilerParams(dimension_semantics=("arbitrary",)),
    )(x, fnw, target)


def _adamw(w, g, m, v):
    m = ADAM_B1 * m + (1.0 - ADAM_B1) * g
    v = ADAM_B2 * v + (1.0 - ADAM_B2) * jnp.square(g)
    m_hat = m / (1.0 - ADAM_B1 ** ADAM_STEP)
    v_hat = v / (1.0 - ADAM_B2 ** ADAM_STEP)
    delta = -ADAM_LR * (m_hat / (jnp.sqrt(v_hat) + ADAM_EPS) + ADAM_WD * w)
    return delta, m, v


def _reduce_adamw(parts, w, m, v, tr, name):
    depth = len(parts)
    p, r, cdim = parts[0].shape
    n_blk = r // tr

    def body(*refs):
        p_refs = refs[:depth]
        w_ref, m_ref, v_ref, g_ref, d_ref, nm_ref, nv_ref = refs[depth:]
        for layer in range(depth):
            @pl.when(pl.program_id(0) == layer)
            def _(p_ref=p_refs[layer]):
                g = p_ref[0].astype(F32)
                for i in range(1, p):
                    g = g + p_ref[i].astype(F32)
                g_ref[0] = g
                d_ref[0], nm_ref[0], nv_ref[0] = _adamw(w_ref[0], g, m_ref[0], v_ref[0])

    def parts_spec(layer):
        return pl.BlockSpec((p, tr, cdim), lambda d, i: (0, jnp.clip(i + (d - layer) * n_blk, 0, n_blk - 1), 0))

    blk = pl.BlockSpec((1, tr, cdim), lambda d, i: (d, i, 0))
    return pl.pallas_call(
        body, name=name, grid=(depth, n_blk),
        out_shape=[jax.ShapeDtypeStruct(w.shape, F32)] * 4,
        in_specs=[parts_spec(layer) for layer in range(depth)] + [blk, blk, blk],
        out_specs=[blk] * 4,
        compiler_params=pltpu.CompilerParams(dimension_semantics=("arbitrary", "arbitrary"),
                                             vmem_limit_bytes=VMEM_LIMIT),
    )(*parts, w, m, v)


def _adamw_small(ssum, entries, name):
    direct = [e[3] for e in entries if not isinstance(e[3], list)]
    n_direct = len(direct)

    def body(*refs):
        ssum_ref, direct_refs = refs[0], list(refs[1:1 + n_direct])
        ins = refs[1 + n_direct:1 + n_direct + 3 * len(entries)]
        outs = refs[1 + n_direct + 3 * len(entries):]
        for k, (w, _, _, grad) in enumerate(entries):
            w_ref, m_ref, v_ref = ins[3 * k:3 * k + 3]
            g_ref, d_ref, nm_ref, nv_ref = outs[4 * k:4 * k + 4]
            if isinstance(grad, list):
                for row, off in enumerate(grad):
                    rows = slice(row, row + 1)
                    g = ssum_ref[:, off:off + w.shape[1]]
                    g_ref[rows, :] = g
                    d_ref[rows, :], nm_ref[rows, :], nv_ref[rows, :] = _adamw(w_ref[rows, :], g, m_ref[rows, :],
                                                                              v_ref[rows, :])
            else:
                g = direct_refs.pop(0)[...]
                g_ref[...] = g
                d_ref[...], nm_ref[...], nv_ref[...] = _adamw(w_ref[...], g, m_ref[...], v_ref[...])

    vmem = pl.BlockSpec(memory_space=pltpu.VMEM)
    args = [ssum] + direct + [a for e in entries for a in e[:3]]
    res = pl.pallas_call(
        body, name=name,
        out_shape=[jax.ShapeDtypeStruct(e[0].shape, F32) for e in entries for _ in range(4)],
        in_specs=[vmem] * len(args), out_specs=[vmem] * (4 * len(entries)),
    )(*args)
    return [res[4 * k:4 * k + 4] for k in range(len(entries))]


def _reduce_adamw_cols(parts, w, m, v, name):
    depth = len(parts)
    _, r, cdim = parts[0].shape
    tc = 256

    def body(*refs):
        p_refs = refs[:depth]
        w_ref, m_ref, v_ref, g_ref, d_ref, nm_ref, nv_ref = refs[depth:]
        for layer in range(depth):
            g = p_refs[layer][0].astype(F32)
            for i in range(1, parts[layer].shape[0]):
                g = g + p_refs[layer][i].astype(F32)
            g = g.T
            g_ref[:, layer, :] = g
            d_ref[:, layer, :], nm_ref[:, layer, :], nv_ref[:, layer, :] = _adamw(
                w_ref[:, layer, :], g, m_ref[:, layer, :], v_ref[:, layer, :])

    view = lambda a: jnp.transpose(a, (2, 0, 1))
    blk = pl.BlockSpec((cdim, depth, tc), lambda i: (0, 0, i))
    outs = pl.pallas_call(
        body, name=name, grid=(r // tc,),
        out_shape=[jax.ShapeDtypeStruct((cdim, depth, r), F32)] * 4,
        in_specs=[pl.BlockSpec((a.shape[0], tc, cdim), lambda i: (0, i, 0)) for a in parts] + [blk, blk, blk],
        out_specs=[blk] * 4,
        compiler_params=pltpu.CompilerParams(dimension_semantics=("arbitrary",), vmem_limit_bytes=VMEM_LIMIT),
    )(*parts, view(w), view(m), view(v))
    return [jnp.transpose(o, (1, 2, 0)) for o in outs]


def _pad_lanes(v, width=128):
    return jnp.pad(v.reshape(1, -1), ((0, 0), (0, width - v.shape[-1])))


SMALL_FIELDS = (("norm_w", 1024), ("conv_b", 1536), ("dt_bias", 128), ("a_log", 128), ("d_skip", 128),
                ("ssd_norm_w", 1024), ("sinks", 128), ("dw_b", 512), ("ln_w", 512), ("ln_b", 512))


def kernel(x, norm_w, w_in, ssd_conv_w, ssd_conv_b, ssd_dt_bias, ssd_a_log, ssd_d, ssd_norm_w, attn_sinks, conf_dw_w, conf_dw_b, conf_ln_w, conf_ln_b, w_out, final_norm_w, loss_target, m_norm_w, m_w_in, m_ssd_conv_w, m_ssd_conv_b, m_ssd_dt_bias, m_ssd_a_log, m_ssd_d, m_ssd_norm_w, m_attn_sinks, m_conf_dw_w, m_conf_dw_b, m_conf_ln_w, m_conf_ln_b, m_w_out, m_final_norm_w, v_norm_w, v_w_in, v_ssd_conv_w, v_ssd_conv_b, v_ssd_dt_bias, v_ssd_a_log, v_ssd_d, v_ssd_norm_w, v_attn_sinks, v_conf_dw_w, v_conf_dw_b, v_conf_ln_w, v_conf_ln_b, v_w_out, v_final_norm_w):
    nb, seq, _ = x.shape
    depth = norm_w.shape[0]
    t = nb * seq
    me_idx = 4 * lax.axis_index("x") + 2 * lax.axis_index("y") + lax.axis_index("c")

    w_in_bf, w_out_bf = w_in.astype(BF16), w_out.astype(BF16)
    g_win0, g_cw, g_dw = _all_gather([w_in_bf[0], ssd_conv_w, conf_dw_w], "gather_weights")
    w_in_full = [_repack_w_in(g_win0, "repack_w_in_0")]
    w_out_full = []
    conv_w_full = [jnp.transpose(g_cw[:, l], (1, 0, 2)).reshape(4, 1536) for l in range(depth)]
    dw_w_full = [jnp.transpose(g_dw[:, l], (1, 0, 2)).reshape(CONF_KERNEL, 512) for l in range(depth)]

    def layer_params(l):
        return [conv_w_full[l], ssd_conv_b[l].reshape(1, -1), _pad_lanes(ssd_dt_bias[l]), _pad_lanes(ssd_a_log[l]),
                _pad_lanes(ssd_d[l]), ssd_norm_w[l].reshape(1, -1), _pad_lanes(attn_sinks[l]), dw_w_full[l],
                conf_dw_b[l].reshape(1, -1), conf_ln_w[l].reshape(1, -1), conf_ln_b[l].reshape(1, -1)]

    xs = [x.reshape(t, D_MODEL)]
    saved = []
    for l in range(depth):
        (proj, h), gathered = _inproj_fwd(xs[l], norm_w[l].reshape(1, -1), w_in_full[l], f"inproj_fwd_{l}",
                                          (GATHER, [w_out_bf[l]]))
        w_out_full.append(gathered[0].reshape(MIX_WIDTH, D_MODEL))
        side = (GATHER, [w_in_bf[l + 1]]) if l + 1 < depth else None
        (x_next, *kept), gathered = _mixer_fwd(xs[l], proj, w_out_full[l], layer_params(l), nb, f"mixer_fwd_{l}",
                                               side)
        if side:
            w_in_full.append(_repack_w_in(gathered[0], f"repack_w_in_{l + 1}"))
        saved.append((proj, h, kept))
        xs.append(x_next)
    dx, loss_part, g_fnw = _loss_head(xs[depth], final_norm_w.reshape(1, -1), loss_target.reshape(t, D_MODEL),
                                      "loss_head")

    cols_in = D_IN_PROJ // N_DEV
    rows_out = MIX_WIDTH // N_DEV
    small_rows = [None] * depth
    received = [None] * depth
    for l in reversed(range(depth)):
        proj, h, kept = saved[l]
        res, _ = _mixer_bwd(dx, proj, *kept, w_out_full[l], layer_params(l), nb, f"mixer_bwd_{l}")
        dproj, y_cat = res[0], res[1]
        g_cw_l, g_cb, g_dtb, g_alog, g_dsk, g_nw, g_snk, g_dww, g_dwb, g_lnw, g_lnb = res[2:]
        gw_out_parts = _gw_out(y_cat, dx, f"gw_out_{l}")
        (gw_in,), got_out = _inproj_bwd_w(h, dproj, f"inproj_bwd_w_{l}", (SCATTER, [gw_out_parts]))
        chip_parts = _pair_sum(_unpack_gw_in(gw_in, f"unpack_gw_in_{l}"), f"pair_sum_{l}")
        (dx, g_norm), got_in = _inproj_bwd_x(dproj, w_in_full[l], xs[l], norm_w[l].reshape(1, -1), dx,
                                             f"inproj_bwd_x_{l}", (CHIP_SCATTER, [chip_parts]))
        received[l] = [got_in[0], got_out[0]]
        small_rows[l] = [g_norm, g_cb, g_dtb, g_alog, g_dsk, g_nw, g_snk, g_dwb, g_lnw, g_lnb,
                         g_cw_l.reshape(1, -1), g_dww.reshape(1, -1)]
    grad_x = dx.reshape(nb, seq, D_MODEL)

    small = jnp.concatenate([piece for l in range(depth) for piece in small_rows[l]] + [g_fnw], axis=1)
    ssum = _exchange_small(small, "exchange_small")

    g_w_in, d_w_in, nm_w_in, nv_w_in = _reduce_adamw_cols([received[l][0] for l in range(depth)], w_in, m_w_in,
                                                          v_w_in, "adamw_w_in")
    g_w_out, d_w_out, nm_w_out, nv_w_out = _reduce_adamw([received[l][1] for l in range(depth)], w_out, m_w_out,
                                                         v_w_out, 256, "adamw_w_out")

    per_layer = sum(n for _, n in SMALL_FIELDS) + 4 * 1536 + CONF_KERNEL * 512
    given = {"norm_w": (norm_w, m_norm_w, v_norm_w), "conv_b": (ssd_conv_b, m_ssd_conv_b, v_ssd_conv_b),
             "dt_bias": (ssd_dt_bias, m_ssd_dt_bias, v_ssd_dt_bias), "a_log": (ssd_a_log, m_ssd_a_log, v_ssd_a_log),
             "d_skip": (ssd_d, m_ssd_d, v_ssd_d), "ssd_norm_w": (ssd_norm_w, m_ssd_norm_w, v_ssd_norm_w),
             "sinks": (attn_sinks, m_attn_sinks, v_attn_sinks), "dw_b": (conf_dw_b, m_conf_dw_b, v_conf_dw_b),
             "ln_w": (conf_ln_w, m_conf_ln_w, v_conf_ln_w), "ln_b": (conf_ln_b, m_conf_ln_b, v_conf_ln_b)}
    entries = []
    off = 0
    for fname, n in SMALL_FIELDS:
        entries.append((*given[fname], [l * per_layer + off for l in range(depth)]))
        off += n
    shard_grads = []
    for width, shard, kk in ((1536, 192, 4), (512, 64, CONF_KERNEL)):
        full = [ssum[:, l * per_layer + off:l * per_layer + off + kk * width].reshape(kk, width) for l in range(depth)]
        shard_grads.append(jnp.stack([lax.dynamic_slice(f, (0, me_idx * shard), (kk, shard)) for f in full], axis=0))
        off += kk * width
    entries.append((ssd_conv_w, m_ssd_conv_w, v_ssd_conv_w, shard_grads[0]))
    entries.append((conf_dw_w, m_conf_dw_w, v_conf_dw_w, shard_grads[1]))
    entries.append((final_norm_w.reshape(1, -1), m_final_norm_w.reshape(1, -1), v_final_norm_w.reshape(1, -1),
                    [depth * per_layer]))
    sm = _adamw_small(ssum, entries, "adamw_small")
    sm = {k: quad for k, quad in zip([f for f, _ in SMALL_FIELDS] + ["conv_w", "dw_w", "final"], sm)}

    def outputs(i, big_in_i, big_out_i):
        return [sm["norm_w"][i], big_in_i, sm["conv_w"][i], sm["conv_b"][i], sm["dt_bias"][i], sm["a_log"][i],
                sm["d_skip"][i], sm["ssd_norm_w"][i], sm["sinks"][i], sm["dw_w"][i], sm["dw_b"][i], sm["ln_w"][i],
                sm["ln_b"][i], big_out_i, sm["final"][i].reshape(-1)]

    loss = lax.psum(loss_part[0, 0], MESH_AXES)
    return (loss, grad_x, *outputs(0, g_w_in, g_w_out), *outputs(1, d_w_in, d_w_out),
            *outputs(2, nm_w_in, nm_w_out), *outputs(3, nv_w_in, nv_w_out))
```

```python
import functools

import jax
import jax.numpy as jnp
from jax import lax
from jax.experimental import pallas as pl
from jax.experimental.pallas import tpu as pltpu

F32 = jnp.float32
BF16 = jnp.bfloat16
N_DEV = 8
EPS = 1e-5

D_MODEL = 1024
CHUNK = 128
SSD_HEADS = 16
SSD_HEAD_DIM = 64
SSD_STATE = 128
ATTN_HEADS = 8
ATTN_HEAD_DIM = 64
CONF_KERNEL = 31
MIX_WIDTH = 2048
D_IN_PROJ = 5392
C_Z = 0
C_CA = 2048
C_XBC = 3072
C_Q = 4608
C_K = 5120
C_V = 5248
C_DT = 5376
PROJ_W = 5632
N_COL_TILES = 4
COL_TILE = PROJ_W // N_COL_TILES
XBC_HALO = 8
CONF_HALO = 32
ATTN_STACK_FWD = 2
ATTN_STACK_BWD = 4
VMEM_LIMIT = 56 * 1024 * 1024

ADAM_LR = 0.001
ADAM_B1 = 0.9
ADAM_B2 = 0.999
ADAM_EPS = 1e-08
ADAM_WD = 0.01
ADAM_STEP = 10


def _silu(v):
    return v * jax.nn.sigmoid(v)


def _softplus(v):
    return jnp.maximum(v, 0.0) + jnp.log1p(jnp.exp(-jnp.abs(v)))


def _rmsnorm(v, w):
    return v * lax.rsqrt(jnp.mean(v * v, axis=-1, keepdims=True) + EPS) * w


def _dot(a, b):
    return jnp.dot(a.astype(BF16), b.astype(BF16), preferred_element_type=F32)


def _dot_nt(a, b):
    return lax.dot_general(a.astype(BF16), b.astype(BF16), (((1,), (1,)), ((), ())), preferred_element_type=F32)


def _dot_tn(a, b):
    return lax.dot_general(a.astype(BF16), b.astype(BF16), (((0,), (0,)), ((), ())), preferred_element_type=F32)


def _taps(ext, offs, out_len, w=None, g=None):
    n_rows, n_cols = ext.shape
    by_shift = {}
    for t, off in enumerate(offs):
        by_shift.setdefault(off % 8, []).append((t, off))
    for r, taps in by_shift.items():
        assert max(off for _, off in taps) - r + out_len <= n_rows - r
    accs = []
    sums = [[None] * (n_cols // 128) for _ in offs]
    for blk in range(n_cols // 128):
        cs = slice(blk * 128, (blk + 1) * 128)
        e = ext[:, cs]
        acc = None
        for r, taps in by_shift.items():
            shifted = e if r == 0 else pltpu.roll(e, n_rows - r, axis=0)
            for t, off in taps:
                window = shifted[off - r:off - r + out_len, :]
                if w is not None:
                    term = w[t:t + 1, cs] * window
                    acc = term if acc is None else acc + term
                if g is not None:
                    sums[t][blk] = jnp.sum(g[:, cs] * window, axis=0, keepdims=True)
        accs.append(acc)
    if w is not None:
        return jnp.concatenate(accs, axis=1)
    return jnp.concatenate([jnp.concatenate(row, axis=1) for row in sums], axis=0)


@functools.partial(jax.custom_vjp, nondiff_argnums=(3,))
def _dwconv(ext, w, b, halo):
    kk = w.shape[0]
    return b + _taps(ext, [halo - (kk - 1) + t for t in range(kk)], ext.shape[0] - halo, w=w)


def _dwconv_fwd(ext, w, b, halo):
    return _dwconv(ext, w, b, halo), (ext, w)


def _dwconv_bwd(halo, res, g):
    ext, w = res
    kk = w.shape[0]
    offs = [halo - (kk - 1) + t for t in range(kk)]
    dw = _taps(ext, offs, ext.shape[0] - halo, g=g)
    zeros = jnp.zeros((halo, g.shape[1]), g.dtype)
    gp = jnp.concatenate([zeros, g, zeros], axis=0)
    dext = _taps(gp, [halo - off for off in offs], ext.shape[0], w=w)
    return dext, dw, jnp.sum(g, axis=0, keepdims=True)


_dwconv.defvjp(_dwconv_fwd, _dwconv_bwd)


def _ssd_part(z_ssd, conv_out, dtr, s_in, dtb, alog, dsk, nw):
    qn = conv_out.shape[0]
    nh = SSD_HEADS
    per_group = nh // 2
    n_pair = nh // 2
    xa = _silu(conv_out)
    xs = xa[:, 0:1024]
    dt = _softplus(dtr + dtb)
    a = dt * (-jnp.exp(alog))
    rows = lax.broadcasted_iota(jnp.int32, (qn, qn), 0)
    cols = lax.broadcasted_iota(jnp.int32, (qn, qn), 1)
    causal = rows >= cols
    low = cols < SSD_HEAD_DIM
    a_cs = jnp.dot(causal.astype(F32), a, precision=lax.Precision.HIGHEST, preferred_element_type=F32)
    a_cs_t = a_cs.T
    bgs = [xa[:, 1024 + g * 128:1024 + (g + 1) * 128] for g in range(2)]
    cgs = [xa[:, 1280 + g * 128:1280 + (g + 1) * 128] for g in range(2)]
    cbms = [_dot_nt(cgs[g], bgs[g]) for g in range(2)]
    colb = [jnp.broadcast_to(a_cs[:, h:h + 1], (qn, qn)) for h in range(nh)]
    lastb = [jnp.broadcast_to(colb[h][qn - 1:qn, :], (qn, qn)) for h in range(nh)]
    dtb_wide = [jnp.broadcast_to(dt[:, h:h + 1], (qn, qn)) for h in range(nh)]
    lmats = [jnp.exp(jnp.where(causal, colb[h] - a_cs_t[h:h + 1, :], -jnp.inf)) for h in range(nh)]
    ms = [cbms[h // per_group] * lmats[h] for h in range(nh)]
    x_pair = [xs[:, p * 128:(p + 1) * 128] for p in range(n_pair)]
    x_lo = [x_pair[p] * jnp.where(low, dtb_wide[2 * p], 0.0) for p in range(n_pair)]
    x_hi = [x_pair[p] * jnp.where(low, 0.0, dtb_wide[2 * p + 1]) for p in range(n_pair)]
    y_diag = [_dot(ms[2 * p], x_lo[p]) + _dot(ms[2 * p + 1], x_hi[p]) for p in range(n_pair)]
    col_pair = [jnp.where(low, colb[2 * p], colb[2 * p + 1]) for p in range(n_pair)]
    last_pair = [jnp.where(low, lastb[2 * p], lastb[2 * p + 1]) for p in range(n_pair)]
    ecol = [jnp.exp(col_pair[p]) for p in range(n_pair)]
    xw = [(x_lo[p] + x_hi[p]) * jnp.exp(last_pair[p] - col_pair[p]) for p in range(n_pair)]
    y_off, st = [], []
    for g in range(2):
        ps = range(g * n_pair // 2, (g + 1) * n_pair // 2)
        y_off.append(_dot_nt(cgs[g], s_in[g * 512:(g + 1) * 512, :]) * jnp.concatenate([ecol[p] for p in ps], axis=1))
        st.append(_dot_tn(jnp.concatenate([xw[p] for p in ps], axis=1), bgs[g]))
    e_last = jnp.exp(jnp.broadcast_to(a_cs_t[:, qn - 1:qn], (qn, SSD_STATE)))
    scale = jnp.concatenate([jnp.broadcast_to(e_last[h:h + 1, :], (64, SSD_STATE)) for h in range(nh)], axis=0)
    s_out = scale * s_in + jnp.concatenate(st, axis=0)
    d_wide = jnp.concatenate([jnp.broadcast_to(dsk[:, h:h + 1], (1, 64)) for h in range(nh)], axis=1)
    y = jnp.concatenate(y_diag, axis=1) + jnp.concatenate(y_off, axis=1) + d_wide * xs
    gated = y * _silu(z_ssd)
    halves = []
    for g in range(2):
        gg = gated[:, g * 512:(g + 1) * 512]
        halves.append(gg * lax.rsqrt(jnp.mean(gg * gg, axis=-1, keepdims=True) + EPS))
    return jnp.concatenate(halves, axis=1) * nw, s_out


def _attn_part(z_attn, q, kv, p_kv, snk, kvmask, stack):
    qn = q.shape[0]
    kk = jnp.concatenate([p_kv[:, 0:128], kv[:, 0:128]], axis=0)
    vv = jnp.concatenate([p_kv[:, 128:256], kv[:, 128:256]], axis=0)
    units = range(ATTN_HEADS // stack)
    heads = [range(u * stack, (u + 1) * stack) for u in units]
    kv_of = [u * stack // (ATTN_HEADS // 2) for u in units]
    k_of = [kk[:, g * 64:(g + 1) * 64] for g in kv_of]
    v_of = [vv[:, g * 64:(g + 1) * 64] for g in kv_of]
    qs = [jnp.concatenate([q[:, h * 64:(h + 1) * 64] for h in heads[u]], axis=0) for u in units]
    sk = [jnp.concatenate([jnp.broadcast_to(snk[:, h:h + 1], (qn, 1)) for h in heads[u]], axis=0) for u in units]
    s = [jnp.where(kvmask, _dot_nt(qs[u], k_of[u]) * (ATTN_HEAD_DIM ** -0.5), -jnp.inf) for u in units]
    m = [lax.stop_gradient(jnp.maximum(jnp.max(s[u], axis=-1, keepdims=True), sk[u])) for u in units]
    e = [jnp.exp(s[u] - m[u]) for u in units]
    r_den = [1.0 / (jnp.sum(e[u], axis=-1, keepdims=True) + jnp.exp(sk[u] - m[u])) for u in units]
    o = [_dot(e[u], v_of[u]) * r_den[u] for u in units]
    outs = [o[u][i * qn:(i + 1) * qn, :] for u in units for i in range(stack)]
    return jnp.concatenate(outs, axis=1) * _silu(z_attn)


def _conf_glu(cacg, p_cc):
    c0 = cacg[:, 0:512] * jax.nn.sigmoid(cacg[:, 512:1024])
    pc0 = p_cc[:, 0:512] * jax.nn.sigmoid(p_cc[:, 512:1024])
    return jnp.concatenate([pc0, c0], axis=0)


def _conf_tail(conv_out, z_conf, lnw, lnb):
    xc = conv_out - jnp.mean(conv_out, axis=-1, keepdims=True)
    yln = xc * lax.rsqrt(jnp.mean(xc * xc, axis=-1, keepdims=True) + EPS) * lnw + lnb
    return _silu(yln) * _silu(z_conf)


def _kv_mask(qn, not_first, reps):
    ii = lax.broadcasted_iota(jnp.int32, (reps * qn, 2 * qn), 0) & (qn - 1)
    jj = lax.broadcasted_iota(jnp.int32, (reps * qn, 2 * qn), 1)
    d = jj - ii
    return (d >= 1) & (d <= qn) & (not_first | (jj >= qn))


def _my_place():
    return lax.axis_index("x"), lax.axis_index("y"), lax.axis_index("c")


def _all_gather(arrs, name):
    n = len(arrs)

    def body(*refs):
        ins, outs = refs[:n], refs[n:2 * n]
        send_sems, recv_sems, local_sems = refs[2 * n:]
        x, y, c = _my_place()
        me, sibling = (x, y, c), (x, y, 1 - c)
        chips = [(1 - x, y), (x, 1 - y), (1 - x, 1 - y)]

        def slot(a, p):
            return outs[a].at[4 * p[0] + 2 * p[1] + p[2]]

        def copy(a, kk, block, to, src=None):
            return pltpu.make_async_remote_copy(
                src_ref=slot(a, block) if src is None else src, dst_ref=slot(a, block),
                send_sem=send_sems.at[a, kk], recv_sem=recv_sems.at[a, kk],
                device_id=to, device_id_type=pl.DeviceIdType.MESH)

        mine = [pltpu.make_async_copy(ins[a], slot(a, me), local_sems.at[a]) for a in range(n)]
        for cp in mine:
            cp.start()
        first = []
        for a in range(n):
            first.append(copy(a, 0, me, sibling, src=ins[a]))
            first += [copy(a, 1 + j, me, (*chip, c), src=ins[a]) for j, chip in enumerate(chips)]
        for cp in first:
            cp.start()
        passed = []
        for j, chip in enumerate(chips):
            for a in range(n):
                copy(a, 1 + j, (*chip, c), me).wait_recv()
                fwd = copy(a, 4 + j, (*chip, c), sibling)
                fwd.start()
                passed.append(fwd)
        for a in range(n):
            copy(a, 0, sibling, me).wait_recv()
            for j, chip in enumerate(chips):
                copy(a, 4 + j, (*chip, 1 - c), me).wait_recv()
        for cp in first + passed:
            cp.wait_send()
        for cp in mine:
            cp.wait()

    any_spec = pl.BlockSpec(memory_space=pl.ANY)
    return pl.pallas_call(
        body, name=name,
        out_shape=[jax.ShapeDtypeStruct((N_DEV,) + a.shape, a.dtype) for a in arrs],
        in_specs=[any_spec] * n, out_specs=[any_spec] * n,
        scratch_shapes=[pltpu.SemaphoreType.DMA((n, 7)), pltpu.SemaphoreType.DMA((n, 7)),
                        pltpu.SemaphoreType.DMA((n,))],
    )(*arrs)


GATHER, SCATTER, CHIP_SCATTER = "gather", "scatter", "chip_scatter"


def _direct_copies(mode, ins, outs, send_sems, recv_sems, local_sems):
    x, y, c = _my_place()
    by_chip = mode == CHIP_SCATTER
    place = (lambda px, py, pc: 2 * px + py) if by_chip else (lambda px, py, pc: 4 * px + 2 * py + pc)
    me_idx = place(x, y, c)
    n = len(ins)
    local = [pltpu.make_async_copy(ins[a] if mode == GATHER else ins[a].at[me_idx], outs[a].at[me_idx],
                                   local_sems.at[a]) for a in range(n)]
    remote = []
    for rel in range(1, N_DEV):
        if by_chip and rel & 1:
            continue
        px = 1 - x if rel & 4 else x
        py = 1 - y if rel & 2 else y
        pc = 1 - c if rel & 1 else c
        for a in range(n):
            remote.append(pltpu.make_async_remote_copy(
                src_ref=ins[a] if mode == GATHER else ins[a].at[place(px, py, pc)], dst_ref=outs[a].at[me_idx],
                send_sem=send_sems.at[a, rel - 1], recv_sem=recv_sems.at[a, rel - 1],
                device_id=(px, py, pc), device_id_type=pl.DeviceIdType.MESH))
    return local + remote


def _side_scratch(n):
    return [pltpu.SemaphoreType.DMA((n, 7)), pltpu.SemaphoreType.DMA((n, 7)), pltpu.SemaphoreType.DMA((n,))]


def _side_out_shapes(mode, arrs):
    return [jax.ShapeDtypeStruct((N_DEV,) + a.shape if mode == GATHER else a.shape, a.dtype) for a in arrs]


def _pallas_with_side(body, side, first, last, n_in, n_out, *, in_specs, out_specs, out_shape, scratch_shapes=(),
                      args, **kwargs):
    side_arrs = [] if side is None else list(side[1])
    ns = len(side_arrs)

    def wrapped(*refs):
        own_in, side_in = refs[:n_in], refs[n_in:n_in + ns]
        o = n_in + ns
        own_out, side_out = refs[o:o + n_out], refs[o + n_out:o + n_out + ns]
        scratch = refs[o + n_out + ns:]
        own_scratch, sems = (scratch[:-3], scratch[-3:]) if ns else (scratch, ())
        body(*own_in, *own_out, *own_scratch)
        if ns:
            @pl.when(first())
            def _():
                for cp in _direct_copies(side[0], side_in, side_out, *sems):
                    cp.start()

            @pl.when(last())
            def _():
                for cp in _direct_copies(side[0], side_in, side_out, *sems):
                    cp.wait()

    any_spec = pl.BlockSpec(memory_space=pl.ANY)
    res = pl.pallas_call(
        wrapped,
        in_specs=list(in_specs) + [any_spec] * ns,
        out_specs=list(out_specs) + [any_spec] * ns,
        out_shape=list(out_shape) + (_side_out_shapes(side[0], side_arrs) if ns else []),
        scratch_shapes=list(scratch_shapes) + (_side_scratch(ns) if ns else []),
        **kwargs,
    )(*args, *side_arrs)
    return res[:n_out], res[n_out:]


def _exchange_small(small, name):
    ns = small.shape[1]

    def body(small_ref, sum_ref, small_all, send_sems, recv_sems, local_sems):
        copies = _direct_copies(GATHER, [small_ref], [small_all], send_sems, recv_sems, local_sems)
        for cp in copies:
            cp.start()
        for cp in copies:
            cp.wait()
        total = small_all[0]
        for i in range(1, N_DEV):
            total = total + small_all[i]
        sum_ref[...] = total

    vmem_spec = pl.BlockSpec(memory_space=pltpu.VMEM)
    return pl.pallas_call(
        body, name=name,
        out_shape=jax.ShapeDtypeStruct((1, ns), F32),
        in_specs=[vmem_spec], out_specs=vmem_spec,
        scratch_shapes=[pltpu.VMEM((N_DEV, 1, ns), F32)] + _side_scratch(1),
    )(small)


def _full(shape):
    return pl.BlockSpec(shape, lambda *_: (0,) * len(shape))


def _inproj_fwd(x, nw, w, name, side=None):
    t = x.shape[0]
    tm = 256

    def body(x_ref, nw_ref, w_ref, proj_ref, h_ref):
        h = _rmsnorm(x_ref[...], nw_ref[...]).astype(BF16)
        h_ref[...] = h
        for j in range(N_COL_TILES):
            sl = slice(j * COL_TILE, (j + 1) * COL_TILE)
            proj_ref[:, sl] = jnp.dot(h, w_ref[:, sl], preferred_element_type=F32)

    grid = (t // tm,)
    return _pallas_with_side(
        body, side, *_grid_ends(grid), 3, 2, name=name, grid=grid,
        out_shape=[jax.ShapeDtypeStruct((t, PROJ_W), F32), jax.ShapeDtypeStruct((t, D_MODEL), BF16)],
        in_specs=[pl.BlockSpec((tm, D_MODEL), lambda i: (i, 0)), _full((1, D_MODEL)), _full((D_MODEL, PROJ_W))],
        out_specs=[pl.BlockSpec((tm, PROJ_W), lambda i: (i, 0)), pl.BlockSpec((tm, D_MODEL), lambda i: (i, 0))],
        compiler_params=pltpu.CompilerParams(dimension_semantics=("arbitrary",), vmem_limit_bytes=VMEM_LIMIT),
        args=(x, nw, w))


def _param_specs():
    return [_full((4, 1536)), _full((1, 1536)), _full((1, 128)), _full((1, 128)), _full((1, 128)),
            _full((1, 1024)), _full((1, 128)), _full((CONF_KERNEL, 512)), _full((1, 512)), _full((1, 512)),
            _full((1, 512))]


def _halo_specs(nc, chunk_of):
    def prev_chunk(b, j):
        return jnp.maximum(b * nc + chunk_of(j) - 1, 0)

    per_xbc = CHUNK // XBC_HALO
    per_cc = CHUNK // CONF_HALO
    return [
        pl.BlockSpec((XBC_HALO, 1536), lambda b, j: (prev_chunk(b, j) * per_xbc + per_xbc - 1, C_XBC // 1536)),
        pl.BlockSpec((CHUNK, 256), lambda b, j: (prev_chunk(b, j), C_K // 256)),
        pl.BlockSpec((CONF_HALO, 1024), lambda b, j: (prev_chunk(b, j) * per_cc + per_cc - 1, C_CA // 1024)),
    ]


def _grid_ends(grid):
    first = lambda: functools.reduce(lambda p, q: p & q, [pl.program_id(i) == 0 for i in range(len(grid))])
    last = lambda: functools.reduce(lambda p, q: p & q, [pl.program_id(i) == n - 1 for i, n in enumerate(grid)])
    return first, last


def _mixer_fwd(x, proj, w_out, params, nb, name, side=None):
    t = x.shape[0]
    nc = t // nb // CHUNK

    def body(x_ref, cur_ref, pxbc_ref, pkv_ref, pcc_ref, wo_ref, *rest):
        prm = [r[...] for r in rest[:11]]
        xn_ref, sall_ref, conv_ref, s_scr = rest[11:]
        c = pl.program_id(1)
        not_first = c > 0
        nf = not_first.astype(F32)

        @pl.when(c == 0)
        def _():
            s_scr[...] = jnp.zeros_like(s_scr)

        cw, cb, dtb, alog, dsk, nw, snk, dww, dwb, lnw, lnb = prm
        s_in = s_scr[...]
        sall_ref[0] = s_in
        ssd_conv = _dwconv(jnp.concatenate([pxbc_ref[...] * nf, cur_ref[:, C_XBC:C_XBC + 1536]], axis=0), cw, cb,
                           XBC_HALO)
        y_ssd, s_out = _ssd_part(cur_ref[:, 0:1024], ssd_conv, cur_ref[:, C_DT:C_DT + 128], s_in, dtb, alog, dsk, nw)
        s_scr[...] = s_out
        y_attn = _attn_part(cur_ref[:, 1024:1536], cur_ref[:, C_Q:C_Q + 512], cur_ref[:, C_K:C_K + 256],
                            pkv_ref[...] * nf, snk, _kv_mask(CHUNK, not_first, ATTN_STACK_FWD), ATTN_STACK_FWD)
        conv_out = _dwconv(_conf_glu(cur_ref[:, C_CA:C_CA + 1024], pcc_ref[...] * nf), dww, dwb, CONF_HALO)
        conv_ref[...] = conv_out
        y_conf = _conf_tail(conv_out, cur_ref[:, 1536:2048], lnw, lnb)
        xn_ref[...] = (x_ref[...] + _dot(y_ssd, wo_ref[0:1024, :]) + _dot(y_attn, wo_ref[1024:1536, :])
                       + _dot(y_conf, wo_ref[1536:2048, :]))

    row = lambda b, j: (b * nc + j, 0)
    grid = (nb, nc)
    return _pallas_with_side(
        body, side, *_grid_ends(grid), 17, 3, name=name, grid=grid,
        out_shape=[jax.ShapeDtypeStruct((t, D_MODEL), F32),
                   jax.ShapeDtypeStruct((nb * nc, SSD_HEADS * SSD_HEAD_DIM, SSD_STATE), F32),
                   jax.ShapeDtypeStruct((t, 512), F32)],
        in_specs=[pl.BlockSpec((CHUNK, D_MODEL), row), pl.BlockSpec((CHUNK, PROJ_W), row)]
                 + _halo_specs(nc, lambda j: j) + [_full((MIX_WIDTH, D_MODEL))] + _param_specs(),
        out_specs=[pl.BlockSpec((CHUNK, D_MODEL), row),
                   pl.BlockSpec((1, SSD_HEADS * SSD_HEAD_DIM, SSD_STATE), lambda b, j: (b * nc + j, 0, 0)),
                   pl.BlockSpec((CHUNK, 512), row)],
        scratch_shapes=[pltpu.VMEM((SSD_HEADS * SSD_HEAD_DIM, SSD_STATE), F32)],
        compiler_params=pltpu.CompilerParams(dimension_semantics=("arbitrary", "arbitrary"),
                                             vmem_limit_bytes=VMEM_LIMIT),
        args=(x, proj, proj, proj, proj, w_out, *params))


def _mixer_bwd(dxn, proj, s_all, conv_all, w_out, params, nb, name):
    t = dxn.shape[0]
    nc = t // nb // CHUNK
    n_prm = 11

    def body(dxn_ref, cur_ref, pxbc_ref, pkv_ref, pcc_ref, s_ref, conv_ref, wo_ref, *rest):
        prm = [r[...] for r in rest[:n_prm]]
        dproj_ref, ycat_ref = rest[n_prm:n_prm + 2]
        gprm = rest[n_prm + 2:2 * n_prm + 2]
        ds_scr, pend_xbc, pend_kv, pend_cc = rest[2 * n_prm + 2:]
        b, j = pl.program_id(0), pl.program_id(1)
        c = nc - 1 - j
        not_first = c > 0
        nf = not_first.astype(F32)

        @pl.when((b == 0) & (j == 0))
        def _():
            for r in gprm:
                r[...] = jnp.zeros_like(r)

        @pl.when(j == 0)
        def _():
            ds_scr[...] = jnp.zeros_like(ds_scr)
            pend_xbc[...] = jnp.zeros_like(pend_xbc)
            pend_kv[...] = jnp.zeros_like(pend_kv)
            pend_cc[...] = jnp.zeros_like(pend_cc)

        cw, cb, dtb, alog, dsk, nw, snk, dww, dwb, lnw, lnb = prm
        g_cw, g_cb, g_dtb, g_alog, g_dsk, g_nw, g_snk, g_dww, g_dwb, g_lnw, g_lnb = gprm
        dxn_v = dxn_ref[...]

        def add_tail(d_cur, pending):
            lead = jnp.zeros((CHUNK - pending.shape[0], pending.shape[1]), F32)
            return d_cur + jnp.concatenate([lead, pending], axis=0)

        y, vjp = jax.vjp(_conf_tail, conv_ref[...], cur_ref[:, 1536:2048], lnw, lnb)
        ycat_ref[:, 1536:2048] = y.astype(BF16)
        d_conv, dz, d_lnw, d_lnb = vjp(_dot_nt(dxn_v, wo_ref[1536:2048, :]))
        ext, vjp = jax.vjp(_conf_glu, cur_ref[:, C_CA:C_CA + 1024], pcc_ref[...] * nf)
        d_ext, d_dww, d_dwb = _dwconv_bwd(CONF_HALO, (ext, dww), d_conv)
        dcacg, dpcc = vjp(d_ext)
        dproj_ref[:, 1536:2048] = dz.astype(BF16)
        dproj_ref[:, C_CA:C_CA + 1024] = add_tail(dcacg, pend_cc[...]).astype(BF16)
        pend_cc[...] = dpcc
        for r, g in ((g_dww, d_dww), (g_dwb, d_dwb), (g_lnw, d_lnw), (g_lnb, d_lnb)):
            r[...] += g

        attn = functools.partial(_attn_part, kvmask=_kv_mask(CHUNK, not_first, ATTN_STACK_BWD),
                                 stack=ATTN_STACK_BWD)
        y, vjp = jax.vjp(attn, cur_ref[:, 1024:1536], cur_ref[:, C_Q:C_Q + 512], cur_ref[:, C_K:C_K + 256],
                         pkv_ref[...] * nf, snk)
        ycat_ref[:, 1024:1536] = y.astype(BF16)
        dz, dq, dkv, dpkv, d_snk = vjp(_dot_nt(dxn_v, wo_ref[1024:1536, :]))
        dproj_ref[:, 1024:1536] = dz.astype(BF16)
        dproj_ref[:, C_Q:C_Q + 512] = dq.astype(BF16)
        dproj_ref[:, C_K:C_K + 256] = (dkv + pend_kv[...]).astype(BF16)
        pend_kv[...] = dpkv
        g_snk[...] += d_snk

        ext = jnp.concatenate([pxbc_ref[...] * nf, cur_ref[:, C_XBC:C_XBC + 1536]], axis=0)
        (y, _), vjp = jax.vjp(_ssd_part, cur_ref[:, 0:1024], _dwconv(ext, cw, cb, XBC_HALO),
                              cur_ref[:, C_DT:C_DT + 128], s_ref[0], dtb, alog, dsk, nw)
        ycat_ref[:, 0:1024] = y.astype(BF16)
        dz, d_conv, ddtr, ds_in, d_dtb, d_alog, d_dsk, d_nw = vjp((_dot_nt(dxn_v, wo_ref[0:1024, :]), ds_scr[...]))
        d_ext, d_cw, d_cb = _dwconv_bwd(XBC_HALO, (ext, cw), d_conv)
        dpxbc, dxbc = d_ext[0:XBC_HALO, :], d_ext[XBC_HALO:, :]
        dproj_ref[:, 0:1024] = dz.astype(BF16)
        dproj_ref[:, C_XBC:C_XBC + 1536] = add_tail(dxbc, pend_xbc[...]).astype(BF16)
        dproj_ref[:, C_DT:C_DT + 128] = ddtr.astype(BF16)
        dproj_ref[:, C_DT + 128:PROJ_W] = jnp.zeros((CHUNK, PROJ_W - C_DT - 128), BF16)
        pend_xbc[...] = dpxbc
        ds_scr[...] = ds_in
        for r, g in ((g_cw, d_cw), (g_cb, d_cb), (g_dtb, d_dtb), (g_alog, d_alog), (g_dsk, d_dsk), (g_nw, d_nw)):
            r[...] += g

    row = lambda b, j: (b * nc + nc - 1 - j, 0)
    prm_shapes = [(4, 1536), (1, 1536), (1, 128), (1, 128), (1, 128), (1, 1024), (1, 128), (CONF_KERNEL, 512),
                  (1, 512), (1, 512), (1, 512)]
    grid = (nb, nc)
    return _pallas_with_side(
        body, None, *_grid_ends(grid), 8 + n_prm, 2 + n_prm, name=name, grid=grid,
        out_shape=[jax.ShapeDtypeStruct((t, PROJ_W), BF16), jax.ShapeDtypeStruct((t, MIX_WIDTH), BF16)]
                  + [jax.ShapeDtypeStruct(s, F32) for s in prm_shapes],
        in_specs=[pl.BlockSpec((CHUNK, D_MODEL), row), pl.BlockSpec((CHUNK, PROJ_W), row)]
                 + _halo_specs(nc, lambda j: nc - 1 - j)
                 + [pl.BlockSpec((1, SSD_HEADS * SSD_HEAD_DIM, SSD_STATE), lambda b, j: (b * nc + nc - 1 - j, 0, 0)),
                    pl.BlockSpec((CHUNK, 512), row), _full((MIX_WIDTH, D_MODEL))] + _param_specs(),
        out_specs=[pl.BlockSpec((CHUNK, PROJ_W), row), pl.BlockSpec((CHUNK, MIX_WIDTH), row)]
                  + [_full(s) for s in prm_shapes],
        scratch_shapes=[pltpu.VMEM((SSD_HEADS * SSD_HEAD_DIM, SSD_STATE), F32), pltpu.VMEM((XBC_HALO, 1536), F32),
                        pltpu.VMEM((CHUNK, 256), F32), pltpu.VMEM((CONF_HALO, 1024), F32)],
        compiler_params=pltpu.CompilerParams(dimension_semantics=("arbitrary", "arbitrary"),
                                             vmem_limit_bytes=VMEM_LIMIT),
        args=(dxn, proj, proj, proj, proj, s_all, conv_all, w_out, *params))


def _gw_out(y_cat, dxn, name):
    t = y_cat.shape[0]
    tk = 512

    def body(y_ref, dxn_ref, out_ref, acc):
        k = pl.program_id(0)

        @pl.when(k == 0)
        def _():
            acc[...] = jnp.zeros_like(acc)

        acc[...] += _dot_tn(y_ref[...], dxn_ref[...])

        @pl.when(k == t // tk - 1)
        def _():
            out_ref[...] = acc[...].astype(BF16)

    out = pl.pallas_call(
        body, name=name, grid=(t // tk,),
        out_shape=jax.ShapeDtypeStruct((MIX_WIDTH, D_MODEL), BF16),
        in_specs=[pl.BlockSpec((tk, MIX_WIDTH), lambda k: (k, 0)), pl.BlockSpec((tk, D_MODEL), lambda k: (k, 0))],
        out_specs=_full((MIX_WIDTH, D_MODEL)),
        scratch_shapes=[pltpu.VMEM((MIX_WIDTH, D_MODEL), F32)],
        compiler_params=pltpu.CompilerParams(dimension_semantics=("arbitrary",), vmem_limit_bytes=VMEM_LIMIT),
    )(y_cat, dxn)
    return out.reshape(N_DEV, MIX_WIDTH // N_DEV, D_MODEL)


def _inproj_bwd_x(dproj, w, x, nw, dxn, name, side=None):
    t = x.shape[0]
    tm = 256

    def body(dp_ref, w_ref, x_ref, nw_ref, dxn_ref, dx_ref, gnw_ref):
        @pl.when(pl.program_id(0) == 0)
        def _():
            gnw_ref[...] = jnp.zeros_like(gnw_ref)

        dh = jnp.zeros((tm, D_MODEL), F32)
        for j in range(N_COL_TILES):
            sl = slice(j * COL_TILE, (j + 1) * COL_TILE)
            dh = dh + _dot_nt(dp_ref[:, sl], w_ref[:, sl])
        _, vjp = jax.vjp(_rmsnorm, x_ref[...], nw_ref[...])
        dx, dnw = vjp(dh)
        dx_ref[...] = dxn_ref[...] + dx
        gnw_ref[...] += dnw

    tok = lambda i: (i, 0)
    grid = (t // tm,)
    return _pallas_with_side(
        body, side, *_grid_ends(grid), 5, 2, name=name, grid=grid,
        out_shape=[jax.ShapeDtypeStruct((t, D_MODEL), F32), jax.ShapeDtypeStruct((1, D_MODEL), F32)],
        in_specs=[pl.BlockSpec((tm, PROJ_W), tok), _full((D_MODEL, PROJ_W)), pl.BlockSpec((tm, D_MODEL), tok),
                  _full((1, D_MODEL)), pl.BlockSpec((tm, D_MODEL), tok)],
        out_specs=[pl.BlockSpec((tm, D_MODEL), tok), _full((1, D_MODEL))],
        compiler_params=pltpu.CompilerParams(dimension_semantics=("arbitrary",), vmem_limit_bytes=VMEM_LIMIT),
        args=(dproj, w, x, nw, dxn))


def _inproj_bwd_w(h, dproj, name, side=None):
    t = h.shape[0]
    tk = 512

    def body(h_ref, dp_ref, gw_ref):
        @pl.when(pl.program_id(1) == 0)
        def _():
            gw_ref[...] = jnp.zeros_like(gw_ref)

        gw_ref[...] += _dot_tn(h_ref[...], dp_ref[...])

    grid = (N_COL_TILES, t // tk)
    return _pallas_with_side(
        body, side, *_grid_ends(grid), 2, 1, name=name, grid=grid,
        out_shape=[jax.ShapeDtypeStruct((D_MODEL, PROJ_W), F32)],
        in_specs=[pl.BlockSpec((tk, D_MODEL), lambda n, k: (k, 0)), pl.BlockSpec((tk, COL_TILE), lambda n, k: (k, n))],
        out_specs=[pl.BlockSpec((D_MODEL, COL_TILE), lambda n, k: (0, n))],
        compiler_params=pltpu.CompilerParams(dimension_semantics=("arbitrary", "arbitrary"),
                                             vmem_limit_bytes=VMEM_LIMIT),
        args=(h, dproj))


def _repack_runs():
    pieces = ((0, 2048, C_Z), (2048, 3584, C_XBC), (3584, 3600, C_DT), (3600, 4368, C_Q), (4368, D_IN_PROJ, C_CA))
    per = D_IN_PROJ // N_DEV
    runs = []
    for j in range(N_DEV):
        lo, hi = per * j, per * (j + 1)
        for a, b, dst in pieces:
            s, e = max(lo, a), min(hi, b)
            if s < e:
                runs.append((j, s - lo, e - lo, dst + s - a))
    return runs


def _repack_w_in(g, name):
    tr = 256

    def body(g_ref, o_ref):
        for j, a, b, dst in _repack_runs():
            o_ref[:, dst:dst + b - a] = g_ref[j, :, a:b]
        o_ref[:, C_DT + 16:PROJ_W] = jnp.zeros((tr, PROJ_W - C_DT - 16), g.dtype)

    return pl.pallas_call(
        body, name=name, grid=(D_MODEL // tr,),
        out_shape=jax.ShapeDtypeStruct((D_MODEL, PROJ_W), g.dtype),
        in_specs=[pl.BlockSpec((N_DEV, tr, D_IN_PROJ // N_DEV), lambda i: (0, i, 0))],
        out_specs=pl.BlockSpec((tr, PROJ_W), lambda i: (i, 0)),
        compiler_params=pltpu.CompilerParams(dimension_semantics=("arbitrary",)),
    )(g)


def _unpack_gw_in(g, name):
    tr = 256

    def body(g_ref, o_ref):
        for j, a, b, dst in _repack_runs():
            o_ref[j, :, a:b] = g_ref[:, dst:dst + b - a].astype(BF16)

    return pl.pallas_call(
        body, name=name, grid=(D_MODEL // tr,),
        out_shape=jax.ShapeDtypeStruct((N_DEV, D_MODEL, D_IN_PROJ // N_DEV), BF16),
        in_specs=[pl.BlockSpec((tr, PROJ_W), lambda i: (i, 0))],
        out_specs=pl.BlockSpec((N_DEV, tr, D_IN_PROJ // N_DEV), lambda i: (0, i, 0)),
        compiler_params=pltpu.CompilerParams(dimension_semantics=("arbitrary",)),
    )(g)


def _pair_sum(parts, name):
    n_dev, r, cdim = parts.shape
    n_chip = n_dev // 2
    by_chip = parts.reshape(n_chip, 2, r, cdim)

    def swap_body(p_ref, got_ref, send_sem, recv_sem):
        x, y, c = _my_place()
        cp = pltpu.make_async_remote_copy(
            src_ref=p_ref.at[:, pl.ds(1 - c, 1)], dst_ref=got_ref, send_sem=send_sem, recv_sem=recv_sem,
            device_id=(x, y, 1 - c), device_id_type=pl.DeviceIdType.MESH)
        cp.start()
        cp.wait()

    any_spec = pl.BlockSpec(memory_space=pl.ANY)
    got = pl.pallas_call(
        swap_body, name=name + "_swap",
        out_shape=jax.ShapeDtypeStruct((n_chip, 1, r, cdim), parts.dtype),
        in_specs=[any_spec], out_specs=any_spec,
        scratch_shapes=[pltpu.SemaphoreType.DMA, pltpu.SemaphoreType.DMA],
    )(by_chip)

    tr = 256

    def add_body(core_ref, p_ref, got_ref, o_ref):
        o_ref[0] = (p_ref[0, 0].astype(F32) + got_ref[0, 0].astype(F32)).astype(o_ref.dtype)

    return pl.pallas_call(
        add_body, name=name + "_add",
        grid_spec=pltpu.PrefetchScalarGridSpec(
            num_scalar_prefetch=1, grid=(n_chip, r // tr),
            in_specs=[pl.BlockSpec((1, 1, tr, cdim), lambda k, i, core: (k, core[0], i, 0)),
                      pl.BlockSpec((1, 1, tr, cdim), lambda k, i, core: (k, 0, i, 0))],
            out_specs=pl.BlockSpec((1, tr, cdim), lambda k, i, core: (k, i, 0))),
        out_shape=jax.ShapeDtypeStruct((n_chip, r, cdim), parts.dtype),
        compiler_params=pltpu.CompilerParams(dimension_semantics=("arbitrary", "arbitrary")),
    )(lax.axis_index("c").astype(jnp.int32).reshape(1), by_chip, got)


def _loss_head(x, fnw, target, name):
    t = x.shape[0]
    tm = 512

    def body(x_ref, w_ref, t_ref, dx_ref, loss_ref, gw_ref):
        @pl.when(pl.program_id(0) == 0)
        def _():
            loss_ref[...] = jnp.zeros_like(loss_ref)
            gw_ref[...] = jnp.zeros_like(gw_ref)

        y, vjp = jax.vjp(_rmsnorm, x_ref[...], w_ref[...])
        err = y - t_ref[...]
        loss_ref[...] += 0.5 * jnp.sum(jnp.mean(err * err, axis=-1, keepdims=True), axis=0, keepdims=True)
        dx, dw = vjp(err * (1.0 / D_MODEL))
        dx_ref[...] = dx
        gw_ref[...] += dw

    tok = lambda i: (i, 0)
    return pl.pallas_call(
        body, name=name, grid=(t // tm,),
        out_shape=[jax.ShapeDtypeStruct((t, D_MODEL), F32), jax.ShapeDtypeStruct((1, 1), F32),
                   jax.ShapeDtypeStruct((1, D_MODEL), F32)],
        in_specs=[pl.BlockSpec((tm, D_MODEL), tok), _full((1, D_MODEL)), pl.BlockSpec((tm, D_MODEL), tok)],
        out_specs=[pl.BlockSpec((tm, D_MODEL), tok), _full((1, 1)), _full((1, D_MODEL))],
        compiler_params=pltpu.CompilerParams(dimension_semantics=("arbitrary",)),
    )(x, fnw, target)


def _adamw(w, g, m, v):
    m = ADAM_B1 * m + (1.0 - ADAM_B1) * g
    v = ADAM_B2 * v + (1.0 - ADAM_B2) * jnp.square(g)
    m_hat = m / (1.0 - ADAM_B1 ** ADAM_STEP)
    v_hat = v / (1.0 - ADAM_B2 ** ADAM_STEP)
    delta = -ADAM_LR * (m_hat / (jnp.sqrt(v_hat) + ADAM_EPS) + ADAM_WD * w)
    return delta, m, v


def _reduce_adamw(parts, w, m, v, tr, name):
    depth = len(parts)
    p, r, cdim = parts[0].shape
    n_blk = r // tr

    def body(*refs):
        p_refs = refs[:depth]
        w_ref, m_ref, v_ref, g_ref, d_ref, nm_ref, nv_ref = refs[depth:]
        for layer in range(depth):
            @pl.when(pl.program_id(0) == layer)
            def _(p_ref=p_refs[layer]):
                g = p_ref[0].astype(F32)
                for i in range(1, p):
                    g = g + p_ref[i].astype(F32)
                g_ref[0] = g
                d_ref[0], nm_ref[0], nv_ref[0] = _adamw(w_ref[0], g, m_ref[0], v_ref[0])

    def parts_spec(layer):
        return pl.BlockSpec((p, tr, cdim), lambda d, i: (0, jnp.clip(i + (d - layer) * n_blk, 0, n_blk - 1), 0))

    blk = pl.BlockSpec((1, tr, cdim), lambda d, i: (d, i, 0))
    return pl.pallas_call(
        body, name=name, grid=(depth, n_blk),
        out_shape=[jax.ShapeDtypeStruct(w.shape, F32)] * 4,
        in_specs=[parts_spec(layer) for layer in range(depth)] + [blk, blk, blk],
        out_specs=[blk] * 4,
        compiler_params=pltpu.CompilerParams(dimension_semantics=("arbitrary", "arbitrary"),
                                             vmem_limit_bytes=VMEM_LIMIT),
    )(*parts, w, m, v)


def _adamw_small(ssum, entries, name):
    direct = [e[3] for e in entries if not isinstance(e[3], list)]
    n_direct = len(direct)

    def body(*refs):
        ssum_ref, direct_refs = refs[0], list(refs[1:1 + n_direct])
        ins = refs[1 + n_direct:1 + n_direct + 3 * len(entries)]
        outs = refs[1 + n_direct + 3 * len(entries):]
        for k, (w, _, _, grad) in enumerate(entries):
            w_ref, m_ref, v_ref = ins[3 * k:3 * k + 3]
            g_ref, d_ref, nm_ref, nv_ref = outs[4 * k:4 * k + 4]
            if isinstance(grad, list):
                for row, off in enumerate(grad):
                    rows = slice(row, row + 1)
                    g = ssum_ref[:, off:off + w.shape[1]]
                    g_ref[rows, :] = g
                    d_ref[rows, :], nm_ref[rows, :], nv_ref[rows, :] = _adamw(w_ref[rows, :], g, m_ref[rows, :],
                                                                              v_ref[rows, :])
            else:
                g = direct_refs.pop(0)[...]
                g_ref[...] = g
                d_ref[...], nm_ref[...], nv_ref[...] = _adamw(w_ref[...], g, m_ref[...], v_ref[...])

    vmem = pl.BlockSpec(memory_space=pltpu.VMEM)
    args = [ssum] + direct + [a for e in entries for a in e[:3]]
    res = pl.pallas_call(
        body, name=name,
        out_shape=[jax.ShapeDtypeStruct(e[0].shape, F32) for e in entries for _ in range(4)],
        in_specs=[vmem] * len(args), out_specs=[vmem] * (4 * len(entries)),
    )(*args)
    return [res[4 * k:4 * k + 4] for k in range(len(entries))]


def _reduce_adamw_cols(parts, w, m, v, name):
    depth = len(parts)
    _, r, cdim = parts[0].shape
    tc = 512

    def body(*refs):
        p_refs = refs[:depth]
        w_ref, m_ref, v_ref, g_ref, d_ref, nm_ref, nv_ref = refs[depth:]
        for layer in range(depth):
            g = p_refs[layer][0].astype(F32)
            for i in range(1, parts[layer].shape[0]):
                g = g + p_refs[layer][i].astype(F32)
            g = g.T
            g_ref[:, layer, :] = g
            d_ref[:, layer, :], nm_ref[:, layer, :], nv_ref[:, layer, :] = _adamw(
                w_ref[:, layer, :], g, m_ref[:, layer, :], v_ref[:, layer, :])

    view = lambda a: jnp.transpose(a, (2, 0, 1))
    blk = pl.BlockSpec((cdim, depth, tc), lambda i: (0, 0, i))
    outs = pl.pallas_call(
        body, name=name, grid=(r // tc,),
        out_shape=[jax.ShapeDtypeStruct((cdim, depth, r), F32)] * 4,
        in_specs=[pl.BlockSpec((a.shape[0], tc, cdim), lambda i: (0, i, 0)) for a in parts] + [blk, blk, blk],
        out_specs=[blk] * 4,
        compiler_params=pltpu.CompilerParams(dimension_semantics=("arbitrary",), vmem_limit_bytes=VMEM_LIMIT),
    )(*parts, view(w), view(m), view(v))
    return [jnp.transpose(o, (1, 2, 0)) for o in outs]


def _pad_lanes(v, width=128):
    return jnp.pad(v.reshape(1, -1), ((0, 0), (0, width - v.shape[-1])))


SMALL_FIELDS = (("norm_w", 1024), ("conv_b", 1536), ("dt_bias", 128), ("a_log", 128), ("d_skip", 128),
                ("ssd_norm_w", 1024), ("sinks", 128), ("dw_b", 512), ("ln_w", 512), ("ln_b", 512))


def kernel(x, norm_w, w_in, ssd_conv_w, ssd_conv_b, ssd_dt_bias, ssd_a_log, ssd_d, ssd_norm_w, attn_sinks, conf_dw_w, conf_dw_b, conf_ln_w, conf_ln_b, w_out, final_norm_w, loss_target, m_norm_w, m_w_in, m_ssd_conv_w, m_ssd_conv_b, m_ssd_dt_bias, m_ssd_a_log, m_ssd_d, m_ssd_norm_w, m_attn_sinks, m_conf_dw_w, m_conf_dw_b, m_conf_ln_w, m_conf_ln_b, m_w_out, m_final_norm_w, v_norm_w, v_w_in, v_ssd_conv_w, v_ssd_conv_b, v_ssd_dt_bias, v_ssd_a_log, v_ssd_d, v_ssd_norm_w, v_attn_sinks, v_conf_dw_w, v_conf_dw_b, v_conf_ln_w, v_conf_ln_b, v_w_out, v_final_norm_w):
    nb, seq, _ = x.shape
    depth = norm_w.shape[0]
    t = nb * seq
    me_idx = 4 * lax.axis_index("x") + 2 * lax.axis_index("y") + lax.axis_index("c")

    w_in_bf, w_out_bf = w_in.astype(BF16), w_out.astype(BF16)
    g_win0, g_cw, g_dw = _all_gather([w_in_bf[0], ssd_conv_w, conf_dw_w], "gather_weights")
    w_in_full = [_repack_w_in(g_win0, "repack_w_in_0")]
    w_out_full = []
    conv_w_full = [jnp.transpose(g_cw[:, l], (1, 0, 2)).reshape(4, 1536) for l in range(depth)]
    dw_w_full = [jnp.transpose(g_dw[:, l], (1, 0, 2)).reshape(CONF_KERNEL, 512) for l in range(depth)]

    def layer_params(l):
        return [conv_w_full[l], ssd_conv_b[l].reshape(1, -1), _pad_lanes(ssd_dt_bias[l]), _pad_lanes(ssd_a_log[l]),
                _pad_lanes(ssd_d[l]), ssd_norm_w[l].reshape(1, -1), _pad_lanes(attn_sinks[l]), dw_w_full[l],
                conf_dw_b[l].reshape(1, -1), conf_ln_w[l].reshape(1, -1), conf_ln_b[l].reshape(1, -1)]

    xs = [x.reshape(t, D_MODEL)]
    saved = []
    for l in range(depth):
        (proj, h), gathered = _inproj_fwd(xs[l], norm_w[l].reshape(1, -1), w_in_full[l], f"inproj_fwd_{l}",
                                          (GATHER, [w_out_bf[l]]))
        w_out_full.append(gathered[0].reshape(MIX_WIDTH, D_MODEL))
        side = (GATHER, [w_in_bf[l + 1]]) if l + 1 < depth else None
        (x_next, *kept), gathered = _mixer_fwd(xs[l], proj, w_out_full[l], layer_params(l), nb, f"mixer_fwd_{l}",
                                               side)
        if side:
            w_in_full.append(_repack_w_in(gathered[0], f"repack_w_in_{l + 1}"))
        saved.append((proj, h, kept))
        xs.append(x_next)
    dx, loss_part, g_fnw = _loss_head(xs[depth], final_norm_w.reshape(1, -1), loss_target.reshape(t, D_MODEL),
                                      "loss_head")

    cols_in = D_IN_PROJ // N_DEV
    rows_out = MIX_WIDTH // N_DEV
    small_rows = [None] * depth
    received = [None] * depth
    for l in reversed(range(depth)):
        proj, h, kept = saved[l]
        res, _ = _mixer_bwd(dx, proj, *kept, w_out_full[l], layer_params(l), nb, f"mixer_bwd_{l}")
        dproj, y_cat = res[0], res[1]
        g_cw_l, g_cb, g_dtb, g_alog, g_dsk, g_nw, g_snk, g_dww, g_dwb, g_lnw, g_lnb = res[2:]
        gw_out_parts = _gw_out(y_cat, dx, f"gw_out_{l}")
        (gw_in,), got_out = _inproj_bwd_w(h, dproj, f"inproj_bwd_w_{l}", (SCATTER, [gw_out_parts]))
        chip_parts = _pair_sum(_unpack_gw_in(gw_in, f"unpack_gw_in_{l}"), f"pair_sum_{l}")
        (dx, g_norm), got_in = _inproj_bwd_x(dproj, w_in_full[l], xs[l], norm_w[l].reshape(1, -1), dx,
                                             f"inproj_bwd_x_{l}", (CHIP_SCATTER, [chip_parts]))
        received[l] = [got_in[0], got_out[0]]
        small_rows[l] = [g_norm, g_cb, g_dtb, g_alog, g_dsk, g_nw, g_snk, g_dwb, g_lnw, g_lnb,
                         g_cw_l.reshape(1, -1), g_dww.reshape(1, -1)]
    grad_x = dx.reshape(nb, seq, D_MODEL)

    small = jnp.concatenate([piece for l in range(depth) for piece in small_rows[l]]
                            + [g_fnw, _pad_lanes(loss_part)], axis=1)
    ssum = _exchange_small(small, "exchange_small")
    loss = ssum[0, small.shape[1] - 128]

    g_w_in, d_w_in, nm_w_in, nv_w_in = _reduce_adamw_cols([received[l][0] for l in range(depth)], w_in, m_w_in,
                                                          v_w_in, "adamw_w_in")
    g_w_out, d_w_out, nm_w_out, nv_w_out = _reduce_adamw([received[l][1] for l in range(depth)], w_out, m_w_out,
                                                         v_w_out, 256, "adamw_w_out")

    per_layer = sum(n for _, n in SMALL_FIELDS) + 4 * 1536 + CONF_KERNEL * 512
    given = {"norm_w": (norm_w, m_norm_w, v_norm_w), "conv_b": (ssd_conv_b, m_ssd_conv_b, v_ssd_conv_b),
             "dt_bias": (ssd_dt_bias, m_ssd_dt_bias, v_ssd_dt_bias), "a_log": (ssd_a_log, m_ssd_a_log, v_ssd_a_log),
             "d_skip": (ssd_d, m_ssd_d, v_ssd_d), "ssd_norm_w": (ssd_norm_w, m_ssd_norm_w, v_ssd_norm_w),
             "sinks": (attn_sinks, m_attn_sinks, v_attn_sinks), "dw_b": (conf_dw_b, m_conf_dw_b, v_conf_dw_b),
             "ln_w": (conf_ln_w, m_conf_ln_w, v_conf_ln_w), "ln_b": (conf_ln_b, m_conf_ln_b, v_conf_ln_b)}
    entries = []
    off = 0
    for fname, n in SMALL_FIELDS:
        entries.append((*given[fname], [l * per_layer + off for l in range(depth)]))
        off += n
    shard_grads = []
    for width, shard, kk in ((1536, 192, 4), (512, 64, CONF_KERNEL)):
        full = [ssum[:, l * per_layer + off:l * per_layer + off + kk * width].reshape(kk, width) for l in range(depth)]
        shard_grads.append(jnp.stack([lax.dynamic_slice(f, (0, me_idx * shard), (kk, shard)) for f in full], axis=0))
        off += kk * width
    entries.append((ssd_conv_w, m_ssd_conv_w, v_ssd_conv_w, shard_grads[0]))
    entries.append((conf_dw_w, m_conf_dw_w, v_conf_dw_w, shard_grads[1]))
    entries.append((final_norm_w.reshape(1, -1), m_final_norm_w.reshape(1, -1), v_final_norm_w.reshape(1, -1),
                    [depth * per_layer]))
    sm = _adamw_small(ssum, entries, "adamw_small")
    sm = {k: quad for k, quad in zip([f for f, _ in SMALL_FIELDS] + ["conv_w", "dw_w", "final"], sm)}

    def outputs(i, big_in_i, big_out_i):
        return [sm["norm_w"][i], big_in_i, sm["conv_w"][i], sm["conv_b"][i], sm["dt_bias"][i], sm["a_log"][i],
                sm["d_skip"][i], sm["ssd_norm_w"][i], sm["sinks"][i], sm["dw_w"][i], sm["dw_b"][i], sm["ln_w"][i],
                sm["ln_b"][i], big_out_i, sm["final"][i].reshape(-1)]

    return (loss, grad_x, *outputs(0, g_w_in, g_w_out), *outputs(1, d_w_in, d_w_out),
            *outputs(2, nm_w_in, nm_w_out), *outputs(3, nv_w_in, nv_w_out))
```

```python
import functools

import jax
import jax.numpy as jnp
from jax import lax
from jax.experimental import pallas as pl
from jax.experimental.pallas import tpu as pltpu

F32 = jnp.float32
BF16 = jnp.bfloat16
N_DEV = 8
EPS = 1e-5

D_MODEL = 1024
CHUNK = 128
SSD_HEADS = 16
SSD_HEAD_DIM = 64
SSD_STATE = 128
ATTN_HEADS = 8
ATTN_HEAD_DIM = 64
CONF_KERNEL = 31
MIX_WIDTH = 2048
D_IN_PROJ = 5392
C_Z = 0
C_CA = 2048
C_XBC = 3072
C_Q = 4608
C_K = 5120
C_V = 5248
C_DT = 5376
PROJ_W = 5632
N_COL_TILES = 4
COL_TILE = PROJ_W // N_COL_TILES
XBC_HALO = 8
CONF_HALO = 32
ATTN_STACK_FWD = 2
ATTN_STACK_BWD = 4
VMEM_LIMIT = 56 * 1024 * 1024

ADAM_LR = 0.001
ADAM_B1 = 0.9
ADAM_B2 = 0.999
ADAM_EPS = 1e-08
ADAM_WD = 0.01
ADAM_STEP = 10


def _silu(v):
    return v * jax.nn.sigmoid(v)


def _softplus(v):
    return jnp.maximum(v, 0.0) + jnp.log1p(jnp.exp(-jnp.abs(v)))


def _rmsnorm(v, w):
    return v * lax.rsqrt(jnp.mean(v * v, axis=-1, keepdims=True) + EPS) * w


def _dot(a, b):
    return jnp.dot(a.astype(BF16), b.astype(BF16), preferred_element_type=F32)


def _dot_nt(a, b):
    return lax.dot_general(a.astype(BF16), b.astype(BF16), (((1,), (1,)), ((), ())), preferred_element_type=F32)


def _dot_tn(a, b):
    return lax.dot_general(a.astype(BF16), b.astype(BF16), (((0,), (0,)), ((), ())), preferred_element_type=F32)


def _taps(ext, offs, out_len, w=None, g=None):
    n_rows, n_cols = ext.shape
    by_shift = {}
    for t, off in enumerate(offs):
        by_shift.setdefault(off % 8, []).append((t, off))
    for r, taps in by_shift.items():
        assert max(off for _, off in taps) - r + out_len <= n_rows - r
    accs = []
    sums = [[None] * (n_cols // 128) for _ in offs]
    for blk in range(n_cols // 128):
        cs = slice(blk * 128, (blk + 1) * 128)
        e = ext[:, cs]
        acc = None
        for r, taps in by_shift.items():
            shifted = e if r == 0 else pltpu.roll(e, n_rows - r, axis=0)
            for t, off in taps:
                window = shifted[off - r:off - r + out_len, :]
                if w is not None:
                    term = w[t:t + 1, cs] * window
                    acc = term if acc is None else acc + term
                if g is not None:
                    sums[t][blk] = jnp.sum(g[:, cs] * window, axis=0, keepdims=True)
        accs.append(acc)
    if w is not None:
        return jnp.concatenate(accs, axis=1)
    return jnp.concatenate([jnp.concatenate(row, axis=1) for row in sums], axis=0)


@functools.partial(jax.custom_vjp, nondiff_argnums=(3,))
def _dwconv(ext, w, b, halo):
    kk = w.shape[0]
    return b + _taps(ext, [halo - (kk - 1) + t for t in range(kk)], ext.shape[0] - halo, w=w)


def _dwconv_fwd(ext, w, b, halo):
    return _dwconv(ext, w, b, halo), (ext, w)


def _dwconv_bwd(halo, res, g):
    ext, w = res
    kk = w.shape[0]
    offs = [halo - (kk - 1) + t for t in range(kk)]
    dw = _taps(ext, offs, ext.shape[0] - halo, g=g)
    zeros = jnp.zeros((halo, g.shape[1]), g.dtype)
    gp = jnp.concatenate([zeros, g, zeros], axis=0)
    dext = _taps(gp, [halo - off for off in offs], ext.shape[0], w=w)
    return dext, dw, jnp.sum(g, axis=0, keepdims=True)


_dwconv.defvjp(_dwconv_fwd, _dwconv_bwd)


def _ssd_part(z_ssd, conv_out, dtr, s_in, dtb, alog, dsk, nw):
    qn = conv_out.shape[0]
    nh = SSD_HEADS
    per_group = nh // 2
    n_pair = nh // 2
    xa = _silu(conv_out)
    xs = xa[:, 0:1024]
    dt = _softplus(dtr + dtb)
    a = dt * (-jnp.exp(alog))
    rows = lax.broadcasted_iota(jnp.int32, (qn, qn), 0)
    cols = lax.broadcasted_iota(jnp.int32, (qn, qn), 1)
    causal = rows >= cols
    low = cols < SSD_HEAD_DIM
    a_cs = jnp.dot(causal.astype(F32), a, precision=lax.Precision.HIGHEST, preferred_element_type=F32)
    a_cs_t = a_cs.T
    bgs = [xa[:, 1024 + g * 128:1024 + (g + 1) * 128] for g in range(2)]
    cgs = [xa[:, 1280 + g * 128:1280 + (g + 1) * 128] for g in range(2)]
    cbms = [_dot_nt(cgs[g], bgs[g]) for g in range(2)]
    colb = [jnp.broadcast_to(a_cs[:, h:h + 1], (qn, qn)) for h in range(nh)]
    lastb = [jnp.broadcast_to(colb[h][qn - 1:qn, :], (qn, qn)) for h in range(nh)]
    dtb_wide = [jnp.broadcast_to(dt[:, h:h + 1], (qn, qn)) for h in range(nh)]
    lmats = [jnp.exp(jnp.where(causal, colb[h] - a_cs_t[h:h + 1, :], -jnp.inf)) for h in range(nh)]
    ms = [cbms[h // per_group] * lmats[h] for h in range(nh)]
    x_pair = [xs[:, p * 128:(p + 1) * 128] for p in range(n_pair)]
    x_lo = [x_pair[p] * jnp.where(low, dtb_wide[2 * p], 0.0) for p in range(n_pair)]
    x_hi = [x_pair[p] * jnp.where(low, 0.0, dtb_wide[2 * p + 1]) for p in range(n_pair)]
    y_diag = [_dot(ms[2 * p], x_lo[p]) + _dot(ms[2 * p + 1], x_hi[p]) for p in range(n_pair)]
    col_pair = [jnp.where(low, colb[2 * p], colb[2 * p + 1]) for p in range(n_pair)]
    last_pair = [jnp.where(low, lastb[2 * p], lastb[2 * p + 1]) for p in range(n_pair)]
    ecol = [jnp.exp(col_pair[p]) for p in range(n_pair)]
    xw = [(x_lo[p] + x_hi[p]) * jnp.exp(last_pair[p] - col_pair[p]) for p in range(n_pair)]
    y_off, st = [], []
    for g in range(2):
        ps = range(g * n_pair // 2, (g + 1) * n_pair // 2)
        y_off.append(_dot_nt(cgs[g], s_in[g * 512:(g + 1) * 512, :]) * jnp.concatenate([ecol[p] for p in ps], axis=1))
        st.append(_dot_tn(jnp.concatenate([xw[p] for p in ps], axis=1), bgs[g]))
    e_last = jnp.exp(jnp.broadcast_to(a_cs_t[:, qn - 1:qn], (qn, SSD_STATE)))
    scale = jnp.concatenate([jnp.broadcast_to(e_last[h:h + 1, :], (64, SSD_STATE)) for h in range(nh)], axis=0)
    s_out = scale * s_in + jnp.concatenate(st, axis=0)
    d_wide = jnp.concatenate([jnp.broadcast_to(dsk[:, h:h + 1], (1, 64)) for h in range(nh)], axis=1)
    y = jnp.concatenate(y_diag, axis=1) + jnp.concatenate(y_off, axis=1) + d_wide * xs
    gated = y * _silu(z_ssd)
    halves = []
    for g in range(2):
        gg = gated[:, g * 512:(g + 1) * 512]
        halves.append(gg * lax.rsqrt(jnp.mean(gg * gg, axis=-1, keepdims=True) + EPS))
    return jnp.concatenate(halves, axis=1) * nw, s_out


def _attn_part(z_attn, q, kv, p_kv, snk, kvmask, stack):
    qn = q.shape[0]
    kk = jnp.concatenate([p_kv[:, 0:128], kv[:, 0:128]], axis=0)
    vv = jnp.concatenate([p_kv[:, 128:256], kv[:, 128:256]], axis=0)
    units = range(ATTN_HEADS // stack)
    heads = [range(u * stack, (u + 1) * stack) for u in units]
    kv_of = [u * stack // (ATTN_HEADS // 2) for u in units]
    k_of = [kk[:, g * 64:(g + 1) * 64] for g in kv_of]
    v_of = [vv[:, g * 64:(g + 1) * 64] for g in kv_of]
    qs = [jnp.concatenate([q[:, h * 64:(h + 1) * 64] for h in heads[u]], axis=0) for u in units]
    sk = [jnp.concatenate([jnp.broadcast_to(snk[:, h:h + 1], (qn, 1)) for h in heads[u]], axis=0) for u in units]
    s = [jnp.where(kvmask, _dot_nt(qs[u], k_of[u]) * (ATTN_HEAD_DIM ** -0.5), -jnp.inf) for u in units]
    m = [lax.stop_gradient(jnp.maximum(jnp.max(s[u], axis=-1, keepdims=True), sk[u])) for u in units]
    e = [jnp.exp(s[u] - m[u]) for u in units]
    r_den = [1.0 / (jnp.sum(e[u], axis=-1, keepdims=True) + jnp.exp(sk[u] - m[u])) for u in units]
    o = [_dot(e[u], v_of[u]) * r_den[u] for u in units]
    outs = [o[u][i * qn:(i + 1) * qn, :] for u in units for i in range(stack)]
    return jnp.concatenate(outs, axis=1) * _silu(z_attn)


def _conf_glu(cacg, p_cc):
    c0 = cacg[:, 0:512] * jax.nn.sigmoid(cacg[:, 512:1024])
    pc0 = p_cc[:, 0:512] * jax.nn.sigmoid(p_cc[:, 512:1024])
    return jnp.concatenate([pc0, c0], axis=0)


def _conf_tail(conv_out, z_conf, lnw, lnb):
    xc = conv_out - jnp.mean(conv_out, axis=-1, keepdims=True)
    yln = xc * lax.rsqrt(jnp.mean(xc * xc, axis=-1, keepdims=True) + EPS) * lnw + lnb
    return _silu(yln) * _silu(z_conf)


def _kv_mask(qn, not_first, reps):
    ii = lax.broadcasted_iota(jnp.int32, (reps * qn, 2 * qn), 0) & (qn - 1)
    jj = lax.broadcasted_iota(jnp.int32, (reps * qn, 2 * qn), 1)
    d = jj - ii
    return (d >= 1) & (d <= qn) & (not_first | (jj >= qn))


def _my_place():
    return lax.axis_index("x"), lax.axis_index("y"), lax.axis_index("c")


def _all_gather(arrs, name):
    n = len(arrs)

    def body(*refs):
        ins, outs = refs[:n], refs[n:2 * n]
        send_sems, recv_sems, local_sems = refs[2 * n:]
        x, y, c = _my_place()
        me, sibling = (x, y, c), (x, y, 1 - c)
        chips = [(1 - x, y), (x, 1 - y), (1 - x, 1 - y)]

        def slot(a, p):
            return outs[a].at[4 * p[0] + 2 * p[1] + p[2]]

        def copy(a, kk, block, to, src=None):
            return pltpu.make_async_remote_copy(
                src_ref=slot(a, block) if src is None else src, dst_ref=slot(a, block),
                send_sem=send_sems.at[a, kk], recv_sem=recv_sems.at[a, kk],
                device_id=to, device_id_type=pl.DeviceIdType.MESH)

        mine = [pltpu.make_async_copy(ins[a], slot(a, me), local_sems.at[a]) for a in range(n)]
        for cp in mine:
            cp.start()
        first = []
        for a in range(n):
            first.append(copy(a, 0, me, sibling, src=ins[a]))
            first += [copy(a, 1 + j, me, (*chip, c), src=ins[a]) for j, chip in enumerate(chips)]
        for cp in first:
            cp.start()
        passed = []
        for j, chip in enumerate(chips):
            for a in range(n):
                copy(a, 1 + j, (*chip, c), me).wait_recv()
                fwd = copy(a, 4 + j, (*chip, c), sibling)
                fwd.start()
                passed.append(fwd)
        for a in range(n):
            copy(a, 0, sibling, me).wait_recv()
            for j, chip in enumerate(chips):
                copy(a, 4 + j, (*chip, 1 - c), me).wait_recv()
        for cp in first + passed:
            cp.wait_send()
        for cp in mine:
            cp.wait()

    any_spec = pl.BlockSpec(memory_space=pl.ANY)
    return pl.pallas_call(
        body, name=name,
        out_shape=[jax.ShapeDtypeStruct((N_DEV,) + a.shape, a.dtype) for a in arrs],
        in_specs=[any_spec] * n, out_specs=[any_spec] * n,
        scratch_shapes=[pltpu.SemaphoreType.DMA((n, 7)), pltpu.SemaphoreType.DMA((n, 7)),
                        pltpu.SemaphoreType.DMA((n,))],
    )(*arrs)


GATHER, SCATTER, CHIP_SCATTER = "gather", "scatter", "chip_scatter"


def _direct_copies(mode, ins, outs, send_sems, recv_sems, local_sems):
    x, y, c = _my_place()
    by_chip = mode == CHIP_SCATTER
    place = (lambda px, py, pc: 2 * px + py) if by_chip else (lambda px, py, pc: 4 * px + 2 * py + pc)
    me_idx = place(x, y, c)
    n = len(ins)
    local = [pltpu.make_async_copy(ins[a] if mode == GATHER else ins[a].at[me_idx], outs[a].at[me_idx],
                                   local_sems.at[a]) for a in range(n)]
    remote = []
    for rel in range(1, N_DEV):
        if by_chip and rel & 1:
            continue
        px = 1 - x if rel & 4 else x
        py = 1 - y if rel & 2 else y
        pc = 1 - c if rel & 1 else c
        for a in range(n):
            remote.append(pltpu.make_async_remote_copy(
                src_ref=ins[a] if mode == GATHER else ins[a].at[place(px, py, pc)], dst_ref=outs[a].at[me_idx],
                send_sem=send_sems.at[a, rel - 1], recv_sem=recv_sems.at[a, rel - 1],
                device_id=(px, py, pc), device_id_type=pl.DeviceIdType.MESH))
    return local + remote


def _side_scratch(n):
    return [pltpu.SemaphoreType.DMA((n, 7)), pltpu.SemaphoreType.DMA((n, 7)), pltpu.SemaphoreType.DMA((n,))]


def _side_out_shapes(mode, arrs):
    return [jax.ShapeDtypeStruct((N_DEV,) + a.shape if mode == GATHER else a.shape, a.dtype) for a in arrs]


def _pallas_with_side(body, side, first, last, n_in, n_out, *, in_specs, out_specs, out_shape, scratch_shapes=(),
                      args, **kwargs):
    side_arrs = [] if side is None else list(side[1])
    ns = len(side_arrs)

    def wrapped(*refs):
        own_in, side_in = refs[:n_in], refs[n_in:n_in + ns]
        o = n_in + ns
        own_out, side_out = refs[o:o + n_out], refs[o + n_out:o + n_out + ns]
        scratch = refs[o + n_out + ns:]
        own_scratch, sems = (scratch[:-3], scratch[-3:]) if ns else (scratch, ())
        body(*own_in, *own_out, *own_scratch)
        if ns:
            @pl.when(first())
            def _():
                for cp in _direct_copies(side[0], side_in, side_out, *sems):
                    cp.start()

            @pl.when(last())
            def _():
                for cp in _direct_copies(side[0], side_in, side_out, *sems):
                    cp.wait()

    any_spec = pl.BlockSpec(memory_space=pl.ANY)
    res = pl.pallas_call(
        wrapped,
        in_specs=list(in_specs) + [any_spec] * ns,
        out_specs=list(out_specs) + [any_spec] * ns,
        out_shape=list(out_shape) + (_side_out_shapes(side[0], side_arrs) if ns else []),
        scratch_shapes=list(scratch_shapes) + (_side_scratch(ns) if ns else []),
        **kwargs,
    )(*args, *side_arrs)
    return res[:n_out], res[n_out:]


def _exchange_small(small, name):
    ns = small.shape[1]

    def body(small_ref, sum_ref, small_all, send_sems, recv_sems, local_sems):
        copies = _direct_copies(GATHER, [small_ref], [small_all], send_sems, recv_sems, local_sems)
        for cp in copies:
            cp.start()
        for cp in copies:
            cp.wait()
        total = small_all[0]
        for i in range(1, N_DEV):
            total = total + small_all[i]
        sum_ref[...] = total

    vmem_spec = pl.BlockSpec(memory_space=pltpu.VMEM)
    return pl.pallas_call(
        body, name=name,
        out_shape=jax.ShapeDtypeStruct((1, ns), F32),
        in_specs=[vmem_spec], out_specs=vmem_spec,
        scratch_shapes=[pltpu.VMEM((N_DEV, 1, ns), F32)] + _side_scratch(1),
    )(small)


def _full(shape):
    return pl.BlockSpec(shape, lambda *_: (0,) * len(shape))


def _inproj_fwd(x, nw, w, name, side=None):
    t = x.shape[0]
    tm = 512

    def body(x_ref, nw_ref, w_ref, proj_ref, h_ref):
        h = _rmsnorm(x_ref[...], nw_ref[...]).astype(BF16)
        h_ref[...] = h
        for j in range(N_COL_TILES):
            sl = slice(j * COL_TILE, (j + 1) * COL_TILE)
            proj_ref[:, sl] = jnp.dot(h, w_ref[:, sl], preferred_element_type=F32)

    grid = (t // tm,)
    return _pallas_with_side(
        body, side, *_grid_ends(grid), 3, 2, name=name, grid=grid,
        out_shape=[jax.ShapeDtypeStruct((t, PROJ_W), F32), jax.ShapeDtypeStruct((t, D_MODEL), BF16)],
        in_specs=[pl.BlockSpec((tm, D_MODEL), lambda i: (i, 0)), _full((1, D_MODEL)), _full((D_MODEL, PROJ_W))],
        out_specs=[pl.BlockSpec((tm, PROJ_W), lambda i: (i, 0)), pl.BlockSpec((tm, D_MODEL), lambda i: (i, 0))],
        compiler_params=pltpu.CompilerParams(dimension_semantics=("arbitrary",), vmem_limit_bytes=VMEM_LIMIT),
        args=(x, nw, w))


def _param_specs():
    return [_full((4, 1536)), _full((1, 1536)), _full((1, 128)), _full((1, 128)), _full((1, 128)),
            _full((1, 1024)), _full((1, 128)), _full((CONF_KERNEL, 512)), _full((1, 512)), _full((1, 512)),
            _full((1, 512))]


def _halo_specs(nc, chunk_of):
    def prev_chunk(b, j):
        return jnp.maximum(b * nc + chunk_of(j) - 1, 0)

    per_xbc = CHUNK // XBC_HALO
    per_cc = CHUNK // CONF_HALO
    return [
        pl.BlockSpec((XBC_HALO, 1536), lambda b, j: (prev_chunk(b, j) * per_xbc + per_xbc - 1, C_XBC // 1536)),
        pl.BlockSpec((CHUNK, 256), lambda b, j: (prev_chunk(b, j), C_K // 256)),
        pl.BlockSpec((CONF_HALO, 1024), lambda b, j: (prev_chunk(b, j) * per_cc + per_cc - 1, C_CA // 1024)),
    ]


def _grid_ends(grid):
    first = lambda: functools.reduce(lambda p, q: p & q, [pl.program_id(i) == 0 for i in range(len(grid))])
    last = lambda: functools.reduce(lambda p, q: p & q, [pl.program_id(i) == n - 1 for i, n in enumerate(grid)])
    return first, last


def _mixer_fwd(x, proj, w_out, params, nb, name, side=None):
    t = x.shape[0]
    nc = t // nb // CHUNK

    def body(x_ref, cur_ref, pxbc_ref, pkv_ref, pcc_ref, wo_ref, *rest):
        prm = [r[...] for r in rest[:11]]
        xn_ref, sall_ref, conv_ref, s_scr = rest[11:]
        c = pl.program_id(1)
        not_first = c > 0
        nf = not_first.astype(F32)

        @pl.when(c == 0)
        def _():
            s_scr[...] = jnp.zeros_like(s_scr)

        cw, cb, dtb, alog, dsk, nw, snk, dww, dwb, lnw, lnb = prm
        s_in = s_scr[...]
        sall_ref[0] = s_in
        ssd_conv = _dwconv(jnp.concatenate([pxbc_ref[...] * nf, cur_ref[:, C_XBC:C_XBC + 1536]], axis=0), cw, cb,
                           XBC_HALO)
        y_ssd, s_out = _ssd_part(cur_ref[:, 0:1024], ssd_conv, cur_ref[:, C_DT:C_DT + 128], s_in, dtb, alog, dsk, nw)
        s_scr[...] = s_out
        y_attn = _attn_part(cur_ref[:, 1024:1536], cur_ref[:, C_Q:C_Q + 512], cur_ref[:, C_K:C_K + 256],
                            pkv_ref[...] * nf, snk, _kv_mask(CHUNK, not_first, ATTN_STACK_FWD), ATTN_STACK_FWD)
        conv_out = _dwconv(_conf_glu(cur_ref[:, C_CA:C_CA + 1024], pcc_ref[...] * nf), dww, dwb, CONF_HALO)
        conv_ref[...] = conv_out
        y_conf = _conf_tail(conv_out, cur_ref[:, 1536:2048], lnw, lnb)
        xn_ref[...] = (x_ref[...] + _dot(y_ssd, wo_ref[0:1024, :]) + _dot(y_attn, wo_ref[1024:1536, :])
                       + _dot(y_conf, wo_ref[1536:2048, :]))

    row = lambda b, j: (b * nc + j, 0)
    grid = (nb, nc)
    return _pallas_with_side(
        body, side, *_grid_ends(grid), 17, 3, name=name, grid=grid,
        out_shape=[jax.ShapeDtypeStruct((t, D_MODEL), F32),
                   jax.ShapeDtypeStruct((nb * nc, SSD_HEADS * SSD_HEAD_DIM, SSD_STATE), F32),
                   jax.ShapeDtypeStruct((t, 512), F32)],
        in_specs=[pl.BlockSpec((CHUNK, D_MODEL), row), pl.BlockSpec((CHUNK, PROJ_W), row)]
                 + _halo_specs(nc, lambda j: j) + [_full((MIX_WIDTH, D_MODEL))] + _param_specs(),
        out_specs=[pl.BlockSpec((CHUNK, D_MODEL), row),
                   pl.BlockSpec((1, SSD_HEADS * SSD_HEAD_DIM, SSD_STATE), lambda b, j: (b * nc + j, 0, 0)),
                   pl.BlockSpec((CHUNK, 512), row)],
        scratch_shapes=[pltpu.VMEM((SSD_HEADS * SSD_HEAD_DIM, SSD_STATE), F32)],
        compiler_params=pltpu.CompilerParams(dimension_semantics=("arbitrary", "arbitrary"),
                                             vmem_limit_bytes=VMEM_LIMIT),
        args=(x, proj, proj, proj, proj, w_out, *params))


def _mixer_bwd(dxn, proj, s_all, conv_all, w_out, params, nb, name):
    t = dxn.shape[0]
    nc = t // nb // CHUNK
    n_prm = 11

    def body(dxn_ref, cur_ref, pxbc_ref, pkv_ref, pcc_ref, s_ref, conv_ref, wo_ref, *rest):
        prm = [r[...] for r in rest[:n_prm]]
        dproj_ref, ycat_ref = rest[n_prm:n_prm + 2]
        gprm = rest[n_prm + 2:2 * n_prm + 2]
        ds_scr, pend_xbc, pend_kv, pend_cc = rest[2 * n_prm + 2:]
        b, j = pl.program_id(0), pl.program_id(1)
        c = nc - 1 - j
        not_first = c > 0
        nf = not_first.astype(F32)

        @pl.when((b == 0) & (j == 0))
        def _():
            for r in gprm:
                r[...] = jnp.zeros_like(r)

        @pl.when(j == 0)
        def _():
            ds_scr[...] = jnp.zeros_like(ds_scr)
            pend_xbc[...] = jnp.zeros_like(pend_xbc)
            pend_kv[...] = jnp.zeros_like(pend_kv)
            pend_cc[...] = jnp.zeros_like(pend_cc)

        cw, cb, dtb, alog, dsk, nw, snk, dww, dwb, lnw, lnb = prm
        g_cw, g_cb, g_dtb, g_alog, g_dsk, g_nw, g_snk, g_dww, g_dwb, g_lnw, g_lnb = gprm
        dxn_v = dxn_ref[...]

        def add_tail(d_cur, pending):
            lead = jnp.zeros((CHUNK - pending.shape[0], pending.shape[1]), F32)
            return d_cur + jnp.concatenate([lead, pending], axis=0)

        y, vjp = jax.vjp(_conf_tail, conv_ref[...], cur_ref[:, 1536:2048], lnw, lnb)
        ycat_ref[:, 1536:2048] = y.astype(BF16)
        d_conv, dz, d_lnw, d_lnb = vjp(_dot_nt(dxn_v, wo_ref[1536:2048, :]))
        ext, vjp = jax.vjp(_conf_glu, cur_ref[:, C_CA:C_CA + 1024], pcc_ref[...] * nf)
        d_ext, d_dww, d_dwb = _dwconv_bwd(CONF_HALO, (ext, dww), d_conv)
        dcacg, dpcc = vjp(d_ext)
        dproj_ref[:, 1536:2048] = dz.astype(BF16)
        dproj_ref[:, C_CA:C_CA + 1024] = add_tail(dcacg, pend_cc[...]).astype(BF16)
        pend_cc[...] = dpcc
        for r, g in ((g_dww, d_dww), (g_dwb, d_dwb), (g_lnw, d_lnw), (g_lnb, d_lnb)):
            r[...] += g

        attn = functools.partial(_attn_part, kvmask=_kv_mask(CHUNK, not_first, ATTN_STACK_BWD),
                                 stack=ATTN_STACK_BWD)
        y, vjp = jax.vjp(attn, cur_ref[:, 1024:1536], cur_ref[:, C_Q:C_Q + 512], cur_ref[:, C_K:C_K + 256],
                         pkv_ref[...] * nf, snk)
        ycat_ref[:, 1024:1536] = y.astype(BF16)
        dz, dq, dkv, dpkv, d_snk = vjp(_dot_nt(dxn_v, wo_ref[1024:1536, :]))
        dproj_ref[:, 1024:1536] = dz.astype(BF16)
        dproj_ref[:, C_Q:C_Q + 512] = dq.astype(BF16)
        dproj_ref[:, C_K:C_K + 256] = (dkv + pend_kv[...]).astype(BF16)
        pend_kv[...] = dpkv
        g_snk[...] += d_snk

        ext = jnp.concatenate([pxbc_ref[...] * nf, cur_ref[:, C_XBC:C_XBC + 1536]], axis=0)
        (y, _), vjp = jax.vjp(_ssd_part, cur_ref[:, 0:1024], _dwconv(ext, cw, cb, XBC_HALO),
                              cur_ref[:, C_DT:C_DT + 128], s_ref[0], dtb, alog, dsk, nw)
        ycat_ref[:, 0:1024] = y.astype(BF16)
        dz, d_conv, ddtr, ds_in, d_dtb, d_alog, d_dsk, d_nw = vjp((_dot_nt(dxn_v, wo_ref[0:1024, :]), ds_scr[...]))
        d_ext, d_cw, d_cb = _dwconv_bwd(XBC_HALO, (ext, cw), d_conv)
        dpxbc, dxbc = d_ext[0:XBC_HALO, :], d_ext[XBC_HALO:, :]
        dproj_ref[:, 0:1024] = dz.astype(BF16)
        dproj_ref[:, C_XBC:C_XBC + 1536] = add_tail(dxbc, pend_xbc[...]).astype(BF16)
        dproj_ref[:, C_DT:C_DT + 128] = ddtr.astype(BF16)
        dproj_ref[:, C_DT + 128:PROJ_W] = jnp.zeros((CHUNK, PROJ_W - C_DT - 128), BF16)
        pend_xbc[...] = dpxbc
        ds_scr[...] = ds_in
        for r, g in ((g_cw, d_cw), (g_cb, d_cb), (g_dtb, d_dtb), (g_alog, d_alog), (g_dsk, d_dsk), (g_nw, d_nw)):
            r[...] += g

    row = lambda b, j: (b * nc + nc - 1 - j, 0)
    prm_shapes = [(4, 1536), (1, 1536), (1, 128), (1, 128), (1, 128), (1, 1024), (1, 128), (CONF_KERNEL, 512),
                  (1, 512), (1, 512), (1, 512)]
    grid = (nb, nc)
    return _pallas_with_side(
        body, None, *_grid_ends(grid), 8 + n_prm, 2 + n_prm, name=name, grid=grid,
        out_shape=[jax.ShapeDtypeStruct((t, PROJ_W), BF16), jax.ShapeDtypeStruct((t, MIX_WIDTH), BF16)]
                  + [jax.ShapeDtypeStruct(s, F32) for s in prm_shapes],
        in_specs=[pl.BlockSpec((CHUNK, D_MODEL), row), pl.BlockSpec((CHUNK, PROJ_W), row)]
                 + _halo_specs(nc, lambda j: nc - 1 - j)
                 + [pl.BlockSpec((1, SSD_HEADS * SSD_HEAD_DIM, SSD_STATE), lambda b, j: (b * nc + nc - 1 - j, 0, 0)),
                    pl.BlockSpec((CHUNK, 512), row), _full((MIX_WIDTH, D_MODEL))] + _param_specs(),
        out_specs=[pl.BlockSpec((CHUNK, PROJ_W), row), pl.BlockSpec((CHUNK, MIX_WIDTH), row)]
                  + [_full(s) for s in prm_shapes],
        scratch_shapes=[pltpu.VMEM((SSD_HEADS * SSD_HEAD_DIM, SSD_STATE), F32), pltpu.VMEM((XBC_HALO, 1536), F32),
                        pltpu.VMEM((CHUNK, 256), F32), pltpu.VMEM((CONF_HALO, 1024), F32)],
        compiler_params=pltpu.CompilerParams(dimension_semantics=("arbitrary", "arbitrary"),
                                             vmem_limit_bytes=VMEM_LIMIT),
        args=(dxn, proj, proj, proj, proj, s_all, conv_all, w_out, *params))


def _gw_out(y_cat, dxn, name):
    t = y_cat.shape[0]
    tk = min(t, 2048)

    def body(y_ref, dxn_ref, out_ref, acc):
        k = pl.program_id(0)

        @pl.when(k == 0)
        def _():
            acc[...] = jnp.zeros_like(acc)

        acc[...] += _dot_tn(y_ref[...], dxn_ref[...])

        @pl.when(k == t // tk - 1)
        def _():
            out_ref[...] = acc[...].astype(BF16)

    out = pl.pallas_call(
        body, name=name, grid=(t // tk,),
        out_shape=jax.ShapeDtypeStruct((MIX_WIDTH, D_MODEL), BF16),
        in_specs=[pl.BlockSpec((tk, MIX_WIDTH), lambda k: (k, 0)), pl.BlockSpec((tk, D_MODEL), lambda k: (k, 0))],
        out_specs=_full((MIX_WIDTH, D_MODEL)),
        scratch_shapes=[pltpu.VMEM((MIX_WIDTH, D_MODEL), F32)],
        compiler_params=pltpu.CompilerParams(dimension_semantics=("arbitrary",), vmem_limit_bytes=VMEM_LIMIT),
    )(y_cat, dxn)
    return out.reshape(N_DEV, MIX_WIDTH // N_DEV, D_MODEL)


def _inproj_bwd_x(dproj, w, x, nw, dxn, name, side=None):
    t = x.shape[0]
    tm = 512

    def body(dp_ref, w_ref, x_ref, nw_ref, dxn_ref, dx_ref, gnw_ref):
        @pl.when(pl.program_id(0) == 0)
        def _():
            gnw_ref[...] = jnp.zeros_like(gnw_ref)

        dh = jnp.zeros((tm, D_MODEL), F32)
        for j in range(N_COL_TILES):
            sl = slice(j * COL_TILE, (j + 1) * COL_TILE)
            dh = dh + _dot_nt(dp_ref[:, sl], w_ref[:, sl])
        _, vjp = jax.vjp(_rmsnorm, x_ref[...], nw_ref[...])
        dx, dnw = vjp(dh)
        dx_ref[...] = dxn_ref[...] + dx
        gnw_ref[...] += dnw

    tok = lambda i: (i, 0)
    grid = (t // tm,)
    return _pallas_with_side(
        body, side, *_grid_ends(grid), 5, 2, name=name, grid=grid,
        out_shape=[jax.ShapeDtypeStruct((t, D_MODEL), F32), jax.ShapeDtypeStruct((1, D_MODEL), F32)],
        in_specs=[pl.BlockSpec((tm, PROJ_W), tok), _full((D_MODEL, PROJ_W)), pl.BlockSpec((tm, D_MODEL), tok),
                  _full((1, D_MODEL)), pl.BlockSpec((tm, D_MODEL), tok)],
        out_specs=[pl.BlockSpec((tm, D_MODEL), tok), _full((1, D_MODEL))],
        compiler_params=pltpu.CompilerParams(dimension_semantics=("arbitrary",), vmem_limit_bytes=VMEM_LIMIT),
        args=(dproj, w, x, nw, dxn))


def _inproj_bwd_w(h, dproj, name, side=None):
    t = h.shape[0]
    tk = min(t, 4096)

    def body(h_ref, dp_ref, gw_ref):
        @pl.when(pl.program_id(1) == 0)
        def _():
            gw_ref[...] = jnp.zeros_like(gw_ref)

        gw_ref[...] += _dot_tn(h_ref[...], dp_ref[...])

    grid = (N_COL_TILES, t // tk)
    return _pallas_with_side(
        body, side, *_grid_ends(grid), 2, 1, name=name, grid=grid,
        out_shape=[jax.ShapeDtypeStruct((D_MODEL, PROJ_W), F32)],
        in_specs=[pl.BlockSpec((tk, D_MODEL), lambda n, k: (k, 0)), pl.BlockSpec((tk, COL_TILE), lambda n, k: (k, n))],
        out_specs=[pl.BlockSpec((D_MODEL, COL_TILE), lambda n, k: (0, n))],
        compiler_params=pltpu.CompilerParams(dimension_semantics=("arbitrary", "arbitrary"),
                                             vmem_limit_bytes=VMEM_LIMIT),
        args=(h, dproj))


def _repack_runs():
    pieces = ((0, 2048, C_Z), (2048, 3584, C_XBC), (3584, 3600, C_DT), (3600, 4368, C_Q), (4368, D_IN_PROJ, C_CA))
    per = D_IN_PROJ // N_DEV
    runs = []
    for j in range(N_DEV):
        lo, hi = per * j, per * (j + 1)
        for a, b, dst in pieces:
            s, e = max(lo, a), min(hi, b)
            if s < e:
                runs.append((j, s - lo, e - lo, dst + s - a))
    return runs


def _repack_w_in(g, name):
    tr = 256

    def body(g_ref, o_ref):
        for j, a, b, dst in _repack_runs():
            o_ref[:, dst:dst + b - a] = g_ref[j, :, a:b]
        o_ref[:, C_DT + 16:PROJ_W] = jnp.zeros((tr, PROJ_W - C_DT - 16), g.dtype)

    return pl.pallas_call(
        body, name=name, grid=(D_MODEL // tr,),
        out_shape=jax.ShapeDtypeStruct((D_MODEL, PROJ_W), g.dtype),
        in_specs=[pl.BlockSpec((N_DEV, tr, D_IN_PROJ // N_DEV), lambda i: (0, i, 0))],
        out_specs=pl.BlockSpec((tr, PROJ_W), lambda i: (i, 0)),
        compiler_params=pltpu.CompilerParams(dimension_semantics=("arbitrary",)),
    )(g)


def _unpack_gw_in(g, name):
    tr = 256

    def body(g_ref, o_ref):
        for j, a, b, dst in _repack_runs():
            o_ref[j, :, a:b] = g_ref[:, dst:dst + b - a].astype(BF16)

    return pl.pallas_call(
        body, name=name, grid=(D_MODEL // tr,),
        out_shape=jax.ShapeDtypeStruct((N_DEV, D_MODEL, D_IN_PROJ // N_DEV), BF16),
        in_specs=[pl.BlockSpec((tr, PROJ_W), lambda i: (i, 0))],
        out_specs=pl.BlockSpec((N_DEV, tr, D_IN_PROJ // N_DEV), lambda i: (0, i, 0)),
        compiler_params=pltpu.CompilerParams(dimension_semantics=("arbitrary",)),
    )(g)


def _pair_sum(parts, name):
    n_dev, r, cdim = parts.shape
    n_chip = n_dev // 2
    by_chip = parts.reshape(n_chip, 2, r, cdim)

    def swap_body(p_ref, got_ref, send_sem, recv_sem):
        x, y, c = _my_place()
        cp = pltpu.make_async_remote_copy(
            src_ref=p_ref.at[:, pl.ds(1 - c, 1)], dst_ref=got_ref, send_sem=send_sem, recv_sem=recv_sem,
            device_id=(x, y, 1 - c), device_id_type=pl.DeviceIdType.MESH)
        cp.start()
        cp.wait()

    any_spec = pl.BlockSpec(memory_space=pl.ANY)
    got = pl.pallas_call(
        swap_body, name=name + "_swap",
        out_shape=jax.ShapeDtypeStruct((n_chip, 1, r, cdim), parts.dtype),
        in_specs=[any_spec], out_specs=any_spec,
        scratch_shapes=[pltpu.SemaphoreType.DMA, pltpu.SemaphoreType.DMA],
    )(by_chip)

    tr = 256

    def add_body(core_ref, p_ref, got_ref, o_ref):
        o_ref[0] = (p_ref[0, 0].astype(F32) + got_ref[0, 0].astype(F32)).astype(o_ref.dtype)

    return pl.pallas_call(
        add_body, name=name + "_add",
        grid_spec=pltpu.PrefetchScalarGridSpec(
            num_scalar_prefetch=1, grid=(n_chip, r // tr),
            in_specs=[pl.BlockSpec((1, 1, tr, cdim), lambda k, i, core: (k, core[0], i, 0)),
                      pl.BlockSpec((1, 1, tr, cdim), lambda k, i, core: (k, 0, i, 0))],
            out_specs=pl.BlockSpec((1, tr, cdim), lambda k, i, core: (k, i, 0))),
        out_shape=jax.ShapeDtypeStruct((n_chip, r, cdim), parts.dtype),
        compiler_params=pltpu.CompilerParams(dimension_semantics=("arbitrary", "arbitrary")),
    )(lax.axis_index("c").astype(jnp.int32).reshape(1), by_chip, got)


def _loss_head(x, fnw, target, name):
    t = x.shape[0]
    tm = 512

    def body(x_ref, w_ref, t_ref, dx_ref, loss_ref, gw_ref):
        @pl.when(pl.program_id(0) == 0)
        def _():
            loss_ref[...] = jnp.zeros_like(loss_ref)
            gw_ref[...] = jnp.zeros_like(gw_ref)

        y, vjp = jax.vjp(_rmsnorm, x_ref[...], w_ref[...])
        err = y - t_ref[...]
        loss_ref[...] += 0.5 * jnp.sum(jnp.mean(err * err, axis=-1, keepdims=True), axis=0, keepdims=True)
        dx, dw = vjp(err * (1.0 / D_MODEL))
        dx_ref[...] = dx
        gw_ref[...] += dw

    tok = lambda i: (i, 0)
    return pl.pallas_call(
        body, name=name, grid=(t // tm,),
        out_shape=[jax.ShapeDtypeStruct((t, D_MODEL), F32), jax.ShapeDtypeStruct((1, 1), F32),
                   jax.ShapeDtypeStruct((1, D_MODEL), F32)],
        in_specs=[pl.BlockSpec((tm, D_MODEL), tok), _full((1, D_MODEL)), pl.BlockSpec((tm, D_MODEL), tok)],
        out_specs=[pl.BlockSpec((tm, D_MODEL), tok), _full((1, 1)), _full((1, D_MODEL))],
        compiler_params=pltpu.CompilerParams(dimension_semantics=("arbitrary",)),
    )(x, fnw, target)


def _adamw(w, g, m, v):
    m = ADAM_B1 * m + (1.0 - ADAM_B1) * g
    v = ADAM_B2 * v + (1.0 - ADAM_B2) * jnp.square(g)
    m_hat = m / (1.0 - ADAM_B1 ** ADAM_STEP)
    v_hat = v / (1.0 - ADAM_B2 ** ADAM_STEP)
    delta = -ADAM_LR * (m_hat / (jnp.sqrt(v_hat) + ADAM_EPS) + ADAM_WD * w)
    return delta, m, v


def _reduce_adamw(parts, w, m, v, tr, name):
    depth = len(parts)
    p, r, cdim = parts[0].shape
    n_blk = r // tr

    def body(*refs):
        p_refs = refs[:depth]
        w_ref, m_ref, v_ref, g_ref, d_ref, nm_ref, nv_ref = refs[depth:]
        for layer in range(depth):
            @pl.when(pl.program_id(0) == layer)
            def _(p_ref=p_refs[layer]):
                g = p_ref[0].astype(F32)
                for i in range(1, p):
                    g = g + p_ref[i].astype(F32)
                g_ref[0] = g
                d_ref[0], nm_ref[0], nv_ref[0] = _adamw(w_ref[0], g, m_ref[0], v_ref[0])

    def parts_spec(layer):
        return pl.BlockSpec((p, tr, cdim), lambda d, i: (0, jnp.clip(i + (d - layer) * n_blk, 0, n_blk - 1), 0))

    blk = pl.BlockSpec((1, tr, cdim), lambda d, i: (d, i, 0))
    return pl.pallas_call(
        body, name=name, grid=(depth, n_blk),
        out_shape=[jax.ShapeDtypeStruct(w.shape, F32)] * 4,
        in_specs=[parts_spec(layer) for layer in range(depth)] + [blk, blk, blk],
        out_specs=[blk] * 4,
        compiler_params=pltpu.CompilerParams(dimension_semantics=("arbitrary", "arbitrary"),
                                             vmem_limit_bytes=VMEM_LIMIT),
    )(*parts, w, m, v)


def _adamw_small(ssum, entries, name):
    direct = [e[3] for e in entries if not isinstance(e[3], list)]
    n_direct = len(direct)

    def body(*refs):
        ssum_ref, direct_refs = refs[0], list(refs[1:1 + n_direct])
        ins = refs[1 + n_direct:1 + n_direct + 3 * len(entries)]
        outs = refs[1 + n_direct + 3 * len(entries):]
        for k, (w, _, _, grad) in enumerate(entries):
            w_ref, m_ref, v_ref = ins[3 * k:3 * k + 3]
            g_ref, d_ref, nm_ref, nv_ref = outs[4 * k:4 * k + 4]
            if isinstance(grad, list):
                for row, off in enumerate(grad):
                    rows = slice(row, row + 1)
                    g = ssum_ref[:, off:off + w.shape[1]]
                    g_ref[rows, :] = g
                    d_ref[rows, :], nm_ref[rows, :], nv_ref[rows, :] = _adamw(w_ref[rows, :], g, m_ref[rows, :],
                                                                              v_ref[rows, :])
            else:
                g = direct_refs.pop(0)[...]
                g_ref[...] = g
                d_ref[...], nm_ref[...], nv_ref[...] = _adamw(w_ref[...], g, m_ref[...], v_ref[...])

    vmem = pl.BlockSpec(memory_space=pltpu.VMEM)
    args = [ssum] + direct + [a for e in entries for a in e[:3]]
    res = pl.pallas_call(
        body, name=name,
        out_shape=[jax.ShapeDtypeStruct(e[0].shape, F32) for e in entries for _ in range(4)],
        in_specs=[vmem] * len(args), out_specs=[vmem] * (4 * len(entries)),
    )(*args)
    return [res[4 * k:4 * k + 4] for k in range(len(entries))]


def _reduce_adamw_cols(parts, w, m, v, name):
    depth = len(parts)
    _, r, cdim = parts[0].shape
    tc = 512

    def body(*refs):
        p_refs = refs[:depth]
        w_ref, m_ref, v_ref, g_ref, d_ref, nm_ref, nv_ref = refs[depth:]
        for layer in range(depth):
            g = p_refs[layer][0].astype(F32)
            for i in range(1, parts[layer].shape[0]):
                g = g + p_refs[layer][i].astype(F32)
            g = g.T
            g_ref[:, layer, :] = g
            d_ref[:, layer, :], nm_ref[:, layer, :], nv_ref[:, layer, :] = _adamw(
                w_ref[:, layer, :], g, m_ref[:, layer, :], v_ref[:, layer, :])

    view = lambda a: jnp.transpose(a, (2, 0, 1))
    blk = pl.BlockSpec((cdim, depth, tc), lambda i: (0, 0, i))
    outs = pl.pallas_call(
        body, name=name, grid=(r // tc,),
        out_shape=[jax.ShapeDtypeStruct((cdim, depth, r), F32)] * 4,
        in_specs=[pl.BlockSpec((a.shape[0], tc, cdim), lambda i: (0, i, 0)) for a in parts] + [blk, blk, blk],
        out_specs=[blk] * 4,
        compiler_params=pltpu.CompilerParams(dimension_semantics=("arbitrary",), vmem_limit_bytes=VMEM_LIMIT),
    )(*parts, view(w), view(m), view(v))
    return [jnp.transpose(o, (1, 2, 0)) for o in outs]


def _pad_lanes(v, width=128):
    return jnp.pad(v.reshape(1, -1), ((0, 0), (0, width - v.shape[-1])))


SMALL_FIELDS = (("norm_w", 1024), ("conv_b", 1536), ("dt_bias", 128), ("a_log", 128), ("d_skip", 128),
                ("ssd_norm_w", 1024), ("sinks", 128), ("dw_b", 512), ("ln_w", 512), ("ln_b", 512))


def kernel(x, norm_w, w_in, ssd_conv_w, ssd_conv_b, ssd_dt_bias, ssd_a_log, ssd_d, ssd_norm_w, attn_sinks, conf_dw_w, conf_dw_b, conf_ln_w, conf_ln_b, w_out, final_norm_w, loss_target, m_norm_w, m_w_in, m_ssd_conv_w, m_ssd_conv_b, m_ssd_dt_bias, m_ssd_a_log, m_ssd_d, m_ssd_norm_w, m_attn_sinks, m_conf_dw_w, m_conf_dw_b, m_conf_ln_w, m_conf_ln_b, m_w_out, m_final_norm_w, v_norm_w, v_w_in, v_ssd_conv_w, v_ssd_conv_b, v_ssd_dt_bias, v_ssd_a_log, v_ssd_d, v_ssd_norm_w, v_attn_sinks, v_conf_dw_w, v_conf_dw_b, v_conf_ln_w, v_conf_ln_b, v_w_out, v_final_norm_w):
    nb, seq, _ = x.shape
    depth = norm_w.shape[0]
    t = nb * seq
    me_idx = 4 * lax.axis_index("x") + 2 * lax.axis_index("y") + lax.axis_index("c")

    w_in_bf, w_out_bf = w_in.astype(BF16), w_out.astype(BF16)
    g_win0, g_cw, g_dw = _all_gather([w_in_bf[0], ssd_conv_w, conf_dw_w], "gather_weights")
    w_in_full = [_repack_w_in(g_win0, "repack_w_in_0")]
    w_out_full = []
    conv_w_full = [jnp.transpose(g_cw[:, l], (1, 0, 2)).reshape(4, 1536) for l in range(depth)]
    dw_w_full = [jnp.transpose(g_dw[:, l], (1, 0, 2)).reshape(CONF_KERNEL, 512) for l in range(depth)]

    def layer_params(l):
        return [conv_w_full[l], ssd_conv_b[l].reshape(1, -1), _pad_lanes(ssd_dt_bias[l]), _pad_lanes(ssd_a_log[l]),
                _pad_lanes(ssd_d[l]), ssd_norm_w[l].reshape(1, -1), _pad_lanes(attn_sinks[l]), dw_w_full[l],
                conf_dw_b[l].reshape(1, -1), conf_ln_w[l].reshape(1, -1), conf_ln_b[l].reshape(1, -1)]

    xs = [x.reshape(t, D_MODEL)]
    saved = []
    for l in range(depth):
        (proj, h), gathered = _inproj_fwd(xs[l], norm_w[l].reshape(1, -1), w_in_full[l], f"inproj_fwd_{l}",
                                          (GATHER, [w_out_bf[l]]))
        w_out_full.append(gathered[0].reshape(MIX_WIDTH, D_MODEL))
        side = (GATHER, [w_in_bf[l + 1]]) if l + 1 < depth else None
        (x_next, *kept), gathered = _mixer_fwd(xs[l], proj, w_out_full[l], layer_params(l), nb, f"mixer_fwd_{l}",
                                               side)
        if side:
            w_in_full.append(_repack_w_in(gathered[0], f"repack_w_in_{l + 1}"))
        saved.append((proj, h, kept))
        xs.append(x_next)
    dx, loss_part, g_fnw = _loss_head(xs[depth], final_norm_w.reshape(1, -1), loss_target.reshape(t, D_MODEL),
                                      "loss_head")

    cols_in = D_IN_PROJ // N_DEV
    rows_out = MIX_WIDTH // N_DEV
    small_rows = [None] * depth
    received = [None] * depth
    for l in reversed(range(depth)):
        proj, h, kept = saved[l]
        res, _ = _mixer_bwd(dx, proj, *kept, w_out_full[l], layer_params(l), nb, f"mixer_bwd_{l}")
        dproj, y_cat = res[0], res[1]
        g_cw_l, g_cb, g_dtb, g_alog, g_dsk, g_nw, g_snk, g_dww, g_dwb, g_lnw, g_lnb = res[2:]
        gw_out_parts = _gw_out(y_cat, dx, f"gw_out_{l}")
        (gw_in,), got_out = _inproj_bwd_w(h, dproj, f"inproj_bwd_w_{l}", (SCATTER, [gw_out_parts]))
        chip_parts = _pair_sum(_unpack_gw_in(gw_in, f"unpack_gw_in_{l}"), f"pair_sum_{l}")
        (dx, g_norm), got_in = _inproj_bwd_x(dproj, w_in_full[l], xs[l], norm_w[l].reshape(1, -1), dx,
                                             f"inproj_bwd_x_{l}", (CHIP_SCATTER, [chip_parts]))
        received[l] = [got_in[0], got_out[0]]
        small_rows[l] = [g_norm, g_cb, g_dtb, g_alog, g_dsk, g_nw, g_snk, g_dwb, g_lnw, g_lnb,
                         g_cw_l.reshape(1, -1), g_dww.reshape(1, -1)]
    grad_x = dx.reshape(nb, seq, D_MODEL)

    small = jnp.concatenate([piece for l in range(depth) for piece in small_rows[l]]
                            + [g_fnw, _pad_lanes(loss_part)], axis=1)
    ssum = _exchange_small(small, "exchange_small")
    loss = ssum[0, small.shape[1] - 128]

    g_w_in, d_w_in, nm_w_in, nv_w_in = _reduce_adamw_cols([received[l][0] for l in range(depth)], w_in, m_w_in,
                                                          v_w_in, "adamw_w_in")
    g_w_out, d_w_out, nm_w_out, nv_w_out = _reduce_adamw([received[l][1] for l in range(depth)], w_out, m_w_out,
                                                         v_w_out, 256, "adamw_w_out")

    per_layer = sum(n for _, n in SMALL_FIELDS) + 4 * 1536 + CONF_KERNEL * 512
    given = {"norm_w": (norm_w, m_norm_w, v_norm_w), "conv_b": (ssd_conv_b, m_ssd_conv_b, v_ssd_conv_b),
             "dt_bias": (ssd_dt_bias, m_ssd_dt_bias, v_ssd_dt_bias), "a_log": (ssd_a_log, m_ssd_a_log, v_ssd_a_log),
             "d_skip": (ssd_d, m_ssd_d, v_ssd_d), "ssd_norm_w": (ssd_norm_w, m_ssd_norm_w, v_ssd_norm_w),
             "sinks": (attn_sinks, m_attn_sinks, v_attn_sinks), "dw_b": (conf_dw_b, m_conf_dw_b, v_conf_dw_b),
             "ln_w": (conf_ln_w, m_conf_ln_w, v_conf_ln_w), "ln_b": (conf_ln_b, m_conf_ln_b, v_conf_ln_b)}
    entries = []
    off = 0
    for fname, n in SMALL_FIELDS:
        entries.append((*given[fname], [l * per_layer + off for l in range(depth)]))
        off += n
    shard_grads = []
    for width, shard, kk in ((1536, 192, 4), (512, 64, CONF_KERNEL)):
        full = [ssum[:, l * per_layer + off:l * per_layer + off + kk * width].reshape(kk, width) for l in range(depth)]
        shard_grads.append(jnp.stack([lax.dynamic_slice(f, (0, me_idx * shard), (kk, shard)) for f in full], axis=0))
        off += kk * width
    entries.append((ssd_conv_w, m_ssd_conv_w, v_ssd_conv_w, shard_grads[0]))
    entries.append((conf_dw_w, m_conf_dw_w, v_conf_dw_w, shard_grads[1]))
    entries.append((final_norm_w.reshape(1, -1), m_final_norm_w.reshape(1, -1), v_final_norm_w.reshape(1, -1),
                    [depth * per_layer]))
    sm = _adamw_small(ssum, entries, "adamw_small")
    sm = {k: quad for k, quad in zip([f for f, _ in SMALL_FIELDS] + ["conv_w", "dw_w", "final"], sm)}

    def outputs(i, big_in_i, big_out_i):
        return [sm["norm_w"][i], big_in_i, sm["conv_w"][i], sm["conv_b"][i], sm["dt_bias"][i], sm["a_log"][i],
                sm["d_skip"][i], sm["ssd_norm_w"][i], sm["sinks"][i], sm["dw_w"][i], sm["dw_b"][i], sm["ln_w"][i],
                sm["ln_b"][i], big_out_i, sm["final"][i].reshape(-1)]

    return (loss, grad_x, *outputs(0, g_w_in, g_w_out), *outputs(1, d_w_in, d_w_out),
            *outputs(2, nm_w_in, nm_w_out), *outputs(3, nv_w_in, nv_w_out))
```

```python
import functools

import jax
import jax.numpy as jnp
from jax import lax
from jax.experimental import pallas as pl
from jax.experimental.pallas import tpu as pltpu

F32 = jnp.float32
BF16 = jnp.bfloat16
N_DEV = 8
EPS = 1e-5

D_MODEL = 1024
CHUNK = 128
SSD_HEADS = 16
SSD_HEAD_DIM = 64
SSD_STATE = 128
ATTN_HEADS = 8
ATTN_HEAD_DIM = 64
CONF_KERNEL = 31
MIX_WIDTH = 2048
D_IN_PROJ = 5392
C_Z = 0
C_CA = 2048
C_XBC = 3072
C_Q = 4608
C_K = 5120
C_V = 5248
C_DT = 5376
PROJ_W = 5632
N_COL_TILES = 4
COL_TILE = PROJ_W // N_COL_TILES
XBC_HALO = 8
CONF_HALO = 32
ATTN_STACK_FWD = 2
ATTN_STACK_BWD = 4
VMEM_LIMIT = 56 * 1024 * 1024

ADAM_LR = 0.001
ADAM_B1 = 0.9
ADAM_B2 = 0.999
ADAM_EPS = 1e-08
ADAM_WD = 0.01
ADAM_STEP = 10


def _silu(v):
    return v * jax.nn.sigmoid(v)


def _softplus(v):
    return jnp.maximum(v, 0.0) + jnp.log1p(jnp.exp(-jnp.abs(v)))


def _rmsnorm(v, w):
    return v * lax.rsqrt(jnp.mean(v * v, axis=-1, keepdims=True) + EPS) * w


def _dot(a, b):
    return jnp.dot(a.astype(BF16), b.astype(BF16), preferred_element_type=F32)


def _dot_nt(a, b):
    return lax.dot_general(a.astype(BF16), b.astype(BF16), (((1,), (1,)), ((), ())), preferred_element_type=F32)


def _dot_tn(a, b):
    return lax.dot_general(a.astype(BF16), b.astype(BF16), (((0,), (0,)), ((), ())), preferred_element_type=F32)


def _taps(ext, offs, out_len, w=None, g=None):
    n_rows, n_cols = ext.shape
    by_shift = {}
    for t, off in enumerate(offs):
        by_shift.setdefault(off % 8, []).append((t, off))
    for r, taps in by_shift.items():
        assert max(off for _, off in taps) - r + out_len <= n_rows - r
    accs = []
    sums = [[None] * (n_cols // 128) for _ in offs]
    for blk in range(n_cols // 128):
        cs = slice(blk * 128, (blk + 1) * 128)
        e = ext[:, cs]
        acc = None
        for r, taps in by_shift.items():
            shifted = e if r == 0 else pltpu.roll(e, n_rows - r, axis=0)
            for t, off in taps:
                window = shifted[off - r:off - r + out_len, :]
                if w is not None:
                    term = w[t:t + 1, cs] * window
                    acc = term if acc is None else acc + term
                if g is not None:
                    sums[t][blk] = jnp.sum(g[:, cs] * window, axis=0, keepdims=True)
        accs.append(acc)
    if w is not None:
        return jnp.concatenate(accs, axis=1)
    return jnp.concatenate([jnp.concatenate(row, axis=1) for row in sums], axis=0)


@functools.partial(jax.custom_vjp, nondiff_argnums=(3,))
def _dwconv(ext, w, b, halo):
    kk = w.shape[0]
    return b + _taps(ext, [halo - (kk - 1) + t for t in range(kk)], ext.shape[0] - halo, w=w)


def _dwconv_fwd(ext, w, b, halo):
    return _dwconv(ext, w, b, halo), (ext, w)


def _dwconv_bwd(halo, res, g):
    ext, w = res
    kk = w.shape[0]
    offs = [halo - (kk - 1) + t for t in range(kk)]
    dw = _taps(ext, offs, ext.shape[0] - halo, g=g)
    zeros = jnp.zeros((halo, g.shape[1]), g.dtype)
    gp = jnp.concatenate([zeros, g, zeros], axis=0)
    dext = _taps(gp, [halo - off for off in offs], ext.shape[0], w=w)
    return dext, dw, jnp.sum(g, axis=0, keepdims=True)


_dwconv.defvjp(_dwconv_fwd, _dwconv_bwd)


def _ssd_part(z_ssd, conv_out, dtr, s_in, dtb, alog, dsk, nw):
    qn = conv_out.shape[0]
    nh = SSD_HEADS
    per_group = nh // 2
    n_pair = nh // 2
    xa = _silu(conv_out)
    xs = xa[:, 0:1024]
    dt = _softplus(dtr + dtb)
    a = dt * (-jnp.exp(alog))
    rows = lax.broadcasted_iota(jnp.int32, (qn, qn), 0)
    cols = lax.broadcasted_iota(jnp.int32, (qn, qn), 1)
    causal = rows >= cols
    low = cols < SSD_HEAD_DIM
    a_cs = jnp.dot(causal.astype(F32), a, precision=lax.Precision.HIGHEST, preferred_element_type=F32)
    a_cs_t = a_cs.T
    bgs = [xa[:, 1024 + g * 128:1024 + (g + 1) * 128] for g in range(2)]
    cgs = [xa[:, 1280 + g * 128:1280 + (g + 1) * 128] for g in range(2)]
    cbms = [_dot_nt(cgs[g], bgs[g]) for g in range(2)]
    colb = [jnp.broadcast_to(a_cs[:, h:h + 1], (qn, qn)) for h in range(nh)]
    lastb = [jnp.broadcast_to(colb[h][qn - 1:qn, :], (qn, qn)) for h in range(nh)]
    dtb_wide = [jnp.broadcast_to(dt[:, h:h + 1], (qn, qn)) for h in range(nh)]
    lmats = [jnp.exp(jnp.where(causal, colb[h] - a_cs_t[h:h + 1, :], -jnp.inf)) for h in range(nh)]
    ms = [cbms[h // per_group] * lmats[h] for h in range(nh)]
    x_pair = [xs[:, p * 128:(p + 1) * 128] for p in range(n_pair)]
    x_lo = [x_pair[p] * jnp.where(low, dtb_wide[2 * p], 0.0) for p in range(n_pair)]
    x_hi = [x_pair[p] * jnp.where(low, 0.0, dtb_wide[2 * p + 1]) for p in range(n_pair)]
    y_diag = [_dot(ms[2 * p], x_lo[p]) + _dot(ms[2 * p + 1], x_hi[p]) for p in range(n_pair)]
    col_pair = [jnp.where(low, colb[2 * p], colb[2 * p + 1]) for p in range(n_pair)]
    last_pair = [jnp.where(low, lastb[2 * p], lastb[2 * p + 1]) for p in range(n_pair)]
    ecol = [jnp.exp(col_pair[p]) for p in range(n_pair)]
    xw = [(x_lo[p] + x_hi[p]) * jnp.exp(last_pair[p] - col_pair[p]) for p in range(n_pair)]
    y_off, st = [], []
    for g in range(2):
        ps = range(g * n_pair // 2, (g + 1) * n_pair // 2)
        y_off.append(_dot_nt(cgs[g], s_in[g * 512:(g + 1) * 512, :]) * jnp.concatenate([ecol[p] for p in ps], axis=1))
        st.append(_dot_tn(jnp.concatenate([xw[p] for p in ps], axis=1), bgs[g]))
    e_last = jnp.exp(jnp.broadcast_to(a_cs_t[:, qn - 1:qn], (qn, SSD_STATE)))
    scale = jnp.concatenate([jnp.broadcast_to(e_last[h:h + 1, :], (64, SSD_STATE)) for h in range(nh)], axis=0)
    s_out = scale * s_in + jnp.concatenate(st, axis=0)
    d_wide = jnp.concatenate([jnp.broadcast_to(dsk[:, h:h + 1], (1, 64)) for h in range(nh)], axis=1)
    y = jnp.concatenate(y_diag, axis=1) + jnp.concatenate(y_off, axis=1) + d_wide * xs
    gated = y * _silu(z_ssd)
    halves = []
    for g in range(2):
        gg = gated[:, g * 512:(g + 1) * 512]
        halves.append(gg * lax.rsqrt(jnp.mean(gg * gg, axis=-1, keepdims=True) + EPS))
    return jnp.concatenate(halves, axis=1) * nw, s_out


def _attn_part(z_attn, q, kv, p_kv, snk, kvmask, stack):
    qn = q.shape[0]
    kk = jnp.concatenate([p_kv[:, 0:128], kv[:, 0:128]], axis=0)
    vv = jnp.concatenate([p_kv[:, 128:256], kv[:, 128:256]], axis=0)
    units = range(ATTN_HEADS // stack)
    heads = [range(u * stack, (u + 1) * stack) for u in units]
    kv_of = [u * stack // (ATTN_HEADS // 2) for u in units]
    k_of = [kk[:, g * 64:(g + 1) * 64] for g in kv_of]
    v_of = [vv[:, g * 64:(g + 1) * 64] for g in kv_of]
    qs = [jnp.concatenate([q[:, h * 64:(h + 1) * 64] for h in heads[u]], axis=0) for u in units]
    sk = [jnp.concatenate([jnp.broadcast_to(snk[:, h:h + 1], (qn, 1)) for h in heads[u]], axis=0) for u in units]
    s = [jnp.where(kvmask, _dot_nt(qs[u], k_of[u]) * (ATTN_HEAD_DIM ** -0.5), -jnp.inf) for u in units]
    m = [lax.stop_gradient(jnp.maximum(jnp.max(s[u], axis=-1, keepdims=True), sk[u])) for u in units]
    e = [jnp.exp(s[u] - m[u]) for u in units]
    r_den = [1.0 / (jnp.sum(e[u], axis=-1, keepdims=True) + jnp.exp(sk[u] - m[u])) for u in units]
    o = [_dot(e[u], v_of[u]) * r_den[u] for u in units]
    outs = [o[u][i * qn:(i + 1) * qn, :] for u in units for i in range(stack)]
    return jnp.concatenate(outs, axis=1) * _silu(z_attn)


def _conf_glu(cacg, p_cc):
    c0 = cacg[:, 0:512] * jax.nn.sigmoid(cacg[:, 512:1024])
    pc0 = p_cc[:, 0:512] * jax.nn.sigmoid(p_cc[:, 512:1024])
    return jnp.concatenate([pc0, c0], axis=0)


def _conf_tail(conv_out, z_conf, lnw, lnb):
    xc = conv_out - jnp.mean(conv_out, axis=-1, keepdims=True)
    yln = xc * lax.rsqrt(jnp.mean(xc * xc, axis=-1, keepdims=True) + EPS) * lnw + lnb
    return _silu(yln) * _silu(z_conf)


def _kv_mask(qn, not_first, reps):
    ii = lax.broadcasted_iota(jnp.int32, (reps * qn, 2 * qn), 0) & (qn - 1)
    jj = lax.broadcasted_iota(jnp.int32, (reps * qn, 2 * qn), 1)
    d = jj - ii
    return (d >= 1) & (d <= qn) & (not_first | (jj >= qn))


def _my_place():
    return lax.axis_index("x"), lax.axis_index("y"), lax.axis_index("c")


def _all_gather(arrs, name):
    n = len(arrs)

    def body(*refs):
        ins, outs = refs[:n], refs[n:2 * n]
        send_sems, recv_sems, local_sems = refs[2 * n:]
        x, y, c = _my_place()
        me, sibling = (x, y, c), (x, y, 1 - c)
        chips = [(1 - x, y), (x, 1 - y), (1 - x, 1 - y)]

        def slot(a, p):
            return outs[a].at[4 * p[0] + 2 * p[1] + p[2]]

        def copy(a, kk, block, to, src=None):
            return pltpu.make_async_remote_copy(
                src_ref=slot(a, block) if src is None else src, dst_ref=slot(a, block),
                send_sem=send_sems.at[a, kk], recv_sem=recv_sems.at[a, kk],
                device_id=to, device_id_type=pl.DeviceIdType.MESH)

        mine = [pltpu.make_async_copy(ins[a], slot(a, me), local_sems.at[a]) for a in range(n)]
        for cp in mine:
            cp.start()
        first = []
        for a in range(n):
            first.append(copy(a, 0, me, sibling, src=ins[a]))
            first += [copy(a, 1 + j, me, (*chip, c), src=ins[a]) for j, chip in enumerate(chips)]
        for cp in first:
            cp.start()
        passed = []
        for j, chip in enumerate(chips):
            for a in range(n):
                copy(a, 1 + j, (*chip, c), me).wait_recv()
                fwd = copy(a, 4 + j, (*chip, c), sibling)
                fwd.start()
                passed.append(fwd)
        for a in range(n):
            copy(a, 0, sibling, me).wait_recv()
            for j, chip in enumerate(chips):
                copy(a, 4 + j, (*chip, 1 - c), me).wait_recv()
        for cp in first + passed:
            cp.wait_send()
        for cp in mine:
            cp.wait()

    any_spec = pl.BlockSpec(memory_space=pl.ANY)
    return pl.pallas_call(
        body, name=name,
        out_shape=[jax.ShapeDtypeStruct((N_DEV,) + a.shape, a.dtype) for a in arrs],
        in_specs=[any_spec] * n, out_specs=[any_spec] * n,
        scratch_shapes=[pltpu.SemaphoreType.DMA((n, 7)), pltpu.SemaphoreType.DMA((n, 7)),
                        pltpu.SemaphoreType.DMA((n,))],
    )(*arrs)


GATHER, SCATTER, CHIP_SCATTER = "gather", "scatter", "chip_scatter"


def _direct_copies(mode, ins, outs, send_sems, recv_sems, local_sems):
    x, y, c = _my_place()
    by_chip = mode == CHIP_SCATTER
    place = (lambda px, py, pc: 2 * px + py) if by_chip else (lambda px, py, pc: 4 * px + 2 * py + pc)
    me_idx = place(x, y, c)
    n = len(ins)
    local = [pltpu.make_async_copy(ins[a] if mode == GATHER else ins[a].at[me_idx], outs[a].at[me_idx],
                                   local_sems.at[a]) for a in range(n)]
    remote = []
    for rel in range(1, N_DEV):
        if by_chip and rel & 1:
            continue
        px = 1 - x if rel & 4 else x
        py = 1 - y if rel & 2 else y
        pc = 1 - c if rel & 1 else c
        for a in range(n):
            remote.append(pltpu.make_async_remote_copy(
                src_ref=ins[a] if mode == GATHER else ins[a].at[place(px, py, pc)], dst_ref=outs[a].at[me_idx],
                send_sem=send_sems.at[a, rel - 1], recv_sem=recv_sems.at[a, rel - 1],
                device_id=(px, py, pc), device_id_type=pl.DeviceIdType.MESH))
    return local + remote


def _side_scratch(n):
    return [pltpu.SemaphoreType.DMA((n, 7)), pltpu.SemaphoreType.DMA((n, 7)), pltpu.SemaphoreType.DMA((n,))]


def _side_out_shapes(mode, arrs):
    return [jax.ShapeDtypeStruct((N_DEV,) + a.shape if mode == GATHER else a.shape, a.dtype) for a in arrs]


def _pallas_with_side(body, side, first, last, n_in, n_out, *, in_specs, out_specs, out_shape, scratch_shapes=(),
                      args, **kwargs):
    side_arrs = [] if side is None else list(side[1])
    ns = len(side_arrs)

    def wrapped(*refs):
        own_in, side_in = refs[:n_in], refs[n_in:n_in + ns]
        o = n_in + ns
        own_out, side_out = refs[o:o + n_out], refs[o + n_out:o + n_out + ns]
        scratch = refs[o + n_out + ns:]
        own_scratch, sems = (scratch[:-3], scratch[-3:]) if ns else (scratch, ())
        if ns:
            @pl.when(first())
            def _():
                for cp in _direct_copies(side[0], side_in, side_out, *sems):
                    cp.start()

        body(*own_in, *own_out, *own_scratch)
        if ns:
            @pl.when(last())
            def _():
                for cp in _direct_copies(side[0], side_in, side_out, *sems):
                    cp.wait()

    any_spec = pl.BlockSpec(memory_space=pl.ANY)
    res = pl.pallas_call(
        wrapped,
        in_specs=list(in_specs) + [any_spec] * ns,
        out_specs=list(out_specs) + [any_spec] * ns,
        out_shape=list(out_shape) + (_side_out_shapes(side[0], side_arrs) if ns else []),
        scratch_shapes=list(scratch_shapes) + (_side_scratch(ns) if ns else []),
        **kwargs,
    )(*args, *side_arrs)
    return res[:n_out], res[n_out:]


def _exchange_small(everyone, per_device, name):
    n_all, n_own = everyone.shape[1], per_device.shape[2]

    def body(all_ref, own_ref, all_sum, own_sum, all_slots, own_slots, *sems):
        copies = _direct_copies(GATHER, [all_ref], [all_slots], *sems[:3])
        copies += _direct_copies(SCATTER, [own_ref], [own_slots], *sems[3:])
        for cp in copies:
            cp.start()
        for cp in copies:
            cp.wait()
        for slots, out in ((all_slots, all_sum), (own_slots, own_sum)):
            total = slots[0]
            for i in range(1, N_DEV):
                total = total + slots[i]
            out[...] = total

    vmem_spec = pl.BlockSpec(memory_space=pltpu.VMEM)
    return pl.pallas_call(
        body, name=name,
        out_shape=[jax.ShapeDtypeStruct((1, n_all), F32), jax.ShapeDtypeStruct((1, n_own), F32)],
        in_specs=[vmem_spec, vmem_spec], out_specs=[vmem_spec, vmem_spec],
        scratch_shapes=[pltpu.VMEM((N_DEV, 1, n_all), F32), pltpu.VMEM((N_DEV, 1, n_own), F32)]
                       + _side_scratch(1) + _side_scratch(1),
    )(everyone, per_device)


def _full(shape):
    return pl.BlockSpec(shape, lambda *_: (0,) * len(shape))


def _inproj_fwd(x, nw, w, name, side=None):
    t = x.shape[0]
    tm = 256

    def body(x_ref, nw_ref, w_ref, proj_ref, h_ref):
        h = _rmsnorm(x_ref[...], nw_ref[...]).astype(BF16)
        h_ref[...] = h
        for j in range(N_COL_TILES):
            sl = slice(j * COL_TILE, (j + 1) * COL_TILE)
            proj_ref[:, sl] = jnp.dot(h, w_ref[:, sl], preferred_element_type=F32)

    grid = (t // tm,)
    return _pallas_with_side(
        body, side, *_grid_ends(grid), 3, 2, name=name, grid=grid,
        out_shape=[jax.ShapeDtypeStruct((t, PROJ_W), F32), jax.ShapeDtypeStruct((t, D_MODEL), BF16)],
        in_specs=[pl.BlockSpec((tm, D_MODEL), lambda i: (i, 0)), _full((1, D_MODEL)), _full((D_MODEL, PROJ_W))],
        out_specs=[pl.BlockSpec((tm, PROJ_W), lambda i: (i, 0)), pl.BlockSpec((tm, D_MODEL), lambda i: (i, 0))],
        compiler_params=pltpu.CompilerParams(dimension_semantics=("arbitrary",), vmem_limit_bytes=VMEM_LIMIT),
        args=(x, nw, w))


def _param_specs():
    return [_full((4, 1536)), _full((1, 1536)), _full((1, 128)), _full((1, 128)), _full((1, 128)),
            _full((1, 1024)), _full((1, 128)), _full((CONF_KERNEL, 512)), _full((1, 512)), _full((1, 512)),
            _full((1, 512))]


def _halo_specs(nc, chunk_of):
    def prev_chunk(b, j):
        return jnp.maximum(b * nc + chunk_of(j) - 1, 0)

    per_xbc = CHUNK // XBC_HALO
    per_cc = CHUNK // CONF_HALO
    return [
        pl.BlockSpec((XBC_HALO, 1536), lambda b, j: (prev_chunk(b, j) * per_xbc + per_xbc - 1, C_XBC // 1536)),
        pl.BlockSpec((CHUNK, 256), lambda b, j: (prev_chunk(b, j), C_K // 256)),
        pl.BlockSpec((CONF_HALO, 1024), lambda b, j: (prev_chunk(b, j) * per_cc + per_cc - 1, C_CA // 1024)),
    ]


def _grid_ends(grid):
    first = lambda: functools.reduce(lambda p, q: p & q, [pl.program_id(i) == 0 for i in range(len(grid))])
    last = lambda: functools.reduce(lambda p, q: p & q, [pl.program_id(i) == n - 1 for i, n in enumerate(grid)])
    return first, last


def _mixer_fwd(x, proj, w_out, params, nb, name, side=None):
    t = x.shape[0]
    nc = t // nb // CHUNK

    def body(x_ref, cur_ref, pxbc_ref, pkv_ref, pcc_ref, wo_ref, *rest):
        prm = [r[...] for r in rest[:11]]
        xn_ref, sall_ref, conv_ref, s_scr = rest[11:]
        c = pl.program_id(1)
        not_first = c > 0
        nf = not_first.astype(F32)

        @pl.when(c == 0)
        def _():
            s_scr[...] = jnp.zeros_like(s_scr)

        cw, cb, dtb, alog, dsk, nw, snk, dww, dwb, lnw, lnb = prm
        s_in = s_scr[...]
        sall_ref[0] = s_in
        ssd_conv = _dwconv(jnp.concatenate([pxbc_ref[...] * nf, cur_ref[:, C_XBC:C_XBC + 1536]], axis=0), cw, cb,
                           XBC_HALO)
        y_ssd, s_out = _ssd_part(cur_ref[:, 0:1024], ssd_conv, cur_ref[:, C_DT:C_DT + 128], s_in, dtb, alog, dsk, nw)
        s_scr[...] = s_out
        y_attn = _attn_part(cur_ref[:, 1024:1536], cur_ref[:, C_Q:C_Q + 512], cur_ref[:, C_K:C_K + 256],
                            pkv_ref[...] * nf, snk, _kv_mask(CHUNK, not_first, ATTN_STACK_FWD), ATTN_STACK_FWD)
        conv_out = _dwconv(_conf_glu(cur_ref[:, C_CA:C_CA + 1024], pcc_ref[...] * nf), dww, dwb, CONF_HALO)
        conv_ref[...] = conv_out
        y_conf = _conf_tail(conv_out, cur_ref[:, 1536:2048], lnw, lnb)
        xn_ref[...] = (x_ref[...] + _dot(y_ssd, wo_ref[0:1024, :]) + _dot(y_attn, wo_ref[1024:1536, :])
                       + _dot(y_conf, wo_ref[1536:2048, :]))

    row = lambda b, j: (b * nc + j, 0)
    grid = (nb, nc)
    return _pallas_with_side(
        body, side, *_grid_ends(grid), 17, 3, name=name, grid=grid,
        out_shape=[jax.ShapeDtypeStruct((t, D_MODEL), F32),
                   jax.ShapeDtypeStruct((nb * nc, SSD_HEADS * SSD_HEAD_DIM, SSD_STATE), F32),
                   jax.ShapeDtypeStruct((t, 512), F32)],
        in_specs=[pl.BlockSpec((CHUNK, D_MODEL), row), pl.BlockSpec((CHUNK, PROJ_W), row)]
                 + _halo_specs(nc, lambda j: j) + [_full((MIX_WIDTH, D_MODEL))] + _param_specs(),
        out_specs=[pl.BlockSpec((CHUNK, D_MODEL), row),
                   pl.BlockSpec((1, SSD_HEADS * SSD_HEAD_DIM, SSD_STATE), lambda b, j: (b * nc + j, 0, 0)),
                   pl.BlockSpec((CHUNK, 512), row)],
        scratch_shapes=[pltpu.VMEM((SSD_HEADS * SSD_HEAD_DIM, SSD_STATE), F32)],
        compiler_params=pltpu.CompilerParams(dimension_semantics=("arbitrary", "arbitrary"),
                                             vmem_limit_bytes=VMEM_LIMIT),
        args=(x, proj, proj, proj, proj, w_out, *params))


def _mixer_bwd(dxn, proj, s_all, conv_all, w_out, params, nb, name):
    t = dxn.shape[0]
    nc = t // nb // CHUNK
    n_prm = 11

    def body(dxn_ref, cur_ref, pxbc_ref, pkv_ref, pcc_ref, s_ref, conv_ref, wo_ref, *rest):
        prm = [r[...] for r in rest[:n_prm]]
        dproj_ref, ycat_ref = rest[n_prm:n_prm + 2]
        gprm = rest[n_prm + 2:2 * n_prm + 2]
        ds_scr, pend_xbc, pend_kv, pend_cc = rest[2 * n_prm + 2:]
        b, j = pl.program_id(0), pl.program_id(1)
        c = nc - 1 - j
        not_first = c > 0
        nf = not_first.astype(F32)

        @pl.when((b == 0) & (j == 0))
        def _():
            for r in gprm:
                r[...] = jnp.zeros_like(r)

        @pl.when(j == 0)
        def _():
            ds_scr[...] = jnp.zeros_like(ds_scr)
            pend_xbc[...] = jnp.zeros_like(pend_xbc)
            pend_kv[...] = jnp.zeros_like(pend_kv)
            pend_cc[...] = jnp.zeros_like(pend_cc)

        cw, cb, dtb, alog, dsk, nw, snk, dww, dwb, lnw, lnb = prm
        g_cw, g_cb, g_dtb, g_alog, g_dsk, g_nw, g_snk, g_dww, g_dwb, g_lnw, g_lnb = gprm
        dxn_v = dxn_ref[...]

        def add_tail(d_cur, pending):
            lead = jnp.zeros((CHUNK - pending.shape[0], pending.shape[1]), F32)
            return d_cur + jnp.concatenate([lead, pending], axis=0)

        y, vjp = jax.vjp(_conf_tail, conv_ref[...], cur_ref[:, 1536:2048], lnw, lnb)
        ycat_ref[:, 1536:2048] = y.astype(BF16)
        d_conv, dz, d_lnw, d_lnb = vjp(_dot_nt(dxn_v, wo_ref[1536:2048, :]))
        ext, vjp = jax.vjp(_conf_glu, cur_ref[:, C_CA:C_CA + 1024], pcc_ref[...] * nf)
        d_ext, d_dww, d_dwb = _dwconv_bwd(CONF_HALO, (ext, dww), d_conv)
        dcacg, dpcc = vjp(d_ext)
        dproj_ref[:, 1536:2048] = dz.astype(BF16)
        dproj_ref[:, C_CA:C_CA + 1024] = add_tail(dcacg, pend_cc[...]).astype(BF16)
        pend_cc[...] = dpcc
        for r, g in ((g_dww, d_dww), (g_dwb, d_dwb), (g_lnw, d_lnw), (g_lnb, d_lnb)):
            r[...] += g

        attn = functools.partial(_attn_part, kvmask=_kv_mask(CHUNK, not_first, ATTN_STACK_BWD),
                                 stack=ATTN_STACK_BWD)
        y, vjp = jax.vjp(attn, cur_ref[:, 1024:1536], cur_ref[:, C_Q:C_Q + 512], cur_ref[:, C_K:C_K + 256],
                         pkv_ref[...] * nf, snk)
        ycat_ref[:, 1024:1536] = y.astype(BF16)
        dz, dq, dkv, dpkv, d_snk = vjp(_dot_nt(dxn_v, wo_ref[1024:1536, :]))
        dproj_ref[:, 1024:1536] = dz.astype(BF16)
        dproj_ref[:, C_Q:C_Q + 512] = dq.astype(BF16)
        dproj_ref[:, C_K:C_K + 256] = (dkv + pend_kv[...]).astype(BF16)
        pend_kv[...] = dpkv
        g_snk[...] += d_snk

        ext = jnp.concatenate([pxbc_ref[...] * nf, cur_ref[:, C_XBC:C_XBC + 1536]], axis=0)
        (y, _), vjp = jax.vjp(_ssd_part, cur_ref[:, 0:1024], _dwconv(ext, cw, cb, XBC_HALO),
                              cur_ref[:, C_DT:C_DT + 128], s_ref[0], dtb, alog, dsk, nw)
        ycat_ref[:, 0:1024] = y.astype(BF16)
        dz, d_conv, ddtr, ds_in, d_dtb, d_alog, d_dsk, d_nw = vjp((_dot_nt(dxn_v, wo_ref[0:1024, :]), ds_scr[...]))
        d_ext, d_cw, d_cb = _dwconv_bwd(XBC_HALO, (ext, cw), d_conv)
        dpxbc, dxbc = d_ext[0:XBC_HALO, :], d_ext[XBC_HALO:, :]
        dproj_ref[:, 0:1024] = dz.astype(BF16)
        dproj_ref[:, C_XBC:C_XBC + 1536] = add_tail(dxbc, pend_xbc[...]).astype(BF16)
        dproj_ref[:, C_DT:C_DT + 128] = ddtr.astype(BF16)
        dproj_ref[:, C_DT + 128:PROJ_W] = jnp.zeros((CHUNK, PROJ_W - C_DT - 128), BF16)
        pend_xbc[...] = dpxbc
        ds_scr[...] = ds_in
        for r, g in ((g_cw, d_cw), (g_cb, d_cb), (g_dtb, d_dtb), (g_alog, d_alog), (g_dsk, d_dsk), (g_nw, d_nw)):
            r[...] += g

    row = lambda b, j: (b * nc + nc - 1 - j, 0)
    prm_shapes = [(4, 1536), (1, 1536), (1, 128), (1, 128), (1, 128), (1, 1024), (1, 128), (CONF_KERNEL, 512),
                  (1, 512), (1, 512), (1, 512)]
    grid = (nb, nc)
    return _pallas_with_side(
        body, None, *_grid_ends(grid), 8 + n_prm, 2 + n_prm, name=name, grid=grid,
        out_shape=[jax.ShapeDtypeStruct((t, PROJ_W), BF16), jax.ShapeDtypeStruct((t, MIX_WIDTH), BF16)]
                  + [jax.ShapeDtypeStruct(s, F32) for s in prm_shapes],
        in_specs=[pl.BlockSpec((CHUNK, D_MODEL), row), pl.BlockSpec((CHUNK, PROJ_W), row)]
                 + _halo_specs(nc, lambda j: nc - 1 - j)
                 + [pl.BlockSpec((1, SSD_HEADS * SSD_HEAD_DIM, SSD_STATE), lambda b, j: (b * nc + nc - 1 - j, 0, 0)),
                    pl.BlockSpec((CHUNK, 512), row), _full((MIX_WIDTH, D_MODEL))] + _param_specs(),
        out_specs=[pl.BlockSpec((CHUNK, PROJ_W), row), pl.BlockSpec((CHUNK, MIX_WIDTH), row)]
                  + [_full(s) for s in prm_shapes],
        scratch_shapes=[pltpu.VMEM((SSD_HEADS * SSD_HEAD_DIM, SSD_STATE), F32), pltpu.VMEM((XBC_HALO, 1536), F32),
                        pltpu.VMEM((CHUNK, 256), F32), pltpu.VMEM((CONF_HALO, 1024), F32)],
        compiler_params=pltpu.CompilerParams(dimension_semantics=("arbitrary", "arbitrary"),
                                             vmem_limit_bytes=VMEM_LIMIT),
        args=(dxn, proj, proj, proj, proj, s_all, conv_all, w_out, *params))


def _gw_out(y_cat, dxn, name):
    t = y_cat.shape[0]
    tk = 512

    def body(y_ref, dxn_ref, out_ref, acc):
        k = pl.program_id(0)

        @pl.when(k == 0)
        def _():
            acc[...] = jnp.zeros_like(acc)

        acc[...] += _dot_tn(y_ref[...], dxn_ref[...])

        @pl.when(k == t // tk - 1)
        def _():
            out_ref[...] = acc[...].astype(BF16)

    out = pl.pallas_call(
        body, name=name, grid=(t // tk,),
        out_shape=jax.ShapeDtypeStruct((MIX_WIDTH, D_MODEL), BF16),
        in_specs=[pl.BlockSpec((tk, MIX_WIDTH), lambda k: (k, 0)), pl.BlockSpec((tk, D_MODEL), lambda k: (k, 0))],
        out_specs=_full((MIX_WIDTH, D_MODEL)),
        scratch_shapes=[pltpu.VMEM((MIX_WIDTH, D_MODEL), F32)],
        compiler_params=pltpu.CompilerParams(dimension_semantics=("arbitrary",), vmem_limit_bytes=VMEM_LIMIT),
    )(y_cat, dxn)
    return out.reshape(N_DEV, MIX_WIDTH // N_DEV, D_MODEL)


def _inproj_bwd_x(dproj, w, x, nw, dxn, name, side=None):
    t = x.shape[0]
    tm = 256

    def body(dp_ref, w_ref, x_ref, nw_ref, dxn_ref, dx_ref, gnw_ref):
        @pl.when(pl.program_id(0) == 0)
        def _():
            gnw_ref[...] = jnp.zeros_like(gnw_ref)

        dh = jnp.zeros((tm, D_MODEL), F32)
        for j in range(N_COL_TILES):
            sl = slice(j * COL_TILE, (j + 1) * COL_TILE)
            dh = dh + _dot_nt(dp_ref[:, sl], w_ref[:, sl])
        _, vjp = jax.vjp(_rmsnorm, x_ref[...], nw_ref[...])
        dx, dnw = vjp(dh)
        dx_ref[...] = dxn_ref[...] + dx
        gnw_ref[...] += dnw

    tok = lambda i: (i, 0)
    grid = (t // tm,)
    return _pallas_with_side(
        body, side, *_grid_ends(grid), 5, 2, name=name, grid=grid,
        out_shape=[jax.ShapeDtypeStruct((t, D_MODEL), F32), jax.ShapeDtypeStruct((1, D_MODEL), F32)],
        in_specs=[pl.BlockSpec((tm, PROJ_W), tok), _full((D_MODEL, PROJ_W)), pl.BlockSpec((tm, D_MODEL), tok),
                  _full((1, D_MODEL)), pl.BlockSpec((tm, D_MODEL), tok)],
        out_specs=[pl.BlockSpec((tm, D_MODEL), tok), _full((1, D_MODEL))],
        compiler_params=pltpu.CompilerParams(dimension_semantics=("arbitrary",), vmem_limit_bytes=VMEM_LIMIT),
        args=(dproj, w, x, nw, dxn))


def _inproj_bwd_w(h, dproj, name, side=None):
    t = h.shape[0]
    tk = 512

    def body(h_ref, dp_ref, gw_ref):
        @pl.when(pl.program_id(1) == 0)
        def _():
            gw_ref[...] = jnp.zeros_like(gw_ref)

        gw_ref[...] += _dot_tn(h_ref[...], dp_ref[...])

    grid = (N_COL_TILES, t // tk)
    return _pallas_with_side(
        body, side, *_grid_ends(grid), 2, 1, name=name, grid=grid,
        out_shape=[jax.ShapeDtypeStruct((D_MODEL, PROJ_W), F32)],
        in_specs=[pl.BlockSpec((tk, D_MODEL), lambda n, k: (k, 0)), pl.BlockSpec((tk, COL_TILE), lambda n, k: (k, n))],
        out_specs=[pl.BlockSpec((D_MODEL, COL_TILE), lambda n, k: (0, n))],
        compiler_params=pltpu.CompilerParams(dimension_semantics=("arbitrary", "arbitrary"),
                                             vmem_limit_bytes=VMEM_LIMIT),
        args=(h, dproj))


def _repack_runs():
    pieces = ((0, 2048, C_Z), (2048, 3584, C_XBC), (3584, 3600, C_DT), (3600, 4368, C_Q), (4368, D_IN_PROJ, C_CA))
    per = D_IN_PROJ // N_DEV
    runs = []
    for j in range(N_DEV):
        lo, hi = per * j, per * (j + 1)
        for a, b, dst in pieces:
            s, e = max(lo, a), min(hi, b)
            if s < e:
                runs.append((j, s - lo, e - lo, dst + s - a))
    return runs


def _repack_w_in(g, name):
    tr = 256

    def body(g_ref, o_ref):
        for j, a, b, dst in _repack_runs():
            o_ref[:, dst:dst + b - a] = g_ref[j, :, a:b]
        o_ref[:, C_DT + 16:PROJ_W] = jnp.zeros((tr, PROJ_W - C_DT - 16), g.dtype)

    return pl.pallas_call(
        body, name=name, grid=(D_MODEL // tr,),
        out_shape=jax.ShapeDtypeStruct((D_MODEL, PROJ_W), g.dtype),
        in_specs=[pl.BlockSpec((N_DEV, tr, D_IN_PROJ // N_DEV), lambda i: (0, i, 0))],
        out_specs=pl.BlockSpec((tr, PROJ_W), lambda i: (i, 0)),
        compiler_params=pltpu.CompilerParams(dimension_semantics=("arbitrary",)),
    )(g)


def _unpack_gw_in(g, name):
    tr = 256

    def body(g_ref, o_ref):
        for j, a, b, dst in _repack_runs():
            o_ref[j, :, a:b] = g_ref[:, dst:dst + b - a].astype(BF16)

    return pl.pallas_call(
        body, name=name, grid=(D_MODEL // tr,),
        out_shape=jax.ShapeDtypeStruct((N_DEV, D_MODEL, D_IN_PROJ // N_DEV), BF16),
        in_specs=[pl.BlockSpec((tr, PROJ_W), lambda i: (i, 0))],
        out_specs=pl.BlockSpec((N_DEV, tr, D_IN_PROJ // N_DEV), lambda i: (0, i, 0)),
        compiler_params=pltpu.CompilerParams(dimension_semantics=("arbitrary",)),
    )(g)


def _pair_sum(parts, name):
    n_dev, r, cdim = parts.shape
    n_chip = n_dev // 2
    by_chip = parts.reshape(n_chip, 2, r, cdim)

    def swap_body(p_ref, got_ref, send_sem, recv_sem):
        x, y, c = _my_place()
        cp = pltpu.make_async_remote_copy(
            src_ref=p_ref.at[:, pl.ds(1 - c, 1)], dst_ref=got_ref, send_sem=send_sem, recv_sem=recv_sem,
            device_id=(x, y, 1 - c), device_id_type=pl.DeviceIdType.MESH)
        cp.start()
        cp.wait()

    any_spec = pl.BlockSpec(memory_space=pl.ANY)
    got = pl.pallas_call(
        swap_body, name=name + "_swap",
        out_shape=jax.ShapeDtypeStruct((n_chip, 1, r, cdim), parts.dtype),
        in_specs=[any_spec], out_specs=any_spec,
        scratch_shapes=[pltpu.SemaphoreType.DMA, pltpu.SemaphoreType.DMA],
    )(by_chip)

    tr = 256

    def add_body(core_ref, p_ref, got_ref, o_ref):
        o_ref[0] = (p_ref[0, 0].astype(F32) + got_ref[0, 0].astype(F32)).astype(o_ref.dtype)

    return pl.pallas_call(
        add_body, name=name + "_add",
        grid_spec=pltpu.PrefetchScalarGridSpec(
            num_scalar_prefetch=1, grid=(n_chip, r // tr),
            in_specs=[pl.BlockSpec((1, 1, tr, cdim), lambda k, i, core: (k, core[0], i, 0)),
                      pl.BlockSpec((1, 1, tr, cdim), lambda k, i, core: (k, 0, i, 0))],
            out_specs=pl.BlockSpec((1, tr, cdim), lambda k, i, core: (k, i, 0))),
        out_shape=jax.ShapeDtypeStruct((n_chip, r, cdim), parts.dtype),
        compiler_params=pltpu.CompilerParams(dimension_semantics=("arbitrary", "arbitrary")),
    )(lax.axis_index("c").astype(jnp.int32).reshape(1), by_chip, got)


def _loss_head(x, fnw, target, name):
    t = x.shape[0]
    tm = 512

    def body(x_ref, w_ref, t_ref, dx_ref, loss_ref, gw_ref):
        @pl.when(pl.program_id(0) == 0)
        def _():
            loss_ref[...] = jnp.zeros_like(loss_ref)
            gw_ref[...] = jnp.zeros_like(gw_ref)

        y, vjp = jax.vjp(_rmsnorm, x_ref[...], w_ref[...])
        err = y - t_ref[...]
        loss_ref[...] += 0.5 * jnp.sum(jnp.mean(err * err, axis=-1, keepdims=True), axis=0, keepdims=True)
        dx, dw = vjp(err * (1.0 / D_MODEL))
        dx_ref[...] = dx
        gw_ref[...] += dw

    tok = lambda i: (i, 0)
    return pl.pallas_call(
        body, name=name, grid=(t // tm,),
        out_shape=[jax.ShapeDtypeStruct((t, D_MODEL), F32), jax.ShapeDtypeStruct((1, 1), F32),
                   jax.ShapeDtypeStruct((1, D_MODEL), F32)],
        in_specs=[pl.BlockSpec((tm, D_MODEL), tok), _full((1, D_MODEL)), pl.BlockSpec((tm, D_MODEL), tok)],
        out_specs=[pl.BlockSpec((tm, D_MODEL), tok), _full((1, 1)), _full((1, D_MODEL))],
        compiler_params=pltpu.CompilerParams(dimension_semantics=("arbitrary",)),
    )(x, fnw, target)


def _adamw(w, g, m, v):
    m = ADAM_B1 * m + (1.0 - ADAM_B1) * g
    v = ADAM_B2 * v + (1.0 - ADAM_B2) * jnp.square(g)
    m_hat = m / (1.0 - ADAM_B1 ** ADAM_STEP)
    v_hat = v / (1.0 - ADAM_B2 ** ADAM_STEP)
    delta = -ADAM_LR * (m_hat / (jnp.sqrt(v_hat) + ADAM_EPS) + ADAM_WD * w)
    return delta, m, v


def _reduce_adamw(parts, w, m, v, tr, name):
    depth = len(parts)
    p, r, cdim = parts[0].shape
    n_blk = r // tr

    def body(*refs):
        p_refs = refs[:depth]
        w_ref, m_ref, v_ref, g_ref, d_ref, nm_ref, nv_ref = refs[depth:]
        for layer in range(depth):
            @pl.when(pl.program_id(0) == layer)
            def _(p_ref=p_refs[layer]):
                g = p_ref[0].astype(F32)
                for i in range(1, p):
                    g = g + p_ref[i].astype(F32)
                g_ref[0] = g
                d_ref[0], nm_ref[0], nv_ref[0] = _adamw(w_ref[0], g, m_ref[0], v_ref[0])

    def parts_spec(layer):
        return pl.BlockSpec((p, tr, cdim), lambda d, i: (0, jnp.clip(i + (d - layer) * n_blk, 0, n_blk - 1), 0))

    blk = pl.BlockSpec((1, tr, cdim), lambda d, i: (d, i, 0))
    return pl.pallas_call(
        body, name=name, grid=(depth, n_blk),
        out_shape=[jax.ShapeDtypeStruct(w.shape, F32)] * 4,
        in_specs=[parts_spec(layer) for layer in range(depth)] + [blk, blk, blk],
        out_specs=[blk] * 4,
        compiler_params=pltpu.CompilerParams(dimension_semantics=("arbitrary", "arbitrary"),
                                             vmem_limit_bytes=VMEM_LIMIT),
    )(*parts, w, m, v)


def _adamw_small(ssum, entries, name):
    direct = [e[3] for e in entries if not isinstance(e[3], list)]
    n_direct = len(direct)

    def body(*refs):
        ssum_ref, direct_refs = refs[0], list(refs[1:1 + n_direct])
        ins = refs[1 + n_direct:1 + n_direct + 3 * len(entries)]
        outs = refs[1 + n_direct + 3 * len(entries):]
        for k, (w, _, _, grad) in enumerate(entries):
            w_ref, m_ref, v_ref = ins[3 * k:3 * k + 3]
            g_ref, d_ref, nm_ref, nv_ref = outs[4 * k:4 * k + 4]
            if isinstance(grad, list):
                for row, off in enumerate(grad):
                    rows = slice(row, row + 1)
                    g = ssum_ref[:, off:off + w.shape[1]]
                    g_ref[rows, :] = g
                    d_ref[rows, :], nm_ref[rows, :], nv_ref[rows, :] = _adamw(w_ref[rows, :], g, m_ref[rows, :],
                                                                              v_ref[rows, :])
            else:
                g = direct_refs.pop(0)[...]
                g_ref[...] = g
                d_ref[...], nm_ref[...], nv_ref[...] = _adamw(w_ref[...], g, m_ref[...], v_ref[...])

    vmem = pl.BlockSpec(memory_space=pltpu.VMEM)
    args = [ssum] + direct + [a for e in entries for a in e[:3]]
    res = pl.pallas_call(
        body, name=name,
        out_shape=[jax.ShapeDtypeStruct(e[0].shape, F32) for e in entries for _ in range(4)],
        in_specs=[vmem] * len(args), out_specs=[vmem] * (4 * len(entries)),
    )(*args)
    return [res[4 * k:4 * k + 4] for k in range(len(entries))]


def _reduce_adamw_cols(parts, w, m, v, name):
    depth = len(parts)
    _, r, cdim = parts[0].shape
    tc = 512

    def body(*refs):
        p_refs = refs[:depth]
        w_ref, m_ref, v_ref, g_ref, d_ref, nm_ref, nv_ref = refs[depth:]
        for layer in range(depth):
            g = p_refs[layer][0].astype(F32)
            for i in range(1, parts[layer].shape[0]):
                g = g + p_refs[layer][i].astype(F32)
            g = g.T
            g_ref[:, layer, :] = g
            d_ref[:, layer, :], nm_ref[:, layer, :], nv_ref[:, layer, :] = _adamw(
                w_ref[:, layer, :], g, m_ref[:, layer, :], v_ref[:, layer, :])

    view = lambda a: jnp.transpose(a, (2, 0, 1))
    blk = pl.BlockSpec((cdim, depth, tc), lambda i: (0, 0, i))
    outs = pl.pallas_call(
        body, name=name, grid=(r // tc,),
        out_shape=[jax.ShapeDtypeStruct((cdim, depth, r), F32)] * 4,
        in_specs=[pl.BlockSpec((a.shape[0], tc, cdim), lambda i: (0, i, 0)) for a in parts] + [blk, blk, blk],
        out_specs=[blk] * 4,
        compiler_params=pltpu.CompilerParams(dimension_semantics=("arbitrary",), vmem_limit_bytes=VMEM_LIMIT),
    )(*parts, view(w), view(m), view(v))
    return [jnp.transpose(o, (1, 2, 0)) for o in outs]


def _pad_lanes(v, width=128):
    return jnp.pad(v.reshape(1, -1), ((0, 0), (0, width - v.shape[-1])))


SMALL_FIELDS = (("norm_w", 1024), ("conv_b", 1536), ("dt_bias", 128), ("a_log", 128), ("d_skip", 128),
                ("ssd_norm_w", 1024), ("sinks", 128), ("dw_b", 512), ("ln_w", 512), ("ln_b", 512))


def kernel(x, norm_w, w_in, ssd_conv_w, ssd_conv_b, ssd_dt_bias, ssd_a_log, ssd_d, ssd_norm_w, attn_sinks, conf_dw_w, conf_dw_b, conf_ln_w, conf_ln_b, w_out, final_norm_w, loss_target, m_norm_w, m_w_in, m_ssd_conv_w, m_ssd_conv_b, m_ssd_dt_bias, m_ssd_a_log, m_ssd_d, m_ssd_norm_w, m_attn_sinks, m_conf_dw_w, m_conf_dw_b, m_conf_ln_w, m_conf_ln_b, m_w_out, m_final_norm_w, v_norm_w, v_w_in, v_ssd_conv_w, v_ssd_conv_b, v_ssd_dt_bias, v_ssd_a_log, v_ssd_d, v_ssd_norm_w, v_attn_sinks, v_conf_dw_w, v_conf_dw_b, v_conf_ln_w, v_conf_ln_b, v_w_out, v_final_norm_w):
    nb, seq, _ = x.shape
    depth = norm_w.shape[0]
    t = nb * seq

    w_in_bf, w_out_bf = w_in.astype(BF16), w_out.astype(BF16)
    g_win0, g_cw, g_dw = _all_gather([w_in_bf[0], ssd_conv_w, conf_dw_w], "gather_weights")
    w_in_full = [_repack_w_in(g_win0, "repack_w_in_0")]
    w_out_full = []
    conv_w_full = [jnp.transpose(g_cw[:, l], (1, 0, 2)).reshape(4, 1536) for l in range(depth)]
    dw_w_full = [jnp.transpose(g_dw[:, l], (1, 0, 2)).reshape(CONF_KERNEL, 512) for l in range(depth)]

    def layer_params(l):
        return [conv_w_full[l], ssd_conv_b[l].reshape(1, -1), _pad_lanes(ssd_dt_bias[l]), _pad_lanes(ssd_a_log[l]),
                _pad_lanes(ssd_d[l]), ssd_norm_w[l].reshape(1, -1), _pad_lanes(attn_sinks[l]), dw_w_full[l],
                conf_dw_b[l].reshape(1, -1), conf_ln_w[l].reshape(1, -1), conf_ln_b[l].reshape(1, -1)]

    xs = [x.reshape(t, D_MODEL)]
    saved = []
    for l in range(depth):
        (proj, h), gathered = _inproj_fwd(xs[l], norm_w[l].reshape(1, -1), w_in_full[l], f"inproj_fwd_{l}",
                                          (GATHER, [w_out_bf[l]]))
        w_out_full.append(gathered[0].reshape(MIX_WIDTH, D_MODEL))
        side = (GATHER, [w_in_bf[l + 1]]) if l + 1 < depth else None
        (x_next, *kept), gathered = _mixer_fwd(xs[l], proj, w_out_full[l], layer_params(l), nb, f"mixer_fwd_{l}",
                                               side)
        if side:
            w_in_full.append(_repack_w_in(gathered[0], f"repack_w_in_{l + 1}"))
        saved.append((proj, h, kept))
        xs.append(x_next)
    dx, loss_part, g_fnw = _loss_head(xs[depth], final_norm_w.reshape(1, -1), loss_target.reshape(t, D_MODEL),
                                      "loss_head")

    cols_in = D_IN_PROJ // N_DEV
    rows_out = MIX_WIDTH // N_DEV
    small_rows, shard_rows = [None] * depth, [None] * depth
    received = [None] * depth
    for l in reversed(range(depth)):
        proj, h, kept = saved[l]
        res, _ = _mixer_bwd(dx, proj, *kept, w_out_full[l], layer_params(l), nb, f"mixer_bwd_{l}")
        dproj, y_cat = res[0], res[1]
        g_cw_l, g_cb, g_dtb, g_alog, g_dsk, g_nw, g_snk, g_dww, g_dwb, g_lnw, g_lnb = res[2:]
        gw_out_parts = _gw_out(y_cat, dx, f"gw_out_{l}")
        (gw_in,), got_out = _inproj_bwd_w(h, dproj, f"inproj_bwd_w_{l}", (SCATTER, [gw_out_parts]))
        chip_parts = _pair_sum(_unpack_gw_in(gw_in, f"unpack_gw_in_{l}"), f"pair_sum_{l}")
        (dx, g_norm), got_in = _inproj_bwd_x(dproj, w_in_full[l], xs[l], norm_w[l].reshape(1, -1), dx,
                                             f"inproj_bwd_x_{l}", (CHIP_SCATTER, [chip_parts]))
        received[l] = [got_in[0], got_out[0]]
        small_rows[l] = [g_norm, g_cb, g_dtb, g_alog, g_dsk, g_nw, g_snk, g_dwb, g_lnw, g_lnb]
        shard_rows[l] = [jnp.transpose(g.reshape(g.shape[0], N_DEV, -1), (1, 0, 2)).reshape(N_DEV, -1)
                         for g in (g_cw_l, g_dww)]
    grad_x = dx.reshape(nb, seq, D_MODEL)

    small = jnp.concatenate([piece for l in range(depth) for piece in small_rows[l]]
                            + [g_fnw, _pad_lanes(loss_part)], axis=1)
    shard_small = jnp.concatenate([piece for l in range(depth) for piece in shard_rows[l]], axis=1)
    ssum, shard_sum = _exchange_small(small, shard_small.reshape(N_DEV, 1, -1), "exchange_small")
    loss = ssum[0, small.shape[1] - 128]

    g_w_in, d_w_in, nm_w_in, nv_w_in = _reduce_adamw_cols([received[l][0] for l in range(depth)], w_in, m_w_in,
                                                          v_w_in, "adamw_w_in")
    g_w_out, d_w_out, nm_w_out, nv_w_out = _reduce_adamw([received[l][1] for l in range(depth)], w_out, m_w_out,
                                                         v_w_out, 256, "adamw_w_out")

    per_layer = sum(n for _, n in SMALL_FIELDS)
    given = {"norm_w": (norm_w, m_norm_w, v_norm_w), "conv_b": (ssd_conv_b, m_ssd_conv_b, v_ssd_conv_b),
             "dt_bias": (ssd_dt_bias, m_ssd_dt_bias, v_ssd_dt_bias), "a_log": (ssd_a_log, m_ssd_a_log, v_ssd_a_log),
             "d_skip": (ssd_d, m_ssd_d, v_ssd_d), "ssd_norm_w": (ssd_norm_w, m_ssd_norm_w, v_ssd_norm_w),
             "sinks": (attn_sinks, m_attn_sinks, v_attn_sinks), "dw_b": (conf_dw_b, m_conf_dw_b, v_conf_dw_b),
             "ln_w": (conf_ln_w, m_conf_ln_w, v_conf_ln_w), "ln_b": (conf_ln_b, m_conf_ln_b, v_conf_ln_b)}
    entries = []
    off = 0
    for fname, n in SMALL_FIELDS:
        entries.append((*given[fname], [l * per_layer + off for l in range(depth)]))
        off += n
    n_cw, n_dw = ssd_conv_w[0].size, conf_dw_w[0].size
    per_layer_shard = n_cw + n_dw
    cw_grad = jnp.stack([shard_sum[0, l * per_layer_shard:l * per_layer_shard + n_cw].reshape(ssd_conv_w.shape[1:])
                         for l in range(depth)], axis=0)
    dw_grad = jnp.stack([shard_sum[0, l * per_layer_shard + n_cw:(l + 1) * per_layer_shard]
                         .reshape(conf_dw_w.shape[1:]) for l in range(depth)], axis=0)
    entries.append((ssd_conv_w, m_ssd_conv_w, v_ssd_conv_w, cw_grad))
    entries.append((conf_dw_w, m_conf_dw_w, v_conf_dw_w, dw_grad))
    entries.append((final_norm_w.reshape(1, -1), m_final_norm_w.reshape(1, -1), v_final_norm_w.reshape(1, -1),
                    [depth * per_layer]))
    sm = _adamw_small(ssum, entries, "adamw_small")
    sm = {k: quad for k, quad in zip([f for f, _ in SMALL_FIELDS] + ["conv_w", "dw_w", "final"], sm)}

    def outputs(i, big_in_i, big_out_i):
        return [sm["norm_w"][i], big_in_i, sm["conv_w"][i], sm["conv_b"][i], sm["dt_bias"][i], sm["a_log"][i],
                sm["d_skip"][i], sm["ssd_norm_w"][i], sm["sinks"][i], sm["dw_w"][i], sm["dw_b"][i], sm["ln_w"][i],
                sm["ln_b"][i], big_out_i, sm["final"][i].reshape(-1)]

    return (loss, grad_x, *outputs(0, g_w_in, g_w_out), *outputs(1, d_w_in, d_w_out),
            *outputs(2, nm_w_in, nm_w_out), *outputs(3, nv_w_in, nv_w_out))
```

```python
import functools

import jax
import jax.numpy as jnp
from jax import lax
from jax.experimental import pallas as pl
from jax.experimental.pallas import tpu as pltpu

F32 = jnp.float32
BF16 = jnp.bfloat16
N_DEV = 8
EPS = 1e-5

D_MODEL = 1024
CHUNK = 128
SSD_HEADS = 16
SSD_HEAD_DIM = 64
SSD_STATE = 128
ATTN_HEADS = 8
ATTN_HEAD_DIM = 64
CONF_KERNEL = 31
MIX_WIDTH = 2048
D_IN_PROJ = 5392
C_Z = 0
C_CA = 2048
C_XBC = 3072
C_Q = 4608
C_K = 5120
C_DT = 5376
PROJ_W = 5632
N_COL_TILES = 4
COL_TILE = PROJ_W // N_COL_TILES
XBC_HALO = 8
CONF_HALO = 32
ATTN_STACK_FWD = 2
ATTN_STACK_BWD = 4
VMEM_LIMIT = 56 * 1024 * 1024

ADAM_LR = 0.001
ADAM_B1 = 0.9
ADAM_B2 = 0.999
ADAM_EPS = 1e-08
ADAM_WD = 0.01
ADAM_STEP = 10


def _silu(v):
    return v * jax.nn.sigmoid(v)


def _softplus(v):
    return jnp.maximum(v, 0.0) + jnp.log1p(jnp.exp(-jnp.abs(v)))


def _rmsnorm(v, w):
    return v * lax.rsqrt(jnp.mean(v * v, axis=-1, keepdims=True) + EPS) * w


def _dot(a, b):
    return jnp.dot(a.astype(BF16), b.astype(BF16), preferred_element_type=F32)


def _dot_nt(a, b):
    return lax.dot_general(a.astype(BF16), b.astype(BF16), (((1,), (1,)), ((), ())), preferred_element_type=F32)


def _dot_tn(a, b):
    return lax.dot_general(a.astype(BF16), b.astype(BF16), (((0,), (0,)), ((), ())), preferred_element_type=F32)


def _taps(ext, offs, out_len, w=None, g=None):
    n_rows, n_cols = ext.shape
    by_shift = {}
    for t, off in enumerate(offs):
        by_shift.setdefault(off % 8, []).append((t, off))
    for r, taps in by_shift.items():
        assert max(off for _, off in taps) - r + out_len <= n_rows - r
    accs = []
    sums = [[None] * (n_cols // 128) for _ in offs]
    for blk in range(n_cols // 128):
        cs = slice(blk * 128, (blk + 1) * 128)
        e = ext[:, cs]
        acc = None
        for r, taps in by_shift.items():
            shifted = e if r == 0 else pltpu.roll(e, n_rows - r, axis=0)
            for t, off in taps:
                window = shifted[off - r:off - r + out_len, :]
                if w is not None:
                    term = w[t:t + 1, cs] * window
                    acc = term if acc is None else acc + term
                if g is not None:
                    sums[t][blk] = jnp.sum(g[:, cs] * window, axis=0, keepdims=True)
        accs.append(acc)
    if w is not None:
        return jnp.concatenate(accs, axis=1)
    return jnp.concatenate([jnp.concatenate(row, axis=1) for row in sums], axis=0)


@functools.partial(jax.custom_vjp, nondiff_argnums=(3,))
def _dwconv(ext, w, b, halo):
    kk = w.shape[0]
    return b + _taps(ext, [halo - (kk - 1) + t for t in range(kk)], ext.shape[0] - halo, w=w)


def _dwconv_fwd(ext, w, b, halo):
    return _dwconv(ext, w, b, halo), (ext, w)


def _dwconv_bwd(halo, res, g):
    ext, w = res
    kk = w.shape[0]
    offs = [halo - (kk - 1) + t for t in range(kk)]
    dw = _taps(ext, offs, ext.shape[0] - halo, g=g)
    zeros = jnp.zeros((halo, g.shape[1]), g.dtype)
    gp = jnp.concatenate([zeros, g, zeros], axis=0)
    dext = _taps(gp, [halo - off for off in offs], ext.shape[0], w=w)
    return dext, dw, jnp.sum(g, axis=0, keepdims=True)


_dwconv.defvjp(_dwconv_fwd, _dwconv_bwd)


def _ssd_part(z_ssd, conv_out, dtr, s_in, dtb, alog, dsk, nw):
    qn = conv_out.shape[0]
    nh = SSD_HEADS
    per_group = nh // 2
    n_pair = nh // 2
    xa = _silu(conv_out)
    xs = xa[:, 0:1024]
    dt = _softplus(dtr + dtb)
    a = dt * (-jnp.exp(alog))
    rows = lax.broadcasted_iota(jnp.int32, (qn, qn), 0)
    cols = lax.broadcasted_iota(jnp.int32, (qn, qn), 1)
    causal = rows >= cols
    low = cols < SSD_HEAD_DIM
    a_cs = jnp.dot(causal.astype(F32), a, precision=lax.Precision.HIGHEST, preferred_element_type=F32)
    a_cs_t = a_cs.T
    bgs = [xa[:, 1024 + g * 128:1024 + (g + 1) * 128] for g in range(2)]
    cgs = [xa[:, 1280 + g * 128:1280 + (g + 1) * 128] for g in range(2)]
    cbms = [_dot_nt(cgs[g], bgs[g]) for g in range(2)]
    colb = [jnp.broadcast_to(a_cs[:, h:h + 1], (qn, qn)) for h in range(nh)]
    lastb = [jnp.broadcast_to(colb[h][qn - 1:qn, :], (qn, qn)) for h in range(nh)]
    dtb_wide = [jnp.broadcast_to(dt[:, h:h + 1], (qn, qn)) for h in range(nh)]
    lmats = [jnp.exp(jnp.where(causal, colb[h] - a_cs_t[h:h + 1, :], -jnp.inf)) for h in range(nh)]
    ms = [cbms[h // per_group] * lmats[h] for h in range(nh)]
    x_pair = [xs[:, p * 128:(p + 1) * 128] for p in range(n_pair)]
    x_lo = [x_pair[p] * jnp.where(low, dtb_wide[2 * p], 0.0) for p in range(n_pair)]
    x_hi = [x_pair[p] * jnp.where(low, 0.0, dtb_wide[2 * p + 1]) for p in range(n_pair)]
    y_diag = [_dot(ms[2 * p], x_lo[p]) + _dot(ms[2 * p + 1], x_hi[p]) for p in range(n_pair)]
    col_pair = [jnp.where(low, colb[2 * p], colb[2 * p + 1]) for p in range(n_pair)]
    last_pair = [jnp.where(low, lastb[2 * p], lastb[2 * p + 1]) for p in range(n_pair)]
    ecol = [jnp.exp(col_pair[p]) for p in range(n_pair)]
    xw = [(x_lo[p] + x_hi[p]) * jnp.exp(last_pair[p] - col_pair[p]) for p in range(n_pair)]
    y_off, st = [], []
    for g in range(2):
        ps = range(g * n_pair // 2, (g + 1) * n_pair // 2)
        y_off.append(_dot_nt(cgs[g], s_in[g * 512:(g + 1) * 512, :]) * jnp.concatenate([ecol[p] for p in ps], axis=1))
        st.append(_dot_tn(jnp.concatenate([xw[p] for p in ps], axis=1), bgs[g]))
    e_last = jnp.exp(jnp.broadcast_to(a_cs_t[:, qn - 1:qn], (qn, SSD_STATE)))
    scale = jnp.concatenate([jnp.broadcast_to(e_last[h:h + 1, :], (64, SSD_STATE)) for h in range(nh)], axis=0)
    s_out = scale * s_in + jnp.concatenate(st, axis=0)
    d_wide = jnp.concatenate([jnp.broadcast_to(dsk[:, h:h + 1], (1, 64)) for h in range(nh)], axis=1)
    y = jnp.concatenate(y_diag, axis=1) + jnp.concatenate(y_off, axis=1) + d_wide * xs
    gated = y * _silu(z_ssd)
    halves = []
    for g in range(2):
        gg = gated[:, g * 512:(g + 1) * 512]
        halves.append(gg * lax.rsqrt(jnp.mean(gg * gg, axis=-1, keepdims=True) + EPS))
    return jnp.concatenate(halves, axis=1) * nw, s_out


def _attn_part(z_attn, q, kv, p_kv, snk, kvmask, stack):
    qn = q.shape[0]
    kk = jnp.concatenate([p_kv[:, 0:128], kv[:, 0:128]], axis=0)
    vv = jnp.concatenate([p_kv[:, 128:256], kv[:, 128:256]], axis=0)
    units = range(ATTN_HEADS // stack)
    heads = [range(u * stack, (u + 1) * stack) for u in units]
    kv_of = [u * stack // (ATTN_HEADS // 2) for u in units]
    k_of = [kk[:, g * 64:(g + 1) * 64] for g in kv_of]
    v_of = [vv[:, g * 64:(g + 1) * 64] for g in kv_of]
    qs = [jnp.concatenate([q[:, h * 64:(h + 1) * 64] for h in heads[u]], axis=0) for u in units]
    sk = [jnp.concatenate([jnp.broadcast_to(snk[:, h:h + 1], (qn, 1)) for h in heads[u]], axis=0) for u in units]
    s = [jnp.where(kvmask, _dot_nt(qs[u], k_of[u]) * (ATTN_HEAD_DIM ** -0.5), -jnp.inf) for u in units]
    m = [lax.stop_gradient(jnp.maximum(jnp.max(s[u], axis=-1, keepdims=True), sk[u])) for u in units]
    e = [jnp.exp(s[u] - m[u]) for u in units]
    r_den = [1.0 / (jnp.sum(e[u], axis=-1, keepdims=True) + jnp.exp(sk[u] - m[u])) for u in units]
    o = [_dot(e[u], v_of[u]) * r_den[u] for u in units]
    outs = [o[u][i * qn:(i + 1) * qn, :] for u in units for i in range(stack)]
    return jnp.concatenate(outs, axis=1) * _silu(z_attn)


def _conf_glu(cacg, p_cc):
    c0 = cacg[:, 0:512] * jax.nn.sigmoid(cacg[:, 512:1024])
    pc0 = p_cc[:, 0:512] * jax.nn.sigmoid(p_cc[:, 512:1024])
    return jnp.concatenate([pc0, c0], axis=0)


def _conf_tail(conv_out, z_conf, lnw, lnb):
    xc = conv_out - jnp.mean(conv_out, axis=-1, keepdims=True)
    yln = xc * lax.rsqrt(jnp.mean(xc * xc, axis=-1, keepdims=True) + EPS) * lnw + lnb
    return _silu(yln) * _silu(z_conf)


def _kv_mask(qn, not_first, reps):
    ii = lax.broadcasted_iota(jnp.int32, (reps * qn, 2 * qn), 0) & (qn - 1)
    jj = lax.broadcasted_iota(jnp.int32, (reps * qn, 2 * qn), 1)
    d = jj - ii
    return (d >= 1) & (d <= qn) & (not_first | (jj >= qn))


def _my_place():
    return lax.axis_index("x"), lax.axis_index("y"), lax.axis_index("c")


def _all_gather(arrs, name):
    n = len(arrs)

    def body(*refs):
        ins, outs = refs[:n], refs[n:2 * n]
        send_sems, recv_sems, local_sems = refs[2 * n:]
        x, y, c = _my_place()
        me, sibling = (x, y, c), (x, y, 1 - c)
        chips = [(1 - x, y), (x, 1 - y), (1 - x, 1 - y)]

        def slot(a, p):
            return outs[a].at[4 * p[0] + 2 * p[1] + p[2]]

        def copy(a, kk, block, to, src=None):
            return pltpu.make_async_remote_copy(
                src_ref=slot(a, block) if src is None else src, dst_ref=slot(a, block),
                send_sem=send_sems.at[a, kk], recv_sem=recv_sems.at[a, kk],
                device_id=to, device_id_type=pl.DeviceIdType.MESH)

        mine = [pltpu.make_async_copy(ins[a], slot(a, me), local_sems.at[a]) for a in range(n)]
        for cp in mine:
            cp.start()
        first = []
        for a in range(n):
            first.append(copy(a, 0, me, sibling, src=ins[a]))
            first += [copy(a, 1 + j, me, (*chip, c), src=ins[a]) for j, chip in enumerate(chips)]
        for cp in first:
            cp.start()
        passed = []
        for j, chip in enumerate(chips):
            for a in range(n):
                copy(a, 1 + j, (*chip, c), me).wait_recv()
                fwd = copy(a, 4 + j, (*chip, c), sibling)
                fwd.start()
                passed.append(fwd)
        for a in range(n):
            copy(a, 0, sibling, me).wait_recv()
            for j, chip in enumerate(chips):
                copy(a, 4 + j, (*chip, 1 - c), me).wait_recv()
        for cp in first + passed:
            cp.wait_send()
        for cp in mine:
            cp.wait()

    any_spec = pl.BlockSpec(memory_space=pl.ANY)
    return pl.pallas_call(
        body, name=name,
        out_shape=[jax.ShapeDtypeStruct((N_DEV,) + a.shape, a.dtype) for a in arrs],
        in_specs=[any_spec] * n, out_specs=[any_spec] * n,
        scratch_shapes=[pltpu.SemaphoreType.DMA((n, 7)), pltpu.SemaphoreType.DMA((n, 7)),
                        pltpu.SemaphoreType.DMA((n,))],
    )(*arrs)


GATHER, SCATTER, CHIP_SCATTER = "gather", "scatter", "chip_scatter"


def _direct_copies(mode, ins, outs, send_sems, recv_sems, local_sems):
    x, y, c = _my_place()
    by_chip = mode == CHIP_SCATTER
    place = (lambda px, py, pc: 2 * px + py) if by_chip else (lambda px, py, pc: 4 * px + 2 * py + pc)
    me_idx = place(x, y, c)
    n = len(ins)
    local = [pltpu.make_async_copy(ins[a] if mode == GATHER else ins[a].at[me_idx], outs[a].at[me_idx],
                                   local_sems.at[a]) for a in range(n)]
    remote = []
    for rel in range(1, N_DEV):
        if by_chip and rel & 1:
            continue
        px = 1 - x if rel & 4 else x
        py = 1 - y if rel & 2 else y
        pc = 1 - c if rel & 1 else c
        for a in range(n):
            remote.append(pltpu.make_async_remote_copy(
                src_ref=ins[a] if mode == GATHER else ins[a].at[place(px, py, pc)], dst_ref=outs[a].at[me_idx],
                send_sem=send_sems.at[a, rel - 1], recv_sem=recv_sems.at[a, rel - 1],
                device_id=(px, py, pc), device_id_type=pl.DeviceIdType.MESH))
    return local + remote


def _side_scratch(n):
    return [pltpu.SemaphoreType.DMA((n, 7)), pltpu.SemaphoreType.DMA((n, 7)), pltpu.SemaphoreType.DMA((n,))]


def _side_out_shapes(mode, arrs):
    return [jax.ShapeDtypeStruct((N_DEV,) + a.shape if mode == GATHER else a.shape, a.dtype) for a in arrs]


def _pallas_with_side(body, side, first, last, n_in, n_out, *, in_specs, out_specs, out_shape, scratch_shapes=(),
                      args, **kwargs):
    side_arrs = [] if side is None else list(side[1])
    ns = len(side_arrs)

    def wrapped(*refs):
        own_in, side_in = refs[:n_in], refs[n_in:n_in + ns]
        o = n_in + ns
        own_out, side_out = refs[o:o + n_out], refs[o + n_out:o + n_out + ns]
        scratch = refs[o + n_out + ns:]
        own_scratch, sems = (scratch[:-3], scratch[-3:]) if ns else (scratch, ())
        if ns:
            @pl.when(first())
            def _():
                for cp in _direct_copies(side[0], side_in, side_out, *sems):
                    cp.start()

        body(*own_in, *own_out, *own_scratch)
        if ns:
            @pl.when(last())
            def _():
                for cp in _direct_copies(side[0], side_in, side_out, *sems):
                    cp.wait()

    any_spec = pl.BlockSpec(memory_space=pl.ANY)
    res = pl.pallas_call(
        wrapped,
        in_specs=list(in_specs) + [any_spec] * ns,
        out_specs=list(out_specs) + [any_spec] * ns,
        out_shape=list(out_shape) + (_side_out_shapes(side[0], side_arrs) if ns else []),
        scratch_shapes=list(scratch_shapes) + (_side_scratch(ns) if ns else []),
        **kwargs,
    )(*args, *side_arrs)
    return res[:n_out], res[n_out:]


def _exchange_small(everyone, per_device, name):
    n_all, n_own = everyone.shape[1], per_device.shape[2]

    def body(all_ref, own_ref, all_sum, own_sum, all_slots, own_slots, *sems):
        copies = _direct_copies(GATHER, [all_ref], [all_slots], *sems[:3])
        copies += _direct_copies(SCATTER, [own_ref], [own_slots], *sems[3:])
        for cp in copies:
            cp.start()
        for cp in copies:
            cp.wait()
        for slots, out in ((all_slots, all_sum), (own_slots, own_sum)):
            total = slots[0]
            for i in range(1, N_DEV):
                total = total + slots[i]
            out[...] = total

    vmem_spec = pl.BlockSpec(memory_space=pltpu.VMEM)
    return pl.pallas_call(
        body, name=name,
        out_shape=[jax.ShapeDtypeStruct((1, n_all), F32), jax.ShapeDtypeStruct((1, n_own), F32)],
        in_specs=[vmem_spec, vmem_spec], out_specs=[vmem_spec, vmem_spec],
        scratch_shapes=[pltpu.VMEM((N_DEV, 1, n_all), F32), pltpu.VMEM((N_DEV, 1, n_own), F32)]
                       + _side_scratch(1) + _side_scratch(1),
    )(everyone, per_device)


def _full(shape):
    return pl.BlockSpec(shape, lambda *_: (0,) * len(shape))


def _inproj_fwd(x, nw, w, name, side=None):
    t = x.shape[0]
    tm = 256

    def body(x_ref, nw_ref, w_ref, proj_ref, h_ref):
        h = _rmsnorm(x_ref[...], nw_ref[...]).astype(BF16)
        h_ref[...] = h
        for j in range(N_COL_TILES):
            sl = slice(j * COL_TILE, (j + 1) * COL_TILE)
            proj_ref[:, sl] = jnp.dot(h, w_ref[:, sl], preferred_element_type=F32)

    grid = (t // tm,)
    return _pallas_with_side(
        body, side, *_grid_ends(grid), 3, 2, name=name, grid=grid,
        out_shape=[jax.ShapeDtypeStruct((t, PROJ_W), F32), jax.ShapeDtypeStruct((t, D_MODEL), BF16)],
        in_specs=[pl.BlockSpec((tm, D_MODEL), lambda i: (i, 0)), _full((1, D_MODEL)), _full((D_MODEL, PROJ_W))],
        out_specs=[pl.BlockSpec((tm, PROJ_W), lambda i: (i, 0)), pl.BlockSpec((tm, D_MODEL), lambda i: (i, 0))],
        compiler_params=pltpu.CompilerParams(dimension_semantics=("arbitrary",), vmem_limit_bytes=VMEM_LIMIT),
        args=(x, nw, w))


def _param_specs():
    return [_full((4, 1536)), _full((1, 1536)), _full((1, 128)), _full((1, 128)), _full((1, 128)),
            _full((1, 1024)), _full((1, 128)), _full((CONF_KERNEL, 512)), _full((1, 512)), _full((1, 512)),
            _full((1, 512))]


def _halo_specs(nc, chunk_of):
    def prev_chunk(b, j):
        return jnp.maximum(b * nc + chunk_of(j) - 1, 0)

    per_xbc = CHUNK // XBC_HALO
    per_cc = CHUNK // CONF_HALO
    return [
        pl.BlockSpec((XBC_HALO, 1536), lambda b, j: (prev_chunk(b, j) * per_xbc + per_xbc - 1, C_XBC // 1536)),
        pl.BlockSpec((CHUNK, 256), lambda b, j: (prev_chunk(b, j), C_K // 256)),
        pl.BlockSpec((CONF_HALO, 1024), lambda b, j: (prev_chunk(b, j) * per_cc + per_cc - 1, C_CA // 1024)),
    ]


def _grid_ends(grid):
    first = lambda: functools.reduce(lambda p, q: p & q, [pl.program_id(i) == 0 for i in range(len(grid))])
    last = lambda: functools.reduce(lambda p, q: p & q, [pl.program_id(i) == n - 1 for i, n in enumerate(grid)])
    return first, last


def _mixer_fwd(x, proj, w_out, params, nb, name, side=None, head=None):
    t = x.shape[0]
    nc = t // nb // CHUNK
    n_head = 0 if head is None else 2

    def body(x_ref, cur_ref, pxbc_ref, pkv_ref, pcc_ref, wo_ref, *rest):
        prm = [r[...] for r in rest[:11]]
        head_in = rest[11:11 + n_head]
        xn_ref, sall_ref, conv_ref = rest[11 + n_head:14 + n_head]
        head_out = rest[14 + n_head:14 + 2 * n_head]
        s_scr = rest[14 + 2 * n_head]
        c = pl.program_id(1)
        not_first = c > 0
        nf = not_first.astype(F32)

        @pl.when(c == 0)
        def _():
            s_scr[...] = jnp.zeros_like(s_scr)

        cw, cb, dtb, alog, dsk, nw, snk, dww, dwb, lnw, lnb = prm
        s_in = s_scr[...]
        sall_ref[0] = s_in
        ssd_conv = _dwconv(jnp.concatenate([pxbc_ref[...] * nf, cur_ref[:, C_XBC:C_XBC + 1536]], axis=0), cw, cb,
                           XBC_HALO)
        y_ssd, s_out = _ssd_part(cur_ref[:, 0:1024], ssd_conv, cur_ref[:, C_DT:C_DT + 128], s_in, dtb, alog, dsk, nw)
        s_scr[...] = s_out
        y_attn = _attn_part(cur_ref[:, 1024:1536], cur_ref[:, C_Q:C_Q + 512], cur_ref[:, C_K:C_K + 256],
                            pkv_ref[...] * nf, snk, _kv_mask(CHUNK, not_first, ATTN_STACK_FWD), ATTN_STACK_FWD)
        conv_out = _dwconv(_conf_glu(cur_ref[:, C_CA:C_CA + 1024], pcc_ref[...] * nf), dww, dwb, CONF_HALO)
        conv_ref[...] = conv_out
        y_conf = _conf_tail(conv_out, cur_ref[:, 1536:2048], lnw, lnb)
        x_next = (x_ref[...] + _dot(y_ssd, wo_ref[0:1024, :]) + _dot(y_attn, wo_ref[1024:1536, :])
                  + _dot(y_conf, wo_ref[1536:2048, :]))
        if head is None:
            xn_ref[...] = x_next
        else:
            fnw_ref, tgt_ref = head_in
            loss_ref, gfnw_ref = head_out

            @pl.when((pl.program_id(0) == 0) & (c == 0))
            def _():
                loss_ref[...] = jnp.zeros_like(loss_ref)
                gfnw_ref[...] = jnp.zeros_like(gfnw_ref)

            y, vjp = jax.vjp(_rmsnorm, x_next, fnw_ref[...])
            err = y - tgt_ref[...]
            loss_ref[...] += 0.5 * jnp.sum(jnp.mean(err * err, axis=-1, keepdims=True), axis=0, keepdims=True)
            xn_ref[...], d_fnw = vjp(err * (1.0 / D_MODEL))
            gfnw_ref[...] += d_fnw

    row = lambda b, j: (b * nc + j, 0)
    grid = (nb, nc)
    return _pallas_with_side(
        body, side, *_grid_ends(grid), 17 + n_head, 3 + n_head, name=name, grid=grid,
        out_shape=[jax.ShapeDtypeStruct((t, D_MODEL), F32),
                   jax.ShapeDtypeStruct((nb * nc, SSD_HEADS * SSD_HEAD_DIM, SSD_STATE), F32),
                   jax.ShapeDtypeStruct((t, 512), F32)]
                  + [jax.ShapeDtypeStruct(s, F32) for s in ((1, 1), (1, D_MODEL))[:n_head]],
        in_specs=[pl.BlockSpec((CHUNK, D_MODEL), row), pl.BlockSpec((CHUNK, PROJ_W), row)]
                 + _halo_specs(nc, lambda j: j) + [_full((MIX_WIDTH, D_MODEL))] + _param_specs()
                 + [_full((1, D_MODEL)), pl.BlockSpec((CHUNK, D_MODEL), row)][:n_head],
        out_specs=[pl.BlockSpec((CHUNK, D_MODEL), row),
                   pl.BlockSpec((1, SSD_HEADS * SSD_HEAD_DIM, SSD_STATE), lambda b, j: (b * nc + j, 0, 0)),
                   pl.BlockSpec((CHUNK, 512), row)] + [_full((1, 1)), _full((1, D_MODEL))][:n_head],
        scratch_shapes=[pltpu.VMEM((SSD_HEADS * SSD_HEAD_DIM, SSD_STATE), F32)],
        compiler_params=pltpu.CompilerParams(dimension_semantics=("arbitrary", "arbitrary"),
                                             vmem_limit_bytes=VMEM_LIMIT),
        args=(x, proj, proj, proj, proj, w_out, *params, *(head or ())))


def _mixer_bwd(dxn, proj, s_all, conv_all, w_out, params, nb, name):
    t = dxn.shape[0]
    nc = t // nb // CHUNK
    n_prm = 11

    def body(dxn_ref, cur_ref, pxbc_ref, pkv_ref, pcc_ref, s_ref, conv_ref, wo_ref, *rest):
        prm = [r[...] for r in rest[:n_prm]]
        dproj_ref, ycat_ref = rest[n_prm:n_prm + 2]
        gprm = rest[n_prm + 2:2 * n_prm + 2]
        ds_scr, pend_xbc, pend_kv, pend_cc = rest[2 * n_prm + 2:]
        b, j = pl.program_id(0), pl.program_id(1)
        c = nc - 1 - j
        not_first = c > 0
        nf = not_first.astype(F32)

        @pl.when((b == 0) & (j == 0))
        def _():
            for r in gprm:
                r[...] = jnp.zeros_like(r)

        @pl.when(j == 0)
        def _():
            ds_scr[...] = jnp.zeros_like(ds_scr)
            pend_xbc[...] = jnp.zeros_like(pend_xbc)
            pend_kv[...] = jnp.zeros_like(pend_kv)
            pend_cc[...] = jnp.zeros_like(pend_cc)

        cw, cb, dtb, alog, dsk, nw, snk, dww, dwb, lnw, lnb = prm
        g_cw, g_cb, g_dtb, g_alog, g_dsk, g_nw, g_snk, g_dww, g_dwb, g_lnw, g_lnb = gprm
        dxn_v = dxn_ref[...]

        def add_tail(d_cur, pending):
            lead = jnp.zeros((CHUNK - pending.shape[0], pending.shape[1]), F32)
            return d_cur + jnp.concatenate([lead, pending], axis=0)

        y, vjp = jax.vjp(_conf_tail, conv_ref[...], cur_ref[:, 1536:2048], lnw, lnb)
        ycat_ref[:, 1536:2048] = y.astype(BF16)
        d_conv, dz, d_lnw, d_lnb = vjp(_dot_nt(dxn_v, wo_ref[1536:2048, :]))
        ext, vjp = jax.vjp(_conf_glu, cur_ref[:, C_CA:C_CA + 1024], pcc_ref[...] * nf)
        d_ext, d_dww, d_dwb = _dwconv_bwd(CONF_HALO, (ext, dww), d_conv)
        dcacg, dpcc = vjp(d_ext)
        dproj_ref[:, 1536:2048] = dz.astype(BF16)
        dproj_ref[:, C_CA:C_CA + 1024] = add_tail(dcacg, pend_cc[...]).astype(BF16)
        pend_cc[...] = dpcc
        for r, g in ((g_dww, d_dww), (g_dwb, d_dwb), (g_lnw, d_lnw), (g_lnb, d_lnb)):
            r[...] += g

        attn = functools.partial(_attn_part, kvmask=_kv_mask(CHUNK, not_first, ATTN_STACK_BWD),
                                 stack=ATTN_STACK_BWD)
        y, vjp = jax.vjp(attn, cur_ref[:, 1024:1536], cur_ref[:, C_Q:C_Q + 512], cur_ref[:, C_K:C_K + 256],
                         pkv_ref[...] * nf, snk)
        ycat_ref[:, 1024:1536] = y.astype(BF16)
        dz, dq, dkv, dpkv, d_snk = vjp(_dot_nt(dxn_v, wo_ref[1024:1536, :]))
        dproj_ref[:, 1024:1536] = dz.astype(BF16)
        dproj_ref[:, C_Q:C_Q + 512] = dq.astype(BF16)
        dproj_ref[:, C_K:C_K + 256] = (dkv + pend_kv[...]).astype(BF16)
        pend_kv[...] = dpkv
        g_snk[...] += d_snk

        ext = jnp.concatenate([pxbc_ref[...] * nf, cur_ref[:, C_XBC:C_XBC + 1536]], axis=0)
        (y, _), vjp = jax.vjp(_ssd_part, cur_ref[:, 0:1024], _dwconv(ext, cw, cb, XBC_HALO),
                              cur_ref[:, C_DT:C_DT + 128], s_ref[0], dtb, alog, dsk, nw)
        ycat_ref[:, 0:1024] = y.astype(BF16)
        dz, d_conv, ddtr, ds_in, d_dtb, d_alog, d_dsk, d_nw = vjp((_dot_nt(dxn_v, wo_ref[0:1024, :]), ds_scr[...]))
        d_ext, d_cw, d_cb = _dwconv_bwd(XBC_HALO, (ext, cw), d_conv)
        dpxbc, dxbc = d_ext[0:XBC_HALO, :], d_ext[XBC_HALO:, :]
        dproj_ref[:, 0:1024] = dz.astype(BF16)
        dproj_ref[:, C_XBC:C_XBC + 1536] = add_tail(dxbc, pend_xbc[...]).astype(BF16)
        dproj_ref[:, C_DT:C_DT + 128] = ddtr.astype(BF16)
        dproj_ref[:, C_DT + 128:PROJ_W] = jnp.zeros((CHUNK, PROJ_W - C_DT - 128), BF16)
        pend_xbc[...] = dpxbc
        ds_scr[...] = ds_in
        for r, g in ((g_cw, d_cw), (g_cb, d_cb), (g_dtb, d_dtb), (g_alog, d_alog), (g_dsk, d_dsk), (g_nw, d_nw)):
            r[...] += g

    row = lambda b, j: (b * nc + nc - 1 - j, 0)
    prm_shapes = [(4, 1536), (1, 1536), (1, 128), (1, 128), (1, 128), (1, 1024), (1, 128), (CONF_KERNEL, 512),
                  (1, 512), (1, 512), (1, 512)]
    grid = (nb, nc)
    return _pallas_with_side(
        body, None, *_grid_ends(grid), 8 + n_prm, 2 + n_prm, name=name, grid=grid,
        out_shape=[jax.ShapeDtypeStruct((t, PROJ_W), BF16), jax.ShapeDtypeStruct((t, MIX_WIDTH), BF16)]
                  + [jax.ShapeDtypeStruct(s, F32) for s in prm_shapes],
        in_specs=[pl.BlockSpec((CHUNK, D_MODEL), row), pl.BlockSpec((CHUNK, PROJ_W), row)]
                 + _halo_specs(nc, lambda j: nc - 1 - j)
                 + [pl.BlockSpec((1, SSD_HEADS * SSD_HEAD_DIM, SSD_STATE), lambda b, j: (b * nc + nc - 1 - j, 0, 0)),
                    pl.BlockSpec((CHUNK, 512), row), _full((MIX_WIDTH, D_MODEL))] + _param_specs(),
        out_specs=[pl.BlockSpec((CHUNK, PROJ_W), row), pl.BlockSpec((CHUNK, MIX_WIDTH), row)]
                  + [_full(s) for s in prm_shapes],
        scratch_shapes=[pltpu.VMEM((SSD_HEADS * SSD_HEAD_DIM, SSD_STATE), F32), pltpu.VMEM((XBC_HALO, 1536), F32),
                        pltpu.VMEM((CHUNK, 256), F32), pltpu.VMEM((CONF_HALO, 1024), F32)],
        compiler_params=pltpu.CompilerParams(dimension_semantics=("arbitrary", "arbitrary"),
                                             vmem_limit_bytes=VMEM_LIMIT),
        args=(dxn, proj, proj, proj, proj, s_all, conv_all, w_out, *params))


def _gw_out(y_cat, dxn, name):
    t = y_cat.shape[0]
    tk = 512

    def body(y_ref, dxn_ref, out_ref, acc):
        k = pl.program_id(0)

        @pl.when(k == 0)
        def _():
            acc[...] = jnp.zeros_like(acc)

        acc[...] += _dot_tn(y_ref[...], dxn_ref[...])

        @pl.when(k == t // tk - 1)
        def _():
            out_ref[...] = acc[...].astype(BF16)

    out = pl.pallas_call(
        body, name=name, grid=(t // tk,),
        out_shape=jax.ShapeDtypeStruct((MIX_WIDTH, D_MODEL), BF16),
        in_specs=[pl.BlockSpec((tk, MIX_WIDTH), lambda k: (k, 0)), pl.BlockSpec((tk, D_MODEL), lambda k: (k, 0))],
        out_specs=_full((MIX_WIDTH, D_MODEL)),
        scratch_shapes=[pltpu.VMEM((MIX_WIDTH, D_MODEL), F32)],
        compiler_params=pltpu.CompilerParams(dimension_semantics=("arbitrary",), vmem_limit_bytes=VMEM_LIMIT),
    )(y_cat, dxn)
    return out.reshape(N_DEV, MIX_WIDTH // N_DEV, D_MODEL)


def _inproj_bwd_x(dproj, w, x, nw, dxn, name, side=None):
    t = x.shape[0]
    tm = 256

    def body(dp_ref, w_ref, x_ref, nw_ref, dxn_ref, dx_ref, gnw_ref):
        @pl.when(pl.program_id(0) == 0)
        def _():
            gnw_ref[...] = jnp.zeros_like(gnw_ref)

        dh = jnp.zeros((tm, D_MODEL), F32)
        for j in range(N_COL_TILES):
            sl = slice(j * COL_TILE, (j + 1) * COL_TILE)
            dh = dh + _dot_nt(dp_ref[:, sl], w_ref[:, sl])
        _, vjp = jax.vjp(_rmsnorm, x_ref[...], nw_ref[...])
        dx, dnw = vjp(dh)
        dx_ref[...] = dxn_ref[...] + dx
        gnw_ref[...] += dnw

    tok = lambda i: (i, 0)
    grid = (t // tm,)
    return _pallas_with_side(
        body, side, *_grid_ends(grid), 5, 2, name=name, grid=grid,
        out_shape=[jax.ShapeDtypeStruct((t, D_MODEL), F32), jax.ShapeDtypeStruct((1, D_MODEL), F32)],
        in_specs=[pl.BlockSpec((tm, PROJ_W), tok), _full((D_MODEL, PROJ_W)), pl.BlockSpec((tm, D_MODEL), tok),
                  _full((1, D_MODEL)), pl.BlockSpec((tm, D_MODEL), tok)],
        out_specs=[pl.BlockSpec((tm, D_MODEL), tok), _full((1, D_MODEL))],
        compiler_params=pltpu.CompilerParams(dimension_semantics=("arbitrary",), vmem_limit_bytes=VMEM_LIMIT),
        args=(dproj, w, x, nw, dxn))


def _inproj_bwd_w(h, dproj, name, side=None):
    t = h.shape[0]
    tk = 512

    def body(h_ref, dp_ref, gw_ref):
        @pl.when(pl.program_id(1) == 0)
        def _():
            gw_ref[...] = jnp.zeros_like(gw_ref)

        gw_ref[...] += _dot_tn(h_ref[...], dp_ref[...])

    grid = (N_COL_TILES, t // tk)
    return _pallas_with_side(
        body, side, *_grid_ends(grid), 2, 1, name=name, grid=grid,
        out_shape=[jax.ShapeDtypeStruct((D_MODEL, PROJ_W), F32)],
        in_specs=[pl.BlockSpec((tk, D_MODEL), lambda n, k: (k, 0)), pl.BlockSpec((tk, COL_TILE), lambda n, k: (k, n))],
        out_specs=[pl.BlockSpec((D_MODEL, COL_TILE), lambda n, k: (0, n))],
        compiler_params=pltpu.CompilerParams(dimension_semantics=("arbitrary", "arbitrary"),
                                             vmem_limit_bytes=VMEM_LIMIT),
        args=(h, dproj))


def _repack_runs():
    pieces = ((0, 2048, C_Z), (2048, 3584, C_XBC), (3584, 3600, C_DT), (3600, 4368, C_Q), (4368, D_IN_PROJ, C_CA))
    per = D_IN_PROJ // N_DEV
    runs = []
    for j in range(N_DEV):
        lo, hi = per * j, per * (j + 1)
        for a, b, dst in pieces:
            s, e = max(lo, a), min(hi, b)
            if s < e:
                runs.append((j, s - lo, e - lo, dst + s - a))
    return runs


def _repack_w_in(g, name):
    tr = 256

    def body(g_ref, o_ref):
        for j, a, b, dst in _repack_runs():
            o_ref[:, dst:dst + b - a] = g_ref[j, :, a:b]
        o_ref[:, C_DT + 16:PROJ_W] = jnp.zeros((tr, PROJ_W - C_DT - 16), g.dtype)

    return pl.pallas_call(
        body, name=name, grid=(D_MODEL // tr,),
        out_shape=jax.ShapeDtypeStruct((D_MODEL, PROJ_W), g.dtype),
        in_specs=[pl.BlockSpec((N_DEV, tr, D_IN_PROJ // N_DEV), lambda i: (0, i, 0))],
        out_specs=pl.BlockSpec((tr, PROJ_W), lambda i: (i, 0)),
        compiler_params=pltpu.CompilerParams(dimension_semantics=("arbitrary",)),
    )(g)


def _unpack_gw_in(g, name):
    tr = 256

    def body(g_ref, o_ref):
        for j, a, b, dst in _repack_runs():
            o_ref[j, :, a:b] = g_ref[:, dst:dst + b - a].astype(BF16)

    return pl.pallas_call(
        body, name=name, grid=(D_MODEL // tr,),
        out_shape=jax.ShapeDtypeStruct((N_DEV, D_MODEL, D_IN_PROJ // N_DEV), BF16),
        in_specs=[pl.BlockSpec((tr, PROJ_W), lambda i: (i, 0))],
        out_specs=pl.BlockSpec((N_DEV, tr, D_IN_PROJ // N_DEV), lambda i: (0, i, 0)),
        compiler_params=pltpu.CompilerParams(dimension_semantics=("arbitrary",)),
    )(g)


def _pair_sum(parts, name):
    n_dev, r, cdim = parts.shape
    n_chip = n_dev // 2
    by_chip = parts.reshape(n_chip, 2, r, cdim)

    def swap_body(p_ref, got_ref, send_sem, recv_sem):
        x, y, c = _my_place()
        cp = pltpu.make_async_remote_copy(
            src_ref=p_ref.at[:, pl.ds(1 - c, 1)], dst_ref=got_ref, send_sem=send_sem, recv_sem=recv_sem,
            device_id=(x, y, 1 - c), device_id_type=pl.DeviceIdType.MESH)
        cp.start()
        cp.wait()

    any_spec = pl.BlockSpec(memory_space=pl.ANY)
    got = pl.pallas_call(
        swap_body, name=name + "_swap",
        out_shape=jax.ShapeDtypeStruct((n_chip, 1, r, cdim), parts.dtype),
        in_specs=[any_spec], out_specs=any_spec,
        scratch_shapes=[pltpu.SemaphoreType.DMA, pltpu.SemaphoreType.DMA],
    )(by_chip)

    tr = 256

    def add_body(core_ref, p_ref, got_ref, o_ref):
        o_ref[0] = (p_ref[0, 0].astype(F32) + got_ref[0, 0].astype(F32)).astype(o_ref.dtype)

    return pl.pallas_call(
        add_body, name=name + "_add",
        grid_spec=pltpu.PrefetchScalarGridSpec(
            num_scalar_prefetch=1, grid=(n_chip, r // tr),
            in_specs=[pl.BlockSpec((1, 1, tr, cdim), lambda k, i, core: (k, core[0], i, 0)),
                      pl.BlockSpec((1, 1, tr, cdim), lambda k, i, core: (k, 0, i, 0))],
            out_specs=pl.BlockSpec((1, tr, cdim), lambda k, i, core: (k, i, 0))),
        out_shape=jax.ShapeDtypeStruct((n_chip, r, cdim), parts.dtype),
        compiler_params=pltpu.CompilerParams(dimension_semantics=("arbitrary", "arbitrary")),
    )(lax.axis_index("c").astype(jnp.int32).reshape(1), by_chip, got)


def _adamw(w, g, m, v):
    m = ADAM_B1 * m + (1.0 - ADAM_B1) * g
    v = ADAM_B2 * v + (1.0 - ADAM_B2) * jnp.square(g)
    m_hat = m / (1.0 - ADAM_B1 ** ADAM_STEP)
    v_hat = v / (1.0 - ADAM_B2 ** ADAM_STEP)
    delta = -ADAM_LR * (m_hat / (jnp.sqrt(v_hat) + ADAM_EPS) + ADAM_WD * w)
    return delta, m, v


def _reduce_adamw(parts, w, m, v, tr, name):
    depth = len(parts)
    p, r, cdim = parts[0].shape
    n_blk = r // tr

    def body(*refs):
        p_refs = refs[:depth]
        w_ref, m_ref, v_ref, g_ref, d_ref, nm_ref, nv_ref = refs[depth:]
        for layer in range(depth):
            @pl.when(pl.program_id(0) == layer)
            def _(p_ref=p_refs[layer]):
                g = p_ref[0].astype(F32)
                for i in range(1, p):
                    g = g + p_ref[i].astype(F32)
                g_ref[0] = g
                d_ref[0], nm_ref[0], nv_ref[0] = _adamw(w_ref[0], g, m_ref[0], v_ref[0])

    def parts_spec(layer):
        return pl.BlockSpec((p, tr, cdim), lambda d, i: (0, jnp.clip(i + (d - layer) * n_blk, 0, n_blk - 1), 0))

    blk = pl.BlockSpec((1, tr, cdim), lambda d, i: (d, i, 0))
    return pl.pallas_call(
        body, name=name, grid=(depth, n_blk),
        out_shape=[jax.ShapeDtypeStruct(w.shape, F32)] * 4,
        in_specs=[parts_spec(layer) for layer in range(depth)] + [blk, blk, blk],
        out_specs=[blk] * 4,
        compiler_params=pltpu.CompilerParams(dimension_semantics=("arbitrary", "arbitrary"),
                                             vmem_limit_bytes=VMEM_LIMIT),
    )(*parts, w, m, v)


def _adamw_small(ssum, entries, name):
    direct = [e[3] for e in entries if not isinstance(e[3], list)]
    n_direct = len(direct)

    def body(*refs):
        ssum_ref, direct_refs = refs[0], list(refs[1:1 + n_direct])
        ins = refs[1 + n_direct:1 + n_direct + 3 * len(entries)]
        outs = refs[1 + n_direct + 3 * len(entries):]
        for k, (w, _, _, grad) in enumerate(entries):
            w_ref, m_ref, v_ref = ins[3 * k:3 * k + 3]
            g_ref, d_ref, nm_ref, nv_ref = outs[4 * k:4 * k + 4]
            if isinstance(grad, list):
                for row, off in enumerate(grad):
                    rows = slice(row, row + 1)
                    g = ssum_ref[:, off:off + w.shape[1]]
                    g_ref[rows, :] = g
                    d_ref[rows, :], nm_ref[rows, :], nv_ref[rows, :] = _adamw(w_ref[rows, :], g, m_ref[rows, :],
                                                                              v_ref[rows, :])
            else:
                g = direct_refs.pop(0)[...]
                g_ref[...] = g
                d_ref[...], nm_ref[...], nv_ref[...] = _adamw(w_ref[...], g, m_ref[...], v_ref[...])

    vmem = pl.BlockSpec(memory_space=pltpu.VMEM)
    args = [ssum] + direct + [a for e in entries for a in e[:3]]
    res = pl.pallas_call(
        body, name=name,
        out_shape=[jax.ShapeDtypeStruct(e[0].shape, F32) for e in entries for _ in range(4)],
        in_specs=[vmem] * len(args), out_specs=[vmem] * (4 * len(entries)),
    )(*args)
    return [res[4 * k:4 * k + 4] for k in range(len(entries))]


def _reduce_adamw_cols(parts, w, m, v, name):
    depth = len(parts)
    _, r, cdim = parts[0].shape
    tc = 512

    def body(*refs):
        p_refs = refs[:depth]
        w_ref, m_ref, v_ref, g_ref, d_ref, nm_ref, nv_ref = refs[depth:]
        for layer in range(depth):
            g = p_refs[layer][0].astype(F32)
            for i in range(1, parts[layer].shape[0]):
                g = g + p_refs[layer][i].astype(F32)
            g = g.T
            g_ref[:, layer, :] = g
            d_ref[:, layer, :], nm_ref[:, layer, :], nv_ref[:, layer, :] = _adamw(
                w_ref[:, layer, :], g, m_ref[:, layer, :], v_ref[:, layer, :])

    view = lambda a: jnp.transpose(a, (2, 0, 1))
    blk = pl.BlockSpec((cdim, depth, tc), lambda i: (0, 0, i))
    outs = pl.pallas_call(
        body, name=name, grid=(r // tc,),
        out_shape=[jax.ShapeDtypeStruct((cdim, depth, r), F32)] * 4,
        in_specs=[pl.BlockSpec((a.shape[0], tc, cdim), lambda i: (0, i, 0)) for a in parts] + [blk, blk, blk],
        out_specs=[blk] * 4,
        compiler_params=pltpu.CompilerParams(dimension_semantics=("arbitrary",), vmem_limit_bytes=VMEM_LIMIT),
    )(*parts, view(w), view(m), view(v))
    return [jnp.transpose(o, (1, 2, 0)) for o in outs]


def _pad_lanes(v, width=128):
    return jnp.pad(v.reshape(1, -1), ((0, 0), (0, width - v.shape[-1])))


SMALL_FIELDS = (("norm_w", 1024), ("conv_b", 1536), ("dt_bias", 128), ("a_log", 128), ("d_skip", 128),
                ("ssd_norm_w", 1024), ("sinks", 128), ("dw_b", 512), ("ln_w", 512), ("ln_b", 512))


def kernel(x, norm_w, w_in, ssd_conv_w, ssd_conv_b, ssd_dt_bias, ssd_a_log, ssd_d, ssd_norm_w, attn_sinks, conf_dw_w, conf_dw_b, conf_ln_w, conf_ln_b, w_out, final_norm_w, loss_target, m_norm_w, m_w_in, m_ssd_conv_w, m_ssd_conv_b, m_ssd_dt_bias, m_ssd_a_log, m_ssd_d, m_ssd_norm_w, m_attn_sinks, m_conf_dw_w, m_conf_dw_b, m_conf_ln_w, m_conf_ln_b, m_w_out, m_final_norm_w, v_norm_w, v_w_in, v_ssd_conv_w, v_ssd_conv_b, v_ssd_dt_bias, v_ssd_a_log, v_ssd_d, v_ssd_norm_w, v_attn_sinks, v_conf_dw_w, v_conf_dw_b, v_conf_ln_w, v_conf_ln_b, v_w_out, v_final_norm_w):
    nb, seq, _ = x.shape
    depth = norm_w.shape[0]
    t = nb * seq

    w_in_bf, w_out_bf = w_in.astype(BF16), w_out.astype(BF16)
    g_win0, g_cw, g_dw = _all_gather([w_in_bf[0], ssd_conv_w, conf_dw_w], "gather_weights")
    w_in_full = [_repack_w_in(g_win0, "repack_w_in_0")]
    w_out_full = []
    conv_w_full = [jnp.transpose(g_cw[:, l], (1, 0, 2)).reshape(4, 1536) for l in range(depth)]
    dw_w_full = [jnp.transpose(g_dw[:, l], (1, 0, 2)).reshape(CONF_KERNEL, 512) for l in range(depth)]

    def layer_params(l):
        return [conv_w_full[l], ssd_conv_b[l].reshape(1, -1), _pad_lanes(ssd_dt_bias[l]), _pad_lanes(ssd_a_log[l]),
                _pad_lanes(ssd_d[l]), ssd_norm_w[l].reshape(1, -1), _pad_lanes(attn_sinks[l]), dw_w_full[l],
                conf_dw_b[l].reshape(1, -1), conf_ln_w[l].reshape(1, -1), conf_ln_b[l].reshape(1, -1)]

    xs = [x.reshape(t, D_MODEL)]
    saved = []
    for l in range(depth):
        (proj, h), gathered = _inproj_fwd(xs[l], norm_w[l].reshape(1, -1), w_in_full[l], f"inproj_fwd_{l}",
                                          (GATHER, [w_out_bf[l]]))
        w_out_full.append(gathered[0].reshape(MIX_WIDTH, D_MODEL))
        last = l + 1 == depth
        side = None if last else (GATHER, [w_in_bf[l + 1]])
        head = (final_norm_w.reshape(1, -1), loss_target.reshape(t, D_MODEL)) if last else None
        res, gathered = _mixer_fwd(xs[l], proj, w_out_full[l], layer_params(l), nb, f"mixer_fwd_{l}", side, head)
        if side:
            w_in_full.append(_repack_w_in(gathered[0], f"repack_w_in_{l + 1}"))
        saved.append((proj, h, res[1:3]))
        xs.append(res[0])
    dx, loss_part, g_fnw = res[0], res[3], res[4]

    small_rows, shard_rows = [None] * depth, [None] * depth
    received = [None] * depth
    for l in reversed(range(depth)):
        proj, h, kept = saved[l]
        res, _ = _mixer_bwd(dx, proj, *kept, w_out_full[l], layer_params(l), nb, f"mixer_bwd_{l}")
        dproj, y_cat = res[0], res[1]
        g_cw_l, g_cb, g_dtb, g_alog, g_dsk, g_nw, g_snk, g_dww, g_dwb, g_lnw, g_lnb = res[2:]
        gw_out_parts = _gw_out(y_cat, dx, f"gw_out_{l}")
        (gw_in,), got_out = _inproj_bwd_w(h, dproj, f"inproj_bwd_w_{l}", (SCATTER, [gw_out_parts]))
        chip_parts = _pair_sum(_unpack_gw_in(gw_in, f"unpack_gw_in_{l}"), f"pair_sum_{l}")
        (dx, g_norm), got_in = _inproj_bwd_x(dproj, w_in_full[l], xs[l], norm_w[l].reshape(1, -1), dx,
                                             f"inproj_bwd_x_{l}", (CHIP_SCATTER, [chip_parts]))
        received[l] = [got_in[0], got_out[0]]
        small_rows[l] = [g_norm, g_cb, g_dtb, g_alog, g_dsk, g_nw, g_snk, g_dwb, g_lnw, g_lnb]
        shard_rows[l] = [jnp.transpose(g.reshape(g.shape[0], N_DEV, -1), (1, 0, 2)).reshape(N_DEV, -1)
                         for g in (g_cw_l, g_dww)]
    grad_x = dx.reshape(nb, seq, D_MODEL)

    small = jnp.concatenate([piece for l in range(depth) for piece in small_rows[l]]
                            + [g_fnw, _pad_lanes(loss_part)], axis=1)
    shard_small = jnp.concatenate([piece for l in range(depth) for piece in shard_rows[l]], axis=1)
    ssum, shard_sum = _exchange_small(small, shard_small.reshape(N_DEV, 1, -1), "exchange_small")
    loss = ssum[0, small.shape[1] - 128]

    g_w_in, d_w_in, nm_w_in, nv_w_in = _reduce_adamw_cols([received[l][0] for l in range(depth)], w_in, m_w_in,
                                                          v_w_in, "adamw_w_in")
    g_w_out, d_w_out, nm_w_out, nv_w_out = _reduce_adamw([received[l][1] for l in range(depth)], w_out, m_w_out,
                                                         v_w_out, 256, "adamw_w_out")

    per_layer = sum(n for _, n in SMALL_FIELDS)
    given = {"norm_w": (norm_w, m_norm_w, v_norm_w), "conv_b": (ssd_conv_b, m_ssd_conv_b, v_ssd_conv_b),
             "dt_bias": (ssd_dt_bias, m_ssd_dt_bias, v_ssd_dt_bias), "a_log": (ssd_a_log, m_ssd_a_log, v_ssd_a_log),
             "d_skip": (ssd_d, m_ssd_d, v_ssd_d), "ssd_norm_w": (ssd_norm_w, m_ssd_norm_w, v_ssd_norm_w),
             "sinks": (attn_sinks, m_attn_sinks, v_attn_sinks), "dw_b": (conf_dw_b, m_conf_dw_b, v_conf_dw_b),
             "ln_w": (conf_ln_w, m_conf_ln_w, v_conf_ln_w), "ln_b": (conf_ln_b, m_conf_ln_b, v_conf_ln_b)}
    entries = []
    off = 0
    for fname, n in SMALL_FIELDS:
        entries.append((*given[fname], [l * per_layer + off for l in range(depth)]))
        off += n
    n_cw, n_dw = ssd_conv_w[0].size, conf_dw_w[0].size
    per_layer_shard = n_cw + n_dw
    cw_grad = jnp.stack([shard_sum[0, l * per_layer_shard:l * per_layer_shard + n_cw].reshape(ssd_conv_w.shape[1:])
                         for l in range(depth)], axis=0)
    dw_grad = jnp.stack([shard_sum[0, l * per_layer_shard + n_cw:(l + 1) * per_layer_shard]
                         .reshape(conf_dw_w.shape[1:]) for l in range(depth)], axis=0)
    entries.append((ssd_conv_w, m_ssd_conv_w, v_ssd_conv_w, cw_grad))
    entries.append((conf_dw_w, m_conf_dw_w, v_conf_dw_w, dw_grad))
    entries.append((final_norm_w.reshape(1, -1), m_final_norm_w.reshape(1, -1), v_final_norm_w.reshape(1, -1),
                    [depth * per_layer]))
    sm = _adamw_small(ssum, entries, "adamw_small")
    sm = {k: quad for k, quad in zip([f for f, _ in SMALL_FIELDS] + ["conv_w", "dw_w", "final"], sm)}

    def outputs(i, big_in_i, big_out_i):
        return [sm["norm_w"][i], big_in_i, sm["conv_w"][i], sm["conv_b"][i], sm["dt_bias"][i], sm["a_log"][i],
                sm["d_skip"][i], sm["ssd_norm_w"][i], sm["sinks"][i], sm["dw_w"][i], sm["dw_b"][i], sm["ln_w"][i],
                sm["ln_b"][i], big_out_i, sm["final"][i].reshape(-1)]

    return (loss, grad_x, *outputs(0, g_w_in, g_w_out), *outputs(1, d_w_in, d_w_out),
            *outputs(2, nm_w_in, nm_w_out), *outputs(3, nv_w_in, nv_w_out))
```

```python
import functools

import jax
import jax.numpy as jnp
from jax import lax
from jax.experimental import pallas as pl
from jax.experimental.pallas import tpu as pltpu

F32 = jnp.float32
BF16 = jnp.bfloat16
N_DEV = 8
EPS = 1e-5

D_MODEL = 1024
CHUNK = 128
SSD_HEADS = 16
SSD_HEAD_DIM = 64
SSD_STATE = 128
ATTN_HEADS = 8
ATTN_HEAD_DIM = 64
CONF_KERNEL = 31
MIX_WIDTH = 2048
D_IN_PROJ = 5392
C_Z = 0
C_CA = 2048
C_XBC = 3072
C_Q = 4608
C_K = 5120
C_DT = 5376
PROJ_W = 5632
N_COL_TILES = 4
COL_TILE = PROJ_W // N_COL_TILES
XBC_HALO = 8
CONF_HALO = 32
ATTN_STACK_FWD = 2
ATTN_STACK_BWD = 4
VMEM_LIMIT = 56 * 1024 * 1024

ADAM_LR = 0.001
ADAM_B1 = 0.9
ADAM_B2 = 0.999
ADAM_EPS = 1e-08
ADAM_WD = 0.01
ADAM_STEP = 10


def _silu(v):
    return v * jax.nn.sigmoid(v)


def _softplus(v):
    return jnp.maximum(v, 0.0) + jnp.log1p(jnp.exp(-jnp.abs(v)))


def _rmsnorm(v, w):
    return v * lax.rsqrt(jnp.mean(v * v, axis=-1, keepdims=True) + EPS) * w


def _dot(a, b):
    return jnp.dot(a.astype(BF16), b.astype(BF16), preferred_element_type=F32)


def _dot_nt(a, b):
    return lax.dot_general(a.astype(BF16), b.astype(BF16), (((1,), (1,)), ((), ())), preferred_element_type=F32)


def _dot_tn(a, b):
    return lax.dot_general(a.astype(BF16), b.astype(BF16), (((0,), (0,)), ((), ())), preferred_element_type=F32)


def _taps(ext, offs, out_len, w=None, g=None):
    n_rows, n_cols = ext.shape
    by_shift = {}
    for t, off in enumerate(offs):
        by_shift.setdefault(off % 8, []).append((t, off))
    for r, taps in by_shift.items():
        assert max(off for _, off in taps) - r + out_len <= n_rows - r
    accs = []
    sums = [[None] * (n_cols // 128) for _ in offs]
    for blk in range(n_cols // 128):
        cs = slice(blk * 128, (blk + 1) * 128)
        e = ext[:, cs]
        acc = None
        for r, taps in by_shift.items():
            shifted = e if r == 0 else pltpu.roll(e, n_rows - r, axis=0)
            for t, off in taps:
                window = shifted[off - r:off - r + out_len, :]
                if w is not None:
                    term = w[t:t + 1, cs] * window
                    acc = term if acc is None else acc + term
                if g is not None:
                    sums[t][blk] = jnp.sum(g[:, cs] * window, axis=0, keepdims=True)
        accs.append(acc)
    if w is not None:
        return jnp.concatenate(accs, axis=1)
    return jnp.concatenate([jnp.concatenate(row, axis=1) for row in sums], axis=0)


@functools.partial(jax.custom_vjp, nondiff_argnums=(3,))
def _dwconv(ext, w, b, halo):
    kk = w.shape[0]
    return b + _taps(ext, [halo - (kk - 1) + t for t in range(kk)], ext.shape[0] - halo, w=w)


def _dwconv_fwd(ext, w, b, halo):
    return _dwconv(ext, w, b, halo), (ext, w)


def _dwconv_bwd(halo, res, g):
    ext, w = res
    kk = w.shape[0]
    offs = [halo - (kk - 1) + t for t in range(kk)]
    dw = _taps(ext, offs, ext.shape[0] - halo, g=g)
    zeros = jnp.zeros((halo, g.shape[1]), g.dtype)
    gp = jnp.concatenate([zeros, g, zeros], axis=0)
    dext = _taps(gp, [halo - off for off in offs], ext.shape[0], w=w)
    return dext, dw, jnp.sum(g, axis=0, keepdims=True)


_dwconv.defvjp(_dwconv_fwd, _dwconv_bwd)


def _ssd_part(z_ssd, conv_out, dtr, s_in, dtb, alog, dsk, nw):
    qn = conv_out.shape[0]
    nh = SSD_HEADS
    per_group = nh // 2
    n_pair = nh // 2
    xa = _silu(conv_out)
    xs = xa[:, 0:1024]
    dt = _softplus(dtr + dtb)
    a = dt * (-jnp.exp(alog))
    rows = lax.broadcasted_iota(jnp.int32, (qn, qn), 0)
    cols = lax.broadcasted_iota(jnp.int32, (qn, qn), 1)
    causal = rows >= cols
    low = cols < SSD_HEAD_DIM
    a_cs = jnp.dot(causal.astype(F32), a, precision=lax.Precision.HIGHEST, preferred_element_type=F32)
    a_cs_t = a_cs.T
    bgs = [xa[:, 1024 + g * 128:1024 + (g + 1) * 128] for g in range(2)]
    cgs = [xa[:, 1280 + g * 128:1280 + (g + 1) * 128] for g in range(2)]
    cbms = [_dot_nt(cgs[g], bgs[g]) for g in range(2)]
    colb = [jnp.broadcast_to(a_cs[:, h:h + 1], (qn, qn)) for h in range(nh)]
    dtb_wide = [jnp.broadcast_to(dt[:, h:h + 1], (qn, qn)) for h in range(nh)]
    lmats = [jnp.exp(jnp.where(causal, colb[h] - a_cs_t[h:h + 1, :], -jnp.inf)) for h in range(nh)]
    ms = [cbms[h // per_group] * lmats[h] for h in range(nh)]
    x_pair = [xs[:, p * 128:(p + 1) * 128] for p in range(n_pair)]
    xdt = [x_pair[p] * jnp.where(low, dtb_wide[2 * p], dtb_wide[2 * p + 1]) for p in range(n_pair)]
    x_lo = [jnp.where(low, xdt[p], 0.0) for p in range(n_pair)]
    x_hi = [jnp.where(low, 0.0, xdt[p]) for p in range(n_pair)]
    y_diag = [_dot(ms[2 * p], x_lo[p]) + _dot(ms[2 * p + 1], x_hi[p]) for p in range(n_pair)]
    col_pair = [jnp.where(low, colb[2 * p], colb[2 * p + 1]) for p in range(n_pair)]
    last_pair = [jnp.broadcast_to(col_pair[p][qn - 1:qn, :], (qn, qn)) for p in range(n_pair)]
    ecol = [jnp.exp(col_pair[p]) for p in range(n_pair)]
    xw = [xdt[p] * jnp.exp(last_pair[p] - col_pair[p]) for p in range(n_pair)]
    y_off, st = [], []
    for g in range(2):
        ps = range(g * n_pair // 2, (g + 1) * n_pair // 2)
        y_off.append(_dot_nt(cgs[g], s_in[g * 512:(g + 1) * 512, :]) * jnp.concatenate([ecol[p] for p in ps], axis=1))
        st.append(_dot_tn(jnp.concatenate([xw[p] for p in ps], axis=1), bgs[g]))
    e_last = jnp.exp(jnp.broadcast_to(a_cs_t[:, qn - 1:qn], (qn, SSD_STATE)))
    scale = jnp.concatenate([jnp.broadcast_to(e_last[h:h + 1, :], (64, SSD_STATE)) for h in range(nh)], axis=0)
    s_out = scale * s_in + jnp.concatenate(st, axis=0)
    d_wide = jnp.concatenate([jnp.broadcast_to(dsk[:, h:h + 1], (1, 64)) for h in range(nh)], axis=1)
    y = jnp.concatenate(y_diag, axis=1) + jnp.concatenate(y_off, axis=1) + d_wide * xs
    gated = y * _silu(z_ssd)
    halves = []
    for g in range(2):
        gg = gated[:, g * 512:(g + 1) * 512]
        halves.append(gg * lax.rsqrt(jnp.mean(gg * gg, axis=-1, keepdims=True) + EPS))
    return jnp.concatenate(halves, axis=1) * nw, s_out


def _attn_part(z_attn, q, kv, p_kv, snk, kvmask, stack):
    qn = q.shape[0]
    kk = jnp.concatenate([p_kv[:, 0:128], kv[:, 0:128]], axis=0)
    vv = jnp.concatenate([p_kv[:, 128:256], kv[:, 128:256]], axis=0)
    units = range(ATTN_HEADS // stack)
    heads = [range(u * stack, (u + 1) * stack) for u in units]
    kv_of = [u * stack // (ATTN_HEADS // 2) for u in units]
    k_of = [kk[:, g * 64:(g + 1) * 64] for g in kv_of]
    v_of = [vv[:, g * 64:(g + 1) * 64] for g in kv_of]
    qs = [jnp.concatenate([q[:, h * 64:(h + 1) * 64] for h in heads[u]], axis=0) for u in units]
    sk = [jnp.concatenate([jnp.broadcast_to(snk[:, h:h + 1], (qn, 1)) for h in heads[u]], axis=0) for u in units]
    s = [jnp.where(kvmask, _dot_nt(qs[u], k_of[u]) * (ATTN_HEAD_DIM ** -0.5), -jnp.inf) for u in units]
    m = [lax.stop_gradient(jnp.maximum(jnp.max(s[u], axis=-1, keepdims=True), sk[u])) for u in units]
    e = [jnp.exp(s[u] - m[u]) for u in units]
    r_den = [1.0 / (jnp.sum(e[u], axis=-1, keepdims=True) + jnp.exp(sk[u] - m[u])) for u in units]
    o = [_dot(e[u], v_of[u]) * r_den[u] for u in units]
    outs = [o[u][i * qn:(i + 1) * qn, :] for u in units for i in range(stack)]
    return jnp.concatenate(outs, axis=1) * _silu(z_attn)


def _conf_glu(cacg, p_cc):
    c0 = cacg[:, 0:512] * jax.nn.sigmoid(cacg[:, 512:1024])
    pc0 = p_cc[:, 0:512] * jax.nn.sigmoid(p_cc[:, 512:1024])
    return jnp.concatenate([pc0, c0], axis=0)


def _conf_tail(conv_out, z_conf, lnw, lnb):
    xc = conv_out - jnp.mean(conv_out, axis=-1, keepdims=True)
    yln = xc * lax.rsqrt(jnp.mean(xc * xc, axis=-1, keepdims=True) + EPS) * lnw + lnb
    return _silu(yln) * _silu(z_conf)


def _kv_mask(qn, not_first, reps):
    ii = lax.broadcasted_iota(jnp.int32, (reps * qn, 2 * qn), 0) & (qn - 1)
    jj = lax.broadcasted_iota(jnp.int32, (reps * qn, 2 * qn), 1)
    d = jj - ii
    return (d >= 1) & (d <= qn) & (not_first | (jj >= qn))


def _my_place():
    return lax.axis_index("x"), lax.axis_index("y"), lax.axis_index("c")


def _all_gather(arrs, name):
    n = len(arrs)

    def body(*refs):
        ins, outs = refs[:n], refs[n:2 * n]
        send_sems, recv_sems, local_sems = refs[2 * n:]
        x, y, c = _my_place()
        me, sibling = (x, y, c), (x, y, 1 - c)
        chips = [(1 - x, y), (x, 1 - y), (1 - x, 1 - y)]

        def slot(a, p):
            return outs[a].at[4 * p[0] + 2 * p[1] + p[2]]

        def copy(a, kk, block, to, src=None):
            return pltpu.make_async_remote_copy(
                src_ref=slot(a, block) if src is None else src, dst_ref=slot(a, block),
                send_sem=send_sems.at[a, kk], recv_sem=recv_sems.at[a, kk],
                device_id=to, device_id_type=pl.DeviceIdType.MESH)

        mine = [pltpu.make_async_copy(ins[a], slot(a, me), local_sems.at[a]) for a in range(n)]
        for cp in mine:
            cp.start()
        first = []
        for a in range(n):
            first.append(copy(a, 0, me, sibling, src=ins[a]))
            first += [copy(a, 1 + j, me, (*chip, c), src=ins[a]) for j, chip in enumerate(chips)]
        for cp in first:
            cp.start()
        passed = []
        for j, chip in enumerate(chips):
            for a in range(n):
                copy(a, 1 + j, (*chip, c), me).wait_recv()
                fwd = copy(a, 4 + j, (*chip, c), sibling)
                fwd.start()
                passed.append(fwd)
        for a in range(n):
            copy(a, 0, sibling, me).wait_recv()
            for j, chip in enumerate(chips):
                copy(a, 4 + j, (*chip, 1 - c), me).wait_recv()
        for cp in first + passed:
            cp.wait_send()
        for cp in mine:
            cp.wait()

    any_spec = pl.BlockSpec(memory_space=pl.ANY)
    return pl.pallas_call(
        body, name=name,
        out_shape=[jax.ShapeDtypeStruct((N_DEV,) + a.shape, a.dtype) for a in arrs],
        in_specs=[any_spec] * n, out_specs=[any_spec] * n,
        scratch_shapes=[pltpu.SemaphoreType.DMA((n, 7)), pltpu.SemaphoreType.DMA((n, 7)),
                        pltpu.SemaphoreType.DMA((n,))],
    )(*arrs)


GATHER, SCATTER, CHIP_SCATTER = "gather", "scatter", "chip_scatter"


def _direct_copies(mode, ins, outs, send_sems, recv_sems, local_sems):
    x, y, c = _my_place()
    by_chip = mode == CHIP_SCATTER
    place = (lambda px, py, pc: 2 * px + py) if by_chip else (lambda px, py, pc: 4 * px + 2 * py + pc)
    me_idx = place(x, y, c)
    n = len(ins)
    local = [pltpu.make_async_copy(ins[a] if mode == GATHER else ins[a].at[me_idx], outs[a].at[me_idx],
                                   local_sems.at[a]) for a in range(n)]
    remote = []
    for rel in range(1, N_DEV):
        if by_chip and rel & 1:
            continue
        px = 1 - x if rel & 4 else x
        py = 1 - y if rel & 2 else y
        pc = 1 - c if rel & 1 else c
        for a in range(n):
            remote.append(pltpu.make_async_remote_copy(
                src_ref=ins[a] if mode == GATHER else ins[a].at[place(px, py, pc)], dst_ref=outs[a].at[me_idx],
                send_sem=send_sems.at[a, rel - 1], recv_sem=recv_sems.at[a, rel - 1],
                device_id=(px, py, pc), device_id_type=pl.DeviceIdType.MESH))
    return local + remote


def _side_scratch(n):
    return [pltpu.SemaphoreType.DMA((n, 7)), pltpu.SemaphoreType.DMA((n, 7)), pltpu.SemaphoreType.DMA((n,))]


def _side_out_shapes(mode, arrs):
    return [jax.ShapeDtypeStruct((N_DEV,) + a.shape if mode == GATHER else a.shape, a.dtype) for a in arrs]


def _pallas_with_side(body, side, first, last, n_in, n_out, *, in_specs, out_specs, out_shape, scratch_shapes=(),
                      args, **kwargs):
    side_arrs = [] if side is None else list(side[1])
    ns = len(side_arrs)

    def wrapped(*refs):
        own_in, side_in = refs[:n_in], refs[n_in:n_in + ns]
        o = n_in + ns
        own_out, side_out = refs[o:o + n_out], refs[o + n_out:o + n_out + ns]
        scratch = refs[o + n_out + ns:]
        own_scratch, sems = (scratch[:-3], scratch[-3:]) if ns else (scratch, ())
        if ns:
            @pl.when(first())
            def _():
                for cp in _direct_copies(side[0], side_in, side_out, *sems):
                    cp.start()

        body(*own_in, *own_out, *own_scratch)
        if ns:
            @pl.when(last())
            def _():
                for cp in _direct_copies(side[0], side_in, side_out, *sems):
                    cp.wait()

    any_spec = pl.BlockSpec(memory_space=pl.ANY)
    res = pl.pallas_call(
        wrapped,
        in_specs=list(in_specs) + [any_spec] * ns,
        out_specs=list(out_specs) + [any_spec] * ns,
        out_shape=list(out_shape) + (_side_out_shapes(side[0], side_arrs) if ns else []),
        scratch_shapes=list(scratch_shapes) + (_side_scratch(ns) if ns else []),
        **kwargs,
    )(*args, *side_arrs)
    return res[:n_out], res[n_out:]


def _exchange_small(everyone, per_device, name):
    n_all, n_own = everyone.shape[1], per_device.shape[2]

    def body(all_ref, own_ref, all_sum, own_sum, all_slots, own_slots, *sems):
        copies = _direct_copies(GATHER, [all_ref], [all_slots], *sems[:3])
        copies += _direct_copies(SCATTER, [own_ref], [own_slots], *sems[3:])
        for cp in copies:
            cp.start()
        for cp in copies:
            cp.wait()
        for slots, out in ((all_slots, all_sum), (own_slots, own_sum)):
            total = slots[0]
            for i in range(1, N_DEV):
                total = total + slots[i]
            out[...] = total

    vmem_spec = pl.BlockSpec(memory_space=pltpu.VMEM)
    return pl.pallas_call(
        body, name=name,
        out_shape=[jax.ShapeDtypeStruct((1, n_all), F32), jax.ShapeDtypeStruct((1, n_own), F32)],
        in_specs=[vmem_spec, vmem_spec], out_specs=[vmem_spec, vmem_spec],
        scratch_shapes=[pltpu.VMEM((N_DEV, 1, n_all), F32), pltpu.VMEM((N_DEV, 1, n_own), F32)]
                       + _side_scratch(1) + _side_scratch(1),
    )(everyone, per_device)


def _full(shape):
    return pl.BlockSpec(shape, lambda *_: (0,) * len(shape))


def _inproj_fwd(x, nw, w, name, side=None):
    t = x.shape[0]
    tm = 256

    def body(x_ref, nw_ref, w_ref, proj_ref, h_ref):
        h = _rmsnorm(x_ref[...], nw_ref[...]).astype(BF16)
        h_ref[...] = h
        for j in range(N_COL_TILES):
            sl = slice(j * COL_TILE, (j + 1) * COL_TILE)
            proj_ref[:, sl] = jnp.dot(h, w_ref[:, sl], preferred_element_type=F32)

    grid = (t // tm,)
    return _pallas_with_side(
        body, side, *_grid_ends(grid), 3, 2, name=name, grid=grid,
        out_shape=[jax.ShapeDtypeStruct((t, PROJ_W), F32), jax.ShapeDtypeStruct((t, D_MODEL), BF16)],
        in_specs=[pl.BlockSpec((tm, D_MODEL), lambda i: (i, 0)), _full((1, D_MODEL)), _full((D_MODEL, PROJ_W))],
        out_specs=[pl.BlockSpec((tm, PROJ_W), lambda i: (i, 0)), pl.BlockSpec((tm, D_MODEL), lambda i: (i, 0))],
        compiler_params=pltpu.CompilerParams(dimension_semantics=("arbitrary",), vmem_limit_bytes=VMEM_LIMIT),
        args=(x, nw, w))


def _param_specs():
    return [_full((4, 1536)), _full((1, 1536)), _full((1, 128)), _full((1, 128)), _full((1, 128)),
            _full((1, 1024)), _full((1, 128)), _full((CONF_KERNEL, 512)), _full((1, 512)), _full((1, 512)),
            _full((1, 512))]


def _halo_specs(nc, chunk_of):
    def prev_chunk(b, j):
        return jnp.maximum(b * nc + chunk_of(j) - 1, 0)

    per_xbc = CHUNK // XBC_HALO
    per_cc = CHUNK // CONF_HALO
    return [
        pl.BlockSpec((XBC_HALO, 1536), lambda b, j: (prev_chunk(b, j) * per_xbc + per_xbc - 1, C_XBC // 1536)),
        pl.BlockSpec((CHUNK, 256), lambda b, j: (prev_chunk(b, j), C_K // 256)),
        pl.BlockSpec((CONF_HALO, 1024), lambda b, j: (prev_chunk(b, j) * per_cc + per_cc - 1, C_CA // 1024)),
    ]


def _grid_ends(grid):
    first = lambda: functools.reduce(lambda p, q: p & q, [pl.program_id(i) == 0 for i in range(len(grid))])
    last = lambda: functools.reduce(lambda p, q: p & q, [pl.program_id(i) == n - 1 for i, n in enumerate(grid)])
    return first, last


def _mixer_fwd(x, proj, w_out, params, nb, name, side=None, head=None):
    t = x.shape[0]
    nc = t // nb // CHUNK
    n_head = 0 if head is None else 2

    def body(x_ref, cur_ref, pxbc_ref, pkv_ref, pcc_ref, wo_ref, *rest):
        prm = [r[...] for r in rest[:11]]
        head_in = rest[11:11 + n_head]
        xn_ref, sall_ref, conv_ref = rest[11 + n_head:14 + n_head]
        head_out = rest[14 + n_head:14 + 2 * n_head]
        s_scr = rest[14 + 2 * n_head]
        c = pl.program_id(1)
        not_first = c > 0
        nf = not_first.astype(F32)

        @pl.when(c == 0)
        def _():
            s_scr[...] = jnp.zeros_like(s_scr)

        cw, cb, dtb, alog, dsk, nw, snk, dww, dwb, lnw, lnb = prm
        s_in = s_scr[...]
        sall_ref[0] = s_in
        ssd_conv = _dwconv(jnp.concatenate([pxbc_ref[...] * nf, cur_ref[:, C_XBC:C_XBC + 1536]], axis=0), cw, cb,
                           XBC_HALO)
        y_ssd, s_out = _ssd_part(cur_ref[:, 0:1024], ssd_conv, cur_ref[:, C_DT:C_DT + 128], s_in, dtb, alog, dsk, nw)
        s_scr[...] = s_out
        y_attn = _attn_part(cur_ref[:, 1024:1536], cur_ref[:, C_Q:C_Q + 512], cur_ref[:, C_K:C_K + 256],
                            pkv_ref[...] * nf, snk, _kv_mask(CHUNK, not_first, ATTN_STACK_FWD), ATTN_STACK_FWD)
        conv_out = _dwconv(_conf_glu(cur_ref[:, C_CA:C_CA + 1024], pcc_ref[...] * nf), dww, dwb, CONF_HALO)
        conv_ref[...] = conv_out
        y_conf = _conf_tail(conv_out, cur_ref[:, 1536:2048], lnw, lnb)
        x_next = (x_ref[...] + _dot(y_ssd, wo_ref[0:1024, :]) + _dot(y_attn, wo_ref[1024:1536, :])
                  + _dot(y_conf, wo_ref[1536:2048, :]))
        if head is None:
            xn_ref[...] = x_next
        else:
            fnw_ref, tgt_ref = head_in
            loss_ref, gfnw_ref = head_out

            @pl.when((pl.program_id(0) == 0) & (c == 0))
            def _():
                loss_ref[...] = jnp.zeros_like(loss_ref)
                gfnw_ref[...] = jnp.zeros_like(gfnw_ref)

            y, vjp = jax.vjp(_rmsnorm, x_next, fnw_ref[...])
            err = y - tgt_ref[...]
            loss_ref[...] += 0.5 * jnp.sum(jnp.mean(err * err, axis=-1, keepdims=True), axis=0, keepdims=True)
            xn_ref[...], d_fnw = vjp(err * (1.0 / D_MODEL))
            gfnw_ref[...] += d_fnw

    row = lambda b, j: (b * nc + j, 0)
    grid = (nb, nc)
    return _pallas_with_side(
        body, side, *_grid_ends(grid), 17 + n_head, 3 + n_head, name=name, grid=grid,
        out_shape=[jax.ShapeDtypeStruct((t, D_MODEL), F32),
                   jax.ShapeDtypeStruct((nb * nc, SSD_HEADS * SSD_HEAD_DIM, SSD_STATE), F32),
                   jax.ShapeDtypeStruct((t, 512), F32)]
                  + [jax.ShapeDtypeStruct(s, F32) for s in ((1, 1), (1, D_MODEL))[:n_head]],
        in_specs=[pl.BlockSpec((CHUNK, D_MODEL), row), pl.BlockSpec((CHUNK, PROJ_W), row)]
                 + _halo_specs(nc, lambda j: j) + [_full((MIX_WIDTH, D_MODEL))] + _param_specs()
                 + [_full((1, D_MODEL)), pl.BlockSpec((CHUNK, D_MODEL), row)][:n_head],
        out_specs=[pl.BlockSpec((CHUNK, D_MODEL), row),
                   pl.BlockSpec((1, SSD_HEADS * SSD_HEAD_DIM, SSD_STATE), lambda b, j: (b * nc + j, 0, 0)),
                   pl.BlockSpec((CHUNK, 512), row)] + [_full((1, 1)), _full((1, D_MODEL))][:n_head],
        scratch_shapes=[pltpu.VMEM((SSD_HEADS * SSD_HEAD_DIM, SSD_STATE), F32)],
        compiler_params=pltpu.CompilerParams(dimension_semantics=("arbitrary", "arbitrary"),
                                             vmem_limit_bytes=VMEM_LIMIT),
        args=(x, proj, proj, proj, proj, w_out, *params, *(head or ())))


def _mixer_bwd(dxn, proj, s_all, conv_all, w_out, params, nb, name):
    t = dxn.shape[0]
    nc = t // nb // CHUNK
    n_prm = 11

    def body(dxn_ref, cur_ref, pxbc_ref, pkv_ref, pcc_ref, s_ref, conv_ref, wo_ref, *rest):
        prm = [r[...] for r in rest[:n_prm]]
        dproj_ref, ycat_ref = rest[n_prm:n_prm + 2]
        gprm = rest[n_prm + 2:2 * n_prm + 2]
        ds_scr, pend_xbc, pend_kv, pend_cc = rest[2 * n_prm + 2:]
        b, j = pl.program_id(0), pl.program_id(1)
        c = nc - 1 - j
        not_first = c > 0
        nf = not_first.astype(F32)

        @pl.when((b == 0) & (j == 0))
        def _():
            for r in gprm:
                r[...] = jnp.zeros_like(r)

        @pl.when(j == 0)
        def _():
            ds_scr[...] = jnp.zeros_like(ds_scr)
            pend_xbc[...] = jnp.zeros_like(pend_xbc)
            pend_kv[...] = jnp.zeros_like(pend_kv)
            pend_cc[...] = jnp.zeros_like(pend_cc)

        cw, cb, dtb, alog, dsk, nw, snk, dww, dwb, lnw, lnb = prm
        g_cw, g_cb, g_dtb, g_alog, g_dsk, g_nw, g_snk, g_dww, g_dwb, g_lnw, g_lnb = gprm
        dxn_v = dxn_ref[...]

        def add_tail(d_cur, pending):
            lead = jnp.zeros((CHUNK - pending.shape[0], pending.shape[1]), F32)
            return d_cur + jnp.concatenate([lead, pending], axis=0)

        y, vjp = jax.vjp(_conf_tail, conv_ref[...], cur_ref[:, 1536:2048], lnw, lnb)
        ycat_ref[:, 1536:2048] = y.astype(BF16)
        d_conv, dz, d_lnw, d_lnb = vjp(_dot_nt(dxn_v, wo_ref[1536:2048, :]))
        ext, vjp = jax.vjp(_conf_glu, cur_ref[:, C_CA:C_CA + 1024], pcc_ref[...] * nf)
        d_ext, d_dww, d_dwb = _dwconv_bwd(CONF_HALO, (ext, dww), d_conv)
        dcacg, dpcc = vjp(d_ext)
        dproj_ref[:, 1536:2048] = dz.astype(BF16)
        dproj_ref[:, C_CA:C_CA + 1024] = add_tail(dcacg, pend_cc[...]).astype(BF16)
        pend_cc[...] = dpcc
        for r, g in ((g_dww, d_dww), (g_dwb, d_dwb), (g_lnw, d_lnw), (g_lnb, d_lnb)):
            r[...] += g

        attn = functools.partial(_attn_part, kvmask=_kv_mask(CHUNK, not_first, ATTN_STACK_BWD),
                                 stack=ATTN_STACK_BWD)
        y, vjp = jax.vjp(attn, cur_ref[:, 1024:1536], cur_ref[:, C_Q:C_Q + 512], cur_ref[:, C_K:C_K + 256],
                         pkv_ref[...] * nf, snk)
        ycat_ref[:, 1024:1536] = y.astype(BF16)
        dz, dq, dkv, dpkv, d_snk = vjp(_dot_nt(dxn_v, wo_ref[1024:1536, :]))
        dproj_ref[:, 1024:1536] = dz.astype(BF16)
        dproj_ref[:, C_Q:C_Q + 512] = dq.astype(BF16)
        dproj_ref[:, C_K:C_K + 256] = (dkv + pend_kv[...]).astype(BF16)
        pend_kv[...] = dpkv
        g_snk[...] += d_snk

        ext = jnp.concatenate([pxbc_ref[...] * nf, cur_ref[:, C_XBC:C_XBC + 1536]], axis=0)
        (y, _), vjp = jax.vjp(_ssd_part, cur_ref[:, 0:1024], _dwconv(ext, cw, cb, XBC_HALO),
                              cur_ref[:, C_DT:C_DT + 128], s_ref[0], dtb, alog, dsk, nw)
        ycat_ref[:, 0:1024] = y.astype(BF16)
        dz, d_conv, ddtr, ds_in, d_dtb, d_alog, d_dsk, d_nw = vjp((_dot_nt(dxn_v, wo_ref[0:1024, :]), ds_scr[...]))
        d_ext, d_cw, d_cb = _dwconv_bwd(XBC_HALO, (ext, cw), d_conv)
        dpxbc, dxbc = d_ext[0:XBC_HALO, :], d_ext[XBC_HALO:, :]
        dproj_ref[:, 0:1024] = dz.astype(BF16)
        dproj_ref[:, C_XBC:C_XBC + 1536] = add_tail(dxbc, pend_xbc[...]).astype(BF16)
        dproj_ref[:, C_DT:C_DT + 128] = ddtr.astype(BF16)
        dproj_ref[:, C_DT + 128:PROJ_W] = jnp.zeros((CHUNK, PROJ_W - C_DT - 128), BF16)
        pend_xbc[...] = dpxbc
        ds_scr[...] = ds_in
        for r, g in ((g_cw, d_cw), (g_cb, d_cb), (g_dtb, d_dtb), (g_alog, d_alog), (g_dsk, d_dsk), (g_nw, d_nw)):
            r[...] += g

    row = lambda b, j: (b * nc + nc - 1 - j, 0)
    prm_shapes = [(4, 1536), (1, 1536), (1, 128), (1, 128), (1, 128), (1, 1024), (1, 128), (CONF_KERNEL, 512),
                  (1, 512), (1, 512), (1, 512)]
    grid = (nb, nc)
    return _pallas_with_side(
        body, None, *_grid_ends(grid), 8 + n_prm, 2 + n_prm, name=name, grid=grid,
        out_shape=[jax.ShapeDtypeStruct((t, PROJ_W), BF16), jax.ShapeDtypeStruct((t, MIX_WIDTH), BF16)]
                  + [jax.ShapeDtypeStruct(s, F32) for s in prm_shapes],
        in_specs=[pl.BlockSpec((CHUNK, D_MODEL), row), pl.BlockSpec((CHUNK, PROJ_W), row)]
                 + _halo_specs(nc, lambda j: nc - 1 - j)
                 + [pl.BlockSpec((1, SSD_HEADS * SSD_HEAD_DIM, SSD_STATE), lambda b, j: (b * nc + nc - 1 - j, 0, 0)),
                    pl.BlockSpec((CHUNK, 512), row), _full((MIX_WIDTH, D_MODEL))] + _param_specs(),
        out_specs=[pl.BlockSpec((CHUNK, PROJ_W), row), pl.BlockSpec((CHUNK, MIX_WIDTH), row)]
                  + [_full(s) for s in prm_shapes],
        scratch_shapes=[pltpu.VMEM((SSD_HEADS * SSD_HEAD_DIM, SSD_STATE), F32), pltpu.VMEM((XBC_HALO, 1536), F32),
                        pltpu.VMEM((CHUNK, 256), F32), pltpu.VMEM((CONF_HALO, 1024), F32)],
        compiler_params=pltpu.CompilerParams(dimension_semantics=("arbitrary", "arbitrary"),
                                             vmem_limit_bytes=VMEM_LIMIT),
        args=(dxn, proj, proj, proj, proj, s_all, conv_all, w_out, *params))


def _gw_out(y_cat, dxn, name):
    t = y_cat.shape[0]
    tk = 512

    def body(y_ref, dxn_ref, out_ref, acc):
        k = pl.program_id(0)

        @pl.when(k == 0)
        def _():
            acc[...] = jnp.zeros_like(acc)

        acc[...] += _dot_tn(y_ref[...], dxn_ref[...])

        @pl.when(k == t // tk - 1)
        def _():
            out_ref[...] = acc[...].astype(BF16)

    out = pl.pallas_call(
        body, name=name, grid=(t // tk,),
        out_shape=jax.ShapeDtypeStruct((MIX_WIDTH, D_MODEL), BF16),
        in_specs=[pl.BlockSpec((tk, MIX_WIDTH), lambda k: (k, 0)), pl.BlockSpec((tk, D_MODEL), lambda k: (k, 0))],
        out_specs=_full((MIX_WIDTH, D_MODEL)),
        scratch_shapes=[pltpu.VMEM((MIX_WIDTH, D_MODEL), F32)],
        compiler_params=pltpu.CompilerParams(dimension_semantics=("arbitrary",), vmem_limit_bytes=VMEM_LIMIT),
    )(y_cat, dxn)
    return out.reshape(N_DEV, MIX_WIDTH // N_DEV, D_MODEL)


def _inproj_bwd_x(dproj, w, x, nw, dxn, name, side=None):
    t = x.shape[0]
    tm = 256

    def body(dp_ref, w_ref, x_ref, nw_ref, dxn_ref, dx_ref, gnw_ref):
        @pl.when(pl.program_id(0) == 0)
        def _():
            gnw_ref[...] = jnp.zeros_like(gnw_ref)

        dh = jnp.zeros((tm, D_MODEL), F32)
        for j in range(N_COL_TILES):
            sl = slice(j * COL_TILE, (j + 1) * COL_TILE)
            dh = dh + _dot_nt(dp_ref[:, sl], w_ref[:, sl])
        _, vjp = jax.vjp(_rmsnorm, x_ref[...], nw_ref[...])
        dx, dnw = vjp(dh)
        dx_ref[...] = dxn_ref[...] + dx
        gnw_ref[...] += dnw

    tok = lambda i: (i, 0)
    grid = (t // tm,)
    return _pallas_with_side(
        body, side, *_grid_ends(grid), 5, 2, name=name, grid=grid,
        out_shape=[jax.ShapeDtypeStruct((t, D_MODEL), F32), jax.ShapeDtypeStruct((1, D_MODEL), F32)],
        in_specs=[pl.BlockSpec((tm, PROJ_W), tok), _full((D_MODEL, PROJ_W)), pl.BlockSpec((tm, D_MODEL), tok),
                  _full((1, D_MODEL)), pl.BlockSpec((tm, D_MODEL), tok)],
        out_specs=[pl.BlockSpec((tm, D_MODEL), tok), _full((1, D_MODEL))],
        compiler_params=pltpu.CompilerParams(dimension_semantics=("arbitrary",), vmem_limit_bytes=VMEM_LIMIT),
        args=(dproj, w, x, nw, dxn))


def _inproj_bwd_w(h, dproj, name, side=None):
    t = h.shape[0]
    tk = 512

    def body(h_ref, dp_ref, gw_ref):
        @pl.when(pl.program_id(1) == 0)
        def _():
            gw_ref[...] = jnp.zeros_like(gw_ref)

        gw_ref[...] += _dot_tn(h_ref[...], dp_ref[...])

    grid = (N_COL_TILES, t // tk)
    return _pallas_with_side(
        body, side, *_grid_ends(grid), 2, 1, name=name, grid=grid,
        out_shape=[jax.ShapeDtypeStruct((D_MODEL, PROJ_W), F32)],
        in_specs=[pl.BlockSpec((tk, D_MODEL), lambda n, k: (k, 0)), pl.BlockSpec((tk, COL_TILE), lambda n, k: (k, n))],
        out_specs=[pl.BlockSpec((D_MODEL, COL_TILE), lambda n, k: (0, n))],
        compiler_params=pltpu.CompilerParams(dimension_semantics=("arbitrary", "arbitrary"),
                                             vmem_limit_bytes=VMEM_LIMIT),
        args=(h, dproj))


def _repack_runs():
    pieces = ((0, 2048, C_Z), (2048, 3584, C_XBC), (3584, 3600, C_DT), (3600, 4368, C_Q), (4368, D_IN_PROJ, C_CA))
    per = D_IN_PROJ // N_DEV
    runs = []
    for j in range(N_DEV):
        lo, hi = per * j, per * (j + 1)
        for a, b, dst in pieces:
            s, e = max(lo, a), min(hi, b)
            if s < e:
                runs.append((j, s - lo, e - lo, dst + s - a))
    return runs


def _repack_w_in(g, name):
    tr = 256

    def body(g_ref, o_ref):
        for j, a, b, dst in _repack_runs():
            o_ref[:, dst:dst + b - a] = g_ref[j, :, a:b]
        o_ref[:, C_DT + 16:PROJ_W] = jnp.zeros((tr, PROJ_W - C_DT - 16), g.dtype)

    return pl.pallas_call(
        body, name=name, grid=(D_MODEL // tr,),
        out_shape=jax.ShapeDtypeStruct((D_MODEL, PROJ_W), g.dtype),
        in_specs=[pl.BlockSpec((N_DEV, tr, D_IN_PROJ // N_DEV), lambda i: (0, i, 0))],
        out_specs=pl.BlockSpec((tr, PROJ_W), lambda i: (i, 0)),
        compiler_params=pltpu.CompilerParams(dimension_semantics=("arbitrary",)),
    )(g)


def _unpack_gw_in(g, name):
    tr = 256

    def body(g_ref, o_ref):
        for j, a, b, dst in _repack_runs():
            o_ref[j, :, a:b] = g_ref[:, dst:dst + b - a].astype(BF16)

    return pl.pallas_call(
        body, name=name, grid=(D_MODEL // tr,),
        out_shape=jax.ShapeDtypeStruct((N_DEV, D_MODEL, D_IN_PROJ // N_DEV), BF16),
        in_specs=[pl.BlockSpec((tr, PROJ_W), lambda i: (i, 0))],
        out_specs=pl.BlockSpec((N_DEV, tr, D_IN_PROJ // N_DEV), lambda i: (0, i, 0)),
        compiler_params=pltpu.CompilerParams(dimension_semantics=("arbitrary",)),
    )(g)


def _pair_sum(parts, name):
    n_dev, r, cdim = parts.shape
    n_chip = n_dev // 2
    by_chip = parts.reshape(n_chip, 2, r, cdim)

    def swap_body(p_ref, got_ref, send_sem, recv_sem):
        x, y, c = _my_place()
        cp = pltpu.make_async_remote_copy(
            src_ref=p_ref.at[:, pl.ds(1 - c, 1)], dst_ref=got_ref, send_sem=send_sem, recv_sem=recv_sem,
            device_id=(x, y, 1 - c), device_id_type=pl.DeviceIdType.MESH)
        cp.start()
        cp.wait()

    any_spec = pl.BlockSpec(memory_space=pl.ANY)
    got = pl.pallas_call(
        swap_body, name=name + "_swap",
        out_shape=jax.ShapeDtypeStruct((n_chip, 1, r, cdim), parts.dtype),
        in_specs=[any_spec], out_specs=any_spec,
        scratch_shapes=[pltpu.SemaphoreType.DMA, pltpu.SemaphoreType.DMA],
    )(by_chip)

    tr = 256

    def add_body(core_ref, p_ref, got_ref, o_ref):
        o_ref[0] = (p_ref[0, 0].astype(F32) + got_ref[0, 0].astype(F32)).astype(o_ref.dtype)

    return pl.pallas_call(
        add_body, name=name + "_add",
        grid_spec=pltpu.PrefetchScalarGridSpec(
            num_scalar_prefetch=1, grid=(n_chip, r // tr),
            in_specs=[pl.BlockSpec((1, 1, tr, cdim), lambda k, i, core: (k, core[0], i, 0)),
                      pl.BlockSpec((1, 1, tr, cdim), lambda k, i, core: (k, 0, i, 0))],
            out_specs=pl.BlockSpec((1, tr, cdim), lambda k, i, core: (k, i, 0))),
        out_shape=jax.ShapeDtypeStruct((n_chip, r, cdim), parts.dtype),
        compiler_params=pltpu.CompilerParams(dimension_semantics=("arbitrary", "arbitrary")),
    )(lax.axis_index("c").astype(jnp.int32).reshape(1), by_chip, got)


def _adamw(w, g, m, v):
    m = ADAM_B1 * m + (1.0 - ADAM_B1) * g
    v = ADAM_B2 * v + (1.0 - ADAM_B2) * jnp.square(g)
    m_hat = m / (1.0 - ADAM_B1 ** ADAM_STEP)
    v_hat = v / (1.0 - ADAM_B2 ** ADAM_STEP)
    delta = -ADAM_LR * (m_hat / (jnp.sqrt(v_hat) + ADAM_EPS) + ADAM_WD * w)
    return delta, m, v


def _reduce_adamw(parts, w, m, v, tr, name):
    depth = len(parts)
    p, r, cdim = parts[0].shape
    n_blk = r // tr

    def body(*refs):
        p_refs = refs[:depth]
        w_ref, m_ref, v_ref, g_ref, d_ref, nm_ref, nv_ref = refs[depth:]
        for layer in range(depth):
            @pl.when(pl.program_id(0) == layer)
            def _(p_ref=p_refs[layer]):
                g = p_ref[0].astype(F32)
                for i in range(1, p):
                    g = g + p_ref[i].astype(F32)
                g_ref[0] = g
                d_ref[0], nm_ref[0], nv_ref[0] = _adamw(w_ref[0], g, m_ref[0], v_ref[0])

    def parts_spec(layer):
        return pl.BlockSpec((p, tr, cdim), lambda d, i: (0, jnp.clip(i + (d - layer) * n_blk, 0, n_blk - 1), 0))

    blk = pl.BlockSpec((1, tr, cdim), lambda d, i: (d, i, 0))
    return pl.pallas_call(
        body, name=name, grid=(depth, n_blk),
        out_shape=[jax.ShapeDtypeStruct(w.shape, F32)] * 4,
        in_specs=[parts_spec(layer) for layer in range(depth)] + [blk, blk, blk],
        out_specs=[blk] * 4,
        compiler_params=pltpu.CompilerParams(dimension_semantics=("arbitrary", "arbitrary"),
                                             vmem_limit_bytes=VMEM_LIMIT),
    )(*parts, w, m, v)


def _adamw_small(ssum, entries, name):
    direct = [e[3] for e in entries if not isinstance(e[3], list)]
    n_direct = len(direct)

    def body(*refs):
        ssum_ref, direct_refs = refs[0], list(refs[1:1 + n_direct])
        ins = refs[1 + n_direct:1 + n_direct + 3 * len(entries)]
        outs = refs[1 + n_direct + 3 * len(entries):]
        for k, (w, _, _, grad) in enumerate(entries):
            w_ref, m_ref, v_ref = ins[3 * k:3 * k + 3]
            g_ref, d_ref, nm_ref, nv_ref = outs[4 * k:4 * k + 4]
            if isinstance(grad, list):
                for row, off in enumerate(grad):
                    rows = slice(row, row + 1)
                    g = ssum_ref[:, off:off + w.shape[1]]
                    g_ref[rows, :] = g
                    d_ref[rows, :], nm_ref[rows, :], nv_ref[rows, :] = _adamw(w_ref[rows, :], g, m_ref[rows, :],
                                                                              v_ref[rows, :])
            else:
                g = direct_refs.pop(0)[...]
                g_ref[...] = g
                d_ref[...], nm_ref[...], nv_ref[...] = _adamw(w_ref[...], g, m_ref[...], v_ref[...])

    vmem = pl.BlockSpec(memory_space=pltpu.VMEM)
    args = [ssum] + direct + [a for e in entries for a in e[:3]]
    res = pl.pallas_call(
        body, name=name,
        out_shape=[jax.ShapeDtypeStruct(e[0].shape, F32) for e in entries for _ in range(4)],
        in_specs=[vmem] * len(args), out_specs=[vmem] * (4 * len(entries)),
    )(*args)
    return [res[4 * k:4 * k + 4] for k in range(len(entries))]


def _reduce_adamw_cols(parts, w, m, v, name):
    depth = len(parts)
    _, r, cdim = parts[0].shape
    tc = 512

    def body(*refs):
        p_refs = refs[:depth]
        w_ref, m_ref, v_ref, g_ref, d_ref, nm_ref, nv_ref = refs[depth:]
        for layer in range(depth):
            g = p_refs[layer][0].astype(F32)
            for i in range(1, parts[layer].shape[0]):
                g = g + p_refs[layer][i].astype(F32)
            g = g.T
            g_ref[:, layer, :] = g
            d_ref[:, layer, :], nm_ref[:, layer, :], nv_ref[:, layer, :] = _adamw(
                w_ref[:, layer, :], g, m_ref[:, layer, :], v_ref[:, layer, :])

    view = lambda a: jnp.transpose(a, (2, 0, 1))
    blk = pl.BlockSpec((cdim, depth, tc), lambda i: (0, 0, i))
    outs = pl.pallas_call(
        body, name=name, grid=(r // tc,),
        out_shape=[jax.ShapeDtypeStruct((cdim, depth, r), F32)] * 4,
        in_specs=[pl.BlockSpec((a.shape[0], tc, cdim), lambda i: (0, i, 0)) for a in parts] + [blk, blk, blk],
        out_specs=[blk] * 4,
        compiler_params=pltpu.CompilerParams(dimension_semantics=("arbitrary",), vmem_limit_bytes=VMEM_LIMIT),
    )(*parts, view(w), view(m), view(v))
    return [jnp.transpose(o, (1, 2, 0)) for o in outs]


def _pad_lanes(v, width=128):
    return jnp.pad(v.reshape(1, -1), ((0, 0), (0, width - v.shape[-1])))


SMALL_FIELDS = (("norm_w", 1024), ("conv_b", 1536), ("dt_bias", 128), ("a_log", 128), ("d_skip", 128),
                ("ssd_norm_w", 1024), ("sinks", 128), ("dw_b", 512), ("ln_w", 512), ("ln_b", 512))


def kernel(x, norm_w, w_in, ssd_conv_w, ssd_conv_b, ssd_dt_bias, ssd_a_log, ssd_d, ssd_norm_w, attn_sinks, conf_dw_w, conf_dw_b, conf_ln_w, conf_ln_b, w_out, final_norm_w, loss_target, m_norm_w, m_w_in, m_ssd_conv_w, m_ssd_conv_b, m_ssd_dt_bias, m_ssd_a_log, m_ssd_d, m_ssd_norm_w, m_attn_sinks, m_conf_dw_w, m_conf_dw_b, m_conf_ln_w, m_conf_ln_b, m_w_out, m_final_norm_w, v_norm_w, v_w_in, v_ssd_conv_w, v_ssd_conv_b, v_ssd_dt_bias, v_ssd_a_log, v_ssd_d, v_ssd_norm_w, v_attn_sinks, v_conf_dw_w, v_conf_dw_b, v_conf_ln_w, v_conf_ln_b, v_w_out, v_final_norm_w):
    nb, seq, _ = x.shape
    depth = norm_w.shape[0]
    t = nb * seq

    w_in_bf, w_out_bf = w_in.astype(BF16), w_out.astype(BF16)
    g_win0, g_cw, g_dw = _all_gather([w_in_bf[0], ssd_conv_w, conf_dw_w], "gather_weights")
    w_in_full = [_repack_w_in(g_win0, "repack_w_in_0")]
    w_out_full = []
    conv_w_full = [jnp.transpose(g_cw[:, l], (1, 0, 2)).reshape(4, 1536) for l in range(depth)]
    dw_w_full = [jnp.transpose(g_dw[:, l], (1, 0, 2)).reshape(CONF_KERNEL, 512) for l in range(depth)]

    def layer_params(l):
        return [conv_w_full[l], ssd_conv_b[l].reshape(1, -1), _pad_lanes(ssd_dt_bias[l]), _pad_lanes(ssd_a_log[l]),
                _pad_lanes(ssd_d[l]), ssd_norm_w[l].reshape(1, -1), _pad_lanes(attn_sinks[l]), dw_w_full[l],
                conf_dw_b[l].reshape(1, -1), conf_ln_w[l].reshape(1, -1), conf_ln_b[l].reshape(1, -1)]

    xs = [x.reshape(t, D_MODEL)]
    saved = []
    for l in range(depth):
        (proj, h), gathered = _inproj_fwd(xs[l], norm_w[l].reshape(1, -1), w_in_full[l], f"inproj_fwd_{l}",
                                          (GATHER, [w_out_bf[l]]))
        w_out_full.append(gathered[0].reshape(MIX_WIDTH, D_MODEL))
        last = l + 1 == depth
        side = None if last else (GATHER, [w_in_bf[l + 1]])
        head = (final_norm_w.reshape(1, -1), loss_target.reshape(t, D_MODEL)) if last else None
        res, gathered = _mixer_fwd(xs[l], proj, w_out_full[l], layer_params(l), nb, f"mixer_fwd_{l}", side, head)
        if side:
            w_in_full.append(_repack_w_in(gathered[0], f"repack_w_in_{l + 1}"))
        saved.append((proj, h, res[1:3]))
        xs.append(res[0])
    dx, loss_part, g_fnw = res[0], res[3], res[4]

    small_rows, shard_rows = [None] * depth, [None] * depth
    received = [None] * depth
    for l in reversed(range(depth)):
        proj, h, kept = saved[l]
        res, _ = _mixer_bwd(dx, proj, *kept, w_out_full[l], layer_params(l), nb, f"mixer_bwd_{l}")
        dproj, y_cat = res[0], res[1]
        g_cw_l, g_cb, g_dtb, g_alog, g_dsk, g_nw, g_snk, g_dww, g_dwb, g_lnw, g_lnb = res[2:]
        gw_out_parts = _gw_out(y_cat, dx, f"gw_out_{l}")
        (gw_in,), got_out = _inproj_bwd_w(h, dproj, f"inproj_bwd_w_{l}", (SCATTER, [gw_out_parts]))
        chip_parts = _pair_sum(_unpack_gw_in(gw_in, f"unpack_gw_in_{l}"), f"pair_sum_{l}")
        (dx, g_norm), got_in = _inproj_bwd_x(dproj, w_in_full[l], xs[l], norm_w[l].reshape(1, -1), dx,
                                             f"inproj_bwd_x_{l}", (CHIP_SCATTER, [chip_parts]))
        received[l] = [got_in[0], got_out[0]]
        small_rows[l] = [g_norm, g_cb, g_dtb, g_alog, g_dsk, g_nw, g_snk, g_dwb, g_lnw, g_lnb]
        shard_rows[l] = [jnp.transpose(g.reshape(g.shape[0], N_DEV, -1), (1, 0, 2)).reshape(N_DEV, -1)
                         for g in (g_cw_l, g_dww)]
    grad_x = dx.reshape(nb, seq, D_MODEL)

    small = jnp.concatenate([piece for l in range(depth) for piece in small_rows[l]]
                            + [g_fnw, _pad_lanes(loss_part)], axis=1)
    shard_small = jnp.concatenate([piece for l in range(depth) for piece in shard_rows[l]], axis=1)
    ssum, shard_sum = _exchange_small(small, shard_small.reshape(N_DEV, 1, -1), "exchange_small")
    loss = ssum[0, small.shape[1] - 128]

    g_w_in, d_w_in, nm_w_in, nv_w_in = _reduce_adamw_cols([received[l][0] for l in range(depth)], w_in, m_w_in,
                                                          v_w_in, "adamw_w_in")
    g_w_out, d_w_out, nm_w_out, nv_w_out = _reduce_adamw([received[l][1] for l in range(depth)], w_out, m_w_out,
                                                         v_w_out, 256, "adamw_w_out")

    per_layer = sum(n for _, n in SMALL_FIELDS)
    given = {"norm_w": (norm_w, m_norm_w, v_norm_w), "conv_b": (ssd_conv_b, m_ssd_conv_b, v_ssd_conv_b),
             "dt_bias": (ssd_dt_bias, m_ssd_dt_bias, v_ssd_dt_bias), "a_log": (ssd_a_log, m_ssd_a_log, v_ssd_a_log),
             "d_skip": (ssd_d, m_ssd_d, v_ssd_d), "ssd_norm_w": (ssd_norm_w, m_ssd_norm_w, v_ssd_norm_w),
             "sinks": (attn_sinks, m_attn_sinks, v_attn_sinks), "dw_b": (conf_dw_b, m_conf_dw_b, v_conf_dw_b),
             "ln_w": (conf_ln_w, m_conf_ln_w, v_conf_ln_w), "ln_b": (conf_ln_b, m_conf_ln_b, v_conf_ln_b)}
    entries = []
    off = 0
    for fname, n in SMALL_FIELDS:
        entries.append((*given[fname], [l * per_layer + off for l in range(depth)]))
        off += n
    n_cw, n_dw = ssd_conv_w[0].size, conf_dw_w[0].size
    per_layer_shard = n_cw + n_dw
    cw_grad = jnp.stack([shard_sum[0, l * per_layer_shard:l * per_layer_shard + n_cw].reshape(ssd_conv_w.shape[1:])
                         for l in range(depth)], axis=0)
    dw_grad = jnp.stack([shard_sum[0, l * per_layer_shard + n_cw:(l + 1) * per_layer_shard]
                         .reshape(conf_dw_w.shape[1:]) for l in range(depth)], axis=0)
    entries.append((ssd_conv_w, m_ssd_conv_w, v_ssd_conv_w, cw_grad))
    entries.append((conf_dw_w, m_conf_dw_w, v_conf_dw_w, dw_grad))
    entries.append((final_norm_w.reshape(1, -1), m_final_norm_w.reshape(1, -1), v_final_norm_w.reshape(1, -1),
                    [depth * per_layer]))
    sm = _adamw_small(ssum, entries, "adamw_small")
    sm = {k: quad for k, quad in zip([f for f, _ in SMALL_FIELDS] + ["conv_w", "dw_w", "final"], sm)}

    def outputs(i, big_in_i, big_out_i):
        return [sm["norm_w"][i], big_in_i, sm["conv_w"][i], sm["conv_b"][i], sm["dt_bias"][i], sm["a_log"][i],
                sm["d_skip"][i], sm["ssd_norm_w"][i], sm["sinks"][i], sm["dw_w"][i], sm["dw_b"][i], sm["ln_w"][i],
                sm["ln_b"][i], big_out_i, sm["final"][i].reshape(-1)]

    return (loss, grad_x, *outputs(0, g_w_in, g_w_out), *outputs(1, d_w_in, d_w_out),
            *outputs(2, nm_w_in, nm_w_out), *outputs(3, nv_w_in, nv_w_out))
```

```python
import functools

import jax
import jax.numpy as jnp
from jax import lax
from jax.experimental import pallas as pl
from jax.experimental.pallas import tpu as pltpu

F32 = jnp.float32
BF16 = jnp.bfloat16
N_DEV = 8
EPS = 1e-5

D_MODEL = 1024
CHUNK = 128
SSD_HEADS = 16
SSD_HEAD_DIM = 64
SSD_STATE = 128
ATTN_HEADS = 8
ATTN_HEAD_DIM = 64
CONF_KERNEL = 31
MIX_WIDTH = 2048
D_IN_PROJ = 5392
C_Z = 0
C_CA = 2048
C_XBC = 3072
C_Q = 4608
C_K = 5120
C_DT = 5376
PROJ_W = 5632
N_COL_TILES = 4
COL_TILE = PROJ_W // N_COL_TILES
XBC_HALO = 8
CONF_HALO = 32
ATTN_STACK_FWD = 2
ATTN_STACK_BWD = 4
VMEM_LIMIT = 56 * 1024 * 1024

ADAM_LR = 0.001
ADAM_B1 = 0.9
ADAM_B2 = 0.999
ADAM_EPS = 1e-08
ADAM_WD = 0.01
ADAM_STEP = 10


def _silu(v):
    return v * jax.nn.sigmoid(v)


def _softplus(v):
    return jnp.maximum(v, 0.0) + jnp.log1p(jnp.exp(-jnp.abs(v)))


def _rmsnorm(v, w):
    return v * lax.rsqrt(jnp.mean(v * v, axis=-1, keepdims=True) + EPS) * w


def _dot(a, b):
    return jnp.dot(a.astype(BF16), b.astype(BF16), preferred_element_type=F32)


def _dot_nt(a, b):
    return lax.dot_general(a.astype(BF16), b.astype(BF16), (((1,), (1,)), ((), ())), preferred_element_type=F32)


def _dot_tn(a, b):
    return lax.dot_general(a.astype(BF16), b.astype(BF16), (((0,), (0,)), ((), ())), preferred_element_type=F32)


def _taps(ext, offs, out_len, w=None, g=None):
    n_rows, n_cols = ext.shape
    by_shift = {}
    for t, off in enumerate(offs):
        by_shift.setdefault(off % 8, []).append((t, off))
    for r, taps in by_shift.items():
        assert max(off for _, off in taps) - r + out_len <= n_rows - r
    accs = []
    sums = [[None] * (n_cols // 128) for _ in offs]
    for blk in range(n_cols // 128):
        cs = slice(blk * 128, (blk + 1) * 128)
        e = ext[:, cs]
        acc = None
        for r, taps in by_shift.items():
            shifted = e if r == 0 else pltpu.roll(e, n_rows - r, axis=0)
            for t, off in taps:
                window = shifted[off - r:off - r + out_len, :]
                if w is not None:
                    term = w[t:t + 1, cs] * window
                    acc = term if acc is None else acc + term
                if g is not None:
                    sums[t][blk] = jnp.sum(g[:, cs] * window, axis=0, keepdims=True)
        accs.append(acc)
    if w is not None:
        return jnp.concatenate(accs, axis=1)
    return jnp.concatenate([jnp.concatenate(row, axis=1) for row in sums], axis=0)


@functools.partial(jax.custom_vjp, nondiff_argnums=(3,))
def _dwconv(ext, w, b, halo):
    kk = w.shape[0]
    return b + _taps(ext, [halo - (kk - 1) + t for t in range(kk)], ext.shape[0] - halo, w=w)


def _dwconv_fwd(ext, w, b, halo):
    return _dwconv(ext, w, b, halo), (ext, w)


def _dwconv_bwd(halo, res, g):
    ext, w = res
    kk = w.shape[0]
    offs = [halo - (kk - 1) + t for t in range(kk)]
    dw = _taps(ext, offs, ext.shape[0] - halo, g=g)
    zeros = jnp.zeros((halo, g.shape[1]), g.dtype)
    gp = jnp.concatenate([zeros, g, zeros], axis=0)
    dext = _taps(gp, [halo - off for off in offs], ext.shape[0], w=w)
    return dext, dw, jnp.sum(g, axis=0, keepdims=True)


_dwconv.defvjp(_dwconv_fwd, _dwconv_bwd)


def _ssd_part(z_ssd, conv_out, dtr, s_in, dtb, alog, dsk, nw):
    qn = conv_out.shape[0]
    nh = SSD_HEADS
    per_group = nh // 2
    n_pair = nh // 2
    xa = _silu(conv_out)
    xs = xa[:, 0:1024]
    dt = _softplus(dtr + dtb)
    a = dt * (-jnp.exp(alog))
    rows = lax.broadcasted_iota(jnp.int32, (qn, qn), 0)
    cols = lax.broadcasted_iota(jnp.int32, (qn, qn), 1)
    causal = rows >= cols
    low = cols < SSD_HEAD_DIM
    a_cs = jnp.dot(causal.astype(F32), a, precision=lax.Precision.HIGHEST, preferred_element_type=F32)
    a_cs_t = a_cs.T
    bgs = [xa[:, 1024 + g * 128:1024 + (g + 1) * 128] for g in range(2)]
    cgs = [xa[:, 1280 + g * 128:1280 + (g + 1) * 128] for g in range(2)]
    cbms = [_dot_nt(cgs[g], bgs[g]) for g in range(2)]
    colb = [jnp.broadcast_to(a_cs[:, h:h + 1], (qn, qn)) for h in range(nh)]
    dtb_wide = [jnp.broadcast_to(dt[:, h:h + 1], (qn, qn)) for h in range(nh)]
    lmats = [jnp.exp(jnp.where(causal, colb[h] - a_cs_t[h:h + 1, :], -jnp.inf)) for h in range(nh)]
    ms = [cbms[h // per_group] * lmats[h] for h in range(nh)]
    x_pair = [xs[:, p * 128:(p + 1) * 128] for p in range(n_pair)]
    xdt = [x_pair[p] * jnp.where(low, dtb_wide[2 * p], dtb_wide[2 * p + 1]) for p in range(n_pair)]
    x_lo = [jnp.where(low, xdt[p], 0.0) for p in range(n_pair)]
    x_hi = [jnp.where(low, 0.0, xdt[p]) for p in range(n_pair)]
    y_diag = [_dot(ms[2 * p], x_lo[p]) + _dot(ms[2 * p + 1], x_hi[p]) for p in range(n_pair)]
    col_pair = [jnp.where(low, colb[2 * p], colb[2 * p + 1]) for p in range(n_pair)]
    last_pair = [jnp.broadcast_to(col_pair[p][qn - 1:qn, :], (qn, qn)) for p in range(n_pair)]
    ecol = [jnp.exp(col_pair[p]) for p in range(n_pair)]
    xw = [xdt[p] * jnp.exp(last_pair[p] - col_pair[p]) for p in range(n_pair)]
    y_off, st = [], []
    for g in range(2):
        ps = range(g * n_pair // 2, (g + 1) * n_pair // 2)
        y_off.append(_dot_nt(cgs[g], s_in[g * 512:(g + 1) * 512, :]) * jnp.concatenate([ecol[p] for p in ps], axis=1))
        st.append(_dot_tn(jnp.concatenate([xw[p] for p in ps], axis=1), bgs[g]))
    e_last = jnp.exp(jnp.broadcast_to(a_cs_t[:, qn - 1:qn], (qn, SSD_STATE)))
    scale = jnp.concatenate([jnp.broadcast_to(e_last[h:h + 1, :], (64, SSD_STATE)) for h in range(nh)], axis=0)
    s_out = scale * s_in + jnp.concatenate(st, axis=0)
    d_wide = jnp.concatenate([jnp.broadcast_to(dsk[:, h:h + 1], (1, 64)) for h in range(nh)], axis=1)
    y = jnp.concatenate(y_diag, axis=1) + jnp.concatenate(y_off, axis=1) + d_wide * xs
    gated = y * _silu(z_ssd)
    halves = []
    for g in range(2):
        gg = gated[:, g * 512:(g + 1) * 512]
        halves.append(gg * lax.rsqrt(jnp.mean(gg * gg, axis=-1, keepdims=True) + EPS))
    return jnp.concatenate(halves, axis=1) * nw, s_out


def _attn_part(z_attn, q, kv, p_kv, snk, kv_bias, stack):
    qn = q.shape[0]
    kk = jnp.concatenate([p_kv[:, 0:128], kv[:, 0:128]], axis=0)
    vv = jnp.concatenate([p_kv[:, 128:256], kv[:, 128:256]], axis=0)
    units = range(ATTN_HEADS // stack)
    heads = [range(u * stack, (u + 1) * stack) for u in units]
    kv_of = [u * stack // (ATTN_HEADS // 2) for u in units]
    k_of = [kk[:, g * 64:(g + 1) * 64] for g in kv_of]
    v_of = [vv[:, g * 64:(g + 1) * 64] for g in kv_of]
    qs = [jnp.concatenate([q[:, h * 64:(h + 1) * 64] for h in heads[u]], axis=0) for u in units]
    sk = [jnp.concatenate([jnp.broadcast_to(snk[:, h:h + 1], (qn, 1)) for h in heads[u]], axis=0) for u in units]
    s = [_dot_nt(qs[u], k_of[u]) * (ATTN_HEAD_DIM ** -0.5) + kv_bias for u in units]
    m = [lax.stop_gradient(jnp.maximum(jnp.max(s[u], axis=-1, keepdims=True), sk[u])) for u in units]
    e = [jnp.exp(s[u] - m[u]) for u in units]
    r_den = [1.0 / (jnp.sum(e[u], axis=-1, keepdims=True) + jnp.exp(sk[u] - m[u])) for u in units]
    o = [_dot(e[u], v_of[u]) * r_den[u] for u in units]
    outs = [o[u][i * qn:(i + 1) * qn, :] for u in units for i in range(stack)]
    return jnp.concatenate(outs, axis=1) * _silu(z_attn)


def _conf_glu(cacg, p_cc):
    c0 = cacg[:, 0:512] * jax.nn.sigmoid(cacg[:, 512:1024])
    pc0 = p_cc[:, 0:512] * jax.nn.sigmoid(p_cc[:, 512:1024])
    return jnp.concatenate([pc0, c0], axis=0)


def _conf_tail(conv_out, z_conf, lnw, lnb):
    xc = conv_out - jnp.mean(conv_out, axis=-1, keepdims=True)
    yln = xc * lax.rsqrt(jnp.mean(xc * xc, axis=-1, keepdims=True) + EPS) * lnw + lnb
    return _silu(yln) * _silu(z_conf)


def _kv_bias(bias_scr, first_step, not_first):
    _, rows, cols = bias_scr.shape
    qn = cols // 2

    @pl.when(first_step)
    def _():
        ii = lax.broadcasted_iota(jnp.int32, (rows, cols), 0) & (qn - 1)
        jj = lax.broadcasted_iota(jnp.int32, (rows, cols), 1)
        d = jj - ii
        band = (d >= 1) & (d <= qn)
        bias_scr[0] = jnp.where(band & (jj >= qn), 0.0, -jnp.inf)
        bias_scr[1] = jnp.where(band, 0.0, -jnp.inf)

    return bias_scr[not_first.astype(jnp.int32)]


def _my_place():
    return lax.axis_index("x"), lax.axis_index("y"), lax.axis_index("c")


def _all_gather(arrs, name):
    n = len(arrs)

    def body(*refs):
        ins, outs = refs[:n], refs[n:2 * n]
        send_sems, recv_sems, local_sems = refs[2 * n:]
        x, y, c = _my_place()
        me, sibling = (x, y, c), (x, y, 1 - c)
        chips = [(1 - x, y), (x, 1 - y), (1 - x, 1 - y)]

        def slot(a, p):
            return outs[a].at[4 * p[0] + 2 * p[1] + p[2]]

        def copy(a, kk, block, to, src=None):
            return pltpu.make_async_remote_copy(
                src_ref=slot(a, block) if src is None else src, dst_ref=slot(a, block),
                send_sem=send_sems.at[a, kk], recv_sem=recv_sems.at[a, kk],
                device_id=to, device_id_type=pl.DeviceIdType.MESH)

        mine = [pltpu.make_async_copy(ins[a], slot(a, me), local_sems.at[a]) for a in range(n)]
        for cp in mine:
            cp.start()
        first = []
        for a in range(n):
            first.append(copy(a, 0, me, sibling, src=ins[a]))
            first += [copy(a, 1 + j, me, (*chip, c), src=ins[a]) for j, chip in enumerate(chips)]
        for cp in first:
            cp.start()
        passed = []
        for j, chip in enumerate(chips):
            for a in range(n):
                copy(a, 1 + j, (*chip, c), me).wait_recv()
                fwd = copy(a, 4 + j, (*chip, c), sibling)
                fwd.start()
                passed.append(fwd)
        for a in range(n):
            copy(a, 0, sibling, me).wait_recv()
            for j, chip in enumerate(chips):
                copy(a, 4 + j, (*chip, 1 - c), me).wait_recv()
        for cp in first + passed:
            cp.wait_send()
        for cp in mine:
            cp.wait()

    any_spec = pl.BlockSpec(memory_space=pl.ANY)
    return pl.pallas_call(
        body, name=name,
        out_shape=[jax.ShapeDtypeStruct((N_DEV,) + a.shape, a.dtype) for a in arrs],
        in_specs=[any_spec] * n, out_specs=[any_spec] * n,
        scratch_shapes=[pltpu.SemaphoreType.DMA((n, 7)), pltpu.SemaphoreType.DMA((n, 7)),
                        pltpu.SemaphoreType.DMA((n,))],
    )(*arrs)


GATHER, SCATTER, CHIP_SCATTER = "gather", "scatter", "chip_scatter"


def _direct_copies(mode, ins, outs, send_sems, recv_sems, local_sems):
    x, y, c = _my_place()
    by_chip = mode == CHIP_SCATTER
    place = (lambda px, py, pc: 2 * px + py) if by_chip else (lambda px, py, pc: 4 * px + 2 * py + pc)
    me_idx = place(x, y, c)
    n = len(ins)
    local = [pltpu.make_async_copy(ins[a] if mode == GATHER else ins[a].at[me_idx], outs[a].at[me_idx],
                                   local_sems.at[a]) for a in range(n)]
    remote = []
    for rel in range(1, N_DEV):
        if by_chip and rel & 1:
            continue
        px = 1 - x if rel & 4 else x
        py = 1 - y if rel & 2 else y
        pc = 1 - c if rel & 1 else c
        for a in range(n):
            remote.append(pltpu.make_async_remote_copy(
                src_ref=ins[a] if mode == GATHER else ins[a].at[place(px, py, pc)], dst_ref=outs[a].at[me_idx],
                send_sem=send_sems.at[a, rel - 1], recv_sem=recv_sems.at[a, rel - 1],
                device_id=(px, py, pc), device_id_type=pl.DeviceIdType.MESH))
    return local + remote


def _side_scratch(n):
    return [pltpu.SemaphoreType.DMA((n, 7)), pltpu.SemaphoreType.DMA((n, 7)), pltpu.SemaphoreType.DMA((n,))]


def _side_out_shapes(mode, arrs):
    return [jax.ShapeDtypeStruct((N_DEV,) + a.shape if mode == GATHER else a.shape, a.dtype) for a in arrs]


def _pallas_with_side(body, side, first, last, n_in, n_out, *, in_specs, out_specs, out_shape, scratch_shapes=(),
                      args, **kwargs):
    side_arrs = [] if side is None else list(side[1])
    ns = len(side_arrs)

    def wrapped(*refs):
        own_in, side_in = refs[:n_in], refs[n_in:n_in + ns]
        o = n_in + ns
        own_out, side_out = refs[o:o + n_out], refs[o + n_out:o + n_out + ns]
        scratch = refs[o + n_out + ns:]
        own_scratch, sems = (scratch[:-3], scratch[-3:]) if ns else (scratch, ())
        if ns:
            @pl.when(first())
            def _():
                for cp in _direct_copies(side[0], side_in, side_out, *sems):
                    cp.start()

        body(*own_in, *own_out, *own_scratch)
        if ns:
            @pl.when(last())
            def _():
                for cp in _direct_copies(side[0], side_in, side_out, *sems):
                    cp.wait()

    any_spec = pl.BlockSpec(memory_space=pl.ANY)
    res = pl.pallas_call(
        wrapped,
        in_specs=list(in_specs) + [any_spec] * ns,
        out_specs=list(out_specs) + [any_spec] * ns,
        out_shape=list(out_shape) + (_side_out_shapes(side[0], side_arrs) if ns else []),
        scratch_shapes=list(scratch_shapes) + (_side_scratch(ns) if ns else []),
        **kwargs,
    )(*args, *side_arrs)
    return res[:n_out], res[n_out:]


def _exchange_small(everyone, per_device, name):
    n_all, n_own = everyone.shape[1], per_device.shape[2]

    def body(all_ref, own_ref, all_sum, own_sum, all_slots, own_slots, *sems):
        copies = _direct_copies(GATHER, [all_ref], [all_slots], *sems[:3])
        copies += _direct_copies(SCATTER, [own_ref], [own_slots], *sems[3:])
        for cp in copies:
            cp.start()
        for cp in copies:
            cp.wait()
        for slots, out in ((all_slots, all_sum), (own_slots, own_sum)):
            total = slots[0]
            for i in range(1, N_DEV):
                total = total + slots[i]
            out[...] = total

    vmem_spec = pl.BlockSpec(memory_space=pltpu.VMEM)
    return pl.pallas_call(
        body, name=name,
        out_shape=[jax.ShapeDtypeStruct((1, n_all), F32), jax.ShapeDtypeStruct((1, n_own), F32)],
        in_specs=[vmem_spec, vmem_spec], out_specs=[vmem_spec, vmem_spec],
        scratch_shapes=[pltpu.VMEM((N_DEV, 1, n_all), F32), pltpu.VMEM((N_DEV, 1, n_own), F32)]
                       + _side_scratch(1) + _side_scratch(1),
    )(everyone, per_device)


def _full(shape):
    return pl.BlockSpec(shape, lambda *_: (0,) * len(shape))


def _inproj_fwd(x, nw, w, name, side=None):
    t = x.shape[0]
    tm = 256

    def body(x_ref, nw_ref, w_ref, proj_ref, h_ref):
        h = _rmsnorm(x_ref[...], nw_ref[...]).astype(BF16)
        h_ref[...] = h
        for j in range(N_COL_TILES):
            sl = slice(j * COL_TILE, (j + 1) * COL_TILE)
            proj_ref[:, sl] = jnp.dot(h, w_ref[:, sl], preferred_element_type=F32)

    grid = (t // tm,)
    return _pallas_with_side(
        body, side, *_grid_ends(grid), 3, 2, name=name, grid=grid,
        out_shape=[jax.ShapeDtypeStruct((t, PROJ_W), F32), jax.ShapeDtypeStruct((t, D_MODEL), BF16)],
        in_specs=[pl.BlockSpec((tm, D_MODEL), lambda i: (i, 0)), _full((1, D_MODEL)), _full((D_MODEL, PROJ_W))],
        out_specs=[pl.BlockSpec((tm, PROJ_W), lambda i: (i, 0)), pl.BlockSpec((tm, D_MODEL), lambda i: (i, 0))],
        compiler_params=pltpu.CompilerParams(dimension_semantics=("arbitrary",), vmem_limit_bytes=VMEM_LIMIT),
        args=(x, nw, w))


def _param_specs():
    return [_full((4, 1536)), _full((1, 1536)), _full((1, 128)), _full((1, 128)), _full((1, 128)),
            _full((1, 1024)), _full((1, 128)), _full((CONF_KERNEL, 512)), _full((1, 512)), _full((1, 512)),
            _full((1, 512))]


def _halo_specs(nc, chunk_of):
    def prev_chunk(b, j):
        return jnp.maximum(b * nc + chunk_of(j) - 1, 0)

    per_xbc = CHUNK // XBC_HALO
    per_cc = CHUNK // CONF_HALO
    return [
        pl.BlockSpec((XBC_HALO, 1536), lambda b, j: (prev_chunk(b, j) * per_xbc + per_xbc - 1, C_XBC // 1536)),
        pl.BlockSpec((CHUNK, 256), lambda b, j: (prev_chunk(b, j), C_K // 256)),
        pl.BlockSpec((CONF_HALO, 1024), lambda b, j: (prev_chunk(b, j) * per_cc + per_cc - 1, C_CA // 1024)),
    ]


def _grid_ends(grid):
    first = lambda: functools.reduce(lambda p, q: p & q, [pl.program_id(i) == 0 for i in range(len(grid))])
    last = lambda: functools.reduce(lambda p, q: p & q, [pl.program_id(i) == n - 1 for i, n in enumerate(grid)])
    return first, last


def _mixer_fwd(x, proj, w_out, params, nb, name, side=None, head=None):
    t = x.shape[0]
    nc = t // nb // CHUNK
    n_head = 0 if head is None else 2

    def body(x_ref, cur_ref, pxbc_ref, pkv_ref, pcc_ref, wo_ref, *rest):
        prm = [r[...] for r in rest[:11]]
        head_in = rest[11:11 + n_head]
        xn_ref, sall_ref, conv_ref = rest[11 + n_head:14 + n_head]
        head_out = rest[14 + n_head:14 + 2 * n_head]
        s_scr, bias_scr = rest[14 + 2 * n_head:16 + 2 * n_head]
        c = pl.program_id(1)
        not_first = c > 0
        nf = not_first.astype(F32)
        kv_bias = _kv_bias(bias_scr, (pl.program_id(0) == 0) & (c == 0), not_first)

        @pl.when(c == 0)
        def _():
            s_scr[...] = jnp.zeros_like(s_scr)

        cw, cb, dtb, alog, dsk, nw, snk, dww, dwb, lnw, lnb = prm
        s_in = s_scr[...]
        sall_ref[0] = s_in
        ssd_conv = _dwconv(jnp.concatenate([pxbc_ref[...] * nf, cur_ref[:, C_XBC:C_XBC + 1536]], axis=0), cw, cb,
                           XBC_HALO)
        y_ssd, s_out = _ssd_part(cur_ref[:, 0:1024], ssd_conv, cur_ref[:, C_DT:C_DT + 128], s_in, dtb, alog, dsk, nw)
        s_scr[...] = s_out
        y_attn = _attn_part(cur_ref[:, 1024:1536], cur_ref[:, C_Q:C_Q + 512], cur_ref[:, C_K:C_K + 256],
                            pkv_ref[...] * nf, snk, kv_bias, ATTN_STACK_FWD)
        conv_out = _dwconv(_conf_glu(cur_ref[:, C_CA:C_CA + 1024], pcc_ref[...] * nf), dww, dwb, CONF_HALO)
        conv_ref[...] = conv_out
        y_conf = _conf_tail(conv_out, cur_ref[:, 1536:2048], lnw, lnb)
        x_next = (x_ref[...] + _dot(y_ssd, wo_ref[0:1024, :]) + _dot(y_attn, wo_ref[1024:1536, :])
                  + _dot(y_conf, wo_ref[1536:2048, :]))
        if head is None:
            xn_ref[...] = x_next
        else:
            fnw_ref, tgt_ref = head_in
            loss_ref, gfnw_ref = head_out

            @pl.when((pl.program_id(0) == 0) & (c == 0))
            def _():
                loss_ref[...] = jnp.zeros_like(loss_ref)
                gfnw_ref[...] = jnp.zeros_like(gfnw_ref)

            y, vjp = jax.vjp(_rmsnorm, x_next, fnw_ref[...])
            err = y - tgt_ref[...]
            loss_ref[...] += 0.5 * jnp.sum(jnp.mean(err * err, axis=-1, keepdims=True), axis=0, keepdims=True)
            xn_ref[...], d_fnw = vjp(err * (1.0 / D_MODEL))
            gfnw_ref[...] += d_fnw

    row = lambda b, j: (b * nc + j, 0)
    grid = (nb, nc)
    return _pallas_with_side(
        body, side, *_grid_ends(grid), 17 + n_head, 3 + n_head, name=name, grid=grid,
        out_shape=[jax.ShapeDtypeStruct((t, D_MODEL), F32),
                   jax.ShapeDtypeStruct((nb * nc, SSD_HEADS * SSD_HEAD_DIM, SSD_STATE), F32),
                   jax.ShapeDtypeStruct((t, 512), F32)]
                  + [jax.ShapeDtypeStruct(s, F32) for s in ((1, 1), (1, D_MODEL))[:n_head]],
        in_specs=[pl.BlockSpec((CHUNK, D_MODEL), row), pl.BlockSpec((CHUNK, PROJ_W), row)]
                 + _halo_specs(nc, lambda j: j) + [_full((MIX_WIDTH, D_MODEL))] + _param_specs()
                 + [_full((1, D_MODEL)), pl.BlockSpec((CHUNK, D_MODEL), row)][:n_head],
        out_specs=[pl.BlockSpec((CHUNK, D_MODEL), row),
                   pl.BlockSpec((1, SSD_HEADS * SSD_HEAD_DIM, SSD_STATE), lambda b, j: (b * nc + j, 0, 0)),
                   pl.BlockSpec((CHUNK, 512), row)] + [_full((1, 1)), _full((1, D_MODEL))][:n_head],
        scratch_shapes=[pltpu.VMEM((SSD_HEADS * SSD_HEAD_DIM, SSD_STATE), F32),
                        pltpu.VMEM((2, ATTN_STACK_FWD * CHUNK, 2 * CHUNK), F32)],
        compiler_params=pltpu.CompilerParams(dimension_semantics=("arbitrary", "arbitrary"),
                                             vmem_limit_bytes=VMEM_LIMIT),
        args=(x, proj, proj, proj, proj, w_out, *params, *(head or ())))


def _mixer_bwd(dxn, proj, s_all, conv_all, w_out, params, nb, name):
    t = dxn.shape[0]
    nc = t // nb // CHUNK
    n_prm = 11

    def body(dxn_ref, cur_ref, pxbc_ref, pkv_ref, pcc_ref, s_ref, conv_ref, wo_ref, *rest):
        prm = [r[...] for r in rest[:n_prm]]
        dproj_ref, ycat_ref = rest[n_prm:n_prm + 2]
        gprm = rest[n_prm + 2:2 * n_prm + 2]
        ds_scr, pend_xbc, pend_kv, pend_cc, bias_scr = rest[2 * n_prm + 2:]
        b, j = pl.program_id(0), pl.program_id(1)
        c = nc - 1 - j
        not_first = c > 0
        nf = not_first.astype(F32)
        kv_bias = _kv_bias(bias_scr, (b == 0) & (j == 0), not_first)

        @pl.when((b == 0) & (j == 0))
        def _():
            for r in gprm:
                r[...] = jnp.zeros_like(r)

        @pl.when(j == 0)
        def _():
            ds_scr[...] = jnp.zeros_like(ds_scr)
            pend_xbc[...] = jnp.zeros_like(pend_xbc)
            pend_kv[...] = jnp.zeros_like(pend_kv)
            pend_cc[...] = jnp.zeros_like(pend_cc)

        cw, cb, dtb, alog, dsk, nw, snk, dww, dwb, lnw, lnb = prm
        g_cw, g_cb, g_dtb, g_alog, g_dsk, g_nw, g_snk, g_dww, g_dwb, g_lnw, g_lnb = gprm
        dxn_v = dxn_ref[...]

        def add_tail(d_cur, pending):
            lead = jnp.zeros((CHUNK - pending.shape[0], pending.shape[1]), F32)
            return d_cur + jnp.concatenate([lead, pending], axis=0)

        y, vjp = jax.vjp(_conf_tail, conv_ref[...], cur_ref[:, 1536:2048], lnw, lnb)
        ycat_ref[:, 1536:2048] = y.astype(BF16)
        d_conv, dz, d_lnw, d_lnb = vjp(_dot_nt(dxn_v, wo_ref[1536:2048, :]))
        ext, vjp = jax.vjp(_conf_glu, cur_ref[:, C_CA:C_CA + 1024], pcc_ref[...] * nf)
        d_ext, d_dww, d_dwb = _dwconv_bwd(CONF_HALO, (ext, dww), d_conv)
        dcacg, dpcc = vjp(d_ext)
        dproj_ref[:, 1536:2048] = dz.astype(BF16)
        dproj_ref[:, C_CA:C_CA + 1024] = add_tail(dcacg, pend_cc[...]).astype(BF16)
        pend_cc[...] = dpcc
        for r, g in ((g_dww, d_dww), (g_dwb, d_dwb), (g_lnw, d_lnw), (g_lnb, d_lnb)):
            r[...] += g

        attn = functools.partial(_attn_part, kv_bias=kv_bias, stack=ATTN_STACK_BWD)
        y, vjp = jax.vjp(attn, cur_ref[:, 1024:1536], cur_ref[:, C_Q:C_Q + 512], cur_ref[:, C_K:C_K + 256],
                         pkv_ref[...] * nf, snk)
        ycat_ref[:, 1024:1536] = y.astype(BF16)
        dz, dq, dkv, dpkv, d_snk = vjp(_dot_nt(dxn_v, wo_ref[1024:1536, :]))
        dproj_ref[:, 1024:1536] = dz.astype(BF16)
        dproj_ref[:, C_Q:C_Q + 512] = dq.astype(BF16)
        dproj_ref[:, C_K:C_K + 256] = (dkv + pend_kv[...]).astype(BF16)
        pend_kv[...] = dpkv
        g_snk[...] += d_snk

        ext = jnp.concatenate([pxbc_ref[...] * nf, cur_ref[:, C_XBC:C_XBC + 1536]], axis=0)
        (y, _), vjp = jax.vjp(_ssd_part, cur_ref[:, 0:1024], _dwconv(ext, cw, cb, XBC_HALO),
                              cur_ref[:, C_DT:C_DT + 128], s_ref[0], dtb, alog, dsk, nw)
        ycat_ref[:, 0:1024] = y.astype(BF16)
        dz, d_conv, ddtr, ds_in, d_dtb, d_alog, d_dsk, d_nw = vjp((_dot_nt(dxn_v, wo_ref[0:1024, :]), ds_scr[...]))
        d_ext, d_cw, d_cb = _dwconv_bwd(XBC_HALO, (ext, cw), d_conv)
        dpxbc, dxbc = d_ext[0:XBC_HALO, :], d_ext[XBC_HALO:, :]
        dproj_ref[:, 0:1024] = dz.astype(BF16)
        dproj_ref[:, C_XBC:C_XBC + 1536] = add_tail(dxbc, pend_xbc[...]).astype(BF16)
        dproj_ref[:, C_DT:C_DT + 128] = ddtr.astype(BF16)
        dproj_ref[:, C_DT + 128:PROJ_W] = jnp.zeros((CHUNK, PROJ_W - C_DT - 128), BF16)
        pend_xbc[...] = dpxbc
        ds_scr[...] = ds_in
        for r, g in ((g_cw, d_cw), (g_cb, d_cb), (g_dtb, d_dtb), (g_alog, d_alog), (g_dsk, d_dsk), (g_nw, d_nw)):
            r[...] += g

    row = lambda b, j: (b * nc + nc - 1 - j, 0)
    prm_shapes = [(4, 1536), (1, 1536), (1, 128), (1, 128), (1, 128), (1, 1024), (1, 128), (CONF_KERNEL, 512),
                  (1, 512), (1, 512), (1, 512)]
    grid = (nb, nc)
    return _pallas_with_side(
        body, None, *_grid_ends(grid), 8 + n_prm, 2 + n_prm, name=name, grid=grid,
        out_shape=[jax.ShapeDtypeStruct((t, PROJ_W), BF16), jax.ShapeDtypeStruct((t, MIX_WIDTH), BF16)]
                  + [jax.ShapeDtypeStruct(s, F32) for s in prm_shapes],
        in_specs=[pl.BlockSpec((CHUNK, D_MODEL), row), pl.BlockSpec((CHUNK, PROJ_W), row)]
                 + _halo_specs(nc, lambda j: nc - 1 - j)
                 + [pl.BlockSpec((1, SSD_HEADS * SSD_HEAD_DIM, SSD_STATE), lambda b, j: (b * nc + nc - 1 - j, 0, 0)),
                    pl.BlockSpec((CHUNK, 512), row), _full((MIX_WIDTH, D_MODEL))] + _param_specs(),
        out_specs=[pl.BlockSpec((CHUNK, PROJ_W), row), pl.BlockSpec((CHUNK, MIX_WIDTH), row)]
                  + [_full(s) for s in prm_shapes],
        scratch_shapes=[pltpu.VMEM((SSD_HEADS * SSD_HEAD_DIM, SSD_STATE), F32), pltpu.VMEM((XBC_HALO, 1536), F32),
                        pltpu.VMEM((CHUNK, 256), F32), pltpu.VMEM((CONF_HALO, 1024), F32),
                        pltpu.VMEM((2, ATTN_STACK_BWD * CHUNK, 2 * CHUNK), F32)],
        compiler_params=pltpu.CompilerParams(dimension_semantics=("arbitrary", "arbitrary"),
                                             vmem_limit_bytes=VMEM_LIMIT),
        args=(dxn, proj, proj, proj, proj, s_all, conv_all, w_out, *params))


def _gw_out(y_cat, dxn, name):
    t = y_cat.shape[0]
    tk = 512

    def body(y_ref, dxn_ref, out_ref, acc):
        k = pl.program_id(0)

        @pl.when(k == 0)
        def _():
            acc[...] = jnp.zeros_like(acc)

        acc[...] += _dot_tn(y_ref[...], dxn_ref[...])

        @pl.when(k == t // tk - 1)
        def _():
            out_ref[...] = acc[...].astype(BF16)

    out = pl.pallas_call(
        body, name=name, grid=(t // tk,),
        out_shape=jax.ShapeDtypeStruct((MIX_WIDTH, D_MODEL), BF16),
        in_specs=[pl.BlockSpec((tk, MIX_WIDTH), lambda k: (k, 0)), pl.BlockSpec((tk, D_MODEL), lambda k: (k, 0))],
        out_specs=_full((MIX_WIDTH, D_MODEL)),
        scratch_shapes=[pltpu.VMEM((MIX_WIDTH, D_MODEL), F32)],
        compiler_params=pltpu.CompilerParams(dimension_semantics=("arbitrary",), vmem_limit_bytes=VMEM_LIMIT),
    )(y_cat, dxn)
    return out.reshape(N_DEV, MIX_WIDTH // N_DEV, D_MODEL)


def _inproj_bwd_x(dproj, w, x, nw, dxn, name, side=None):
    t = x.shape[0]
    tm = 256

    def body(dp_ref, w_ref, x_ref, nw_ref, dxn_ref, dx_ref, gnw_ref):
        @pl.when(pl.program_id(0) == 0)
        def _():
            gnw_ref[...] = jnp.zeros_like(gnw_ref)

        dh = jnp.zeros((tm, D_MODEL), F32)
        for j in range(N_COL_TILES):
            sl = slice(j * COL_TILE, (j + 1) * COL_TILE)
            dh = dh + _dot_nt(dp_ref[:, sl], w_ref[:, sl])
        _, vjp = jax.vjp(_rmsnorm, x_ref[...], nw_ref[...])
        dx, dnw = vjp(dh)
        dx_ref[...] = dxn_ref[...] + dx
        gnw_ref[...] += dnw

    tok = lambda i: (i, 0)
    grid = (t // tm,)
    return _pallas_with_side(
        body, side, *_grid_ends(grid), 5, 2, name=name, grid=grid,
        out_shape=[jax.ShapeDtypeStruct((t, D_MODEL), F32), jax.ShapeDtypeStruct((1, D_MODEL), F32)],
        in_specs=[pl.BlockSpec((tm, PROJ_W), tok), _full((D_MODEL, PROJ_W)), pl.BlockSpec((tm, D_MODEL), tok),
                  _full((1, D_MODEL)), pl.BlockSpec((tm, D_MODEL), tok)],
        out_specs=[pl.BlockSpec((tm, D_MODEL), tok), _full((1, D_MODEL))],
        compiler_params=pltpu.CompilerParams(dimension_semantics=("arbitrary",), vmem_limit_bytes=VMEM_LIMIT),
        args=(dproj, w, x, nw, dxn))


def _inproj_bwd_w(h, dproj, name, side=None):
    t = h.shape[0]
    tk = 512

    def body(h_ref, dp_ref, gw_ref):
        @pl.when(pl.program_id(1) == 0)
        def _():
            gw_ref[...] = jnp.zeros_like(gw_ref)

        gw_ref[...] += _dot_tn(h_ref[...], dp_ref[...])

    grid = (N_COL_TILES, t // tk)
    return _pallas_with_side(
        body, side, *_grid_ends(grid), 2, 1, name=name, grid=grid,
        out_shape=[jax.ShapeDtypeStruct((D_MODEL, PROJ_W), F32)],
        in_specs=[pl.BlockSpec((tk, D_MODEL), lambda n, k: (k, 0)), pl.BlockSpec((tk, COL_TILE), lambda n, k: (k, n))],
        out_specs=[pl.BlockSpec((D_MODEL, COL_TILE), lambda n, k: (0, n))],
        compiler_params=pltpu.CompilerParams(dimension_semantics=("arbitrary", "arbitrary"),
                                             vmem_limit_bytes=VMEM_LIMIT),
        args=(h, dproj))


def _repack_runs():
    pieces = ((0, 2048, C_Z), (2048, 3584, C_XBC), (3584, 3600, C_DT), (3600, 4368, C_Q), (4368, D_IN_PROJ, C_CA))
    per = D_IN_PROJ // N_DEV
    runs = []
    for j in range(N_DEV):
        lo, hi = per * j, per * (j + 1)
        for a, b, dst in pieces:
            s, e = max(lo, a), min(hi, b)
            if s < e:
                runs.append((j, s - lo, e - lo, dst + s - a))
    return runs


def _repack_w_in(g, name):
    tr = 256

    def body(g_ref, o_ref):
        for j, a, b, dst in _repack_runs():
            o_ref[:, dst:dst + b - a] = g_ref[j, :, a:b]
        o_ref[:, C_DT + 16:PROJ_W] = jnp.zeros((tr, PROJ_W - C_DT - 16), g.dtype)

    return pl.pallas_call(
        body, name=name, grid=(D_MODEL // tr,),
        out_shape=jax.ShapeDtypeStruct((D_MODEL, PROJ_W), g.dtype),
        in_specs=[pl.BlockSpec((N_DEV, tr, D_IN_PROJ // N_DEV), lambda i: (0, i, 0))],
        out_specs=pl.BlockSpec((tr, PROJ_W), lambda i: (i, 0)),
        compiler_params=pltpu.CompilerParams(dimension_semantics=("arbitrary",)),
    )(g)


def _unpack_gw_in(g, name):
    tr = 256

    def body(g_ref, o_ref):
        for j, a, b, dst in _repack_runs():
            o_ref[j, :, a:b] = g_ref[:, dst:dst + b - a].astype(BF16)

    return pl.pallas_call(
        body, name=name, grid=(D_MODEL // tr,),
        out_shape=jax.ShapeDtypeStruct((N_DEV, D_MODEL, D_IN_PROJ // N_DEV), BF16),
        in_specs=[pl.BlockSpec((tr, PROJ_W), lambda i: (i, 0))],
        out_specs=pl.BlockSpec((N_DEV, tr, D_IN_PROJ // N_DEV), lambda i: (0, i, 0)),
        compiler_params=pltpu.CompilerParams(dimension_semantics=("arbitrary",)),
    )(g)


def _pair_sum(parts, name):
    n_dev, r, cdim = parts.shape
    n_chip = n_dev // 2
    by_chip = parts.reshape(n_chip, 2, r, cdim)

    def swap_body(p_ref, got_ref, send_sem, recv_sem):
        x, y, c = _my_place()
        cp = pltpu.make_async_remote_copy(
            src_ref=p_ref.at[:, pl.ds(1 - c, 1)], dst_ref=got_ref, send_sem=send_sem, recv_sem=recv_sem,
            device_id=(x, y, 1 - c), device_id_type=pl.DeviceIdType.MESH)
        cp.start()
        cp.wait()

    any_spec = pl.BlockSpec(memory_space=pl.ANY)
    got = pl.pallas_call(
        swap_body, name=name + "_swap",
        out_shape=jax.ShapeDtypeStruct((n_chip, 1, r, cdim), parts.dtype),
        in_specs=[any_spec], out_specs=any_spec,
        scratch_shapes=[pltpu.SemaphoreType.DMA, pltpu.SemaphoreType.DMA],
    )(by_chip)

    tr = 256

    def add_body(core_ref, p_ref, got_ref, o_ref):
        o_ref[0] = (p_ref[0, 0].astype(F32) + got_ref[0, 0].astype(F32)).astype(o_ref.dtype)

    return pl.pallas_call(
        add_body, name=name + "_add",
        grid_spec=pltpu.PrefetchScalarGridSpec(
            num_scalar_prefetch=1, grid=(n_chip, r // tr),
            in_specs=[pl.BlockSpec((1, 1, tr, cdim), lambda k, i, core: (k, core[0], i, 0)),
                      pl.BlockSpec((1, 1, tr, cdim), lambda k, i, core: (k, 0, i, 0))],
            out_specs=pl.BlockSpec((1, tr, cdim), lambda k, i, core: (k, i, 0))),
        out_shape=jax.ShapeDtypeStruct((n_chip, r, cdim), parts.dtype),
        compiler_params=pltpu.CompilerParams(dimension_semantics=("arbitrary", "arbitrary")),
    )(lax.axis_index("c").astype(jnp.int32).reshape(1), by_chip, got)


def _adamw(w, g, m, v):
    m = ADAM_B1 * m + (1.0 - ADAM_B1) * g
    v = ADAM_B2 * v + (1.0 - ADAM_B2) * jnp.square(g)
    m_hat = m / (1.0 - ADAM_B1 ** ADAM_STEP)
    v_hat = v / (1.0 - ADAM_B2 ** ADAM_STEP)
    delta = -ADAM_LR * (m_hat / (jnp.sqrt(v_hat) + ADAM_EPS) + ADAM_WD * w)
    return delta, m, v


def _reduce_adamw(parts, w, m, v, tr, name):
    depth = len(parts)
    p, r, cdim = parts[0].shape
    n_blk = r // tr

    def body(*refs):
        p_refs = refs[:depth]
        w_ref, m_ref, v_ref, g_ref, d_ref, nm_ref, nv_ref = refs[depth:]
        for layer in range(depth):
            @pl.when(pl.program_id(0) == layer)
            def _(p_ref=p_refs[layer]):
                g = p_ref[0].astype(F32)
                for i in range(1, p):
                    g = g + p_ref[i].astype(F32)
                g_ref[0] = g
                d_ref[0], nm_ref[0], nv_ref[0] = _adamw(w_ref[0], g, m_ref[0], v_ref[0])

    def parts_spec(layer):
        return pl.BlockSpec((p, tr, cdim), lambda d, i: (0, jnp.clip(i + (d - layer) * n_blk, 0, n_blk - 1), 0))

    blk = pl.BlockSpec((1, tr, cdim), lambda d, i: (d, i, 0))
    return pl.pallas_call(
        body, name=name, grid=(depth, n_blk),
        out_shape=[jax.ShapeDtypeStruct(w.shape, F32)] * 4,
        in_specs=[parts_spec(layer) for layer in range(depth)] + [blk, blk, blk],
        out_specs=[blk] * 4,
        compiler_params=pltpu.CompilerParams(dimension_semantics=("arbitrary", "arbitrary"),
                                             vmem_limit_bytes=VMEM_LIMIT),
    )(*parts, w, m, v)


def _adamw_small(ssum, entries, name):
    direct = [e[3] for e in entries if not isinstance(e[3], list)]
    n_direct = len(direct)

    def body(*refs):
        ssum_ref, direct_refs = refs[0], list(refs[1:1 + n_direct])
        ins = refs[1 + n_direct:1 + n_direct + 3 * len(entries)]
        outs = refs[1 + n_direct + 3 * len(entries):]
        for k, (w, _, _, grad) in enumerate(entries):
            w_ref, m_ref, v_ref = ins[3 * k:3 * k + 3]
            g_ref, d_ref, nm_ref, nv_ref = outs[4 * k:4 * k + 4]
            if isinstance(grad, list):
                for row, off in enumerate(grad):
                    rows = slice(row, row + 1)
                    g = ssum_ref[:, off:off + w.shape[1]]
                    g_ref[rows, :] = g
                    d_ref[rows, :], nm_ref[rows, :], nv_ref[rows, :] = _adamw(w_ref[rows, :], g, m_ref[rows, :],
                                                                              v_ref[rows, :])
            else:
                g = direct_refs.pop(0)[...]
                g_ref[...] = g
                d_ref[...], nm_ref[...], nv_ref[...] = _adamw(w_ref[...], g, m_ref[...], v_ref[...])

    vmem = pl.BlockSpec(memory_space=pltpu.VMEM)
    args = [ssum] + direct + [a for e in entries for a in e[:3]]
    res = pl.pallas_call(
        body, name=name,
        out_shape=[jax.ShapeDtypeStruct(e[0].shape, F32) for e in entries for _ in range(4)],
        in_specs=[vmem] * len(args), out_specs=[vmem] * (4 * len(entries)),
    )(*args)
    return [res[4 * k:4 * k + 4] for k in range(len(entries))]


def _reduce_adamw_cols(parts, w, m, v, name):
    depth = len(parts)
    _, r, cdim = parts[0].shape
    tc = 512

    def body(*refs):
        p_refs = refs[:depth]
        w_ref, m_ref, v_ref, g_ref, d_ref, nm_ref, nv_ref = refs[depth:]
        for layer in range(depth):
            g = p_refs[layer][0].astype(F32)
            for i in range(1, parts[layer].shape[0]):
                g = g + p_refs[layer][i].astype(F32)
            g = g.T
            g_ref[:, layer, :] = g
            d_ref[:, layer, :], nm_ref[:, layer, :], nv_ref[:, layer, :] = _adamw(
                w_ref[:, layer, :], g, m_ref[:, layer, :], v_ref[:, layer, :])

    view = lambda a: jnp.transpose(a, (2, 0, 1))
    blk = pl.BlockSpec((cdim, depth, tc), lambda i: (0, 0, i))
    outs = pl.pallas_call(
        body, name=name, grid=(r // tc,),
        out_shape=[jax.ShapeDtypeStruct((cdim, depth, r), F32)] * 4,
        in_specs=[pl.BlockSpec((a.shape[0], tc, cdim), lambda i: (0, i, 0)) for a in parts] + [blk, blk, blk],
        out_specs=[blk] * 4,
        compiler_params=pltpu.CompilerParams(dimension_semantics=("arbitrary",), vmem_limit_bytes=VMEM_LIMIT),
    )(*parts, view(w), view(m), view(v))
    return [jnp.transpose(o, (1, 2, 0)) for o in outs]


def _pad_lanes(v, width=128):
    return jnp.pad(v.reshape(1, -1), ((0, 0), (0, width - v.shape[-1])))


SMALL_FIELDS = (("norm_w", 1024), ("conv_b", 1536), ("dt_bias", 128), ("a_log", 128), ("d_skip", 128),
                ("ssd_norm_w", 1024), ("sinks", 128), ("dw_b", 512), ("ln_w", 512), ("ln_b", 512))


def kernel(x, norm_w, w_in, ssd_conv_w, ssd_conv_b, ssd_dt_bias, ssd_a_log, ssd_d, ssd_norm_w, attn_sinks, conf_dw_w, conf_dw_b, conf_ln_w, conf_ln_b, w_out, final_norm_w, loss_target, m_norm_w, m_w_in, m_ssd_conv_w, m_ssd_conv_b, m_ssd_dt_bias, m_ssd_a_log, m_ssd_d, m_ssd_norm_w, m_attn_sinks, m_conf_dw_w, m_conf_dw_b, m_conf_ln_w, m_conf_ln_b, m_w_out, m_final_norm_w, v_norm_w, v_w_in, v_ssd_conv_w, v_ssd_conv_b, v_ssd_dt_bias, v_ssd_a_log, v_ssd_d, v_ssd_norm_w, v_attn_sinks, v_conf_dw_w, v_conf_dw_b, v_conf_ln_w, v_conf_ln_b, v_w_out, v_final_norm_w):
    nb, seq, _ = x.shape
    depth = norm_w.shape[0]
    t = nb * seq

    w_in_bf, w_out_bf = w_in.astype(BF16), w_out.astype(BF16)
    g_win0, g_cw, g_dw = _all_gather([w_in_bf[0], ssd_conv_w, conf_dw_w], "gather_weights")
    w_in_full = [_repack_w_in(g_win0, "repack_w_in_0")]
    w_out_full = []
    conv_w_full = [jnp.transpose(g_cw[:, l], (1, 0, 2)).reshape(4, 1536) for l in range(depth)]
    dw_w_full = [jnp.transpose(g_dw[:, l], (1, 0, 2)).reshape(CONF_KERNEL, 512) for l in range(depth)]

    def layer_params(l):
        return [conv_w_full[l], ssd_conv_b[l].reshape(1, -1), _pad_lanes(ssd_dt_bias[l]), _pad_lanes(ssd_a_log[l]),
                _pad_lanes(ssd_d[l]), ssd_norm_w[l].reshape(1, -1), _pad_lanes(attn_sinks[l]), dw_w_full[l],
                conf_dw_b[l].reshape(1, -1), conf_ln_w[l].reshape(1, -1), conf_ln_b[l].reshape(1, -1)]

    xs = [x.reshape(t, D_MODEL)]
    saved = []
    for l in range(depth):
        (proj, h), gathered = _inproj_fwd(xs[l], norm_w[l].reshape(1, -1), w_in_full[l], f"inproj_fwd_{l}",
                                          (GATHER, [w_out_bf[l]]))
        w_out_full.append(gathered[0].reshape(MIX_WIDTH, D_MODEL))
        last = l + 1 == depth
        side = None if last else (GATHER, [w_in_bf[l + 1]])
        head = (final_norm_w.reshape(1, -1), loss_target.reshape(t, D_MODEL)) if last else None
        res, gathered = _mixer_fwd(xs[l], proj, w_out_full[l], layer_params(l), nb, f"mixer_fwd_{l}", side, head)
        if side:
            w_in_full.append(_repack_w_in(gathered[0], f"repack_w_in_{l + 1}"))
        saved.append((proj, h, res[1:3]))
        xs.append(res[0])
    dx, loss_part, g_fnw = res[0], res[3], res[4]

    small_rows, shard_rows = [None] * depth, [None] * depth
    received = [None] * depth
    for l in reversed(range(depth)):
        proj, h, kept = saved[l]
        res, _ = _mixer_bwd(dx, proj, *kept, w_out_full[l], layer_params(l), nb, f"mixer_bwd_{l}")
        dproj, y_cat = res[0], res[1]
        g_cw_l, g_cb, g_dtb, g_alog, g_dsk, g_nw, g_snk, g_dww, g_dwb, g_lnw, g_lnb = res[2:]
        gw_out_parts = _gw_out(y_cat, dx, f"gw_out_{l}")
        (gw_in,), got_out = _inproj_bwd_w(h, dproj, f"inproj_bwd_w_{l}", (SCATTER, [gw_out_parts]))
        chip_parts = _pair_sum(_unpack_gw_in(gw_in, f"unpack_gw_in_{l}"), f"pair_sum_{l}")
        (dx, g_norm), got_in = _inproj_bwd_x(dproj, w_in_full[l], xs[l], norm_w[l].reshape(1, -1), dx,
                                             f"inproj_bwd_x_{l}", (CHIP_SCATTER, [chip_parts]))
        received[l] = [got_in[0], got_out[0]]
        small_rows[l] = [g_norm, g_cb, g_dtb, g_alog, g_dsk, g_nw, g_snk, g_dwb, g_lnw, g_lnb]
        shard_rows[l] = [jnp.transpose(g.reshape(g.shape[0], N_DEV, -1), (1, 0, 2)).reshape(N_DEV, -1)
                         for g in (g_cw_l, g_dww)]
    grad_x = dx.reshape(nb, seq, D_MODEL)

    small = jnp.concatenate([piece for l in range(depth) for piece in small_rows[l]]
                            + [g_fnw, _pad_lanes(loss_part)], axis=1)
    shard_small = jnp.concatenate([piece for l in range(depth) for piece in shard_rows[l]], axis=1)
    ssum, shard_sum = _exchange_small(small, shard_small.reshape(N_DEV, 1, -1), "exchange_small")
    loss = ssum[0, small.shape[1] - 128]

    g_w_in, d_w_in, nm_w_in, nv_w_in = _reduce_adamw_cols([received[l][0] for l in range(depth)], w_in, m_w_in,
                                                          v_w_in, "adamw_w_in")
    g_w_out, d_w_out, nm_w_out, nv_w_out = _reduce_adamw([received[l][1] for l in range(depth)], w_out, m_w_out,
                                                         v_w_out, 256, "adamw_w_out")

    per_layer = sum(n for _, n in SMALL_FIELDS)
    given = {"norm_w": (norm_w, m_norm_w, v_norm_w), "conv_b": (ssd_conv_b, m_ssd_conv_b, v_ssd_conv_b),
             "dt_bias": (ssd_dt_bias, m_ssd_dt_bias, v_ssd_dt_bias), "a_log": (ssd_a_log, m_ssd_a_log, v_ssd_a_log),
             "d_skip": (ssd_d, m_ssd_d, v_ssd_d), "ssd_norm_w": (ssd_norm_w, m_ssd_norm_w, v_ssd_norm_w),
             "sinks": (attn_sinks, m_attn_sinks, v_attn_sinks), "dw_b": (conf_dw_b, m_conf_dw_b, v_conf_dw_b),
             "ln_w": (conf_ln_w, m_conf_ln_w, v_conf_ln_w), "ln_b": (conf_ln_b, m_conf_ln_b, v_conf_ln_b)}
    entries = []
    off = 0
    for fname, n in SMALL_FIELDS:
        entries.append((*given[fname], [l * per_layer + off for l in range(depth)]))
        off += n
    n_cw, n_dw = ssd_conv_w[0].size, conf_dw_w[0].size
    per_layer_shard = n_cw + n_dw
    cw_grad = jnp.stack([shard_sum[0, l * per_layer_shard:l * per_layer_shard + n_cw].reshape(ssd_conv_w.shape[1:])
                         for l in range(depth)], axis=0)
    dw_grad = jnp.stack([shard_sum[0, l * per_layer_shard + n_cw:(l + 1) * per_layer_shard]
                         .reshape(conf_dw_w.shape[1:]) for l in range(depth)], axis=0)
    entries.append((ssd_conv_w, m_ssd_conv_w, v_ssd_conv_w, cw_grad))
    entries.append((conf_dw_w, m_conf_dw_w, v_conf_dw_w, dw_grad))
    entries.append((final_norm_w.reshape(1, -1), m_final_norm_w.reshape(1, -1), v_final_norm_w.reshape(1, -1),
                    [depth * per_layer]))
    sm = _adamw_small(ssum, entries, "adamw_small")
    sm = {k: quad for k, quad in zip([f for f, _ in SMALL_FIELDS] + ["conv_w", "dw_w", "final"], sm)}

    def outputs(i, big_in_i, big_out_i):
        return [sm["norm_w"][i], big_in_i, sm["conv_w"][i], sm["conv_b"][i], sm["dt_bias"][i], sm["a_log"][i],
                sm["d_skip"][i], sm["ssd_norm_w"][i], sm["sinks"][i], sm["dw_w"][i], sm["dw_b"][i], sm["ln_w"][i],
                sm["ln_b"][i], big_out_i, sm["final"][i].reshape(-1)]

    return (loss, grad_x, *outputs(0, g_w_in, g_w_out), *outputs(1, d_w_in, d_w_out),
            *outputs(2, nm_w_in, nm_w_out), *outputs(3, nv_w_in, nv_w_out))
```

```python
import functools

import jax
import jax.numpy as jnp
from jax import lax
from jax.experimental import pallas as pl
from jax.experimental.pallas import tpu as pltpu

F32 = jnp.float32
BF16 = jnp.bfloat16
N_DEV = 8
EPS = 1e-5

D_MODEL = 1024
CHUNK = 128
SSD_HEADS = 16
SSD_HEAD_DIM = 64
SSD_STATE = 128
ATTN_HEADS = 8
ATTN_HEAD_DIM = 64
CONF_KERNEL = 31
MIX_WIDTH = 2048
D_IN_PROJ = 5392
C_Z = 0
C_CA = 2048
C_XBC = 3072
C_Q = 4608
C_K = 5120
C_DT = 5376
PROJ_W = 5632
N_COL_TILES = 4
COL_TILE = PROJ_W // N_COL_TILES
XBC_HALO = 8
CONF_HALO = 32
ATTN_STACK_FWD = 2
ATTN_STACK_BWD = 4
VMEM_LIMIT = 56 * 1024 * 1024

ADAM_LR = 0.001
ADAM_B1 = 0.9
ADAM_B2 = 0.999
ADAM_EPS = 1e-08
ADAM_WD = 0.01
ADAM_STEP = 10


def _silu(v):
    return v * jax.nn.sigmoid(v)


def _softplus(v):
    return jnp.maximum(v, 0.0) + jnp.log1p(jnp.exp(-jnp.abs(v)))


def _rmsnorm(v, w):
    return v * lax.rsqrt(jnp.mean(v * v, axis=-1, keepdims=True) + EPS) * w


def _dot(a, b):
    return jnp.dot(a.astype(BF16), b.astype(BF16), preferred_element_type=F32)


def _dot_nt(a, b):
    return lax.dot_general(a.astype(BF16), b.astype(BF16), (((1,), (1,)), ((), ())), preferred_element_type=F32)


def _dot_tn(a, b):
    return lax.dot_general(a.astype(BF16), b.astype(BF16), (((0,), (0,)), ((), ())), preferred_element_type=F32)


def _taps(ext, offs, out_len, w=None, g=None):
    n_rows, n_cols = ext.shape
    by_shift = {}
    for t, off in enumerate(offs):
        by_shift.setdefault(off % 8, []).append((t, off))
    for r, taps in by_shift.items():
        assert max(off for _, off in taps) - r + out_len <= n_rows - r
    accs = []
    sums = [[None] * (n_cols // 128) for _ in offs]
    for blk in range(n_cols // 128):
        cs = slice(blk * 128, (blk + 1) * 128)
        e = ext[:, cs]
        acc = None
        for r, taps in by_shift.items():
            shifted = e if r == 0 else pltpu.roll(e, n_rows - r, axis=0)
            for t, off in taps:
                window = shifted[off - r:off - r + out_len, :]
                if w is not None:
                    term = w[t:t + 1, cs] * window
                    acc = term if acc is None else acc + term
                if g is not None:
                    sums[t][blk] = jnp.sum(g[:, cs] * window, axis=0, keepdims=True)
        accs.append(acc)
    if w is not None:
        return jnp.concatenate(accs, axis=1)
    return jnp.concatenate([jnp.concatenate(row, axis=1) for row in sums], axis=0)


@functools.partial(jax.custom_vjp, nondiff_argnums=(3,))
def _dwconv(ext, w, b, halo):
    kk = w.shape[0]
    return b + _taps(ext, [halo - (kk - 1) + t for t in range(kk)], ext.shape[0] - halo, w=w)


def _dwconv_fwd(ext, w, b, halo):
    return _dwconv(ext, w, b, halo), (ext, w)


def _dwconv_bwd(halo, res, g):
    ext, w = res
    kk = w.shape[0]
    offs = [halo - (kk - 1) + t for t in range(kk)]
    dw = _taps(ext, offs, ext.shape[0] - halo, g=g)
    zeros = jnp.zeros((halo, g.shape[1]), g.dtype)
    gp = jnp.concatenate([zeros, g, zeros], axis=0)
    dext = _taps(gp, [halo - off for off in offs], ext.shape[0], w=w)
    return dext, dw, jnp.sum(g, axis=0, keepdims=True)


_dwconv.defvjp(_dwconv_fwd, _dwconv_bwd)


def _ssd_part(z_ssd, conv_out, dtr, s_in, dtb, alog, dsk, nw):
    qn = conv_out.shape[0]
    nh = SSD_HEADS
    per_group = nh // 2
    n_pair = nh // 2
    xa = _silu(conv_out)
    xs = xa[:, 0:1024]
    dt = _softplus(dtr + dtb)
    a = dt * (-jnp.exp(alog))
    rows = lax.broadcasted_iota(jnp.int32, (qn, qn), 0)
    cols = lax.broadcasted_iota(jnp.int32, (qn, qn), 1)
    causal = rows >= cols
    low = cols < SSD_HEAD_DIM
    a_cs = jnp.dot(causal.astype(F32), a, precision=lax.Precision.HIGHEST, preferred_element_type=F32)
    a_cs_t = a_cs.T
    bgs = [xa[:, 1024 + g * 128:1024 + (g + 1) * 128] for g in range(2)]
    cgs = [xa[:, 1280 + g * 128:1280 + (g + 1) * 128] for g in range(2)]
    cbms = [_dot_nt(cgs[g], bgs[g]) for g in range(2)]
    colb = [jnp.broadcast_to(a_cs[:, h:h + 1], (qn, qn)) for h in range(nh)]
    dtb_wide = [jnp.broadcast_to(dt[:, h:h + 1], (qn, qn)) for h in range(nh)]
    lmats = [jnp.exp(jnp.where(causal, colb[h] - a_cs_t[h:h + 1, :], -jnp.inf)) for h in range(nh)]
    ms = [cbms[h // per_group] * lmats[h] for h in range(nh)]
    x_pair = [xs[:, p * 128:(p + 1) * 128] for p in range(n_pair)]
    xdt = [x_pair[p] * jnp.where(low, dtb_wide[2 * p], dtb_wide[2 * p + 1]) for p in range(n_pair)]
    x_lo = [jnp.where(low, xdt[p], 0.0) for p in range(n_pair)]
    x_hi = [jnp.where(low, 0.0, xdt[p]) for p in range(n_pair)]
    y_diag = [_dot(ms[2 * p], x_lo[p]) + _dot(ms[2 * p + 1], x_hi[p]) for p in range(n_pair)]
    col_pair = [jnp.where(low, colb[2 * p], colb[2 * p + 1]) for p in range(n_pair)]
    last_pair = [jnp.broadcast_to(col_pair[p][qn - 1:qn, :], (qn, qn)) for p in range(n_pair)]
    ecol = [jnp.exp(col_pair[p]) for p in range(n_pair)]
    xw = [xdt[p] * jnp.exp(last_pair[p] - col_pair[p]) for p in range(n_pair)]
    y_off, st = [], []
    for g in range(2):
        ps = range(g * n_pair // 2, (g + 1) * n_pair // 2)
        y_off.append(_dot_nt(cgs[g], s_in[g * 512:(g + 1) * 512, :]) * jnp.concatenate([ecol[p] for p in ps], axis=1))
        st.append(_dot_tn(jnp.concatenate([xw[p] for p in ps], axis=1), bgs[g]))
    e_last = jnp.exp(jnp.broadcast_to(a_cs_t[:, qn - 1:qn], (qn, SSD_STATE)))
    scale = jnp.concatenate([jnp.broadcast_to(e_last[h:h + 1, :], (64, SSD_STATE)) for h in range(nh)], axis=0)
    s_out = scale * s_in + jnp.concatenate(st, axis=0)
    d_wide = jnp.concatenate([jnp.broadcast_to(dsk[:, h:h + 1], (1, 64)) for h in range(nh)], axis=1)
    y = jnp.concatenate(y_diag, axis=1) + jnp.concatenate(y_off, axis=1) + d_wide * xs
    gated = y * _silu(z_ssd)
    halves = []
    for g in range(2):
        gg = gated[:, g * 512:(g + 1) * 512]
        halves.append(gg * lax.rsqrt(jnp.mean(gg * gg, axis=-1, keepdims=True) + EPS))
    return jnp.concatenate(halves, axis=1) * nw, s_out


def _attn_part(z_attn, q, kv, p_kv, snk, kv_bias, stack):
    qn = q.shape[0]
    kk = jnp.concatenate([p_kv[:, 0:128], kv[:, 0:128]], axis=0)
    vv = jnp.concatenate([p_kv[:, 128:256], kv[:, 128:256]], axis=0)
    units = range(ATTN_HEADS // stack)
    heads = [range(u * stack, (u + 1) * stack) for u in units]
    kv_of = [u * stack // (ATTN_HEADS // 2) for u in units]
    k_of = [kk[:, g * 64:(g + 1) * 64] for g in kv_of]
    v_of = [vv[:, g * 64:(g + 1) * 64] for g in kv_of]
    qs = [jnp.concatenate([q[:, h * 64:(h + 1) * 64] for h in heads[u]], axis=0) for u in units]
    sk = [jnp.concatenate([jnp.broadcast_to(snk[:, h:h + 1], (qn, 1)) for h in heads[u]], axis=0) for u in units]
    s = [_dot_nt(qs[u], k_of[u]) * (ATTN_HEAD_DIM ** -0.5) + kv_bias for u in units]
    m = [lax.stop_gradient(jnp.maximum(jnp.max(s[u], axis=-1, keepdims=True), sk[u])) for u in units]
    e = [jnp.exp(s[u] - m[u]) for u in units]
    r_den = [1.0 / (jnp.sum(e[u], axis=-1, keepdims=True) + jnp.exp(sk[u] - m[u])) for u in units]
    o = [_dot(e[u], v_of[u]) * r_den[u] for u in units]
    outs = [o[u][i * qn:(i + 1) * qn, :] for u in units for i in range(stack)]
    return jnp.concatenate(outs, axis=1) * _silu(z_attn)


def _conf_glu(cacg, p_cc):
    c0 = cacg[:, 0:512] * jax.nn.sigmoid(cacg[:, 512:1024])
    pc0 = p_cc[:, 0:512] * jax.nn.sigmoid(p_cc[:, 512:1024])
    return jnp.concatenate([pc0, c0], axis=0)


def _conf_tail(conv_out, z_conf, lnw, lnb):
    xc = conv_out - jnp.mean(conv_out, axis=-1, keepdims=True)
    yln = xc * lax.rsqrt(jnp.mean(xc * xc, axis=-1, keepdims=True) + EPS) * lnw + lnb
    return _silu(yln) * _silu(z_conf)


def _kv_bias(bias_scr, first_step, not_first):
    _, rows, cols = bias_scr.shape
    qn = cols // 2

    @pl.when(first_step)
    def _():
        ii = lax.broadcasted_iota(jnp.int32, (rows, cols), 0) & (qn - 1)
        jj = lax.broadcasted_iota(jnp.int32, (rows, cols), 1)
        d = jj - ii
        band = (d >= 1) & (d <= qn)
        bias_scr[0] = jnp.where(band & (jj >= qn), 0.0, -jnp.inf)
        bias_scr[1] = jnp.where(band, 0.0, -jnp.inf)

    return bias_scr[not_first.astype(jnp.int32)]


def _my_place():
    return lax.axis_index("x"), lax.axis_index("y"), lax.axis_index("c")


def _all_gather(arrs, name):
    n = len(arrs)

    def body(*refs):
        ins, outs = refs[:n], refs[n:2 * n]
        send_sems, recv_sems, local_sems = refs[2 * n:]
        x, y, c = _my_place()
        me, sibling = (x, y, c), (x, y, 1 - c)
        chips = [(1 - x, y), (x, 1 - y), (1 - x, 1 - y)]

        def slot(a, p):
            return outs[a].at[4 * p[0] + 2 * p[1] + p[2]]

        def copy(a, kk, block, to, src=None):
            return pltpu.make_async_remote_copy(
                src_ref=slot(a, block) if src is None else src, dst_ref=slot(a, block),
                send_sem=send_sems.at[a, kk], recv_sem=recv_sems.at[a, kk],
                device_id=to, device_id_type=pl.DeviceIdType.MESH)

        mine = [pltpu.make_async_copy(ins[a], slot(a, me), local_sems.at[a]) for a in range(n)]
        for cp in mine:
            cp.start()
        first = []
        for a in range(n):
            first.append(copy(a, 0, me, sibling, src=ins[a]))
            first += [copy(a, 1 + j, me, (*chip, c), src=ins[a]) for j, chip in enumerate(chips)]
        for cp in first:
            cp.start()
        passed = []
        for j, chip in enumerate(chips):
            for a in range(n):
                copy(a, 1 + j, (*chip, c), me).wait_recv()
                fwd = copy(a, 4 + j, (*chip, c), sibling)
                fwd.start()
                passed.append(fwd)
        for a in range(n):
            copy(a, 0, sibling, me).wait_recv()
            for j, chip in enumerate(chips):
                copy(a, 4 + j, (*chip, 1 - c), me).wait_recv()
        for cp in first + passed:
            cp.wait_send()
        for cp in mine:
            cp.wait()

    any_spec = pl.BlockSpec(memory_space=pl.ANY)
    return pl.pallas_call(
        body, name=name,
        out_shape=[jax.ShapeDtypeStruct((N_DEV,) + a.shape, a.dtype) for a in arrs],
        in_specs=[any_spec] * n, out_specs=[any_spec] * n,
        scratch_shapes=[pltpu.SemaphoreType.DMA((n, 7)), pltpu.SemaphoreType.DMA((n, 7)),
                        pltpu.SemaphoreType.DMA((n,))],
    )(*arrs)


GATHER, SCATTER, CHIP_SCATTER = "gather", "scatter", "chip_scatter"


def _direct_copies(mode, ins, outs, send_sems, recv_sems, local_sems):
    x, y, c = _my_place()
    by_chip = mode == CHIP_SCATTER
    place = (lambda px, py, pc: 2 * px + py) if by_chip else (lambda px, py, pc: 4 * px + 2 * py + pc)
    me_idx = place(x, y, c)
    n = len(ins)
    local = [pltpu.make_async_copy(ins[a] if mode == GATHER else ins[a].at[me_idx], outs[a].at[me_idx],
                                   local_sems.at[a]) for a in range(n)]
    remote = []
    for rel in range(1, N_DEV):
        if by_chip and rel & 1:
            continue
        px = 1 - x if rel & 4 else x
        py = 1 - y if rel & 2 else y
        pc = 1 - c if rel & 1 else c
        for a in range(n):
            remote.append(pltpu.make_async_remote_copy(
                src_ref=ins[a] if mode == GATHER else ins[a].at[place(px, py, pc)], dst_ref=outs[a].at[me_idx],
                send_sem=send_sems.at[a, rel - 1], recv_sem=recv_sems.at[a, rel - 1],
                device_id=(px, py, pc), device_id_type=pl.DeviceIdType.MESH))
    return local + remote


def _side_scratch(n):
    return [pltpu.SemaphoreType.DMA((n, 7)), pltpu.SemaphoreType.DMA((n, 7)), pltpu.SemaphoreType.DMA((n,))]


def _side_out_shapes(mode, arrs):
    return [jax.ShapeDtypeStruct((N_DEV,) + a.shape if mode == GATHER else a.shape, a.dtype) for a in arrs]


def _pallas_with_side(body, side, first, last, n_in, n_out, *, in_specs, out_specs, out_shape, scratch_shapes=(),
                      args, **kwargs):
    side_arrs = [] if side is None else list(side[1])
    ns = len(side_arrs)

    def wrapped(*refs):
        own_in, side_in = refs[:n_in], refs[n_in:n_in + ns]
        o = n_in + ns
        own_out, side_out = refs[o:o + n_out], refs[o + n_out:o + n_out + ns]
        scratch = refs[o + n_out + ns:]
        own_scratch, sems = (scratch[:-3], scratch[-3:]) if ns else (scratch, ())
        if ns:
            @pl.when(first())
            def _():
                for cp in _direct_copies(side[0], side_in, side_out, *sems):
                    cp.start()

        body(*own_in, *own_out, *own_scratch)
        if ns:
            @pl.when(last())
            def _():
                for cp in _direct_copies(side[0], side_in, side_out, *sems):
                    cp.wait()

    any_spec = pl.BlockSpec(memory_space=pl.ANY)
    res = pl.pallas_call(
        wrapped,
        in_specs=list(in_specs) + [any_spec] * ns,
        out_specs=list(out_specs) + [any_spec] * ns,
        out_shape=list(out_shape) + (_side_out_shapes(side[0], side_arrs) if ns else []),
        scratch_shapes=list(scratch_shapes) + (_side_scratch(ns) if ns else []),
        **kwargs,
    )(*args, *side_arrs)
    return res[:n_out], res[n_out:]


def _exchange_small(everyone, per_device, name):
    n_all, n_own = everyone.shape[1], per_device.shape[2]

    def body(all_ref, own_ref, all_sum, own_sum, all_slots, own_slots, *sems):
        copies = _direct_copies(GATHER, [all_ref], [all_slots], *sems[:3])
        copies += _direct_copies(SCATTER, [own_ref], [own_slots], *sems[3:])
        for cp in copies:
            cp.start()
        for cp in copies:
            cp.wait()
        for slots, out in ((all_slots, all_sum), (own_slots, own_sum)):
            total = slots[0]
            for i in range(1, N_DEV):
                total = total + slots[i]
            out[...] = total

    vmem_spec = pl.BlockSpec(memory_space=pltpu.VMEM)
    return pl.pallas_call(
        body, name=name,
        out_shape=[jax.ShapeDtypeStruct((1, n_all), F32), jax.ShapeDtypeStruct((1, n_own), F32)],
        in_specs=[vmem_spec, vmem_spec], out_specs=[vmem_spec, vmem_spec],
        scratch_shapes=[pltpu.VMEM((N_DEV, 1, n_all), F32), pltpu.VMEM((N_DEV, 1, n_own), F32)]
                       + _side_scratch(1) + _side_scratch(1),
    )(everyone, per_device)


def _full(shape):
    return pl.BlockSpec(shape, lambda *_: (0,) * len(shape))


def _inproj_fwd(x, nw, w, name, side=None):
    t = x.shape[0]
    tm = 256

    def body(x_ref, nw_ref, w_ref, proj_ref, h_ref):
        h = _rmsnorm(x_ref[...], nw_ref[...]).astype(BF16)
        h_ref[...] = h
        for j in range(N_COL_TILES):
            sl = slice(j * COL_TILE, (j + 1) * COL_TILE)
            proj_ref[:, sl] = jnp.dot(h, w_ref[:, sl], preferred_element_type=F32)

    grid = (t // tm,)
    return _pallas_with_side(
        body, side, *_grid_ends(grid), 3, 2, name=name, grid=grid,
        out_shape=[jax.ShapeDtypeStruct((t, PROJ_W), F32), jax.ShapeDtypeStruct((t, D_MODEL), BF16)],
        in_specs=[pl.BlockSpec((tm, D_MODEL), lambda i: (i, 0)), _full((1, D_MODEL)), _full((D_MODEL, PROJ_W))],
        out_specs=[pl.BlockSpec((tm, PROJ_W), lambda i: (i, 0)), pl.BlockSpec((tm, D_MODEL), lambda i: (i, 0))],
        compiler_params=pltpu.CompilerParams(dimension_semantics=("arbitrary",), vmem_limit_bytes=VMEM_LIMIT),
        args=(x, nw, w))


def _param_specs():
    return [_full((4, 1536)), _full((1, 1536)), _full((1, 128)), _full((1, 128)), _full((1, 128)),
            _full((1, 1024)), _full((1, 128)), _full((CONF_KERNEL, 512)), _full((1, 512)), _full((1, 512)),
            _full((1, 512))]


def _halo_specs(nc, chunk_of):
    def prev_chunk(b, j):
        return jnp.maximum(b * nc + chunk_of(j) - 1, 0)

    per_xbc = CHUNK // XBC_HALO
    per_cc = CHUNK // CONF_HALO
    return [
        pl.BlockSpec((XBC_HALO, 1536), lambda b, j: (prev_chunk(b, j) * per_xbc + per_xbc - 1, C_XBC // 1536)),
        pl.BlockSpec((CHUNK, 256), lambda b, j: (prev_chunk(b, j), C_K // 256)),
        pl.BlockSpec((CONF_HALO, 1024), lambda b, j: (prev_chunk(b, j) * per_cc + per_cc - 1, C_CA // 1024)),
    ]


def _grid_ends(grid):
    first = lambda: functools.reduce(lambda p, q: p & q, [pl.program_id(i) == 0 for i in range(len(grid))])
    last = lambda: functools.reduce(lambda p, q: p & q, [pl.program_id(i) == n - 1 for i, n in enumerate(grid)])
    return first, last


def _mixer_fwd(x, proj, w_out, params, nb, name, side=None, head=None):
    t = x.shape[0]
    nc = t // nb // CHUNK
    n_head = 0 if head is None else 2

    def body(x_ref, cur_ref, pxbc_ref, pkv_ref, pcc_ref, wo_ref, *rest):
        prm = [r[...] for r in rest[:11]]
        head_in = rest[11:11 + n_head]
        xn_ref, sall_ref, conv_ref = rest[11 + n_head:14 + n_head]
        head_out = rest[14 + n_head:14 + 2 * n_head]
        s_scr, bias_scr = rest[14 + 2 * n_head:16 + 2 * n_head]
        c = pl.program_id(1)
        not_first = c > 0
        nf = not_first.astype(F32)
        kv_bias = _kv_bias(bias_scr, (pl.program_id(0) == 0) & (c == 0), not_first)

        @pl.when(c == 0)
        def _():
            s_scr[...] = jnp.zeros_like(s_scr)

        cw, cb, dtb, alog, dsk, nw, snk, dww, dwb, lnw, lnb = prm
        s_in = s_scr[...]
        sall_ref[0] = s_in
        ssd_conv = _dwconv(jnp.concatenate([pxbc_ref[...] * nf, cur_ref[:, C_XBC:C_XBC + 1536]], axis=0), cw, cb,
                           XBC_HALO)
        y_ssd, s_out = _ssd_part(cur_ref[:, 0:1024], ssd_conv, cur_ref[:, C_DT:C_DT + 128], s_in, dtb, alog, dsk, nw)
        s_scr[...] = s_out
        y_attn = _attn_part(cur_ref[:, 1024:1536], cur_ref[:, C_Q:C_Q + 512], cur_ref[:, C_K:C_K + 256],
                            pkv_ref[...] * nf, snk, kv_bias, ATTN_STACK_FWD)
        conv_out = _dwconv(_conf_glu(cur_ref[:, C_CA:C_CA + 1024], pcc_ref[...] * nf), dww, dwb, CONF_HALO)
        conv_ref[...] = conv_out
        y_conf = _conf_tail(conv_out, cur_ref[:, 1536:2048], lnw, lnb)
        x_next = (x_ref[...] + _dot(y_ssd, wo_ref[0:1024, :]) + _dot(y_attn, wo_ref[1024:1536, :])
                  + _dot(y_conf, wo_ref[1536:2048, :]))
        if head is None:
            xn_ref[...] = x_next
        else:
            fnw_ref, tgt_ref = head_in
            loss_ref, gfnw_ref = head_out

            @pl.when((pl.program_id(0) == 0) & (c == 0))
            def _():
                loss_ref[...] = jnp.zeros_like(loss_ref)
                gfnw_ref[...] = jnp.zeros_like(gfnw_ref)

            y, vjp = jax.vjp(_rmsnorm, x_next, fnw_ref[...])
            err = y - tgt_ref[...]
            loss_ref[...] += 0.5 * jnp.sum(jnp.mean(err * err, axis=-1, keepdims=True), axis=0, keepdims=True)
            xn_ref[...], d_fnw = vjp(err * (1.0 / D_MODEL))
            gfnw_ref[...] += d_fnw

    row = lambda b, j: (b * nc + j, 0)
    grid = (nb, nc)
    return _pallas_with_side(
        body, side, *_grid_ends(grid), 17 + n_head, 3 + n_head, name=name, grid=grid,
        out_shape=[jax.ShapeDtypeStruct((t, D_MODEL), F32),
                   jax.ShapeDtypeStruct((nb * nc, SSD_HEADS * SSD_HEAD_DIM, SSD_STATE), F32),
                   jax.ShapeDtypeStruct((t, 512), F32)]
                  + [jax.ShapeDtypeStruct(s, F32) for s in ((1, 1), (1, D_MODEL))[:n_head]],
        in_specs=[pl.BlockSpec((CHUNK, D_MODEL), row), pl.BlockSpec((CHUNK, PROJ_W), row)]
                 + _halo_specs(nc, lambda j: j) + [_full((MIX_WIDTH, D_MODEL))] + _param_specs()
                 + [_full((1, D_MODEL)), pl.BlockSpec((CHUNK, D_MODEL), row)][:n_head],
        out_specs=[pl.BlockSpec((CHUNK, D_MODEL), row),
                   pl.BlockSpec((1, SSD_HEADS * SSD_HEAD_DIM, SSD_STATE), lambda b, j: (b * nc + j, 0, 0)),
                   pl.BlockSpec((CHUNK, 512), row)] + [_full((1, 1)), _full((1, D_MODEL))][:n_head],
        scratch_shapes=[pltpu.VMEM((SSD_HEADS * SSD_HEAD_DIM, SSD_STATE), F32),
                        pltpu.VMEM((2, ATTN_STACK_FWD * CHUNK, 2 * CHUNK), F32)],
        compiler_params=pltpu.CompilerParams(dimension_semantics=("arbitrary", "arbitrary"),
                                             vmem_limit_bytes=VMEM_LIMIT),
        args=(x, proj, proj, proj, proj, w_out, *params, *(head or ())))


def _mixer_bwd(dxn, proj, s_all, conv_all, w_out, params, nb, name):
    t = dxn.shape[0]
    nc = t // nb // CHUNK
    n_prm = 11

    def body(dxn_ref, cur_ref, pxbc_ref, pkv_ref, pcc_ref, s_ref, conv_ref, wo_ref, *rest):
        prm = [r[...] for r in rest[:n_prm]]
        dproj_ref, ycat_ref = rest[n_prm:n_prm + 2]
        gprm = rest[n_prm + 2:2 * n_prm + 2]
        ds_scr, pend_xbc, pend_kv, pend_cc, bias_scr = rest[2 * n_prm + 2:]
        b, j = pl.program_id(0), pl.program_id(1)
        c = nc - 1 - j
        not_first = c > 0
        nf = not_first.astype(F32)
        kv_bias = _kv_bias(bias_scr, (b == 0) & (j == 0), not_first)

        @pl.when((b == 0) & (j == 0))
        def _():
            for r in gprm:
                r[...] = jnp.zeros_like(r)

        @pl.when(j == 0)
        def _():
            ds_scr[...] = jnp.zeros_like(ds_scr)
            pend_xbc[...] = jnp.zeros_like(pend_xbc)
            pend_kv[...] = jnp.zeros_like(pend_kv)
            pend_cc[...] = jnp.zeros_like(pend_cc)

        cw, cb, dtb, alog, dsk, nw, snk, dww, dwb, lnw, lnb = prm
        g_cw, g_cb, g_dtb, g_alog, g_dsk, g_nw, g_snk, g_dww, g_dwb, g_lnw, g_lnb = gprm
        dxn_v = dxn_ref[...]

        def add_tail(d_cur, pending):
            lead = jnp.zeros((CHUNK - pending.shape[0], pending.shape[1]), F32)
            return d_cur + jnp.concatenate([lead, pending], axis=0)

        y, vjp = jax.vjp(_conf_tail, conv_ref[...], cur_ref[:, 1536:2048], lnw, lnb)
        ycat_ref[:, 1536:2048] = y.astype(BF16)
        d_conv, dz, d_lnw, d_lnb = vjp(_dot_nt(dxn_v, wo_ref[1536:2048, :]))
        ext, vjp = jax.vjp(_conf_glu, cur_ref[:, C_CA:C_CA + 1024], pcc_ref[...] * nf)
        d_ext, d_dww, d_dwb = _dwconv_bwd(CONF_HALO, (ext, dww), d_conv)
        dcacg, dpcc = vjp(d_ext)
        dproj_ref[:, 1536:2048] = dz.astype(BF16)
        dproj_ref[:, C_CA:C_CA + 1024] = add_tail(dcacg, pend_cc[...]).astype(BF16)
        pend_cc[...] = dpcc
        for r, g in ((g_dww, d_dww), (g_dwb, d_dwb), (g_lnw, d_lnw), (g_lnb, d_lnb)):
            r[...] += g

        attn = functools.partial(_attn_part, kv_bias=kv_bias, stack=ATTN_STACK_BWD)
        y, vjp = jax.vjp(attn, cur_ref[:, 1024:1536], cur_ref[:, C_Q:C_Q + 512], cur_ref[:, C_K:C_K + 256],
                         pkv_ref[...] * nf, snk)
        ycat_ref[:, 1024:1536] = y.astype(BF16)
        dz, dq, dkv, dpkv, d_snk = vjp(_dot_nt(dxn_v, wo_ref[1024:1536, :]))
        dproj_ref[:, 1024:1536] = dz.astype(BF16)
        dproj_ref[:, C_Q:C_Q + 512] = dq.astype(BF16)
        dproj_ref[:, C_K:C_K + 256] = (dkv + pend_kv[...]).astype(BF16)
        pend_kv[...] = dpkv
        g_snk[...] += d_snk

        ext = jnp.concatenate([pxbc_ref[...] * nf, cur_ref[:, C_XBC:C_XBC + 1536]], axis=0)
        (y, _), vjp = jax.vjp(_ssd_part, cur_ref[:, 0:1024], _dwconv(ext, cw, cb, XBC_HALO),
                              cur_ref[:, C_DT:C_DT + 128], s_ref[0], dtb, alog, dsk, nw)
        ycat_ref[:, 0:1024] = y.astype(BF16)
        dz, d_conv, ddtr, ds_in, d_dtb, d_alog, d_dsk, d_nw = vjp((_dot_nt(dxn_v, wo_ref[0:1024, :]), ds_scr[...]))
        d_ext, d_cw, d_cb = _dwconv_bwd(XBC_HALO, (ext, cw), d_conv)
        dpxbc, dxbc = d_ext[0:XBC_HALO, :], d_ext[XBC_HALO:, :]
        dproj_ref[:, 0:1024] = dz.astype(BF16)
        dproj_ref[:, C_XBC:C_XBC + 1536] = add_tail(dxbc, pend_xbc[...]).astype(BF16)
        dproj_ref[:, C_DT:C_DT + 128] = ddtr.astype(BF16)
        dproj_ref[:, C_DT + 128:PROJ_W] = jnp.zeros((CHUNK, PROJ_W - C_DT - 128), BF16)
        pend_xbc[...] = dpxbc
        ds_scr[...] = ds_in
        for r, g in ((g_cw, d_cw), (g_cb, d_cb), (g_dtb, d_dtb), (g_alog, d_alog), (g_dsk, d_dsk), (g_nw, d_nw)):
            r[...] += g

    row = lambda b, j: (b * nc + nc - 1 - j, 0)
    prm_shapes = [(4, 1536), (1, 1536), (1, 128), (1, 128), (1, 128), (1, 1024), (1, 128), (CONF_KERNEL, 512),
                  (1, 512), (1, 512), (1, 512)]
    grid = (nb, nc)
    return _pallas_with_side(
        body, None, *_grid_ends(grid), 8 + n_prm, 2 + n_prm, name=name, grid=grid,
        out_shape=[jax.ShapeDtypeStruct((t, PROJ_W), BF16), jax.ShapeDtypeStruct((t, MIX_WIDTH), BF16)]
                  + [jax.ShapeDtypeStruct(s, F32) for s in prm_shapes],
        in_specs=[pl.BlockSpec((CHUNK, D_MODEL), row), pl.BlockSpec((CHUNK, PROJ_W), row)]
                 + _halo_specs(nc, lambda j: nc - 1 - j)
                 + [pl.BlockSpec((1, SSD_HEADS * SSD_HEAD_DIM, SSD_STATE), lambda b, j: (b * nc + nc - 1 - j, 0, 0)),
                    pl.BlockSpec((CHUNK, 512), row), _full((MIX_WIDTH, D_MODEL))] + _param_specs(),
        out_specs=[pl.BlockSpec((CHUNK, PROJ_W), row), pl.BlockSpec((CHUNK, MIX_WIDTH), row)]
                  + [_full(s) for s in prm_shapes],
        scratch_shapes=[pltpu.VMEM((SSD_HEADS * SSD_HEAD_DIM, SSD_STATE), F32), pltpu.VMEM((XBC_HALO, 1536), F32),
                        pltpu.VMEM((CHUNK, 256), F32), pltpu.VMEM((CONF_HALO, 1024), F32),
                        pltpu.VMEM((2, ATTN_STACK_BWD * CHUNK, 2 * CHUNK), F32)],
        compiler_params=pltpu.CompilerParams(dimension_semantics=("arbitrary", "arbitrary"),
                                             vmem_limit_bytes=VMEM_LIMIT),
        args=(dxn, proj, proj, proj, proj, s_all, conv_all, w_out, *params))


def _gw_out(y_cat, dxn, name):
    t = y_cat.shape[0]
    tk = 512

    def body(y_ref, dxn_ref, out_ref, acc):
        k = pl.program_id(0)

        @pl.when(k == 0)
        def _():
            acc[...] = jnp.zeros_like(acc)

        acc[...] += _dot_tn(y_ref[...], dxn_ref[...])

        @pl.when(k == t // tk - 1)
        def _():
            out_ref[...] = acc[...].astype(BF16)

    out = pl.pallas_call(
        body, name=name, grid=(t // tk,),
        out_shape=jax.ShapeDtypeStruct((MIX_WIDTH, D_MODEL), BF16),
        in_specs=[pl.BlockSpec((tk, MIX_WIDTH), lambda k: (k, 0)), pl.BlockSpec((tk, D_MODEL), lambda k: (k, 0))],
        out_specs=_full((MIX_WIDTH, D_MODEL)),
        scratch_shapes=[pltpu.VMEM((MIX_WIDTH, D_MODEL), F32)],
        compiler_params=pltpu.CompilerParams(dimension_semantics=("arbitrary",), vmem_limit_bytes=VMEM_LIMIT),
    )(y_cat, dxn)
    return out.reshape(N_DEV, MIX_WIDTH // N_DEV, D_MODEL)


def _inproj_bwd_x(dproj, w, x, nw, dxn, name, side=None):
    t = x.shape[0]
    tm = 256

    def body(dp_ref, w_ref, x_ref, nw_ref, dxn_ref, dx_ref, gnw_ref):
        @pl.when(pl.program_id(0) == 0)
        def _():
            gnw_ref[...] = jnp.zeros_like(gnw_ref)

        dh = jnp.zeros((tm, D_MODEL), F32)
        for j in range(N_COL_TILES):
            sl = slice(j * COL_TILE, (j + 1) * COL_TILE)
            dh = dh + _dot_nt(dp_ref[:, sl], w_ref[:, sl])
        _, vjp = jax.vjp(_rmsnorm, x_ref[...], nw_ref[...])
        dx, dnw = vjp(dh)
        dx_ref[...] = dxn_ref[...] + dx
        gnw_ref[...] += dnw

    tok = lambda i: (i, 0)
    grid = (t // tm,)
    return _pallas_with_side(
        body, side, *_grid_ends(grid), 5, 2, name=name, grid=grid,
        out_shape=[jax.ShapeDtypeStruct((t, D_MODEL), F32), jax.ShapeDtypeStruct((1, D_MODEL), F32)],
        in_specs=[pl.BlockSpec((tm, PROJ_W), tok), _full((D_MODEL, PROJ_W)), pl.BlockSpec((tm, D_MODEL), tok),
                  _full((1, D_MODEL)), pl.BlockSpec((tm, D_MODEL), tok)],
        out_specs=[pl.BlockSpec((tm, D_MODEL), tok), _full((1, D_MODEL))],
        compiler_params=pltpu.CompilerParams(dimension_semantics=("arbitrary",), vmem_limit_bytes=VMEM_LIMIT),
        args=(dproj, w, x, nw, dxn))


def _inproj_bwd_w(h, dproj, name, side=None):
    t = h.shape[0]
    tk = min(t, 1024)

    def body(h_ref, dp_ref, gw_ref):
        @pl.when(pl.program_id(1) == 0)
        def _():
            gw_ref[...] = jnp.zeros_like(gw_ref)

        gw_ref[...] += _dot_tn(h_ref[...], dp_ref[...])

    grid = (N_COL_TILES, t // tk)
    return _pallas_with_side(
        body, side, *_grid_ends(grid), 2, 1, name=name, grid=grid,
        out_shape=[jax.ShapeDtypeStruct((D_MODEL, PROJ_W), F32)],
        in_specs=[pl.BlockSpec((tk, D_MODEL), lambda n, k: (k, 0)), pl.BlockSpec((tk, COL_TILE), lambda n, k: (k, n))],
        out_specs=[pl.BlockSpec((D_MODEL, COL_TILE), lambda n, k: (0, n))],
        compiler_params=pltpu.CompilerParams(dimension_semantics=("arbitrary", "arbitrary"),
                                             vmem_limit_bytes=VMEM_LIMIT),
        args=(h, dproj))


def _repack_runs():
    pieces = ((0, 2048, C_Z), (2048, 3584, C_XBC), (3584, 3600, C_DT), (3600, 4368, C_Q), (4368, D_IN_PROJ, C_CA))
    per = D_IN_PROJ // N_DEV
    runs = []
    for j in range(N_DEV):
        lo, hi = per * j, per * (j + 1)
        for a, b, dst in pieces:
            s, e = max(lo, a), min(hi, b)
            if s < e:
                runs.append((j, s - lo, e - lo, dst + s - a))
    return runs


def _repack_w_in(g, name):
    tr = 256

    def body(g_ref, o_ref):
        for j, a, b, dst in _repack_runs():
            o_ref[:, dst:dst + b - a] = g_ref[j, :, a:b]
        o_ref[:, C_DT + 16:PROJ_W] = jnp.zeros((tr, PROJ_W - C_DT - 16), g.dtype)

    return pl.pallas_call(
        body, name=name, grid=(D_MODEL // tr,),
        out_shape=jax.ShapeDtypeStruct((D_MODEL, PROJ_W), g.dtype),
        in_specs=[pl.BlockSpec((N_DEV, tr, D_IN_PROJ // N_DEV), lambda i: (0, i, 0))],
        out_specs=pl.BlockSpec((tr, PROJ_W), lambda i: (i, 0)),
        compiler_params=pltpu.CompilerParams(dimension_semantics=("arbitrary",)),
    )(g)


def _unpack_gw_in(g, name):
    tr = 256

    def body(g_ref, o_ref):
        for j, a, b, dst in _repack_runs():
            o_ref[j, :, a:b] = g_ref[:, dst:dst + b - a].astype(BF16)

    return pl.pallas_call(
        body, name=name, grid=(D_MODEL // tr,),
        out_shape=jax.ShapeDtypeStruct((N_DEV, D_MODEL, D_IN_PROJ // N_DEV), BF16),
        in_specs=[pl.BlockSpec((tr, PROJ_W), lambda i: (i, 0))],
        out_specs=pl.BlockSpec((N_DEV, tr, D_IN_PROJ // N_DEV), lambda i: (0, i, 0)),
        compiler_params=pltpu.CompilerParams(dimension_semantics=("arbitrary",)),
    )(g)


def _pair_sum(parts, name):
    n_dev, r, cdim = parts.shape
    n_chip = n_dev // 2
    by_chip = parts.reshape(n_chip, 2, r, cdim)

    def swap_body(p_ref, got_ref, send_sem, recv_sem):
        x, y, c = _my_place()
        cp = pltpu.make_async_remote_copy(
            src_ref=p_ref.at[:, pl.ds(1 - c, 1)], dst_ref=got_ref, send_sem=send_sem, recv_sem=recv_sem,
            device_id=(x, y, 1 - c), device_id_type=pl.DeviceIdType.MESH)
        cp.start()
        cp.wait()

    any_spec = pl.BlockSpec(memory_space=pl.ANY)
    got = pl.pallas_call(
        swap_body, name=name + "_swap",
        out_shape=jax.ShapeDtypeStruct((n_chip, 1, r, cdim), parts.dtype),
        in_specs=[any_spec], out_specs=any_spec,
        scratch_shapes=[pltpu.SemaphoreType.DMA, pltpu.SemaphoreType.DMA],
    )(by_chip)

    tr = 256

    def add_body(core_ref, p_ref, got_ref, o_ref):
        o_ref[0] = (p_ref[0, 0].astype(F32) + got_ref[0, 0].astype(F32)).astype(o_ref.dtype)

    return pl.pallas_call(
        add_body, name=name + "_add",
        grid_spec=pltpu.PrefetchScalarGridSpec(
            num_scalar_prefetch=1, grid=(n_chip, r // tr),
            in_specs=[pl.BlockSpec((1, 1, tr, cdim), lambda k, i, core: (k, core[0], i, 0)),
                      pl.BlockSpec((1, 1, tr, cdim), lambda k, i, core: (k, 0, i, 0))],
            out_specs=pl.BlockSpec((1, tr, cdim), lambda k, i, core: (k, i, 0))),
        out_shape=jax.ShapeDtypeStruct((n_chip, r, cdim), parts.dtype),
        compiler_params=pltpu.CompilerParams(dimension_semantics=("arbitrary", "arbitrary")),
    )(lax.axis_index("c").astype(jnp.int32).reshape(1), by_chip, got)


def _adamw(w, g, m, v):
    m = ADAM_B1 * m + (1.0 - ADAM_B1) * g
    v = ADAM_B2 * v + (1.0 - ADAM_B2) * jnp.square(g)
    m_hat = m / (1.0 - ADAM_B1 ** ADAM_STEP)
    v_hat = v / (1.0 - ADAM_B2 ** ADAM_STEP)
    delta = -ADAM_LR * (m_hat / (jnp.sqrt(v_hat) + ADAM_EPS) + ADAM_WD * w)
    return delta, m, v


def _reduce_adamw(parts, w, m, v, tr, name):
    depth = len(parts)
    p, r, cdim = parts[0].shape
    n_blk = r // tr

    def body(*refs):
        p_refs = refs[:depth]
        w_ref, m_ref, v_ref, g_ref, d_ref, nm_ref, nv_ref = refs[depth:]
        for layer in range(depth):
            @pl.when(pl.program_id(0) == layer)
            def _(p_ref=p_refs[layer]):
                g = p_ref[0].astype(F32)
                for i in range(1, p):
                    g = g + p_ref[i].astype(F32)
                g_ref[0] = g
                d_ref[0], nm_ref[0], nv_ref[0] = _adamw(w_ref[0], g, m_ref[0], v_ref[0])

    def parts_spec(layer):
        return pl.BlockSpec((p, tr, cdim), lambda d, i: (0, jnp.clip(i + (d - layer) * n_blk, 0, n_blk - 1), 0))

    blk = pl.BlockSpec((1, tr, cdim), lambda d, i: (d, i, 0))
    return pl.pallas_call(
        body, name=name, grid=(depth, n_blk),
        out_shape=[jax.ShapeDtypeStruct(w.shape, F32)] * 4,
        in_specs=[parts_spec(layer) for layer in range(depth)] + [blk, blk, blk],
        out_specs=[blk] * 4,
        compiler_params=pltpu.CompilerParams(dimension_semantics=("arbitrary", "arbitrary"),
                                             vmem_limit_bytes=VMEM_LIMIT),
    )(*parts, w, m, v)


def _adamw_small(ssum, entries, name):
    direct = [e[3] for e in entries if not isinstance(e[3], list)]
    n_direct = len(direct)

    def body(*refs):
        ssum_ref, direct_refs = refs[0], list(refs[1:1 + n_direct])
        ins = refs[1 + n_direct:1 + n_direct + 3 * len(entries)]
        outs = refs[1 + n_direct + 3 * len(entries):]
        for k, (w, _, _, grad) in enumerate(entries):
            w_ref, m_ref, v_ref = ins[3 * k:3 * k + 3]
            g_ref, d_ref, nm_ref, nv_ref = outs[4 * k:4 * k + 4]
            if isinstance(grad, list):
                for row, off in enumerate(grad):
                    rows = slice(row, row + 1)
                    g = ssum_ref[:, off:off + w.shape[1]]
                    g_ref[rows, :] = g
                    d_ref[rows, :], nm_ref[rows, :], nv_ref[rows, :] = _adamw(w_ref[rows, :], g, m_ref[rows, :],
                                                                              v_ref[rows, :])
            else:
                g = direct_refs.pop(0)[...]
                g_ref[...] = g
                d_ref[...], nm_ref[...], nv_ref[...] = _adamw(w_ref[...], g, m_ref[...], v_ref[...])

    vmem = pl.BlockSpec(memory_space=pltpu.VMEM)
    args = [ssum] + direct + [a for e in entries for a in e[:3]]
    res = pl.pallas_call(
        body, name=name,
        out_shape=[jax.ShapeDtypeStruct(e[0].shape, F32) for e in entries for _ in range(4)],
        in_specs=[vmem] * len(args), out_specs=[vmem] * (4 * len(entries)),
    )(*args)
    return [res[4 * k:4 * k + 4] for k in range(len(entries))]


def _reduce_adamw_cols(parts, w, m, v, name):
    depth = len(parts)
    _, r, cdim = parts[0].shape
    tc = 512

    def body(*refs):
        p_refs = refs[:depth]
        w_ref, m_ref, v_ref, g_ref, d_ref, nm_ref, nv_ref = refs[depth:]
        for layer in range(depth):
            g = p_refs[layer][0].astype(F32)
            for i in range(1, parts[layer].shape[0]):
                g = g + p_refs[layer][i].astype(F32)
            g = g.T
            g_ref[:, layer, :] = g
            d_ref[:, layer, :], nm_ref[:, layer, :], nv_ref[:, layer, :] = _adamw(
                w_ref[:, layer, :], g, m_ref[:, layer, :], v_ref[:, layer, :])

    view = lambda a: jnp.transpose(a, (2, 0, 1))
    blk = pl.BlockSpec((cdim, depth, tc), lambda i: (0, 0, i))
    outs = pl.pallas_call(
        body, name=name, grid=(r // tc,),
        out_shape=[jax.ShapeDtypeStruct((cdim, depth, r), F32)] * 4,
        in_specs=[pl.BlockSpec((a.shape[0], tc, cdim), lambda i: (0, i, 0)) for a in parts] + [blk, blk, blk],
        out_specs=[blk] * 4,
        compiler_params=pltpu.CompilerParams(dimension_semantics=("arbitrary",), vmem_limit_bytes=VMEM_LIMIT),
    )(*parts, view(w), view(m), view(v))
    return [jnp.transpose(o, (1, 2, 0)) for o in outs]


def _pad_lanes(v, width=128):
    return jnp.pad(v.reshape(1, -1), ((0, 0), (0, width - v.shape[-1])))


SMALL_FIELDS = (("norm_w", 1024), ("conv_b", 1536), ("dt_bias", 128), ("a_log", 128), ("d_skip", 128),
                ("ssd_norm_w", 1024), ("sinks", 128), ("dw_b", 512), ("ln_w", 512), ("ln_b", 512))


def kernel(x, norm_w, w_in, ssd_conv_w, ssd_conv_b, ssd_dt_bias, ssd_a_log, ssd_d, ssd_norm_w, attn_sinks, conf_dw_w, conf_dw_b, conf_ln_w, conf_ln_b, w_out, final_norm_w, loss_target, m_norm_w, m_w_in, m_ssd_conv_w, m_ssd_conv_b, m_ssd_dt_bias, m_ssd_a_log, m_ssd_d, m_ssd_norm_w, m_attn_sinks, m_conf_dw_w, m_conf_dw_b, m_conf_ln_w, m_conf_ln_b, m_w_out, m_final_norm_w, v_norm_w, v_w_in, v_ssd_conv_w, v_ssd_conv_b, v_ssd_dt_bias, v_ssd_a_log, v_ssd_d, v_ssd_norm_w, v_attn_sinks, v_conf_dw_w, v_conf_dw_b, v_conf_ln_w, v_conf_ln_b, v_w_out, v_final_norm_w):
    nb, seq, _ = x.shape
    depth = norm_w.shape[0]
    t = nb * seq

    w_in_bf, w_out_bf = w_in.astype(BF16), w_out.astype(BF16)
    g_win0, g_cw, g_dw = _all_gather([w_in_bf[0], ssd_conv_w, conf_dw_w], "gather_weights")
    w_in_full = [_repack_w_in(g_win0, "repack_w_in_0")]
    w_out_full = []
    conv_w_full = [jnp.transpose(g_cw[:, l], (1, 0, 2)).reshape(4, 1536) for l in range(depth)]
    dw_w_full = [jnp.transpose(g_dw[:, l], (1, 0, 2)).reshape(CONF_KERNEL, 512) for l in range(depth)]

    def layer_params(l):
        return [conv_w_full[l], ssd_conv_b[l].reshape(1, -1), _pad_lanes(ssd_dt_bias[l]), _pad_lanes(ssd_a_log[l]),
                _pad_lanes(ssd_d[l]), ssd_norm_w[l].reshape(1, -1), _pad_lanes(attn_sinks[l]), dw_w_full[l],
                conf_dw_b[l].reshape(1, -1), conf_ln_w[l].reshape(1, -1), conf_ln_b[l].reshape(1, -1)]

    xs = [x.reshape(t, D_MODEL)]
    saved = []
    for l in range(depth):
        (proj, h), gathered = _inproj_fwd(xs[l], norm_w[l].reshape(1, -1), w_in_full[l], f"inproj_fwd_{l}",
                                          (GATHER, [w_out_bf[l]]))
        w_out_full.append(gathered[0].reshape(MIX_WIDTH, D_MODEL))
        last = l + 1 == depth
        side = None if last else (GATHER, [w_in_bf[l + 1]])
        head = (final_norm_w.reshape(1, -1), loss_target.reshape(t, D_MODEL)) if last else None
        res, gathered = _mixer_fwd(xs[l], proj, w_out_full[l], layer_params(l), nb, f"mixer_fwd_{l}", side, head)
        if side:
            w_in_full.append(_repack_w_in(gathered[0], f"repack_w_in_{l + 1}"))
        saved.append((proj, h, res[1:3]))
        xs.append(res[0])
    dx, loss_part, g_fnw = res[0], res[3], res[4]

    small_rows, shard_rows = [None] * depth, [None] * depth
    received = [None] * depth
    for l in reversed(range(depth)):
        proj, h, kept = saved[l]
        res, _ = _mixer_bwd(dx, proj, *kept, w_out_full[l], layer_params(l), nb, f"mixer_bwd_{l}")
        dproj, y_cat = res[0], res[1]
        g_cw_l, g_cb, g_dtb, g_alog, g_dsk, g_nw, g_snk, g_dww, g_dwb, g_lnw, g_lnb = res[2:]
        gw_out_parts = _gw_out(y_cat, dx, f"gw_out_{l}")
        (gw_in,), got_out = _inproj_bwd_w(h, dproj, f"inproj_bwd_w_{l}", (SCATTER, [gw_out_parts]))
        chip_parts = _pair_sum(_unpack_gw_in(gw_in, f"unpack_gw_in_{l}"), f"pair_sum_{l}")
        (dx, g_norm), got_in = _inproj_bwd_x(dproj, w_in_full[l], xs[l], norm_w[l].reshape(1, -1), dx,
                                             f"inproj_bwd_x_{l}", (CHIP_SCATTER, [chip_parts]))
        received[l] = [got_in[0], got_out[0]]
        small_rows[l] = [g_norm, g_cb, g_dtb, g_alog, g_dsk, g_nw, g_snk, g_dwb, g_lnw, g_lnb]
        shard_rows[l] = [jnp.transpose(g.reshape(g.shape[0], N_DEV, -1), (1, 0, 2)).reshape(N_DEV, -1)
                         for g in (g_cw_l, g_dww)]
    grad_x = dx.reshape(nb, seq, D_MODEL)

    small = jnp.concatenate([piece for l in range(depth) for piece in small_rows[l]]
                            + [g_fnw, _pad_lanes(loss_part)], axis=1)
    shard_small = jnp.concatenate([piece for l in range(depth) for piece in shard_rows[l]], axis=1)
    ssum, shard_sum = _exchange_small(small, shard_small.reshape(N_DEV, 1, -1), "exchange_small")
    loss = ssum[0, small.shape[1] - 128]

    g_w_in, d_w_in, nm_w_in, nv_w_in = _reduce_adamw_cols([received[l][0] for l in range(depth)], w_in, m_w_in,
                                                          v_w_in, "adamw_w_in")
    g_w_out, d_w_out, nm_w_out, nv_w_out = _reduce_adamw([received[l][1] for l in range(depth)], w_out, m_w_out,
                                                         v_w_out, 256, "adamw_w_out")

    per_layer = sum(n for _, n in SMALL_FIELDS)
    given = {"norm_w": (norm_w, m_norm_w, v_norm_w), "conv_b": (ssd_conv_b, m_ssd_conv_b, v_ssd_conv_b),
             "dt_bias": (ssd_dt_bias, m_ssd_dt_bias, v_ssd_dt_bias), "a_log": (ssd_a_log, m_ssd_a_log, v_ssd_a_log),
             "d_skip": (ssd_d, m_ssd_d, v_ssd_d), "ssd_norm_w": (ssd_norm_w, m_ssd_norm_w, v_ssd_norm_w),
             "sinks": (attn_sinks, m_attn_sinks, v_attn_sinks), "dw_b": (conf_dw_b, m_conf_dw_b, v_conf_dw_b),
             "ln_w": (conf_ln_w, m_conf_ln_w, v_conf_ln_w), "ln_b": (conf_ln_b, m_conf_ln_b, v_conf_ln_b)}
    entries = []
    off = 0
    for fname, n in SMALL_FIELDS:
        entries.append((*given[fname], [l * per_layer + off for l in range(depth)]))
        off += n
    n_cw, n_dw = ssd_conv_w[0].size, conf_dw_w[0].size
    per_layer_shard = n_cw + n_dw
    cw_grad = jnp.stack([shard_sum[0, l * per_layer_shard:l * per_layer_shard + n_cw].reshape(ssd_conv_w.shape[1:])
                         for l in range(depth)], axis=0)
    dw_grad = jnp.stack([shard_sum[0, l * per_layer_shard + n_cw:(l + 1) * per_layer_shard]
                         .reshape(conf_dw_w.shape[1:]) for l in range(depth)], axis=0)
    entries.append((ssd_conv_w, m_ssd_conv_w, v_ssd_conv_w, cw_grad))
    entries.append((conf_dw_w, m_conf_dw_w, v_conf_dw_w, dw_grad))
    entries.append((final_norm_w.reshape(1, -1), m_final_norm_w.reshape(1, -1), v_final_norm_w.reshape(1, -1),
                    [depth * per_layer]))
    sm = _adamw_small(ssum, entries, "adamw_small")
    sm = {k: quad for k, quad in zip([f for f, _ in SMALL_FIELDS] + ["conv_w", "dw_w", "final"], sm)}

    def outputs(i, big_in_i, big_out_i):
        return [sm["norm_w"][i], big_in_i, sm["conv_w"][i], sm["conv_b"][i], sm["dt_bias"][i], sm["a_log"][i],
                sm["d_skip"][i], sm["ssd_norm_w"][i], sm["sinks"][i], sm["dw_w"][i], sm["dw_b"][i], sm["ln_w"][i],
                sm["ln_b"][i], big_out_i, sm["final"][i].reshape(-1)]

    return (loss, grad_x, *outputs(0, g_w_in, g_w_out), *outputs(1, d_w_in, d_w_out),
            *outputs(2, nm_w_in, nm_w_out), *outputs(3, nv_w_in, nv_w_out))
```

```python
import functools

import jax
import jax.numpy as jnp
from jax import lax
from jax.experimental import pallas as pl
from jax.experimental.pallas import tpu as pltpu

F32 = jnp.float32
BF16 = jnp.bfloat16
N_DEV = 8
EPS = 1e-5

D_MODEL = 1024
CHUNK = 128
SSD_HEADS = 16
SSD_HEAD_DIM = 64
SSD_STATE = 128
ATTN_HEADS = 8
ATTN_HEAD_DIM = 64
CONF_KERNEL = 31
MIX_WIDTH = 2048
D_IN_PROJ = 5392
C_Z = 0
C_CA = 2048
C_XBC = 3072
C_Q = 4608
C_K = 5120
C_DT = 5376
PROJ_W = 5632
N_COL_TILES = 4
COL_TILE = PROJ_W // N_COL_TILES
XBC_HALO = 8
CONF_HALO = 32
ATTN_STACK_FWD = 2
ATTN_STACK_BWD = 4
VMEM_LIMIT = 56 * 1024 * 1024

ADAM_LR = 0.001
ADAM_B1 = 0.9
ADAM_B2 = 0.999
ADAM_EPS = 1e-08
ADAM_WD = 0.01
ADAM_STEP = 10


def _silu(v):
    return v * jax.nn.sigmoid(v)


def _softplus(v):
    return jnp.maximum(v, 0.0) + jnp.log1p(jnp.exp(-jnp.abs(v)))


def _rmsnorm(v, w):
    return v * lax.rsqrt(jnp.mean(v * v, axis=-1, keepdims=True) + EPS) * w


def _dot(a, b):
    return jnp.dot(a.astype(BF16), b.astype(BF16), preferred_element_type=F32)


def _dot_nt(a, b):
    return lax.dot_general(a.astype(BF16), b.astype(BF16), (((1,), (1,)), ((), ())), preferred_element_type=F32)


def _dot_tn(a, b):
    return lax.dot_general(a.astype(BF16), b.astype(BF16), (((0,), (0,)), ((), ())), preferred_element_type=F32)


def _taps(ext, offs, out_len, w=None, g=None):
    n_rows, n_cols = ext.shape
    by_shift = {}
    for t, off in enumerate(offs):
        by_shift.setdefault(off % 8, []).append((t, off))
    for r, taps in by_shift.items():
        assert max(off for _, off in taps) - r + out_len <= n_rows - r
    accs = []
    sums = [[None] * (n_cols // 128) for _ in offs]
    for blk in range(n_cols // 128):
        cs = slice(blk * 128, (blk + 1) * 128)
        e = ext[:, cs]
        acc = None
        for r, taps in by_shift.items():
            shifted = e if r == 0 else pltpu.roll(e, n_rows - r, axis=0)
            for t, off in taps:
                window = shifted[off - r:off - r + out_len, :]
                if w is not None:
                    term = w[t:t + 1, cs] * window
                    acc = term if acc is None else acc + term
                if g is not None:
                    sums[t][blk] = jnp.sum(g[:, cs] * window, axis=0, keepdims=True)
        accs.append(acc)
    if w is not None:
        return jnp.concatenate(accs, axis=1)
    return jnp.concatenate([jnp.concatenate(row, axis=1) for row in sums], axis=0)


@functools.partial(jax.custom_vjp, nondiff_argnums=(3,))
def _dwconv(ext, w, b, halo):
    kk = w.shape[0]
    return b + _taps(ext, [halo - (kk - 1) + t for t in range(kk)], ext.shape[0] - halo, w=w)


def _dwconv_fwd(ext, w, b, halo):
    return _dwconv(ext, w, b, halo), (ext, w)


def _dwconv_bwd(halo, res, g):
    ext, w = res
    kk = w.shape[0]
    offs = [halo - (kk - 1) + t for t in range(kk)]
    dw = _taps(ext, offs, ext.shape[0] - halo, g=g)
    zeros = jnp.zeros((halo, g.shape[1]), g.dtype)
    gp = jnp.concatenate([zeros, g, zeros], axis=0)
    dext = _taps(gp, [halo - off for off in offs], ext.shape[0], w=w)
    return dext, dw, jnp.sum(g, axis=0, keepdims=True)


_dwconv.defvjp(_dwconv_fwd, _dwconv_bwd)


def _ssd_part(z_ssd, conv_out, dtr, s_in, dtb, alog, dsk, nw):
    qn = conv_out.shape[0]
    nh = SSD_HEADS
    per_group = nh // 2
    n_pair = nh // 2
    xa = _silu(conv_out)
    xs = xa[:, 0:1024]
    dt = _softplus(dtr + dtb)
    a = dt * (-jnp.exp(alog))
    rows = lax.broadcasted_iota(jnp.int32, (qn, qn), 0)
    cols = lax.broadcasted_iota(jnp.int32, (qn, qn), 1)
    causal = rows >= cols
    low = cols < SSD_HEAD_DIM
    a_cs = jnp.dot(causal.astype(F32), a, precision=lax.Precision.HIGHEST, preferred_element_type=F32)
    a_cs_t = a_cs.T
    bgs = [xa[:, 1024 + g * 128:1024 + (g + 1) * 128] for g in range(2)]
    cgs = [xa[:, 1280 + g * 128:1280 + (g + 1) * 128] for g in range(2)]
    cbms = [_dot_nt(cgs[g], bgs[g]) for g in range(2)]
    colb = [jnp.broadcast_to(a_cs[:, h:h + 1], (qn, qn)) for h in range(nh)]
    dtb_wide = [jnp.broadcast_to(dt[:, h:h + 1], (qn, qn)) for h in range(nh)]
    lmats = [jnp.exp(jnp.where(causal, colb[h] - a_cs_t[h:h + 1, :], -jnp.inf)) for h in range(nh)]
    ms = [cbms[h // per_group] * lmats[h] for h in range(nh)]
    x_pair = [xs[:, p * 128:(p + 1) * 128] for p in range(n_pair)]
    xdt = [x_pair[p] * jnp.where(low, dtb_wide[2 * p], dtb_wide[2 * p + 1]) for p in range(n_pair)]
    x_lo = [jnp.where(low, xdt[p], 0.0) for p in range(n_pair)]
    x_hi = [jnp.where(low, 0.0, xdt[p]) for p in range(n_pair)]
    y_diag = [_dot(ms[2 * p], x_lo[p]) + _dot(ms[2 * p + 1], x_hi[p]) for p in range(n_pair)]
    col_pair = [jnp.where(low, colb[2 * p], colb[2 * p + 1]) for p in range(n_pair)]
    last_pair = [jnp.broadcast_to(col_pair[p][qn - 1:qn, :], (qn, qn)) for p in range(n_pair)]
    ecol = [jnp.exp(col_pair[p]) for p in range(n_pair)]
    xw = [xdt[p] * jnp.exp(last_pair[p] - col_pair[p]) for p in range(n_pair)]
    y_off, st = [], []
    for g in range(2):
        ps = range(g * n_pair // 2, (g + 1) * n_pair // 2)
        y_off.append(_dot_nt(cgs[g], s_in[g * 512:(g + 1) * 512, :]) * jnp.concatenate([ecol[p] for p in ps], axis=1))
        st.append(_dot_tn(jnp.concatenate([xw[p] for p in ps], axis=1), bgs[g]))
    e_last = jnp.exp(jnp.broadcast_to(a_cs_t[:, qn - 1:qn], (qn, SSD_STATE)))
    scale = jnp.concatenate([jnp.broadcast_to(e_last[h:h + 1, :], (64, SSD_STATE)) for h in range(nh)], axis=0)
    s_out = scale * s_in + jnp.concatenate(st, axis=0)
    d_wide = jnp.concatenate([jnp.broadcast_to(dsk[:, h:h + 1], (1, 64)) for h in range(nh)], axis=1)
    y = jnp.concatenate(y_diag, axis=1) + jnp.concatenate(y_off, axis=1) + d_wide * xs
    gated = y * _silu(z_ssd)
    halves = []
    for g in range(2):
        gg = gated[:, g * 512:(g + 1) * 512]
        halves.append(gg * lax.rsqrt(jnp.mean(gg * gg, axis=-1, keepdims=True) + EPS))
    return jnp.concatenate(halves, axis=1) * nw, s_out


def _attn_part(z_attn, q, kv, p_kv, snk, kv_bias, stack):
    qn = q.shape[0]
    kk = jnp.concatenate([p_kv[:, 0:128], kv[:, 0:128]], axis=0)
    vv = jnp.concatenate([p_kv[:, 128:256], kv[:, 128:256]], axis=0)
    units = range(ATTN_HEADS // stack)
    heads = [range(u * stack, (u + 1) * stack) for u in units]
    kv_of = [u * stack // (ATTN_HEADS // 2) for u in units]
    k_of = [kk[:, g * 64:(g + 1) * 64] for g in kv_of]
    v_of = [vv[:, g * 64:(g + 1) * 64] for g in kv_of]
    qs = [jnp.concatenate([q[:, h * 64:(h + 1) * 64] for h in heads[u]], axis=0) for u in units]
    sk = [jnp.concatenate([jnp.broadcast_to(snk[:, h:h + 1], (qn, 1)) for h in heads[u]], axis=0) for u in units]
    s = [_dot_nt(qs[u], k_of[u]) * (ATTN_HEAD_DIM ** -0.5) + kv_bias for u in units]
    m = [lax.stop_gradient(jnp.maximum(jnp.max(s[u], axis=-1, keepdims=True), sk[u])) for u in units]
    e = [jnp.exp(s[u] - m[u]) for u in units]
    r_den = [1.0 / (jnp.sum(e[u], axis=-1, keepdims=True) + jnp.exp(sk[u] - m[u])) for u in units]
    o = [_dot(e[u], v_of[u]) * r_den[u] for u in units]
    outs = [o[u][i * qn:(i + 1) * qn, :] for u in units for i in range(stack)]
    return jnp.concatenate(outs, axis=1) * _silu(z_attn)


def _conf_glu(cacg, p_cc):
    c0 = cacg[:, 0:512] * jax.nn.sigmoid(cacg[:, 512:1024])
    pc0 = p_cc[:, 0:512] * jax.nn.sigmoid(p_cc[:, 512:1024])
    return jnp.concatenate([pc0, c0], axis=0)


def _conf_tail(conv_out, z_conf, lnw, lnb):
    xc = conv_out - jnp.mean(conv_out, axis=-1, keepdims=True)
    yln = xc * lax.rsqrt(jnp.mean(xc * xc, axis=-1, keepdims=True) + EPS) * lnw + lnb
    return _silu(yln) * _silu(z_conf)


def _kv_bias(bias_scr, first_step, not_first):
    _, rows, cols = bias_scr.shape
    qn = cols // 2

    @pl.when(first_step)
    def _():
        ii = lax.broadcasted_iota(jnp.int32, (rows, cols), 0) & (qn - 1)
        jj = lax.broadcasted_iota(jnp.int32, (rows, cols), 1)
        d = jj - ii
        band = (d >= 1) & (d <= qn)
        bias_scr[0] = jnp.where(band & (jj >= qn), 0.0, -jnp.inf)
        bias_scr[1] = jnp.where(band, 0.0, -jnp.inf)

    return bias_scr[not_first.astype(jnp.int32)]


def _my_place():
    return lax.axis_index("x"), lax.axis_index("y"), lax.axis_index("c")


def _all_gather(arrs, name):
    n = len(arrs)

    def body(*refs):
        ins, outs = refs[:n], refs[n:2 * n]
        send_sems, recv_sems, local_sems = refs[2 * n:]
        x, y, c = _my_place()
        me, sibling = (x, y, c), (x, y, 1 - c)
        chips = [(1 - x, y), (x, 1 - y), (1 - x, 1 - y)]

        def slot(a, p):
            return outs[a].at[4 * p[0] + 2 * p[1] + p[2]]

        def copy(a, kk, block, to, src=None):
            return pltpu.make_async_remote_copy(
                src_ref=slot(a, block) if src is None else src, dst_ref=slot(a, block),
                send_sem=send_sems.at[a, kk], recv_sem=recv_sems.at[a, kk],
                device_id=to, device_id_type=pl.DeviceIdType.MESH)

        mine = [pltpu.make_async_copy(ins[a], slot(a, me), local_sems.at[a]) for a in range(n)]
        for cp in mine:
            cp.start()
        first = []
        for a in range(n):
            first.append(copy(a, 0, me, sibling, src=ins[a]))
            first += [copy(a, 1 + j, me, (*chip, c), src=ins[a]) for j, chip in enumerate(chips)]
        for cp in first:
            cp.start()
        passed = []
        for j, chip in enumerate(chips):
            for a in range(n):
                copy(a, 1 + j, (*chip, c), me).wait_recv()
                fwd = copy(a, 4 + j, (*chip, c), sibling)
                fwd.start()
                passed.append(fwd)
        for a in range(n):
            copy(a, 0, sibling, me).wait_recv()
            for j, chip in enumerate(chips):
                copy(a, 4 + j, (*chip, 1 - c), me).wait_recv()
        for cp in first + passed:
            cp.wait_send()
        for cp in mine:
            cp.wait()

    any_spec = pl.BlockSpec(memory_space=pl.ANY)
    return pl.pallas_call(
        body, name=name,
        out_shape=[jax.ShapeDtypeStruct((N_DEV,) + a.shape, a.dtype) for a in arrs],
        in_specs=[any_spec] * n, out_specs=[any_spec] * n,
        scratch_shapes=[pltpu.SemaphoreType.DMA((n, 7)), pltpu.SemaphoreType.DMA((n, 7)),
                        pltpu.SemaphoreType.DMA((n,))],
    )(*arrs)


GATHER, SCATTER, CHIP_SCATTER = "gather", "scatter", "chip_scatter"


def _direct_copies(mode, ins, outs, send_sems, recv_sems, local_sems):
    x, y, c = _my_place()
    by_chip = mode == CHIP_SCATTER
    place = (lambda px, py, pc: 2 * px + py) if by_chip else (lambda px, py, pc: 4 * px + 2 * py + pc)
    me_idx = place(x, y, c)
    n = len(ins)
    local = [pltpu.make_async_copy(ins[a] if mode == GATHER else ins[a].at[me_idx], outs[a].at[me_idx],
                                   local_sems.at[a]) for a in range(n)]
    remote = []
    for rel in range(1, N_DEV):
        if by_chip and rel & 1:
            continue
        px = 1 - x if rel & 4 else x
        py = 1 - y if rel & 2 else y
        pc = 1 - c if rel & 1 else c
        for a in range(n):
            remote.append(pltpu.make_async_remote_copy(
                src_ref=ins[a] if mode == GATHER else ins[a].at[place(px, py, pc)], dst_ref=outs[a].at[me_idx],
                send_sem=send_sems.at[a, rel - 1], recv_sem=recv_sems.at[a, rel - 1],
                device_id=(px, py, pc), device_id_type=pl.DeviceIdType.MESH))
    return local + remote


def _side_scratch(n):
    return [pltpu.SemaphoreType.DMA((n, 7)), pltpu.SemaphoreType.DMA((n, 7)), pltpu.SemaphoreType.DMA((n,))]


def _side_out_shapes(mode, arrs):
    return [jax.ShapeDtypeStruct((N_DEV,) + a.shape if mode == GATHER else a.shape, a.dtype) for a in arrs]


def _pallas_with_side(body, side, first, last, n_in, n_out, *, in_specs, out_specs, out_shape, scratch_shapes=(),
                      args, **kwargs):
    side_arrs = [] if side is None else list(side[1])
    ns = len(side_arrs)

    def wrapped(*refs):
        own_in, side_in = refs[:n_in], refs[n_in:n_in + ns]
        o = n_in + ns
        own_out, side_out = refs[o:o + n_out], refs[o + n_out:o + n_out + ns]
        scratch = refs[o + n_out + ns:]
        own_scratch, sems = (scratch[:-3], scratch[-3:]) if ns else (scratch, ())
        if ns:
            @pl.when(first())
            def _():
                for cp in _direct_copies(side[0], side_in, side_out, *sems):
                    cp.start()

        body(*own_in, *own_out, *own_scratch)
        if ns:
            @pl.when(last())
            def _():
                for cp in _direct_copies(side[0], side_in, side_out, *sems):
                    cp.wait()

    any_spec = pl.BlockSpec(memory_space=pl.ANY)
    res = pl.pallas_call(
        wrapped,
        in_specs=list(in_specs) + [any_spec] * ns,
        out_specs=list(out_specs) + [any_spec] * ns,
        out_shape=list(out_shape) + (_side_out_shapes(side[0], side_arrs) if ns else []),
        scratch_shapes=list(scratch_shapes) + (_side_scratch(ns) if ns else []),
        **kwargs,
    )(*args, *side_arrs)
    return res[:n_out], res[n_out:]


def _exchange_small(everyone, per_device, name):
    n_all, n_own = everyone.shape[1], per_device.shape[2]

    def body(all_ref, own_ref, all_sum, own_sum, all_slots, own_slots, *sems):
        copies = _direct_copies(GATHER, [all_ref], [all_slots], *sems[:3])
        copies += _direct_copies(SCATTER, [own_ref], [own_slots], *sems[3:])
        for cp in copies:
            cp.start()
        for cp in copies:
            cp.wait()
        for slots, out in ((all_slots, all_sum), (own_slots, own_sum)):
            total = slots[0]
            for i in range(1, N_DEV):
                total = total + slots[i]
            out[...] = total

    vmem_spec = pl.BlockSpec(memory_space=pltpu.VMEM)
    return pl.pallas_call(
        body, name=name,
        out_shape=[jax.ShapeDtypeStruct((1, n_all), F32), jax.ShapeDtypeStruct((1, n_own), F32)],
        in_specs=[vmem_spec, vmem_spec], out_specs=[vmem_spec, vmem_spec],
        scratch_shapes=[pltpu.VMEM((N_DEV, 1, n_all), F32), pltpu.VMEM((N_DEV, 1, n_own), F32)]
                       + _side_scratch(1) + _side_scratch(1),
    )(everyone, per_device)


def _full(shape):
    return pl.BlockSpec(shape, lambda *_: (0,) * len(shape))


def _inproj_fwd(x, nw, w, name, side=None):
    t = x.shape[0]
    tm = 256

    def body(x_ref, nw_ref, w_ref, proj_ref, h_ref):
        h = _rmsnorm(x_ref[...], nw_ref[...]).astype(BF16)
        h_ref[...] = h
        for j in range(N_COL_TILES):
            sl = slice(j * COL_TILE, (j + 1) * COL_TILE)
            proj_ref[:, sl] = jnp.dot(h, w_ref[:, sl], preferred_element_type=F32)

    grid = (t // tm,)
    return _pallas_with_side(
        body, side, *_grid_ends(grid), 3, 2, name=name, grid=grid,
        out_shape=[jax.ShapeDtypeStruct((t, PROJ_W), F32), jax.ShapeDtypeStruct((t, D_MODEL), BF16)],
        in_specs=[pl.BlockSpec((tm, D_MODEL), lambda i: (i, 0)), _full((1, D_MODEL)), _full((D_MODEL, PROJ_W))],
        out_specs=[pl.BlockSpec((tm, PROJ_W), lambda i: (i, 0)), pl.BlockSpec((tm, D_MODEL), lambda i: (i, 0))],
        compiler_params=pltpu.CompilerParams(dimension_semantics=("arbitrary",), vmem_limit_bytes=VMEM_LIMIT),
        args=(x, nw, w))


def _param_specs():
    return [_full((4, 1536)), _full((1, 1536)), _full((1, 128)), _full((1, 128)), _full((1, 128)),
            _full((1, 1024)), _full((1, 128)), _full((CONF_KERNEL, 512)), _full((1, 512)), _full((1, 512)),
            _full((1, 512))]


def _halo_specs(nc, chunk_of):
    def prev_chunk(b, j):
        return jnp.maximum(b * nc + chunk_of(j) - 1, 0)

    per_xbc = CHUNK // XBC_HALO
    per_cc = CHUNK // CONF_HALO
    return [
        pl.BlockSpec((XBC_HALO, 1536), lambda b, j: (prev_chunk(b, j) * per_xbc + per_xbc - 1, C_XBC // 1536)),
        pl.BlockSpec((CHUNK, 256), lambda b, j: (prev_chunk(b, j), C_K // 256)),
        pl.BlockSpec((CONF_HALO, 1024), lambda b, j: (prev_chunk(b, j) * per_cc + per_cc - 1, C_CA // 1024)),
    ]


def _grid_ends(grid):
    first = lambda: functools.reduce(lambda p, q: p & q, [pl.program_id(i) == 0 for i in range(len(grid))])
    last = lambda: functools.reduce(lambda p, q: p & q, [pl.program_id(i) == n - 1 for i, n in enumerate(grid)])
    return first, last


def _mixer_fwd(x, proj, w_out, params, nb, name, side=None, head=None):
    t = x.shape[0]
    nc = t // nb // CHUNK
    n_head = 0 if head is None else 2

    def body(x_ref, cur_ref, pxbc_ref, pkv_ref, pcc_ref, wo_ref, *rest):
        prm = [r[...] for r in rest[:11]]
        head_in = rest[11:11 + n_head]
        xn_ref, sall_ref, conv_ref = rest[11 + n_head:14 + n_head]
        head_out = rest[14 + n_head:14 + 2 * n_head]
        s_scr, bias_scr = rest[14 + 2 * n_head:16 + 2 * n_head]
        c = pl.program_id(1)
        not_first = c > 0
        nf = not_first.astype(F32)
        kv_bias = _kv_bias(bias_scr, (pl.program_id(0) == 0) & (c == 0), not_first)

        @pl.when(c == 0)
        def _():
            s_scr[...] = jnp.zeros_like(s_scr)

        cw, cb, dtb, alog, dsk, nw, snk, dww, dwb, lnw, lnb = prm
        s_in = s_scr[...]
        sall_ref[0] = s_in
        ssd_conv = _dwconv(jnp.concatenate([pxbc_ref[...] * nf, cur_ref[:, C_XBC:C_XBC + 1536]], axis=0), cw, cb,
                           XBC_HALO)
        y_ssd, s_out = _ssd_part(cur_ref[:, 0:1024], ssd_conv, cur_ref[:, C_DT:C_DT + 128], s_in, dtb, alog, dsk, nw)
        s_scr[...] = s_out
        y_attn = _attn_part(cur_ref[:, 1024:1536], cur_ref[:, C_Q:C_Q + 512], cur_ref[:, C_K:C_K + 256],
                            pkv_ref[...] * nf, snk, kv_bias, ATTN_STACK_FWD)
        conv_out = _dwconv(_conf_glu(cur_ref[:, C_CA:C_CA + 1024], pcc_ref[...] * nf), dww, dwb, CONF_HALO)
        conv_ref[...] = conv_out
        y_conf = _conf_tail(conv_out, cur_ref[:, 1536:2048], lnw, lnb)
        x_next = (x_ref[...] + _dot(y_ssd, wo_ref[0:1024, :]) + _dot(y_attn, wo_ref[1024:1536, :])
                  + _dot(y_conf, wo_ref[1536:2048, :]))
        if head is None:
            xn_ref[...] = x_next
        else:
            fnw_ref, tgt_ref = head_in
            loss_ref, gfnw_ref = head_out

            @pl.when((pl.program_id(0) == 0) & (c == 0))
            def _():
                loss_ref[...] = jnp.zeros_like(loss_ref)
                gfnw_ref[...] = jnp.zeros_like(gfnw_ref)

            y, vjp = jax.vjp(_rmsnorm, x_next, fnw_ref[...])
            err = y - tgt_ref[...]
            loss_ref[...] += 0.5 * jnp.sum(jnp.mean(err * err, axis=-1, keepdims=True), axis=0, keepdims=True)
            xn_ref[...], d_fnw = vjp(err * (1.0 / D_MODEL))
            gfnw_ref[...] += d_fnw

    row = lambda b, j: (b * nc + j, 0)
    grid = (nb, nc)
    return _pallas_with_side(
        body, side, *_grid_ends(grid), 17 + n_head, 3 + n_head, name=name, grid=grid,
        out_shape=[jax.ShapeDtypeStruct((t, D_MODEL), F32),
                   jax.ShapeDtypeStruct((nb * nc, SSD_HEADS * SSD_HEAD_DIM, SSD_STATE), F32),
                   jax.ShapeDtypeStruct((t, 512), F32)]
                  + [jax.ShapeDtypeStruct(s, F32) for s in ((1, 1), (1, D_MODEL))[:n_head]],
        in_specs=[pl.BlockSpec((CHUNK, D_MODEL), row), pl.BlockSpec((CHUNK, PROJ_W), row)]
                 + _halo_specs(nc, lambda j: j) + [_full((MIX_WIDTH, D_MODEL))] + _param_specs()
                 + [_full((1, D_MODEL)), pl.BlockSpec((CHUNK, D_MODEL), row)][:n_head],
        out_specs=[pl.BlockSpec((CHUNK, D_MODEL), row),
                   pl.BlockSpec((1, SSD_HEADS * SSD_HEAD_DIM, SSD_STATE), lambda b, j: (b * nc + j, 0, 0)),
                   pl.BlockSpec((CHUNK, 512), row)] + [_full((1, 1)), _full((1, D_MODEL))][:n_head],
        scratch_shapes=[pltpu.VMEM((SSD_HEADS * SSD_HEAD_DIM, SSD_STATE), F32),
                        pltpu.VMEM((2, ATTN_STACK_FWD * CHUNK, 2 * CHUNK), F32)],
        compiler_params=pltpu.CompilerParams(dimension_semantics=("arbitrary", "arbitrary"),
                                             vmem_limit_bytes=VMEM_LIMIT),
        args=(x, proj, proj, proj, proj, w_out, *params, *(head or ())))


def _mixer_bwd(dxn, proj, s_all, conv_all, w_out, params, nb, name):
    t = dxn.shape[0]
    nc = t // nb // CHUNK
    n_prm = 11

    def body(dxn_ref, cur_ref, pxbc_ref, pkv_ref, pcc_ref, s_ref, conv_ref, wo_ref, *rest):
        prm = [r[...] for r in rest[:n_prm]]
        dproj_ref, ycat_ref = rest[n_prm:n_prm + 2]
        gprm = rest[n_prm + 2:2 * n_prm + 2]
        ds_scr, pend_xbc, pend_kv, pend_cc, bias_scr = rest[2 * n_prm + 2:]
        b, j = pl.program_id(0), pl.program_id(1)
        c = nc - 1 - j
        not_first = c > 0
        nf = not_first.astype(F32)
        kv_bias = _kv_bias(bias_scr, (b == 0) & (j == 0), not_first)

        @pl.when((b == 0) & (j == 0))
        def _():
            for r in gprm:
                r[...] = jnp.zeros_like(r)

        @pl.when(j == 0)
        def _():
            ds_scr[...] = jnp.zeros_like(ds_scr)
            pend_xbc[...] = jnp.zeros_like(pend_xbc)
            pend_kv[...] = jnp.zeros_like(pend_kv)
            pend_cc[...] = jnp.zeros_like(pend_cc)

        cw, cb, dtb, alog, dsk, nw, snk, dww, dwb, lnw, lnb = prm
        g_cw, g_cb, g_dtb, g_alog, g_dsk, g_nw, g_snk, g_dww, g_dwb, g_lnw, g_lnb = gprm
        dxn_v = dxn_ref[...]

        def add_tail(d_cur, pending):
            lead = jnp.zeros((CHUNK - pending.shape[0], pending.shape[1]), F32)
            return d_cur + jnp.concatenate([lead, pending], axis=0)

        y, vjp = jax.vjp(_conf_tail, conv_ref[...], cur_ref[:, 1536:2048], lnw, lnb)
        ycat_ref[:, 1536:2048] = y.astype(BF16)
        d_conv, dz, d_lnw, d_lnb = vjp(_dot_nt(dxn_v, wo_ref[1536:2048, :]))
        ext, vjp = jax.vjp(_conf_glu, cur_ref[:, C_CA:C_CA + 1024], pcc_ref[...] * nf)
        d_ext, d_dww, d_dwb = _dwconv_bwd(CONF_HALO, (ext, dww), d_conv)
        dcacg, dpcc = vjp(d_ext)
        dproj_ref[:, 1536:2048] = dz.astype(BF16)
        dproj_ref[:, C_CA:C_CA + 1024] = add_tail(dcacg, pend_cc[...]).astype(BF16)
        pend_cc[...] = dpcc
        for r, g in ((g_dww, d_dww), (g_dwb, d_dwb), (g_lnw, d_lnw), (g_lnb, d_lnb)):
            r[...] += g

        attn = functools.partial(_attn_part, kv_bias=kv_bias, stack=ATTN_STACK_BWD)
        y, vjp = jax.vjp(attn, cur_ref[:, 1024:1536], cur_ref[:, C_Q:C_Q + 512], cur_ref[:, C_K:C_K + 256],
                         pkv_ref[...] * nf, snk)
        ycat_ref[:, 1024:1536] = y.astype(BF16)
        dz, dq, dkv, dpkv, d_snk = vjp(_dot_nt(dxn_v, wo_ref[1024:1536, :]))
        dproj_ref[:, 1024:1536] = dz.astype(BF16)
        dproj_ref[:, C_Q:C_Q + 512] = dq.astype(BF16)
        dproj_ref[:, C_K:C_K + 256] = (dkv + pend_kv[...]).astype(BF16)
        pend_kv[...] = dpkv
        g_snk[...] += d_snk

        ext = jnp.concatenate([pxbc_ref[...] * nf, cur_ref[:, C_XBC:C_XBC + 1536]], axis=0)
        (y, _), vjp = jax.vjp(_ssd_part, cur_ref[:, 0:1024], _dwconv(ext, cw, cb, XBC_HALO),
                              cur_ref[:, C_DT:C_DT + 128], s_ref[0], dtb, alog, dsk, nw)
        ycat_ref[:, 0:1024] = y.astype(BF16)
        dz, d_conv, ddtr, ds_in, d_dtb, d_alog, d_dsk, d_nw = vjp((_dot_nt(dxn_v, wo_ref[0:1024, :]), ds_scr[...]))
        d_ext, d_cw, d_cb = _dwconv_bwd(XBC_HALO, (ext, cw), d_conv)
        dpxbc, dxbc = d_ext[0:XBC_HALO, :], d_ext[XBC_HALO:, :]
        dproj_ref[:, 0:1024] = dz.astype(BF16)
        dproj_ref[:, C_XBC:C_XBC + 1536] = add_tail(dxbc, pend_xbc[...]).astype(BF16)
        dproj_ref[:, C_DT:C_DT + 128] = ddtr.astype(BF16)
        dproj_ref[:, C_DT + 128:PROJ_W] = jnp.zeros((CHUNK, PROJ_W - C_DT - 128), BF16)
        pend_xbc[...] = dpxbc
        ds_scr[...] = ds_in
        for r, g in ((g_cw, d_cw), (g_cb, d_cb), (g_dtb, d_dtb), (g_alog, d_alog), (g_dsk, d_dsk), (g_nw, d_nw)):
            r[...] += g

    row = lambda b, j: (b * nc + nc - 1 - j, 0)
    prm_shapes = [(4, 1536), (1, 1536), (1, 128), (1, 128), (1, 128), (1, 1024), (1, 128), (CONF_KERNEL, 512),
                  (1, 512), (1, 512), (1, 512)]
    grid = (nb, nc)
    return _pallas_with_side(
        body, None, *_grid_ends(grid), 8 + n_prm, 2 + n_prm, name=name, grid=grid,
        out_shape=[jax.ShapeDtypeStruct((t, PROJ_W), BF16), jax.ShapeDtypeStruct((t, MIX_WIDTH), BF16)]
                  + [jax.ShapeDtypeStruct(s, F32) for s in prm_shapes],
        in_specs=[pl.BlockSpec((CHUNK, D_MODEL), row), pl.BlockSpec((CHUNK, PROJ_W), row)]
                 + _halo_specs(nc, lambda j: nc - 1 - j)
                 + [pl.BlockSpec((1, SSD_HEADS * SSD_HEAD_DIM, SSD_STATE), lambda b, j: (b * nc + nc - 1 - j, 0, 0)),
                    pl.BlockSpec((CHUNK, 512), row), _full((MIX_WIDTH, D_MODEL))] + _param_specs(),
        out_specs=[pl.BlockSpec((CHUNK, PROJ_W), row), pl.BlockSpec((CHUNK, MIX_WIDTH), row)]
                  + [_full(s) for s in prm_shapes],
        scratch_shapes=[pltpu.VMEM((SSD_HEADS * SSD_HEAD_DIM, SSD_STATE), F32), pltpu.VMEM((XBC_HALO, 1536), F32),
                        pltpu.VMEM((CHUNK, 256), F32), pltpu.VMEM((CONF_HALO, 1024), F32),
                        pltpu.VMEM((2, ATTN_STACK_BWD * CHUNK, 2 * CHUNK), F32)],
        compiler_params=pltpu.CompilerParams(dimension_semantics=("arbitrary", "arbitrary"),
                                             vmem_limit_bytes=VMEM_LIMIT),
        args=(dxn, proj, proj, proj, proj, s_all, conv_all, w_out, *params))


def _gw_out(y_cat, dxn, name):
    t = y_cat.shape[0]
    tk = min(t, 1024)

    def body(y_ref, dxn_ref, out_ref, acc):
        k = pl.program_id(0)

        @pl.when(k == 0)
        def _():
            acc[...] = jnp.zeros_like(acc)

        acc[...] += _dot_tn(y_ref[...], dxn_ref[...])

        @pl.when(k == t // tk - 1)
        def _():
            out_ref[...] = acc[...].astype(BF16)

    out = pl.pallas_call(
        body, name=name, grid=(t // tk,),
        out_shape=jax.ShapeDtypeStruct((MIX_WIDTH, D_MODEL), BF16),
        in_specs=[pl.BlockSpec((tk, MIX_WIDTH), lambda k: (k, 0)), pl.BlockSpec((tk, D_MODEL), lambda k: (k, 0))],
        out_specs=_full((MIX_WIDTH, D_MODEL)),
        scratch_shapes=[pltpu.VMEM((MIX_WIDTH, D_MODEL), F32)],
        compiler_params=pltpu.CompilerParams(dimension_semantics=("arbitrary",), vmem_limit_bytes=VMEM_LIMIT),
    )(y_cat, dxn)
    return out.reshape(N_DEV, MIX_WIDTH // N_DEV, D_MODEL)


def _inproj_bwd_x(dproj, w, x, nw, dxn, name, side=None):
    t = x.shape[0]
    tm = 256

    def body(dp_ref, w_ref, x_ref, nw_ref, dxn_ref, dx_ref, gnw_ref):
        @pl.when(pl.program_id(0) == 0)
        def _():
            gnw_ref[...] = jnp.zeros_like(gnw_ref)

        dh = jnp.zeros((tm, D_MODEL), F32)
        for j in range(N_COL_TILES):
            sl = slice(j * COL_TILE, (j + 1) * COL_TILE)
            dh = dh + _dot_nt(dp_ref[:, sl], w_ref[:, sl])
        _, vjp = jax.vjp(_rmsnorm, x_ref[...], nw_ref[...])
        dx, dnw = vjp(dh)
        dx_ref[...] = dxn_ref[...] + dx
        gnw_ref[...] += dnw

    tok = lambda i: (i, 0)
    grid = (t // tm,)
    return _pallas_with_side(
        body, side, *_grid_ends(grid), 5, 2, name=name, grid=grid,
        out_shape=[jax.ShapeDtypeStruct((t, D_MODEL), F32), jax.ShapeDtypeStruct((1, D_MODEL), F32)],
        in_specs=[pl.BlockSpec((tm, PROJ_W), tok), _full((D_MODEL, PROJ_W)), pl.BlockSpec((tm, D_MODEL), tok),
                  _full((1, D_MODEL)), pl.BlockSpec((tm, D_MODEL), tok)],
        out_specs=[pl.BlockSpec((tm, D_MODEL), tok), _full((1, D_MODEL))],
        compiler_params=pltpu.CompilerParams(dimension_semantics=("arbitrary",), vmem_limit_bytes=VMEM_LIMIT),
        args=(dproj, w, x, nw, dxn))


def _inproj_bwd_w(h, dproj, name, side=None):
    t = h.shape[0]
    tk = min(t, 1024)

    def body(h_ref, dp_ref, gw_ref):
        @pl.when(pl.program_id(1) == 0)
        def _():
            gw_ref[...] = jnp.zeros_like(gw_ref)

        gw_ref[...] += _dot_tn(h_ref[...], dp_ref[...])

    grid = (N_COL_TILES, t // tk)
    return _pallas_with_side(
        body, side, *_grid_ends(grid), 2, 1, name=name, grid=grid,
        out_shape=[jax.ShapeDtypeStruct((D_MODEL, PROJ_W), F32)],
        in_specs=[pl.BlockSpec((tk, D_MODEL), lambda n, k: (k, 0)), pl.BlockSpec((tk, COL_TILE), lambda n, k: (k, n))],
        out_specs=[pl.BlockSpec((D_MODEL, COL_TILE), lambda n, k: (0, n))],
        compiler_params=pltpu.CompilerParams(dimension_semantics=("arbitrary", "arbitrary"),
                                             vmem_limit_bytes=VMEM_LIMIT),
        args=(h, dproj))


def _repack_runs():
    pieces = ((0, 2048, C_Z), (2048, 3584, C_XBC), (3584, 3600, C_DT), (3600, 4368, C_Q), (4368, D_IN_PROJ, C_CA))
    per = D_IN_PROJ // N_DEV
    runs = []
    for j in range(N_DEV):
        lo, hi = per * j, per * (j + 1)
        for a, b, dst in pieces:
            s, e = max(lo, a), min(hi, b)
            if s < e:
                runs.append((j, s - lo, e - lo, dst + s - a))
    return runs


def _repack_w_in(g, name):
    tr = 256

    def body(g_ref, o_ref):
        for j, a, b, dst in _repack_runs():
            o_ref[:, dst:dst + b - a] = g_ref[j, :, a:b]
        o_ref[:, C_DT + 16:PROJ_W] = jnp.zeros((tr, PROJ_W - C_DT - 16), g.dtype)

    return pl.pallas_call(
        body, name=name, grid=(D_MODEL // tr,),
        out_shape=jax.ShapeDtypeStruct((D_MODEL, PROJ_W), g.dtype),
        in_specs=[pl.BlockSpec((N_DEV, tr, D_IN_PROJ // N_DEV), lambda i: (0, i, 0))],
        out_specs=pl.BlockSpec((tr, PROJ_W), lambda i: (i, 0)),
        compiler_params=pltpu.CompilerParams(dimension_semantics=("arbitrary",)),
    )(g)


def _unpack_gw_in(g, name):
    tr = 256

    def body(g_ref, o_ref):
        for j, a, b, dst in _repack_runs():
            o_ref[j, :, a:b] = g_ref[:, dst:dst + b - a].astype(BF16)

    return pl.pallas_call(
        body, name=name, grid=(D_MODEL // tr,),
        out_shape=jax.ShapeDtypeStruct((N_DEV, D_MODEL, D_IN_PROJ // N_DEV), BF16),
        in_specs=[pl.BlockSpec((tr, PROJ_W), lambda i: (i, 0))],
        out_specs=pl.BlockSpec((N_DEV, tr, D_IN_PROJ // N_DEV), lambda i: (0, i, 0)),
        compiler_params=pltpu.CompilerParams(dimension_semantics=("arbitrary",)),
    )(g)


def _pair_sum(parts, name):
    n_dev, r, cdim = parts.shape
    n_chip = n_dev // 2
    by_chip = parts.reshape(n_chip, 2, r, cdim)

    def swap_body(p_ref, got_ref, send_sem, recv_sem):
        x, y, c = _my_place()
        cp = pltpu.make_async_remote_copy(
            src_ref=p_ref.at[:, pl.ds(1 - c, 1)], dst_ref=got_ref, send_sem=send_sem, recv_sem=recv_sem,
            device_id=(x, y, 1 - c), device_id_type=pl.DeviceIdType.MESH)
        cp.start()
        cp.wait()

    any_spec = pl.BlockSpec(memory_space=pl.ANY)
    got = pl.pallas_call(
        swap_body, name=name + "_swap",
        out_shape=jax.ShapeDtypeStruct((n_chip, 1, r, cdim), parts.dtype),
        in_specs=[any_spec], out_specs=any_spec,
        scratch_shapes=[pltpu.SemaphoreType.DMA, pltpu.SemaphoreType.DMA],
    )(by_chip)

    tr = 256

    def add_body(core_ref, p_ref, got_ref, o_ref):
        o_ref[0] = (p_ref[0, 0].astype(F32) + got_ref[0, 0].astype(F32)).astype(o_ref.dtype)

    return pl.pallas_call(
        add_body, name=name + "_add",
        grid_spec=pltpu.PrefetchScalarGridSpec(
            num_scalar_prefetch=1, grid=(n_chip, r // tr),
            in_specs=[pl.BlockSpec((1, 1, tr, cdim), lambda k, i, core: (k, core[0], i, 0)),
                      pl.BlockSpec((1, 1, tr, cdim), lambda k, i, core: (k, 0, i, 0))],
            out_specs=pl.BlockSpec((1, tr, cdim), lambda k, i, core: (k, i, 0))),
        out_shape=jax.ShapeDtypeStruct((n_chip, r, cdim), parts.dtype),
        compiler_params=pltpu.CompilerParams(dimension_semantics=("arbitrary", "arbitrary")),
    )(lax.axis_index("c").astype(jnp.int32).reshape(1), by_chip, got)


def _adamw(w, g, m, v):
    m = ADAM_B1 * m + (1.0 - ADAM_B1) * g
    v = ADAM_B2 * v + (1.0 - ADAM_B2) * jnp.square(g)
    m_hat = m / (1.0 - ADAM_B1 ** ADAM_STEP)
    v_hat = v / (1.0 - ADAM_B2 ** ADAM_STEP)
    delta = -ADAM_LR * (m_hat / (jnp.sqrt(v_hat) + ADAM_EPS) + ADAM_WD * w)
    return delta, m, v


def _reduce_adamw(parts, w, m, v, tr, name):
    depth = len(parts)
    p, r, cdim = parts[0].shape
    n_blk = r // tr

    def body(*refs):
        p_refs = refs[:depth]
        w_ref, m_ref, v_ref, g_ref, d_ref, nm_ref, nv_ref = refs[depth:]
        for layer in range(depth):
            @pl.when(pl.program_id(0) == layer)
            def _(p_ref=p_refs[layer]):
                g = p_ref[0].astype(F32)
                for i in range(1, p):
                    g = g + p_ref[i].astype(F32)
                g_ref[0] = g
                d_ref[0], nm_ref[0], nv_ref[0] = _adamw(w_ref[0], g, m_ref[0], v_ref[0])

    def parts_spec(layer):
        return pl.BlockSpec((p, tr, cdim), lambda d, i: (0, jnp.clip(i + (d - layer) * n_blk, 0, n_blk - 1), 0))

    blk = pl.BlockSpec((1, tr, cdim), lambda d, i: (d, i, 0))
    return pl.pallas_call(
        body, name=name, grid=(depth, n_blk),
        out_shape=[jax.ShapeDtypeStruct(w.shape, F32)] * 4,
        in_specs=[parts_spec(layer) for layer in range(depth)] + [blk, blk, blk],
        out_specs=[blk] * 4,
        compiler_params=pltpu.CompilerParams(dimension_semantics=("arbitrary", "arbitrary"),
                                             vmem_limit_bytes=VMEM_LIMIT),
    )(*parts, w, m, v)


def _adamw_small(ssum, entries, name):
    direct = [e[3] for e in entries if not isinstance(e[3], list)]
    n_direct = len(direct)

    def body(*refs):
        ssum_ref, direct_refs = refs[0], list(refs[1:1 + n_direct])
        ins = refs[1 + n_direct:1 + n_direct + 3 * len(entries)]
        outs = refs[1 + n_direct + 3 * len(entries):]
        for k, (w, _, _, grad) in enumerate(entries):
            w_ref, m_ref, v_ref = ins[3 * k:3 * k + 3]
            g_ref, d_ref, nm_ref, nv_ref = outs[4 * k:4 * k + 4]
            if isinstance(grad, list):
                for row, off in enumerate(grad):
                    rows = slice(row, row + 1)
                    g = ssum_ref[:, off:off + w.shape[1]]
                    g_ref[rows, :] = g
                    d_ref[rows, :], nm_ref[rows, :], nv_ref[rows, :] = _adamw(w_ref[rows, :], g, m_ref[rows, :],
                                                                              v_ref[rows, :])
            else:
                g = direct_refs.pop(0)[...]
                g_ref[...] = g
                d_ref[...], nm_ref[...], nv_ref[...] = _adamw(w_ref[...], g, m_ref[...], v_ref[...])

    vmem = pl.BlockSpec(memory_space=pltpu.VMEM)
    args = [ssum] + direct + [a for e in entries for a in e[:3]]
    res = pl.pallas_call(
        body, name=name,
        out_shape=[jax.ShapeDtypeStruct(e[0].shape, F32) for e in entries for _ in range(4)],
        in_specs=[vmem] * len(args), out_specs=[vmem] * (4 * len(entries)),
    )(*args)
    return [res[4 * k:4 * k + 4] for k in range(len(entries))]


def _reduce_adamw_cols(parts, w, m, v, name):
    depth = len(parts)
    _, r, cdim = parts[0].shape
    tc = 512

    def body(*refs):
        p_refs = refs[:depth]
        w_ref, m_ref, v_ref, g_ref, d_ref, nm_ref, nv_ref = refs[depth:]
        for layer in range(depth):
            g = p_refs[layer][0].astype(F32)
            for i in range(1, parts[layer].shape[0]):
                g = g + p_refs[layer][i].astype(F32)
            g = g.T
            g_ref[:, layer, :] = g
            d_ref[:, layer, :], nm_ref[:, layer, :], nv_ref[:, layer, :] = _adamw(
                w_ref[:, layer, :], g, m_ref[:, layer, :], v_ref[:, layer, :])

    view = lambda a: jnp.transpose(a, (2, 0, 1))
    blk = pl.BlockSpec((cdim, depth, tc), lambda i: (0, 0, i))
    outs = pl.pallas_call(
        body, name=name, grid=(r // tc,),
        out_shape=[jax.ShapeDtypeStruct((cdim, depth, r), F32)] * 4,
        in_specs=[pl.BlockSpec((a.shape[0], tc, cdim), lambda i: (0, i, 0)) for a in parts] + [blk, blk, blk],
        out_specs=[blk] * 4,
        compiler_params=pltpu.CompilerParams(dimension_semantics=("arbitrary",), vmem_limit_bytes=VMEM_LIMIT),
    )(*parts, view(w), view(m), view(v))
    return [jnp.transpose(o, (1, 2, 0)) for o in outs]


def _pad_lanes(v, width=128):
    return jnp.pad(v.reshape(1, -1), ((0, 0), (0, width - v.shape[-1])))


SMALL_FIELDS = (("norm_w", 1024), ("conv_b", 1536), ("dt_bias", 128), ("a_log", 128), ("d_skip", 128),
                ("ssd_norm_w", 1024), ("sinks", 128), ("dw_b", 512), ("ln_w", 512), ("ln_b", 512))


def kernel(x, norm_w, w_in, ssd_conv_w, ssd_conv_b, ssd_dt_bias, ssd_a_log, ssd_d, ssd_norm_w, attn_sinks, conf_dw_w, conf_dw_b, conf_ln_w, conf_ln_b, w_out, final_norm_w, loss_target, m_norm_w, m_w_in, m_ssd_conv_w, m_ssd_conv_b, m_ssd_dt_bias, m_ssd_a_log, m_ssd_d, m_ssd_norm_w, m_attn_sinks, m_conf_dw_w, m_conf_dw_b, m_conf_ln_w, m_conf_ln_b, m_w_out, m_final_norm_w, v_norm_w, v_w_in, v_ssd_conv_w, v_ssd_conv_b, v_ssd_dt_bias, v_ssd_a_log, v_ssd_d, v_ssd_norm_w, v_attn_sinks, v_conf_dw_w, v_conf_dw_b, v_conf_ln_w, v_conf_ln_b, v_w_out, v_final_norm_w):
    nb, seq, _ = x.shape
    depth = norm_w.shape[0]
    t = nb * seq

    w_in_bf, w_out_bf = w_in.astype(BF16), w_out.astype(BF16)
    g_win0, g_cw, g_dw = _all_gather([w_in_bf[0], ssd_conv_w, conf_dw_w], "gather_weights")
    w_in_full = [_repack_w_in(g_win0, "repack_w_in_0")]
    w_out_full = []
    conv_w_full = [jnp.transpose(g_cw[:, l], (1, 0, 2)).reshape(4, 1536) for l in range(depth)]
    dw_w_full = [jnp.transpose(g_dw[:, l], (1, 0, 2)).reshape(CONF_KERNEL, 512) for l in range(depth)]

    def layer_params(l):
        return [conv_w_full[l], ssd_conv_b[l].reshape(1, -1), _pad_lanes(ssd_dt_bias[l]), _pad_lanes(ssd_a_log[l]),
                _pad_lanes(ssd_d[l]), ssd_norm_w[l].reshape(1, -1), _pad_lanes(attn_sinks[l]), dw_w_full[l],
                conf_dw_b[l].reshape(1, -1), conf_ln_w[l].reshape(1, -1), conf_ln_b[l].reshape(1, -1)]

    xs = [x.reshape(t, D_MODEL)]
    saved = []
    for l in range(depth):
        (proj, h), gathered = _inproj_fwd(xs[l], norm_w[l].reshape(1, -1), w_in_full[l], f"inproj_fwd_{l}",
                                          (GATHER, [w_out_bf[l]]))
        w_out_full.append(gathered[0].reshape(MIX_WIDTH, D_MODEL))
        last = l + 1 == depth
        side = None if last else (GATHER, [w_in_bf[l + 1]])
        head = (final_norm_w.reshape(1, -1), loss_target.reshape(t, D_MODEL)) if last else None
        res, gathered = _mixer_fwd(xs[l], proj, w_out_full[l], layer_params(l), nb, f"mixer_fwd_{l}", side, head)
        if side:
            w_in_full.append(_repack_w_in(gathered[0], f"repack_w_in_{l + 1}"))
        saved.append((proj, h, res[1:3]))
        xs.append(res[0])
    dx, loss_part, g_fnw = res[0], res[3], res[4]

    small_rows, shard_rows = [None] * depth, [None] * depth
    received = [None] * depth
    for l in reversed(range(depth)):
        proj, h, kept = saved[l]
        res, _ = _mixer_bwd(dx, proj, *kept, w_out_full[l], layer_params(l), nb, f"mixer_bwd_{l}")
        dproj, y_cat = res[0], res[1]
        g_cw_l, g_cb, g_dtb, g_alog, g_dsk, g_nw, g_snk, g_dww, g_dwb, g_lnw, g_lnb = res[2:]
        gw_out_parts = _gw_out(y_cat, dx, f"gw_out_{l}")
        (gw_in,), got_out = _inproj_bwd_w(h, dproj, f"inproj_bwd_w_{l}", (SCATTER, [gw_out_parts]))
        chip_parts = _pair_sum(_unpack_gw_in(gw_in, f"unpack_gw_in_{l}"), f"pair_sum_{l}")
        (dx, g_norm), got_in = _inproj_bwd_x(dproj, w_in_full[l], xs[l], norm_w[l].reshape(1, -1), dx,
                                             f"inproj_bwd_x_{l}", (CHIP_SCATTER, [chip_parts]))
        received[l] = [got_in[0], got_out[0]]
        small_rows[l] = [g_norm, g_cb, g_dtb, g_alog, g_dsk, g_nw, g_snk, g_dwb, g_lnw, g_lnb]
        shard_rows[l] = [jnp.transpose(g.reshape(g.shape[0], N_DEV, -1), (1, 0, 2)).reshape(N_DEV, -1)
                         for g in (g_cw_l, g_dww)]
    grad_x = dx.reshape(nb, seq, D_MODEL)

    small = jnp.concatenate([piece for l in range(depth) for piece in small_rows[l]]
                            + [g_fnw, _pad_lanes(loss_part)], axis=1)
    shard_small = jnp.concatenate([piece for l in range(depth) for piece in shard_rows[l]], axis=1)
    ssum, shard_sum = _exchange_small(small, shard_small.reshape(N_DEV, 1, -1), "exchange_small")
    loss = ssum[0, small.shape[1] - 128]

    g_w_in, d_w_in, nm_w_in, nv_w_in = _reduce_adamw_cols([received[l][0] for l in range(depth)], w_in, m_w_in,
                                                          v_w_in, "adamw_w_in")
    g_w_out, d_w_out, nm_w_out, nv_w_out = _reduce_adamw([received[l][1] for l in range(depth)], w_out, m_w_out,
                                                         v_w_out, 256, "adamw_w_out")

    per_layer = sum(n for _, n in SMALL_FIELDS)
    given = {"norm_w": (norm_w, m_norm_w, v_norm_w), "conv_b": (ssd_conv_b, m_ssd_conv_b, v_ssd_conv_b),
             "dt_bias": (ssd_dt_bias, m_ssd_dt_bias, v_ssd_dt_bias), "a_log": (ssd_a_log, m_ssd_a_log, v_ssd_a_log),
             "d_skip": (ssd_d, m_ssd_d, v_ssd_d), "ssd_norm_w": (ssd_norm_w, m_ssd_norm_w, v_ssd_norm_w),
             "sinks": (attn_sinks, m_attn_sinks, v_attn_sinks), "dw_b": (conf_dw_b, m_conf_dw_b, v_conf_dw_b),
             "ln_w": (conf_ln_w, m_conf_ln_w, v_conf_ln_w), "ln_b": (conf_ln_b, m_conf_ln_b, v_conf_ln_b)}
    entries = []
    off = 0
    for fname, n in SMALL_FIELDS:
        entries.append((*given[fname], [l * per_layer + off for l in range(depth)]))
        off += n
    n_cw, n_dw = ssd_conv_w[0].size, conf_dw_w[0].size
    per_layer_shard = n_cw + n_dw
    cw_grad = jnp.stack([shard_sum[0, l * per_layer_shard:l * per_layer_shard + n_cw].reshape(ssd_conv_w.shape[1:])
                         for l in range(depth)], axis=0)
    dw_grad = jnp.stack([shard_sum[0, l * per_layer_shard + n_cw:(l + 1) * per_layer_shard]
                         .reshape(conf_dw_w.shape[1:]) for l in range(depth)], axis=0)
    entries.append((ssd_conv_w, m_ssd_conv_w, v_ssd_conv_w, cw_grad))
    entries.append((conf_dw_w, m_conf_dw_w, v_conf_dw_w, dw_grad))
    entries.append((final_norm_w.reshape(1, -1), m_final_norm_w.reshape(1, -1), v_final_norm_w.reshape(1, -1),
                    [depth * per_layer]))
    sm = _adamw_small(ssum, entries, "adamw_small")
    sm = {k: quad for k, quad in zip([f for f, _ in SMALL_FIELDS] + ["conv_w", "dw_w", "final"], sm)}

    def outputs(i, big_in_i, big_out_i):
        return [sm["norm_w"][i], big_in_i, sm["conv_w"][i], sm["conv_b"][i], sm["dt_bias"][i], sm["a_log"][i],
                sm["d_skip"][i], sm["ssd_norm_w"][i], sm["sinks"][i], sm["dw_w"][i], sm["dw_b"][i], sm["ln_w"][i],
                sm["ln_b"][i], big_out_i, sm["final"][i].reshape(-1)]

    return (loss, grad_x, *outputs(0, g_w_in, g_w_out), *outputs(1, d_w_in, d_w_out),
            *outputs(2, nm_w_in, nm_w_out), *outputs(3, nv_w_in, nv_w_out))
```

```python
import functools

import jax
import jax.numpy as jnp
from jax import lax
from jax.experimental import pallas as pl
from jax.experimental.pallas import tpu as pltpu

F32 = jnp.float32
BF16 = jnp.bfloat16
N_DEV = 8
EPS = 1e-5

D_MODEL = 1024
CHUNK = 128
SSD_HEADS = 16
SSD_HEAD_DIM = 64
SSD_STATE = 128
ATTN_HEADS = 8
ATTN_HEAD_DIM = 64
CONF_KERNEL = 31
MIX_WIDTH = 2048
D_IN_PROJ = 5392
C_Z = 0
C_CA = 2048
C_XBC = 3072
C_Q = 4608
C_K = 5120
C_DT = 5376
PROJ_W = 5632
N_COL_TILES = 4
COL_TILE = PROJ_W // N_COL_TILES
XBC_HALO = 8
CONF_HALO = 32
ATTN_STACK_FWD = 2
ATTN_STACK_BWD = 4
VMEM_LIMIT = 56 * 1024 * 1024

ADAM_LR = 0.001
ADAM_B1 = 0.9
ADAM_B2 = 0.999
ADAM_EPS = 1e-08
ADAM_WD = 0.01
ADAM_STEP = 10


def _silu(v):
    return v * jax.nn.sigmoid(v)


def _softplus(v):
    return jnp.maximum(v, 0.0) + jnp.log1p(jnp.exp(-jnp.abs(v)))


def _rmsnorm(v, w):
    return v * lax.rsqrt(jnp.mean(v * v, axis=-1, keepdims=True) + EPS) * w


def _dot(a, b):
    return jnp.dot(a.astype(BF16), b.astype(BF16), preferred_element_type=F32)


def _dot_nt(a, b):
    return lax.dot_general(a.astype(BF16), b.astype(BF16), (((1,), (1,)), ((), ())), preferred_element_type=F32)


def _dot_tn(a, b):
    return lax.dot_general(a.astype(BF16), b.astype(BF16), (((0,), (0,)), ((), ())), preferred_element_type=F32)


def _taps(ext, offs, out_len, w=None, g=None):
    n_rows, n_cols = ext.shape
    by_shift = {}
    for t, off in enumerate(offs):
        by_shift.setdefault(off % 8, []).append((t, off))
    for r, taps in by_shift.items():
        assert max(off for _, off in taps) - r + out_len <= n_rows - r
    accs = []
    sums = [[None] * (n_cols // 128) for _ in offs]
    for blk in range(n_cols // 128):
        cs = slice(blk * 128, (blk + 1) * 128)
        e = ext[:, cs]
        acc = None
        for r, taps in by_shift.items():
            shifted = e if r == 0 else pltpu.roll(e, n_rows - r, axis=0)
            for t, off in taps:
                window = shifted[off - r:off - r + out_len, :]
                if w is not None:
                    term = w[t:t + 1, cs] * window
                    acc = term if acc is None else acc + term
                if g is not None:
                    sums[t][blk] = jnp.sum(g[:, cs] * window, axis=0, keepdims=True)
        accs.append(acc)
    if w is not None:
        return jnp.concatenate(accs, axis=1)
    return jnp.concatenate([jnp.concatenate(row, axis=1) for row in sums], axis=0)


@functools.partial(jax.custom_vjp, nondiff_argnums=(3,))
def _dwconv(ext, w, b, halo):
    kk = w.shape[0]
    return b + _taps(ext, [halo - (kk - 1) + t for t in range(kk)], ext.shape[0] - halo, w=w)


def _dwconv_fwd(ext, w, b, halo):
    return _dwconv(ext, w, b, halo), (ext, w)


def _dwconv_bwd(halo, res, g):
    ext, w = res
    kk = w.shape[0]
    offs = [halo - (kk - 1) + t for t in range(kk)]
    dw = _taps(ext, offs, ext.shape[0] - halo, g=g)
    zeros = jnp.zeros((halo, g.shape[1]), g.dtype)
    gp = jnp.concatenate([zeros, g, zeros], axis=0)
    dext = _taps(gp, [halo - off for off in offs], ext.shape[0], w=w)
    return dext, dw, jnp.sum(g, axis=0, keepdims=True)


_dwconv.defvjp(_dwconv_fwd, _dwconv_bwd)


def _ssd_part(z_ssd, conv_out, dtr, s_in, dtb, alog, dsk, nw):
    qn = conv_out.shape[0]
    nh = SSD_HEADS
    per_group = nh // 2
    n_pair = nh // 2
    xa = _silu(conv_out)
    xs = xa[:, 0:1024]
    dt = _softplus(dtr + dtb)
    a = dt * (-jnp.exp(alog))
    rows = lax.broadcasted_iota(jnp.int32, (qn, qn), 0)
    cols = lax.broadcasted_iota(jnp.int32, (qn, qn), 1)
    causal = rows >= cols
    low = cols < SSD_HEAD_DIM
    a_cs = jnp.dot(causal.astype(F32), a, precision=lax.Precision.HIGHEST, preferred_element_type=F32)
    a_cs_t = a_cs.T
    bgs = [xa[:, 1024 + g * 128:1024 + (g + 1) * 128] for g in range(2)]
    cgs = [xa[:, 1280 + g * 128:1280 + (g + 1) * 128] for g in range(2)]
    cbms = [_dot_nt(cgs[g], bgs[g]) for g in range(2)]
    y_diag, ecol, xw = [], [], []
    for p in range(n_pair):
        pair = (2 * p, 2 * p + 1)
        colb = [jnp.broadcast_to(a_cs[:, h:h + 1], (qn, qn)) for h in pair]
        dtb_wide = [jnp.broadcast_to(dt[:, h:h + 1], (qn, qn)) for h in pair]
        ms = [cbms[h // per_group] * jnp.exp(jnp.where(causal, colb[i] - a_cs_t[h:h + 1, :], -jnp.inf))
              for i, h in enumerate(pair)]
        xdt = xs[:, p * 128:(p + 1) * 128] * jnp.where(low, dtb_wide[0], dtb_wide[1])
        y_diag.append(_dot(ms[0], jnp.where(low, xdt, 0.0)) + _dot(ms[1], jnp.where(low, 0.0, xdt)))
        col_pair = jnp.where(low, colb[0], colb[1])
        last_pair = jnp.broadcast_to(col_pair[qn - 1:qn, :], (qn, qn))
        ecol.append(jnp.exp(col_pair))
        xw.append(xdt * jnp.exp(last_pair - col_pair))
    y_off, st = [], []
    for g in range(2):
        ps = range(g * n_pair // 2, (g + 1) * n_pair // 2)
        y_off.append(_dot_nt(cgs[g], s_in[g * 512:(g + 1) * 512, :]) * jnp.concatenate([ecol[p] for p in ps], axis=1))
        st.append(_dot_tn(jnp.concatenate([xw[p] for p in ps], axis=1), bgs[g]))
    e_last = jnp.exp(jnp.broadcast_to(a_cs_t[:, qn - 1:qn], (qn, SSD_STATE)))
    scale = jnp.concatenate([jnp.broadcast_to(e_last[h:h + 1, :], (64, SSD_STATE)) for h in range(nh)], axis=0)
    s_out = scale * s_in + jnp.concatenate(st, axis=0)
    d_wide = jnp.concatenate([jnp.broadcast_to(dsk[:, h:h + 1], (1, 64)) for h in range(nh)], axis=1)
    y = jnp.concatenate(y_diag, axis=1) + jnp.concatenate(y_off, axis=1) + d_wide * xs
    gated = y * _silu(z_ssd)
    halves = []
    for g in range(2):
        gg = gated[:, g * 512:(g + 1) * 512]
        halves.append(gg * lax.rsqrt(jnp.mean(gg * gg, axis=-1, keepdims=True) + EPS))
    return jnp.concatenate(halves, axis=1) * nw, s_out


def _attn_part(z_attn, q, kv, p_kv, snk, kv_bias, stack):
    qn = q.shape[0]
    kk = jnp.concatenate([p_kv[:, 0:128], kv[:, 0:128]], axis=0)
    vv = jnp.concatenate([p_kv[:, 128:256], kv[:, 128:256]], axis=0)
    units = range(ATTN_HEADS // stack)
    heads = [range(u * stack, (u + 1) * stack) for u in units]
    kv_of = [u * stack // (ATTN_HEADS // 2) for u in units]
    k_of = [kk[:, g * 64:(g + 1) * 64] for g in kv_of]
    v_of = [vv[:, g * 64:(g + 1) * 64] for g in kv_of]
    qs = [jnp.concatenate([q[:, h * 64:(h + 1) * 64] for h in heads[u]], axis=0) for u in units]
    sk = [jnp.concatenate([jnp.broadcast_to(snk[:, h:h + 1], (qn, 1)) for h in heads[u]], axis=0) for u in units]
    s = [_dot_nt(qs[u], k_of[u]) * (ATTN_HEAD_DIM ** -0.5) + kv_bias for u in units]
    m = [lax.stop_gradient(jnp.maximum(jnp.max(s[u], axis=-1, keepdims=True), sk[u])) for u in units]
    e = [jnp.exp(s[u] - m[u]) for u in units]
    r_den = [1.0 / (jnp.sum(e[u], axis=-1, keepdims=True) + jnp.exp(sk[u] - m[u])) for u in units]
    o = [_dot(e[u], v_of[u]) * r_den[u] for u in units]
    outs = [o[u][i * qn:(i + 1) * qn, :] for u in units for i in range(stack)]
    return jnp.concatenate(outs, axis=1) * _silu(z_attn)


def _conf_glu(cacg, p_cc):
    c0 = cacg[:, 0:512] * jax.nn.sigmoid(cacg[:, 512:1024])
    pc0 = p_cc[:, 0:512] * jax.nn.sigmoid(p_cc[:, 512:1024])
    return jnp.concatenate([pc0, c0], axis=0)


def _conf_tail(conv_out, z_conf, lnw, lnb):
    xc = conv_out - jnp.mean(conv_out, axis=-1, keepdims=True)
    yln = xc * lax.rsqrt(jnp.mean(xc * xc, axis=-1, keepdims=True) + EPS) * lnw + lnb
    return _silu(yln) * _silu(z_conf)


def _kv_bias(bias_scr, first_step, not_first):
    _, rows, cols = bias_scr.shape
    qn = cols // 2

    @pl.when(first_step)
    def _():
        ii = lax.broadcasted_iota(jnp.int32, (rows, cols), 0) & (qn - 1)
        jj = lax.broadcasted_iota(jnp.int32, (rows, cols), 1)
        d = jj - ii
        band = (d >= 1) & (d <= qn)
        bias_scr[0] = jnp.where(band & (jj >= qn), 0.0, -jnp.inf)
        bias_scr[1] = jnp.where(band, 0.0, -jnp.inf)

    return bias_scr[not_first.astype(jnp.int32)]


def _my_place():
    return lax.axis_index("x"), lax.axis_index("y"), lax.axis_index("c")


def _all_gather(arrs, name):
    n = len(arrs)

    def body(*refs):
        ins, outs = refs[:n], refs[n:2 * n]
        send_sems, recv_sems, local_sems = refs[2 * n:]
        x, y, c = _my_place()
        me, sibling = (x, y, c), (x, y, 1 - c)
        chips = [(1 - x, y), (x, 1 - y), (1 - x, 1 - y)]

        def slot(a, p):
            return outs[a].at[4 * p[0] + 2 * p[1] + p[2]]

        def copy(a, kk, block, to, src=None):
            return pltpu.make_async_remote_copy(
                src_ref=slot(a, block) if src is None else src, dst_ref=slot(a, block),
                send_sem=send_sems.at[a, kk], recv_sem=recv_sems.at[a, kk],
                device_id=to, device_id_type=pl.DeviceIdType.MESH)

        mine = [pltpu.make_async_copy(ins[a], slot(a, me), local_sems.at[a]) for a in range(n)]
        for cp in mine:
            cp.start()
        first = []
        for a in range(n):
            first.append(copy(a, 0, me, sibling, src=ins[a]))
            first += [copy(a, 1 + j, me, (*chip, c), src=ins[a]) for j, chip in enumerate(chips)]
        for cp in first:
            cp.start()
        passed = []
        for j, chip in enumerate(chips):
            for a in range(n):
                copy(a, 1 + j, (*chip, c), me).wait_recv()
                fwd = copy(a, 4 + j, (*chip, c), sibling)
                fwd.start()
                passed.append(fwd)
        for a in range(n):
            copy(a, 0, sibling, me).wait_recv()
            for j, chip in enumerate(chips):
                copy(a, 4 + j, (*chip, 1 - c), me).wait_recv()
        for cp in first + passed:
            cp.wait_send()
        for cp in mine:
            cp.wait()

    any_spec = pl.BlockSpec(memory_space=pl.ANY)
    return pl.pallas_call(
        body, name=name,
        out_shape=[jax.ShapeDtypeStruct((N_DEV,) + a.shape, a.dtype) for a in arrs],
        in_specs=[any_spec] * n, out_specs=[any_spec] * n,
        scratch_shapes=[pltpu.SemaphoreType.DMA((n, 7)), pltpu.SemaphoreType.DMA((n, 7)),
                        pltpu.SemaphoreType.DMA((n,))],
    )(*arrs)


GATHER, SCATTER, CHIP_SCATTER = "gather", "scatter", "chip_scatter"


def _direct_copies(mode, ins, outs, send_sems, recv_sems, local_sems):
    x, y, c = _my_place()
    by_chip = mode == CHIP_SCATTER
    place = (lambda px, py, pc: 2 * px + py) if by_chip else (lambda px, py, pc: 4 * px + 2 * py + pc)
    me_idx = place(x, y, c)
    n = len(ins)
    local = [pltpu.make_async_copy(ins[a] if mode == GATHER else ins[a].at[me_idx], outs[a].at[me_idx],
                                   local_sems.at[a]) for a in range(n)]
    remote = []
    for rel in range(1, N_DEV):
        if by_chip and rel & 1:
            continue
        px = 1 - x if rel & 4 else x
        py = 1 - y if rel & 2 else y
        pc = 1 - c if rel & 1 else c
        for a in range(n):
            remote.append(pltpu.make_async_remote_copy(
                src_ref=ins[a] if mode == GATHER else ins[a].at[place(px, py, pc)], dst_ref=outs[a].at[me_idx],
                send_sem=send_sems.at[a, rel - 1], recv_sem=recv_sems.at[a, rel - 1],
                device_id=(px, py, pc), device_id_type=pl.DeviceIdType.MESH))
    return local + remote


def _side_scratch(n):
    return [pltpu.SemaphoreType.DMA((n, 7)), pltpu.SemaphoreType.DMA((n, 7)), pltpu.SemaphoreType.DMA((n,))]


def _side_out_shapes(mode, arrs):
    return [jax.ShapeDtypeStruct((N_DEV,) + a.shape if mode == GATHER else a.shape, a.dtype) for a in arrs]


def _pallas_with_side(body, side, first, last, n_in, n_out, *, in_specs, out_specs, out_shape, scratch_shapes=(),
                      args, **kwargs):
    side_arrs = [] if side is None else list(side[1])
    ns = len(side_arrs)

    def wrapped(*refs):
        own_in, side_in = refs[:n_in], refs[n_in:n_in + ns]
        o = n_in + ns
        own_out, side_out = refs[o:o + n_out], refs[o + n_out:o + n_out + ns]
        scratch = refs[o + n_out + ns:]
        own_scratch, sems = (scratch[:-3], scratch[-3:]) if ns else (scratch, ())
        if ns:
            @pl.when(first())
            def _():
                for cp in _direct_copies(side[0], side_in, side_out, *sems):
                    cp.start()

        body(*own_in, *own_out, *own_scratch)
        if ns:
            @pl.when(last())
            def _():
                for cp in _direct_copies(side[0], side_in, side_out, *sems):
                    cp.wait()

    any_spec = pl.BlockSpec(memory_space=pl.ANY)
    res = pl.pallas_call(
        wrapped,
        in_specs=list(in_specs) + [any_spec] * ns,
        out_specs=list(out_specs) + [any_spec] * ns,
        out_shape=list(out_shape) + (_side_out_shapes(side[0], side_arrs) if ns else []),
        scratch_shapes=list(scratch_shapes) + (_side_scratch(ns) if ns else []),
        **kwargs,
    )(*args, *side_arrs)
    return res[:n_out], res[n_out:]


def _exchange_small(everyone, per_device, name):
    n_all, n_own = everyone.shape[1], per_device.shape[2]

    def body(all_ref, own_ref, all_sum, own_sum, all_slots, own_slots, *sems):
        copies = _direct_copies(GATHER, [all_ref], [all_slots], *sems[:3])
        copies += _direct_copies(SCATTER, [own_ref], [own_slots], *sems[3:])
        for cp in copies:
            cp.start()
        for cp in copies:
            cp.wait()
        for slots, out in ((all_slots, all_sum), (own_slots, own_sum)):
            total = slots[0]
            for i in range(1, N_DEV):
                total = total + slots[i]
            out[...] = total

    vmem_spec = pl.BlockSpec(memory_space=pltpu.VMEM)
    return pl.pallas_call(
        body, name=name,
        out_shape=[jax.ShapeDtypeStruct((1, n_all), F32), jax.ShapeDtypeStruct((1, n_own), F32)],
        in_specs=[vmem_spec, vmem_spec], out_specs=[vmem_spec, vmem_spec],
        scratch_shapes=[pltpu.VMEM((N_DEV, 1, n_all), F32), pltpu.VMEM((N_DEV, 1, n_own), F32)]
                       + _side_scratch(1) + _side_scratch(1),
    )(everyone, per_device)


def _full(shape):
    return pl.BlockSpec(shape, lambda *_: (0,) * len(shape))


def _inproj_fwd(x, nw, w, name, side=None):
    t = x.shape[0]
    tm = 256

    def body(x_ref, nw_ref, w_ref, proj_ref, h_ref):
        h = _rmsnorm(x_ref[...], nw_ref[...]).astype(BF16)
        h_ref[...] = h
        for j in range(N_COL_TILES):
            sl = slice(j * COL_TILE, (j + 1) * COL_TILE)
            proj_ref[:, sl] = jnp.dot(h, w_ref[:, sl], preferred_element_type=F32)

    grid = (t // tm,)
    return _pallas_with_side(
        body, side, *_grid_ends(grid), 3, 2, name=name, grid=grid,
        out_shape=[jax.ShapeDtypeStruct((t, PROJ_W), F32), jax.ShapeDtypeStruct((t, D_MODEL), BF16)],
        in_specs=[pl.BlockSpec((tm, D_MODEL), lambda i: (i, 0)), _full((1, D_MODEL)), _full((D_MODEL, PROJ_W))],
        out_specs=[pl.BlockSpec((tm, PROJ_W), lambda i: (i, 0)), pl.BlockSpec((tm, D_MODEL), lambda i: (i, 0))],
        compiler_params=pltpu.CompilerParams(dimension_semantics=("arbitrary",), vmem_limit_bytes=VMEM_LIMIT),
        args=(x, nw, w))


def _param_specs():
    return [_full((4, 1536)), _full((1, 1536)), _full((1, 128)), _full((1, 128)), _full((1, 128)),
            _full((1, 1024)), _full((1, 128)), _full((CONF_KERNEL, 512)), _full((1, 512)), _full((1, 512)),
            _full((1, 512))]


def _halo_specs(nc, chunk_of):
    def prev_chunk(b, j):
        return jnp.maximum(b * nc + chunk_of(j) - 1, 0)

    per_xbc = CHUNK // XBC_HALO
    per_cc = CHUNK // CONF_HALO
    return [
        pl.BlockSpec((XBC_HALO, 1536), lambda b, j: (prev_chunk(b, j) * per_xbc + per_xbc - 1, C_XBC // 1536)),
        pl.BlockSpec((CHUNK, 256), lambda b, j: (prev_chunk(b, j), C_K // 256)),
        pl.BlockSpec((CONF_HALO, 1024), lambda b, j: (prev_chunk(b, j) * per_cc + per_cc - 1, C_CA // 1024)),
    ]


def _grid_ends(grid):
    first = lambda: functools.reduce(lambda p, q: p & q, [pl.program_id(i) == 0 for i in range(len(grid))])
    last = lambda: functools.reduce(lambda p, q: p & q, [pl.program_id(i) == n - 1 for i, n in enumerate(grid)])
    return first, last


def _mixer_fwd(x, proj, w_out, params, nb, name, side=None, head=None):
    t = x.shape[0]
    nc = t // nb // CHUNK
    n_head = 0 if head is None else 2

    def body(x_ref, cur_ref, pxbc_ref, pkv_ref, pcc_ref, wo_ref, *rest):
        prm = [r[...] for r in rest[:11]]
        head_in = rest[11:11 + n_head]
        xn_ref, sall_ref, conv_ref = rest[11 + n_head:14 + n_head]
        head_out = rest[14 + n_head:14 + 2 * n_head]
        s_scr, bias_scr = rest[14 + 2 * n_head:16 + 2 * n_head]
        c = pl.program_id(1)
        not_first = c > 0
        nf = not_first.astype(F32)
        kv_bias = _kv_bias(bias_scr, (pl.program_id(0) == 0) & (c == 0), not_first)

        @pl.when(c == 0)
        def _():
            s_scr[...] = jnp.zeros_like(s_scr)

        cw, cb, dtb, alog, dsk, nw, snk, dww, dwb, lnw, lnb = prm
        s_in = s_scr[...]
        sall_ref[0] = s_in
        ssd_conv = _dwconv(jnp.concatenate([pxbc_ref[...] * nf, cur_ref[:, C_XBC:C_XBC + 1536]], axis=0), cw, cb,
                           XBC_HALO)
        y_ssd, s_out = _ssd_part(cur_ref[:, 0:1024], ssd_conv, cur_ref[:, C_DT:C_DT + 128], s_in, dtb, alog, dsk, nw)
        s_scr[...] = s_out
        y_attn = _attn_part(cur_ref[:, 1024:1536], cur_ref[:, C_Q:C_Q + 512], cur_ref[:, C_K:C_K + 256],
                            pkv_ref[...] * nf, snk, kv_bias, ATTN_STACK_FWD)
        conv_out = _dwconv(_conf_glu(cur_ref[:, C_CA:C_CA + 1024], pcc_ref[...] * nf), dww, dwb, CONF_HALO)
        conv_ref[...] = conv_out
        y_conf = _conf_tail(conv_out, cur_ref[:, 1536:2048], lnw, lnb)
        x_next = (x_ref[...] + _dot(y_ssd, wo_ref[0:1024, :]) + _dot(y_attn, wo_ref[1024:1536, :])
                  + _dot(y_conf, wo_ref[1536:2048, :]))
        if head is None:
            xn_ref[...] = x_next
        else:
            fnw_ref, tgt_ref = head_in
            loss_ref, gfnw_ref = head_out

            @pl.when((pl.program_id(0) == 0) & (c == 0))
            def _():
                loss_ref[...] = jnp.zeros_like(loss_ref)
                gfnw_ref[...] = jnp.zeros_like(gfnw_ref)

            y, vjp = jax.vjp(_rmsnorm, x_next, fnw_ref[...])
            err = y - tgt_ref[...]
            loss_ref[...] += 0.5 * jnp.sum(jnp.mean(err * err, axis=-1, keepdims=True), axis=0, keepdims=True)
            xn_ref[...], d_fnw = vjp(err * (1.0 / D_MODEL))
            gfnw_ref[...] += d_fnw

    row = lambda b, j: (b * nc + j, 0)
    grid = (nb, nc)
    return _pallas_with_side(
        body, side, *_grid_ends(grid), 17 + n_head, 3 + n_head, name=name, grid=grid,
        out_shape=[jax.ShapeDtypeStruct((t, D_MODEL), F32),
                   jax.ShapeDtypeStruct((nb * nc, SSD_HEADS * SSD_HEAD_DIM, SSD_STATE), F32),
                   jax.ShapeDtypeStruct((t, 512), F32)]
                  + [jax.ShapeDtypeStruct(s, F32) for s in ((1, 1), (1, D_MODEL))[:n_head]],
        in_specs=[pl.BlockSpec((CHUNK, D_MODEL), row), pl.BlockSpec((CHUNK, PROJ_W), row)]
                 + _halo_specs(nc, lambda j: j) + [_full((MIX_WIDTH, D_MODEL))] + _param_specs()
                 + [_full((1, D_MODEL)), pl.BlockSpec((CHUNK, D_MODEL), row)][:n_head],
        out_specs=[pl.BlockSpec((CHUNK, D_MODEL), row),
                   pl.BlockSpec((1, SSD_HEADS * SSD_HEAD_DIM, SSD_STATE), lambda b, j: (b * nc + j, 0, 0)),
                   pl.BlockSpec((CHUNK, 512), row)] + [_full((1, 1)), _full((1, D_MODEL))][:n_head],
        scratch_shapes=[pltpu.VMEM((SSD_HEADS * SSD_HEAD_DIM, SSD_STATE), F32),
                        pltpu.VMEM((2, ATTN_STACK_FWD * CHUNK, 2 * CHUNK), F32)],
        compiler_params=pltpu.CompilerParams(dimension_semantics=("arbitrary", "arbitrary"),
                                             vmem_limit_bytes=VMEM_LIMIT),
        args=(x, proj, proj, proj, proj, w_out, *params, *(head or ())))


def _mixer_bwd(dxn, proj, s_all, conv_all, w_out, params, nb, name):
    t = dxn.shape[0]
    nc = t // nb // CHUNK
    n_prm = 11

    def body(dxn_ref, cur_ref, pxbc_ref, pkv_ref, pcc_ref, s_ref, conv_ref, wo_ref, *rest):
        prm = [r[...] for r in rest[:n_prm]]
        dproj_ref, ycat_ref = rest[n_prm:n_prm + 2]
        gprm = rest[n_prm + 2:2 * n_prm + 2]
        ds_scr, pend_xbc, pend_kv, pend_cc, bias_scr = rest[2 * n_prm + 2:]
        b, j = pl.program_id(0), pl.program_id(1)
        c = nc - 1 - j
        not_first = c > 0
        nf = not_first.astype(F32)
        kv_bias = _kv_bias(bias_scr, (b == 0) & (j == 0), not_first)

        @pl.when((b == 0) & (j == 0))
        def _():
            for r in gprm:
                r[...] = jnp.zeros_like(r)

        @pl.when(j == 0)
        def _():
            ds_scr[...] = jnp.zeros_like(ds_scr)
            pend_xbc[...] = jnp.zeros_like(pend_xbc)
            pend_kv[...] = jnp.zeros_like(pend_kv)
            pend_cc[...] = jnp.zeros_like(pend_cc)

        cw, cb, dtb, alog, dsk, nw, snk, dww, dwb, lnw, lnb = prm
        g_cw, g_cb, g_dtb, g_alog, g_dsk, g_nw, g_snk, g_dww, g_dwb, g_lnw, g_lnb = gprm
        dxn_v = dxn_ref[...]

        def add_tail(d_cur, pending):
            lead = jnp.zeros((CHUNK - pending.shape[0], pending.shape[1]), F32)
            return d_cur + jnp.concatenate([lead, pending], axis=0)

        y, vjp = jax.vjp(_conf_tail, conv_ref[...], cur_ref[:, 1536:2048], lnw, lnb)
        ycat_ref[:, 1536:2048] = y.astype(BF16)
        d_conv, dz, d_lnw, d_lnb = vjp(_dot_nt(dxn_v, wo_ref[1536:2048, :]))
        ext, vjp = jax.vjp(_conf_glu, cur_ref[:, C_CA:C_CA + 1024], pcc_ref[...] * nf)
        d_ext, d_dww, d_dwb = _dwconv_bwd(CONF_HALO, (ext, dww), d_conv)
        dcacg, dpcc = vjp(d_ext)
        dproj_ref[:, 1536:2048] = dz.astype(BF16)
        dproj_ref[:, C_CA:C_CA + 1024] = add_tail(dcacg, pend_cc[...]).astype(BF16)
        pend_cc[...] = dpcc
        for r, g in ((g_dww, d_dww), (g_dwb, d_dwb), (g_lnw, d_lnw), (g_lnb, d_lnb)):
            r[...] += g

        attn = functools.partial(_attn_part, kv_bias=kv_bias, stack=ATTN_STACK_BWD)
        y, vjp = jax.vjp(attn, cur_ref[:, 1024:1536], cur_ref[:, C_Q:C_Q + 512], cur_ref[:, C_K:C_K + 256],
                         pkv_ref[...] * nf, snk)
        ycat_ref[:, 1024:1536] = y.astype(BF16)
        dz, dq, dkv, dpkv, d_snk = vjp(_dot_nt(dxn_v, wo_ref[1024:1536, :]))
        dproj_ref[:, 1024:1536] = dz.astype(BF16)
        dproj_ref[:, C_Q:C_Q + 512] = dq.astype(BF16)
        dproj_ref[:, C_K:C_K + 256] = (dkv + pend_kv[...]).astype(BF16)
        pend_kv[...] = dpkv
        g_snk[...] += d_snk

        ext = jnp.concatenate([pxbc_ref[...] * nf, cur_ref[:, C_XBC:C_XBC + 1536]], axis=0)
        (y, _), vjp = jax.vjp(_ssd_part, cur_ref[:, 0:1024], _dwconv(ext, cw, cb, XBC_HALO),
                              cur_ref[:, C_DT:C_DT + 128], s_ref[0], dtb, alog, dsk, nw)
        ycat_ref[:, 0:1024] = y.astype(BF16)
        dz, d_conv, ddtr, ds_in, d_dtb, d_alog, d_dsk, d_nw = vjp((_dot_nt(dxn_v, wo_ref[0:1024, :]), ds_scr[...]))
        d_ext, d_cw, d_cb = _dwconv_bwd(XBC_HALO, (ext, cw), d_conv)
        dpxbc, dxbc = d_ext[0:XBC_HALO, :], d_ext[XBC_HALO:, :]
        dproj_ref[:, 0:1024] = dz.astype(BF16)
        dproj_ref[:, C_XBC:C_XBC + 1536] = add_tail(dxbc, pend_xbc[...]).astype(BF16)
        dproj_ref[:, C_DT:C_DT + 128] = ddtr.astype(BF16)
        dproj_ref[:, C_DT + 128:PROJ_W] = jnp.zeros((CHUNK, PROJ_W - C_DT - 128), BF16)
        pend_xbc[...] = dpxbc
        ds_scr[...] = ds_in
        for r, g in ((g_cw, d_cw), (g_cb, d_cb), (g_dtb, d_dtb), (g_alog, d_alog), (g_dsk, d_dsk), (g_nw, d_nw)):
            r[...] += g

    row = lambda b, j: (b * nc + nc - 1 - j, 0)
    prm_shapes = [(4, 1536), (1, 1536), (1, 128), (1, 128), (1, 128), (1, 1024), (1, 128), (CONF_KERNEL, 512),
                  (1, 512), (1, 512), (1, 512)]
    grid = (nb, nc)
    return _pallas_with_side(
        body, None, *_grid_ends(grid), 8 + n_prm, 2 + n_prm, name=name, grid=grid,
        out_shape=[jax.ShapeDtypeStruct((t, PROJ_W), BF16), jax.ShapeDtypeStruct((t, MIX_WIDTH), BF16)]
                  + [jax.ShapeDtypeStruct(s, F32) for s in prm_shapes],
        in_specs=[pl.BlockSpec((CHUNK, D_MODEL), row), pl.BlockSpec((CHUNK, PROJ_W), row)]
                 + _halo_specs(nc, lambda j: nc - 1 - j)
                 + [pl.BlockSpec((1, SSD_HEADS * SSD_HEAD_DIM, SSD_STATE), lambda b, j: (b * nc + nc - 1 - j, 0, 0)),
                    pl.BlockSpec((CHUNK, 512), row), _full((MIX_WIDTH, D_MODEL))] + _param_specs(),
        out_specs=[pl.BlockSpec((CHUNK, PROJ_W), row), pl.BlockSpec((CHUNK, MIX_WIDTH), row)]
                  + [_full(s) for s in prm_shapes],
        scratch_shapes=[pltpu.VMEM((SSD_HEADS * SSD_HEAD_DIM, SSD_STATE), F32), pltpu.VMEM((XBC_HALO, 1536), F32),
                        pltpu.VMEM((CHUNK, 256), F32), pltpu.VMEM((CONF_HALO, 1024), F32),
                        pltpu.VMEM((2, ATTN_STACK_BWD * CHUNK, 2 * CHUNK), F32)],
        compiler_params=pltpu.CompilerParams(dimension_semantics=("arbitrary", "arbitrary"),
                                             vmem_limit_bytes=VMEM_LIMIT),
        args=(dxn, proj, proj, proj, proj, s_all, conv_all, w_out, *params))


def _gw_out(y_cat, dxn, name):
    t = y_cat.shape[0]
    tk = min(t, 1024)

    def body(y_ref, dxn_ref, out_ref, acc):
        k = pl.program_id(0)

        @pl.when(k == 0)
        def _():
            acc[...] = jnp.zeros_like(acc)

        acc[...] += _dot_tn(y_ref[...], dxn_ref[...])

        @pl.when(k == t // tk - 1)
        def _():
            out_ref[...] = acc[...].astype(BF16)

    out = pl.pallas_call(
        body, name=name, grid=(t // tk,),
        out_shape=jax.ShapeDtypeStruct((MIX_WIDTH, D_MODEL), BF16),
        in_specs=[pl.BlockSpec((tk, MIX_WIDTH), lambda k: (k, 0)), pl.BlockSpec((tk, D_MODEL), lambda k: (k, 0))],
        out_specs=_full((MIX_WIDTH, D_MODEL)),
        scratch_shapes=[pltpu.VMEM((MIX_WIDTH, D_MODEL), F32)],
        compiler_params=pltpu.CompilerParams(dimension_semantics=("arbitrary",), vmem_limit_bytes=VMEM_LIMIT),
    )(y_cat, dxn)
    return out.reshape(N_DEV, MIX_WIDTH // N_DEV, D_MODEL)


def _inproj_bwd_x(dproj, w, x, nw, dxn, name, side=None):
    t = x.shape[0]
    tm = 256

    def body(dp_ref, w_ref, x_ref, nw_ref, dxn_ref, dx_ref, gnw_ref):
        @pl.when(pl.program_id(0) == 0)
        def _():
            gnw_ref[...] = jnp.zeros_like(gnw_ref)

        dh = jnp.zeros((tm, D_MODEL), F32)
        for j in range(N_COL_TILES):
            sl = slice(j * COL_TILE, (j + 1) * COL_TILE)
            dh = dh + _dot_nt(dp_ref[:, sl], w_ref[:, sl])
        _, vjp = jax.vjp(_rmsnorm, x_ref[...], nw_ref[...])
        dx, dnw = vjp(dh)
        dx_ref[...] = dxn_ref[...] + dx
        gnw_ref[...] += dnw

    tok = lambda i: (i, 0)
    grid = (t // tm,)
    return _pallas_with_side(
        body, side, *_grid_ends(grid), 5, 2, name=name, grid=grid,
        out_shape=[jax.ShapeDtypeStruct((t, D_MODEL), F32), jax.ShapeDtypeStruct((1, D_MODEL), F32)],
        in_specs=[pl.BlockSpec((tm, PROJ_W), tok), _full((D_MODEL, PROJ_W)), pl.BlockSpec((tm, D_MODEL), tok),
                  _full((1, D_MODEL)), pl.BlockSpec((tm, D_MODEL), tok)],
        out_specs=[pl.BlockSpec((tm, D_MODEL), tok), _full((1, D_MODEL))],
        compiler_params=pltpu.CompilerParams(dimension_semantics=("arbitrary",), vmem_limit_bytes=VMEM_LIMIT),
        args=(dproj, w, x, nw, dxn))


def _inproj_bwd_w(h, dproj, name, side=None):
    t = h.shape[0]
    tk = min(t, 1024)

    def body(h_ref, dp_ref, gw_ref):
        @pl.when(pl.program_id(1) == 0)
        def _():
            gw_ref[...] = jnp.zeros_like(gw_ref)

        gw_ref[...] += _dot_tn(h_ref[...], dp_ref[...])

    grid = (N_COL_TILES, t // tk)
    return _pallas_with_side(
        body, side, *_grid_ends(grid), 2, 1, name=name, grid=grid,
        out_shape=[jax.ShapeDtypeStruct((D_MODEL, PROJ_W), F32)],
        in_specs=[pl.BlockSpec((tk, D_MODEL), lambda n, k: (k, 0)), pl.BlockSpec((tk, COL_TILE), lambda n, k: (k, n))],
        out_specs=[pl.BlockSpec((D_MODEL, COL_TILE), lambda n, k: (0, n))],
        compiler_params=pltpu.CompilerParams(dimension_semantics=("arbitrary", "arbitrary"),
                                             vmem_limit_bytes=VMEM_LIMIT),
        args=(h, dproj))


def _repack_runs():
    pieces = ((0, 2048, C_Z), (2048, 3584, C_XBC), (3584, 3600, C_DT), (3600, 4368, C_Q), (4368, D_IN_PROJ, C_CA))
    per = D_IN_PROJ // N_DEV
    runs = []
    for j in range(N_DEV):
        lo, hi = per * j, per * (j + 1)
        for a, b, dst in pieces:
            s, e = max(lo, a), min(hi, b)
            if s < e:
                runs.append((j, s - lo, e - lo, dst + s - a))
    return runs


def _repack_w_in(g, name):
    tr = 256

    def body(g_ref, o_ref):
        for j, a, b, dst in _repack_runs():
            o_ref[:, dst:dst + b - a] = g_ref[j, :, a:b]
        o_ref[:, C_DT + 16:PROJ_W] = jnp.zeros((tr, PROJ_W - C_DT - 16), g.dtype)

    return pl.pallas_call(
        body, name=name, grid=(D_MODEL // tr,),
        out_shape=jax.ShapeDtypeStruct((D_MODEL, PROJ_W), g.dtype),
        in_specs=[pl.BlockSpec((N_DEV, tr, D_IN_PROJ // N_DEV), lambda i: (0, i, 0))],
        out_specs=pl.BlockSpec((tr, PROJ_W), lambda i: (i, 0)),
        compiler_params=pltpu.CompilerParams(dimension_semantics=("arbitrary",)),
    )(g)


def _unpack_gw_in(g, name):
    tr = 256

    def body(g_ref, o_ref):
        for j, a, b, dst in _repack_runs():
            o_ref[j, :, a:b] = g_ref[:, dst:dst + b - a].astype(BF16)

    return pl.pallas_call(
        body, name=name, grid=(D_MODEL // tr,),
        out_shape=jax.ShapeDtypeStruct((N_DEV, D_MODEL, D_IN_PROJ // N_DEV), BF16),
        in_specs=[pl.BlockSpec((tr, PROJ_W), lambda i: (i, 0))],
        out_specs=pl.BlockSpec((N_DEV, tr, D_IN_PROJ // N_DEV), lambda i: (0, i, 0)),
        compiler_params=pltpu.CompilerParams(dimension_semantics=("arbitrary",)),
    )(g)


def _pair_sum(parts, name):
    n_dev, r, cdim = parts.shape
    n_chip = n_dev // 2
    by_chip = parts.reshape(n_chip, 2, r, cdim)

    def swap_body(p_ref, got_ref, send_sem, recv_sem):
        x, y, c = _my_place()
        cp = pltpu.make_async_remote_copy(
            src_ref=p_ref.at[:, pl.ds(1 - c, 1)], dst_ref=got_ref, send_sem=send_sem, recv_sem=recv_sem,
            device_id=(x, y, 1 - c), device_id_type=pl.DeviceIdType.MESH)
        cp.start()
        cp.wait()

    any_spec = pl.BlockSpec(memory_space=pl.ANY)
    got = pl.pallas_call(
        swap_body, name=name + "_swap",
        out_shape=jax.ShapeDtypeStruct((n_chip, 1, r, cdim), parts.dtype),
        in_specs=[any_spec], out_specs=any_spec,
        scratch_shapes=[pltpu.SemaphoreType.DMA, pltpu.SemaphoreType.DMA],
    )(by_chip)

    tr = 256

    def add_body(core_ref, p_ref, got_ref, o_ref):
        o_ref[0] = (p_ref[0, 0].astype(F32) + got_ref[0, 0].astype(F32)).astype(o_ref.dtype)

    return pl.pallas_call(
        add_body, name=name + "_add",
        grid_spec=pltpu.PrefetchScalarGridSpec(
            num_scalar_prefetch=1, grid=(n_chip, r // tr),
            in_specs=[pl.BlockSpec((1, 1, tr, cdim), lambda k, i, core: (k, core[0], i, 0)),
                      pl.BlockSpec((1, 1, tr, cdim), lambda k, i, core: (k, 0, i, 0))],
            out_specs=pl.BlockSpec((1, tr, cdim), lambda k, i, core: (k, i, 0))),
        out_shape=jax.ShapeDtypeStruct((n_chip, r, cdim), parts.dtype),
        compiler_params=pltpu.CompilerParams(dimension_semantics=("arbitrary", "arbitrary")),
    )(lax.axis_index("c").astype(jnp.int32).reshape(1), by_chip, got)


def _adamw(w, g, m, v):
    m = ADAM_B1 * m + (1.0 - ADAM_B1) * g
    v = ADAM_B2 * v + (1.0 - ADAM_B2) * jnp.square(g)
    m_hat = m / (1.0 - ADAM_B1 ** ADAM_STEP)
    v_hat = v / (1.0 - ADAM_B2 ** ADAM_STEP)
    delta = -ADAM_LR * (m_hat / (jnp.sqrt(v_hat) + ADAM_EPS) + ADAM_WD * w)
    return delta, m, v


def _reduce_adamw(parts, w, m, v, tr, name):
    depth = len(parts)
    p, r, cdim = parts[0].shape
    n_blk = r // tr

    def body(*refs):
        p_refs = refs[:depth]
        w_ref, m_ref, v_ref, g_ref, d_ref, nm_ref, nv_ref = refs[depth:]
        for layer in range(depth):
            @pl.when(pl.program_id(0) == layer)
            def _(p_ref=p_refs[layer]):
                g = p_ref[0].astype(F32)
                for i in range(1, p):
                    g = g + p_ref[i].astype(F32)
                g_ref[0] = g
                d_ref[0], nm_ref[0], nv_ref[0] = _adamw(w_ref[0], g, m_ref[0], v_ref[0])

    def parts_spec(layer):
        return pl.BlockSpec((p, tr, cdim), lambda d, i: (0, jnp.clip(i + (d - layer) * n_blk, 0, n_blk - 1), 0))

    blk = pl.BlockSpec((1, tr, cdim), lambda d, i: (d, i, 0))
    return pl.pallas_call(
        body, name=name, grid=(depth, n_blk),
        out_shape=[jax.ShapeDtypeStruct(w.shape, F32)] * 4,
        in_specs=[parts_spec(layer) for layer in range(depth)] + [blk, blk, blk],
        out_specs=[blk] * 4,
        compiler_params=pltpu.CompilerParams(dimension_semantics=("arbitrary", "arbitrary"),
                                             vmem_limit_bytes=VMEM_LIMIT),
    )(*parts, w, m, v)


def _adamw_small(ssum, entries, name):
    direct = [e[3] for e in entries if not isinstance(e[3], list)]
    n_direct = len(direct)

    def body(*refs):
        ssum_ref, direct_refs = refs[0], list(refs[1:1 + n_direct])
        ins = refs[1 + n_direct:1 + n_direct + 3 * len(entries)]
        outs = refs[1 + n_direct + 3 * len(entries):]
        for k, (w, _, _, grad) in enumerate(entries):
            w_ref, m_ref, v_ref = ins[3 * k:3 * k + 3]
            g_ref, d_ref, nm_ref, nv_ref = outs[4 * k:4 * k + 4]
            if isinstance(grad, list):
                for row, off in enumerate(grad):
                    rows = slice(row, row + 1)
                    g = ssum_ref[:, off:off + w.shape[1]]
                    g_ref[rows, :] = g
                    d_ref[rows, :], nm_ref[rows, :], nv_ref[rows, :] = _adamw(w_ref[rows, :], g, m_ref[rows, :],
                                                                              v_ref[rows, :])
            else:
                g = direct_refs.pop(0)[...]
                g_ref[...] = g
                d_ref[...], nm_ref[...], nv_ref[...] = _adamw(w_ref[...], g, m_ref[...], v_ref[...])

    vmem = pl.BlockSpec(memory_space=pltpu.VMEM)
    args = [ssum] + direct + [a for e in entries for a in e[:3]]
    res = pl.pallas_call(
        body, name=name,
        out_shape=[jax.ShapeDtypeStruct(e[0].shape, F32) for e in entries for _ in range(4)],
        in_specs=[vmem] * len(args), out_specs=[vmem] * (4 * len(entries)),
    )(*args)
    return [res[4 * k:4 * k + 4] for k in range(len(entries))]


def _reduce_adamw_cols(parts, w, m, v, name):
    depth = len(parts)
    _, r, cdim = parts[0].shape
    tc = 512

    def body(*refs):
        p_refs = refs[:depth]
        w_ref, m_ref, v_ref, g_ref, d_ref, nm_ref, nv_ref = refs[depth:]
        for layer in range(depth):
            g = p_refs[layer][0].astype(F32)
            for i in range(1, parts[layer].shape[0]):
                g = g + p_refs[layer][i].astype(F32)
            g = g.T
            g_ref[:, layer, :] = g
            d_ref[:, layer, :], nm_ref[:, layer, :], nv_ref[:, layer, :] = _adamw(
                w_ref[:, layer, :], g, m_ref[:, layer, :], v_ref[:, layer, :])

    view = lambda a: jnp.transpose(a, (2, 0, 1))
    blk = pl.BlockSpec((cdim, depth, tc), lambda i: (0, 0, i))
    outs = pl.pallas_call(
        body, name=name, grid=(r // tc,),
        out_shape=[jax.ShapeDtypeStruct((cdim, depth, r), F32)] * 4,
        in_specs=[pl.BlockSpec((a.shape[0], tc, cdim), lambda i: (0, i, 0)) for a in parts] + [blk, blk, blk],
        out_specs=[blk] * 4,
        compiler_params=pltpu.CompilerParams(dimension_semantics=("arbitrary",), vmem_limit_bytes=VMEM_LIMIT),
    )(*parts, view(w), view(m), view(v))
    return [jnp.transpose(o, (1, 2, 0)) for o in outs]


def _pad_lanes(v, width=128):
    return jnp.pad(v.reshape(1, -1), ((0, 0), (0, width - v.shape[-1])))


SMALL_FIELDS = (("norm_w", 1024), ("conv_b", 1536), ("dt_bias", 128), ("a_log", 128), ("d_skip", 128),
                ("ssd_norm_w", 1024), ("sinks", 128), ("dw_b", 512), ("ln_w", 512), ("ln_b", 512))


def kernel(x, norm_w, w_in, ssd_conv_w, ssd_conv_b, ssd_dt_bias, ssd_a_log, ssd_d, ssd_norm_w, attn_sinks, conf_dw_w, conf_dw_b, conf_ln_w, conf_ln_b, w_out, final_norm_w, loss_target, m_norm_w, m_w_in, m_ssd_conv_w, m_ssd_conv_b, m_ssd_dt_bias, m_ssd_a_log, m_ssd_d, m_ssd_norm_w, m_attn_sinks, m_conf_dw_w, m_conf_dw_b, m_conf_ln_w, m_conf_ln_b, m_w_out, m_final_norm_w, v_norm_w, v_w_in, v_ssd_conv_w, v_ssd_conv_b, v_ssd_dt_bias, v_ssd_a_log, v_ssd_d, v_ssd_norm_w, v_attn_sinks, v_conf_dw_w, v_conf_dw_b, v_conf_ln_w, v_conf_ln_b, v_w_out, v_final_norm_w):
    nb, seq, _ = x.shape
    depth = norm_w.shape[0]
    t = nb * seq

    w_in_bf, w_out_bf = w_in.astype(BF16), w_out.astype(BF16)
    g_win0, g_cw, g_dw = _all_gather([w_in_bf[0], ssd_conv_w, conf_dw_w], "gather_weights")
    w_in_full = [_repack_w_in(g_win0, "repack_w_in_0")]
    w_out_full = []
    conv_w_full = [jnp.transpose(g_cw[:, l], (1, 0, 2)).reshape(4, 1536) for l in range(depth)]
    dw_w_full = [jnp.transpose(g_dw[:, l], (1, 0, 2)).reshape(CONF_KERNEL, 512) for l in range(depth)]

    def layer_params(l):
        return [conv_w_full[l], ssd_conv_b[l].reshape(1, -1), _pad_lanes(ssd_dt_bias[l]), _pad_lanes(ssd_a_log[l]),
                _pad_lanes(ssd_d[l]), ssd_norm_w[l].reshape(1, -1), _pad_lanes(attn_sinks[l]), dw_w_full[l],
                conf_dw_b[l].reshape(1, -1), conf_ln_w[l].reshape(1, -1), conf_ln_b[l].reshape(1, -1)]

    xs = [x.reshape(t, D_MODEL)]
    saved = []
    for l in range(depth):
        (proj, h), gathered = _inproj_fwd(xs[l], norm_w[l].reshape(1, -1), w_in_full[l], f"inproj_fwd_{l}",
                                          (GATHER, [w_out_bf[l]]))
        w_out_full.append(gathered[0].reshape(MIX_WIDTH, D_MODEL))
        last = l + 1 == depth
        side = None if last else (GATHER, [w_in_bf[l + 1]])
        head = (final_norm_w.reshape(1, -1), loss_target.reshape(t, D_MODEL)) if last else None
        res, gathered = _mixer_fwd(xs[l], proj, w_out_full[l], layer_params(l), nb, f"mixer_fwd_{l}", side, head)
        if side:
            w_in_full.append(_repack_w_in(gathered[0], f"repack_w_in_{l + 1}"))
        saved.append((proj, h, res[1:3]))
        xs.append(res[0])
    dx, loss_part, g_fnw = res[0], res[3], res[4]

    small_rows, shard_rows = [None] * depth, [None] * depth
    received = [None] * depth
    for l in reversed(range(depth)):
        proj, h, kept = saved[l]
        res, _ = _mixer_bwd(dx, proj, *kept, w_out_full[l], layer_params(l), nb, f"mixer_bwd_{l}")
        dproj, y_cat = res[0], res[1]
        g_cw_l, g_cb, g_dtb, g_alog, g_dsk, g_nw, g_snk, g_dww, g_dwb, g_lnw, g_lnb = res[2:]
        gw_out_parts = _gw_out(y_cat, dx, f"gw_out_{l}")
        (gw_in,), got_out = _inproj_bwd_w(h, dproj, f"inproj_bwd_w_{l}", (SCATTER, [gw_out_parts]))
        chip_parts = _pair_sum(_unpack_gw_in(gw_in, f"unpack_gw_in_{l}"), f"pair_sum_{l}")
        (dx, g_norm), got_in = _inproj_bwd_x(dproj, w_in_full[l], xs[l], norm_w[l].reshape(1, -1), dx,
                                             f"inproj_bwd_x_{l}", (CHIP_SCATTER, [chip_parts]))
        received[l] = [got_in[0], got_out[0]]
        small_rows[l] = [g_norm, g_cb, g_dtb, g_alog, g_dsk, g_nw, g_snk, g_dwb, g_lnw, g_lnb]
        shard_rows[l] = [jnp.transpose(g.reshape(g.shape[0], N_DEV, -1), (1, 0, 2)).reshape(N_DEV, -1)
                         for g in (g_cw_l, g_dww)]
    grad_x = dx.reshape(nb, seq, D_MODEL)

    small = jnp.concatenate([piece for l in range(depth) for piece in small_rows[l]]
                            + [g_fnw, _pad_lanes(loss_part)], axis=1)
    shard_small = jnp.concatenate([piece for l in range(depth) for piece in shard_rows[l]], axis=1)
    ssum, shard_sum = _exchange_small(small, shard_small.reshape(N_DEV, 1, -1), "exchange_small")
    loss = ssum[0, small.shape[1] - 128]

    g_w_in, d_w_in, nm_w_in, nv_w_in = _reduce_adamw_cols([received[l][0] for l in range(depth)], w_in, m_w_in,
                                                          v_w_in, "adamw_w_in")
    g_w_out, d_w_out, nm_w_out, nv_w_out = _reduce_adamw([received[l][1] for l in range(depth)], w_out, m_w_out,
                                                         v_w_out, 256, "adamw_w_out")

    per_layer = sum(n for _, n in SMALL_FIELDS)
    given = {"norm_w": (norm_w, m_norm_w, v_norm_w), "conv_b": (ssd_conv_b, m_ssd_conv_b, v_ssd_conv_b),
             "dt_bias": (ssd_dt_bias, m_ssd_dt_bias, v_ssd_dt_bias), "a_log": (ssd_a_log, m_ssd_a_log, v_ssd_a_log),
             "d_skip": (ssd_d, m_ssd_d, v_ssd_d), "ssd_norm_w": (ssd_norm_w, m_ssd_norm_w, v_ssd_norm_w),
             "sinks": (attn_sinks, m_attn_sinks, v_attn_sinks), "dw_b": (conf_dw_b, m_conf_dw_b, v_conf_dw_b),
             "ln_w": (conf_ln_w, m_conf_ln_w, v_conf_ln_w), "ln_b": (conf_ln_b, m_conf_ln_b, v_conf_ln_b)}
    entries = []
    off = 0
    for fname, n in SMALL_FIELDS:
        entries.append((*given[fname], [l * per_layer + off for l in range(depth)]))
        off += n
    n_cw, n_dw = ssd_conv_w[0].size, conf_dw_w[0].size
    per_layer_shard = n_cw + n_dw
    cw_grad = jnp.stack([shard_sum[0, l * per_layer_shard:l * per_layer_shard + n_cw].reshape(ssd_conv_w.shape[1:])
                         for l in range(depth)], axis=0)
    dw_grad = jnp.stack([shard_sum[0, l * per_layer_shard + n_cw:(l + 1) * per_layer_shard]
                         .reshape(conf_dw_w.shape[1:]) for l in range(depth)], axis=0)
    entries.append((ssd_conv_w, m_ssd_conv_w, v_ssd_conv_w, cw_grad))
    entries.append((conf_dw_w, m_conf_dw_w, v_conf_dw_w, dw_grad))
    entries.append((final_norm_w.reshape(1, -1), m_final_norm_w.reshape(1, -1), v_final_norm_w.reshape(1, -1),
                    [depth * per_layer]))
    sm = _adamw_small(ssum, entries, "adamw_small")
    sm = {k: quad for k, quad in zip([f for f, _ in SMALL_FIELDS] + ["conv_w", "dw_w", "final"], sm)}

    def outputs(i, big_in_i, big_out_i):
        return [sm["norm_w"][i], big_in_i, sm["conv_w"][i], sm["conv_b"][i], sm["dt_bias"][i], sm["a_log"][i],
                sm["d_skip"][i], sm["ssd_norm_w"][i], sm["sinks"][i], sm["dw_w"][i], sm["dw_b"][i], sm["ln_w"][i],
                sm["ln_b"][i], big_out_i, sm["final"][i].reshape(-1)]

    return (loss, grad_x, *outputs(0, g_w_in, g_w_out), *outputs(1, d_w_in, d_w_out),
            *outputs(2, nm_w_in, nm_w_out), *outputs(3, nv_w_in, nv_w_out))
```
